```python
import jax, jax.numpy as jnp
from jax import lax
import numpy as np

D_MODEL = 1024
BATCH = 8
SEQ = 16384
DEPTH = 1

HEAD_DIM = 64
N_FOX_HEADS = 8
N_DIL_HEADS = 8
FOX_WIDTH = N_FOX_HEADS * HEAD_DIM
DIL_WIDTH = N_DIL_HEADS * HEAD_DIM
MIX_WIDTH = FOX_WIDTH + DIL_WIDTH
DILATED_PATTERNS = ((128, 1), (512, 4), (2048, 16))
Q_BLOCK = 128
N_MEM = 256
N_MEM_HEADS = 4
MEM_WIDTH = N_MEM_HEADS * HEAD_DIM
N_BUCKETS = 32
MAX_DISTANCE = 2048
D_FF = -(-8 * D_MODEL // (3 * 256)) * 256
RMS_EPS = 1e-6
IN_SIZES = (FOX_WIDTH, FOX_WIDTH, FOX_WIDTH, N_FOX_HEADS, DIL_WIDTH, DIL_WIDTH, DIL_WIDTH)
IN_COLS = sum(IN_SIZES)

kernel_name = "hybrid_fox_dilated_memxattn_swiglu"


def rmsnorm(x, g):
    xf = x.astype(jnp.float32)
    y = xf * lax.rsqrt(jnp.mean(xf * xf, axis=-1, keepdims=True) + RMS_EPS)
    return (y * g.astype(jnp.float32)).astype(x.dtype)


def split_heads(t, n):
    B, S, _ = t.shape
    return t.reshape(B, S, n, HEAD_DIM).transpose(0, 2, 1, 3)


def merge_heads(t):
    B, H, S, Dh = t.shape
    return t.transpose(0, 2, 1, 3).reshape(B, S, H * Dh)


def t5_causal_bucket(dist):
    max_exact = N_BUCKETS // 2
    d = np.maximum(dist, 1).astype(np.float32)
    large = max_exact + (np.log(d / max_exact) / np.log(MAX_DISTANCE / max_exact)
                         * (N_BUCKETS - max_exact)).astype(np.int32)
    large = np.minimum(large, N_BUCKETS - 1)
    return np.where(dist < max_exact, dist, large).astype(np.int32)


def forgetting_attention(q, k, v, log_f):
    B, H, S, Dh = q.shape
    nb = S // Q_BLOCK
    c = jnp.cumsum(log_f, axis=-1)
    qb = q.reshape(B, H, nb, Q_BLOCK, Dh).transpose(2, 0, 1, 3, 4)
    cb = c.reshape(B, H, nb, Q_BLOCK).transpose(2, 0, 1, 3)
    key_pos = jnp.arange(S)
    scale = Dh ** -0.5

    def block(args):
        qi, ci, i = args
        s = jnp.einsum('bhqd,bhkd->bhqk', qi, k).astype(jnp.float32) * scale
        s = s + ci[..., :, None] - c[..., None, :]
        q_pos = i * Q_BLOCK + jnp.arange(Q_BLOCK)
        s = jnp.where(key_pos[None, :] <= q_pos[:, None], s, -jnp.inf)
        p = jax.nn.softmax(s, axis=-1)
        return jnp.einsum('bhqk,bhkd->bhqd', p.astype(v.dtype), v)

    o = lax.map(block, (qb, cb, jnp.arange(nb)))
    return o.transpose(1, 2, 0, 3, 4).reshape(B, H, S, Dh)


def dilated_branch(q, k, v, rel_bias, window, dilation):
    B, H, S, Dh = q.shape
    w = window // dilation
    span = w * dilation
    S_pad = -(-S // span) * span
    L = S_pad // dilation
    nb = L // w

    def split(t):
        t = jnp.pad(t, ((0, 0), (0, 0), (0, S_pad - S), (0, 0)))
        t = t.reshape(B, H, L, dilation, Dh).transpose(0, 1, 3, 2, 4)
        return t.reshape(B, H, dilation, nb, w, Dh)

    def with_prev(t):
        prev = jnp.pad(t[:, :, :, :-1], ((0, 0), (0, 0), (0, 0), (1, 0), (0, 0), (0, 0)))
        return jnp.concatenate([prev, t], axis=4)

    qs = split(q)
    kk = with_prev(split(k))
    vv = with_prev(split(v))
    s = jnp.einsum('bhrnqd,bhrnkd->bhrnqk', qs, kk).astype(jnp.float32) * (Dh ** -0.5)

    qi = np.arange(w)[:, None]
    kj = np.arange(2 * w)[None, :]
    sub_dist = qi + w - kj
    band = (sub_dist >= 0) & (sub_dist <= w)
    bucket = t5_causal_bucket(np.clip(sub_dist, 0, w) * dilation)
    bias = rel_bias.astype(jnp.float32)[bucket]
    s = s + jnp.transpose(bias, (2, 0, 1))[None, :, None, None]
    not_first = (np.arange(nb)[:, None, None] > 0) | (kj[None] >= w)
    mask = jnp.asarray(band[None] & not_first)
    s = jnp.where(mask, s, -jnp.inf)

    m = jnp.max(s, axis=-1, keepdims=True)
    e = jnp.exp(s - m)
    l = jnp.sum(e, axis=-1, keepdims=True)
    o = jnp.einsum('bhrnqk,bhrnkd->bhrnqd', (e / l).astype(v.dtype), vv)
    lse = (m + jnp.log(l))[..., 0]

    def merge(t):
        t = t.reshape(B, H, dilation, L, *t.shape[5:])
        t = jnp.moveaxis(t, 2, 3)
        t = t.reshape(B, H, S_pad, *t.shape[4:])
        return t[:, :, :S]

    return merge(o), merge(lse)


def dilated_attention(q, k, v, rel_bias):
    outs, lses = [], []
    for window, dilation in DILATED_PATTERNS:
        o, lse = dilated_branch(q, k, v, rel_bias, window, dilation)
        outs.append(o)
        lses.append(lse)
    alpha = jax.nn.softmax(jnp.stack(lses, axis=0), axis=0)
    return jnp.einsum('pbhs,pbhsd->bhsd', alpha.astype(v.dtype), jnp.stack(outs, axis=0))


def memory_cross_attention(h, hm, w_xq, w_xk, w_xv, w_xo):
    q = split_heads(h @ w_xq, N_MEM_HEADS)
    k = split_heads(hm @ w_xk, N_MEM_HEADS)
    v = split_heads(hm @ w_xv, N_MEM_HEADS)
    s = jnp.einsum('bhqd,bhkd->bhqk', q, k).astype(jnp.float32) * (HEAD_DIM ** -0.5)
    p = jax.nn.softmax(s, axis=-1)
    o = jnp.einsum('bhqk,bhkd->bhqd', p.astype(v.dtype), v)
    return merge_heads(o) @ w_xo


def _fwd_setup_inputs(seed: int = 0) -> dict:
    key = jax.random.key(seed)
    ks = jax.random.split(key, 24)
    nrm = jax.random.normal

    def w(k, shape, fan_in):
        return nrm(k, shape, jnp.float32) * fan_in ** -0.5

    def gain(k):
        return 1.0 + 0.1 * nrm(k, (DEPTH, D_MODEL), jnp.float32)

    return {
        "x": nrm(ks[0], (BATCH, SEQ, D_MODEL), jnp.float32),
        "mem": nrm(ks[1], (BATCH, N_MEM, D_MODEL), jnp.float32),
        "g_mix_pre": gain(ks[2]),
        "w_in": w(ks[3], (DEPTH, D_MODEL, IN_COLS), D_MODEL),
        "b_f": 2.0 + 0.5 * nrm(ks[4], (DEPTH, N_FOX_HEADS), jnp.float32),
        "rel_bias": 0.5 * nrm(ks[5], (N_BUCKETS, N_DIL_HEADS), jnp.float32),
        "w_out": w(ks[6], (DEPTH, MIX_WIDTH, D_MODEL), MIX_WIDTH),
        "g_mix_post": gain(ks[7]),
        "g_xattn_pre": gain(ks[8]),
        "g_mem": gain(ks[9]),
        "w_xq": w(ks[10], (DEPTH, D_MODEL, MEM_WIDTH), D_MODEL),
        "w_xk": w(ks[11], (DEPTH, D_MODEL, MEM_WIDTH), D_MODEL),
        "w_xv": w(ks[12], (DEPTH, D_MODEL, MEM_WIDTH), D_MODEL),
        "w_xo": w(ks[13], (DEPTH, MEM_WIDTH, D_MODEL), MEM_WIDTH),
        "g_xattn_post": gain(ks[14]),
        "g_ffn_pre": gain(ks[15]),
        "w_gate": w(ks[16], (DEPTH, D_MODEL, D_FF), D_MODEL),
        "w_up": w(ks[17], (DEPTH, D_MODEL, D_FF), D_MODEL),
        "w_down": w(ks[18], (DEPTH, D_FF, D_MODEL), D_FF),
        "g_ffn_post": gain(ks[19]),
    }


def _fwd_reference(x, mem, g_mix_pre, w_in, b_f, rel_bias, w_out, g_mix_post,
              g_xattn_pre, g_mem, w_xq, w_xk, w_xv, w_xo, g_xattn_post,
              g_ffn_pre, w_gate, w_up, w_down, g_ffn_post):
    split_points = [int(p) for p in np.cumsum(IN_SIZES)[:-1]]
    for layer in range(DEPTH):
        h = rmsnorm(x, g_mix_pre[layer])
        proj = h @ w_in[layer]
        fq, fk, fv, fgate, dq, dk, dv = jnp.split(proj, split_points, axis=-1)
        log_f = jax.nn.log_sigmoid((fgate + b_f[layer]).astype(jnp.float32))
        log_f = log_f.transpose(0, 2, 1)
        o_fox = forgetting_attention(split_heads(fq, N_FOX_HEADS), split_heads(fk, N_FOX_HEADS),
                                     split_heads(fv, N_FOX_HEADS), log_f)
        o_dil = dilated_attention(split_heads(dq, N_DIL_HEADS), split_heads(dk, N_DIL_HEADS),
                                  split_heads(dv, N_DIL_HEADS), rel_bias)
        o = merge_heads(jnp.concatenate([o_fox, o_dil], axis=1))
        x = x + rmsnorm(o @ w_out[layer], g_mix_post[layer])

        h = rmsnorm(x, g_xattn_pre[layer])
        hm = rmsnorm(mem, g_mem[layer])
        y = memory_cross_attention(h, hm, w_xq[layer], w_xk[layer], w_xv[layer], w_xo[layer])
        x = x + rmsnorm(y, g_xattn_post[layer])

        h = rmsnorm(x, g_ffn_pre[layer])
        y = (jax.nn.silu(h @ w_gate[layer]) * (h @ w_up[layer])) @ w_down[layer]
        x = x + rmsnorm(y, g_ffn_post[layer])
    return x


import jax as _jax
import jax.numpy as _jnp

TWIN_FORMAT = 'train_step'
FWD_PARAMS = ['x', 'mem', 'g_mix_pre', 'w_in', 'b_f', 'rel_bias', 'w_out', 'g_mix_post', 'g_xattn_pre', 'g_mem', 'w_xq', 'w_xk', 'w_xv', 'w_xo', 'g_xattn_post', 'g_ffn_pre', 'w_gate', 'w_up', 'w_down', 'g_ffn_post']
TWIN_WEIGHTS = ['g_mix_pre', 'w_in', 'b_f', 'rel_bias', 'w_out', 'g_mix_post', 'g_xattn_pre', 'g_mem', 'w_xq', 'w_xk', 'w_xv', 'w_xo', 'g_xattn_post', 'g_ffn_pre', 'w_gate', 'w_up', 'w_down', 'g_ffn_post']
TWIN_DIFF_INPUT = 'x'
TWIN_INPUTS = ['x', 'mem', 'g_mix_pre', 'w_in', 'b_f', 'rel_bias', 'w_out', 'g_mix_post', 'g_xattn_pre', 'g_mem', 'w_xq', 'w_xk', 'w_xv', 'w_xo', 'g_xattn_post', 'g_ffn_pre', 'w_gate', 'w_up', 'w_down', 'g_ffn_post', 'loss_target', 'm_g_mix_pre', 'm_w_in', 'm_b_f', 'm_rel_bias', 'm_w_out', 'm_g_mix_post', 'm_g_xattn_pre', 'm_g_mem', 'm_w_xq', 'm_w_xk', 'm_w_xv', 'm_w_xo', 'm_g_xattn_post', 'm_g_ffn_pre', 'm_w_gate', 'm_w_up', 'm_w_down', 'm_g_ffn_post', 'v_g_mix_pre', 'v_w_in', 'v_b_f', 'v_rel_bias', 'v_w_out', 'v_g_mix_post', 'v_g_xattn_pre', 'v_g_mem', 'v_w_xq', 'v_w_xk', 'v_w_xv', 'v_w_xo', 'v_g_xattn_post', 'v_g_ffn_pre', 'v_w_gate', 'v_w_up', 'v_w_down', 'v_g_ffn_post']
TWIN_OUTPUTS = ['loss', 'grad_x', 'grad_g_mix_pre', 'grad_w_in', 'grad_b_f', 'grad_rel_bias', 'grad_w_out', 'grad_g_mix_post', 'grad_g_xattn_pre', 'grad_g_mem', 'grad_w_xq', 'grad_w_xk', 'grad_w_xv', 'grad_w_xo', 'grad_g_xattn_post', 'grad_g_ffn_pre', 'grad_w_gate', 'grad_w_up', 'grad_w_down', 'grad_g_ffn_post', 'delta_g_mix_pre', 'delta_w_in', 'delta_b_f', 'delta_rel_bias', 'delta_w_out', 'delta_g_mix_post', 'delta_g_xattn_pre', 'delta_g_mem', 'delta_w_xq', 'delta_w_xk', 'delta_w_xv', 'delta_w_xo', 'delta_g_xattn_post', 'delta_g_ffn_pre', 'delta_w_gate', 'delta_w_up', 'delta_w_down', 'delta_g_ffn_post', 'new_m_g_mix_pre', 'new_m_w_in', 'new_m_b_f', 'new_m_rel_bias', 'new_m_w_out', 'new_m_g_mix_post', 'new_m_g_xattn_pre', 'new_m_g_mem', 'new_m_w_xq', 'new_m_w_xk', 'new_m_w_xv', 'new_m_w_xo', 'new_m_g_xattn_post', 'new_m_g_ffn_pre', 'new_m_w_gate', 'new_m_w_up', 'new_m_w_down', 'new_m_g_ffn_post', 'new_v_g_mix_pre', 'new_v_w_in', 'new_v_b_f', 'new_v_rel_bias', 'new_v_w_out', 'new_v_g_mix_post', 'new_v_g_xattn_pre', 'new_v_g_mem', 'new_v_w_xq', 'new_v_w_xk', 'new_v_w_xv', 'new_v_w_xo', 'new_v_g_xattn_post', 'new_v_g_ffn_pre', 'new_v_w_gate', 'new_v_w_up', 'new_v_w_down', 'new_v_g_ffn_post']
TWIN_LEAF_KINDS = {'loss': 'loss', 'grad_x': 'grad_x', 'grad_g_mix_pre': 'grad_w', 'grad_w_in': 'grad_w', 'grad_b_f': 'grad_w', 'grad_rel_bias': 'grad_w', 'grad_w_out': 'grad_w', 'grad_g_mix_post': 'grad_w', 'grad_g_xattn_pre': 'grad_w', 'grad_g_mem': 'grad_w', 'grad_w_xq': 'grad_w', 'grad_w_xk': 'grad_w', 'grad_w_xv': 'grad_w', 'grad_w_xo': 'grad_w', 'grad_g_xattn_post': 'grad_w', 'grad_g_ffn_pre': 'grad_w', 'grad_w_gate': 'grad_w', 'grad_w_up': 'grad_w', 'grad_w_down': 'grad_w', 'grad_g_ffn_post': 'grad_w', 'delta_g_mix_pre': 'delta_w', 'delta_w_in': 'delta_w', 'delta_b_f': 'delta_w', 'delta_rel_bias': 'delta_w', 'delta_w_out': 'delta_w', 'delta_g_mix_post': 'delta_w', 'delta_g_xattn_pre': 'delta_w', 'delta_g_mem': 'delta_w', 'delta_w_xq': 'delta_w', 'delta_w_xk': 'delta_w', 'delta_w_xv': 'delta_w', 'delta_w_xo': 'delta_w', 'delta_g_xattn_post': 'delta_w', 'delta_g_ffn_pre': 'delta_w', 'delta_w_gate': 'delta_w', 'delta_w_up': 'delta_w', 'delta_w_down': 'delta_w', 'delta_g_ffn_post': 'delta_w', 'new_m_g_mix_pre': 'new_m', 'new_m_w_in': 'new_m', 'new_m_b_f': 'new_m', 'new_m_rel_bias': 'new_m', 'new_m_w_out': 'new_m', 'new_m_g_mix_post': 'new_m', 'new_m_g_xattn_pre': 'new_m', 'new_m_g_mem': 'new_m', 'new_m_w_xq': 'new_m', 'new_m_w_xk': 'new_m', 'new_m_w_xv': 'new_m', 'new_m_w_xo': 'new_m', 'new_m_g_xattn_post': 'new_m', 'new_m_g_ffn_pre': 'new_m', 'new_m_w_gate': 'new_m', 'new_m_w_up': 'new_m', 'new_m_w_down': 'new_m', 'new_m_g_ffn_post': 'new_m', 'new_v_g_mix_pre': 'new_v', 'new_v_w_in': 'new_v', 'new_v_b_f': 'new_v', 'new_v_rel_bias': 'new_v', 'new_v_w_out': 'new_v', 'new_v_g_mix_post': 'new_v', 'new_v_g_xattn_pre': 'new_v', 'new_v_g_mem': 'new_v', 'new_v_w_xq': 'new_v', 'new_v_w_xk': 'new_v', 'new_v_w_xv': 'new_v', 'new_v_w_xo': 'new_v', 'new_v_g_xattn_post': 'new_v', 'new_v_g_ffn_pre': 'new_v', 'new_v_w_gate': 'new_v', 'new_v_w_up': 'new_v', 'new_v_w_down': 'new_v', 'new_v_g_ffn_post': 'new_v'}


def _forward(args):
    return _fwd_reference(*[args[k] for k in FWD_PARAMS])


def _output_shape():
    def fwd():
        inp = _fwd_setup_inputs(0)
        return _fwd_reference(*[inp[k] for k in FWD_PARAMS])
    out = _jax.eval_shape(fwd)
    return out.shape, out.dtype

N_MICROBATCH = 1
ADAM_LR = 0.001
ADAM_B1 = 0.9
ADAM_B2 = 0.999
ADAM_EPS = 1e-08
ADAM_WD = 0.01
ADAM_STEP = 10
PER_EXAMPLE_BATCH_AXIS = {'x': 0, 'mem': 0, 'loss_target': 0}
SHARED_INPUTS = []
_WEIGHT_DTYPES = {'g_mix_pre': _jnp.float32, 'w_in': _jnp.float32, 'b_f': _jnp.float32, 'rel_bias': _jnp.float32, 'w_out': _jnp.float32, 'g_mix_post': _jnp.float32, 'g_xattn_pre': _jnp.float32, 'g_mem': _jnp.float32, 'w_xq': _jnp.float32, 'w_xk': _jnp.float32, 'w_xv': _jnp.float32, 'w_xo': _jnp.float32, 'g_xattn_post': _jnp.float32, 'g_ffn_pre': _jnp.float32, 'w_gate': _jnp.float32, 'w_up': _jnp.float32, 'w_down': _jnp.float32, 'g_ffn_post': _jnp.float32}
MOMENT_SCALE = {'g_mix_pre': 2.410155e+00, 'w_in': 1.246375e+00, 'b_f': 4.837556e+01, 'rel_bias': 9.366870e-01, 'w_out': 2.133831e+00, 'g_mix_post': 1.280289e+02, 'g_xattn_pre': 1.491418e+00, 'g_mem': 5.154830e+00, 'w_xq': 3.237131e+00, 'w_xk': 3.564887e+00, 'w_xv': 9.669805e+00, 'w_xo': 5.303300e+00, 'g_xattn_post': 1.303144e+02, 'g_ffn_pre': 3.431132e+00, 'w_gate': 1.063824e+00, 'w_up': 2.104623e+00, 'w_down': 3.593211e+00, 'g_ffn_post': 1.283477e+02}


def _to_microbatches(a, axis):
    t = _jnp.moveaxis(a, axis, 0)
    t = t.reshape((N_MICROBATCH, t.shape[0] // N_MICROBATCH) + t.shape[1:])
    return _jnp.moveaxis(t, 1, axis + 1)


def setup_inputs(seed: int = 0) -> dict:
    inp = _fwd_setup_inputs(seed)
    key = _jax.random.fold_in(_jax.random.key(seed), 7919)
    shape, _ = _output_shape()
    out = dict(inp)
    out["loss_target"] = _jax.random.normal(_jax.random.fold_in(key, 0), shape, _jnp.float32)
    for i, name in enumerate(TWIN_WEIGHTS):
        w = inp[name].astype(_jnp.float32)
        if MOMENT_SCALE is None:
            s = _jnp.sqrt(_jnp.mean(_jnp.square(w)) + 1e-30)
        else:
            s = MOMENT_SCALE[name]
        km, kv = _jax.random.split(_jax.random.fold_in(key, i + 1))
        out[name] = w
        out["m_" + name] = s * _jax.random.normal(km, w.shape, _jnp.float32)
        out["v_" + name] = (s * s) * _jax.random.uniform(kv, w.shape, _jnp.float32, 0.5, 1.5)
    if N_MICROBATCH > 1:
        for name, axis in PER_EXAMPLE_BATCH_AXIS.items():
            out[name] = _to_microbatches(out[name], axis)
    return {'x': out['x'], 'mem': out['mem'], 'g_mix_pre': out['g_mix_pre'], 'w_in': out['w_in'], 'b_f': out['b_f'], 'rel_bias': out['rel_bias'], 'w_out': out['w_out'], 'g_mix_post': out['g_mix_post'], 'g_xattn_pre': out['g_xattn_pre'], 'g_mem': out['g_mem'], 'w_xq': out['w_xq'], 'w_xk': out['w_xk'], 'w_xv': out['w_xv'], 'w_xo': out['w_xo'], 'g_xattn_post': out['g_xattn_post'], 'g_ffn_pre': out['g_ffn_pre'], 'w_gate': out['w_gate'], 'w_up': out['w_up'], 'w_down': out['w_down'], 'g_ffn_post': out['g_ffn_post'], 'loss_target': out['loss_target'], 'm_g_mix_pre': out['m_g_mix_pre'], 'm_w_in': out['m_w_in'], 'm_b_f': out['m_b_f'], 'm_rel_bias': out['m_rel_bias'], 'm_w_out': out['m_w_out'], 'm_g_mix_post': out['m_g_mix_post'], 'm_g_xattn_pre': out['m_g_xattn_pre'], 'm_g_mem': out['m_g_mem'], 'm_w_xq': out['m_w_xq'], 'm_w_xk': out['m_w_xk'], 'm_w_xv': out['m_w_xv'], 'm_w_xo': out['m_w_xo'], 'm_g_xattn_post': out['m_g_xattn_post'], 'm_g_ffn_pre': out['m_g_ffn_pre'], 'm_w_gate': out['m_w_gate'], 'm_w_up': out['m_w_up'], 'm_w_down': out['m_w_down'], 'm_g_ffn_post': out['m_g_ffn_post'], 'v_g_mix_pre': out['v_g_mix_pre'], 'v_w_in': out['v_w_in'], 'v_b_f': out['v_b_f'], 'v_rel_bias': out['v_rel_bias'], 'v_w_out': out['v_w_out'], 'v_g_mix_post': out['v_g_mix_post'], 'v_g_xattn_pre': out['v_g_xattn_pre'], 'v_g_mem': out['v_g_mem'], 'v_w_xq': out['v_w_xq'], 'v_w_xk': out['v_w_xk'], 'v_w_xv': out['v_w_xv'], 'v_w_xo': out['v_w_xo'], 'v_g_xattn_post': out['v_g_xattn_post'], 'v_g_ffn_pre': out['v_g_ffn_pre'], 'v_w_gate': out['v_w_gate'], 'v_w_up': out['v_w_up'], 'v_w_down': out['v_w_down'], 'v_g_ffn_post': out['v_g_ffn_post']}


def _loss(weights, diff, rest, loss_target):
    with _jax.named_scope("forward"):
        args = {**rest, TWIN_DIFF_INPUT: diff, **{k: w.astype(_WEIGHT_DTYPES[k]) for k, w in weights.items()}}
        y = _forward(args)
    with _jax.named_scope("loss_head"):
        err = _jnp.square(y.astype(_jnp.float32) - loss_target)
        return 0.5 * _jnp.sum(_jnp.mean(err, axis=-1)) if err.ndim else 0.5 * err


def _adamw(w, g, m, v):
    m = ADAM_B1 * m + (1.0 - ADAM_B1) * g
    v = ADAM_B2 * v + (1.0 - ADAM_B2) * _jnp.square(g)
    m_hat = m / (1.0 - ADAM_B1 ** ADAM_STEP)
    v_hat = v / (1.0 - ADAM_B2 ** ADAM_STEP)
    delta = -ADAM_LR * (m_hat / (_jnp.sqrt(v_hat) + ADAM_EPS) + ADAM_WD * w)
    return delta, m, v


def reference(x, mem, g_mix_pre, w_in, b_f, rel_bias, w_out, g_mix_post, g_xattn_pre, g_mem, w_xq, w_xk, w_xv, w_xo, g_xattn_post, g_ffn_pre, w_gate, w_up, w_down, g_ffn_post, loss_target, m_g_mix_pre, m_w_in, m_b_f, m_rel_bias, m_w_out, m_g_mix_post, m_g_xattn_pre, m_g_mem, m_w_xq, m_w_xk, m_w_xv, m_w_xo, m_g_xattn_post, m_g_ffn_pre, m_w_gate, m_w_up, m_w_down, m_g_ffn_post, v_g_mix_pre, v_w_in, v_b_f, v_rel_bias, v_w_out, v_g_mix_post, v_g_xattn_pre, v_g_mem, v_w_xq, v_w_xk, v_w_xv, v_w_xo, v_g_xattn_post, v_g_ffn_pre, v_w_gate, v_w_up, v_w_down, v_g_ffn_post):
    given = dict(x=x, mem=mem, g_mix_pre=g_mix_pre, w_in=w_in, b_f=b_f, rel_bias=rel_bias, w_out=w_out, g_mix_post=g_mix_post, g_xattn_pre=g_xattn_pre, g_mem=g_mem, w_xq=w_xq, w_xk=w_xk, w_xv=w_xv, w_xo=w_xo, g_xattn_post=g_xattn_post, g_ffn_pre=g_ffn_pre, w_gate=w_gate, w_up=w_up, w_down=w_down, g_ffn_post=g_ffn_post, loss_target=loss_target, m_g_mix_pre=m_g_mix_pre, m_w_in=m_w_in, m_b_f=m_b_f, m_rel_bias=m_rel_bias, m_w_out=m_w_out, m_g_mix_post=m_g_mix_post, m_g_xattn_pre=m_g_xattn_pre, m_g_mem=m_g_mem, m_w_xq=m_w_xq, m_w_xk=m_w_xk, m_w_xv=m_w_xv, m_w_xo=m_w_xo, m_g_xattn_post=m_g_xattn_post, m_g_ffn_pre=m_g_ffn_pre, m_w_gate=m_w_gate, m_w_up=m_w_up, m_w_down=m_w_down, m_g_ffn_post=m_g_ffn_post, v_g_mix_pre=v_g_mix_pre, v_w_in=v_w_in, v_b_f=v_b_f, v_rel_bias=v_rel_bias, v_w_out=v_w_out, v_g_mix_post=v_g_mix_post, v_g_xattn_pre=v_g_xattn_pre, v_g_mem=v_g_mem, v_w_xq=v_w_xq, v_w_xk=v_w_xk, v_w_xv=v_w_xv, v_w_xo=v_w_xo, v_g_xattn_post=v_g_xattn_post, v_g_ffn_pre=v_g_ffn_pre, v_w_gate=v_w_gate, v_w_up=v_w_up, v_w_down=v_w_down, v_g_ffn_post=v_g_ffn_post)
    weights = {n: given[n] for n in TWIN_WEIGHTS}
    shared = {n: given[n] for n in SHARED_INPUTS}
    per_example = {n: given[n] for n in ['x', 'mem']}
    grad_fn = _jax.value_and_grad(_loss, argnums=(0, 1))

    def one_microbatch(ex, loss_target):
        ex = dict(ex)
        diff = ex.pop(TWIN_DIFF_INPUT)
        return grad_fn(weights, diff, {**shared, **ex}, loss_target)

    if N_MICROBATCH == 1:
        loss, (grad_w, grad_x) = one_microbatch(per_example, given["loss_target"])
    else:
        def body(carry, xs):
            loss_sum, grad_sum = carry
            l_k, (gw_k, gx_k) = one_microbatch(xs[0], xs[1])
            with _jax.named_scope("update"):
                return (loss_sum + l_k, _jax.tree.map(_jnp.add, grad_sum, gw_k)), gx_k

        init = (_jnp.zeros((), _jnp.float32), _jax.tree.map(_jnp.zeros_like, weights))
        (loss, grad_w), grad_x = _jax.lax.scan(body, init, (per_example, given["loss_target"]))
    with _jax.named_scope("update"):
        delta_w, new_m, new_v = {}, {}, {}
        for n in TWIN_WEIGHTS:
            delta_w[n], new_m[n], new_v[n] = _adamw(weights[n], grad_w[n], given["m_" + n], given["v_" + n])
    return (loss, grad_x, *[grad_w[n] for n in TWIN_WEIGHTS], *[delta_w[n] for n in TWIN_WEIGHTS],
            *[new_m[n] for n in TWIN_WEIGHTS], *[new_v[n] for n in TWIN_WEIGHTS])
```

```python
import numpy as np
import jax
import jax.numpy as jnp
from jax import lax
from jax.experimental import pallas as pl
from jax.experimental.pallas import tpu as pltpu

F32 = jnp.float32
BF16 = jnp.bfloat16
MESH_IDS = pl.DeviceIdType.MESH

LANES = 128
HEAD_DIM = 64
N_FOX_HEADS = 8
N_DIL_HEADS = 8
N_MEM_HEADS = 4
FOX_WIDTH = N_FOX_HEADS * HEAD_DIM
DIL_WIDTH = N_DIL_HEADS * HEAD_DIM
DILATIONS = (1, 4, 16)
BAND = 128
BAND_CHUNK_MAX = 8 * BAND
N_BUCKETS = 32
MAX_DISTANCE = 2048
QK_SCALE = HEAD_DIM ** -0.5
RMS_EPS = 1e-6
NEG = -1e30
VMEM_LIMIT = 56 << 20

ADAM_LR = 0.001
ADAM_B1 = 0.9
ADAM_B2 = 0.999
ADAM_EPS = 1e-08
ADAM_WD = 0.01
ADAM_STEP = 10

N_CHIPS = 4
PACK_COLS = 1024


def _params(*sem):
    return pltpu.CompilerParams(dimension_semantics=sem, vmem_limit_bytes=VMEM_LIMIT)


def _fit(n, cap):
    if n <= cap:
        return n
    t = (cap // LANES) * LANES
    while t >= LANES:
        if n % t == 0:
            return t
        t -= LANES
    raise ValueError(f"no lane-aligned tile for {n} under {cap}")


def _dot(a, b, dims):
    return lax.dot_general(a, b, (dims, ((), ())), preferred_element_type=F32)


_NN = ((1,), (0,))
_NT = ((1,), (1,))
_TN = ((0,), (0,))


def _mm(a, b, mode, out_dtype, name, tm=1024, tn=1024, tk=512):
    if mode == "nn":
        (M, K), N = a.shape, b.shape[1]
    elif mode == "nt":
        (M, K), N = a.shape, b.shape[0]
    else:
        (K, M), N = a.shape, b.shape[1]
    tm, tn, tk = _fit(M, tm), _fit(N, tn), _fit(K, tk)
    nk = K // tk
    if mode == "tn":
        a_spec = pl.BlockSpec((tk, tm), lambda i, j, k: (k, i))
    else:
        a_spec = pl.BlockSpec((tm, tk), lambda i, j, k: (i, k))
    if mode == "nt":
        b_spec = pl.BlockSpec((tn, tk), lambda i, j, k: (j, k))
    else:
        b_spec = pl.BlockSpec((tk, tn), lambda i, j, k: (k, j))
    dims = {"nn": _NN, "nt": _NT, "tn": _TN}[mode]

    def body(a_ref, b_ref, o_ref, acc_ref):
        k = pl.program_id(2)

        @pl.when(k == 0)
        def _():
            acc_ref[...] = jnp.zeros_like(acc_ref)

        acc_ref[...] += _dot(a_ref[...].astype(BF16), b_ref[...].astype(BF16), dims)

        @pl.when(k == nk - 1)
        def _():
            o_ref[...] = acc_ref[...].astype(o_ref.dtype)

    return pl.pallas_call(
        body, name=name, grid=(M // tm, N // tn, nk),
        in_specs=[a_spec, b_spec],
        out_specs=pl.BlockSpec((tm, tn), lambda i, j, k: (i, j)),
        out_shape=jax.ShapeDtypeStruct((M, N), out_dtype),
        scratch_shapes=[pltpu.VMEM((tm, tn), F32)],
        compiler_params=_params("parallel", "parallel", "arbitrary"),
    )(a, b)


def _rms_rows(x):
    return lax.rsqrt(jnp.mean(x * x, axis=-1, keepdims=True) + RMS_EPS)


def _rms_fwd(x, g, name, tr=512):
    S, D = x.shape
    tr = _fit(S, tr)

    def body(x_ref, g_ref, h_ref):
        xv = x_ref[...]
        h_ref[...] = (xv * _rms_rows(xv) * g_ref[...]).astype(BF16)

    return pl.pallas_call(
        body, name=name, grid=(S // tr,),
        in_specs=[pl.BlockSpec((tr, D), lambda i: (i, 0)), pl.BlockSpec((1, D), lambda i: (0, 0))],
        out_specs=pl.BlockSpec((tr, D), lambda i: (i, 0)),
        out_shape=jax.ShapeDtypeStruct((S, D), BF16),
        compiler_params=_params("parallel"),
    )(x, g)


def _resid_norm(xres, y, g_post, g_next, name, tr=512):
    S, D = xres.shape
    tr = _fit(S, tr)

    def body(x_ref, y_ref, gp_ref, gn_ref, xn_ref, h_ref):
        yv = y_ref[...]
        xn = x_ref[...] + yv * _rms_rows(yv) * gp_ref[...]
        xn_ref[...] = xn
        h_ref[...] = (xn * _rms_rows(xn) * gn_ref[...]).astype(BF16)

    row = pl.BlockSpec((tr, D), lambda i: (i, 0))
    vec = pl.BlockSpec((1, D), lambda i: (0, 0))
    return pl.pallas_call(
        body, name=name, grid=(S // tr,),
        in_specs=[row, row, vec, vec], out_specs=[row, row],
        out_shape=[jax.ShapeDtypeStruct((S, D), F32), jax.ShapeDtypeStruct((S, D), BF16)],
        compiler_params=_params("parallel"),
    )(xres, y, g_post, g_next)


def _final_loss(xres, y, g_post, target, name, tr=512):
    S, D = xres.shape
    tr = _fit(S, tr)

    def body(x_ref, y_ref, gp_ref, t_ref, d_ref, loss_ref):
        i = pl.program_id(0)
        yv = y_ref[...]
        err = x_ref[...] + yv * _rms_rows(yv) * gp_ref[...] - t_ref[...]
        d_ref[...] = err * (1.0 / D)

        @pl.when(i == 0)
        def _():
            loss_ref[...] = jnp.zeros_like(loss_ref)

        part = jnp.sum(jnp.sum(err * err, axis=1, keepdims=True), axis=0, keepdims=True)
        loss_ref[...] += jnp.broadcast_to(part * (0.5 / D), loss_ref.shape)

    row = pl.BlockSpec((tr, D), lambda i: (i, 0))
    vec = pl.BlockSpec((1, D), lambda i: (0, 0))
    return pl.pallas_call(
        body, name=name, grid=(S // tr,),
        in_specs=[row, row, vec, row],
        out_specs=[row, pl.BlockSpec((8, LANES), lambda i: (0, 0))],
        out_shape=[jax.ShapeDtypeStruct((S, D), F32), jax.ShapeDtypeStruct((8, LANES), F32)],
        compiler_params=_params("arbitrary"),
    )(xres, y, g_post, target)


def _rms_bwd(xin, g, dy, dres, name, tr=512):
    S, D = xin.shape
    tr = _fit(S, tr)
    has_res = dres is not None

    def body(*refs):
        if has_res:
            x_ref, g_ref, dy_ref, dr_ref, dx_ref, dxb_ref, dg_ref = refs
        else:
            x_ref, g_ref, dy_ref, dx_ref, dxb_ref, dg_ref = refs
        i = pl.program_id(0)
        xv = x_ref[...]
        dyv = dy_ref[...].astype(F32)
        xhat = xv * _rms_rows(xv)
        dxhat = dyv * g_ref[...]
        r = _rms_rows(xv)
        dx = r * (dxhat - xhat * jnp.mean(dxhat * xhat, axis=-1, keepdims=True))
        if has_res:
            dx = dx + dr_ref[...]
        dx_ref[...] = dx
        dxb_ref[...] = dx.astype(BF16)

        @pl.when(i == 0)
        def _():
            dg_ref[...] = jnp.zeros_like(dg_ref)

        dg_ref[...] += jnp.broadcast_to(jnp.sum(dyv * xhat, axis=0, keepdims=True), dg_ref.shape)

    row = pl.BlockSpec((tr, D), lambda i: (i, 0))
    vec = pl.BlockSpec((1, D), lambda i: (0, 0))
    acc = pl.BlockSpec((8, D), lambda i: (0, 0))
    ins = [xin, g, dy] + ([dres] if has_res else [])
    return pl.pallas_call(
        body, name=name, grid=(S // tr,),
        in_specs=[row, vec, row] + ([row] if has_res else []),
        out_specs=[row, row, acc],
        out_shape=[jax.ShapeDtypeStruct((S, D), F32), jax.ShapeDtypeStruct((S, D), BF16),
                   jax.ShapeDtypeStruct((8, D), F32)],
        compiler_params=_params("arbitrary"),
    )(*ins)


def _tri(n, upper):
    r = lax.broadcasted_iota(jnp.int32, (n, n), 0)
    c = lax.broadcasted_iota(jnp.int32, (n, n), 1)
    return jnp.where((r <= c) if upper else (r >= c), 1.0, 0.0).astype(F32)


def _forget_fwd(fg_t, b_col, name, ts=512):
    H, S = fg_t.shape
    ts = _fit(S, ts)

    def body(f_ref, b_ref, c_ref, carry_ref):
        i = pl.program_id(0)

        @pl.when(i == 0)
        def _():
            carry_ref[...] = jnp.zeros_like(carry_ref)

        z = f_ref[...] + b_ref[...]
        logf = jnp.minimum(z, 0.0) - jnp.log(1.0 + jnp.exp(-jnp.abs(z)))
        run = lax.dot_general(logf, _tri(ts, True), (_NN, ((), ())), precision=lax.Precision.HIGHEST,
                              preferred_element_type=F32) + carry_ref[:, 0:1]
        c_ref[...] = run
        carry_ref[...] = jnp.broadcast_to(
            carry_ref[:, 0:1] + jnp.sum(logf, axis=1, keepdims=True), carry_ref.shape)

    return pl.pallas_call(
        body, name=name, grid=(S // ts,),
        in_specs=[pl.BlockSpec((H, ts), lambda i: (0, i)), pl.BlockSpec((H, 1), lambda i: (0, 0))],
        out_specs=pl.BlockSpec((H, ts), lambda i: (0, i)),
        out_shape=jax.ShapeDtypeStruct((H, S), F32),
        scratch_shapes=[pltpu.VMEM((H, LANES), F32)],
        compiler_params=_params("arbitrary"),
    )(fg_t, b_col)


def _forget_bwd(fg_t, b_col, dc_plus, dc_minus, name, ts=512):
    H, S = fg_t.shape
    ts = _fit(S, ts)
    nb = S // ts

    def body(f_ref, b_ref, dcp_ref, dcm_ref, df_ref, db_ref, carry_ref):
        i = pl.program_id(0)

        @pl.when(i == 0)
        def _():
            carry_ref[...] = jnp.zeros_like(carry_ref)
            db_ref[...] = jnp.zeros_like(db_ref)

        dc = dcp_ref[...] - dcm_ref[...]
        suffix = lax.dot_general(dc, _tri(ts, False), (_NN, ((), ())), precision=lax.Precision.HIGHEST,
                                 preferred_element_type=F32) + carry_ref[:, 0:1]
        z = f_ref[...] + b_ref[...]
        sig_neg = 1.0 / (1.0 + jnp.exp(z))
        df = suffix * sig_neg
        df_ref[...] = df
        carry_ref[...] = jnp.broadcast_to(
            carry_ref[:, 0:1] + jnp.sum(dc, axis=1, keepdims=True), carry_ref.shape)
        db_ref[...] += jnp.broadcast_to(jnp.sum(df, axis=1, keepdims=True), db_ref.shape)

    rev = pl.BlockSpec((H, ts), lambda i: (0, nb - 1 - i))
    return pl.pallas_call(
        body, name=name, grid=(nb,),
        in_specs=[rev, pl.BlockSpec((H, 1), lambda i: (0, 0)), rev, rev],
        out_specs=[rev, pl.BlockSpec((H, LANES), lambda i: (0, 0))],
        out_shape=[jax.ShapeDtypeStruct((H, S), F32), jax.ShapeDtypeStruct((H, LANES), F32)],
        scratch_shapes=[pltpu.VMEM((H, LANES), F32)],
        compiler_params=_params("arbitrary"),
    )(fg_t, b_col, dc_plus, dc_minus)


def _tile_lanes(x, n):
    return x if n == LANES else jnp.tile(x, (1, n // LANES))


def _fox_fwd(q, k, v, c_row, name, tq=512):
    H, S, Dh = q.shape
    tq = _fit(S, tq)
    tk = tq
    nq = S // tq

    def body(q_ref, k_ref, v_ref, c_ref, o_ref, lse_ref, m_ref, l_ref, acc_ref):
        i = pl.program_id(1)
        qv = q_ref[...] * QK_SCALE
        cq0 = c_ref[:, pl.ds(pl.multiple_of(i * tq, LANES), LANES)][:, 0:1]
        m_ref[...] = jnp.full_like(m_ref, NEG)
        l_ref[...] = jnp.zeros_like(l_ref)
        acc_ref[...] = jnp.zeros_like(acc_ref)

        def step(j, diagonal):
            off = pl.multiple_of(j * tk, LANES)
            s = _dot(qv, k_ref[pl.ds(off, tk), :], _NT) + (cq0 - c_ref[:, pl.ds(off, tk)])
            if diagonal:
                row = lax.broadcasted_iota(jnp.int32, (tq, tk), 0)
                col = lax.broadcasted_iota(jnp.int32, (tq, tk), 1)
                s = jnp.where(col <= row, s, NEG)
            m_old = m_ref[...]
            m_new = jnp.maximum(m_old, jnp.max(s, axis=1, keepdims=True))
            p = jnp.exp(s - m_new)
            alpha = jnp.exp(m_old - m_new)
            l_ref[...] = alpha * l_ref[...] + jnp.sum(p, axis=1, keepdims=True)
            acc_ref[...] = alpha * acc_ref[...] + _dot(p.astype(BF16), v_ref[pl.ds(off, tk), :], _NN)
            m_ref[...] = m_new

        def loop_body(j, carry):
            step(j, False)
            return carry

        lax.fori_loop(0, i, loop_body, 0)
        step(i, True)
        o_ref[...] = acc_ref[...] / l_ref[...]
        lse_ref[...] = jnp.broadcast_to(m_ref[...] + jnp.log(l_ref[...]), lse_ref.shape)

    full = pl.BlockSpec((None, S, Dh), lambda h, i: (h, 0, 0))
    return pl.pallas_call(
        body, name=name, grid=(H, nq),
        in_specs=[pl.BlockSpec((None, tq, Dh), lambda h, i: (h, i, 0)), full, full,
                  pl.BlockSpec((None, 1, S), lambda h, i: (h, 0, 0))],
        out_specs=[pl.BlockSpec((None, tq, Dh), lambda h, i: (h, i, 0)),
                   pl.BlockSpec((None, tq, LANES), lambda h, i: (h, i, 0))],
        out_shape=[jax.ShapeDtypeStruct((H, S, Dh), F32), jax.ShapeDtypeStruct((H, S, LANES), F32)],
        scratch_shapes=[pltpu.VMEM((tq, 1), F32), pltpu.VMEM((tq, 1), F32), pltpu.VMEM((tq, Dh), F32)],
        compiler_params=_params("parallel", "arbitrary"),
    )(q, k, v, c_row)


def _fox_bwd(q, k, v, do, c_row, c_rep, lse_row, delta_row, name, tq=512):
    H, S, Dh = q.shape
    tq = _fit(S, tq)
    tk = tq
    nk = S // tk

    def body(k_ref, v_ref, crep_ref, q_ref, do_ref, c_ref, lse_ref, dl_ref,
             dq_ref, dk_ref, dv_ref, dcs_ref, drs_ref, dka_ref, dva_ref, dca_ref):
        j = pl.program_id(1)

        @pl.when(j == 0)
        def _():
            dq_ref[...] = jnp.zeros_like(dq_ref)
            drs_ref[...] = jnp.zeros_like(drs_ref)

        kv = k_ref[...]
        vv = v_ref[...]
        c_col = _tile_lanes(crep_ref[...], tq)
        dka_ref[...] = jnp.zeros_like(dka_ref)
        dva_ref[...] = jnp.zeros_like(dva_ref)
        dca_ref[...] = jnp.zeros_like(dca_ref)

        def step(i, diagonal):
            off = pl.multiple_of(i * tq, LANES)
            qv = q_ref[pl.ds(off, tq), :] * QK_SCALE
            dov = do_ref[pl.ds(off, tq), :]
            cq0 = c_ref[:, pl.ds(off, LANES)][:, 0:1]
            e = _dot(kv, qv, _NT) + (cq0 - lse_ref[:, pl.ds(off, tq)]) - c_col
            if diagonal:
                key = lax.broadcasted_iota(jnp.int32, (tk, tq), 0)
                qry = lax.broadcasted_iota(jnp.int32, (tk, tq), 1)
                e = jnp.where(key <= qry, e, NEG)
            p_t = jnp.exp(e)
            dva_ref[...] += _dot(p_t.astype(BF16), dov, _NN)
            ds_t = p_t * (_dot(vv, dov, _NT) - dl_ref[:, pl.ds(off, tq)])
            ds_b = ds_t.astype(BF16)
            dka_ref[...] += _dot(ds_b, qv, _NN)
            dq_ref[pl.ds(off, tq), :] += _dot(ds_b, kv, _TN) * QK_SCALE
            drs_ref[:, pl.ds(off, tq)] += jnp.sum(ds_t, axis=0, keepdims=True)
            part = ds_t[:, 0:LANES]
            for cidx in range(1, tq // LANES):
                part = part + ds_t[:, cidx * LANES:(cidx + 1) * LANES]
            dca_ref[...] += part

        step(j, True)

        def loop_body(i, carry):
            step(i, False)
            return carry

        lax.fori_loop(j + 1, nk, loop_body, 0)
        dk_ref[...] = dka_ref[...]
        dv_ref[...] = dva_ref[...]
        dcs_ref[...] = jnp.broadcast_to(jnp.sum(dca_ref[...], axis=1, keepdims=True), dcs_ref.shape)

    tile = pl.BlockSpec((None, tk, Dh), lambda h, j: (h, j, 0))
    rep = pl.BlockSpec((None, tk, LANES), lambda h, j: (h, j, 0))
    full = pl.BlockSpec((None, S, Dh), lambda h, j: (h, 0, 0))
    rowv = pl.BlockSpec((None, 1, S), lambda h, j: (h, 0, 0))
    return pl.pallas_call(
        body, name=name, grid=(H, nk),
        in_specs=[tile, tile, rep, full, full, rowv, rowv, rowv],
        out_specs=[full, tile, tile, rep, rowv],
        out_shape=[jax.ShapeDtypeStruct((H, S, Dh), F32), jax.ShapeDtypeStruct((H, S, Dh), F32),
                   jax.ShapeDtypeStruct((H, S, Dh), F32), jax.ShapeDtypeStruct((H, S, LANES), F32),
                   jax.ShapeDtypeStruct((H, 1, S), F32)],
        scratch_shapes=[pltpu.VMEM((tk, Dh), F32), pltpu.VMEM((tk, Dh), F32), pltpu.VMEM((tk, LANES), F32)],
        compiler_params=_params("parallel", "arbitrary"),
    )(k, v, c_rep, q, do, c_row, lse_row, delta_row)


def _band_chunk(S):
    chunk = min(BAND_CHUNK_MAX, S // DILATIONS[-1])
    assert chunk % BAND == 0 and S % (chunk * DILATIONS[-1]) == 0
    return chunk


def _band_specs(S):
    chunk_rows = _band_chunk(S)
    chunk = pl.BlockSpec((None, None, chunk_rows, HEAD_DIM), lambda p, h, i: (p, h, i, 0))
    prev = pl.BlockSpec((None, None, BAND, HEAD_DIM),
                        lambda p, h, i: (p, h, jnp.maximum(i * (chunk_rows // BAND) - 1, 0), 0))
    bias = pl.BlockSpec((None, None, BAND, 2 * BAND), lambda p, h, i: (p, h, 0, 0))
    rep = pl.BlockSpec((None, None, chunk_rows, LANES), lambda p, h, i: (p, h, i, 0))
    return chunk, prev, bias, rep


def _band_first_scale(S, p, i):
    per_class = jnp.right_shift(jnp.int32(S // _band_chunk(S)), 2 * p)
    return (i % per_class) == 0


def _band_scores(qb, kb, bias, first_block):
    s = _dot(qb, kb, _NT) + bias
    if first_block is not None:
        col = lax.broadcasted_iota(jnp.int32, s.shape, 1)
        s = jnp.where(jnp.logical_and(first_block, col < BAND), NEG, s)
    return s


def _band_lse(q, k, bias, name):
    P, H, S, Dh = q.shape
    BAND_CHUNK = _band_chunk(S)
    nb = BAND_CHUNK // BAND
    chunk, prev, bias_spec, rep = _band_specs(S)

    def body(q_ref, k_ref, kp_ref, b_ref, lse_ref, kext_ref):
        p, i = pl.program_id(0), pl.program_id(2)
        first = _band_first_scale(S, p, i)
        kext_ref[0:BAND, :] = kp_ref[...]
        kext_ref[BAND:, :] = k_ref[...]
        bias_v = b_ref[...]
        for b in range(nb):
            qb = q_ref[b * BAND:(b + 1) * BAND, :] * QK_SCALE
            s = _band_scores(qb, kext_ref[b * BAND:(b + 2) * BAND, :], bias_v, first if b == 0 else None)
            m = jnp.max(s, axis=1, keepdims=True)
            lse = m + jnp.log(jnp.sum(jnp.exp(s - m), axis=1, keepdims=True))
            lse_ref[b * BAND:(b + 1) * BAND, :] = jnp.broadcast_to(lse, (BAND, LANES))

    return pl.pallas_call(
        body, name=name, grid=(P, H, S // BAND_CHUNK),
        in_specs=[chunk, chunk, prev, bias_spec], out_specs=rep,
        out_shape=jax.ShapeDtypeStruct((P, H, S, LANES), F32),
        scratch_shapes=[pltpu.VMEM((BAND_CHUNK + BAND, Dh), BF16)],
        compiler_params=_params("parallel", "parallel", "parallel"),
    )(q, k, k, bias)


def _band_out(q, k, v, bias, lse_rep, name):
    P, H, S, Dh = q.shape
    BAND_CHUNK = _band_chunk(S)
    nb = BAND_CHUNK // BAND
    chunk, prev, bias_spec, rep = _band_specs(S)

    def body(q_ref, k_ref, kp_ref, v_ref, vp_ref, b_ref, lse_ref, o_ref, kext_ref, vext_ref):
        p, i = pl.program_id(0), pl.program_id(2)
        first = _band_first_scale(S, p, i)
        kext_ref[0:BAND, :] = kp_ref[...]
        kext_ref[BAND:, :] = k_ref[...]
        vext_ref[0:BAND, :] = vp_ref[...]
        vext_ref[BAND:, :] = v_ref[...]
        bias_v = b_ref[...]
        for b in range(nb):
            rows = slice(b * BAND, (b + 1) * BAND)
            ext = slice(b * BAND, (b + 2) * BAND)
            s = _band_scores(q_ref[rows, :] * QK_SCALE, kext_ref[ext, :], bias_v, first if b == 0 else None)
            pr = jnp.exp(s - _tile_lanes(lse_ref[rows, :], 2 * BAND))
            o_ref[rows, :] = _dot(pr.astype(BF16), vext_ref[ext, :], _NN)

    return pl.pallas_call(
        body, name=name, grid=(P, H, S // BAND_CHUNK),
        in_specs=[chunk, chunk, prev, chunk, prev, bias_spec, rep], out_specs=chunk,
        out_shape=jax.ShapeDtypeStruct((P, H, S, Dh), F32),
        scratch_shapes=[pltpu.VMEM((BAND_CHUNK + BAND, Dh), BF16), pltpu.VMEM((BAND_CHUNK + BAND, Dh), BF16)],
        compiler_params=_params("parallel", "parallel", "parallel"),
    )(q, k, k, v, v, bias, lse_rep)


def _band_bwd(q, k, v, do, bias, lse_rep, delta_rep, name):
    P, H, S, Dh = q.shape
    BAND_CHUNK = _band_chunk(S)
    nb = BAND_CHUNK // BAND
    nchunks = S // BAND_CHUNK
    chunk, prev, bias_spec, rep = _band_specs(S)
    nxt_idx = lambda p, h, i: (p, h, jnp.minimum((i + 1) * nb, S // BAND - 1), 0)
    nxt = pl.BlockSpec((None, None, BAND, Dh), nxt_idx)
    nxt_rep = pl.BlockSpec((None, None, BAND, LANES), nxt_idx)

    def body(q_ref, k_ref, kp_ref, v_ref, vp_ref, do_ref, b_ref, lse_ref, dl_ref,
             qn_ref, don_ref, lsen_ref, dln_ref,
             dq_ref, dk_ref, dv_ref, db_ref, kext_ref, vext_ref, dkext_ref, dvext_ref):
        p, i = pl.program_id(0), pl.program_id(2)
        first = _band_first_scale(S, p, i)
        next_first = jnp.logical_or(_band_first_scale(S, p, i + 1), i + 1 >= nchunks)
        tail = slice(BAND + BAND_CHUNK, 2 * BAND + BAND_CHUNK)
        kext_ref[0:BAND, :] = kp_ref[...]
        kext_ref[BAND:BAND + BAND_CHUNK, :] = k_ref[...]
        kext_ref[tail, :] = jnp.zeros((BAND, Dh), BF16)
        vext_ref[0:BAND, :] = vp_ref[...]
        vext_ref[BAND:BAND + BAND_CHUNK, :] = v_ref[...]
        vext_ref[tail, :] = jnp.zeros((BAND, Dh), BF16)
        dkext_ref[...] = jnp.zeros_like(dkext_ref)
        dvext_ref[...] = jnp.zeros_like(dvext_ref)
        bias_v = b_ref[...]

        @pl.when(i == 0)
        def _():
            db_ref[...] = jnp.zeros_like(db_ref)

        db_acc = jnp.zeros((BAND, 2 * BAND), F32)
        for b in range(nb):
            rows = slice(b * BAND, (b + 1) * BAND)
            ext = slice(b * BAND, (b + 2) * BAND)
            qb = q_ref[rows, :] * QK_SCALE
            dob = do_ref[rows, :]
            s = _band_scores(qb, kext_ref[ext, :], bias_v, first if b == 0 else None)
            pr = jnp.exp(s - _tile_lanes(lse_ref[rows, :], 2 * BAND))
            ds = pr * (_dot(dob, vext_ref[ext, :], _NT) - _tile_lanes(dl_ref[rows, :], 2 * BAND))
            ds_b = ds.astype(BF16)
            dq_ref[rows, :] = _dot(ds_b, kext_ref[ext, :], _NN) * QK_SCALE
            dkext_ref[ext, :] += _dot(ds_b, qb, _TN)
            dvext_ref[ext, :] += _dot(pr.astype(BF16), dob, _TN)
            db_acc = db_acc + ds
        db_ref[...] += db_acc

        ext = slice(BAND_CHUNK, BAND_CHUNK + 2 * BAND)
        qn = qn_ref[...] * QK_SCALE
        don = don_ref[...]
        sn = _dot(qn, kext_ref[ext, :], _NT) + bias_v
        col = lax.broadcasted_iota(jnp.int32, sn.shape, 1)
        sn = jnp.where(jnp.logical_or(next_first, col >= BAND), NEG, sn)
        prn = jnp.exp(sn - _tile_lanes(lsen_ref[...], 2 * BAND))
        dsn = prn * (_dot(don, vext_ref[ext, :], _NT) - _tile_lanes(dln_ref[...], 2 * BAND))
        dkext_ref[ext, :] += _dot(dsn.astype(BF16), qn, _TN)
        dvext_ref[ext, :] += _dot(prn.astype(BF16), don, _TN)
        dk_ref[...] = dkext_ref[BAND:BAND + BAND_CHUNK, :]
        dv_ref[...] = dvext_ref[BAND:BAND + BAND_CHUNK, :]

    return pl.pallas_call(
        body, name=name, grid=(P, H, nchunks),
        in_specs=[chunk, chunk, prev, chunk, prev, chunk, bias_spec, rep, rep, nxt, nxt, nxt_rep, nxt_rep],
        out_specs=[chunk, chunk, chunk, bias_spec],
        out_shape=[jax.ShapeDtypeStruct((P, H, S, Dh), F32)] * 3
                  + [jax.ShapeDtypeStruct((P, H, BAND, 2 * BAND), F32)],
        scratch_shapes=[pltpu.VMEM((BAND_CHUNK + 2 * BAND, Dh), BF16), pltpu.VMEM((BAND_CHUNK + 2 * BAND, Dh), BF16),
                        pltpu.VMEM((BAND_CHUNK + 2 * BAND, Dh), F32), pltpu.VMEM((BAND_CHUNK + 2 * BAND, Dh), F32)],
        compiler_params=_params("parallel", "parallel", "arbitrary"),
    )(q, k, k, v, v, do, bias, lse_rep, delta_rep, q, do, lse_rep, delta_rep)


def _lse_join(lse3, name):
    P, H, S = lse3.shape

    def body(l_ref, o_ref):
        a, b, c = l_ref[0], l_ref[1], l_ref[2]
        m = jnp.maximum(jnp.maximum(a, b), c)
        o_ref[...] = m + jnp.log(jnp.exp(a - m) + jnp.exp(b - m) + jnp.exp(c - m))

    return pl.pallas_call(body, name=name, out_shape=jax.ShapeDtypeStruct((H, S), F32))(lse3)


def _bucket_reduce(dbias, bucket_map, name):
    P, H = dbias.shape[:2]

    def body(db_ref, bk_ref, o_ref):
        p, h = pl.program_id(0), pl.program_id(1)

        @pl.when(jnp.logical_and(p == 0, h == 0))
        def _():
            o_ref[...] = jnp.zeros_like(o_ref)

        db, bk = db_ref[...], bk_ref[...]
        row = lax.broadcasted_iota(jnp.int32, (N_BUCKETS, LANES), 0)
        lane = lax.broadcasted_iota(jnp.int32, (N_BUCKETS, LANES), 1)

        def one(b, acc):
            val = jnp.sum(jnp.sum(jnp.where(bk == b, db, 0.0), axis=1, keepdims=True), axis=0, keepdims=True)
            return acc + jnp.where(jnp.logical_and(row == b, lane == h), val, 0.0)

        o_ref[...] += lax.fori_loop(0, N_BUCKETS, one, jnp.zeros((N_BUCKETS, LANES), F32))

    return pl.pallas_call(
        body, name=name, grid=(P, H),
        in_specs=[pl.BlockSpec((None, None, BAND, 2 * BAND), lambda p, h: (p, h, 0, 0)),
                  pl.BlockSpec((None, BAND, 2 * BAND), lambda p, h: (p, 0, 0))],
        out_specs=pl.BlockSpec((N_BUCKETS, LANES), lambda p, h: (0, 0)),
        out_shape=jax.ShapeDtypeStruct((N_BUCKETS, LANES), F32),
        compiler_params=_params("arbitrary", "arbitrary"),
    )(dbias, bucket_map)


def _mem_fwd(q, k, v, name, tq=1024):
    H, S, Dh = q.shape
    N = k.shape[1]
    tq = _fit(S, tq)

    def body(q_ref, k_ref, v_ref, o_ref, lse_ref):
        s = _dot(q_ref[...] * QK_SCALE, k_ref[...], _NT)
        m = jnp.max(s, axis=1, keepdims=True)
        e = jnp.exp(s - m)
        l = jnp.sum(e, axis=1, keepdims=True)
        o_ref[...] = _dot((e / l).astype(BF16), v_ref[...], _NN)
        lse_ref[...] = jnp.broadcast_to(m + jnp.log(l), lse_ref.shape)

    qs = pl.BlockSpec((None, tq, Dh), lambda h, i: (h, i, 0))
    ks = pl.BlockSpec((None, N, Dh), lambda h, i: (h, 0, 0))
    return pl.pallas_call(
        body, name=name, grid=(H, S // tq),
        in_specs=[qs, ks, ks],
        out_specs=[qs, pl.BlockSpec((None, tq, LANES), lambda h, i: (h, i, 0))],
        out_shape=[jax.ShapeDtypeStruct((H, S, Dh), F32), jax.ShapeDtypeStruct((H, S, LANES), F32)],
        compiler_params=_params("parallel", "parallel"),
    )(q, k, v)


def _mem_bwd(q, k, v, do, lse_rep, delta_rep, name, tq=1024):
    H, S, Dh = q.shape
    N = k.shape[1]
    tq = _fit(S, tq)

    def body(q_ref, k_ref, v_ref, do_ref, lse_ref, dl_ref, dq_ref, dk_ref, dv_ref):
        i = pl.program_id(1)

        @pl.when(i == 0)
        def _():
            dk_ref[...] = jnp.zeros_like(dk_ref)
            dv_ref[...] = jnp.zeros_like(dv_ref)

        qv = q_ref[...] * QK_SCALE
        dov = do_ref[...]
        pr = jnp.exp(_dot(qv, k_ref[...], _NT) - _tile_lanes(lse_ref[...], N))
        ds = pr * (_dot(dov, v_ref[...], _NT) - _tile_lanes(dl_ref[...], N))
        ds_b = ds.astype(BF16)
        dq_ref[...] = _dot(ds_b, k_ref[...], _NN) * QK_SCALE
        dk_ref[...] += _dot(ds_b, qv, _TN)
        dv_ref[...] += _dot(pr.astype(BF16), dov, _TN)

    qs = pl.BlockSpec((None, tq, Dh), lambda h, i: (h, i, 0))
    ks = pl.BlockSpec((None, N, Dh), lambda h, i: (h, 0, 0))
    rep = pl.BlockSpec((None, tq, LANES), lambda h, i: (h, i, 0))
    return pl.pallas_call(
        body, name=name, grid=(H, S // tq),
        in_specs=[qs, ks, ks, qs, rep, rep], out_specs=[qs, ks, ks],
        out_shape=[jax.ShapeDtypeStruct((H, S, Dh), F32), jax.ShapeDtypeStruct((H, N, Dh), F32),
                   jax.ShapeDtypeStruct((H, N, Dh), F32)],
        compiler_params=_params("parallel", "arbitrary"),
    )(q, k, v, do, lse_rep, delta_rep)


def _head_rowdot(a, b, n_heads, name, tr=512):
    S, W = a.shape
    tr = _fit(S, tr)

    def body(a_ref, b_ref, o_ref):
        prod = a_ref[...].astype(F32) * b_ref[...].astype(F32)
        col = lax.broadcasted_iota(jnp.int32, (W, LANES), 0)
        lane = lax.broadcasted_iota(jnp.int32, (W, LANES), 1)
        sel = jnp.where((col // HEAD_DIM) % n_heads == lane, 1.0, 0.0).astype(F32)
        o_ref[...] = lax.dot_general(prod, sel, (_NN, ((), ())), precision=lax.Precision.HIGHEST,
                                     preferred_element_type=F32)

    row = pl.BlockSpec((tr, W), lambda i: (i, 0))
    return pl.pallas_call(
        body, name=name, grid=(S // tr,), in_specs=[row, row],
        out_specs=pl.BlockSpec((tr, LANES), lambda i: (i, 0)),
        out_shape=jax.ShapeDtypeStruct((S, LANES), F32),
        compiler_params=_params("parallel"),
    )(a, b)


FF_TILE = 256


def _ffn_up(h, w_gu, name, tm=512):
    S, D = h.shape
    F2 = w_gu.shape[1]
    tm = _fit(S, tm)

    def body(h_ref, w_ref, gu_ref, act_ref):
        gu = _dot(h_ref[...], w_ref[...], _NN)
        gu_ref[...] = gu
        g, u = gu[:, :FF_TILE], gu[:, FF_TILE:]
        act_ref[...] = (g * (1.0 / (1.0 + jnp.exp(-g))) * u).astype(BF16)

    return pl.pallas_call(
        body, name=name, grid=(S // tm, F2 // (2 * FF_TILE)),
        in_specs=[pl.BlockSpec((tm, D), lambda i, j: (i, 0)), pl.BlockSpec((D, 2 * FF_TILE), lambda i, j: (0, j))],
        out_specs=[pl.BlockSpec((tm, 2 * FF_TILE), lambda i, j: (i, j)),
                   pl.BlockSpec((tm, FF_TILE), lambda i, j: (i, j))],
        out_shape=[jax.ShapeDtypeStruct((S, F2), F32), jax.ShapeDtypeStruct((S, F2 // 2), BF16)],
        compiler_params=_params("parallel", "arbitrary"),
    )(h, w_gu)


def _ffn_dact(dy, w_down, gu, name, tm=512):
    S, D = dy.shape
    F2 = gu.shape[1]
    tm = _fit(S, tm)

    def body(dy_ref, w_ref, gu_ref, dgu_ref):
        dact = _dot(dy_ref[...], w_ref[...], _NT)
        gu_v = gu_ref[...]
        g, u = gu_v[:, :FF_TILE], gu_v[:, FF_TILE:]
        sig = 1.0 / (1.0 + jnp.exp(-g))
        silu = g * sig
        dgu_ref[:, :FF_TILE] = (dact * u * (sig + silu * (1.0 - sig))).astype(BF16)
        dgu_ref[:, FF_TILE:] = (dact * silu).astype(BF16)

    return pl.pallas_call(
        body, name=name, grid=(S // tm, F2 // (2 * FF_TILE)),
        in_specs=[pl.BlockSpec((tm, D), lambda i, j: (i, 0)), pl.BlockSpec((FF_TILE, D), lambda i, j: (j, 0)),
                  pl.BlockSpec((tm, 2 * FF_TILE), lambda i, j: (i, j))],
        out_specs=pl.BlockSpec((tm, 2 * FF_TILE), lambda i, j: (i, j)),
        out_shape=jax.ShapeDtypeStruct((S, F2), BF16),
        compiler_params=_params("parallel", "arbitrary"),
    )(dy, w_down, gu)


def _add_n(arrs, name, tr=512):
    R, C = arrs[0].shape
    tr = _fit_rows(R, tr)

    def body(*refs):
        acc = refs[0][...]
        for r in refs[1:-1]:
            acc = acc + r[...]
        refs[-1][...] = acc

    row = pl.BlockSpec((tr, C), lambda i: (i, 0))
    return pl.pallas_call(
        body, name=name, grid=(R // tr,), in_specs=[row] * len(arrs), out_specs=row,
        out_shape=jax.ShapeDtypeStruct((R, C), F32), compiler_params=_params("parallel"),
    )(*arrs)


def _fit_rows(n, cap):
    if n <= cap:
        return n
    t = (cap // 8) * 8
    while t >= 8:
        if n % t == 0:
            return t
        t -= 8
    raise ValueError(f"no sublane-aligned tile for {n} under {cap}")


def _adamw(w, g, m, v, name, tr=512):
    R, C = w.shape
    tr = _fit_rows(R, tr)
    c1 = 1.0 / (1.0 - ADAM_B1 ** ADAM_STEP)
    c2 = 1.0 / (1.0 - ADAM_B2 ** ADAM_STEP)

    def body(w_ref, g_ref, m_ref, v_ref, d_ref, nm_ref, nv_ref):
        gv = g_ref[...]
        nm = ADAM_B1 * m_ref[...] + (1.0 - ADAM_B1) * gv
        nv = ADAM_B2 * v_ref[...] + (1.0 - ADAM_B2) * (gv * gv)
        nm_ref[...] = nm
        nv_ref[...] = nv
        d_ref[...] = -ADAM_LR * ((nm * c1) / (jnp.sqrt(nv * c2) + ADAM_EPS) + ADAM_WD * w_ref[...])

    row = pl.BlockSpec((tr, C), lambda i: (i, 0))
    return pl.pallas_call(
        body, name=name, grid=(R // tr,), in_specs=[row] * 4, out_specs=[row] * 3,
        out_shape=[jax.ShapeDtypeStruct((R, C), F32)] * 3, compiler_params=_params("parallel"),
    )(w, g, m, v)


def _place():
    return lax.axis_index("x"), lax.axis_index("y"), lax.axis_index("c")


_ANY = pl.BlockSpec(memory_space=pl.ANY)


def _chip_all_gather(shard, name):
    R, C = shard.shape

    def body(x_ref, out_ref, send_sems, recv_sems, local_sem):
        x, y, c = _place()
        chips = [(1 - x, y), (x, 1 - y), (1 - x, 1 - y)]
        mine = pltpu.make_async_copy(x_ref, out_ref.at[2 * x + y], local_sem)
        mine.start()

        def copy(k, slot, to):
            return pltpu.make_async_remote_copy(
                src_ref=x_ref, dst_ref=out_ref.at[slot], send_sem=send_sems.at[k], recv_sem=recv_sems.at[k],
                device_id=to, device_id_type=MESH_IDS)

        sends = [copy(k, 2 * x + y, (cx, cy, c)) for k, (cx, cy) in enumerate(chips)]
        for cp in sends:
            cp.start()
        for k, (cx, cy) in enumerate(chips):
            copy(k, 2 * cx + cy, (cx, cy, c)).wait_recv()
        for cp in sends:
            cp.wait_send()
        mine.wait()

    return pl.pallas_call(
        body, name=name, in_specs=[_ANY], out_specs=_ANY,
        out_shape=jax.ShapeDtypeStruct((N_CHIPS, R, C), shard.dtype),
        scratch_shapes=[pltpu.SemaphoreType.DMA((3,)), pltpu.SemaphoreType.DMA((3,)), pltpu.SemaphoreType.DMA],
    )(shard)


def _sibling_exchange(buf, name):
    def body(x_ref, out_ref, send_sem, recv_sem):
        x, y, c = _place()
        cp = pltpu.make_async_remote_copy(
            src_ref=x_ref, dst_ref=out_ref, send_sem=send_sem, recv_sem=recv_sem,
            device_id=(x, y, 1 - c), device_id_type=MESH_IDS)
        cp.start()
        cp.wait()

    return pl.pallas_call(
        body, name=name, in_specs=[_ANY], out_specs=_ANY,
        out_shape=jax.ShapeDtypeStruct(buf.shape, buf.dtype),
        scratch_shapes=[pltpu.SemaphoreType.DMA, pltpu.SemaphoreType.DMA],
    )(buf)


def _chip_scatter(parts, name):
    _, R, C = parts.shape

    def body(p_ref, out_ref, send_sems, recv_sems):
        x, y, c = _place()
        chips = [(1 - x, y), (x, 1 - y), (1 - x, 1 - y)]

        def copy(k, slab, to):
            return pltpu.make_async_remote_copy(
                src_ref=p_ref.at[slab], dst_ref=out_ref.at[k], send_sem=send_sems.at[k], recv_sem=recv_sems.at[k],
                device_id=to, device_id_type=MESH_IDS)

        sends = [copy(k, 2 * cx + cy, (cx, cy, c)) for k, (cx, cy) in enumerate(chips)]
        for cp in sends:
            cp.start()
        for cp in sends:
            cp.wait_recv()
        for cp in sends:
            cp.wait_send()

    return pl.pallas_call(
        body, name=name, in_specs=[_ANY], out_specs=_ANY,
        out_shape=jax.ShapeDtypeStruct((3, R, C), parts.dtype),
        scratch_shapes=[pltpu.SemaphoreType.DMA((3,)), pltpu.SemaphoreType.DMA((3,))],
    )(parts)


def _all_to_all_small(vec, name):
    R, C = vec.shape

    def body(v_ref, out_ref, send_sems, recv_sems, local_sem):
        x, y, c = _place()
        me = 4 * x + 2 * y + c
        mine = pltpu.make_async_copy(v_ref, out_ref.at[me], local_sem)
        mine.start()
        flips = [(dx, dy, dc) for dx in (0, 1) for dy in (0, 1) for dc in (0, 1)][1:]

        def peer(f):
            return (x ^ f[0], y ^ f[1], c ^ f[2])

        def copy(k, slot, to):
            return pltpu.make_async_remote_copy(
                src_ref=v_ref, dst_ref=out_ref.at[slot], send_sem=send_sems.at[k], recv_sem=recv_sems.at[k],
                device_id=to, device_id_type=MESH_IDS)

        sends = [copy(k, me, peer(f)) for k, f in enumerate(flips)]
        for cp in sends:
            cp.start()
        for k, f in enumerate(flips):
            px, py, pc = peer(f)
            copy(k, 4 * px + 2 * py + pc, peer(f)).wait_recv()
        for cp in sends:
            cp.wait_send()
        mine.wait()

    return pl.pallas_call(
        body, name=name, in_specs=[_ANY], out_specs=_ANY,
        out_shape=jax.ShapeDtypeStruct((8, R, C), vec.dtype),
        scratch_shapes=[pltpu.SemaphoreType.DMA((7,)), pltpu.SemaphoreType.DMA((7,)), pltpu.SemaphoreType.DMA],
    )(vec)


def _to_heads(t, n):
    S = t.shape[0]
    return t.reshape(S, n, HEAD_DIM).transpose(1, 0, 2)


def _from_heads(t):
    H, S, Dh = t.shape
    return t.transpose(1, 0, 2).reshape(S, H * Dh)


def _perm(t, d):
    H, S = t.shape[:2]
    rest = t.shape[2:]
    return t.reshape((H, S // d, d) + rest).swapaxes(1, 2).reshape((H, S) + rest)


def _unperm(t, d):
    H, S = t.shape[:2]
    rest = t.shape[2:]
    return t.reshape((H, d, S // d) + rest).swapaxes(1, 2).reshape((H, S) + rest)


def _rep(t):
    return jnp.broadcast_to(t[..., None], t.shape + (LANES,))


def _t5_bucket(dist):
    max_exact = N_BUCKETS // 2
    d = np.maximum(dist, 1).astype(np.float32)
    large = max_exact + (np.log(d / max_exact) / np.log(MAX_DISTANCE / max_exact)
                         * (N_BUCKETS - max_exact)).astype(np.int32)
    large = np.minimum(large, N_BUCKETS - 1)
    return np.where(dist < max_exact, dist, large).astype(np.int32)


def _band_tables():
    qi = np.arange(BAND)[:, None]
    kj = np.arange(2 * BAND)[None, :]
    sub = qi + BAND - kj
    band = (sub >= 0) & (sub <= BAND)
    out = []
    for d in DILATIONS:
        bucket = _t5_bucket(np.clip(sub, 0, BAND) * d)
        out.append(np.where(band, bucket, -1).astype(np.int32))
    return np.stack(out)


_PACK = (("w_in", 770), ("w_out", 256), ("w_xq", 64), ("w_xk", 64), ("w_xv", 64), ("w_xo", 64),
         ("w_gate", 704), ("w_up", 704), ("w_down", 704))


def _pack(shards):
    rows = [shards[n].reshape(-1, PACK_COLS) for n, _ in _PACK]
    total = sum(r.shape[0] for r in rows)
    pad = (-total) % 16
    if pad:
        rows.append(jnp.zeros((pad, PACK_COLS), rows[0].dtype))
    return jnp.concatenate(rows, axis=0)


def _unpack(pack, shapes):
    out, r = {}, 0
    for n, _ in _PACK:
        cnt = int(np.prod(shapes[n])) // PACK_COLS
        out[n] = pack[r:r + cnt].reshape(shapes[n])
        r += cnt
    return out


_COL_SHARDED = ("w_in", "w_xo", "w_gate", "w_up")


def _full_weight(gathered, name):
    return jnp.concatenate(gathered, axis=1 if name in _COL_SHARDED else 0)


def _split_weight(full, name):
    return jnp.split(full, N_CHIPS, axis=1 if name in _COL_SHARDED else 0)


_SMALL = ("g_mix_pre", "g_mix_post", "g_xattn_pre", "g_mem", "g_xattn_post", "g_ffn_pre", "g_ffn_post")


def _pack_small(vals):
    D = vals["g_mix_pre"].shape[1]
    rows = [vals[n].reshape(1, D) for n in _SMALL]
    misc = jnp.concatenate([vals["b_f"].reshape(-1), vals["rel_bias"].reshape(-1)])
    rows.append(jnp.pad(misc, (0, D - misc.shape[0])).reshape(1, D))
    rows.append(jnp.zeros((16 - len(rows), D), F32))
    return jnp.concatenate(rows, axis=0)


def _unpack_small(pack):
    out = {n: pack[i:i + 1] for i, n in enumerate(_SMALL)}
    out["b_f"] = pack[7, 0:N_FOX_HEADS].reshape(1, N_FOX_HEADS)
    out["rel_bias"] = pack[7, N_FOX_HEADS:N_FOX_HEADS + N_BUCKETS * N_DIL_HEADS].reshape(N_BUCKETS, N_DIL_HEADS)
    return out


def kernel(x, mem, g_mix_pre, w_in, b_f, rel_bias, w_out, g_mix_post, g_xattn_pre, g_mem, w_xq, w_xk, w_xv, w_xo, g_xattn_post, g_ffn_pre, w_gate, w_up, w_down, g_ffn_post, loss_target, m_g_mix_pre, m_w_in, m_b_f, m_rel_bias, m_w_out, m_g_mix_post, m_g_xattn_pre, m_g_mem, m_w_xq, m_w_xk, m_w_xv, m_w_xo, m_g_xattn_post, m_g_ffn_pre, m_w_gate, m_w_up, m_w_down, m_g_ffn_post, v_g_mix_pre, v_w_in, v_b_f, v_rel_bias, v_w_out, v_g_mix_post, v_g_xattn_pre, v_g_mem, v_w_xq, v_w_xk, v_w_xv, v_w_xo, v_g_xattn_post, v_g_ffn_pre, v_w_gate, v_w_up, v_w_down, v_g_ffn_post):
    args = dict(locals())
    big = [n for n, _ in _PACK]
    names = ["g_mix_pre", "w_in", "b_f", "rel_bias", "w_out", "g_mix_post", "g_xattn_pre", "g_mem", "w_xq",
             "w_xk", "w_xv", "w_xo", "g_xattn_post", "g_ffn_pre", "w_gate", "w_up", "w_down", "g_ffn_post"]
    xs = x[0]
    S, D = xs.shape
    assert S % (BAND * DILATIONS[-1]) == 0
    shard_shapes = {n: args[n].shape[1:] for n in big}
    my_x, my_y, my_c = lax.axis_index("x"), lax.axis_index("y"), lax.axis_index("c")

    w_pack = _pack({n: args[n][0] for n in big})
    gathered = _chip_all_gather(w_pack.astype(BF16), "weights_all_gather")
    per_chip = [_unpack(gathered[j], shard_shapes) for j in range(N_CHIPS)]
    W = {n: _full_weight([pc[n] for pc in per_chip], n) for n in big}
    w_fox, w_fg, w_dil = (W["w_in"][:, :3 * FOX_WIDTH], W["w_in"][:, 3 * FOX_WIDTH:3 * FOX_WIDTH + N_FOX_HEADS],
                          W["w_in"][:, 3 * FOX_WIDTH + N_FOX_HEADS:])
    w_qkv = jnp.concatenate([w_fox, w_dil], axis=1)
    w_fg_pad = jnp.pad(w_fg, ((0, 0), (0, LANES - N_FOX_HEADS)))
    F = W["w_gate"].shape[1]
    nft = F // FF_TILE
    w_gu = jnp.stack([W["w_gate"].reshape(D, nft, FF_TILE), W["w_up"].reshape(D, nft, FF_TILE)],
                     axis=2).reshape(D, 2 * F)

    h1 = _rms_fwd(xs, g_mix_pre, "rms_mix_pre")
    qkv = _mm(h1, w_qkv, "nn", BF16, "proj_qkv")
    fg = _mm(h1, w_fg_pad, "nn", F32, "proj_gate")
    fg_t = fg[:, :N_FOX_HEADS].T
    b_col = b_f.reshape(N_FOX_HEADS, 1)
    c_t = _forget_fwd(fg_t, b_col, "forget_cumsum")
    c_row = c_t.reshape(N_FOX_HEADS, 1, S)
    fq, fk, fv = (_to_heads(qkv[:, i * FOX_WIDTH:(i + 1) * FOX_WIDTH], N_FOX_HEADS) for i in range(3))
    o_fox, lse_fox = _fox_fwd(fq, fk, fv, c_row, "fox_fwd")

    dq_, dk_, dv_ = (_to_heads(qkv[:, 3 * FOX_WIDTH + i * DIL_WIDTH:3 * FOX_WIDTH + (i + 1) * DIL_WIDTH],
                               N_DIL_HEADS) for i in range(3))
    qd = jnp.stack([_perm(dq_, d) for d in DILATIONS])
    kd = jnp.stack([_perm(dk_, d) for d in DILATIONS])
    vd = jnp.stack([_perm(dv_, d) for d in DILATIONS])
    bucket_map = _band_tables()
    bias_tab = jnp.where(bucket_map[..., None] >= 0, rel_bias[np.maximum(bucket_map, 0)], NEG)
    bias_tab = bias_tab.transpose(0, 3, 1, 2)
    lse_p = _band_lse(qd, kd, bias_tab, "dilated_lse")[..., 0]
    lse_tok = jnp.stack([_unperm(lse_p[p], d) for p, d in enumerate(DILATIONS)])
    lse_joint = _lse_join(lse_tok, "dilated_lse_join")
    lse_joint_rep = _rep(jnp.stack([_perm(lse_joint, d) for d in DILATIONS]))
    o_p = _band_out(qd, kd, vd, bias_tab, lse_joint_rep, "dilated_out")
    o_cat = jnp.concatenate(
        [_from_heads(o_fox)] + [_from_heads(_unperm(o_p[p], d)) for p, d in enumerate(DILATIONS)],
        axis=1).astype(BF16)
    w_out_b = W["w_out"]
    w_out_cat = jnp.concatenate([w_out_b[:FOX_WIDTH]] + [w_out_b[FOX_WIDTH:]] * 3, axis=0)
    a = _mm(o_cat, w_out_cat, "nn", F32, "proj_out")
    x1, h2 = _resid_norm(xs, a, g_mix_post, g_xattn_pre, "resid_mix")

    hm = _rms_fwd(mem[0], g_mem, "rms_mem")
    q2 = _mm(h2, W["w_xq"], "nn", BF16, "xattn_q")
    w_xkv = jnp.concatenate([W["w_xk"], W["w_xv"]], axis=1)
    kvm = _mm(hm, w_xkv, "nn", BF16, "xattn_kv")
    MW = N_MEM_HEADS * HEAD_DIM
    q2h, kmh, vmh = _to_heads(q2, N_MEM_HEADS), _to_heads(kvm[:, :MW], N_MEM_HEADS), _to_heads(kvm[:, MW:], N_MEM_HEADS)
    o_mem, lse_mem = _mem_fwd(q2h, kmh, vmh, "xattn_fwd")
    oc = _from_heads(o_mem).astype(BF16)
    y2 = _mm(oc, W["w_xo"], "nn", F32, "xattn_o")
    x2, h3 = _resid_norm(x1, y2, g_xattn_post, g_ffn_pre, "resid_xattn")

    gu, act = _ffn_up(h3, w_gu, "ffn_up")
    y3 = _mm(act, W["w_down"], "nn", F32, "ffn_down", tk=1536)
    dx3, loss_tile = _final_loss(x2, y3, g_ffn_post, loss_target[0], "final_loss")

    grads = {}
    small = {}
    dy3, dy3_b, dg = _rms_bwd(y3, g_ffn_post, dx3, None, "bwd_norm_ffn_post")
    small["g_ffn_post"] = dg[0:1]
    grads["w_down"] = _mm(act, dy3_b, "tn", F32, "grad_w_down")
    dgu = _ffn_dact(dy3_b, W["w_down"], gu, "ffn_dact")
    dw_gu = _mm(h3, dgu, "tn", F32, "grad_w_gu").reshape(D, nft, 2, FF_TILE)
    grads["w_gate"], grads["w_up"] = dw_gu[:, :, 0].reshape(D, F), dw_gu[:, :, 1].reshape(D, F)
    dh3 = _mm(dgu, w_gu, "nt", F32, "bwd_ffn_in", tk=1024)
    dx2, _, dg = _rms_bwd(x2, g_ffn_pre, dh3, dx3, "bwd_norm_ffn_pre")
    small["g_ffn_pre"] = dg[0:1]

    dy2, dy2_b, dg = _rms_bwd(y2, g_xattn_post, dx2, None, "bwd_norm_xattn_post")
    small["g_xattn_post"] = dg[0:1]
    grads["w_xo"] = _mm(oc, dy2_b, "tn", F32, "grad_w_xo")
    doc = _mm(dy2_b, W["w_xo"], "nt", BF16, "bwd_xattn_o")
    delta_mem = _head_rowdot(doc, oc, N_MEM_HEADS, "xattn_delta")
    delta_mem_rep = _rep(delta_mem[:, :N_MEM_HEADS].T)
    dq2h, dkmh, dvmh = _mem_bwd(q2h, kmh, vmh, _to_heads(doc, N_MEM_HEADS), lse_mem, delta_mem_rep, "xattn_bwd")
    dq2 = _from_heads(dq2h).astype(BF16)
    dkvm = jnp.concatenate([_from_heads(dkmh), _from_heads(dvmh)], axis=1).astype(BF16)
    grads["w_xq"] = _mm(h2, dq2, "tn", F32, "grad_w_xq")
    dw_xkv = _mm(hm, dkvm, "tn", F32, "grad_w_xkv")
    grads["w_xk"], grads["w_xv"] = dw_xkv[:, :MW], dw_xkv[:, MW:]
    dhm = _mm(dkvm, w_xkv, "nt", F32, "bwd_xattn_kv")
    _, _, dg = _rms_bwd(mem[0], g_mem, dhm, None, "bwd_norm_mem")
    small["g_mem"] = dg[0:1]
    dh2 = _mm(dq2, W["w_xq"], "nt", F32, "bwd_xattn_q")
    dx1, _, dg = _rms_bwd(x1, g_xattn_pre, dh2, dx2, "bwd_norm_xattn_pre")
    small["g_xattn_pre"] = dg[0:1]

    da, da_b, dg = _rms_bwd(a, g_mix_post, dx1, None, "bwd_norm_mix_post")
    small["g_mix_post"] = dg[0:1]
    dw_out_cat = _mm(o_cat, da_b, "tn", F32, "grad_w_out")
    dw_out_dil = _add_n([dw_out_cat[FOX_WIDTH + p * DIL_WIDTH:FOX_WIDTH + (p + 1) * DIL_WIDTH] for p in range(3)],
                        "grad_w_out_dil")
    grads["w_out"] = jnp.concatenate([dw_out_cat[:FOX_WIDTH], dw_out_dil], axis=0)
    do = _mm(da_b, w_out_b, "nt", BF16, "bwd_proj_out")
    do_fox, do_dil = do[:, :FOX_WIDTH], do[:, FOX_WIDTH:]

    delta_fox = _head_rowdot(do_fox, o_cat[:, :FOX_WIDTH], N_FOX_HEADS, "fox_delta")[:, :N_FOX_HEADS].T
    dqf, dkf, dvf, dcs, drs = _fox_bwd(fq, fk, fv, _to_heads(do_fox, N_FOX_HEADS), c_row, _rep(c_t),
                                       lse_fox[..., 0].reshape(N_FOX_HEADS, 1, S),
                                       delta_fox.reshape(N_FOX_HEADS, 1, S), "fox_bwd")
    dfg_t, db_f = _forget_bwd(fg_t, b_col, drs[:, 0], dcs[..., 0], "forget_bwd")

    delta_dil = _head_rowdot(jnp.tile(do_dil, (1, 3)), o_cat[:, FOX_WIDTH:], N_DIL_HEADS, "dilated_delta")
    delta_dil_t = delta_dil[:, :N_DIL_HEADS].T
    delta_dil_rep = _rep(jnp.stack([_perm(delta_dil_t, d) for d in DILATIONS]))
    do_dil_h = _to_heads(do_dil, N_DIL_HEADS)
    dod = jnp.stack([_perm(do_dil_h, d) for d in DILATIONS])
    dqd, dkd, dvd, dbias = _band_bwd(qd, kd, vd, dod, bias_tab, lse_joint_rep, delta_dil_rep, "dilated_bwd")
    d_rel = _bucket_reduce(dbias, jnp.asarray(bucket_map), "rel_bias_grad")[:, :N_DIL_HEADS]

    def tok(p, d):
        return jnp.concatenate([_from_heads(_unperm(t[p], d)) for t in (dqd, dkd, dvd)], axis=1)

    d_dil = _add_n([tok(p, d) for p, d in enumerate(DILATIONS)], "dilated_grad_sum")
    dqkv = jnp.concatenate([_from_heads(dqf), _from_heads(dkf), _from_heads(dvf), d_dil],
                           axis=1).astype(BF16)
    dfg_pad = jnp.pad(dfg_t.T, ((0, 0), (0, LANES - N_FOX_HEADS))).astype(BF16)
    dw_qkv = _mm(h1, dqkv, "tn", F32, "grad_w_qkv")
    dw_fg = _mm(h1, dfg_pad, "tn", F32, "grad_w_gate_cols")[:, :N_FOX_HEADS]
    grads["w_in"] = jnp.concatenate([dw_qkv[:, :3 * FOX_WIDTH], dw_fg, dw_qkv[:, 3 * FOX_WIDTH:]], axis=1)
    dcat = jnp.concatenate([dqkv, dfg_pad], axis=1)
    w_cat = jnp.concatenate([w_qkv, w_fg_pad], axis=1)
    dh1 = _mm(dcat, w_cat, "nt", F32, "bwd_proj_in", tk=640)
    grad_x, _, dg = _rms_bwd(xs, g_mix_pre, dh1, dx1, "bwd_norm_mix_pre")
    small["g_mix_pre"] = dg[0:1]
    small["b_f"] = db_f[:, 0].reshape(1, N_FOX_HEADS)
    small["rel_bias"] = d_rel

    split = {n: _split_weight(grads[n], n) for n in big}
    parts = jnp.stack([_pack({n: split[n][j] for n in big}) for j in range(N_CHIPS)])
    R = parts.shape[1]
    half = R // 2
    keep = lax.dynamic_slice_in_dim(parts, my_c * half, half, axis=1)
    give = lax.dynamic_slice_in_dim(parts, (1 - my_c) * half, half, axis=1)
    got = _sibling_exchange(give, "grads_to_sibling")
    chip_sum = _add_n([keep.reshape(-1, PACK_COLS), got.reshape(-1, PACK_COLS)], "grads_add_sibling")
    chip_sum = chip_sum.reshape(N_CHIPS, half, PACK_COLS)
    my_chip = 2 * my_x + my_y
    from_chips = _chip_scatter(chip_sum, "grads_to_chips")
    own = lax.dynamic_index_in_dim(chip_sum, my_chip, axis=0, keepdims=False)
    g_half = _add_n([own, from_chips[0], from_chips[1], from_chips[2]], "grads_add_chips")
    other_half = _sibling_exchange(g_half, "grads_share_sibling")
    g_pack = jnp.where(my_c == 0, jnp.concatenate([g_half, other_half]), jnp.concatenate([other_half, g_half]))

    small_pack = _pack_small(small)
    small_pack = small_pack.at[8, 0].set(loss_tile[0, 0])
    everyone = _all_to_all_small(small_pack, "small_all_gather")
    small_sum = _add_n([everyone[i] for i in range(8)], "small_sum")
    loss = small_sum[8, 0]
    g_small = _unpack_small(small_sum)

    m_pack = _pack({n: args["m_" + n][0] for n in big})
    v_pack = _pack({n: args["v_" + n][0] for n in big})
    d_pack, nm_pack, nv_pack = _adamw(w_pack, g_pack, m_pack, v_pack, "adamw_big")
    outs = {"grad": _unpack(g_pack, shard_shapes), "delta": _unpack(d_pack, shard_shapes),
            "new_m": _unpack(nm_pack, shard_shapes), "new_v": _unpack(nv_pack, shard_shapes)}
    sw = _pack_small({n: args[n] for n in _SMALL + ("b_f", "rel_bias")})
    sm = _pack_small({n: args["m_" + n] for n in _SMALL + ("b_f", "rel_bias")})
    sv = _pack_small({n: args["v_" + n] for n in _SMALL + ("b_f", "rel_bias")})
    sd, snm, snv = _adamw(sw, small_sum.at[8, 0].set(0.0), sm, sv, "adamw_small")
    souts = {"grad": g_small, "delta": _unpack_small(sd), "new_m": _unpack_small(snm), "new_v": _unpack_small(snv)}

    def leaf(kind, n):
        if n in souts[kind]:
            return souts[kind][n].reshape(args[n].shape)
        return outs[kind][n].reshape(args[n].shape)

    result = [loss, grad_x.reshape(x.shape)]
    for kind in ("grad", "delta", "new_m", "new_v"):
        result += [leaf(kind, n) for n in names]
    return tuple(result)
```

```python
import numpy as np
import jax
import jax.numpy as jnp
from jax import lax
from jax.experimental import pallas as pl
from jax.experimental.pallas import tpu as pltpu

F32 = jnp.float32
BF16 = jnp.bfloat16
MESH_IDS = pl.DeviceIdType.MESH

LANES = 128
HEAD_DIM = 64
N_FOX_HEADS = 8
N_DIL_HEADS = 8
N_MEM_HEADS = 4
FOX_WIDTH = N_FOX_HEADS * HEAD_DIM
DIL_WIDTH = N_DIL_HEADS * HEAD_DIM
DILATIONS = (1, 4, 16)
BAND = 128
BAND_CHUNK_MAX = 8 * BAND
N_BUCKETS = 32
MAX_DISTANCE = 2048
QK_SCALE = HEAD_DIM ** -0.5
RMS_EPS = 1e-6
NEG = -1e30
VMEM_LIMIT = 56 << 20

ADAM_LR = 0.001
ADAM_B1 = 0.9
ADAM_B2 = 0.999
ADAM_EPS = 1e-08
ADAM_WD = 0.01
ADAM_STEP = 10

N_CHIPS = 4
PACK_COLS = 1024


def _params(*sem):
    return pltpu.CompilerParams(dimension_semantics=sem, vmem_limit_bytes=VMEM_LIMIT)


def _fit(n, cap):
    if n <= cap:
        return n
    t = (cap // LANES) * LANES
    while t >= LANES:
        if n % t == 0:
            return t
        t -= LANES
    raise ValueError(f"no lane-aligned tile for {n} under {cap}")


def _dot(a, b, dims):
    return lax.dot_general(a, b, (dims, ((), ())), preferred_element_type=F32)


_NN = ((1,), (0,))
_NT = ((1,), (1,))
_TN = ((0,), (0,))


def _mm(a, b, mode, out_dtype, name, tm=1024, tn=1024, tk=512):
    if mode == "nn":
        (M, K), N = a.shape, b.shape[1]
    elif mode == "nt":
        (M, K), N = a.shape, b.shape[0]
    else:
        (K, M), N = a.shape, b.shape[1]
    tm, tn, tk = _fit(M, tm), _fit(N, tn), _fit(K, tk)
    nk = K // tk
    if mode == "tn":
        a_spec = pl.BlockSpec((tk, tm), lambda i, j, k: (k, i))
    else:
        a_spec = pl.BlockSpec((tm, tk), lambda i, j, k: (i, k))
    if mode == "nt":
        b_spec = pl.BlockSpec((tn, tk), lambda i, j, k: (j, k))
    else:
        b_spec = pl.BlockSpec((tk, tn), lambda i, j, k: (k, j))
    dims = {"nn": _NN, "nt": _NT, "tn": _TN}[mode]

    def body(a_ref, b_ref, o_ref, acc_ref):
        k = pl.program_id(2)

        @pl.when(k == 0)
        def _():
            acc_ref[...] = jnp.zeros_like(acc_ref)

        acc_ref[...] += _dot(a_ref[...].astype(BF16), b_ref[...].astype(BF16), dims)

        @pl.when(k == nk - 1)
        def _():
            o_ref[...] = acc_ref[...].astype(o_ref.dtype)

    return pl.pallas_call(
        body, name=name, grid=(M // tm, N // tn, nk),
        in_specs=[a_spec, b_spec],
        out_specs=pl.BlockSpec((tm, tn), lambda i, j, k: (i, j)),
        out_shape=jax.ShapeDtypeStruct((M, N), out_dtype),
        scratch_shapes=[pltpu.VMEM((tm, tn), F32)],
        compiler_params=_params("parallel", "parallel", "arbitrary"),
    )(a, b)


def _rms_rows(x):
    return lax.rsqrt(jnp.mean(x * x, axis=-1, keepdims=True) + RMS_EPS)


def _rms_fwd(x, g, name, tr=512):
    S, D = x.shape
    tr = _fit(S, tr)

    def body(x_ref, g_ref, h_ref):
        xv = x_ref[...]
        h_ref[...] = (xv * _rms_rows(xv) * g_ref[...]).astype(BF16)

    return pl.pallas_call(
        body, name=name, grid=(S // tr,),
        in_specs=[pl.BlockSpec((tr, D), lambda i: (i, 0)), pl.BlockSpec((1, D), lambda i: (0, 0))],
        out_specs=pl.BlockSpec((tr, D), lambda i: (i, 0)),
        out_shape=jax.ShapeDtypeStruct((S, D), BF16),
        compiler_params=_params("parallel"),
    )(x, g)


def _resid_norm(xres, y, g_post, g_next, name, tr=512):
    S, D = xres.shape
    tr = _fit(S, tr)

    def body(x_ref, y_ref, gp_ref, gn_ref, xn_ref, h_ref):
        yv = y_ref[...]
        xn = x_ref[...] + yv * _rms_rows(yv) * gp_ref[...]
        xn_ref[...] = xn
        h_ref[...] = (xn * _rms_rows(xn) * gn_ref[...]).astype(BF16)

    row = pl.BlockSpec((tr, D), lambda i: (i, 0))
    vec = pl.BlockSpec((1, D), lambda i: (0, 0))
    return pl.pallas_call(
        body, name=name, grid=(S // tr,),
        in_specs=[row, row, vec, vec], out_specs=[row, row],
        out_shape=[jax.ShapeDtypeStruct((S, D), F32), jax.ShapeDtypeStruct((S, D), BF16)],
        compiler_params=_params("parallel"),
    )(xres, y, g_post, g_next)


def _final_loss(xres, y, g_post, target, name, tr=512):
    S, D = xres.shape
    tr = _fit(S, tr)

    def body(x_ref, y_ref, gp_ref, t_ref, d_ref, loss_ref):
        i = pl.program_id(0)
        yv = y_ref[...]
        err = x_ref[...] + yv * _rms_rows(yv) * gp_ref[...] - t_ref[...]
        d_ref[...] = err * (1.0 / D)

        @pl.when(i == 0)
        def _():
            loss_ref[...] = jnp.zeros_like(loss_ref)

        part = jnp.sum(jnp.sum(err * err, axis=1, keepdims=True), axis=0, keepdims=True)
        loss_ref[...] += jnp.broadcast_to(part * (0.5 / D), loss_ref.shape)

    row = pl.BlockSpec((tr, D), lambda i: (i, 0))
    vec = pl.BlockSpec((1, D), lambda i: (0, 0))
    return pl.pallas_call(
        body, name=name, grid=(S // tr,),
        in_specs=[row, row, vec, row],
        out_specs=[row, pl.BlockSpec((8, LANES), lambda i: (0, 0))],
        out_shape=[jax.ShapeDtypeStruct((S, D), F32), jax.ShapeDtypeStruct((8, LANES), F32)],
        compiler_params=_params("arbitrary"),
    )(xres, y, g_post, target)


def _rms_bwd(xin, g, dy, dres, name, tr=512):
    S, D = xin.shape
    tr = _fit(S, tr)
    has_res = dres is not None

    def body(*refs):
        if has_res:
            x_ref, g_ref, dy_ref, dr_ref, dx_ref, dxb_ref, dg_ref = refs
        else:
            x_ref, g_ref, dy_ref, dx_ref, dxb_ref, dg_ref = refs
        i = pl.program_id(0)
        xv = x_ref[...]
        dyv = dy_ref[...].astype(F32)
        xhat = xv * _rms_rows(xv)
        dxhat = dyv * g_ref[...]
        r = _rms_rows(xv)
        dx = r * (dxhat - xhat * jnp.mean(dxhat * xhat, axis=-1, keepdims=True))
        if has_res:
            dx = dx + dr_ref[...]
        dx_ref[...] = dx
        dxb_ref[...] = dx.astype(BF16)

        @pl.when(i == 0)
        def _():
            dg_ref[...] = jnp.zeros_like(dg_ref)

        dg_ref[...] += jnp.broadcast_to(jnp.sum(dyv * xhat, axis=0, keepdims=True), dg_ref.shape)

    row = pl.BlockSpec((tr, D), lambda i: (i, 0))
    vec = pl.BlockSpec((1, D), lambda i: (0, 0))
    acc = pl.BlockSpec((8, D), lambda i: (0, 0))
    ins = [xin, g, dy] + ([dres] if has_res else [])
    return pl.pallas_call(
        body, name=name, grid=(S // tr,),
        in_specs=[row, vec, row] + ([row] if has_res else []),
        out_specs=[row, row, acc],
        out_shape=[jax.ShapeDtypeStruct((S, D), F32), jax.ShapeDtypeStruct((S, D), BF16),
                   jax.ShapeDtypeStruct((8, D), F32)],
        compiler_params=_params("arbitrary"),
    )(*ins)


def _tri(n, upper):
    r = lax.broadcasted_iota(jnp.int32, (n, n), 0)
    c = lax.broadcasted_iota(jnp.int32, (n, n), 1)
    return jnp.where((r <= c) if upper else (r >= c), 1.0, 0.0).astype(F32)


def _forget_fwd(fg_t, b_col, name, ts=512):
    H, S = fg_t.shape
    ts = _fit(S, ts)

    def body(f_ref, b_ref, c_ref, carry_ref):
        i = pl.program_id(0)

        @pl.when(i == 0)
        def _():
            carry_ref[...] = jnp.zeros_like(carry_ref)

        z = f_ref[...] + b_ref[...]
        logf = jnp.minimum(z, 0.0) - jnp.log(1.0 + jnp.exp(-jnp.abs(z)))
        run = lax.dot_general(logf, _tri(ts, True), (_NN, ((), ())), precision=lax.Precision.HIGHEST,
                              preferred_element_type=F32) + carry_ref[:, 0:1]
        c_ref[...] = run
        carry_ref[...] = jnp.broadcast_to(
            carry_ref[:, 0:1] + jnp.sum(logf, axis=1, keepdims=True), carry_ref.shape)

    return pl.pallas_call(
        body, name=name, grid=(S // ts,),
        in_specs=[pl.BlockSpec((H, ts), lambda i: (0, i)), pl.BlockSpec((H, 1), lambda i: (0, 0))],
        out_specs=pl.BlockSpec((H, ts), lambda i: (0, i)),
        out_shape=jax.ShapeDtypeStruct((H, S), F32),
        scratch_shapes=[pltpu.VMEM((H, LANES), F32)],
        compiler_params=_params("arbitrary"),
    )(fg_t, b_col)


def _forget_bwd(fg_t, b_col, dc_plus, dc_minus, name, ts=512):
    H, S = fg_t.shape
    ts = _fit(S, ts)
    nb = S // ts

    def body(f_ref, b_ref, dcp_ref, dcm_ref, df_ref, db_ref, carry_ref):
        i = pl.program_id(0)

        @pl.when(i == 0)
        def _():
            carry_ref[...] = jnp.zeros_like(carry_ref)
            db_ref[...] = jnp.zeros_like(db_ref)

        dc = dcp_ref[...] - dcm_ref[...]
        suffix = lax.dot_general(dc, _tri(ts, False), (_NN, ((), ())), precision=lax.Precision.HIGHEST,
                                 preferred_element_type=F32) + carry_ref[:, 0:1]
        z = f_ref[...] + b_ref[...]
        sig_neg = 1.0 / (1.0 + jnp.exp(z))
        df = suffix * sig_neg
        df_ref[...] = df
        carry_ref[...] = jnp.broadcast_to(
            carry_ref[:, 0:1] + jnp.sum(dc, axis=1, keepdims=True), carry_ref.shape)
        db_ref[...] += jnp.broadcast_to(jnp.sum(df, axis=1, keepdims=True), db_ref.shape)

    rev = pl.BlockSpec((H, ts), lambda i: (0, nb - 1 - i))
    return pl.pallas_call(
        body, name=name, grid=(nb,),
        in_specs=[rev, pl.BlockSpec((H, 1), lambda i: (0, 0)), rev, rev],
        out_specs=[rev, pl.BlockSpec((H, LANES), lambda i: (0, 0))],
        out_shape=[jax.ShapeDtypeStruct((H, S), F32), jax.ShapeDtypeStruct((H, LANES), F32)],
        scratch_shapes=[pltpu.VMEM((H, LANES), F32)],
        compiler_params=_params("arbitrary"),
    )(fg_t, b_col, dc_plus, dc_minus)


def _tile_lanes(x, n):
    return x if n == LANES else jnp.tile(x, (1, n // LANES))


def _fox_fwd(qt, k, vt, c_row, c_rep, name, tq=512, tk=1024):
    H, Dh, S = qt.shape
    tk = _fit(S, tk)
    tq = _fit(tk, tq)
    ratio = tk // tq

    def body(qt_ref, k_ref, vt_ref, c_ref, crep_ref, o_ref, lse_ref, m_ref, l_ref, acc_ref):
        i = pl.program_id(1)
        qv = qt_ref[...] * QK_SCALE
        cq0 = c_ref[:, pl.ds(pl.multiple_of(i * tq, LANES), LANES)][:, 0:1]
        m_ref[...] = jnp.full_like(m_ref, NEG)
        l_ref[...] = jnp.zeros_like(l_ref)
        acc_ref[...] = jnp.zeros_like(acc_ref)
        j_diag = i // ratio
        q_off = (i - j_diag * ratio) * tq

        def step(j, diagonal):
            off = pl.multiple_of(j * tk, LANES)
            s = _dot(k_ref[pl.ds(off, tk), :], qv, _NN) + _tile_lanes(cq0 - crep_ref[pl.ds(off, tk), :], tq)
            if diagonal:
                key = lax.broadcasted_iota(jnp.int32, (tk, tq), 0)
                qry = lax.broadcasted_iota(jnp.int32, (tk, tq), 1) + q_off
                s = jnp.where(key <= qry, s, NEG)
            m_old = m_ref[...]
            m_new = jnp.maximum(m_old, jnp.max(s, axis=0, keepdims=True))
            p = jnp.exp(s - m_new)
            alpha = jnp.exp(m_old - m_new)
            l_ref[...] = alpha * l_ref[...] + jnp.sum(p, axis=0, keepdims=True)
            acc_ref[...] = alpha * acc_ref[...] + _dot(vt_ref[:, pl.ds(off, tk)], p.astype(BF16), _NN)
            m_ref[...] = m_new

        def loop_body(j, carry):
            step(j, False)
            return carry

        lax.fori_loop(0, j_diag, loop_body, 0)
        step(j_diag, True)
        o_ref[...] = acc_ref[...] / l_ref[...]
        lse_ref[...] = m_ref[...] + jnp.log(l_ref[...]) - cq0

    lanes_full = pl.BlockSpec((None, Dh, S), lambda h, i: (h, 0, 0))
    lanes_tile = pl.BlockSpec((None, Dh, tq), lambda h, i: (h, 0, i))
    return pl.pallas_call(
        body, name=name, grid=(H, S // tq),
        in_specs=[lanes_tile, pl.BlockSpec((None, S, Dh), lambda h, i: (h, 0, 0)), lanes_full,
                  pl.BlockSpec((None, 1, S), lambda h, i: (h, 0, 0)),
                  pl.BlockSpec((None, S, LANES), lambda h, i: (h, 0, 0))],
        out_specs=[lanes_tile, pl.BlockSpec((None, 1, tq), lambda h, i: (h, 0, i))],
        out_shape=[jax.ShapeDtypeStruct((H, Dh, S), F32), jax.ShapeDtypeStruct((H, 1, S), F32)],
        scratch_shapes=[pltpu.VMEM((1, tq), F32), pltpu.VMEM((1, tq), F32), pltpu.VMEM((Dh, tq), F32)],
        compiler_params=_params("parallel", "arbitrary"),
    )(qt, k, vt, c_row, c_rep)


def _fox_bwd(qt, k, kt, v, dot, c_rep, lse_row, delta_row, name, tq=1024, tk=512):
    H, Dh, S = qt.shape
    tq = _fit(S, tq)
    tk = _fit(tq, tk)
    ratio = tq // tk
    nq = S // tq

    def body(k_ref, kt_ref, v_ref, crep_ref, qt_ref, dot_ref, lse_ref, dl_ref,
             dqt_ref, dkt_ref, dvt_ref, dcs_ref, drs_ref, dka_ref, dva_ref, dca_ref):
        j = pl.program_id(1)

        @pl.when(j == 0)
        def _():
            dqt_ref[...] = jnp.zeros_like(dqt_ref)
            drs_ref[...] = jnp.zeros_like(drs_ref)

        kv = k_ref[...]
        ktv = kt_ref[...]
        vv = v_ref[...]
        c_col = _tile_lanes(crep_ref[...], tq)
        dka_ref[...] = jnp.zeros_like(dka_ref)
        dva_ref[...] = jnp.zeros_like(dva_ref)
        dca_ref[...] = jnp.zeros_like(dca_ref)
        i_diag = j // ratio
        k_off = (j - i_diag * ratio) * tk

        def step(i, diagonal):
            off = pl.multiple_of(i * tq, LANES)
            qv = qt_ref[:, pl.ds(off, tq)] * QK_SCALE
            dov = dot_ref[:, pl.ds(off, tq)]
            e = _dot(kv, qv, _NN) - lse_ref[:, pl.ds(off, tq)] - c_col
            if diagonal:
                key = lax.broadcasted_iota(jnp.int32, (tk, tq), 0) + k_off
                qry = lax.broadcasted_iota(jnp.int32, (tk, tq), 1)
                e = jnp.where(key <= qry, e, NEG)
            p_t = jnp.exp(e)
            dva_ref[...] += _dot(dov, p_t.astype(BF16), _NT)
            ds_t = p_t * (_dot(vv, dov, _NN) - dl_ref[:, pl.ds(off, tq)])
            ds_b = ds_t.astype(BF16)
            dka_ref[...] += _dot(qv, ds_b, _NT)
            dqt_ref[:, pl.ds(off, tq)] += _dot(ktv, ds_b, _NN) * QK_SCALE
            drs_ref[:, pl.ds(off, tq)] += jnp.sum(ds_t, axis=0, keepdims=True)
            part = ds_t[:, 0:LANES]
            for cidx in range(1, tq // LANES):
                part = part + ds_t[:, cidx * LANES:(cidx + 1) * LANES]
            dca_ref[...] += part

        step(i_diag, True)

        def loop_body(i, carry):
            step(i, False)
            return carry

        lax.fori_loop(i_diag + 1, nq, loop_body, 0)
        dkt_ref[...] = dka_ref[...]
        dvt_ref[...] = dva_ref[...]
        dcs_ref[...] = jnp.broadcast_to(jnp.sum(dca_ref[...], axis=1, keepdims=True), dcs_ref.shape)

    rows_tile = pl.BlockSpec((None, tk, Dh), lambda h, j: (h, j, 0))
    lanes_tile = pl.BlockSpec((None, Dh, tk), lambda h, j: (h, 0, j))
    rep = pl.BlockSpec((None, tk, LANES), lambda h, j: (h, j, 0))
    lanes_full = pl.BlockSpec((None, Dh, S), lambda h, j: (h, 0, 0))
    rowv = pl.BlockSpec((None, 1, S), lambda h, j: (h, 0, 0))
    return pl.pallas_call(
        body, name=name, grid=(H, S // tk),
        in_specs=[rows_tile, lanes_tile, rows_tile, rep, lanes_full, lanes_full, rowv, rowv],
        out_specs=[lanes_full, lanes_tile, lanes_tile, rep, rowv],
        out_shape=[jax.ShapeDtypeStruct((H, Dh, S), F32), jax.ShapeDtypeStruct((H, Dh, S), F32),
                   jax.ShapeDtypeStruct((H, Dh, S), F32), jax.ShapeDtypeStruct((H, S, LANES), F32),
                   jax.ShapeDtypeStruct((H, 1, S), F32)],
        scratch_shapes=[pltpu.VMEM((Dh, tk), F32), pltpu.VMEM((Dh, tk), F32), pltpu.VMEM((tk, LANES), F32)],
        compiler_params=_params("parallel", "arbitrary"),
    )(k, kt, v, c_rep, qt, dot, lse_row, delta_row)


def _band_chunk(S):
    chunk = min(BAND_CHUNK_MAX, S // DILATIONS[-1])
    assert chunk % BAND == 0 and S % (chunk * DILATIONS[-1]) == 0
    return chunk


def _band_specs(S):
    chunk_rows = _band_chunk(S)
    chunk = pl.BlockSpec((None, None, chunk_rows, HEAD_DIM), lambda p, h, i: (p, h, i, 0))
    prev = pl.BlockSpec((None, None, BAND, HEAD_DIM),
                        lambda p, h, i: (p, h, jnp.maximum(i * (chunk_rows // BAND) - 1, 0), 0))
    bias = pl.BlockSpec((None, None, BAND, 2 * BAND), lambda p, h, i: (p, h, 0, 0))
    rep = pl.BlockSpec((None, None, chunk_rows, LANES), lambda p, h, i: (p, h, i, 0))
    return chunk, prev, bias, rep


def _band_first_scale(S, p, i):
    per_class = jnp.right_shift(jnp.int32(S // _band_chunk(S)), 2 * p)
    return (i % per_class) == 0


def _band_scores(qb, kb, bias, first_block):
    s = _dot(qb, kb, _NT) + bias
    if first_block is not None:
        col = lax.broadcasted_iota(jnp.int32, s.shape, 1)
        s = jnp.where(jnp.logical_and(first_block, col < BAND), NEG, s)
    return s


def _band_lse(q, k, bias, name):
    P, H, S, Dh = q.shape
    BAND_CHUNK = _band_chunk(S)
    nb = BAND_CHUNK // BAND
    chunk, prev, bias_spec, rep = _band_specs(S)

    def body(q_ref, k_ref, kp_ref, b_ref, lse_ref, kext_ref):
        p, i = pl.program_id(0), pl.program_id(2)
        first = _band_first_scale(S, p, i)
        kext_ref[0:BAND, :] = kp_ref[...]
        kext_ref[BAND:, :] = k_ref[...]
        bias_v = b_ref[...]
        for b in range(nb):
            qb = q_ref[b * BAND:(b + 1) * BAND, :] * QK_SCALE
            s = _band_scores(qb, kext_ref[b * BAND:(b + 2) * BAND, :], bias_v, first if b == 0 else None)
            m = jnp.max(s, axis=1, keepdims=True)
            lse = m + jnp.log(jnp.sum(jnp.exp(s - m), axis=1, keepdims=True))
            lse_ref[b * BAND:(b + 1) * BAND, :] = jnp.broadcast_to(lse, (BAND, LANES))

    return pl.pallas_call(
        body, name=name, grid=(P, H, S // BAND_CHUNK),
        in_specs=[chunk, chunk, prev, bias_spec], out_specs=rep,
        out_shape=jax.ShapeDtypeStruct((P, H, S, LANES), F32),
        scratch_shapes=[pltpu.VMEM((BAND_CHUNK + BAND, Dh), BF16)],
        compiler_params=_params("parallel", "parallel", "parallel"),
    )(q, k, k, bias)


def _band_out(q, k, v, bias, lse_rep, name):
    P, H, S, Dh = q.shape
    BAND_CHUNK = _band_chunk(S)
    nb = BAND_CHUNK // BAND
    chunk, prev, bias_spec, rep = _band_specs(S)

    def body(q_ref, k_ref, kp_ref, v_ref, vp_ref, b_ref, lse_ref, o_ref, kext_ref, vext_ref):
        p, i = pl.program_id(0), pl.program_id(2)
        first = _band_first_scale(S, p, i)
        kext_ref[0:BAND, :] = kp_ref[...]
        kext_ref[BAND:, :] = k_ref[...]
        vext_ref[0:BAND, :] = vp_ref[...]
        vext_ref[BAND:, :] = v_ref[...]
        bias_v = b_ref[...]
        for b in range(nb):
            rows = slice(b * BAND, (b + 1) * BAND)
            ext = slice(b * BAND, (b + 2) * BAND)
            s = _band_scores(q_ref[rows, :] * QK_SCALE, kext_ref[ext, :], bias_v, first if b == 0 else None)
            pr = jnp.exp(s - _tile_lanes(lse_ref[rows, :], 2 * BAND))
            o_ref[rows, :] = _dot(pr.astype(BF16), vext_ref[ext, :], _NN)

    return pl.pallas_call(
        body, name=name, grid=(P, H, S // BAND_CHUNK),
        in_specs=[chunk, chunk, prev, chunk, prev, bias_spec, rep], out_specs=chunk,
        out_shape=jax.ShapeDtypeStruct((P, H, S, Dh), F32),
        scratch_shapes=[pltpu.VMEM((BAND_CHUNK + BAND, Dh), BF16), pltpu.VMEM((BAND_CHUNK + BAND, Dh), BF16)],
        compiler_params=_params("parallel", "parallel", "parallel"),
    )(q, k, k, v, v, bias, lse_rep)


def _band_bwd(q, k, v, do, bias, lse_rep, delta_rep, name):
    P, H, S, Dh = q.shape
    BAND_CHUNK = _band_chunk(S)
    nb = BAND_CHUNK // BAND
    nchunks = S // BAND_CHUNK
    chunk, prev, bias_spec, rep = _band_specs(S)
    nxt_idx = lambda p, h, i: (p, h, jnp.minimum((i + 1) * nb, S // BAND - 1), 0)
    nxt = pl.BlockSpec((None, None, BAND, Dh), nxt_idx)
    nxt_rep = pl.BlockSpec((None, None, BAND, LANES), nxt_idx)

    def body(q_ref, k_ref, kp_ref, v_ref, vp_ref, do_ref, b_ref, lse_ref, dl_ref,
             qn_ref, don_ref, lsen_ref, dln_ref,
             dq_ref, dk_ref, dv_ref, db_ref, kext_ref, vext_ref, dkext_ref, dvext_ref):
        p, i = pl.program_id(0), pl.program_id(2)
        first = _band_first_scale(S, p, i)
        next_first = jnp.logical_or(_band_first_scale(S, p, i + 1), i + 1 >= nchunks)
        tail = slice(BAND + BAND_CHUNK, 2 * BAND + BAND_CHUNK)
        kext_ref[0:BAND, :] = kp_ref[...]
        kext_ref[BAND:BAND + BAND_CHUNK, :] = k_ref[...]
        kext_ref[tail, :] = jnp.zeros((BAND, Dh), BF16)
        vext_ref[0:BAND, :] = vp_ref[...]
        vext_ref[BAND:BAND + BAND_CHUNK, :] = v_ref[...]
        vext_ref[tail, :] = jnp.zeros((BAND, Dh), BF16)
        dkext_ref[...] = jnp.zeros_like(dkext_ref)
        dvext_ref[...] = jnp.zeros_like(dvext_ref)
        bias_v = b_ref[...]

        @pl.when(i == 0)
        def _():
            db_ref[...] = jnp.zeros_like(db_ref)

        db_acc = jnp.zeros((BAND, 2 * BAND), F32)
        for b in range(nb):
            rows = slice(b * BAND, (b + 1) * BAND)
            ext = slice(b * BAND, (b + 2) * BAND)
            qb = q_ref[rows, :] * QK_SCALE
            dob = do_ref[rows, :]
            s = _band_scores(qb, kext_ref[ext, :], bias_v, first if b == 0 else None)
            pr = jnp.exp(s - _tile_lanes(lse_ref[rows, :], 2 * BAND))
            ds = pr * (_dot(dob, vext_ref[ext, :], _NT) - _tile_lanes(dl_ref[rows, :], 2 * BAND))
            ds_b = ds.astype(BF16)
            dq_ref[rows, :] = _dot(ds_b, kext_ref[ext, :], _NN) * QK_SCALE
            dkext_ref[ext, :] += _dot(ds_b, qb, _TN)
            dvext_ref[ext, :] += _dot(pr.astype(BF16), dob, _TN)
            db_acc = db_acc + ds
        db_ref[...] += db_acc

        ext = slice(BAND_CHUNK, BAND_CHUNK + 2 * BAND)
        qn = qn_ref[...] * QK_SCALE
        don = don_ref[...]
        sn = _dot(qn, kext_ref[ext, :], _NT) + bias_v
        col = lax.broadcasted_iota(jnp.int32, sn.shape, 1)
        sn = jnp.where(jnp.logical_or(next_first, col >= BAND), NEG, sn)
        prn = jnp.exp(sn - _tile_lanes(lsen_ref[...], 2 * BAND))
        dsn = prn * (_dot(don, vext_ref[ext, :], _NT) - _tile_lanes(dln_ref[...], 2 * BAND))
        dkext_ref[ext, :] += _dot(dsn.astype(BF16), qn, _TN)
        dvext_ref[ext, :] += _dot(prn.astype(BF16), don, _TN)
        dk_ref[...] = dkext_ref[BAND:BAND + BAND_CHUNK, :]
        dv_ref[...] = dvext_ref[BAND:BAND + BAND_CHUNK, :]

    return pl.pallas_call(
        body, name=name, grid=(P, H, nchunks),
        in_specs=[chunk, chunk, prev, chunk, prev, chunk, bias_spec, rep, rep, nxt, nxt, nxt_rep, nxt_rep],
        out_specs=[chunk, chunk, chunk, bias_spec],
        out_shape=[jax.ShapeDtypeStruct((P, H, S, Dh), F32)] * 3
                  + [jax.ShapeDtypeStruct((P, H, BAND, 2 * BAND), F32)],
        scratch_shapes=[pltpu.VMEM((BAND_CHUNK + 2 * BAND, Dh), BF16), pltpu.VMEM((BAND_CHUNK + 2 * BAND, Dh), BF16),
                        pltpu.VMEM((BAND_CHUNK + 2 * BAND, Dh), F32), pltpu.VMEM((BAND_CHUNK + 2 * BAND, Dh), F32)],
        compiler_params=_params("parallel", "parallel", "arbitrary"),
    )(q, k, k, v, v, do, bias, lse_rep, delta_rep, q, do, lse_rep, delta_rep)


def _lse_join(lse3, name):
    P, H, S = lse3.shape

    def body(l_ref, o_ref):
        a, b, c = l_ref[0], l_ref[1], l_ref[2]
        m = jnp.maximum(jnp.maximum(a, b), c)
        o_ref[...] = m + jnp.log(jnp.exp(a - m) + jnp.exp(b - m) + jnp.exp(c - m))

    return pl.pallas_call(body, name=name, out_shape=jax.ShapeDtypeStruct((H, S), F32))(lse3)


def _bucket_reduce(dbias, bucket_map, name):
    P, H = dbias.shape[:2]

    def body(db_ref, bk_ref, o_ref):
        p, h = pl.program_id(0), pl.program_id(1)

        @pl.when(jnp.logical_and(p == 0, h == 0))
        def _():
            o_ref[...] = jnp.zeros_like(o_ref)

        db, bk = db_ref[...], bk_ref[...]
        row = lax.broadcasted_iota(jnp.int32, (N_BUCKETS, LANES), 0)
        lane = lax.broadcasted_iota(jnp.int32, (N_BUCKETS, LANES), 1)

        def one(b, acc):
            val = jnp.sum(jnp.sum(jnp.where(bk == b, db, 0.0), axis=1, keepdims=True), axis=0, keepdims=True)
            return acc + jnp.where(jnp.logical_and(row == b, lane == h), val, 0.0)

        o_ref[...] += lax.fori_loop(0, N_BUCKETS, one, jnp.zeros((N_BUCKETS, LANES), F32))

    return pl.pallas_call(
        body, name=name, grid=(P, H),
        in_specs=[pl.BlockSpec((None, None, BAND, 2 * BAND), lambda p, h: (p, h, 0, 0)),
                  pl.BlockSpec((None, BAND, 2 * BAND), lambda p, h: (p, 0, 0))],
        out_specs=pl.BlockSpec((N_BUCKETS, LANES), lambda p, h: (0, 0)),
        out_shape=jax.ShapeDtypeStruct((N_BUCKETS, LANES), F32),
        compiler_params=_params("arbitrary", "arbitrary"),
    )(dbias, bucket_map)


def _mem_fwd(q, k, v, name, tq=1024):
    H, S, Dh = q.shape
    N = k.shape[1]
    tq = _fit(S, tq)

    def body(q_ref, k_ref, v_ref, o_ref, lse_ref):
        s = _dot(q_ref[...] * QK_SCALE, k_ref[...], _NT)
        m = jnp.max(s, axis=1, keepdims=True)
        e = jnp.exp(s - m)
        l = jnp.sum(e, axis=1, keepdims=True)
        o_ref[...] = _dot((e / l).astype(BF16), v_ref[...], _NN)
        lse_ref[...] = jnp.broadcast_to(m + jnp.log(l), lse_ref.shape)

    qs = pl.BlockSpec((None, tq, Dh), lambda h, i: (h, i, 0))
    ks = pl.BlockSpec((None, N, Dh), lambda h, i: (h, 0, 0))
    return pl.pallas_call(
        body, name=name, grid=(H, S // tq),
        in_specs=[qs, ks, ks],
        out_specs=[qs, pl.BlockSpec((None, tq, LANES), lambda h, i: (h, i, 0))],
        out_shape=[jax.ShapeDtypeStruct((H, S, Dh), F32), jax.ShapeDtypeStruct((H, S, LANES), F32)],
        compiler_params=_params("parallel", "parallel"),
    )(q, k, v)


def _mem_bwd(q, k, v, do, lse_rep, delta_rep, name, tq=1024):
    H, S, Dh = q.shape
    N = k.shape[1]
    tq = _fit(S, tq)

    def body(q_ref, k_ref, v_ref, do_ref, lse_ref, dl_ref, dq_ref, dk_ref, dv_ref):
        i = pl.program_id(1)

        @pl.when(i == 0)
        def _():
            dk_ref[...] = jnp.zeros_like(dk_ref)
            dv_ref[...] = jnp.zeros_like(dv_ref)

        qv = q_ref[...] * QK_SCALE
        dov = do_ref[...]
        pr = jnp.exp(_dot(qv, k_ref[...], _NT) - _tile_lanes(lse_ref[...], N))
        ds = pr * (_dot(dov, v_ref[...], _NT) - _tile_lanes(dl_ref[...], N))
        ds_b = ds.astype(BF16)
        dq_ref[...] = _dot(ds_b, k_ref[...], _NN) * QK_SCALE
        dk_ref[...] += _dot(ds_b, qv, _TN)
        dv_ref[...] += _dot(pr.astype(BF16), dov, _TN)

    qs = pl.BlockSpec((None, tq, Dh), lambda h, i: (h, i, 0))
    ks = pl.BlockSpec((None, N, Dh), lambda h, i: (h, 0, 0))
    rep = pl.BlockSpec((None, tq, LANES), lambda h, i: (h, i, 0))
    return pl.pallas_call(
        body, name=name, grid=(H, S // tq),
        in_specs=[qs, ks, ks, qs, rep, rep], out_specs=[qs, ks, ks],
        out_shape=[jax.ShapeDtypeStruct((H, S, Dh), F32), jax.ShapeDtypeStruct((H, N, Dh), F32),
                   jax.ShapeDtypeStruct((H, N, Dh), F32)],
        compiler_params=_params("parallel", "arbitrary"),
    )(q, k, v, do, lse_rep, delta_rep)


def _head_rowdot(a, b, n_heads, name, tr=512):
    S, W = a.shape
    tr = _fit(S, tr)

    def body(a_ref, b_ref, o_ref):
        prod = a_ref[...].astype(F32) * b_ref[...].astype(F32)
        col = lax.broadcasted_iota(jnp.int32, (W, LANES), 0)
        lane = lax.broadcasted_iota(jnp.int32, (W, LANES), 1)
        sel = jnp.where((col // HEAD_DIM) % n_heads == lane, 1.0, 0.0).astype(F32)
        o_ref[...] = lax.dot_general(prod, sel, (_NN, ((), ())), precision=lax.Precision.HIGHEST,
                                     preferred_element_type=F32)

    row = pl.BlockSpec((tr, W), lambda i: (i, 0))
    return pl.pallas_call(
        body, name=name, grid=(S // tr,), in_specs=[row, row],
        out_specs=pl.BlockSpec((tr, LANES), lambda i: (i, 0)),
        out_shape=jax.ShapeDtypeStruct((S, LANES), F32),
        compiler_params=_params("parallel"),
    )(a, b)


FF_TILE = 256


def _ffn_up(h, w_gu, name, tm=512):
    S, D = h.shape
    F2 = w_gu.shape[1]
    tm = _fit(S, tm)

    def body(h_ref, w_ref, gu_ref, act_ref):
        gu = _dot(h_ref[...], w_ref[...], _NN)
        gu_ref[...] = gu
        g, u = gu[:, :FF_TILE], gu[:, FF_TILE:]
        act_ref[...] = (g * (1.0 / (1.0 + jnp.exp(-g))) * u).astype(BF16)

    return pl.pallas_call(
        body, name=name, grid=(S // tm, F2 // (2 * FF_TILE)),
        in_specs=[pl.BlockSpec((tm, D), lambda i, j: (i, 0)), pl.BlockSpec((D, 2 * FF_TILE), lambda i, j: (0, j))],
        out_specs=[pl.BlockSpec((tm, 2 * FF_TILE), lambda i, j: (i, j)),
                   pl.BlockSpec((tm, FF_TILE), lambda i, j: (i, j))],
        out_shape=[jax.ShapeDtypeStruct((S, F2), F32), jax.ShapeDtypeStruct((S, F2 // 2), BF16)],
        compiler_params=_params("parallel", "arbitrary"),
    )(h, w_gu)


def _ffn_dact(dy, w_down, gu, name, tm=512):
    S, D = dy.shape
    F2 = gu.shape[1]
    tm = _fit(S, tm)

    def body(dy_ref, w_ref, gu_ref, dgu_ref):
        dact = _dot(dy_ref[...], w_ref[...], _NT)
        gu_v = gu_ref[...]
        g, u = gu_v[:, :FF_TILE], gu_v[:, FF_TILE:]
        sig = 1.0 / (1.0 + jnp.exp(-g))
        silu = g * sig
        dgu_ref[:, :FF_TILE] = (dact * u * (sig + silu * (1.0 - sig))).astype(BF16)
        dgu_ref[:, FF_TILE:] = (dact * silu).astype(BF16)

    return pl.pallas_call(
        body, name=name, grid=(S // tm, F2 // (2 * FF_TILE)),
        in_specs=[pl.BlockSpec((tm, D), lambda i, j: (i, 0)), pl.BlockSpec((FF_TILE, D), lambda i, j: (j, 0)),
                  pl.BlockSpec((tm, 2 * FF_TILE), lambda i, j: (i, j))],
        out_specs=pl.BlockSpec((tm, 2 * FF_TILE), lambda i, j: (i, j)),
        out_shape=jax.ShapeDtypeStruct((S, F2), BF16),
        compiler_params=_params("parallel", "arbitrary"),
    )(dy, w_down, gu)


def _add_n(arrs, name, tr=512):
    R, C = arrs[0].shape
    tr = _fit_rows(R, tr)

    def body(*refs):
        acc = refs[0][...]
        for r in refs[1:-1]:
            acc = acc + r[...]
        refs[-1][...] = acc

    row = pl.BlockSpec((tr, C), lambda i: (i, 0))
    return pl.pallas_call(
        body, name=name, grid=(R // tr,), in_specs=[row] * len(arrs), out_specs=row,
        out_shape=jax.ShapeDtypeStruct((R, C), F32), compiler_params=_params("parallel"),
    )(*arrs)


def _fit_rows(n, cap):
    if n <= cap:
        return n
    t = (cap // 8) * 8
    while t >= 8:
        if n % t == 0:
            return t
        t -= 8
    raise ValueError(f"no sublane-aligned tile for {n} under {cap}")


def _adamw(w, g, m, v, name, tr=512):
    R, C = w.shape
    tr = _fit_rows(R, tr)
    c1 = 1.0 / (1.0 - ADAM_B1 ** ADAM_STEP)
    c2 = 1.0 / (1.0 - ADAM_B2 ** ADAM_STEP)

    def body(w_ref, g_ref, m_ref, v_ref, d_ref, nm_ref, nv_ref):
        gv = g_ref[...]
        nm = ADAM_B1 * m_ref[...] + (1.0 - ADAM_B1) * gv
        nv = ADAM_B2 * v_ref[...] + (1.0 - ADAM_B2) * (gv * gv)
        nm_ref[...] = nm
        nv_ref[...] = nv
        d_ref[...] = -ADAM_LR * ((nm * c1) / (jnp.sqrt(nv * c2) + ADAM_EPS) + ADAM_WD * w_ref[...])

    row = pl.BlockSpec((tr, C), lambda i: (i, 0))
    return pl.pallas_call(
        body, name=name, grid=(R // tr,), in_specs=[row] * 4, out_specs=[row] * 3,
        out_shape=[jax.ShapeDtypeStruct((R, C), F32)] * 3, compiler_params=_params("parallel"),
    )(w, g, m, v)


def _place():
    return lax.axis_index("x"), lax.axis_index("y"), lax.axis_index("c")


_ANY = pl.BlockSpec(memory_space=pl.ANY)


def _chip_all_gather(shard, name):
    R, C = shard.shape

    def body(x_ref, out_ref, send_sems, recv_sems, local_sem):
        x, y, c = _place()
        chips = [(1 - x, y), (x, 1 - y), (1 - x, 1 - y)]
        mine = pltpu.make_async_copy(x_ref, out_ref.at[2 * x + y], local_sem)
        mine.start()

        def copy(k, slot, to):
            return pltpu.make_async_remote_copy(
                src_ref=x_ref, dst_ref=out_ref.at[slot], send_sem=send_sems.at[k], recv_sem=recv_sems.at[k],
                device_id=to, device_id_type=MESH_IDS)

        sends = [copy(k, 2 * x + y, (cx, cy, c)) for k, (cx, cy) in enumerate(chips)]
        for cp in sends:
            cp.start()
        for k, (cx, cy) in enumerate(chips):
            copy(k, 2 * cx + cy, (cx, cy, c)).wait_recv()
        for cp in sends:
            cp.wait_send()
        mine.wait()

    return pl.pallas_call(
        body, name=name, in_specs=[_ANY], out_specs=_ANY,
        out_shape=jax.ShapeDtypeStruct((N_CHIPS, R, C), shard.dtype),
        scratch_shapes=[pltpu.SemaphoreType.DMA((3,)), pltpu.SemaphoreType.DMA((3,)), pltpu.SemaphoreType.DMA],
    )(shard)


def _sibling_exchange(buf, name):
    def body(x_ref, out_ref, send_sem, recv_sem):
        x, y, c = _place()
        cp = pltpu.make_async_remote_copy(
            src_ref=x_ref, dst_ref=out_ref, send_sem=send_sem, recv_sem=recv_sem,
            device_id=(x, y, 1 - c), device_id_type=MESH_IDS)
        cp.start()
        cp.wait()

    return pl.pallas_call(
        body, name=name, in_specs=[_ANY], out_specs=_ANY,
        out_shape=jax.ShapeDtypeStruct(buf.shape, buf.dtype),
        scratch_shapes=[pltpu.SemaphoreType.DMA, pltpu.SemaphoreType.DMA],
    )(buf)


def _chip_scatter(parts, name):
    _, R, C = parts.shape

    def body(p_ref, out_ref, send_sems, recv_sems):
        x, y, c = _place()
        chips = [(1 - x, y), (x, 1 - y), (1 - x, 1 - y)]

        def copy(k, slab, to):
            return pltpu.make_async_remote_copy(
                src_ref=p_ref.at[slab], dst_ref=out_ref.at[k], send_sem=send_sems.at[k], recv_sem=recv_sems.at[k],
                device_id=to, device_id_type=MESH_IDS)

        sends = [copy(k, 2 * cx + cy, (cx, cy, c)) for k, (cx, cy) in enumerate(chips)]
        for cp in sends:
            cp.start()
        for cp in sends:
            cp.wait_recv()
        for cp in sends:
            cp.wait_send()

    return pl.pallas_call(
        body, name=name, in_specs=[_ANY], out_specs=_ANY,
        out_shape=jax.ShapeDtypeStruct((3, R, C), parts.dtype),
        scratch_shapes=[pltpu.SemaphoreType.DMA((3,)), pltpu.SemaphoreType.DMA((3,))],
    )(parts)


def _all_to_all_small(vec, name):
    R, C = vec.shape

    def body(v_ref, out_ref, send_sems, recv_sems, local_sem):
        x, y, c = _place()
        me = 4 * x + 2 * y + c
        mine = pltpu.make_async_copy(v_ref, out_ref.at[me], local_sem)
        mine.start()
        flips = [(dx, dy, dc) for dx in (0, 1) for dy in (0, 1) for dc in (0, 1)][1:]

        def peer(f):
            return (x ^ f[0], y ^ f[1], c ^ f[2])

        def copy(k, slot, to):
            return pltpu.make_async_remote_copy(
                src_ref=v_ref, dst_ref=out_ref.at[slot], send_sem=send_sems.at[k], recv_sem=recv_sems.at[k],
                device_id=to, device_id_type=MESH_IDS)

        sends = [copy(k, me, peer(f)) for k, f in enumerate(flips)]
        for cp in sends:
            cp.start()
        for k, f in enumerate(flips):
            px, py, pc = peer(f)
            copy(k, 4 * px + 2 * py + pc, peer(f)).wait_recv()
        for cp in sends:
            cp.wait_send()
        mine.wait()

    return pl.pallas_call(
        body, name=name, in_specs=[_ANY], out_specs=_ANY,
        out_shape=jax.ShapeDtypeStruct((8, R, C), vec.dtype),
        scratch_shapes=[pltpu.SemaphoreType.DMA((7,)), pltpu.SemaphoreType.DMA((7,)), pltpu.SemaphoreType.DMA],
    )(vec)


def _to_heads(t, n):
    S = t.shape[0]
    return t.reshape(S, n, HEAD_DIM).transpose(1, 0, 2)


def _from_heads(t):
    H, S, Dh = t.shape
    return t.transpose(1, 0, 2).reshape(S, H * Dh)


def _to_heads_t(t, n):
    S = t.shape[0]
    return t.T.reshape(n, HEAD_DIM, S)


def _from_heads_t(t):
    H, Dh, S = t.shape
    return t.reshape(H * Dh, S).T


def _perm(t, d):
    H, S = t.shape[:2]
    rest = t.shape[2:]
    return t.reshape((H, S // d, d) + rest).swapaxes(1, 2).reshape((H, S) + rest)


def _unperm(t, d):
    H, S = t.shape[:2]
    rest = t.shape[2:]
    return t.reshape((H, d, S // d) + rest).swapaxes(1, 2).reshape((H, S) + rest)


def _rep(t):
    return jnp.broadcast_to(t[..., None], t.shape + (LANES,))


def _t5_bucket(dist):
    max_exact = N_BUCKETS // 2
    d = np.maximum(dist, 1).astype(np.float32)
    large = max_exact + (np.log(d / max_exact) / np.log(MAX_DISTANCE / max_exact)
                         * (N_BUCKETS - max_exact)).astype(np.int32)
    large = np.minimum(large, N_BUCKETS - 1)
    return np.where(dist < max_exact, dist, large).astype(np.int32)


def _band_tables():
    qi = np.arange(BAND)[:, None]
    kj = np.arange(2 * BAND)[None, :]
    sub = qi + BAND - kj
    band = (sub >= 0) & (sub <= BAND)
    out = []
    for d in DILATIONS:
        bucket = _t5_bucket(np.clip(sub, 0, BAND) * d)
        out.append(np.where(band, bucket, -1).astype(np.int32))
    return np.stack(out)


_PACK = (("w_in", 770), ("w_out", 256), ("w_xq", 64), ("w_xk", 64), ("w_xv", 64), ("w_xo", 64),
         ("w_gate", 704), ("w_up", 704), ("w_down", 704))


def _pack(shards):
    rows = [shards[n].reshape(-1, PACK_COLS) for n, _ in _PACK]
    total = sum(r.shape[0] for r in rows)
    pad = (-total) % 16
    if pad:
        rows.append(jnp.zeros((pad, PACK_COLS), rows[0].dtype))
    return jnp.concatenate(rows, axis=0)


def _unpack(pack, shapes):
    out, r = {}, 0
    for n, _ in _PACK:
        cnt = int(np.prod(shapes[n])) // PACK_COLS
        out[n] = pack[r:r + cnt].reshape(shapes[n])
        r += cnt
    return out


_COL_SHARDED = ("w_in", "w_xo", "w_gate", "w_up")


def _full_weight(gathered, name):
    return jnp.concatenate(gathered, axis=1 if name in _COL_SHARDED else 0)


def _split_weight(full, name):
    return jnp.split(full, N_CHIPS, axis=1 if name in _COL_SHARDED else 0)


_SMALL = ("g_mix_pre", "g_mix_post", "g_xattn_pre", "g_mem", "g_xattn_post", "g_ffn_pre", "g_ffn_post")


def _pack_small(vals):
    D = vals["g_mix_pre"].shape[1]
    rows = [vals[n].reshape(1, D) for n in _SMALL]
    misc = jnp.concatenate([vals["b_f"].reshape(-1), vals["rel_bias"].reshape(-1)])
    rows.append(jnp.pad(misc, (0, D - misc.shape[0])).reshape(1, D))
    rows.append(jnp.zeros((16 - len(rows), D), F32))
    return jnp.concatenate(rows, axis=0)


def _unpack_small(pack):
    out = {n: pack[i:i + 1] for i, n in enumerate(_SMALL)}
    out["b_f"] = pack[7, 0:N_FOX_HEADS].reshape(1, N_FOX_HEADS)
    out["rel_bias"] = pack[7, N_FOX_HEADS:N_FOX_HEADS + N_BUCKETS * N_DIL_HEADS].reshape(N_BUCKETS, N_DIL_HEADS)
    return out


def kernel(x, mem, g_mix_pre, w_in, b_f, rel_bias, w_out, g_mix_post, g_xattn_pre, g_mem, w_xq, w_xk, w_xv, w_xo, g_xattn_post, g_ffn_pre, w_gate, w_up, w_down, g_ffn_post, loss_target, m_g_mix_pre, m_w_in, m_b_f, m_rel_bias, m_w_out, m_g_mix_post, m_g_xattn_pre, m_g_mem, m_w_xq, m_w_xk, m_w_xv, m_w_xo, m_g_xattn_post, m_g_ffn_pre, m_w_gate, m_w_up, m_w_down, m_g_ffn_post, v_g_mix_pre, v_w_in, v_b_f, v_rel_bias, v_w_out, v_g_mix_post, v_g_xattn_pre, v_g_mem, v_w_xq, v_w_xk, v_w_xv, v_w_xo, v_g_xattn_post, v_g_ffn_pre, v_w_gate, v_w_up, v_w_down, v_g_ffn_post):
    args = dict(locals())
    big = [n for n, _ in _PACK]
    names = ["g_mix_pre", "w_in", "b_f", "rel_bias", "w_out", "g_mix_post", "g_xattn_pre", "g_mem", "w_xq",
             "w_xk", "w_xv", "w_xo", "g_xattn_post", "g_ffn_pre", "w_gate", "w_up", "w_down", "g_ffn_post"]
    xs = x[0]
    S, D = xs.shape
    assert S % (BAND * DILATIONS[-1]) == 0
    shard_shapes = {n: args[n].shape[1:] for n in big}
    my_x, my_y, my_c = lax.axis_index("x"), lax.axis_index("y"), lax.axis_index("c")

    w_pack = _pack({n: args[n][0] for n in big})
    gathered = _chip_all_gather(w_pack.astype(BF16), "weights_all_gather")
    per_chip = [_unpack(gathered[j], shard_shapes) for j in range(N_CHIPS)]
    W = {n: _full_weight([pc[n] for pc in per_chip], n) for n in big}
    w_fox, w_fg, w_dil = (W["w_in"][:, :3 * FOX_WIDTH], W["w_in"][:, 3 * FOX_WIDTH:3 * FOX_WIDTH + N_FOX_HEADS],
                          W["w_in"][:, 3 * FOX_WIDTH + N_FOX_HEADS:])
    w_qkv = jnp.concatenate([w_fox, w_dil], axis=1)
    w_fg_pad = jnp.pad(w_fg, ((0, 0), (0, LANES - N_FOX_HEADS)))
    F = W["w_gate"].shape[1]
    nft = F // FF_TILE
    w_gu = jnp.stack([W["w_gate"].reshape(D, nft, FF_TILE), W["w_up"].reshape(D, nft, FF_TILE)],
                     axis=2).reshape(D, 2 * F)

    h1 = _rms_fwd(xs, g_mix_pre, "rms_mix_pre")
    qkv = _mm(h1, w_qkv, "nn", BF16, "proj_qkv")
    fg = _mm(h1, w_fg_pad, "nn", F32, "proj_gate")
    fg_t = fg[:, :N_FOX_HEADS].T
    b_col = b_f.reshape(N_FOX_HEADS, 1)
    c_t = _forget_fwd(fg_t, b_col, "forget_cumsum")
    c_row = c_t.reshape(N_FOX_HEADS, 1, S)
    c_rep = _rep(c_t)
    fq_s, fk_s, fv_s = (qkv[:, i * FOX_WIDTH:(i + 1) * FOX_WIDTH] for i in range(3))
    fqt, fkt, fvt = (_to_heads_t(t, N_FOX_HEADS) for t in (fq_s, fk_s, fv_s))
    fk, fv = _to_heads(fk_s, N_FOX_HEADS), _to_heads(fv_s, N_FOX_HEADS)
    o_fox_t, lse_fox = _fox_fwd(fqt, fk, fvt, c_row, c_rep, "fox_fwd")

    dq_, dk_, dv_ = (_to_heads(qkv[:, 3 * FOX_WIDTH + i * DIL_WIDTH:3 * FOX_WIDTH + (i + 1) * DIL_WIDTH],
                               N_DIL_HEADS) for i in range(3))
    qd = jnp.stack([_perm(dq_, d) for d in DILATIONS])
    kd = jnp.stack([_perm(dk_, d) for d in DILATIONS])
    vd = jnp.stack([_perm(dv_, d) for d in DILATIONS])
    bucket_map = _band_tables()
    onehot = (jnp.asarray(bucket_map)[..., None] == jnp.arange(N_BUCKETS)).astype(F32)
    bias_tab = jnp.einsum("pqkb,bh->phqk", onehot, rel_bias, precision=lax.Precision.HIGHEST)
    bias_tab = jnp.where(jnp.asarray(bucket_map >= 0)[:, None], bias_tab, NEG)
    lse_p = _band_lse(qd, kd, bias_tab, "dilated_lse")[..., 0]
    lse_tok = jnp.stack([_unperm(lse_p[p], d) for p, d in enumerate(DILATIONS)])
    lse_joint = _lse_join(lse_tok, "dilated_lse_join")
    lse_joint_rep = _rep(jnp.stack([_perm(lse_joint, d) for d in DILATIONS]))
    o_p = _band_out(qd, kd, vd, bias_tab, lse_joint_rep, "dilated_out")
    o_cat = jnp.concatenate(
        [_from_heads_t(o_fox_t)] + [_from_heads(_unperm(o_p[p], d)) for p, d in enumerate(DILATIONS)],
        axis=1).astype(BF16)
    w_out_b = W["w_out"]
    w_out_cat = jnp.concatenate([w_out_b[:FOX_WIDTH]] + [w_out_b[FOX_WIDTH:]] * 3, axis=0)
    a = _mm(o_cat, w_out_cat, "nn", F32, "proj_out")
    x1, h2 = _resid_norm(xs, a, g_mix_post, g_xattn_pre, "resid_mix")

    hm = _rms_fwd(mem[0], g_mem, "rms_mem")
    q2 = _mm(h2, W["w_xq"], "nn", BF16, "xattn_q")
    w_xkv = jnp.concatenate([W["w_xk"], W["w_xv"]], axis=1)
    kvm = _mm(hm, w_xkv, "nn", BF16, "xattn_kv")
    MW = N_MEM_HEADS * HEAD_DIM
    q2h, kmh, vmh = _to_heads(q2, N_MEM_HEADS), _to_heads(kvm[:, :MW], N_MEM_HEADS), _to_heads(kvm[:, MW:], N_MEM_HEADS)
    o_mem, lse_mem = _mem_fwd(q2h, kmh, vmh, "xattn_fwd")
    oc = _from_heads(o_mem).astype(BF16)
    y2 = _mm(oc, W["w_xo"], "nn", F32, "xattn_o")
    x2, h3 = _resid_norm(x1, y2, g_xattn_post, g_ffn_pre, "resid_xattn")

    gu, act = _ffn_up(h3, w_gu, "ffn_up")
    y3 = _mm(act, W["w_down"], "nn", F32, "ffn_down", tk=1536)
    dx3, loss_tile = _final_loss(x2, y3, g_ffn_post, loss_target[0], "final_loss")

    grads = {}
    small = {}
    dy3, dy3_b, dg = _rms_bwd(y3, g_ffn_post, dx3, None, "bwd_norm_ffn_post")
    small["g_ffn_post"] = dg[0:1]
    grads["w_down"] = _mm(act, dy3_b, "tn", F32, "grad_w_down")
    dgu = _ffn_dact(dy3_b, W["w_down"], gu, "ffn_dact")
    dw_gu = _mm(h3, dgu, "tn", F32, "grad_w_gu").reshape(D, nft, 2, FF_TILE)
    grads["w_gate"], grads["w_up"] = dw_gu[:, :, 0].reshape(D, F), dw_gu[:, :, 1].reshape(D, F)
    dh3 = _mm(dgu, w_gu, "nt", F32, "bwd_ffn_in", tk=1024)
    dx2, _, dg = _rms_bwd(x2, g_ffn_pre, dh3, dx3, "bwd_norm_ffn_pre")
    small["g_ffn_pre"] = dg[0:1]

    dy2, dy2_b, dg = _rms_bwd(y2, g_xattn_post, dx2, None, "bwd_norm_xattn_post")
    small["g_xattn_post"] = dg[0:1]
    grads["w_xo"] = _mm(oc, dy2_b, "tn", F32, "grad_w_xo")
    doc = _mm(dy2_b, W["w_xo"], "nt", BF16, "bwd_xattn_o")
    delta_mem = _head_rowdot(doc, oc, N_MEM_HEADS, "xattn_delta")
    delta_mem_rep = _rep(delta_mem[:, :N_MEM_HEADS].T)
    dq2h, dkmh, dvmh = _mem_bwd(q2h, kmh, vmh, _to_heads(doc, N_MEM_HEADS), lse_mem, delta_mem_rep, "xattn_bwd")
    dq2 = _from_heads(dq2h).astype(BF16)
    dkvm = jnp.concatenate([_from_heads(dkmh), _from_heads(dvmh)], axis=1).astype(BF16)
    grads["w_xq"] = _mm(h2, dq2, "tn", F32, "grad_w_xq")
    dw_xkv = _mm(hm, dkvm, "tn", F32, "grad_w_xkv")
    grads["w_xk"], grads["w_xv"] = dw_xkv[:, :MW], dw_xkv[:, MW:]
    dhm = _mm(dkvm, w_xkv, "nt", F32, "bwd_xattn_kv")
    _, _, dg = _rms_bwd(mem[0], g_mem, dhm, None, "bwd_norm_mem")
    small["g_mem"] = dg[0:1]
    dh2 = _mm(dq2, W["w_xq"], "nt", F32, "bwd_xattn_q")
    dx1, _, dg = _rms_bwd(x1, g_xattn_pre, dh2, dx2, "bwd_norm_xattn_pre")
    small["g_xattn_pre"] = dg[0:1]

    da, da_b, dg = _rms_bwd(a, g_mix_post, dx1, None, "bwd_norm_mix_post")
    small["g_mix_post"] = dg[0:1]
    dw_out_cat = _mm(o_cat, da_b, "tn", F32, "grad_w_out")
    dw_out_dil = _add_n([dw_out_cat[FOX_WIDTH + p * DIL_WIDTH:FOX_WIDTH + (p + 1) * DIL_WIDTH] for p in range(3)],
                        "grad_w_out_dil")
    grads["w_out"] = jnp.concatenate([dw_out_cat[:FOX_WIDTH], dw_out_dil], axis=0)
    do = _mm(da_b, w_out_b, "nt", BF16, "bwd_proj_out")
    do_fox, do_dil = do[:, :FOX_WIDTH], do[:, FOX_WIDTH:]

    delta_fox = _head_rowdot(do_fox, o_cat[:, :FOX_WIDTH], N_FOX_HEADS, "fox_delta")[:, :N_FOX_HEADS].T
    dqf, dkf, dvf, dcs, drs = _fox_bwd(fqt, fk, fkt, fv, _to_heads_t(do_fox, N_FOX_HEADS), c_rep,
                                       lse_fox, delta_fox.reshape(N_FOX_HEADS, 1, S), "fox_bwd")
    dfg_t, db_f = _forget_bwd(fg_t, b_col, drs[:, 0], dcs[..., 0], "forget_bwd")

    delta_dil = _head_rowdot(jnp.tile(do_dil, (1, 3)), o_cat[:, FOX_WIDTH:], N_DIL_HEADS, "dilated_delta")
    delta_dil_t = delta_dil[:, :N_DIL_HEADS].T
    delta_dil_rep = _rep(jnp.stack([_perm(delta_dil_t, d) for d in DILATIONS]))
    do_dil_h = _to_heads(do_dil, N_DIL_HEADS)
    dod = jnp.stack([_perm(do_dil_h, d) for d in DILATIONS])
    dqd, dkd, dvd, dbias = _band_bwd(qd, kd, vd, dod, bias_tab, lse_joint_rep, delta_dil_rep, "dilated_bwd")
    d_rel = _bucket_reduce(dbias, jnp.asarray(bucket_map), "rel_bias_grad")[:, :N_DIL_HEADS]

    def tok(p, d):
        return jnp.concatenate([_from_heads(_unperm(t[p], d)) for t in (dqd, dkd, dvd)], axis=1)

    d_dil = _add_n([tok(p, d) for p, d in enumerate(DILATIONS)], "dilated_grad_sum")
    dqkv = jnp.concatenate([_from_heads_t(dqf), _from_heads_t(dkf), _from_heads_t(dvf), d_dil],
                           axis=1).astype(BF16)
    dfg_pad = jnp.pad(dfg_t.T, ((0, 0), (0, LANES - N_FOX_HEADS))).astype(BF16)
    dw_qkv = _mm(h1, dqkv, "tn", F32, "grad_w_qkv")
    dw_fg = _mm(h1, dfg_pad, "tn", F32, "grad_w_gate_cols")[:, :N_FOX_HEADS]
    grads["w_in"] = jnp.concatenate([dw_qkv[:, :3 * FOX_WIDTH], dw_fg, dw_qkv[:, 3 * FOX_WIDTH:]], axis=1)
    dcat = jnp.concatenate([dqkv, dfg_pad], axis=1)
    w_cat = jnp.concatenate([w_qkv, w_fg_pad], axis=1)
    dh1 = _mm(dcat, w_cat, "nt", F32, "bwd_proj_in", tk=640)
    grad_x, _, dg = _rms_bwd(xs, g_mix_pre, dh1, dx1, "bwd_norm_mix_pre")
    small["g_mix_pre"] = dg[0:1]
    small["b_f"] = db_f[:, 0].reshape(1, N_FOX_HEADS)
    small["rel_bias"] = d_rel

    split = {n: _split_weight(grads[n], n) for n in big}
    parts = jnp.stack([_pack({n: split[n][j] for n in big}) for j in range(N_CHIPS)])
    R = parts.shape[1]
    half = R // 2
    keep = lax.dynamic_slice_in_dim(parts, my_c * half, half, axis=1)
    give = lax.dynamic_slice_in_dim(parts, (1 - my_c) * half, half, axis=1)
    got = _sibling_exchange(give, "grads_to_sibling")
    chip_sum = _add_n([keep.reshape(-1, PACK_COLS), got.reshape(-1, PACK_COLS)], "grads_add_sibling")
    chip_sum = chip_sum.reshape(N_CHIPS, half, PACK_COLS)
    my_chip = 2 * my_x + my_y
    from_chips = _chip_scatter(chip_sum, "grads_to_chips")
    own = lax.dynamic_index_in_dim(chip_sum, my_chip, axis=0, keepdims=False)
    g_half = _add_n([own, from_chips[0], from_chips[1], from_chips[2]], "grads_add_chips")
    other_half = _sibling_exchange(g_half, "grads_share_sibling")
    g_pack = jnp.where(my_c == 0, jnp.concatenate([g_half, other_half]), jnp.concatenate([other_half, g_half]))

    small_pack = _pack_small(small)
    small_pack = small_pack.at[8, 0].set(loss_tile[0, 0])
    everyone = _all_to_all_small(small_pack, "small_all_gather")
    small_sum = _add_n([everyone[i] for i in range(8)], "small_sum")
    loss = small_sum[8, 0]
    g_small = _unpack_small(small_sum)

    m_pack = _pack({n: args["m_" + n][0] for n in big})
    v_pack = _pack({n: args["v_" + n][0] for n in big})
    d_pack, nm_pack, nv_pack = _adamw(w_pack, g_pack, m_pack, v_pack, "adamw_big")
    outs = {"grad": _unpack(g_pack, shard_shapes), "delta": _unpack(d_pack, shard_shapes),
            "new_m": _unpack(nm_pack, shard_shapes), "new_v": _unpack(nv_pack, shard_shapes)}
    sw = _pack_small({n: args[n] for n in _SMALL + ("b_f", "rel_bias")})
    sm = _pack_small({n: args["m_" + n] for n in _SMALL + ("b_f", "rel_bias")})
    sv = _pack_small({n: args["v_" + n] for n in _SMALL + ("b_f", "rel_bias")})
    sd, snm, snv = _adamw(sw, small_sum.at[8, 0].set(0.0), sm, sv, "adamw_small")
    souts = {"grad": g_small, "delta": _unpack_small(sd), "new_m": _unpack_small(snm), "new_v": _unpack_small(snv)}

    def leaf(kind, n):
        if n in souts[kind]:
            return souts[kind][n].reshape(args[n].shape)
        return outs[kind][n].reshape(args[n].shape)

    result = [loss, grad_x.reshape(x.shape)]
    for kind in ("grad", "delta", "new_m", "new_v"):
        result += [leaf(kind, n) for n in names]
    return tuple(result)
```

```python
import numpy as np
import jax
import jax.numpy as jnp
from jax import lax
from jax.experimental import pallas as pl
from jax.experimental.pallas import tpu as pltpu

F32 = jnp.float32
BF16 = jnp.bfloat16
MESH_IDS = pl.DeviceIdType.MESH

LANES = 128
HEAD_DIM = 64
N_FOX_HEADS = 8
N_DIL_HEADS = 8
N_MEM_HEADS = 4
FOX_WIDTH = N_FOX_HEADS * HEAD_DIM
DIL_WIDTH = N_DIL_HEADS * HEAD_DIM
DILATIONS = (1, 4, 16)
BAND = 128
BAND_CHUNK_MAX = 8 * BAND
N_BUCKETS = 32
MAX_DISTANCE = 2048
QK_SCALE = HEAD_DIM ** -0.5
RMS_EPS = 1e-6
NEG = -1e30
VMEM_LIMIT = 56 << 20

ADAM_LR = 0.001
ADAM_B1 = 0.9
ADAM_B2 = 0.999
ADAM_EPS = 1e-08
ADAM_WD = 0.01
ADAM_STEP = 10

N_CHIPS = 4
PACK_COLS = 1024


def _params(*sem):
    return pltpu.CompilerParams(dimension_semantics=sem, vmem_limit_bytes=VMEM_LIMIT)


def _fit(n, cap):
    if n <= cap:
        return n
    t = (cap // LANES) * LANES
    while t >= LANES:
        if n % t == 0:
            return t
        t -= LANES
    raise ValueError(f"no lane-aligned tile for {n} under {cap}")


def _dot(a, b, dims):
    return lax.dot_general(a, b, (dims, ((), ())), preferred_element_type=F32)


_NN = ((1,), (0,))
_NT = ((1,), (1,))
_TN = ((0,), (0,))


def _mm(a, b, mode, out_dtype, name, tm=1024, tn=1024, tk=512):
    if mode == "nn":
        (M, K), N = a.shape, b.shape[1]
    elif mode == "nt":
        (M, K), N = a.shape, b.shape[0]
    else:
        (K, M), N = a.shape, b.shape[1]
    tm, tn, tk = _fit(M, tm), _fit(N, tn), _fit(K, tk)
    nk = K // tk
    if mode == "tn":
        a_spec = pl.BlockSpec((tk, tm), lambda i, j, k: (k, i))
    else:
        a_spec = pl.BlockSpec((tm, tk), lambda i, j, k: (i, k))
    if mode == "nt":
        b_spec = pl.BlockSpec((tn, tk), lambda i, j, k: (j, k))
    else:
        b_spec = pl.BlockSpec((tk, tn), lambda i, j, k: (k, j))
    dims = {"nn": _NN, "nt": _NT, "tn": _TN}[mode]

    def body(a_ref, b_ref, o_ref, acc_ref):
        k = pl.program_id(2)

        @pl.when(k == 0)
        def _():
            acc_ref[...] = jnp.zeros_like(acc_ref)

        acc_ref[...] += _dot(a_ref[...].astype(BF16), b_ref[...].astype(BF16), dims)

        @pl.when(k == nk - 1)
        def _():
            o_ref[...] = acc_ref[...].astype(o_ref.dtype)

    return pl.pallas_call(
        body, name=name, grid=(M // tm, N // tn, nk),
        in_specs=[a_spec, b_spec],
        out_specs=pl.BlockSpec((tm, tn), lambda i, j, k: (i, j)),
        out_shape=jax.ShapeDtypeStruct((M, N), out_dtype),
        scratch_shapes=[pltpu.VMEM((tm, tn), F32)],
        compiler_params=_params("parallel", "parallel", "arbitrary"),
    )(a, b)


def _rms_rows(x):
    return lax.rsqrt(jnp.mean(x * x, axis=-1, keepdims=True) + RMS_EPS)


def _rms_fwd(x, g, name, tr=512):
    S, D = x.shape
    tr = _fit(S, tr)

    def body(x_ref, g_ref, h_ref):
        xv = x_ref[...]
        h_ref[...] = (xv * _rms_rows(xv) * g_ref[...]).astype(BF16)

    return pl.pallas_call(
        body, name=name, grid=(S // tr,),
        in_specs=[pl.BlockSpec((tr, D), lambda i: (i, 0)), pl.BlockSpec((1, D), lambda i: (0, 0))],
        out_specs=pl.BlockSpec((tr, D), lambda i: (i, 0)),
        out_shape=jax.ShapeDtypeStruct((S, D), BF16),
        compiler_params=_params("parallel"),
    )(x, g)


def _resid_norm(xres, y, g_post, g_next, name, tr=512):
    S, D = xres.shape
    tr = _fit(S, tr)

    def body(x_ref, y_ref, gp_ref, gn_ref, xn_ref, h_ref):
        yv = y_ref[...]
        xn = x_ref[...] + yv * _rms_rows(yv) * gp_ref[...]
        xn_ref[...] = xn
        h_ref[...] = (xn * _rms_rows(xn) * gn_ref[...]).astype(BF16)

    row = pl.BlockSpec((tr, D), lambda i: (i, 0))
    vec = pl.BlockSpec((1, D), lambda i: (0, 0))
    return pl.pallas_call(
        body, name=name, grid=(S // tr,),
        in_specs=[row, row, vec, vec], out_specs=[row, row],
        out_shape=[jax.ShapeDtypeStruct((S, D), F32), jax.ShapeDtypeStruct((S, D), BF16)],
        compiler_params=_params("parallel"),
    )(xres, y, g_post, g_next)


def _final_loss(xres, y, g_post, target, name, tr=512):
    S, D = xres.shape
    tr = _fit(S, tr)

    def body(x_ref, y_ref, gp_ref, t_ref, d_ref, loss_ref):
        i = pl.program_id(0)
        yv = y_ref[...]
        err = x_ref[...] + yv * _rms_rows(yv) * gp_ref[...] - t_ref[...]
        d_ref[...] = err * (1.0 / D)

        @pl.when(i == 0)
        def _():
            loss_ref[...] = jnp.zeros_like(loss_ref)

        part = jnp.sum(jnp.sum(err * err, axis=1, keepdims=True), axis=0, keepdims=True)
        loss_ref[...] += jnp.broadcast_to(part * (0.5 / D), loss_ref.shape)

    row = pl.BlockSpec((tr, D), lambda i: (i, 0))
    vec = pl.BlockSpec((1, D), lambda i: (0, 0))
    return pl.pallas_call(
        body, name=name, grid=(S // tr,),
        in_specs=[row, row, vec, row],
        out_specs=[row, pl.BlockSpec((8, LANES), lambda i: (0, 0))],
        out_shape=[jax.ShapeDtypeStruct((S, D), F32), jax.ShapeDtypeStruct((8, LANES), F32)],
        compiler_params=_params("arbitrary"),
    )(xres, y, g_post, target)


def _rms_bwd(xin, g, dy, dres, name, tr=512):
    S, D = xin.shape
    tr = _fit(S, tr)
    has_res = dres is not None

    def body(*refs):
        if has_res:
            x_ref, g_ref, dy_ref, dr_ref, dx_ref, dxb_ref, dg_ref = refs
        else:
            x_ref, g_ref, dy_ref, dx_ref, dxb_ref, dg_ref = refs
        i = pl.program_id(0)
        xv = x_ref[...]
        dyv = dy_ref[...].astype(F32)
        xhat = xv * _rms_rows(xv)
        dxhat = dyv * g_ref[...]
        r = _rms_rows(xv)
        dx = r * (dxhat - xhat * jnp.mean(dxhat * xhat, axis=-1, keepdims=True))
        if has_res:
            dx = dx + dr_ref[...]
        dx_ref[...] = dx
        dxb_ref[...] = dx.astype(BF16)

        @pl.when(i == 0)
        def _():
            dg_ref[...] = jnp.zeros_like(dg_ref)

        dg_ref[...] += jnp.broadcast_to(jnp.sum(dyv * xhat, axis=0, keepdims=True), dg_ref.shape)

    row = pl.BlockSpec((tr, D), lambda i: (i, 0))
    vec = pl.BlockSpec((1, D), lambda i: (0, 0))
    acc = pl.BlockSpec((8, D), lambda i: (0, 0))
    ins = [xin, g, dy] + ([dres] if has_res else [])
    return pl.pallas_call(
        body, name=name, grid=(S // tr,),
        in_specs=[row, vec, row] + ([row] if has_res else []),
        out_specs=[row, row, acc],
        out_shape=[jax.ShapeDtypeStruct((S, D), F32), jax.ShapeDtypeStruct((S, D), BF16),
                   jax.ShapeDtypeStruct((8, D), F32)],
        compiler_params=_params("arbitrary"),
    )(*ins)


def _tri(n, upper):
    r = lax.broadcasted_iota(jnp.int32, (n, n), 0)
    c = lax.broadcasted_iota(jnp.int32, (n, n), 1)
    return jnp.where((r <= c) if upper else (r >= c), 1.0, 0.0).astype(F32)


def _forget_fwd(fg_t, b_col, name, ts=512):
    H, S = fg_t.shape
    ts = _fit(S, ts)

    def body(f_ref, b_ref, c_ref, carry_ref):
        i = pl.program_id(0)

        @pl.when(i == 0)
        def _():
            carry_ref[...] = jnp.zeros_like(carry_ref)

        z = f_ref[...] + b_ref[...]
        logf = jnp.minimum(z, 0.0) - jnp.log(1.0 + jnp.exp(-jnp.abs(z)))
        run = lax.dot_general(logf, _tri(ts, True), (_NN, ((), ())), precision=lax.Precision.HIGHEST,
                              preferred_element_type=F32) + carry_ref[:, 0:1]
        c_ref[...] = run
        carry_ref[...] = jnp.broadcast_to(
            carry_ref[:, 0:1] + jnp.sum(logf, axis=1, keepdims=True), carry_ref.shape)

    return pl.pallas_call(
        body, name=name, grid=(S // ts,),
        in_specs=[pl.BlockSpec((H, ts), lambda i: (0, i)), pl.BlockSpec((H, 1), lambda i: (0, 0))],
        out_specs=pl.BlockSpec((H, ts), lambda i: (0, i)),
        out_shape=jax.ShapeDtypeStruct((H, S), F32),
        scratch_shapes=[pltpu.VMEM((H, LANES), F32)],
        compiler_params=_params("arbitrary"),
    )(fg_t, b_col)


def _forget_bwd(fg_t, b_col, dc_plus, dc_minus, name, ts=512):
    H, S = fg_t.shape
    ts = _fit(S, ts)
    nb = S // ts

    def body(f_ref, b_ref, dcp_ref, dcm_ref, df_ref, db_ref, carry_ref):
        i = pl.program_id(0)

        @pl.when(i == 0)
        def _():
            carry_ref[...] = jnp.zeros_like(carry_ref)
            db_ref[...] = jnp.zeros_like(db_ref)

        dc = dcp_ref[...] - dcm_ref[...]
        suffix = lax.dot_general(dc, _tri(ts, False), (_NN, ((), ())), precision=lax.Precision.HIGHEST,
                                 preferred_element_type=F32) + carry_ref[:, 0:1]
        z = f_ref[...] + b_ref[...]
        sig_neg = 1.0 / (1.0 + jnp.exp(z))
        df = suffix * sig_neg
        df_ref[...] = df
        carry_ref[...] = jnp.broadcast_to(
            carry_ref[:, 0:1] + jnp.sum(dc, axis=1, keepdims=True), carry_ref.shape)
        db_ref[...] += jnp.broadcast_to(jnp.sum(df, axis=1, keepdims=True), db_ref.shape)

    rev = pl.BlockSpec((H, ts), lambda i: (0, nb - 1 - i))
    return pl.pallas_call(
        body, name=name, grid=(nb,),
        in_specs=[rev, pl.BlockSpec((H, 1), lambda i: (0, 0)), rev, rev],
        out_specs=[rev, pl.BlockSpec((H, LANES), lambda i: (0, 0))],
        out_shape=[jax.ShapeDtypeStruct((H, S), F32), jax.ShapeDtypeStruct((H, LANES), F32)],
        scratch_shapes=[pltpu.VMEM((H, LANES), F32)],
        compiler_params=_params("arbitrary"),
    )(fg_t, b_col, dc_plus, dc_minus)


def _tile_lanes(x, n):
    return x if n == LANES else jnp.tile(x, (1, n // LANES))


def _fox_fwd(qt, k, vt, c_row, c_rep, name, tq=512, tk=1024):
    H, Dh, S = qt.shape
    tk = _fit(S, tk)
    tq = _fit(tk, tq)
    ratio = tk // tq

    def body(qt_ref, k_ref, vt_ref, c_ref, crep_ref, o_ref, lse_ref, m_ref, l_ref, acc_ref):
        i = pl.program_id(1)
        qv = qt_ref[...] * QK_SCALE
        cq0 = c_ref[:, pl.ds(pl.multiple_of(i * tq, LANES), LANES)][:, 0:1]
        m_ref[...] = jnp.full_like(m_ref, NEG)
        l_ref[...] = jnp.zeros_like(l_ref)
        acc_ref[...] = jnp.zeros_like(acc_ref)
        j_diag = i // ratio
        q_off = (i - j_diag * ratio) * tq

        def step(j, diagonal):
            off = pl.multiple_of(j * tk, LANES)
            s = _dot(k_ref[pl.ds(off, tk), :], qv, _NN) + _tile_lanes(cq0 - crep_ref[pl.ds(off, tk), :], tq)
            if diagonal:
                key = lax.broadcasted_iota(jnp.int32, (tk, tq), 0)
                qry = lax.broadcasted_iota(jnp.int32, (tk, tq), 1) + q_off
                s = jnp.where(key <= qry, s, NEG)
            m_old = m_ref[...]
            m_new = jnp.maximum(m_old, jnp.max(s, axis=0, keepdims=True))
            p = jnp.exp(s - m_new)
            alpha = jnp.exp(m_old - m_new)
            l_ref[...] = alpha * l_ref[...] + jnp.sum(p, axis=0, keepdims=True)
            acc_ref[...] = alpha * acc_ref[...] + _dot(vt_ref[:, pl.ds(off, tk)], p.astype(BF16), _NN)
            m_ref[...] = m_new

        def loop_body(j, carry):
            step(j, False)
            return carry

        lax.fori_loop(0, j_diag, loop_body, 0)
        step(j_diag, True)
        o_ref[...] = acc_ref[...] / l_ref[...]
        lse_ref[...] = m_ref[...] + jnp.log(l_ref[...]) - cq0

    lanes_full = pl.BlockSpec((None, Dh, S), lambda h, i: (h, 0, 0))
    lanes_tile = pl.BlockSpec((None, Dh, tq), lambda h, i: (h, 0, i))
    return pl.pallas_call(
        body, name=name, grid=(H, S // tq),
        in_specs=[lanes_tile, pl.BlockSpec((None, S, Dh), lambda h, i: (h, 0, 0)), lanes_full,
                  pl.BlockSpec((None, 1, S), lambda h, i: (h, 0, 0)),
                  pl.BlockSpec((None, S, LANES), lambda h, i: (h, 0, 0))],
        out_specs=[lanes_tile, pl.BlockSpec((None, 1, tq), lambda h, i: (h, 0, i))],
        out_shape=[jax.ShapeDtypeStruct((H, Dh, S), F32), jax.ShapeDtypeStruct((H, 1, S), F32)],
        scratch_shapes=[pltpu.VMEM((1, tq), F32), pltpu.VMEM((1, tq), F32), pltpu.VMEM((Dh, tq), F32)],
        compiler_params=_params("parallel", "arbitrary"),
    )(qt, k, vt, c_row, c_rep)


def _fox_bwd(qt, k, kt, v, dot, c_rep, lse_row, delta_row, name, tq=1024, tk=512):
    H, Dh, S = qt.shape
    tq = _fit(S, tq)
    tk = _fit(tq, tk)
    ratio = tq // tk
    nq = S // tq

    def body(k_ref, kt_ref, v_ref, crep_ref, qt_ref, dot_ref, lse_ref, dl_ref,
             dqt_ref, dkt_ref, dvt_ref, dcs_ref, drs_ref, dka_ref, dva_ref, dca_ref):
        j = pl.program_id(1)

        @pl.when(j == 0)
        def _():
            dqt_ref[...] = jnp.zeros_like(dqt_ref)
            drs_ref[...] = jnp.zeros_like(drs_ref)

        kv = k_ref[...]
        ktv = kt_ref[...]
        vv = v_ref[...]
        c_col = _tile_lanes(crep_ref[...], tq)
        dka_ref[...] = jnp.zeros_like(dka_ref)
        dva_ref[...] = jnp.zeros_like(dva_ref)
        dca_ref[...] = jnp.zeros_like(dca_ref)
        i_diag = j // ratio
        k_off = (j - i_diag * ratio) * tk

        def step(i, diagonal):
            off = pl.multiple_of(i * tq, LANES)
            qv = qt_ref[:, pl.ds(off, tq)] * QK_SCALE
            dov = dot_ref[:, pl.ds(off, tq)]
            e = _dot(kv, qv, _NN) - lse_ref[:, pl.ds(off, tq)] - c_col
            if diagonal:
                key = lax.broadcasted_iota(jnp.int32, (tk, tq), 0) + k_off
                qry = lax.broadcasted_iota(jnp.int32, (tk, tq), 1)
                e = jnp.where(key <= qry, e, NEG)
            p_t = jnp.exp(e)
            dva_ref[...] += _dot(dov, p_t.astype(BF16), _NT)
            ds_t = p_t * (_dot(vv, dov, _NN) - dl_ref[:, pl.ds(off, tq)])
            ds_b = ds_t.astype(BF16)
            dka_ref[...] += _dot(qv, ds_b, _NT)
            dqt_ref[:, pl.ds(off, tq)] += _dot(ktv, ds_b, _NN) * QK_SCALE
            drs_ref[:, pl.ds(off, tq)] += jnp.sum(ds_t, axis=0, keepdims=True)
            part = ds_t[:, 0:LANES]
            for cidx in range(1, tq // LANES):
                part = part + ds_t[:, cidx * LANES:(cidx + 1) * LANES]
            dca_ref[...] += part

        step(i_diag, True)

        def loop_body(i, carry):
            step(i, False)
            return carry

        lax.fori_loop(i_diag + 1, nq, loop_body, 0)
        dkt_ref[...] = dka_ref[...]
        dvt_ref[...] = dva_ref[...]
        dcs_ref[...] = jnp.broadcast_to(jnp.sum(dca_ref[...], axis=1, keepdims=True), dcs_ref.shape)

    rows_tile = pl.BlockSpec((None, tk, Dh), lambda h, j: (h, j, 0))
    lanes_tile = pl.BlockSpec((None, Dh, tk), lambda h, j: (h, 0, j))
    rep = pl.BlockSpec((None, tk, LANES), lambda h, j: (h, j, 0))
    lanes_full = pl.BlockSpec((None, Dh, S), lambda h, j: (h, 0, 0))
    rowv = pl.BlockSpec((None, 1, S), lambda h, j: (h, 0, 0))
    return pl.pallas_call(
        body, name=name, grid=(H, S // tk),
        in_specs=[rows_tile, lanes_tile, rows_tile, rep, lanes_full, lanes_full, rowv, rowv],
        out_specs=[lanes_full, lanes_tile, lanes_tile, rep, rowv],
        out_shape=[jax.ShapeDtypeStruct((H, Dh, S), F32), jax.ShapeDtypeStruct((H, Dh, S), F32),
                   jax.ShapeDtypeStruct((H, Dh, S), F32), jax.ShapeDtypeStruct((H, S, LANES), F32),
                   jax.ShapeDtypeStruct((H, 1, S), F32)],
        scratch_shapes=[pltpu.VMEM((Dh, tk), F32), pltpu.VMEM((Dh, tk), F32), pltpu.VMEM((tk, LANES), F32)],
        compiler_params=_params("parallel", "arbitrary"),
    )(k, kt, v, c_rep, qt, dot, lse_row, delta_row)


QKV_BLOCKS = 3 * (FOX_WIDTH + DIL_WIDTH) // LANES
DIL_Q_BLOCK = 3 * FOX_WIDTH // LANES
HEAD_PAIRS = N_DIL_HEADS // 2
PAIR_BLOCKS = DIL_WIDTH // LANES


def _band_geometry(S, d):
    L = S // d
    chunk = min(BAND_CHUNK_MAX, L)
    assert L % chunk == 0 and chunk % BAND == 0
    return L, chunk, chunk // BAND, L // chunk


def _band_in_specs(S, d):
    L, chunk, nb, _ = _band_geometry(S, d)

    def col(kind):
        return lambda hp, r, i: (i, r * QKV_BLOCKS + DIL_Q_BLOCK + kind * PAIR_BLOCKS + hp)

    def col_prev(kind):
        return lambda hp, r, i: (jnp.maximum(i * nb - 1, 0), r * QKV_BLOCKS + DIL_Q_BLOCK + kind * PAIR_BLOCKS + hp)

    main = [pl.BlockSpec((chunk, LANES), col(kind)) for kind in range(3)]
    prev = [pl.BlockSpec((BAND, LANES), col_prev(kind)) for kind in range(3)]
    bias = pl.BlockSpec((None, 2, 2 * BAND, BAND), lambda hp, r, i: (hp, 0, 0, 0))
    stat = pl.BlockSpec((None, 2, chunk), lambda hp, r, i: (hp, 0, r * (L // chunk) + i))
    tok = pl.BlockSpec((chunk, LANES), lambda hp, r, i: (i, r * PAIR_BLOCKS + hp))
    return main, prev, bias, stat, tok


def _band_scores_t(kb, qb, bias_t, first):
    s = _dot(kb, qb, _NT) + bias_t
    if first is not None:
        key = lax.broadcasted_iota(jnp.int32, s.shape, 0)
        s = jnp.where(jnp.logical_and(first, key < BAND), NEG, s)
    return s


def _dil_lse(qkv_v, bias_t, d, name):
    L = qkv_v.shape[0]
    S = L * d
    _, chunk, nb, nchunks = _band_geometry(S, d)
    main, prev, bias, stat, _ = _band_in_specs(S, d)

    def body(q_ref, k_ref, kp_ref, b_ref, lse_ref, kext_ref):
        first = pl.program_id(2) == 0
        kext_ref[0:BAND, :] = kp_ref[...]
        kext_ref[BAND:, :] = k_ref[...]
        for a in range(2):
            lanes = slice(a * HEAD_DIM, (a + 1) * HEAD_DIM)
            bias_v = b_ref[a]
            for b in range(nb):
                s = _band_scores_t(kext_ref[b * BAND:(b + 2) * BAND, lanes],
                                   q_ref[b * BAND:(b + 1) * BAND, lanes] * QK_SCALE, bias_v, first if b == 0 else None)
                m = jnp.max(s, axis=0, keepdims=True)
                lse_ref[a:a + 1, b * BAND:(b + 1) * BAND] = m + jnp.log(jnp.sum(jnp.exp(s - m), axis=0, keepdims=True))

    return pl.pallas_call(
        body, name=name, grid=(HEAD_PAIRS, d, nchunks),
        in_specs=[main[0], main[1], prev[1], bias], out_specs=stat,
        out_shape=jax.ShapeDtypeStruct((HEAD_PAIRS, 2, S), F32),
        scratch_shapes=[pltpu.VMEM((chunk + BAND, LANES), BF16)],
        compiler_params=_params("parallel", "parallel", "parallel"),
    )(qkv_v, qkv_v, qkv_v, bias_t)


def _dil_out(qkv_v, bias_t, lse_joint, d, name):
    L = qkv_v.shape[0]
    S = L * d
    _, chunk, nb, nchunks = _band_geometry(S, d)
    main, prev, bias, stat, tok = _band_in_specs(S, d)

    def body(q_ref, k_ref, kp_ref, v_ref, vp_ref, b_ref, lse_ref, o_ref, kext_ref, vext_ref):
        first = pl.program_id(2) == 0
        kext_ref[0:BAND, :] = kp_ref[...]
        kext_ref[BAND:, :] = k_ref[...]
        vext_ref[0:BAND, :] = vp_ref[...]
        vext_ref[BAND:, :] = v_ref[...]
        for a in range(2):
            lanes = slice(a * HEAD_DIM, (a + 1) * HEAD_DIM)
            bias_v = b_ref[a]
            for b in range(nb):
                rows, ext = slice(b * BAND, (b + 1) * BAND), slice(b * BAND, (b + 2) * BAND)
                s = _band_scores_t(kext_ref[ext, lanes], q_ref[rows, lanes] * QK_SCALE, bias_v, first if b == 0 else None)
                p_t = jnp.exp(s - lse_ref[a:a + 1, rows])
                o_ref[rows, lanes] = _dot(p_t.astype(BF16), vext_ref[ext, lanes], _TN).astype(BF16)

    return pl.pallas_call(
        body, name=name, grid=(HEAD_PAIRS, d, nchunks),
        in_specs=[main[0], main[1], prev[1], main[2], prev[2], bias, stat], out_specs=tok,
        out_shape=jax.ShapeDtypeStruct((L, d * DIL_WIDTH), BF16),
        scratch_shapes=[pltpu.VMEM((chunk + BAND, LANES), BF16), pltpu.VMEM((chunk + BAND, LANES), BF16)],
        compiler_params=_params("parallel", "parallel", "parallel"),
    )(qkv_v, qkv_v, qkv_v, qkv_v, qkv_v, bias_t, lse_joint)


def _dil_bwd(qkv_v, do_v, bias_t, lse_joint, delta, d, name):
    L = qkv_v.shape[0]
    S = L * d
    _, chunk, nb, nchunks = _band_geometry(S, d)
    main, prev, bias, stat, tok = _band_in_specs(S, d)
    nblocks = L // BAND

    def nxt_row(i):
        return jnp.minimum((i + 1) * nb, nblocks - 1)

    q_next = pl.BlockSpec((BAND, LANES), lambda hp, r, i: (nxt_row(i), r * QKV_BLOCKS + DIL_Q_BLOCK + hp))
    do_next = pl.BlockSpec((BAND, LANES), lambda hp, r, i: (nxt_row(i), r * PAIR_BLOCKS + hp))
    stat_next = pl.BlockSpec((None, 2, BAND), lambda hp, r, i: (hp, 0, r * nblocks + nxt_row(i)))

    def body(q_ref, k_ref, kp_ref, v_ref, vp_ref, do_ref, b_ref, lse_ref, dl_ref,
             qn_ref, don_ref, lsen_ref, dln_ref,
             dq_ref, dk_ref, dv_ref, db_ref, kext_ref, vext_ref, dkext_ref, dvext_ref):
        r, i = pl.program_id(1), pl.program_id(2)
        first = i == 0
        has_next = i + 1 < nchunks
        tail = slice(BAND + chunk, 2 * BAND + chunk)
        kext_ref[0:BAND, :] = kp_ref[...]
        kext_ref[BAND:BAND + chunk, :] = k_ref[...]
        kext_ref[tail, :] = jnp.zeros((BAND, LANES), BF16)
        vext_ref[0:BAND, :] = vp_ref[...]
        vext_ref[BAND:BAND + chunk, :] = v_ref[...]
        vext_ref[tail, :] = jnp.zeros((BAND, LANES), BF16)
        dkext_ref[...] = jnp.zeros_like(dkext_ref)
        dvext_ref[...] = jnp.zeros_like(dvext_ref)

        @pl.when(jnp.logical_and(r == 0, i == 0))
        def _():
            db_ref[...] = jnp.zeros_like(db_ref)

        def block(a, qb, dob, lse_row, dl_row, ext, mask_rows):
            lanes = slice(a * HEAD_DIM, (a + 1) * HEAD_DIM)
            kb, vb = kext_ref[ext, lanes], vext_ref[ext, lanes]
            s = _dot(kb, qb, _NT) + b_ref[a]
            if mask_rows is not None:
                s = jnp.where(mask_rows, NEG, s)
            p_t = jnp.exp(s - lse_row)
            ds_t = p_t * (_dot(vb, dob, _NT) - dl_row)
            ds_b = ds_t.astype(BF16)
            dkext_ref[ext, lanes] += _dot(ds_b, qb, _NN)
            dvext_ref[ext, lanes] += _dot(p_t.astype(BF16), dob, _NN)
            return ds_t, ds_b, kb

        key = lax.broadcasted_iota(jnp.int32, (2 * BAND, BAND), 0)
        for a in range(2):
            lanes = slice(a * HEAD_DIM, (a + 1) * HEAD_DIM)
            db_acc = jnp.zeros((2 * BAND, BAND), F32)
            for b in range(nb):
                rows, ext = slice(b * BAND, (b + 1) * BAND), slice(b * BAND, (b + 2) * BAND)
                mask = jnp.logical_and(first, key < BAND) if b == 0 else None
                ds_t, ds_b, kb = block(a, q_ref[rows, lanes] * QK_SCALE, do_ref[rows, lanes],
                                       lse_ref[a:a + 1, rows], dl_ref[a:a + 1, rows], ext, mask)
                dq_ref[rows, lanes] = _dot(ds_b, kb, _TN) * QK_SCALE
                db_acc = db_acc + ds_t
            db_ref[a] += db_acc
            block(a, qn_ref[:, lanes] * QK_SCALE, don_ref[:, lanes], lsen_ref[a:a + 1, :], dln_ref[a:a + 1, :],
                  slice(chunk, chunk + 2 * BAND), jnp.logical_or(jnp.logical_not(has_next), key >= BAND))
        dk_ref[...] = dkext_ref[BAND:BAND + chunk, :]
        dv_ref[...] = dvext_ref[BAND:BAND + chunk, :]

    ext_rows = chunk + 2 * BAND
    return pl.pallas_call(
        body, name=name, grid=(HEAD_PAIRS, d, nchunks),
        in_specs=[main[0], main[1], prev[1], main[2], prev[2], tok, bias, stat, stat,
                  q_next, do_next, stat_next, stat_next],
        out_specs=[tok, tok, tok, bias],
        out_shape=[jax.ShapeDtypeStruct((L, d * DIL_WIDTH), F32)] * 3
                  + [jax.ShapeDtypeStruct((HEAD_PAIRS, 2, 2 * BAND, BAND), F32)],
        scratch_shapes=[pltpu.VMEM((ext_rows, LANES), BF16), pltpu.VMEM((ext_rows, LANES), BF16),
                        pltpu.VMEM((ext_rows, LANES), F32), pltpu.VMEM((ext_rows, LANES), F32)],
        compiler_params=_params("arbitrary", "arbitrary", "arbitrary"),
    )(qkv_v, qkv_v, qkv_v, qkv_v, qkv_v, do_v, bias_t, lse_joint, delta, qkv_v, do_v, lse_joint, delta)


def _lse_join(lse3, name):
    P, H, S = lse3.shape

    def body(l_ref, o_ref):
        a, b, c = l_ref[0], l_ref[1], l_ref[2]
        m = jnp.maximum(jnp.maximum(a, b), c)
        o_ref[...] = m + jnp.log(jnp.exp(a - m) + jnp.exp(b - m) + jnp.exp(c - m))

    return pl.pallas_call(body, name=name, out_shape=jax.ShapeDtypeStruct((H, S), F32))(lse3)


def _bucket_reduce(dbias_t, bucket_map_t, name):
    P, H = dbias_t.shape[:2]

    def body(db_ref, bk_ref, o_ref):
        p, h = pl.program_id(0), pl.program_id(1)

        @pl.when(jnp.logical_and(p == 0, h == 0))
        def _():
            o_ref[...] = jnp.zeros_like(o_ref)

        db, bk = db_ref[...], bk_ref[...]
        row = lax.broadcasted_iota(jnp.int32, (N_BUCKETS, LANES), 0)
        lane = lax.broadcasted_iota(jnp.int32, (N_BUCKETS, LANES), 1)

        def one(b, acc):
            val = jnp.sum(jnp.sum(jnp.where(bk == b, db, 0.0), axis=1, keepdims=True), axis=0, keepdims=True)
            return acc + jnp.where(jnp.logical_and(row == b, lane == h), val, 0.0)

        o_ref[...] += lax.fori_loop(0, N_BUCKETS, one, jnp.zeros((N_BUCKETS, LANES), F32))

    return pl.pallas_call(
        body, name=name, grid=(P, H),
        in_specs=[pl.BlockSpec((None, None, 2 * BAND, BAND), lambda p, h: (p, h, 0, 0)),
                  pl.BlockSpec((None, 2 * BAND, BAND), lambda p, h: (p, 0, 0))],
        out_specs=pl.BlockSpec((N_BUCKETS, LANES), lambda p, h: (0, 0)),
        out_shape=jax.ShapeDtypeStruct((N_BUCKETS, LANES), F32),
        compiler_params=_params("arbitrary", "arbitrary"),
    )(dbias_t, bucket_map_t)


def _mem_fwd(q, kv, name, tq=1024):
    S, W = q.shape
    N = kv.shape[0]
    pairs = W // LANES
    tq = _fit(S, tq)

    def body(q_ref, k_ref, v_ref, o_ref, lse_ref):
        for a in range(2):
            lanes = slice(a * HEAD_DIM, (a + 1) * HEAD_DIM)
            s = _dot(k_ref[:, lanes], q_ref[:, lanes] * QK_SCALE, _NT)
            m = jnp.max(s, axis=0, keepdims=True)
            e = jnp.exp(s - m)
            l = jnp.sum(e, axis=0, keepdims=True)
            o_ref[:, lanes] = _dot((e / l).astype(BF16), v_ref[:, lanes], _TN).astype(BF16)
            lse_ref[a:a + 1, :] = m + jnp.log(l)

    return pl.pallas_call(
        body, name=name, grid=(pairs, S // tq),
        in_specs=[pl.BlockSpec((tq, LANES), lambda hp, i: (i, hp)),
                  pl.BlockSpec((N, LANES), lambda hp, i: (0, hp)),
                  pl.BlockSpec((N, LANES), lambda hp, i: (0, pairs + hp))],
        out_specs=[pl.BlockSpec((tq, LANES), lambda hp, i: (i, hp)),
                   pl.BlockSpec((None, 2, tq), lambda hp, i: (hp, 0, i))],
        out_shape=[jax.ShapeDtypeStruct((S, W), BF16), jax.ShapeDtypeStruct((pairs, 2, S), F32)],
        compiler_params=_params("parallel", "parallel"),
    )(q, kv, kv)


def _mem_bwd(q, kv, do, lse, delta, name, tq=1024):
    S, W = q.shape
    N = kv.shape[0]
    pairs = W // LANES
    tq = _fit(S, tq)

    def body(q_ref, k_ref, v_ref, do_ref, lse_ref, dl_ref, dq_ref, dk_ref, dv_ref):
        i = pl.program_id(1)

        @pl.when(i == 0)
        def _():
            dk_ref[...] = jnp.zeros_like(dk_ref)
            dv_ref[...] = jnp.zeros_like(dv_ref)

        for a in range(2):
            lanes = slice(a * HEAD_DIM, (a + 1) * HEAD_DIM)
            qv, dov = q_ref[:, lanes] * QK_SCALE, do_ref[:, lanes]
            kv_, vv = k_ref[:, lanes], v_ref[:, lanes]
            p_t = jnp.exp(_dot(kv_, qv, _NT) - lse_ref[a:a + 1, :])
            ds_t = p_t * (_dot(vv, dov, _NT) - dl_ref[a:a + 1, :])
            ds_b = ds_t.astype(BF16)
            dq_ref[:, lanes] = (_dot(ds_b, kv_, _TN) * QK_SCALE).astype(BF16)
            dk_ref[:, lanes] += _dot(ds_b, qv, _NN)
            dv_ref[:, lanes] += _dot(p_t.astype(BF16), dov, _NN)

    qs = pl.BlockSpec((tq, LANES), lambda hp, i: (i, hp))
    stat = pl.BlockSpec((None, 2, tq), lambda hp, i: (hp, 0, i))
    acc = pl.BlockSpec((N, LANES), lambda hp, i: (0, hp))
    return pl.pallas_call(
        body, name=name, grid=(pairs, S // tq),
        in_specs=[qs, acc, pl.BlockSpec((N, LANES), lambda hp, i: (0, pairs + hp)), qs, stat, stat],
        out_specs=[qs, acc, acc],
        out_shape=[jax.ShapeDtypeStruct((S, W), BF16), jax.ShapeDtypeStruct((N, W), F32),
                   jax.ShapeDtypeStruct((N, W), F32)],
        compiler_params=_params("parallel", "arbitrary"),
    )(q, kv, kv, do, lse, delta)


def _head_rowdot(a, bs, name, tr=512):
    S, W = a.shape
    tr = _fit(S, tr)

    def body(*refs):
        tot = refs[1][...].astype(F32)
        for r in refs[2:-1]:
            tot = tot + r[...].astype(F32)
        prod = refs[0][...].astype(F32) * tot
        col = lax.broadcasted_iota(jnp.int32, (W, LANES), 0)
        lane = lax.broadcasted_iota(jnp.int32, (W, LANES), 1)
        sel = jnp.where(col // HEAD_DIM == lane, 1.0, 0.0).astype(F32)
        refs[-1][...] = lax.dot_general(prod, sel, (_NN, ((), ())), precision=lax.Precision.HIGHEST,
                                        preferred_element_type=F32)

    row = pl.BlockSpec((tr, W), lambda i: (i, 0))
    return pl.pallas_call(
        body, name=name, grid=(S // tr,), in_specs=[row] * (1 + len(bs)),
        out_specs=pl.BlockSpec((tr, LANES), lambda i: (i, 0)),
        out_shape=jax.ShapeDtypeStruct((S, LANES), F32),
        compiler_params=_params("parallel"),
    )(a, *bs)


def _sum_cast_cols(groups, out_dtype, name, tr=256):
    S, W = groups[0][0].shape
    tr = _fit(S, tr)
    flat = [t for g in groups for t in g]

    def body(*refs):
        o_ref = refs[-1]
        k = 0
        for gi, g in enumerate(groups):
            acc = refs[k][...].astype(F32)
            for r in refs[k + 1:k + len(g)]:
                acc = acc + r[...].astype(F32)
            o_ref[:, gi * W:(gi + 1) * W] = acc.astype(out_dtype)
            k += len(g)

    row = pl.BlockSpec((tr, W), lambda i: (i, 0))
    return pl.pallas_call(
        body, name=name, grid=(S // tr,), in_specs=[row] * len(flat),
        out_specs=pl.BlockSpec((tr, W * len(groups)), lambda i: (i, 0)),
        out_shape=jax.ShapeDtypeStruct((S, W * len(groups)), out_dtype),
        compiler_params=_params("parallel"),
    )(*flat)


FF_TILE = 256


def _ffn_up(h, w_gu, name, tm=512):
    S, D = h.shape
    F2 = w_gu.shape[1]
    tm = _fit(S, tm)

    def body(h_ref, w_ref, gu_ref, act_ref):
        gu = _dot(h_ref[...], w_ref[...], _NN)
        gu_ref[...] = gu
        g, u = gu[:, :FF_TILE], gu[:, FF_TILE:]
        act_ref[...] = (g * (1.0 / (1.0 + jnp.exp(-g))) * u).astype(BF16)

    return pl.pallas_call(
        body, name=name, grid=(S // tm, F2 // (2 * FF_TILE)),
        in_specs=[pl.BlockSpec((tm, D), lambda i, j: (i, 0)), pl.BlockSpec((D, 2 * FF_TILE), lambda i, j: (0, j))],
        out_specs=[pl.BlockSpec((tm, 2 * FF_TILE), lambda i, j: (i, j)),
                   pl.BlockSpec((tm, FF_TILE), lambda i, j: (i, j))],
        out_shape=[jax.ShapeDtypeStruct((S, F2), F32), jax.ShapeDtypeStruct((S, F2 // 2), BF16)],
        compiler_params=_params("parallel", "arbitrary"),
    )(h, w_gu)


def _ffn_dact(dy, w_down, gu, name, tm=512):
    S, D = dy.shape
    F2 = gu.shape[1]
    tm = _fit(S, tm)

    def body(dy_ref, w_ref, gu_ref, dgu_ref):
        dact = _dot(dy_ref[...], w_ref[...], _NT)
        gu_v = gu_ref[...]
        g, u = gu_v[:, :FF_TILE], gu_v[:, FF_TILE:]
        sig = 1.0 / (1.0 + jnp.exp(-g))
        silu = g * sig
        dgu_ref[:, :FF_TILE] = (dact * u * (sig + silu * (1.0 - sig))).astype(BF16)
        dgu_ref[:, FF_TILE:] = (dact * silu).astype(BF16)

    return pl.pallas_call(
        body, name=name, grid=(S // tm, F2 // (2 * FF_TILE)),
        in_specs=[pl.BlockSpec((tm, D), lambda i, j: (i, 0)), pl.BlockSpec((FF_TILE, D), lambda i, j: (j, 0)),
                  pl.BlockSpec((tm, 2 * FF_TILE), lambda i, j: (i, j))],
        out_specs=pl.BlockSpec((tm, 2 * FF_TILE), lambda i, j: (i, j)),
        out_shape=jax.ShapeDtypeStruct((S, F2), BF16),
        compiler_params=_params("parallel", "arbitrary"),
    )(dy, w_down, gu)


def _fit_rows(n, cap):
    if n <= cap:
        return n
    t = (cap // 8) * 8
    while t >= 8:
        if n % t == 0:
            return t
        t -= 8
    raise ValueError(f"no sublane-aligned tile for {n} under {cap}")


def _add_n(arrs, name, tr=512):
    R, C = arrs[0].shape
    tr = _fit_rows(R, tr)

    def body(*refs):
        acc = refs[0][...]
        for r in refs[1:-1]:
            acc = acc + r[...]
        refs[-1][...] = acc

    row = pl.BlockSpec((tr, C), lambda i: (i, 0))
    return pl.pallas_call(
        body, name=name, grid=(R // tr,), in_specs=[row] * len(arrs), out_specs=row,
        out_shape=jax.ShapeDtypeStruct((R, C), F32), compiler_params=_params("parallel"),
    )(*arrs)


def _adamw(w, g, m, v, name, tr=512):
    R, C = w.shape
    tr = _fit_rows(R, tr)
    c1 = 1.0 / (1.0 - ADAM_B1 ** ADAM_STEP)
    c2 = 1.0 / (1.0 - ADAM_B2 ** ADAM_STEP)

    def body(w_ref, g_ref, m_ref, v_ref, d_ref, nm_ref, nv_ref):
        gv = g_ref[...]
        nm = ADAM_B1 * m_ref[...] + (1.0 - ADAM_B1) * gv
        nv = ADAM_B2 * v_ref[...] + (1.0 - ADAM_B2) * (gv * gv)
        nm_ref[...] = nm
        nv_ref[...] = nv
        d_ref[...] = -ADAM_LR * ((nm * c1) / (jnp.sqrt(nv * c2) + ADAM_EPS) + ADAM_WD * w_ref[...])

    row = pl.BlockSpec((tr, C), lambda i: (i, 0))
    return pl.pallas_call(
        body, name=name, grid=(R // tr,), in_specs=[row] * 4, out_specs=[row] * 3,
        out_shape=[jax.ShapeDtypeStruct((R, C), F32)] * 3, compiler_params=_params("parallel"),
    )(w, g, m, v)


def _place():
    return lax.axis_index("x"), lax.axis_index("y"), lax.axis_index("c")


_ANY = pl.BlockSpec(memory_space=pl.ANY)


def _chip_all_gather(shard, name):
    R, C = shard.shape

    def body(x_ref, out_ref, send_sems, recv_sems, local_sem):
        x, y, c = _place()
        chips = [(1 - x, y), (x, 1 - y), (1 - x, 1 - y)]
        mine = pltpu.make_async_copy(x_ref, out_ref.at[2 * x + y], local_sem)
        mine.start()

        def copy(k, slot, to):
            return pltpu.make_async_remote_copy(
                src_ref=x_ref, dst_ref=out_ref.at[slot], send_sem=send_sems.at[k], recv_sem=recv_sems.at[k],
                device_id=to, device_id_type=MESH_IDS)

        sends = [copy(k, 2 * x + y, (cx, cy, c)) for k, (cx, cy) in enumerate(chips)]
        for cp in sends:
            cp.start()
        for k, (cx, cy) in enumerate(chips):
            copy(k, 2 * cx + cy, (cx, cy, c)).wait_recv()
        for cp in sends:
            cp.wait_send()
        mine.wait()

    return pl.pallas_call(
        body, name=name, in_specs=[_ANY], out_specs=_ANY,
        out_shape=jax.ShapeDtypeStruct((N_CHIPS, R, C), shard.dtype),
        scratch_shapes=[pltpu.SemaphoreType.DMA((3,)), pltpu.SemaphoreType.DMA((3,)), pltpu.SemaphoreType.DMA],
    )(shard)


def _sibling_exchange(buf, name):
    def body(x_ref, out_ref, send_sem, recv_sem):
        x, y, c = _place()
        cp = pltpu.make_async_remote_copy(
            src_ref=x_ref, dst_ref=out_ref, send_sem=send_sem, recv_sem=recv_sem,
            device_id=(x, y, 1 - c), device_id_type=MESH_IDS)
        cp.start()
        cp.wait()

    return pl.pallas_call(
        body, name=name, in_specs=[_ANY], out_specs=_ANY,
        out_shape=jax.ShapeDtypeStruct(buf.shape, buf.dtype),
        scratch_shapes=[pltpu.SemaphoreType.DMA, pltpu.SemaphoreType.DMA],
    )(buf)


def _chip_scatter(parts, name):
    _, R, C = parts.shape

    def body(p_ref, out_ref, send_sems, recv_sems):
        x, y, c = _place()
        chips = [(1 - x, y), (x, 1 - y), (1 - x, 1 - y)]

        def copy(k, slab, to):
            return pltpu.make_async_remote_copy(
                src_ref=p_ref.at[slab], dst_ref=out_ref.at[k], send_sem=send_sems.at[k], recv_sem=recv_sems.at[k],
                device_id=to, device_id_type=MESH_IDS)

        sends = [copy(k, 2 * cx + cy, (cx, cy, c)) for k, (cx, cy) in enumerate(chips)]
        for cp in sends:
            cp.start()
        for cp in sends:
            cp.wait_recv()
        for cp in sends:
            cp.wait_send()

    return pl.pallas_call(
        body, name=name, in_specs=[_ANY], out_specs=_ANY,
        out_shape=jax.ShapeDtypeStruct((3, R, C), parts.dtype),
        scratch_shapes=[pltpu.SemaphoreType.DMA((3,)), pltpu.SemaphoreType.DMA((3,))],
    )(parts)


def _all_to_all_small(vec, name):
    R, C = vec.shape

    def body(v_ref, out_ref, send_sems, recv_sems, local_sem):
        x, y, c = _place()
        me = 4 * x + 2 * y + c
        mine = pltpu.make_async_copy(v_ref, out_ref.at[me], local_sem)
        mine.start()
        flips = [(dx, dy, dc) for dx in (0, 1) for dy in (0, 1) for dc in (0, 1)][1:]

        def peer(f):
            return (x ^ f[0], y ^ f[1], c ^ f[2])

        def copy(k, slot, to):
            return pltpu.make_async_remote_copy(
                src_ref=v_ref, dst_ref=out_ref.at[slot], send_sem=send_sems.at[k], recv_sem=recv_sems.at[k],
                device_id=to, device_id_type=MESH_IDS)

        sends = [copy(k, me, peer(f)) for k, f in enumerate(flips)]
        for cp in sends:
            cp.start()
        for k, f in enumerate(flips):
            px, py, pc = peer(f)
            copy(k, 4 * px + 2 * py + pc, peer(f)).wait_recv()
        for cp in sends:
            cp.wait_send()
        mine.wait()

    return pl.pallas_call(
        body, name=name, in_specs=[_ANY], out_specs=_ANY,
        out_shape=jax.ShapeDtypeStruct((8, R, C), vec.dtype),
        scratch_shapes=[pltpu.SemaphoreType.DMA((7,)), pltpu.SemaphoreType.DMA((7,)), pltpu.SemaphoreType.DMA],
    )(vec)


def _to_heads(t, n):
    S = t.shape[0]
    return t.reshape(S, n, HEAD_DIM).transpose(1, 0, 2)


def _to_heads_t(t, n):
    S = t.shape[0]
    return t.T.reshape(n, HEAD_DIM, S)


def _from_heads_t(t):
    H, Dh, S = t.shape
    return t.reshape(H * Dh, S).T


def _rep(t):
    return jnp.broadcast_to(t[..., None], t.shape + (LANES,))


def _t5_bucket(dist):
    max_exact = N_BUCKETS // 2
    d = np.maximum(dist, 1).astype(np.float32)
    large = max_exact + (np.log(d / max_exact) / np.log(MAX_DISTANCE / max_exact)
                         * (N_BUCKETS - max_exact)).astype(np.int32)
    large = np.minimum(large, N_BUCKETS - 1)
    return np.where(dist < max_exact, dist, large).astype(np.int32)


def _band_tables():
    qi = np.arange(BAND)[:, None]
    kj = np.arange(2 * BAND)[None, :]
    sub = qi + BAND - kj
    band = (sub >= 0) & (sub <= BAND)
    out = []
    for d in DILATIONS:
        bucket = _t5_bucket(np.clip(sub, 0, BAND) * d)
        out.append(np.where(band, bucket, -1).astype(np.int32))
    return np.stack(out)


_PACK = (("w_in", 770), ("w_out", 256), ("w_xq", 64), ("w_xk", 64), ("w_xv", 64), ("w_xo", 64),
         ("w_gate", 704), ("w_up", 704), ("w_down", 704))


def _pack(shards):
    rows = [shards[n].reshape(-1, PACK_COLS) for n, _ in _PACK]
    total = sum(r.shape[0] for r in rows)
    pad = (-total) % 16
    if pad:
        rows.append(jnp.zeros((pad, PACK_COLS), rows[0].dtype))
    return jnp.concatenate(rows, axis=0)


def _unpack(pack, shapes):
    out, r = {}, 0
    for n, _ in _PACK:
        cnt = int(np.prod(shapes[n])) // PACK_COLS
        out[n] = pack[r:r + cnt].reshape(shapes[n])
        r += cnt
    return out


_COL_SHARDED = ("w_in", "w_xo", "w_gate", "w_up")


def _full_weight(gathered, name):
    return jnp.concatenate(gathered, axis=1 if name in _COL_SHARDED else 0)


def _split_weight(full, name):
    return jnp.split(full, N_CHIPS, axis=1 if name in _COL_SHARDED else 0)


_SMALL = ("g_mix_pre", "g_mix_post", "g_xattn_pre", "g_mem", "g_xattn_post", "g_ffn_pre", "g_ffn_post")


def _pack_small(vals):
    D = vals["g_mix_pre"].shape[1]
    rows = [vals[n].reshape(1, D) for n in _SMALL]
    misc = jnp.concatenate([vals["b_f"].reshape(-1), vals["rel_bias"].reshape(-1)])
    rows.append(jnp.pad(misc, (0, D - misc.shape[0])).reshape(1, D))
    rows.append(jnp.zeros((16 - len(rows), D), F32))
    return jnp.concatenate(rows, axis=0)


def _unpack_small(pack):
    out = {n: pack[i:i + 1] for i, n in enumerate(_SMALL)}
    out["b_f"] = pack[7, 0:N_FOX_HEADS].reshape(1, N_FOX_HEADS)
    out["rel_bias"] = pack[7, N_FOX_HEADS:N_FOX_HEADS + N_BUCKETS * N_DIL_HEADS].reshape(N_BUCKETS, N_DIL_HEADS)
    return out


def kernel(x, mem, g_mix_pre, w_in, b_f, rel_bias, w_out, g_mix_post, g_xattn_pre, g_mem, w_xq, w_xk, w_xv, w_xo, g_xattn_post, g_ffn_pre, w_gate, w_up, w_down, g_ffn_post, loss_target, m_g_mix_pre, m_w_in, m_b_f, m_rel_bias, m_w_out, m_g_mix_post, m_g_xattn_pre, m_g_mem, m_w_xq, m_w_xk, m_w_xv, m_w_xo, m_g_xattn_post, m_g_ffn_pre, m_w_gate, m_w_up, m_w_down, m_g_ffn_post, v_g_mix_pre, v_w_in, v_b_f, v_rel_bias, v_w_out, v_g_mix_post, v_g_xattn_pre, v_g_mem, v_w_xq, v_w_xk, v_w_xv, v_w_xo, v_g_xattn_post, v_g_ffn_pre, v_w_gate, v_w_up, v_w_down, v_g_ffn_post):
    args = dict(locals())
    big = [n for n, _ in _PACK]
    names = ["g_mix_pre", "w_in", "b_f", "rel_bias", "w_out", "g_mix_post", "g_xattn_pre", "g_mem", "w_xq",
             "w_xk", "w_xv", "w_xo", "g_xattn_post", "g_ffn_pre", "w_gate", "w_up", "w_down", "g_ffn_post"]
    xs = x[0]
    S, D = xs.shape
    assert S % (BAND * DILATIONS[-1]) == 0
    shard_shapes = {n: args[n].shape[1:] for n in big}
    my_x, my_y, my_c = lax.axis_index("x"), lax.axis_index("y"), lax.axis_index("c")

    w_pack = _pack({n: args[n][0] for n in big})
    gathered = _chip_all_gather(w_pack.astype(BF16), "weights_all_gather")
    per_chip = [_unpack(gathered[j], shard_shapes) for j in range(N_CHIPS)]
    W = {n: _full_weight([pc[n] for pc in per_chip], n) for n in big}
    w_fox, w_fg, w_dil = (W["w_in"][:, :3 * FOX_WIDTH], W["w_in"][:, 3 * FOX_WIDTH:3 * FOX_WIDTH + N_FOX_HEADS],
                          W["w_in"][:, 3 * FOX_WIDTH + N_FOX_HEADS:])
    w_qkv = jnp.concatenate([w_fox, w_dil], axis=1)
    w_fg_pad = jnp.pad(w_fg, ((0, 0), (0, LANES - N_FOX_HEADS)))
    F = W["w_gate"].shape[1]
    nft = F // FF_TILE
    w_gu = jnp.stack([W["w_gate"].reshape(D, nft, FF_TILE), W["w_up"].reshape(D, nft, FF_TILE)],
                     axis=2).reshape(D, 2 * F)

    h1 = _rms_fwd(xs, g_mix_pre, "rms_mix_pre")
    qkv = _mm(h1, w_qkv, "nn", BF16, "proj_qkv")
    fg = _mm(h1, w_fg_pad, "nn", F32, "proj_gate")
    fg_t = fg[:, :N_FOX_HEADS].T
    b_col = b_f.reshape(N_FOX_HEADS, 1)
    c_t = _forget_fwd(fg_t, b_col, "forget_cumsum")
    c_row = c_t.reshape(N_FOX_HEADS, 1, S)
    c_rep = _rep(c_t)
    fq_s, fk_s, fv_s = (qkv[:, i * FOX_WIDTH:(i + 1) * FOX_WIDTH] for i in range(3))
    fqt, fkt, fvt = (_to_heads_t(t, N_FOX_HEADS) for t in (fq_s, fk_s, fv_s))
    fk, fv = _to_heads(fk_s, N_FOX_HEADS), _to_heads(fv_s, N_FOX_HEADS)
    o_fox_t, lse_fox = _fox_fwd(fqt, fk, fvt, c_row, c_rep, "fox_fwd")

    bucket_map = _band_tables()
    onehot = (jnp.asarray(bucket_map)[..., None] == jnp.arange(N_BUCKETS)).astype(F32)
    bias_tab = jnp.einsum("pqkb,bh->phkq", onehot, rel_bias, precision=lax.Precision.HIGHEST)
    bias_tab = jnp.where(jnp.asarray(bucket_map.transpose(0, 2, 1) >= 0)[:, None], bias_tab, NEG)
    bias_t = bias_tab.reshape(3, HEAD_PAIRS, 2, 2 * BAND, BAND)
    views = [qkv.reshape(S // d, d * qkv.shape[1]) for d in DILATIONS]

    def to_tok(stat, d):
        return stat.reshape(N_DIL_HEADS, d, S // d).swapaxes(1, 2).reshape(N_DIL_HEADS, S)

    def to_perm(stat, d):
        return stat.reshape(N_DIL_HEADS, S // d, d).swapaxes(1, 2).reshape(HEAD_PAIRS, 2, S)

    lse_tok = jnp.stack([to_tok(_dil_lse(views[p], bias_t[p], d, f"dilated_lse_{d}"), d)
                         for p, d in enumerate(DILATIONS)])
    lse_joint = _lse_join(lse_tok, "dilated_lse_join")
    lse_perm = [to_perm(lse_joint, d) for d in DILATIONS]
    o_dil = [_dil_out(views[p], bias_t[p], lse_perm[p], d, f"dilated_out_{d}").reshape(S, DIL_WIDTH)
             for p, d in enumerate(DILATIONS)]
    o_cat = _sum_cast_cols([[_from_heads_t(o_fox_t)]] + [[o] for o in o_dil], BF16, "mixer_out_cat")
    w_out_b = W["w_out"]
    w_out_cat = jnp.concatenate([w_out_b[:FOX_WIDTH]] + [w_out_b[FOX_WIDTH:]] * 3, axis=0)
    a = _mm(o_cat, w_out_cat, "nn", F32, "proj_out")
    x1, h2 = _resid_norm(xs, a, g_mix_post, g_xattn_pre, "resid_mix")

    hm = _rms_fwd(mem[0], g_mem, "rms_mem")
    q2 = _mm(h2, W["w_xq"], "nn", BF16, "xattn_q")
    w_xkv = jnp.concatenate([W["w_xk"], W["w_xv"]], axis=1)
    kvm = _mm(hm, w_xkv, "nn", BF16, "xattn_kv")
    MW = N_MEM_HEADS * HEAD_DIM
    oc, lse_mem = _mem_fwd(q2, kvm, "xattn_fwd")
    y2 = _mm(oc, W["w_xo"], "nn", F32, "xattn_o")
    x2, h3 = _resid_norm(x1, y2, g_xattn_post, g_ffn_pre, "resid_xattn")

    gu, act = _ffn_up(h3, w_gu, "ffn_up")
    y3 = _mm(act, W["w_down"], "nn", F32, "ffn_down", tk=1536)
    dx3, loss_tile = _final_loss(x2, y3, g_ffn_post, loss_target[0], "final_loss")

    grads = {}
    small = {}
    dy3, dy3_b, dg = _rms_bwd(y3, g_ffn_post, dx3, None, "bwd_norm_ffn_post")
    small["g_ffn_post"] = dg[0:1]
    grads["w_down"] = _mm(act, dy3_b, "tn", F32, "grad_w_down")
    dgu = _ffn_dact(dy3_b, W["w_down"], gu, "ffn_dact")
    dw_gu = _mm(h3, dgu, "tn", F32, "grad_w_gu").reshape(D, nft, 2, FF_TILE)
    grads["w_gate"], grads["w_up"] = dw_gu[:, :, 0].reshape(D, F), dw_gu[:, :, 1].reshape(D, F)
    dh3 = _mm(dgu, w_gu, "nt", F32, "bwd_ffn_in", tk=1024)
    dx2, _, dg = _rms_bwd(x2, g_ffn_pre, dh3, dx3, "bwd_norm_ffn_pre")
    small["g_ffn_pre"] = dg[0:1]

    dy2, dy2_b, dg = _rms_bwd(y2, g_xattn_post, dx2, None, "bwd_norm_xattn_post")
    small["g_xattn_post"] = dg[0:1]
    grads["w_xo"] = _mm(oc, dy2_b, "tn", F32, "grad_w_xo")
    doc = _mm(dy2_b, W["w_xo"], "nt", BF16, "bwd_xattn_o")
    delta_mem = _head_rowdot(doc, [oc], "xattn_delta")[:, :N_MEM_HEADS].T.reshape(N_MEM_HEADS // 2, 2, S)
    dq2, dkm, dvm = _mem_bwd(q2, kvm, doc, lse_mem, delta_mem, "xattn_bwd")
    dkvm = jnp.concatenate([dkm, dvm], axis=1).astype(BF16)
    grads["w_xq"] = _mm(h2, dq2, "tn", F32, "grad_w_xq")
    dw_xkv = _mm(hm, dkvm, "tn", F32, "grad_w_xkv")
    grads["w_xk"], grads["w_xv"] = dw_xkv[:, :MW], dw_xkv[:, MW:]
    dhm = _mm(dkvm, w_xkv, "nt", F32, "bwd_xattn_kv")
    _, _, dg = _rms_bwd(mem[0], g_mem, dhm, None, "bwd_norm_mem")
    small["g_mem"] = dg[0:1]
    dh2 = _mm(dq2, W["w_xq"], "nt", F32, "bwd_xattn_q")
    dx1, _, dg = _rms_bwd(x1, g_xattn_pre, dh2, dx2, "bwd_norm_xattn_pre")
    small["g_xattn_pre"] = dg[0:1]

    da, da_b, dg = _rms_bwd(a, g_mix_post, dx1, None, "bwd_norm_mix_post")
    small["g_mix_post"] = dg[0:1]
    dw_out_cat = _mm(o_cat, da_b, "tn", F32, "grad_w_out")
    dw_out_dil = _add_n([dw_out_cat[FOX_WIDTH + p * DIL_WIDTH:FOX_WIDTH + (p + 1) * DIL_WIDTH] for p in range(3)],
                        "grad_w_out_dil")
    grads["w_out"] = jnp.concatenate([dw_out_cat[:FOX_WIDTH], dw_out_dil], axis=0)
    do = _mm(da_b, w_out_b, "nt", BF16, "bwd_proj_out")
    do_fox, do_dil = do[:, :FOX_WIDTH], do[:, FOX_WIDTH:]

    delta_fox = _head_rowdot(do_fox, [o_cat[:, :FOX_WIDTH]], "fox_delta")[:, :N_FOX_HEADS].T
    dqf, dkf, dvf, dcs, drs = _fox_bwd(fqt, fk, fkt, fv, _to_heads_t(do_fox, N_FOX_HEADS), c_rep,
                                       lse_fox, delta_fox.reshape(N_FOX_HEADS, 1, S), "fox_bwd")
    dfg_t, db_f = _forget_bwd(fg_t, b_col, drs[:, 0], dcs[..., 0], "forget_bwd")

    delta_dil = _head_rowdot(do_dil, o_dil, "dilated_delta")[:, :N_DIL_HEADS].T
    dil_grads = [_dil_bwd(views[p], do_dil.reshape(S // d, d * DIL_WIDTH), bias_t[p], lse_perm[p],
                          to_perm(delta_dil, d), d, f"dilated_bwd_{d}") for p, d in enumerate(DILATIONS)]
    dbias_t = jnp.stack([g[3].reshape(N_DIL_HEADS, 2 * BAND, BAND) for g in dil_grads])
    d_rel = _bucket_reduce(dbias_t, jnp.asarray(bucket_map.transpose(0, 2, 1)), "rel_bias_grad")[:, :N_DIL_HEADS]
    dqkv = _sum_cast_cols([[_from_heads_t(dqf)], [_from_heads_t(dkf)], [_from_heads_t(dvf)]]
                          + [[g[j].reshape(S, DIL_WIDTH) for g in dil_grads] for j in range(3)],
                          BF16, "dqkv_assemble")
    dfg_pad = jnp.pad(dfg_t.T, ((0, 0), (0, LANES - N_FOX_HEADS))).astype(BF16)
    dw_qkv = _mm(h1, dqkv, "tn", F32, "grad_w_qkv")
    dw_fg = _mm(h1, dfg_pad, "tn", F32, "grad_w_gate_cols")[:, :N_FOX_HEADS]
    grads["w_in"] = jnp.concatenate([dw_qkv[:, :3 * FOX_WIDTH], dw_fg, dw_qkv[:, 3 * FOX_WIDTH:]], axis=1)
    dcat = jnp.concatenate([dqkv, dfg_pad], axis=1)
    w_cat = jnp.concatenate([w_qkv, w_fg_pad], axis=1)
    dh1 = _mm(dcat, w_cat, "nt", F32, "bwd_proj_in", tk=640)
    grad_x, _, dg = _rms_bwd(xs, g_mix_pre, dh1, dx1, "bwd_norm_mix_pre")
    small["g_mix_pre"] = dg[0:1]
    small["b_f"] = db_f[:, 0].reshape(1, N_FOX_HEADS)
    small["rel_bias"] = d_rel

    split = {n: _split_weight(grads[n], n) for n in big}
    parts = jnp.stack([_pack({n: split[n][j] for n in big}) for j in range(N_CHIPS)])
    R = parts.shape[1]
    half = R // 2
    keep = lax.dynamic_slice_in_dim(parts, my_c * half, half, axis=1)
    give = lax.dynamic_slice_in_dim(parts, (1 - my_c) * half, half, axis=1)
    got = _sibling_exchange(give, "grads_to_sibling")
    chip_sum = _add_n([keep.reshape(-1, PACK_COLS), got.reshape(-1, PACK_COLS)], "grads_add_sibling")
    chip_sum = chip_sum.reshape(N_CHIPS, half, PACK_COLS)
    my_chip = 2 * my_x + my_y
    from_chips = _chip_scatter(chip_sum, "grads_to_chips")
    own = lax.dynamic_index_in_dim(chip_sum, my_chip, axis=0, keepdims=False)
    g_half = _add_n([own, from_chips[0], from_chips[1], from_chips[2]], "grads_add_chips")
    other_half = _sibling_exchange(g_half, "grads_share_sibling")
    g_pack = jnp.where(my_c == 0, jnp.concatenate([g_half, other_half]), jnp.concatenate([other_half, g_half]))

    small_pack = _pack_small(small)
    small_pack = small_pack.at[8, 0].set(loss_tile[0, 0])
    everyone = _all_to_all_small(small_pack, "small_all_gather")
    small_sum = _add_n([everyone[i] for i in range(8)], "small_sum")
    loss = small_sum[8, 0]
    g_small = _unpack_small(small_sum)

    m_pack = _pack({n: args["m_" + n][0] for n in big})
    v_pack = _pack({n: args["v_" + n][0] for n in big})
    d_pack, nm_pack, nv_pack = _adamw(w_pack, g_pack, m_pack, v_pack, "adamw_big")
    outs = {"grad": _unpack(g_pack, shard_shapes), "delta": _unpack(d_pack, shard_shapes),
            "new_m": _unpack(nm_pack, shard_shapes), "new_v": _unpack(nv_pack, shard_shapes)}
    sw = _pack_small({n: args[n] for n in _SMALL + ("b_f", "rel_bias")})
    sm = _pack_small({n: args["m_" + n] for n in _SMALL + ("b_f", "rel_bias")})
    sv = _pack_small({n: args["v_" + n] for n in _SMALL + ("b_f", "rel_bias")})
    sd, snm, snv = _adamw(sw, small_sum.at[8, 0].set(0.0), sm, sv, "adamw_small")
    souts = {"grad": g_small, "delta": _unpack_small(sd), "new_m": _unpack_small(snm), "new_v": _unpack_small(snv)}

    def leaf(kind, n):
        if n in souts[kind]:
            return souts[kind][n].reshape(args[n].shape)
        return outs[kind][n].reshape(args[n].shape)

    result = [loss, grad_x.reshape(x.shape)]
    for kind in ("grad", "delta", "new_m", "new_v"):
        result += [leaf(kind, n) for n in names]
    return tuple(result)
```

```python
import numpy as np
import jax
import jax.numpy as jnp
from jax import lax
from jax.experimental import pallas as pl
from jax.experimental.pallas import tpu as pltpu

F32 = jnp.float32
BF16 = jnp.bfloat16
MESH_IDS = pl.DeviceIdType.MESH

LANES = 128
HEAD_DIM = 64
N_FOX_HEADS = 8
N_DIL_HEADS = 8
N_MEM_HEADS = 4
FOX_WIDTH = N_FOX_HEADS * HEAD_DIM
DIL_WIDTH = N_DIL_HEADS * HEAD_DIM
DILATIONS = (1, 4, 16)
BAND = 128
BAND_CHUNK_MAX = 8 * BAND
N_BUCKETS = 32
MAX_DISTANCE = 2048
QK_SCALE = HEAD_DIM ** -0.5
RMS_EPS = 1e-6
NEG = -1e30
VMEM_LIMIT = 56 << 20

ADAM_LR = 0.001
ADAM_B1 = 0.9
ADAM_B2 = 0.999
ADAM_EPS = 1e-08
ADAM_WD = 0.01
ADAM_STEP = 10

N_CHIPS = 4
PACK_COLS = 1024


def _params(*sem):
    return pltpu.CompilerParams(dimension_semantics=sem, vmem_limit_bytes=VMEM_LIMIT)


def _fit(n, cap):
    if n <= cap:
        return n
    t = (cap // LANES) * LANES
    while t >= LANES:
        if n % t == 0:
            return t
        t -= LANES
    raise ValueError(f"no lane-aligned tile for {n} under {cap}")


def _dot(a, b, dims):
    return lax.dot_general(a, b, (dims, ((), ())), preferred_element_type=F32)


_NN = ((1,), (0,))
_NT = ((1,), (1,))
_TN = ((0,), (0,))


def _mm(a, b, mode, out_dtype, name, tm=1024, tn=1024, tk=512):
    if mode == "nn":
        (M, K), N = a.shape, b.shape[1]
    elif mode == "nt":
        (M, K), N = a.shape, b.shape[0]
    else:
        (K, M), N = a.shape, b.shape[1]
    tm, tn, tk = _fit(M, tm), _fit(N, tn), _fit(K, tk)
    nk = K // tk
    if mode == "tn":
        a_spec = pl.BlockSpec((tk, tm), lambda i, j, k: (k, i))
    else:
        a_spec = pl.BlockSpec((tm, tk), lambda i, j, k: (i, k))
    if mode == "nt":
        b_spec = pl.BlockSpec((tn, tk), lambda i, j, k: (j, k))
    else:
        b_spec = pl.BlockSpec((tk, tn), lambda i, j, k: (k, j))
    dims = {"nn": _NN, "nt": _NT, "tn": _TN}[mode]

    def body(a_ref, b_ref, o_ref, acc_ref):
        k = pl.program_id(2)

        @pl.when(k == 0)
        def _():
            acc_ref[...] = jnp.zeros_like(acc_ref)

        acc_ref[...] += _dot(a_ref[...].astype(BF16), b_ref[...].astype(BF16), dims)

        @pl.when(k == nk - 1)
        def _():
            o_ref[...] = acc_ref[...].astype(o_ref.dtype)

    return pl.pallas_call(
        body, name=name, grid=(M // tm, N // tn, nk),
        in_specs=[a_spec, b_spec],
        out_specs=pl.BlockSpec((tm, tn), lambda i, j, k: (i, j)),
        out_shape=jax.ShapeDtypeStruct((M, N), out_dtype),
        scratch_shapes=[pltpu.VMEM((tm, tn), F32)],
        compiler_params=_params("parallel", "parallel", "arbitrary"),
    )(a, b)


def _rms_rows(x):
    return lax.rsqrt(jnp.mean(x * x, axis=-1, keepdims=True) + RMS_EPS)


def _rms_fwd(x, g, name, tr=512):
    S, D = x.shape
    tr = _fit(S, tr)

    def body(x_ref, g_ref, h_ref):
        xv = x_ref[...]
        h_ref[...] = (xv * _rms_rows(xv) * g_ref[...]).astype(BF16)

    return pl.pallas_call(
        body, name=name, grid=(S // tr,),
        in_specs=[pl.BlockSpec((tr, D), lambda i: (i, 0)), pl.BlockSpec((1, D), lambda i: (0, 0))],
        out_specs=pl.BlockSpec((tr, D), lambda i: (i, 0)),
        out_shape=jax.ShapeDtypeStruct((S, D), BF16),
        compiler_params=_params("parallel"),
    )(x, g)


def _resid_norm(xres, y, g_post, g_next, name, tr=512):
    S, D = xres.shape
    tr = _fit(S, tr)

    def body(x_ref, y_ref, gp_ref, gn_ref, xn_ref, h_ref):
        yv = y_ref[...]
        xn = x_ref[...] + yv * _rms_rows(yv) * gp_ref[...]
        xn_ref[...] = xn
        h_ref[...] = (xn * _rms_rows(xn) * gn_ref[...]).astype(BF16)

    row = pl.BlockSpec((tr, D), lambda i: (i, 0))
    vec = pl.BlockSpec((1, D), lambda i: (0, 0))
    return pl.pallas_call(
        body, name=name, grid=(S // tr,),
        in_specs=[row, row, vec, vec], out_specs=[row, row],
        out_shape=[jax.ShapeDtypeStruct((S, D), F32), jax.ShapeDtypeStruct((S, D), BF16)],
        compiler_params=_params("parallel"),
    )(xres, y, g_post, g_next)


def _final_loss(xres, y, g_post, target, name, tr=512):
    S, D = xres.shape
    tr = _fit(S, tr)

    def body(x_ref, y_ref, gp_ref, t_ref, d_ref, loss_ref):
        i = pl.program_id(0)
        yv = y_ref[...]
        err = x_ref[...] + yv * _rms_rows(yv) * gp_ref[...] - t_ref[...]
        d_ref[...] = err * (1.0 / D)

        @pl.when(i == 0)
        def _():
            loss_ref[...] = jnp.zeros_like(loss_ref)

        part = jnp.sum(jnp.sum(err * err, axis=1, keepdims=True), axis=0, keepdims=True)
        loss_ref[...] += jnp.broadcast_to(part * (0.5 / D), loss_ref.shape)

    row = pl.BlockSpec((tr, D), lambda i: (i, 0))
    vec = pl.BlockSpec((1, D), lambda i: (0, 0))
    return pl.pallas_call(
        body, name=name, grid=(S // tr,),
        in_specs=[row, row, vec, row],
        out_specs=[row, pl.BlockSpec((8, LANES), lambda i: (0, 0))],
        out_shape=[jax.ShapeDtypeStruct((S, D), F32), jax.ShapeDtypeStruct((8, LANES), F32)],
        compiler_params=_params("arbitrary"),
    )(xres, y, g_post, target)


def _rms_bwd(xin, g, dy, dres, name, want=("f32", "bf16"), tr=512):
    S, D = xin.shape
    tr = _fit(S, tr)
    has_res = dres is not None

    def body(*refs):
        refs = list(refs)
        dg_ref = refs.pop()
        dxb_ref = refs.pop() if "bf16" in want else None
        dx_ref = refs.pop() if "f32" in want else None
        dr_ref = refs.pop() if has_res else None
        x_ref, g_ref, dy_ref = refs
        i = pl.program_id(0)
        xv = x_ref[...]
        dyv = dy_ref[...].astype(F32)
        xhat = xv * _rms_rows(xv)
        dxhat = dyv * g_ref[...]
        r = _rms_rows(xv)
        dx = r * (dxhat - xhat * jnp.mean(dxhat * xhat, axis=-1, keepdims=True))
        if has_res:
            dx = dx + dr_ref[...]
        if dx_ref is not None:
            dx_ref[...] = dx
        if dxb_ref is not None:
            dxb_ref[...] = dx.astype(BF16)

        @pl.when(i == 0)
        def _():
            dg_ref[...] = jnp.zeros_like(dg_ref)

        dg_ref[...] += jnp.broadcast_to(jnp.sum(dyv * xhat, axis=0, keepdims=True), dg_ref.shape)

    row = pl.BlockSpec((tr, D), lambda i: (i, 0))
    vec = pl.BlockSpec((1, D), lambda i: (0, 0))
    acc = pl.BlockSpec((8, D), lambda i: (0, 0))
    ins = [xin, g, dy] + ([dres] if has_res else [])
    dtypes = [dt for key, dt in (("f32", F32), ("bf16", BF16)) if key in want]
    outs = pl.pallas_call(
        body, name=name, grid=(S // tr,),
        in_specs=[row, vec, row] + ([row] if has_res else []),
        out_specs=[row] * len(dtypes) + [acc],
        out_shape=[jax.ShapeDtypeStruct((S, D), dt) for dt in dtypes] + [jax.ShapeDtypeStruct((8, D), F32)],
        compiler_params=_params("arbitrary"),
    )(*ins)
    by_key = dict(zip([key for key in ("f32", "bf16") if key in want], outs[:-1]))
    return by_key.get("f32"), by_key.get("bf16"), outs[-1]


def _tri(n, upper):
    r = lax.broadcasted_iota(jnp.int32, (n, n), 0)
    c = lax.broadcasted_iota(jnp.int32, (n, n), 1)
    return jnp.where((r <= c) if upper else (r >= c), 1.0, 0.0).astype(F32)


def _forget_fwd(fg_t, b_col, name, ts=512):
    H, S = fg_t.shape
    ts = _fit(S, ts)

    def body(f_ref, b_ref, c_ref, carry_ref):
        i = pl.program_id(0)

        @pl.when(i == 0)
        def _():
            carry_ref[...] = jnp.zeros_like(carry_ref)

        z = f_ref[...] + b_ref[...]
        logf = jnp.minimum(z, 0.0) - jnp.log(1.0 + jnp.exp(-jnp.abs(z)))
        run = lax.dot_general(logf, _tri(ts, True), (_NN, ((), ())), precision=lax.Precision.HIGHEST,
                              preferred_element_type=F32) + carry_ref[:, 0:1]
        c_ref[...] = run
        carry_ref[...] = jnp.broadcast_to(
            carry_ref[:, 0:1] + jnp.sum(logf, axis=1, keepdims=True), carry_ref.shape)

    return pl.pallas_call(
        body, name=name, grid=(S // ts,),
        in_specs=[pl.BlockSpec((H, ts), lambda i: (0, i)), pl.BlockSpec((H, 1), lambda i: (0, 0))],
        out_specs=pl.BlockSpec((H, ts), lambda i: (0, i)),
        out_shape=jax.ShapeDtypeStruct((H, S), F32),
        scratch_shapes=[pltpu.VMEM((H, LANES), F32)],
        compiler_params=_params("arbitrary"),
    )(fg_t, b_col)


def _forget_bwd(fg_t, b_col, dc_plus, dc_minus, name, ts=512):
    H, S = fg_t.shape
    ts = _fit(S, ts)
    nb = S // ts

    def body(f_ref, b_ref, dcp_ref, dcm_ref, df_ref, db_ref, carry_ref):
        i = pl.program_id(0)

        @pl.when(i == 0)
        def _():
            carry_ref[...] = jnp.zeros_like(carry_ref)
            db_ref[...] = jnp.zeros_like(db_ref)

        dc = dcp_ref[...] - dcm_ref[...]
        suffix = lax.dot_general(dc, _tri(ts, False), (_NN, ((), ())), precision=lax.Precision.HIGHEST,
                                 preferred_element_type=F32) + carry_ref[:, 0:1]
        z = f_ref[...] + b_ref[...]
        sig_neg = 1.0 / (1.0 + jnp.exp(z))
        df = suffix * sig_neg
        df_ref[...] = df
        carry_ref[...] = jnp.broadcast_to(
            carry_ref[:, 0:1] + jnp.sum(dc, axis=1, keepdims=True), carry_ref.shape)
        db_ref[...] += jnp.broadcast_to(jnp.sum(df, axis=1, keepdims=True), db_ref.shape)

    rev = pl.BlockSpec((H, ts), lambda i: (0, nb - 1 - i))
    return pl.pallas_call(
        body, name=name, grid=(nb,),
        in_specs=[rev, pl.BlockSpec((H, 1), lambda i: (0, 0)), rev, rev],
        out_specs=[rev, pl.BlockSpec((H, LANES), lambda i: (0, 0))],
        out_shape=[jax.ShapeDtypeStruct((H, S), F32), jax.ShapeDtypeStruct((H, LANES), F32)],
        scratch_shapes=[pltpu.VMEM((H, LANES), F32)],
        compiler_params=_params("arbitrary"),
    )(fg_t, b_col, dc_plus, dc_minus)


def _tile_lanes(x, n):
    return x if n == LANES else jnp.tile(x, (1, n // LANES))


def _fox_fwd(qt, k, vt, c_row, c_rep, name, tq=512, tk=1024):
    H, Dh, S = qt.shape
    tk = _fit(S, tk)
    tq = _fit(tk, tq)
    ratio = tk // tq

    def body(qt_ref, k_ref, vt_ref, c_ref, crep_ref, o_ref, lse_ref, m_ref, l_ref, acc_ref,
             sa_ref, sb_ref, ta_ref, tb_ref):
        i = pl.program_id(1)
        qv = qt_ref[...] * QK_SCALE
        cq0 = c_ref[:, pl.ds(pl.multiple_of(i * tq, LANES), LANES)][:, 0:1]
        m_ref[...] = jnp.full_like(m_ref, NEG)
        l_ref[...] = jnp.zeros_like(l_ref)
        acc_ref[...] = jnp.zeros_like(acc_ref)
        n = i // ratio
        q_off = (i - n * ratio) * tq

        def scores(j, s_ref, t_ref, diagonal):
            off = pl.multiple_of(j * tk, LANES)
            s = _dot(k_ref[pl.ds(off, tk), :], qv, _NN) + _tile_lanes(cq0 - crep_ref[pl.ds(off, tk), :], tq)
            if diagonal:
                key = lax.broadcasted_iota(jnp.int32, (tk, tq), 0)
                qry = lax.broadcasted_iota(jnp.int32, (tk, tq), 1) + q_off
                s = jnp.where(key <= qry, s, NEG)
            s_ref[...] = s
            t_ref[...] = jnp.max(s, axis=0, keepdims=True)

        def absorb(j, s_ref, t_ref):
            off = pl.multiple_of(j * tk, LANES)
            m_old = m_ref[...]
            m_new = jnp.maximum(m_old, t_ref[...])
            p = jnp.exp(s_ref[...] - m_new)
            alpha = jnp.exp(m_old - m_new)
            l_ref[...] = alpha * l_ref[...] + jnp.sum(p, axis=0, keepdims=True)
            acc_ref[...] = alpha * acc_ref[...] + _dot(vt_ref[:, pl.ds(off, tk)], p.astype(BF16), _NN)
            m_ref[...] = m_new

        scores(n, sa_ref, ta_ref, True)

        def loop_body(jj, carry):
            scores(2 * jj, sb_ref, tb_ref, False)
            absorb(jnp.where(jj == 0, n, 2 * jj - 1), sa_ref, ta_ref)
            scores(2 * jj + 1, sa_ref, ta_ref, False)
            absorb(2 * jj, sb_ref, tb_ref)
            return carry

        pairs = n // 2
        lax.fori_loop(0, pairs, loop_body, 0)
        held = jnp.where(pairs == 0, n, 2 * pairs - 1)

        @pl.when(n % 2 == 1)
        def _():
            scores(n - 1, sb_ref, tb_ref, False)
            absorb(held, sa_ref, ta_ref)
            absorb(n - 1, sb_ref, tb_ref)

        @pl.when(n % 2 == 0)
        def _():
            absorb(held, sa_ref, ta_ref)

        o_ref[...] = acc_ref[...] / l_ref[...]
        lse_ref[...] = m_ref[...] + jnp.log(l_ref[...]) - cq0

    lanes_full = pl.BlockSpec((None, Dh, S), lambda h, i: (h, 0, 0))
    lanes_tile = pl.BlockSpec((None, Dh, tq), lambda h, i: (h, 0, i))
    return pl.pallas_call(
        body, name=name, grid=(H, S // tq),
        in_specs=[lanes_tile, pl.BlockSpec((None, S, Dh), lambda h, i: (h, 0, 0)), lanes_full,
                  pl.BlockSpec((None, 1, S), lambda h, i: (h, 0, 0)),
                  pl.BlockSpec((None, S, LANES), lambda h, i: (h, 0, 0))],
        out_specs=[lanes_tile, pl.BlockSpec((None, 1, tq), lambda h, i: (h, 0, i))],
        out_shape=[jax.ShapeDtypeStruct((H, Dh, S), F32), jax.ShapeDtypeStruct((H, 1, S), F32)],
        scratch_shapes=[pltpu.VMEM((1, tq), F32), pltpu.VMEM((1, tq), F32), pltpu.VMEM((Dh, tq), F32),
                        pltpu.VMEM((tk, tq), F32), pltpu.VMEM((tk, tq), F32),
                        pltpu.VMEM((1, tq), F32), pltpu.VMEM((1, tq), F32)],
        compiler_params=_params("parallel", "arbitrary"),
    )(qt, k, vt, c_row, c_rep)


def _fox_bwd(qt, k, kt, v, dot, c_rep, lse_row, delta_row, name, tq=1024, tk=512):
    H, Dh, S = qt.shape
    tq = _fit(S, tq)
    tk = _fit(tq, tk)
    ratio = tq // tk
    nq = S // tq

    def body(k_ref, kt_ref, v_ref, crep_ref, qt_ref, dot_ref, lse_ref, dl_ref,
             dqt_ref, dkt_ref, dvt_ref, dcs_ref, drs_ref, dka_ref, dva_ref, dca_ref):
        j = pl.program_id(1)

        @pl.when(j == 0)
        def _():
            dqt_ref[...] = jnp.zeros_like(dqt_ref)
            drs_ref[...] = jnp.zeros_like(drs_ref)

        kv = k_ref[...]
        ktv = kt_ref[...]
        vv = v_ref[...]
        c_col = _tile_lanes(crep_ref[...], tq)
        dka_ref[...] = jnp.zeros_like(dka_ref)
        dva_ref[...] = jnp.zeros_like(dva_ref)
        dca_ref[...] = jnp.zeros_like(dca_ref)
        i_diag = j // ratio
        k_off = (j - i_diag * ratio) * tk

        def step(i, diagonal):
            off = pl.multiple_of(i * tq, LANES)
            qv = qt_ref[:, pl.ds(off, tq)] * QK_SCALE
            dov = dot_ref[:, pl.ds(off, tq)]
            e = _dot(kv, qv, _NN) - lse_ref[:, pl.ds(off, tq)] - c_col
            if diagonal:
                key = lax.broadcasted_iota(jnp.int32, (tk, tq), 0) + k_off
                qry = lax.broadcasted_iota(jnp.int32, (tk, tq), 1)
                e = jnp.where(key <= qry, e, NEG)
            p_t = jnp.exp(e)
            dva_ref[...] += _dot(dov, p_t.astype(BF16), _NT)
            ds_t = p_t * (_dot(vv, dov, _NN) - dl_ref[:, pl.ds(off, tq)])
            ds_b = ds_t.astype(BF16)
            dka_ref[...] += _dot(qv, ds_b, _NT)
            dqt_ref[:, pl.ds(off, tq)] += _dot(ktv, ds_b, _NN) * QK_SCALE
            drs_ref[:, pl.ds(off, tq)] += jnp.sum(ds_t, axis=0, keepdims=True)
            part = ds_t[:, 0:LANES]
            for cidx in range(1, tq // LANES):
                part = part + ds_t[:, cidx * LANES:(cidx + 1) * LANES]
            dca_ref[...] += part

        step(i_diag, True)

        def loop_body(i, carry):
            step(i, False)
            return carry

        lax.fori_loop(i_diag + 1, nq, loop_body, 0)
        dkt_ref[...] = dka_ref[...]
        dvt_ref[...] = dva_ref[...]
        dcs_ref[...] = jnp.broadcast_to(jnp.sum(dca_ref[...], axis=1, keepdims=True), dcs_ref.shape)

    rows_tile = pl.BlockSpec((None, tk, Dh), lambda h, j: (h, j, 0))
    lanes_tile = pl.BlockSpec((None, Dh, tk), lambda h, j: (h, 0, j))
    rep = pl.BlockSpec((None, tk, LANES), lambda h, j: (h, j, 0))
    lanes_full = pl.BlockSpec((None, Dh, S), lambda h, j: (h, 0, 0))
    rowv = pl.BlockSpec((None, 1, S), lambda h, j: (h, 0, 0))
    return pl.pallas_call(
        body, name=name, grid=(H, S // tk),
        in_specs=[rows_tile, lanes_tile, rows_tile, rep, lanes_full, lanes_full, rowv, rowv],
        out_specs=[lanes_full, lanes_tile, lanes_tile, rep, rowv],
        out_shape=[jax.ShapeDtypeStruct((H, Dh, S), F32), jax.ShapeDtypeStruct((H, Dh, S), F32),
                   jax.ShapeDtypeStruct((H, Dh, S), F32), jax.ShapeDtypeStruct((H, S, LANES), F32),
                   jax.ShapeDtypeStruct((H, 1, S), F32)],
        scratch_shapes=[pltpu.VMEM((Dh, tk), F32), pltpu.VMEM((Dh, tk), F32), pltpu.VMEM((tk, LANES), F32)],
        compiler_params=_params("parallel", "arbitrary"),
    )(k, kt, v, c_rep, qt, dot, lse_row, delta_row)


QKV_BLOCKS = 3 * (FOX_WIDTH + DIL_WIDTH) // LANES
DIL_Q_BLOCK = 3 * FOX_WIDTH // LANES
HEAD_PAIRS = N_DIL_HEADS // 2
PAIR_BLOCKS = DIL_WIDTH // LANES


def _band_geometry(S, d):
    L = S // d
    chunk = min(BAND_CHUNK_MAX, L)
    assert L % chunk == 0 and chunk % BAND == 0
    return L, chunk, chunk // BAND, L // chunk


def _band_in_specs(S, d):
    L, chunk, nb, _ = _band_geometry(S, d)

    def col(kind):
        return lambda hp, r, i: (i, r * QKV_BLOCKS + DIL_Q_BLOCK + kind * PAIR_BLOCKS + hp)

    def col_prev(kind):
        return lambda hp, r, i: (jnp.maximum(i * nb - 1, 0), r * QKV_BLOCKS + DIL_Q_BLOCK + kind * PAIR_BLOCKS + hp)

    main = [pl.BlockSpec((chunk, LANES), col(kind)) for kind in range(3)]
    prev = [pl.BlockSpec((BAND, LANES), col_prev(kind)) for kind in range(3)]
    bias = pl.BlockSpec((None, 2, 2 * BAND, BAND), lambda hp, r, i: (hp, 0, 0, 0))
    stat = pl.BlockSpec((None, 2, chunk), lambda hp, r, i: (hp, 0, r * (L // chunk) + i))
    tok = pl.BlockSpec((chunk, LANES), lambda hp, r, i: (i, r * PAIR_BLOCKS + hp))
    return main, prev, bias, stat, tok


def _band_scores_t(kb, qb, bias_t, first):
    s = _dot(kb, qb, _NT) + bias_t
    if first is not None:
        key = lax.broadcasted_iota(jnp.int32, s.shape, 0)
        s = jnp.where(jnp.logical_and(first, key < BAND), NEG, s)
    return s


def _dil_lse(qkv_v, bias_t, d, name):
    L = qkv_v.shape[0]
    S = L * d
    _, chunk, nb, nchunks = _band_geometry(S, d)
    main, prev, bias, stat, _ = _band_in_specs(S, d)

    def body(q_ref, k_ref, kp_ref, b_ref, lse_ref, kext_ref):
        first = pl.program_id(2) == 0
        kext_ref[0:BAND, :] = kp_ref[...]
        kext_ref[BAND:, :] = k_ref[...]
        for a in range(2):
            lanes = slice(a * HEAD_DIM, (a + 1) * HEAD_DIM)
            bias_v = b_ref[a]
            for b in range(nb):
                s = _band_scores_t(kext_ref[b * BAND:(b + 2) * BAND, lanes],
                                   q_ref[b * BAND:(b + 1) * BAND, lanes] * QK_SCALE, bias_v, first if b == 0 else None)
                m = jnp.max(s, axis=0, keepdims=True)
                lse_ref[a:a + 1, b * BAND:(b + 1) * BAND] = m + jnp.log(jnp.sum(jnp.exp(s - m), axis=0, keepdims=True))

    return pl.pallas_call(
        body, name=name, grid=(HEAD_PAIRS, d, nchunks),
        in_specs=[main[0], main[1], prev[1], bias], out_specs=stat,
        out_shape=jax.ShapeDtypeStruct((HEAD_PAIRS, 2, S), F32),
        scratch_shapes=[pltpu.VMEM((chunk + BAND, LANES), BF16)],
        compiler_params=_params("parallel", "parallel", "parallel"),
    )(qkv_v, qkv_v, qkv_v, bias_t)


def _dil_out(qkv_v, bias_t, lse_joint, d, name):
    L = qkv_v.shape[0]
    S = L * d
    _, chunk, nb, nchunks = _band_geometry(S, d)
    main, prev, bias, stat, tok = _band_in_specs(S, d)

    def body(q_ref, k_ref, kp_ref, v_ref, vp_ref, b_ref, lse_ref, o_ref, kext_ref, vext_ref):
        first = pl.program_id(2) == 0
        kext_ref[0:BAND, :] = kp_ref[...]
        kext_ref[BAND:, :] = k_ref[...]
        vext_ref[0:BAND, :] = vp_ref[...]
        vext_ref[BAND:, :] = v_ref[...]
        for a in range(2):
            lanes = slice(a * HEAD_DIM, (a + 1) * HEAD_DIM)
            bias_v = b_ref[a]
            for b in range(nb):
                rows, ext = slice(b * BAND, (b + 1) * BAND), slice(b * BAND, (b + 2) * BAND)
                s = _band_scores_t(kext_ref[ext, lanes], q_ref[rows, lanes] * QK_SCALE, bias_v, first if b == 0 else None)
                p_t = jnp.exp(s - lse_ref[a:a + 1, rows])
                o_ref[rows, lanes] = _dot(p_t.astype(BF16), vext_ref[ext, lanes], _TN).astype(BF16)

    return pl.pallas_call(
        body, name=name, grid=(HEAD_PAIRS, d, nchunks),
        in_specs=[main[0], main[1], prev[1], main[2], prev[2], bias, stat], out_specs=tok,
        out_shape=jax.ShapeDtypeStruct((L, d * DIL_WIDTH), BF16),
        scratch_shapes=[pltpu.VMEM((chunk + BAND, LANES), BF16), pltpu.VMEM((chunk + BAND, LANES), BF16)],
        compiler_params=_params("parallel", "parallel", "parallel"),
    )(qkv_v, qkv_v, qkv_v, qkv_v, qkv_v, bias_t, lse_joint)


def _dil_bwd(qkv_v, do_v, bias_t, lse_joint, delta, d, name):
    L = qkv_v.shape[0]
    S = L * d
    _, chunk, nb, nchunks = _band_geometry(S, d)
    main, prev, bias, stat, tok = _band_in_specs(S, d)
    nblocks = L // BAND

    def nxt_row(i):
        return jnp.minimum((i + 1) * nb, nblocks - 1)

    q_next = pl.BlockSpec((BAND, LANES), lambda hp, r, i: (nxt_row(i), r * QKV_BLOCKS + DIL_Q_BLOCK + hp))
    do_next = pl.BlockSpec((BAND, LANES), lambda hp, r, i: (nxt_row(i), r * PAIR_BLOCKS + hp))
    stat_next = pl.BlockSpec((None, 2, BAND), lambda hp, r, i: (hp, 0, r * nblocks + nxt_row(i)))

    def body(q_ref, k_ref, kp_ref, v_ref, vp_ref, do_ref, b_ref, lse_ref, dl_ref,
             qn_ref, don_ref, lsen_ref, dln_ref,
             dq_ref, dk_ref, dv_ref, db_ref, kext_ref, vext_ref, dkext_ref, dvext_ref):
        r, i = pl.program_id(1), pl.program_id(2)
        first = i == 0
        has_next = i + 1 < nchunks
        tail = slice(BAND + chunk, 2 * BAND + chunk)
        kext_ref[0:BAND, :] = kp_ref[...]
        kext_ref[BAND:BAND + chunk, :] = k_ref[...]
        kext_ref[tail, :] = jnp.zeros((BAND, LANES), BF16)
        vext_ref[0:BAND, :] = vp_ref[...]
        vext_ref[BAND:BAND + chunk, :] = v_ref[...]
        vext_ref[tail, :] = jnp.zeros((BAND, LANES), BF16)
        dkext_ref[...] = jnp.zeros_like(dkext_ref)
        dvext_ref[...] = jnp.zeros_like(dvext_ref)

        @pl.when(jnp.logical_and(r == 0, i == 0))
        def _():
            db_ref[...] = jnp.zeros_like(db_ref)

        def block(a, qb, dob, lse_row, dl_row, ext, mask_rows):
            lanes = slice(a * HEAD_DIM, (a + 1) * HEAD_DIM)
            kb, vb = kext_ref[ext, lanes], vext_ref[ext, lanes]
            s = _dot(kb, qb, _NT) + b_ref[a]
            if mask_rows is not None:
                s = jnp.where(mask_rows, NEG, s)
            p_t = jnp.exp(s - lse_row)
            ds_t = p_t * (_dot(vb, dob, _NT) - dl_row)
            ds_b = ds_t.astype(BF16)
            dkext_ref[ext, lanes] += _dot(ds_b, qb, _NN)
            dvext_ref[ext, lanes] += _dot(p_t.astype(BF16), dob, _NN)
            return ds_t, ds_b, kb

        key = lax.broadcasted_iota(jnp.int32, (2 * BAND, BAND), 0)
        for a in range(2):
            lanes = slice(a * HEAD_DIM, (a + 1) * HEAD_DIM)
            db_acc = jnp.zeros((2 * BAND, BAND), F32)
            for b in range(nb):
                rows, ext = slice(b * BAND, (b + 1) * BAND), slice(b * BAND, (b + 2) * BAND)
                mask = jnp.logical_and(first, key < BAND) if b == 0 else None
                ds_t, ds_b, kb = block(a, q_ref[rows, lanes] * QK_SCALE, do_ref[rows, lanes],
                                       lse_ref[a:a + 1, rows], dl_ref[a:a + 1, rows], ext, mask)
                dq_ref[rows, lanes] = _dot(ds_b, kb, _TN) * QK_SCALE
                db_acc = db_acc + ds_t
            db_ref[a] += db_acc
            block(a, qn_ref[:, lanes] * QK_SCALE, don_ref[:, lanes], lsen_ref[a:a + 1, :], dln_ref[a:a + 1, :],
                  slice(chunk, chunk + 2 * BAND), jnp.logical_or(jnp.logical_not(has_next), key >= BAND))
        dk_ref[...] = dkext_ref[BAND:BAND + chunk, :]
        dv_ref[...] = dvext_ref[BAND:BAND + chunk, :]

    ext_rows = chunk + 2 * BAND
    return pl.pallas_call(
        body, name=name, grid=(HEAD_PAIRS, d, nchunks),
        in_specs=[main[0], main[1], prev[1], main[2], prev[2], tok, bias, stat, stat,
                  q_next, do_next, stat_next, stat_next],
        out_specs=[tok, tok, tok, bias],
        out_shape=[jax.ShapeDtypeStruct((L, d * DIL_WIDTH), F32)] * 3
                  + [jax.ShapeDtypeStruct((HEAD_PAIRS, 2, 2 * BAND, BAND), F32)],
        scratch_shapes=[pltpu.VMEM((ext_rows, LANES), BF16), pltpu.VMEM((ext_rows, LANES), BF16),
                        pltpu.VMEM((ext_rows, LANES), F32), pltpu.VMEM((ext_rows, LANES), F32)],
        compiler_params=_params("arbitrary", "arbitrary", "arbitrary"),
    )(qkv_v, qkv_v, qkv_v, qkv_v, qkv_v, do_v, bias_t, lse_joint, delta, qkv_v, do_v, lse_joint, delta)


def _lse_join(lse3, name):
    P, H, S = lse3.shape

    def body(l_ref, o_ref):
        a, b, c = l_ref[0], l_ref[1], l_ref[2]
        m = jnp.maximum(jnp.maximum(a, b), c)
        o_ref[...] = m + jnp.log(jnp.exp(a - m) + jnp.exp(b - m) + jnp.exp(c - m))

    return pl.pallas_call(body, name=name, out_shape=jax.ShapeDtypeStruct((H, S), F32))(lse3)


def _bucket_reduce(dbias_t, bucket_map_t, name):
    P, H = dbias_t.shape[:2]

    def body(db_ref, bk_ref, o_ref):
        p, h = pl.program_id(0), pl.program_id(1)

        @pl.when(jnp.logical_and(p == 0, h == 0))
        def _():
            o_ref[...] = jnp.zeros_like(o_ref)

        db, bk = db_ref[...], bk_ref[...]
        row = lax.broadcasted_iota(jnp.int32, (N_BUCKETS, LANES), 0)
        lane = lax.broadcasted_iota(jnp.int32, (N_BUCKETS, LANES), 1)

        def one(b, acc):
            val = jnp.sum(jnp.sum(jnp.where(bk == b, db, 0.0), axis=1, keepdims=True), axis=0, keepdims=True)
            return acc + jnp.where(jnp.logical_and(row == b, lane == h), val, 0.0)

        o_ref[...] += lax.fori_loop(0, N_BUCKETS, one, jnp.zeros((N_BUCKETS, LANES), F32))

    return pl.pallas_call(
        body, name=name, grid=(P, H),
        in_specs=[pl.BlockSpec((None, None, 2 * BAND, BAND), lambda p, h: (p, h, 0, 0)),
                  pl.BlockSpec((None, 2 * BAND, BAND), lambda p, h: (p, 0, 0))],
        out_specs=pl.BlockSpec((N_BUCKETS, LANES), lambda p, h: (0, 0)),
        out_shape=jax.ShapeDtypeStruct((N_BUCKETS, LANES), F32),
        compiler_params=_params("arbitrary", "arbitrary"),
    )(dbias_t, bucket_map_t)


def _mem_fwd(q, kv, name, tq=1024):
    S, W = q.shape
    N = kv.shape[0]
    pairs = W // LANES
    tq = _fit(S, tq)

    def body(q_ref, k_ref, v_ref, o_ref, lse_ref):
        for a in range(2):
            lanes = slice(a * HEAD_DIM, (a + 1) * HEAD_DIM)
            s = _dot(k_ref[:, lanes], q_ref[:, lanes] * QK_SCALE, _NT)
            m = jnp.max(s, axis=0, keepdims=True)
            e = jnp.exp(s - m)
            l = jnp.sum(e, axis=0, keepdims=True)
            o_ref[:, lanes] = _dot((e / l).astype(BF16), v_ref[:, lanes], _TN).astype(BF16)
            lse_ref[a:a + 1, :] = m + jnp.log(l)

    return pl.pallas_call(
        body, name=name, grid=(pairs, S // tq),
        in_specs=[pl.BlockSpec((tq, LANES), lambda hp, i: (i, hp)),
                  pl.BlockSpec((N, LANES), lambda hp, i: (0, hp)),
                  pl.BlockSpec((N, LANES), lambda hp, i: (0, pairs + hp))],
        out_specs=[pl.BlockSpec((tq, LANES), lambda hp, i: (i, hp)),
                   pl.BlockSpec((None, 2, tq), lambda hp, i: (hp, 0, i))],
        out_shape=[jax.ShapeDtypeStruct((S, W), BF16), jax.ShapeDtypeStruct((pairs, 2, S), F32)],
        compiler_params=_params("parallel", "parallel"),
    )(q, kv, kv)


def _mem_bwd(q, kv, do, lse, delta, name, tq=1024):
    S, W = q.shape
    N = kv.shape[0]
    pairs = W // LANES
    tq = _fit(S, tq)

    def body(q_ref, k_ref, v_ref, do_ref, lse_ref, dl_ref, dq_ref, dk_ref, dv_ref):
        i = pl.program_id(1)

        @pl.when(i == 0)
        def _():
            dk_ref[...] = jnp.zeros_like(dk_ref)
            dv_ref[...] = jnp.zeros_like(dv_ref)

        for a in range(2):
            lanes = slice(a * HEAD_DIM, (a + 1) * HEAD_DIM)
            qv, dov = q_ref[:, lanes] * QK_SCALE, do_ref[:, lanes]
            kv_, vv = k_ref[:, lanes], v_ref[:, lanes]
            p_t = jnp.exp(_dot(kv_, qv, _NT) - lse_ref[a:a + 1, :])
            ds_t = p_t * (_dot(vv, dov, _NT) - dl_ref[a:a + 1, :])
            ds_b = ds_t.astype(BF16)
            dq_ref[:, lanes] = (_dot(ds_b, kv_, _TN) * QK_SCALE).astype(BF16)
            dk_ref[:, lanes] += _dot(ds_b, qv, _NN)
            dv_ref[:, lanes] += _dot(p_t.astype(BF16), dov, _NN)

    qs = pl.BlockSpec((tq, LANES), lambda hp, i: (i, hp))
    stat = pl.BlockSpec((None, 2, tq), lambda hp, i: (hp, 0, i))
    acc = pl.BlockSpec((N, LANES), lambda hp, i: (0, hp))
    return pl.pallas_call(
        body, name=name, grid=(pairs, S // tq),
        in_specs=[qs, acc, pl.BlockSpec((N, LANES), lambda hp, i: (0, pairs + hp)), qs, stat, stat],
        out_specs=[qs, acc, acc],
        out_shape=[jax.ShapeDtypeStruct((S, W), BF16), jax.ShapeDtypeStruct((N, W), F32),
                   jax.ShapeDtypeStruct((N, W), F32)],
        compiler_params=_params("parallel", "arbitrary"),
    )(q, kv, kv, do, lse, delta)


def _head_rowdot(a, bs, name, tr=512):
    S, W = a.shape
    tr = _fit(S, tr)

    def body(*refs):
        tot = refs[1][...].astype(F32)
        for r in refs[2:-1]:
            tot = tot + r[...].astype(F32)
        prod = refs[0][...].astype(F32) * tot
        col = lax.broadcasted_iota(jnp.int32, (W, LANES), 0)
        lane = lax.broadcasted_iota(jnp.int32, (W, LANES), 1)
        sel = jnp.where(col // HEAD_DIM == lane, 1.0, 0.0).astype(F32)
        refs[-1][...] = lax.dot_general(prod, sel, (_NN, ((), ())), precision=lax.Precision.HIGHEST,
                                        preferred_element_type=F32)

    row = pl.BlockSpec((tr, W), lambda i: (i, 0))
    return pl.pallas_call(
        body, name=name, grid=(S // tr,), in_specs=[row] * (1 + len(bs)),
        out_specs=pl.BlockSpec((tr, LANES), lambda i: (i, 0)),
        out_shape=jax.ShapeDtypeStruct((S, LANES), F32),
        compiler_params=_params("parallel"),
    )(a, *bs)


def _sum_cast_cols(groups, out_dtype, name, tr=256):
    S, W = groups[0][0].shape
    tr = _fit(S, tr)
    flat = [t for g in groups for t in g]

    def body(*refs):
        o_ref = refs[-1]
        k = 0
        for gi, g in enumerate(groups):
            acc = refs[k][...].astype(F32)
            for r in refs[k + 1:k + len(g)]:
                acc = acc + r[...].astype(F32)
            o_ref[:, gi * W:(gi + 1) * W] = acc.astype(out_dtype)
            k += len(g)

    row = pl.BlockSpec((tr, W), lambda i: (i, 0))
    return pl.pallas_call(
        body, name=name, grid=(S // tr,), in_specs=[row] * len(flat),
        out_specs=pl.BlockSpec((tr, W * len(groups)), lambda i: (i, 0)),
        out_shape=jax.ShapeDtypeStruct((S, W * len(groups)), out_dtype),
        compiler_params=_params("parallel"),
    )(*flat)


FF_TILE = 256


def _ffn_up(h, w_gu, name, tm=1024):
    S, D = h.shape
    F2 = w_gu.shape[1]
    tm = _fit(S, tm)

    def body(h_ref, w_ref, gu_ref, act_ref):
        gu = _dot(h_ref[...], w_ref[...], _NN)
        gu_ref[...] = gu.astype(BF16)
        g, u = gu[:, :FF_TILE], gu[:, FF_TILE:]
        act_ref[...] = (g * (1.0 / (1.0 + jnp.exp(-g))) * u).astype(BF16)

    return pl.pallas_call(
        body, name=name, grid=(S // tm, F2 // (2 * FF_TILE)),
        in_specs=[pl.BlockSpec((tm, D), lambda i, j: (i, 0)), pl.BlockSpec((D, 2 * FF_TILE), lambda i, j: (0, j))],
        out_specs=[pl.BlockSpec((tm, 2 * FF_TILE), lambda i, j: (i, j)),
                   pl.BlockSpec((tm, FF_TILE), lambda i, j: (i, j))],
        out_shape=[jax.ShapeDtypeStruct((S, F2), BF16), jax.ShapeDtypeStruct((S, F2 // 2), BF16)],
        compiler_params=_params("parallel", "arbitrary"),
    )(h, w_gu)


def _ffn_dact(dy, w_down, gu, name, tm=1024):
    S, D = dy.shape
    F2 = gu.shape[1]
    tm = _fit(S, tm)

    def body(dy_ref, w_ref, gu_ref, dgu_ref):
        dact = _dot(dy_ref[...], w_ref[...], _NT)
        gu_v = gu_ref[...].astype(F32)
        g, u = gu_v[:, :FF_TILE], gu_v[:, FF_TILE:]
        sig = 1.0 / (1.0 + jnp.exp(-g))
        silu = g * sig
        dgu_ref[:, :FF_TILE] = (dact * u * (sig + silu * (1.0 - sig))).astype(BF16)
        dgu_ref[:, FF_TILE:] = (dact * silu).astype(BF16)

    return pl.pallas_call(
        body, name=name, grid=(S // tm, F2 // (2 * FF_TILE)),
        in_specs=[pl.BlockSpec((tm, D), lambda i, j: (i, 0)), pl.BlockSpec((FF_TILE, D), lambda i, j: (j, 0)),
                  pl.BlockSpec((tm, 2 * FF_TILE), lambda i, j: (i, j))],
        out_specs=pl.BlockSpec((tm, 2 * FF_TILE), lambda i, j: (i, j)),
        out_shape=jax.ShapeDtypeStruct((S, F2), BF16),
        compiler_params=_params("parallel", "arbitrary"),
    )(dy, w_down, gu)


def _fit_rows(n, cap):
    if n <= cap:
        return n
    t = (cap // 8) * 8
    while t >= 8:
        if n % t == 0:
            return t
        t -= 8
    raise ValueError(f"no sublane-aligned tile for {n} under {cap}")


def _add_n(arrs, name, tr=512):
    R, C = arrs[0].shape
    tr = _fit_rows(R, tr)

    def body(*refs):
        acc = refs[0][...]
        for r in refs[1:-1]:
            acc = acc + r[...]
        refs[-1][...] = acc

    row = pl.BlockSpec((tr, C), lambda i: (i, 0))
    return pl.pallas_call(
        body, name=name, grid=(R // tr,), in_specs=[row] * len(arrs), out_specs=row,
        out_shape=jax.ShapeDtypeStruct((R, C), F32), compiler_params=_params("parallel"),
    )(*arrs)


def _adamw(w, g, m, v, name, tr=512):
    R, C = w.shape
    tr = _fit_rows(R, tr)
    c1 = 1.0 / (1.0 - ADAM_B1 ** ADAM_STEP)
    c2 = 1.0 / (1.0 - ADAM_B2 ** ADAM_STEP)

    def body(w_ref, g_ref, m_ref, v_ref, d_ref, nm_ref, nv_ref):
        gv = g_ref[...]
        nm = ADAM_B1 * m_ref[...] + (1.0 - ADAM_B1) * gv
        nv = ADAM_B2 * v_ref[...] + (1.0 - ADAM_B2) * (gv * gv)
        nm_ref[...] = nm
        nv_ref[...] = nv
        d_ref[...] = -ADAM_LR * ((nm * c1) / (jnp.sqrt(nv * c2) + ADAM_EPS) + ADAM_WD * w_ref[...])

    row = pl.BlockSpec((tr, C), lambda i: (i, 0))
    return pl.pallas_call(
        body, name=name, grid=(R // tr,), in_specs=[row] * 4, out_specs=[row] * 3,
        out_shape=[jax.ShapeDtypeStruct((R, C), F32)] * 3, compiler_params=_params("parallel"),
    )(w, g, m, v)


def _place():
    return lax.axis_index("x"), lax.axis_index("y"), lax.axis_index("c")


_ANY = pl.BlockSpec(memory_space=pl.ANY)


def _chip_all_gather(shard, name):
    R, C = shard.shape

    def body(x_ref, out_ref, send_sems, recv_sems, local_sem):
        x, y, c = _place()
        chips = [(1 - x, y), (x, 1 - y), (1 - x, 1 - y)]
        mine = pltpu.make_async_copy(x_ref, out_ref.at[2 * x + y], local_sem)
        mine.start()

        def copy(k, slot, to):
            return pltpu.make_async_remote_copy(
                src_ref=x_ref, dst_ref=out_ref.at[slot], send_sem=send_sems.at[k], recv_sem=recv_sems.at[k],
                device_id=to, device_id_type=MESH_IDS)

        sends = [copy(k, 2 * x + y, (cx, cy, c)) for k, (cx, cy) in enumerate(chips)]
        for cp in sends:
            cp.start()
        for k, (cx, cy) in enumerate(chips):
            copy(k, 2 * cx + cy, (cx, cy, c)).wait_recv()
        for cp in sends:
            cp.wait_send()
        mine.wait()

    return pl.pallas_call(
        body, name=name, in_specs=[_ANY], out_specs=_ANY,
        out_shape=jax.ShapeDtypeStruct((N_CHIPS, R, C), shard.dtype),
        scratch_shapes=[pltpu.SemaphoreType.DMA((3,)), pltpu.SemaphoreType.DMA((3,)), pltpu.SemaphoreType.DMA],
    )(shard)


def _sibling_exchange(buf, name):
    def body(x_ref, out_ref, send_sem, recv_sem):
        x, y, c = _place()
        cp = pltpu.make_async_remote_copy(
            src_ref=x_ref, dst_ref=out_ref, send_sem=send_sem, recv_sem=recv_sem,
            device_id=(x, y, 1 - c), device_id_type=MESH_IDS)
        cp.start()
        cp.wait()

    return pl.pallas_call(
        body, name=name, in_specs=[_ANY], out_specs=_ANY,
        out_shape=jax.ShapeDtypeStruct(buf.shape, buf.dtype),
        scratch_shapes=[pltpu.SemaphoreType.DMA, pltpu.SemaphoreType.DMA],
    )(buf)


def _chip_scatter(parts, name):
    _, R, C = parts.shape

    def body(p_ref, out_ref, send_sems, recv_sems):
        x, y, c = _place()
        chips = [(1 - x, y), (x, 1 - y), (1 - x, 1 - y)]

        def copy(k, slab, to):
            return pltpu.make_async_remote_copy(
                src_ref=p_ref.at[slab], dst_ref=out_ref.at[k], send_sem=send_sems.at[k], recv_sem=recv_sems.at[k],
                device_id=to, device_id_type=MESH_IDS)

        sends = [copy(k, 2 * cx + cy, (cx, cy, c)) for k, (cx, cy) in enumerate(chips)]
        for cp in sends:
            cp.start()
        for cp in sends:
            cp.wait_recv()
        for cp in sends:
            cp.wait_send()

    return pl.pallas_call(
        body, name=name, in_specs=[_ANY], out_specs=_ANY,
        out_shape=jax.ShapeDtypeStruct((3, R, C), parts.dtype),
        scratch_shapes=[pltpu.SemaphoreType.DMA((3,)), pltpu.SemaphoreType.DMA((3,))],
    )(parts)


def _all_to_all_small(vec, name):
    R, C = vec.shape

    def body(v_ref, out_ref, send_sems, recv_sems, local_sem):
        x, y, c = _place()
        me = 4 * x + 2 * y + c
        mine = pltpu.make_async_copy(v_ref, out_ref.at[me], local_sem)
        mine.start()
        flips = [(dx, dy, dc) for dx in (0, 1) for dy in (0, 1) for dc in (0, 1)][1:]

        def peer(f):
            return (x ^ f[0], y ^ f[1], c ^ f[2])

        def copy(k, slot, to):
            return pltpu.make_async_remote_copy(
                src_ref=v_ref, dst_ref=out_ref.at[slot], send_sem=send_sems.at[k], recv_sem=recv_sems.at[k],
                device_id=to, device_id_type=MESH_IDS)

        sends = [copy(k, me, peer(f)) for k, f in enumerate(flips)]
        for cp in sends:
            cp.start()
        for k, f in enumerate(flips):
            px, py, pc = peer(f)
            copy(k, 4 * px + 2 * py + pc, peer(f)).wait_recv()
        for cp in sends:
            cp.wait_send()
        mine.wait()

    return pl.pallas_call(
        body, name=name, in_specs=[_ANY], out_specs=_ANY,
        out_shape=jax.ShapeDtypeStruct((8, R, C), vec.dtype),
        scratch_shapes=[pltpu.SemaphoreType.DMA((7,)), pltpu.SemaphoreType.DMA((7,)), pltpu.SemaphoreType.DMA],
    )(vec)


def _to_heads(t, n):
    S = t.shape[0]
    return t.reshape(S, n, HEAD_DIM).transpose(1, 0, 2)


def _to_heads_t(t, n):
    S = t.shape[0]
    return t.T.reshape(n, HEAD_DIM, S)


def _from_heads_t(t):
    H, Dh, S = t.shape
    return t.reshape(H * Dh, S).T


def _rep(t):
    return jnp.broadcast_to(t[..., None], t.shape + (LANES,))


def _t5_bucket(dist):
    max_exact = N_BUCKETS // 2
    d = np.maximum(dist, 1).astype(np.float32)
    large = max_exact + (np.log(d / max_exact) / np.log(MAX_DISTANCE / max_exact)
                         * (N_BUCKETS - max_exact)).astype(np.int32)
    large = np.minimum(large, N_BUCKETS - 1)
    return np.where(dist < max_exact, dist, large).astype(np.int32)


def _band_tables():
    qi = np.arange(BAND)[:, None]
    kj = np.arange(2 * BAND)[None, :]
    sub = qi + BAND - kj
    band = (sub >= 0) & (sub <= BAND)
    out = []
    for d in DILATIONS:
        bucket = _t5_bucket(np.clip(sub, 0, BAND) * d)
        out.append(np.where(band, bucket, -1).astype(np.int32))
    return np.stack(out)


_PACK = (("w_in", 770), ("w_out", 256), ("w_xq", 64), ("w_xk", 64), ("w_xv", 64), ("w_xo", 64),
         ("w_gate", 704), ("w_up", 704), ("w_down", 704))


def _pack(shards):
    rows = [shards[n].reshape(-1, PACK_COLS) for n, _ in _PACK]
    total = sum(r.shape[0] for r in rows)
    pad = (-total) % 16
    if pad:
        rows.append(jnp.zeros((pad, PACK_COLS), rows[0].dtype))
    return jnp.concatenate(rows, axis=0)


def _unpack(pack, shapes):
    out, r = {}, 0
    for n, _ in _PACK:
        cnt = int(np.prod(shapes[n])) // PACK_COLS
        out[n] = pack[r:r + cnt].reshape(shapes[n])
        r += cnt
    return out


_COL_SHARDED = ("w_in", "w_xo", "w_gate", "w_up")


def _full_weight(gathered, name):
    return jnp.concatenate(gathered, axis=1 if name in _COL_SHARDED else 0)


def _split_weight(full, name):
    return jnp.split(full, N_CHIPS, axis=1 if name in _COL_SHARDED else 0)


_SMALL = ("g_mix_pre", "g_mix_post", "g_xattn_pre", "g_mem", "g_xattn_post", "g_ffn_pre", "g_ffn_post")


def _pack_small(vals):
    D = vals["g_mix_pre"].shape[1]
    rows = [vals[n].reshape(1, D) for n in _SMALL]
    misc = jnp.concatenate([vals["b_f"].reshape(-1), vals["rel_bias"].reshape(-1)])
    rows.append(jnp.pad(misc, (0, D - misc.shape[0])).reshape(1, D))
    rows.append(jnp.zeros((16 - len(rows), D), F32))
    return jnp.concatenate(rows, axis=0)


def _unpack_small(pack):
    out = {n: pack[i:i + 1] for i, n in enumerate(_SMALL)}
    out["b_f"] = pack[7, 0:N_FOX_HEADS].reshape(1, N_FOX_HEADS)
    out["rel_bias"] = pack[7, N_FOX_HEADS:N_FOX_HEADS + N_BUCKETS * N_DIL_HEADS].reshape(N_BUCKETS, N_DIL_HEADS)
    return out


def kernel(x, mem, g_mix_pre, w_in, b_f, rel_bias, w_out, g_mix_post, g_xattn_pre, g_mem, w_xq, w_xk, w_xv, w_xo, g_xattn_post, g_ffn_pre, w_gate, w_up, w_down, g_ffn_post, loss_target, m_g_mix_pre, m_w_in, m_b_f, m_rel_bias, m_w_out, m_g_mix_post, m_g_xattn_pre, m_g_mem, m_w_xq, m_w_xk, m_w_xv, m_w_xo, m_g_xattn_post, m_g_ffn_pre, m_w_gate, m_w_up, m_w_down, m_g_ffn_post, v_g_mix_pre, v_w_in, v_b_f, v_rel_bias, v_w_out, v_g_mix_post, v_g_xattn_pre, v_g_mem, v_w_xq, v_w_xk, v_w_xv, v_w_xo, v_g_xattn_post, v_g_ffn_pre, v_w_gate, v_w_up, v_w_down, v_g_ffn_post):
    args = dict(locals())
    big = [n for n, _ in _PACK]
    names = ["g_mix_pre", "w_in", "b_f", "rel_bias", "w_out", "g_mix_post", "g_xattn_pre", "g_mem", "w_xq",
             "w_xk", "w_xv", "w_xo", "g_xattn_post", "g_ffn_pre", "w_gate", "w_up", "w_down", "g_ffn_post"]
    xs = x[0]
    S, D = xs.shape
    assert S % (BAND * DILATIONS[-1]) == 0
    shard_shapes = {n: args[n].shape[1:] for n in big}
    my_x, my_y, my_c = lax.axis_index("x"), lax.axis_index("y"), lax.axis_index("c")

    w_pack = _pack({n: args[n][0] for n in big})
    gathered = _chip_all_gather(w_pack.astype(BF16), "weights_all_gather")
    per_chip = [_unpack(gathered[j], shard_shapes) for j in range(N_CHIPS)]
    W = {n: _full_weight([pc[n] for pc in per_chip], n) for n in big}
    w_fox, w_fg, w_dil = (W["w_in"][:, :3 * FOX_WIDTH], W["w_in"][:, 3 * FOX_WIDTH:3 * FOX_WIDTH + N_FOX_HEADS],
                          W["w_in"][:, 3 * FOX_WIDTH + N_FOX_HEADS:])
    w_qkv = jnp.concatenate([w_fox, w_dil], axis=1)
    w_fg_pad = jnp.pad(w_fg, ((0, 0), (0, LANES - N_FOX_HEADS)))
    F = W["w_gate"].shape[1]
    nft = F // FF_TILE
    w_gu = jnp.stack([W["w_gate"].reshape(D, nft, FF_TILE), W["w_up"].reshape(D, nft, FF_TILE)],
                     axis=2).reshape(D, 2 * F)

    h1 = _rms_fwd(xs, g_mix_pre, "rms_mix_pre")
    qkv = _mm(h1, w_qkv, "nn", BF16, "proj_qkv")
    fg = _mm(h1, w_fg_pad, "nn", F32, "proj_gate")
    fg_t = fg[:, :N_FOX_HEADS].T
    b_col = b_f.reshape(N_FOX_HEADS, 1)
    c_t = _forget_fwd(fg_t, b_col, "forget_cumsum")
    c_row = c_t.reshape(N_FOX_HEADS, 1, S)
    c_rep = _rep(c_t)
    fq_s, fk_s, fv_s = (qkv[:, i * FOX_WIDTH:(i + 1) * FOX_WIDTH] for i in range(3))
    fqt, fkt, fvt = (_to_heads_t(t, N_FOX_HEADS) for t in (fq_s, fk_s, fv_s))
    fk, fv = _to_heads(fk_s, N_FOX_HEADS), _to_heads(fv_s, N_FOX_HEADS)
    o_fox_t, lse_fox = _fox_fwd(fqt, fk, fvt, c_row, c_rep, "fox_fwd")

    bucket_map = _band_tables()
    onehot = (jnp.asarray(bucket_map)[..., None] == jnp.arange(N_BUCKETS)).astype(F32)
    bias_tab = jnp.einsum("pqkb,bh->phkq", onehot, rel_bias, precision=lax.Precision.HIGHEST)
    bias_tab = jnp.where(jnp.asarray(bucket_map.transpose(0, 2, 1) >= 0)[:, None], bias_tab, NEG)
    bias_t = bias_tab.reshape(3, HEAD_PAIRS, 2, 2 * BAND, BAND)
    views = [qkv.reshape(S // d, d * qkv.shape[1]) for d in DILATIONS]

    def to_tok(stat, d):
        return stat.reshape(N_DIL_HEADS, d, S // d).swapaxes(1, 2).reshape(N_DIL_HEADS, S)

    def to_perm(stat, d):
        return stat.reshape(N_DIL_HEADS, S // d, d).swapaxes(1, 2).reshape(HEAD_PAIRS, 2, S)

    lse_tok = jnp.stack([to_tok(_dil_lse(views[p], bias_t[p], d, f"dilated_lse_{d}"), d)
                         for p, d in enumerate(DILATIONS)])
    lse_joint = _lse_join(lse_tok, "dilated_lse_join")
    lse_perm = [to_perm(lse_joint, d) for d in DILATIONS]
    o_dil = [_dil_out(views[p], bias_t[p], lse_perm[p], d, f"dilated_out_{d}").reshape(S, DIL_WIDTH)
             for p, d in enumerate(DILATIONS)]
    o_cat = _sum_cast_cols([[_from_heads_t(o_fox_t)]] + [[o] for o in o_dil], BF16, "mixer_out_cat")
    w_out_b = W["w_out"]
    w_out_cat = jnp.concatenate([w_out_b[:FOX_WIDTH]] + [w_out_b[FOX_WIDTH:]] * 3, axis=0)
    a = _mm(o_cat, w_out_cat, "nn", F32, "proj_out")
    x1, h2 = _resid_norm(xs, a, g_mix_post, g_xattn_pre, "resid_mix")

    hm = _rms_fwd(mem[0], g_mem, "rms_mem")
    q2 = _mm(h2, W["w_xq"], "nn", BF16, "xattn_q")
    w_xkv = jnp.concatenate([W["w_xk"], W["w_xv"]], axis=1)
    kvm = _mm(hm, w_xkv, "nn", BF16, "xattn_kv")
    MW = N_MEM_HEADS * HEAD_DIM
    oc, lse_mem = _mem_fwd(q2, kvm, "xattn_fwd")
    y2 = _mm(oc, W["w_xo"], "nn", F32, "xattn_o")
    x2, h3 = _resid_norm(x1, y2, g_xattn_post, g_ffn_pre, "resid_xattn")

    gu, act = _ffn_up(h3, w_gu, "ffn_up")
    y3 = _mm(act, W["w_down"], "nn", F32, "ffn_down", tk=1536)
    dx3, loss_tile = _final_loss(x2, y3, g_ffn_post, loss_target[0], "final_loss")

    grads = {}
    small = {}
    _, dy3_b, dg = _rms_bwd(y3, g_ffn_post, dx3, None, "bwd_norm_ffn_post", want=("bf16",))
    small["g_ffn_post"] = dg[0:1]
    grads["w_down"] = _mm(act, dy3_b, "tn", F32, "grad_w_down", tm=1408)
    dgu = _ffn_dact(dy3_b, W["w_down"], gu, "ffn_dact")
    dw_gu = _mm(h3, dgu, "tn", F32, "grad_w_gu").reshape(D, nft, 2, FF_TILE)
    grads["w_gate"], grads["w_up"] = dw_gu[:, :, 0].reshape(D, F), dw_gu[:, :, 1].reshape(D, F)
    dh3 = _mm(dgu, w_gu, "nt", F32, "bwd_ffn_in", tk=1024)
    dx2, _, dg = _rms_bwd(x2, g_ffn_pre, dh3, dx3, "bwd_norm_ffn_pre", want=("f32",))
    small["g_ffn_pre"] = dg[0:1]

    _, dy2_b, dg = _rms_bwd(y2, g_xattn_post, dx2, None, "bwd_norm_xattn_post", want=("bf16",))
    small["g_xattn_post"] = dg[0:1]
    grads["w_xo"] = _mm(oc, dy2_b, "tn", F32, "grad_w_xo")
    doc = _mm(dy2_b, W["w_xo"], "nt", BF16, "bwd_xattn_o")
    delta_mem = _head_rowdot(doc, [oc], "xattn_delta")[:, :N_MEM_HEADS].T.reshape(N_MEM_HEADS // 2, 2, S)
    dq2, dkm, dvm = _mem_bwd(q2, kvm, doc, lse_mem, delta_mem, "xattn_bwd")
    dkvm = jnp.concatenate([dkm, dvm], axis=1).astype(BF16)
    grads["w_xq"] = _mm(h2, dq2, "tn", F32, "grad_w_xq")
    dw_xkv = _mm(hm, dkvm, "tn", F32, "grad_w_xkv")
    grads["w_xk"], grads["w_xv"] = dw_xkv[:, :MW], dw_xkv[:, MW:]
    dhm = _mm(dkvm, w_xkv, "nt", F32, "bwd_xattn_kv")
    _, _, dg = _rms_bwd(mem[0], g_mem, dhm, None, "bwd_norm_mem", want=())
    small["g_mem"] = dg[0:1]
    dh2 = _mm(dq2, W["w_xq"], "nt", F32, "bwd_xattn_q")
    dx1, _, dg = _rms_bwd(x1, g_xattn_pre, dh2, dx2, "bwd_norm_xattn_pre", want=("f32",))
    small["g_xattn_pre"] = dg[0:1]

    _, da_b, dg = _rms_bwd(a, g_mix_post, dx1, None, "bwd_norm_mix_post", want=("bf16",))
    small["g_mix_post"] = dg[0:1]
    dw_out_cat = _mm(o_cat, da_b, "tn", F32, "grad_w_out")
    dw_out_dil = _add_n([dw_out_cat[FOX_WIDTH + p * DIL_WIDTH:FOX_WIDTH + (p + 1) * DIL_WIDTH] for p in range(3)],
                        "grad_w_out_dil")
    grads["w_out"] = jnp.concatenate([dw_out_cat[:FOX_WIDTH], dw_out_dil], axis=0)
    do = _mm(da_b, w_out_b, "nt", BF16, "bwd_proj_out")
    do_fox, do_dil = do[:, :FOX_WIDTH], do[:, FOX_WIDTH:]

    delta_fox = _head_rowdot(do_fox, [o_cat[:, :FOX_WIDTH]], "fox_delta")[:, :N_FOX_HEADS].T
    dqf, dkf, dvf, dcs, drs = _fox_bwd(fqt, fk, fkt, fv, _to_heads_t(do_fox, N_FOX_HEADS), c_rep,
                                       lse_fox, delta_fox.reshape(N_FOX_HEADS, 1, S), "fox_bwd")
    dfg_t, db_f = _forget_bwd(fg_t, b_col, drs[:, 0], dcs[..., 0], "forget_bwd")

    delta_dil = _head_rowdot(do_dil, o_dil, "dilated_delta")[:, :N_DIL_HEADS].T
    dil_grads = [_dil_bwd(views[p], do_dil.reshape(S // d, d * DIL_WIDTH), bias_t[p], lse_perm[p],
                          to_perm(delta_dil, d), d, f"dilated_bwd_{d}") for p, d in enumerate(DILATIONS)]
    dbias_t = jnp.stack([g[3].reshape(N_DIL_HEADS, 2 * BAND, BAND) for g in dil_grads])
    d_rel = _bucket_reduce(dbias_t, jnp.asarray(bucket_map.transpose(0, 2, 1)), "rel_bias_grad")[:, :N_DIL_HEADS]
    dqkv = _sum_cast_cols([[_from_heads_t(dqf)], [_from_heads_t(dkf)], [_from_heads_t(dvf)]]
                          + [[g[j].reshape(S, DIL_WIDTH) for g in dil_grads] for j in range(3)],
                          BF16, "dqkv_assemble")
    dfg_pad = jnp.pad(dfg_t.T, ((0, 0), (0, LANES - N_FOX_HEADS))).astype(BF16)
    dw_qkv = _mm(h1, dqkv, "tn", F32, "grad_w_qkv")
    dw_fg = _mm(h1, dfg_pad, "tn", F32, "grad_w_gate_cols")[:, :N_FOX_HEADS]
    grads["w_in"] = jnp.concatenate([dw_qkv[:, :3 * FOX_WIDTH], dw_fg, dw_qkv[:, 3 * FOX_WIDTH:]], axis=1)
    dcat = jnp.concatenate([dqkv, dfg_pad], axis=1)
    w_cat = jnp.concatenate([w_qkv, w_fg_pad], axis=1)
    dh1 = _mm(dcat, w_cat, "nt", F32, "bwd_proj_in", tk=640)
    grad_x, _, dg = _rms_bwd(xs, g_mix_pre, dh1, dx1, "bwd_norm_mix_pre", want=("f32",))
    small["g_mix_pre"] = dg[0:1]
    small["b_f"] = db_f[:, 0].reshape(1, N_FOX_HEADS)
    small["rel_bias"] = d_rel

    split = {n: _split_weight(grads[n], n) for n in big}
    parts = jnp.stack([_pack({n: split[n][j] for n in big}) for j in range(N_CHIPS)])
    R = parts.shape[1]
    half = R // 2
    keep = lax.dynamic_slice_in_dim(parts, my_c * half, half, axis=1)
    give = lax.dynamic_slice_in_dim(parts, (1 - my_c) * half, half, axis=1)
    got = _sibling_exchange(give, "grads_to_sibling")
    chip_sum = _add_n([keep.reshape(-1, PACK_COLS), got.reshape(-1, PACK_COLS)], "grads_add_sibling")
    chip_sum = chip_sum.reshape(N_CHIPS, half, PACK_COLS)
    my_chip = 2 * my_x + my_y
    from_chips = _chip_scatter(chip_sum.astype(BF16), "grads_to_chips")
    own = lax.dynamic_index_in_dim(chip_sum, my_chip, axis=0, keepdims=False)
    g_half = _add_n([own, from_chips[0], from_chips[1], from_chips[2]], "grads_add_chips")
    other_half = _sibling_exchange(g_half, "grads_share_sibling")
    g_pack = jnp.where(my_c == 0, jnp.concatenate([g_half, other_half]), jnp.concatenate([other_half, g_half]))

    small_pack = _pack_small(small)
    small_pack = small_pack.at[8, 0].set(loss_tile[0, 0])
    everyone = _all_to_all_small(small_pack, "small_all_gather")
    small_sum = _add_n([everyone[i] for i in range(8)], "small_sum")
    loss = small_sum[8, 0]
    g_small = _unpack_small(small_sum)

    m_pack = _pack({n: args["m_" + n][0] for n in big})
    v_pack = _pack({n: args["v_" + n][0] for n in big})
    d_pack, nm_pack, nv_pack = _adamw(w_pack, g_pack, m_pack, v_pack, "adamw_big")
    outs = {"grad": _unpack(g_pack, shard_shapes), "delta": _unpack(d_pack, shard_shapes),
            "new_m": _unpack(nm_pack, shard_shapes), "new_v": _unpack(nv_pack, shard_shapes)}
    sw = _pack_small({n: args[n] for n in _SMALL + ("b_f", "rel_bias")})
    sm = _pack_small({n: args["m_" + n] for n in _SMALL + ("b_f", "rel_bias")})
    sv = _pack_small({n: args["v_" + n] for n in _SMALL + ("b_f", "rel_bias")})
    sd, snm, snv = _adamw(sw, small_sum.at[8, 0].set(0.0), sm, sv, "adamw_small")
    souts = {"grad": g_small, "delta": _unpack_small(sd), "new_m": _unpack_small(snm), "new_v": _unpack_small(snv)}

    def leaf(kind, n):
        if n in souts[kind]:
            return souts[kind][n].reshape(args[n].shape)
        return outs[kind][n].reshape(args[n].shape)

    result = [loss, grad_x.reshape(x.shape)]
    for kind in ("grad", "delta", "new_m", "new_v"):
        result += [leaf(kind, n) for n in names]
    return tuple(result)
```

```python
import numpy as np
import jax
import jax.numpy as jnp
from jax import lax
from jax.experimental import pallas as pl
from jax.experimental.pallas import tpu as pltpu

F32 = jnp.float32
BF16 = jnp.bfloat16
MESH_IDS = pl.DeviceIdType.MESH

LANES = 128
HEAD_DIM = 64
N_FOX_HEADS = 8
N_DIL_HEADS = 8
N_MEM_HEADS = 4
FOX_WIDTH = N_FOX_HEADS * HEAD_DIM
DIL_WIDTH = N_DIL_HEADS * HEAD_DIM
DILATIONS = (1, 4, 16)
BAND = 128
BAND_CHUNK_MAX = 8 * BAND
N_BUCKETS = 32
MAX_DISTANCE = 2048
QK_SCALE = HEAD_DIM ** -0.5
RMS_EPS = 1e-6
NEG = -1e30
VMEM_LIMIT = 56 << 20

ADAM_LR = 0.001
ADAM_B1 = 0.9
ADAM_B2 = 0.999
ADAM_EPS = 1e-08
ADAM_WD = 0.01
ADAM_STEP = 10

N_CHIPS = 4
PACK_COLS = 1024


def _params(*sem):
    return pltpu.CompilerParams(dimension_semantics=sem, vmem_limit_bytes=VMEM_LIMIT)


def _fit(n, cap):
    if n <= cap:
        return n
    t = (cap // LANES) * LANES
    while t >= LANES:
        if n % t == 0:
            return t
        t -= LANES
    raise ValueError(f"no lane-aligned tile for {n} under {cap}")


def _dot(a, b, dims):
    return lax.dot_general(a, b, (dims, ((), ())), preferred_element_type=F32)


_NN = ((1,), (0,))
_NT = ((1,), (1,))
_TN = ((0,), (0,))


def _mm(a, b, mode, out_dtype, name, tm=1024, tn=1024, tk=512):
    if mode == "nn":
        (M, K), N = a.shape, b.shape[1]
    elif mode == "nt":
        (M, K), N = a.shape, b.shape[0]
    else:
        (K, M), N = a.shape, b.shape[1]
    tm, tn, tk = _fit(M, tm), _fit(N, tn), _fit(K, tk)
    nk = K // tk
    if mode == "tn":
        a_spec = pl.BlockSpec((tk, tm), lambda i, j, k: (k, i))
    else:
        a_spec = pl.BlockSpec((tm, tk), lambda i, j, k: (i, k))
    if mode == "nt":
        b_spec = pl.BlockSpec((tn, tk), lambda i, j, k: (j, k))
    else:
        b_spec = pl.BlockSpec((tk, tn), lambda i, j, k: (k, j))
    dims = {"nn": _NN, "nt": _NT, "tn": _TN}[mode]

    def body(a_ref, b_ref, o_ref, acc_ref):
        k = pl.program_id(2)

        @pl.when(k == 0)
        def _():
            acc_ref[...] = jnp.zeros_like(acc_ref)

        acc_ref[...] += _dot(a_ref[...].astype(BF16), b_ref[...].astype(BF16), dims)

        @pl.when(k == nk - 1)
        def _():
            o_ref[...] = acc_ref[...].astype(o_ref.dtype)

    return pl.pallas_call(
        body, name=name, grid=(M // tm, N // tn, nk),
        in_specs=[a_spec, b_spec],
        out_specs=pl.BlockSpec((tm, tn), lambda i, j, k: (i, j)),
        out_shape=jax.ShapeDtypeStruct((M, N), out_dtype),
        scratch_shapes=[pltpu.VMEM((tm, tn), F32)],
        compiler_params=_params("parallel", "parallel", "arbitrary"),
    )(a, b)


def _rms_rows(x):
    return lax.rsqrt(jnp.mean(x * x, axis=-1, keepdims=True) + RMS_EPS)


def _rms_fwd(x, g, name, tr=512):
    S, D = x.shape
    tr = _fit(S, tr)

    def body(x_ref, g_ref, h_ref):
        xv = x_ref[...]
        h_ref[...] = (xv * _rms_rows(xv) * g_ref[...]).astype(BF16)

    return pl.pallas_call(
        body, name=name, grid=(S // tr,),
        in_specs=[pl.BlockSpec((tr, D), lambda i: (i, 0)), pl.BlockSpec((1, D), lambda i: (0, 0))],
        out_specs=pl.BlockSpec((tr, D), lambda i: (i, 0)),
        out_shape=jax.ShapeDtypeStruct((S, D), BF16),
        compiler_params=_params("parallel"),
    )(x, g)


def _resid_norm(xres, y, g_post, g_next, name, tr=512):
    S, D = xres.shape
    tr = _fit(S, tr)

    def body(x_ref, y_ref, gp_ref, gn_ref, xn_ref, h_ref):
        yv = y_ref[...]
        xn = x_ref[...] + yv * _rms_rows(yv) * gp_ref[...]
        xn_ref[...] = xn
        h_ref[...] = (xn * _rms_rows(xn) * gn_ref[...]).astype(BF16)

    row = pl.BlockSpec((tr, D), lambda i: (i, 0))
    vec = pl.BlockSpec((1, D), lambda i: (0, 0))
    return pl.pallas_call(
        body, name=name, grid=(S // tr,),
        in_specs=[row, row, vec, vec], out_specs=[row, row],
        out_shape=[jax.ShapeDtypeStruct((S, D), F32), jax.ShapeDtypeStruct((S, D), BF16)],
        compiler_params=_params("parallel"),
    )(xres, y, g_post, g_next)


def _final_loss(xres, y, g_post, target, name, tr=512):
    S, D = xres.shape
    tr = _fit(S, tr)

    def body(x_ref, y_ref, gp_ref, t_ref, d_ref, loss_ref):
        i = pl.program_id(0)
        yv = y_ref[...]
        err = x_ref[...] + yv * _rms_rows(yv) * gp_ref[...] - t_ref[...]
        d_ref[...] = err * (1.0 / D)

        @pl.when(i == 0)
        def _():
            loss_ref[...] = jnp.zeros_like(loss_ref)

        part = jnp.sum(jnp.sum(err * err, axis=1, keepdims=True), axis=0, keepdims=True)
        loss_ref[...] += jnp.broadcast_to(part * (0.5 / D), loss_ref.shape)

    row = pl.BlockSpec((tr, D), lambda i: (i, 0))
    vec = pl.BlockSpec((1, D), lambda i: (0, 0))
    return pl.pallas_call(
        body, name=name, grid=(S // tr,),
        in_specs=[row, row, vec, row],
        out_specs=[row, pl.BlockSpec((8, LANES), lambda i: (0, 0))],
        out_shape=[jax.ShapeDtypeStruct((S, D), F32), jax.ShapeDtypeStruct((8, LANES), F32)],
        compiler_params=_params("arbitrary"),
    )(xres, y, g_post, target)


def _rms_bwd(xin, g, dy, dres, name, want=("f32", "bf16"), tr=512):
    S, D = xin.shape
    tr = _fit(S, tr)
    has_res = dres is not None

    def body(*refs):
        refs = list(refs)
        dg_ref = refs.pop()
        dxb_ref = refs.pop() if "bf16" in want else None
        dx_ref = refs.pop() if "f32" in want else None
        dr_ref = refs.pop() if has_res else None
        x_ref, g_ref, dy_ref = refs
        i = pl.program_id(0)
        xv = x_ref[...]
        dyv = dy_ref[...].astype(F32)
        xhat = xv * _rms_rows(xv)
        dxhat = dyv * g_ref[...]
        r = _rms_rows(xv)
        dx = r * (dxhat - xhat * jnp.mean(dxhat * xhat, axis=-1, keepdims=True))
        if has_res:
            dx = dx + dr_ref[...]
        if dx_ref is not None:
            dx_ref[...] = dx
        if dxb_ref is not None:
            dxb_ref[...] = dx.astype(BF16)

        @pl.when(i == 0)
        def _():
            dg_ref[...] = jnp.zeros_like(dg_ref)

        dg_ref[...] += jnp.broadcast_to(jnp.sum(dyv * xhat, axis=0, keepdims=True), dg_ref.shape)

    row = pl.BlockSpec((tr, D), lambda i: (i, 0))
    vec = pl.BlockSpec((1, D), lambda i: (0, 0))
    acc = pl.BlockSpec((8, D), lambda i: (0, 0))
    ins = [xin, g, dy] + ([dres] if has_res else [])
    dtypes = [dt for key, dt in (("f32", F32), ("bf16", BF16)) if key in want]
    outs = pl.pallas_call(
        body, name=name, grid=(S // tr,),
        in_specs=[row, vec, row] + ([row] if has_res else []),
        out_specs=[row] * len(dtypes) + [acc],
        out_shape=[jax.ShapeDtypeStruct((S, D), dt) for dt in dtypes] + [jax.ShapeDtypeStruct((8, D), F32)],
        compiler_params=_params("arbitrary"),
    )(*ins)
    by_key = dict(zip([key for key in ("f32", "bf16") if key in want], outs[:-1]))
    return by_key.get("f32"), by_key.get("bf16"), outs[-1]


def _tri(n, upper):
    r = lax.broadcasted_iota(jnp.int32, (n, n), 0)
    c = lax.broadcasted_iota(jnp.int32, (n, n), 1)
    return jnp.where((r <= c) if upper else (r >= c), 1.0, 0.0).astype(F32)


def _forget_fwd(fg_t, b_col, name, ts=512):
    H, S = fg_t.shape
    ts = _fit(S, ts)

    def body(f_ref, b_ref, c_ref, carry_ref):
        i = pl.program_id(0)

        @pl.when(i == 0)
        def _():
            carry_ref[...] = jnp.zeros_like(carry_ref)

        z = f_ref[...] + b_ref[...]
        logf = jnp.minimum(z, 0.0) - jnp.log(1.0 + jnp.exp(-jnp.abs(z)))
        run = lax.dot_general(logf, _tri(ts, True), (_NN, ((), ())), precision=lax.Precision.HIGHEST,
                              preferred_element_type=F32) + carry_ref[:, 0:1]
        c_ref[...] = run
        carry_ref[...] = jnp.broadcast_to(
            carry_ref[:, 0:1] + jnp.sum(logf, axis=1, keepdims=True), carry_ref.shape)

    return pl.pallas_call(
        body, name=name, grid=(S // ts,),
        in_specs=[pl.BlockSpec((H, ts), lambda i: (0, i)), pl.BlockSpec((H, 1), lambda i: (0, 0))],
        out_specs=pl.BlockSpec((H, ts), lambda i: (0, i)),
        out_shape=jax.ShapeDtypeStruct((H, S), F32),
        scratch_shapes=[pltpu.VMEM((H, LANES), F32)],
        compiler_params=_params("arbitrary"),
    )(fg_t, b_col)


def _forget_bwd(fg_t, b_col, dc_plus, dc_minus, name, ts=512):
    H, S = fg_t.shape
    ts = _fit(S, ts)
    nb = S // ts

    def body(f_ref, b_ref, dcp_ref, dcm_ref, df_ref, db_ref, carry_ref):
        i = pl.program_id(0)

        @pl.when(i == 0)
        def _():
            carry_ref[...] = jnp.zeros_like(carry_ref)
            db_ref[...] = jnp.zeros_like(db_ref)

        dc = dcp_ref[...] - dcm_ref[...]
        suffix = lax.dot_general(dc, _tri(ts, False), (_NN, ((), ())), precision=lax.Precision.HIGHEST,
                                 preferred_element_type=F32) + carry_ref[:, 0:1]
        z = f_ref[...] + b_ref[...]
        sig_neg = 1.0 / (1.0 + jnp.exp(z))
        df = suffix * sig_neg
        df_ref[...] = df
        carry_ref[...] = jnp.broadcast_to(
            carry_ref[:, 0:1] + jnp.sum(dc, axis=1, keepdims=True), carry_ref.shape)
        db_ref[...] += jnp.broadcast_to(jnp.sum(df, axis=1, keepdims=True), db_ref.shape)

    rev = pl.BlockSpec((H, ts), lambda i: (0, nb - 1 - i))
    return pl.pallas_call(
        body, name=name, grid=(nb,),
        in_specs=[rev, pl.BlockSpec((H, 1), lambda i: (0, 0)), rev, rev],
        out_specs=[rev, pl.BlockSpec((H, LANES), lambda i: (0, 0))],
        out_shape=[jax.ShapeDtypeStruct((H, S), F32), jax.ShapeDtypeStruct((H, LANES), F32)],
        scratch_shapes=[pltpu.VMEM((H, LANES), F32)],
        compiler_params=_params("arbitrary"),
    )(fg_t, b_col, dc_plus, dc_minus)


def _tile_lanes(x, n):
    return x if n == LANES else jnp.tile(x, (1, n // LANES))


def _fox_fwd(qt, k, vt, c_row, c_rep, name, tq=512, tk=1024):
    H, Dh, S = qt.shape
    tk = _fit(S, tk)
    tq = _fit(tk, tq)
    ratio = tk // tq

    def body(qt_ref, k_ref, vt_ref, c_ref, crep_ref, o_ref, lse_ref, m_ref, l_ref, acc_ref,
             sa_ref, sb_ref, ta_ref, tb_ref):
        i = pl.program_id(1)
        qv = qt_ref[...] * QK_SCALE
        cq0 = c_ref[:, pl.ds(pl.multiple_of(i * tq, LANES), LANES)][:, 0:1]
        m_ref[...] = jnp.full_like(m_ref, NEG)
        l_ref[...] = jnp.zeros_like(l_ref)
        acc_ref[...] = jnp.zeros_like(acc_ref)
        n = i // ratio
        q_off = (i - n * ratio) * tq

        def scores(j, s_ref, t_ref, diagonal):
            off = pl.multiple_of(j * tk, LANES)
            s = _dot(k_ref[pl.ds(off, tk), :], qv, _NN) + _tile_lanes(cq0 - crep_ref[pl.ds(off, tk), :], tq)
            if diagonal:
                key = lax.broadcasted_iota(jnp.int32, (tk, tq), 0)
                qry = lax.broadcasted_iota(jnp.int32, (tk, tq), 1) + q_off
                s = jnp.where(key <= qry, s, NEG)
            s_ref[...] = s
            t_ref[...] = jnp.max(s, axis=0, keepdims=True)

        def absorb(j, s_ref, t_ref):
            off = pl.multiple_of(j * tk, LANES)
            m_old = m_ref[...]
            m_new = jnp.maximum(m_old, t_ref[...])
            p = jnp.exp(s_ref[...] - m_new)
            alpha = jnp.exp(m_old - m_new)
            l_ref[...] = alpha * l_ref[...] + jnp.sum(p, axis=0, keepdims=True)
            acc_ref[...] = alpha * acc_ref[...] + _dot(vt_ref[:, pl.ds(off, tk)], p.astype(BF16), _NN)
            m_ref[...] = m_new

        scores(n, sa_ref, ta_ref, True)

        def loop_body(jj, carry):
            scores(2 * jj, sb_ref, tb_ref, False)
            absorb(jnp.where(jj == 0, n, 2 * jj - 1), sa_ref, ta_ref)
            scores(2 * jj + 1, sa_ref, ta_ref, False)
            absorb(2 * jj, sb_ref, tb_ref)
            return carry

        pairs = n // 2
        lax.fori_loop(0, pairs, loop_body, 0)
        held = jnp.where(pairs == 0, n, 2 * pairs - 1)

        @pl.when(n % 2 == 1)
        def _():
            scores(n - 1, sb_ref, tb_ref, False)
            absorb(held, sa_ref, ta_ref)
            absorb(n - 1, sb_ref, tb_ref)

        @pl.when(n % 2 == 0)
        def _():
            absorb(held, sa_ref, ta_ref)

        o_ref[...] = acc_ref[...] / l_ref[...]
        lse_ref[...] = m_ref[...] + jnp.log(l_ref[...]) - cq0

    lanes_full = pl.BlockSpec((None, Dh, S), lambda h, i: (h, 0, 0))
    lanes_tile = pl.BlockSpec((None, Dh, tq), lambda h, i: (h, 0, i))
    return pl.pallas_call(
        body, name=name, grid=(H, S // tq),
        in_specs=[lanes_tile, pl.BlockSpec((None, S, Dh), lambda h, i: (h, 0, 0)), lanes_full,
                  pl.BlockSpec((None, 1, S), lambda h, i: (h, 0, 0)),
                  pl.BlockSpec((None, S, LANES), lambda h, i: (h, 0, 0))],
        out_specs=[lanes_tile, pl.BlockSpec((None, 1, tq), lambda h, i: (h, 0, i))],
        out_shape=[jax.ShapeDtypeStruct((H, Dh, S), F32), jax.ShapeDtypeStruct((H, 1, S), F32)],
        scratch_shapes=[pltpu.VMEM((1, tq), F32), pltpu.VMEM((1, tq), F32), pltpu.VMEM((Dh, tq), F32),
                        pltpu.VMEM((tk, tq), F32), pltpu.VMEM((tk, tq), F32),
                        pltpu.VMEM((1, tq), F32), pltpu.VMEM((1, tq), F32)],
        compiler_params=_params("parallel", "arbitrary"),
    )(qt, k, vt, c_row, c_rep)


def _fox_bwd(qt, k, kt, v, dot, c_rep, lse_row, delta_row, name, tq=1024, tk=512):
    H, Dh, S = qt.shape
    tq = _fit(S, tq)
    tk = _fit(tq, tk)
    ratio = tq // tk
    nq = S // tq

    def body(k_ref, kt_ref, v_ref, crep_ref, qt_ref, dot_ref, lse_ref, dl_ref,
             dqt_ref, dkt_ref, dvt_ref, dcs_ref, drs_ref, dka_ref, dva_ref, dca_ref):
        j = pl.program_id(1)

        @pl.when(j == 0)
        def _():
            dqt_ref[...] = jnp.zeros_like(dqt_ref)
            drs_ref[...] = jnp.zeros_like(drs_ref)

        kv = k_ref[...]
        ktv = kt_ref[...]
        vv = v_ref[...]
        c_col = _tile_lanes(crep_ref[...], tq)
        dka_ref[...] = jnp.zeros_like(dka_ref)
        dva_ref[...] = jnp.zeros_like(dva_ref)
        dca_ref[...] = jnp.zeros_like(dca_ref)
        i_diag = j // ratio
        k_off = (j - i_diag * ratio) * tk

        def step(i, diagonal):
            off = pl.multiple_of(i * tq, LANES)
            qv = qt_ref[:, pl.ds(off, tq)] * QK_SCALE
            dov = dot_ref[:, pl.ds(off, tq)]
            e = _dot(kv, qv, _NN) - lse_ref[:, pl.ds(off, tq)] - c_col
            if diagonal:
                key = lax.broadcasted_iota(jnp.int32, (tk, tq), 0) + k_off
                qry = lax.broadcasted_iota(jnp.int32, (tk, tq), 1)
                e = jnp.where(key <= qry, e, NEG)
            p_t = jnp.exp(e)
            dva_ref[...] += _dot(dov, p_t.astype(BF16), _NT)
            ds_t = p_t * (_dot(vv, dov, _NN) - dl_ref[:, pl.ds(off, tq)])
            ds_b = ds_t.astype(BF16)
            dka_ref[...] += _dot(qv, ds_b, _NT)
            dqt_ref[:, pl.ds(off, tq)] += _dot(ktv, ds_b, _NN) * QK_SCALE
            drs_ref[:, pl.ds(off, tq)] += jnp.sum(ds_t, axis=0, keepdims=True)
            part = ds_t[:, 0:LANES]
            for cidx in range(1, tq // LANES):
                part = part + ds_t[:, cidx * LANES:(cidx + 1) * LANES]
            dca_ref[...] += part

        step(i_diag, True)

        def loop_body(i, carry):
            step(i, False)
            return carry

        lax.fori_loop(i_diag + 1, nq, loop_body, 0)
        dkt_ref[...] = dka_ref[...]
        dvt_ref[...] = dva_ref[...]
        dcs_ref[...] = jnp.broadcast_to(jnp.sum(dca_ref[...], axis=1, keepdims=True), dcs_ref.shape)

    rows_tile = pl.BlockSpec((None, tk, Dh), lambda h, j: (h, j, 0))
    lanes_tile = pl.BlockSpec((None, Dh, tk), lambda h, j: (h, 0, j))
    rep = pl.BlockSpec((None, tk, LANES), lambda h, j: (h, j, 0))
    lanes_full = pl.BlockSpec((None, Dh, S), lambda h, j: (h, 0, 0))
    rowv = pl.BlockSpec((None, 1, S), lambda h, j: (h, 0, 0))
    return pl.pallas_call(
        body, name=name, grid=(H, S // tk),
        in_specs=[rows_tile, lanes_tile, rows_tile, rep, lanes_full, lanes_full, rowv, rowv],
        out_specs=[lanes_full, lanes_tile, lanes_tile, rep, rowv],
        out_shape=[jax.ShapeDtypeStruct((H, Dh, S), F32), jax.ShapeDtypeStruct((H, Dh, S), F32),
                   jax.ShapeDtypeStruct((H, Dh, S), F32), jax.ShapeDtypeStruct((H, S, LANES), F32),
                   jax.ShapeDtypeStruct((H, 1, S), F32)],
        scratch_shapes=[pltpu.VMEM((Dh, tk), F32), pltpu.VMEM((Dh, tk), F32), pltpu.VMEM((tk, LANES), F32)],
        compiler_params=_params("parallel", "arbitrary"),
    )(k, kt, v, c_rep, qt, dot, lse_row, delta_row)


DIL_Q_BLOCK = 3 * FOX_WIDTH // LANES
HEAD_PAIRS = N_DIL_HEADS // 2
PAIR_BLOCKS = DIL_WIDTH // LANES


def _band_geometry(S, d):
    L = S // d
    chunk = min(BAND_CHUNK_MAX, L)
    assert L % chunk == 0 and chunk % BAND == 0
    return L, chunk, chunk // BAND, L // chunk


def _band_in_specs(S, d, base):
    L, chunk, nb, _ = _band_geometry(S, d)

    def col(kind):
        return lambda hp, r, i: (r, i, base + kind * PAIR_BLOCKS + hp)

    def col_prev(kind):
        return lambda hp, r, i: (r, jnp.maximum(i * nb - 1, 0), base + kind * PAIR_BLOCKS + hp)

    main = [pl.BlockSpec((None, chunk, LANES), col(kind)) for kind in range(3)]
    prev = [pl.BlockSpec((None, BAND, LANES), col_prev(kind)) for kind in range(3)]
    bias = pl.BlockSpec((None, 2, 2 * BAND, BAND), lambda hp, r, i: (hp, 0, 0, 0))
    stat = pl.BlockSpec((None, 2, chunk), lambda hp, r, i: (hp, 0, r * (L // chunk) + i))
    tok = pl.BlockSpec((None, chunk, LANES), lambda hp, r, i: (r, i, hp))
    return main, prev, bias, stat, tok


def _to_residues(x, col_block, width, d, name, tr=512):
    S = x.shape[0]
    tr = _fit(S, tr)

    def body(x_ref, o_ref, tmp_ref):
        for j in range(width // LANES):
            cols = slice(j * LANES, (j + 1) * LANES)
            tmp_ref[j] = x_ref[:, cols].astype(F32)
            for r in range(d):
                o_ref[r, :, cols] = tmp_ref[j, pl.ds(r, tr // d, stride=d), :].astype(o_ref.dtype)

    return pl.pallas_call(
        body, name=name, grid=(S // tr,),
        in_specs=[pl.BlockSpec((tr, width), lambda i: (i, col_block))],
        out_specs=pl.BlockSpec((d, tr // d, width), lambda i: (0, i, 0)),
        out_shape=jax.ShapeDtypeStruct((d, S // d, width), x.dtype),
        scratch_shapes=[pltpu.VMEM((width // LANES, tr, LANES), F32)],
        compiler_params=_params("parallel"),
    )(x)


def _token_rows(ref, cols, tmp_ref):
    if len(ref.shape) == 2:
        return ref[:, cols].astype(F32)
    d, rows = ref.shape[0], ref.shape[1]
    for r in range(d):
        tmp_ref[pl.ds(r, rows, stride=d), :] = ref[r, :, cols].astype(F32)
    return tmp_ref[...]


def _row_spec(t, tr):
    if t.ndim == 2:
        return pl.BlockSpec((tr, t.shape[1]), lambda i: (i, 0))
    d = t.shape[0]
    return pl.BlockSpec((d, tr // d, t.shape[2]), lambda i: (0, i, 0))


def _band_scores_t(kb, qb, bias_t, first):
    s = _dot(kb, qb, _NT) + bias_t
    if first is not None:
        key = lax.broadcasted_iota(jnp.int32, s.shape, 0)
        s = jnp.where(jnp.logical_and(first, key < BAND), NEG, s)
    return s


def _dil_lse(qkv_v, base, bias_t, name):
    d, L = qkv_v.shape[:2]
    S = L * d
    _, chunk, nb, nchunks = _band_geometry(S, d)
    main, prev, bias, stat, _ = _band_in_specs(S, d, base)

    def body(q_ref, k_ref, kp_ref, b_ref, lse_ref, kext_ref):
        first = pl.program_id(2) == 0
        kext_ref[0:BAND, :] = kp_ref[...]
        kext_ref[BAND:, :] = k_ref[...]
        for a in range(2):
            lanes = slice(a * HEAD_DIM, (a + 1) * HEAD_DIM)
            bias_v = b_ref[a]
            for b in range(nb):
                s = _band_scores_t(kext_ref[b * BAND:(b + 2) * BAND, lanes],
                                   q_ref[b * BAND:(b + 1) * BAND, lanes] * QK_SCALE, bias_v, first if b == 0 else None)
                m = jnp.max(s, axis=0, keepdims=True)
                lse_ref[a:a + 1, b * BAND:(b + 1) * BAND] = m + jnp.log(jnp.sum(jnp.exp(s - m), axis=0, keepdims=True))

    return pl.pallas_call(
        body, name=name, grid=(HEAD_PAIRS, d, nchunks),
        in_specs=[main[0], main[1], prev[1], bias], out_specs=stat,
        out_shape=jax.ShapeDtypeStruct((HEAD_PAIRS, 2, S), F32),
        scratch_shapes=[pltpu.VMEM((chunk + BAND, LANES), BF16)],
        compiler_params=_params("parallel", "parallel", "parallel"),
    )(qkv_v, qkv_v, qkv_v, bias_t)


def _dil_out(qkv_v, base, bias_t, lse_joint, name):
    d, L = qkv_v.shape[:2]
    S = L * d
    _, chunk, nb, nchunks = _band_geometry(S, d)
    main, prev, bias, stat, tok = _band_in_specs(S, d, base)

    def body(q_ref, k_ref, kp_ref, v_ref, vp_ref, b_ref, lse_ref, o_ref, kext_ref, vext_ref):
        first = pl.program_id(2) == 0
        kext_ref[0:BAND, :] = kp_ref[...]
        kext_ref[BAND:, :] = k_ref[...]
        vext_ref[0:BAND, :] = vp_ref[...]
        vext_ref[BAND:, :] = v_ref[...]
        for a in range(2):
            lanes = slice(a * HEAD_DIM, (a + 1) * HEAD_DIM)
            bias_v = b_ref[a]
            for b in range(nb):
                rows, ext = slice(b * BAND, (b + 1) * BAND), slice(b * BAND, (b + 2) * BAND)
                s = _band_scores_t(kext_ref[ext, lanes], q_ref[rows, lanes] * QK_SCALE, bias_v, first if b == 0 else None)
                p_t = jnp.exp(s - lse_ref[a:a + 1, rows])
                o_ref[rows, lanes] = _dot(p_t.astype(BF16), vext_ref[ext, lanes], _TN).astype(BF16)

    return pl.pallas_call(
        body, name=name, grid=(HEAD_PAIRS, d, nchunks),
        in_specs=[main[0], main[1], prev[1], main[2], prev[2], bias, stat], out_specs=tok,
        out_shape=jax.ShapeDtypeStruct((d, L, DIL_WIDTH), BF16),
        scratch_shapes=[pltpu.VMEM((chunk + BAND, LANES), BF16), pltpu.VMEM((chunk + BAND, LANES), BF16)],
        compiler_params=_params("parallel", "parallel", "parallel"),
    )(qkv_v, qkv_v, qkv_v, qkv_v, qkv_v, bias_t, lse_joint)


def _dil_bwd(qkv_v, base, do_v, bias_t, lse_joint, delta, name):
    d, L = qkv_v.shape[:2]
    S = L * d
    _, chunk, nb, nchunks = _band_geometry(S, d)
    main, prev, bias, stat, tok = _band_in_specs(S, d, base)
    nblocks = L // BAND

    def nxt_row(i):
        return jnp.minimum((i + 1) * nb, nblocks - 1)

    q_next = pl.BlockSpec((None, BAND, LANES), lambda hp, r, i: (r, nxt_row(i), base + hp))
    do_next = pl.BlockSpec((None, BAND, LANES), lambda hp, r, i: (r, nxt_row(i), hp))
    stat_next = pl.BlockSpec((None, 2, BAND), lambda hp, r, i: (hp, 0, r * nblocks + nxt_row(i)))

    def body(q_ref, k_ref, kp_ref, v_ref, vp_ref, do_ref, b_ref, lse_ref, dl_ref,
             qn_ref, don_ref, lsen_ref, dln_ref,
             dq_ref, dk_ref, dv_ref, db_ref, kext_ref, vext_ref, dkext_ref, dvext_ref):
        r, i = pl.program_id(1), pl.program_id(2)
        first = i == 0
        has_next = i + 1 < nchunks
        tail = slice(BAND + chunk, 2 * BAND + chunk)
        kext_ref[0:BAND, :] = kp_ref[...]
        kext_ref[BAND:BAND + chunk, :] = k_ref[...]
        kext_ref[tail, :] = jnp.zeros((BAND, LANES), BF16)
        vext_ref[0:BAND, :] = vp_ref[...]
        vext_ref[BAND:BAND + chunk, :] = v_ref[...]
        vext_ref[tail, :] = jnp.zeros((BAND, LANES), BF16)
        dkext_ref[...] = jnp.zeros_like(dkext_ref)
        dvext_ref[...] = jnp.zeros_like(dvext_ref)

        @pl.when(jnp.logical_and(r == 0, i == 0))
        def _():
            db_ref[...] = jnp.zeros_like(db_ref)

        def block(a, qb, dob, lse_row, dl_row, ext, mask_rows):
            lanes = slice(a * HEAD_DIM, (a + 1) * HEAD_DIM)
            kb, vb = kext_ref[ext, lanes], vext_ref[ext, lanes]
            s = _dot(kb, qb, _NT) + b_ref[a]
            if mask_rows is not None:
                s = jnp.where(mask_rows, NEG, s)
            p_t = jnp.exp(s - lse_row)
            ds_t = p_t * (_dot(vb, dob, _NT) - dl_row)
            ds_b = ds_t.astype(BF16)
            dkext_ref[ext, lanes] += _dot(ds_b, qb, _NN)
            dvext_ref[ext, lanes] += _dot(p_t.astype(BF16), dob, _NN)
            return ds_t, ds_b, kb

        key = lax.broadcasted_iota(jnp.int32, (2 * BAND, BAND), 0)
        for a in range(2):
            lanes = slice(a * HEAD_DIM, (a + 1) * HEAD_DIM)
            db_acc = jnp.zeros((2 * BAND, BAND), F32)
            for b in range(nb):
                rows, ext = slice(b * BAND, (b + 1) * BAND), slice(b * BAND, (b + 2) * BAND)
                mask = jnp.logical_and(first, key < BAND) if b == 0 else None
                ds_t, ds_b, kb = block(a, q_ref[rows, lanes] * QK_SCALE, do_ref[rows, lanes],
                                       lse_ref[a:a + 1, rows], dl_ref[a:a + 1, rows], ext, mask)
                dq_ref[rows, lanes] = _dot(ds_b, kb, _TN) * QK_SCALE
                db_acc = db_acc + ds_t
            db_ref[a] += db_acc
            block(a, qn_ref[:, lanes] * QK_SCALE, don_ref[:, lanes], lsen_ref[a:a + 1, :], dln_ref[a:a + 1, :],
                  slice(chunk, chunk + 2 * BAND), jnp.logical_or(jnp.logical_not(has_next), key >= BAND))
        dk_ref[...] = dkext_ref[BAND:BAND + chunk, :]
        dv_ref[...] = dvext_ref[BAND:BAND + chunk, :]

    ext_rows = chunk + 2 * BAND
    return pl.pallas_call(
        body, name=name, grid=(HEAD_PAIRS, d, nchunks),
        in_specs=[main[0], main[1], prev[1], main[2], prev[2], tok, bias, stat, stat,
                  q_next, do_next, stat_next, stat_next],
        out_specs=[tok, tok, tok, bias],
        out_shape=[jax.ShapeDtypeStruct((d, L, DIL_WIDTH), F32)] * 3
                  + [jax.ShapeDtypeStruct((HEAD_PAIRS, 2, 2 * BAND, BAND), F32)],
        scratch_shapes=[pltpu.VMEM((ext_rows, LANES), BF16), pltpu.VMEM((ext_rows, LANES), BF16),
                        pltpu.VMEM((ext_rows, LANES), F32), pltpu.VMEM((ext_rows, LANES), F32)],
        compiler_params=_params("arbitrary", "arbitrary", "arbitrary"),
    )(qkv_v, qkv_v, qkv_v, qkv_v, qkv_v, do_v, bias_t, lse_joint, delta, qkv_v, do_v, lse_joint, delta)


def _lse_join(lse3, name):
    P, H, S = lse3.shape

    def body(l_ref, o_ref):
        a, b, c = l_ref[0], l_ref[1], l_ref[2]
        m = jnp.maximum(jnp.maximum(a, b), c)
        o_ref[...] = m + jnp.log(jnp.exp(a - m) + jnp.exp(b - m) + jnp.exp(c - m))

    return pl.pallas_call(body, name=name, out_shape=jax.ShapeDtypeStruct((H, S), F32))(lse3)


def _bucket_reduce(dbias_t, bucket_map_t, name):
    P, H = dbias_t.shape[:2]

    def body(db_ref, bk_ref, o_ref):
        p, h = pl.program_id(0), pl.program_id(1)

        @pl.when(jnp.logical_and(p == 0, h == 0))
        def _():
            o_ref[...] = jnp.zeros_like(o_ref)

        db, bk = db_ref[...], bk_ref[...]
        row = lax.broadcasted_iota(jnp.int32, (N_BUCKETS, LANES), 0)
        lane = lax.broadcasted_iota(jnp.int32, (N_BUCKETS, LANES), 1)

        def one(b, acc):
            val = jnp.sum(jnp.sum(jnp.where(bk == b, db, 0.0), axis=1, keepdims=True), axis=0, keepdims=True)
            return acc + jnp.where(jnp.logical_and(row == b, lane == h), val, 0.0)

        o_ref[...] += lax.fori_loop(0, N_BUCKETS, one, jnp.zeros((N_BUCKETS, LANES), F32))

    return pl.pallas_call(
        body, name=name, grid=(P, H),
        in_specs=[pl.BlockSpec((None, None, 2 * BAND, BAND), lambda p, h: (p, h, 0, 0)),
                  pl.BlockSpec((None, 2 * BAND, BAND), lambda p, h: (p, 0, 0))],
        out_specs=pl.BlockSpec((N_BUCKETS, LANES), lambda p, h: (0, 0)),
        out_shape=jax.ShapeDtypeStruct((N_BUCKETS, LANES), F32),
        compiler_params=_params("arbitrary", "arbitrary"),
    )(dbias_t, bucket_map_t)


def _mem_fwd(q, kv, name, tq=1024):
    S, W = q.shape
    N = kv.shape[0]
    pairs = W // LANES
    tq = _fit(S, tq)

    def body(q_ref, k_ref, v_ref, o_ref, lse_ref):
        for a in range(2):
            lanes = slice(a * HEAD_DIM, (a + 1) * HEAD_DIM)
            s = _dot(k_ref[:, lanes], q_ref[:, lanes] * QK_SCALE, _NT)
            m = jnp.max(s, axis=0, keepdims=True)
            e = jnp.exp(s - m)
            l = jnp.sum(e, axis=0, keepdims=True)
            o_ref[:, lanes] = _dot((e / l).astype(BF16), v_ref[:, lanes], _TN).astype(BF16)
            lse_ref[a:a + 1, :] = m + jnp.log(l)

    return pl.pallas_call(
        body, name=name, grid=(pairs, S // tq),
        in_specs=[pl.BlockSpec((tq, LANES), lambda hp, i: (i, hp)),
                  pl.BlockSpec((N, LANES), lambda hp, i: (0, hp)),
                  pl.BlockSpec((N, LANES), lambda hp, i: (0, pairs + hp))],
        out_specs=[pl.BlockSpec((tq, LANES), lambda hp, i: (i, hp)),
                   pl.BlockSpec((None, 2, tq), lambda hp, i: (hp, 0, i))],
        out_shape=[jax.ShapeDtypeStruct((S, W), BF16), jax.ShapeDtypeStruct((pairs, 2, S), F32)],
        compiler_params=_params("parallel", "parallel"),
    )(q, kv, kv)


def _mem_bwd(q, kv, do, lse, delta, name, tq=1024):
    S, W = q.shape
    N = kv.shape[0]
    pairs = W // LANES
    tq = _fit(S, tq)

    def body(q_ref, k_ref, v_ref, do_ref, lse_ref, dl_ref, dq_ref, dk_ref, dv_ref):
        i = pl.program_id(1)

        @pl.when(i == 0)
        def _():
            dk_ref[...] = jnp.zeros_like(dk_ref)
            dv_ref[...] = jnp.zeros_like(dv_ref)

        for a in range(2):
            lanes = slice(a * HEAD_DIM, (a + 1) * HEAD_DIM)
            qv, dov = q_ref[:, lanes] * QK_SCALE, do_ref[:, lanes]
            kv_, vv = k_ref[:, lanes], v_ref[:, lanes]
            p_t = jnp.exp(_dot(kv_, qv, _NT) - lse_ref[a:a + 1, :])
            ds_t = p_t * (_dot(vv, dov, _NT) - dl_ref[a:a + 1, :])
            ds_b = ds_t.astype(BF16)
            dq_ref[:, lanes] = (_dot(ds_b, kv_, _TN) * QK_SCALE).astype(BF16)
            dk_ref[:, lanes] += _dot(ds_b, qv, _NN)
            dv_ref[:, lanes] += _dot(p_t.astype(BF16), dov, _NN)

    qs = pl.BlockSpec((tq, LANES), lambda hp, i: (i, hp))
    stat = pl.BlockSpec((None, 2, tq), lambda hp, i: (hp, 0, i))
    acc = pl.BlockSpec((N, LANES), lambda hp, i: (0, hp))
    return pl.pallas_call(
        body, name=name, grid=(pairs, S // tq),
        in_specs=[qs, acc, pl.BlockSpec((N, LANES), lambda hp, i: (0, pairs + hp)), qs, stat, stat],
        out_specs=[qs, acc, acc],
        out_shape=[jax.ShapeDtypeStruct((S, W), BF16), jax.ShapeDtypeStruct((N, W), F32),
                   jax.ShapeDtypeStruct((N, W), F32)],
        compiler_params=_params("parallel", "arbitrary"),
    )(q, kv, kv, do, lse, delta)


def _head_rowdot(a, bs, name, tr=512):
    S, W = a.shape
    tr = _fit(S, tr)

    def body(*refs):
        a_ref, b_refs, o_ref, tmp_ref = refs[0], refs[1:-2], refs[-2], refs[-1]
        col = lax.broadcasted_iota(jnp.int32, (LANES, LANES), 0)
        lane = lax.broadcasted_iota(jnp.int32, (LANES, LANES), 1)
        acc = jnp.zeros((tr, LANES), F32)
        for j in range(W // LANES):
            cols = slice(j * LANES, (j + 1) * LANES)
            tot = _token_rows(b_refs[0], cols, tmp_ref)
            for r in b_refs[1:]:
                tot = tot + _token_rows(r, cols, tmp_ref)
            sel = jnp.where(col // HEAD_DIM + j * (LANES // HEAD_DIM) == lane, 1.0, 0.0).astype(F32)
            acc = acc + lax.dot_general(a_ref[:, cols].astype(F32) * tot, sel, (_NN, ((), ())),
                                        precision=lax.Precision.HIGHEST, preferred_element_type=F32)
        o_ref[...] = acc

    return pl.pallas_call(
        body, name=name, grid=(S // tr,), in_specs=[_row_spec(t, tr) for t in [a] + list(bs)],
        out_specs=pl.BlockSpec((tr, LANES), lambda i: (i, 0)),
        out_shape=jax.ShapeDtypeStruct((S, LANES), F32),
        scratch_shapes=[pltpu.VMEM((tr, LANES), F32)],
        compiler_params=_params("parallel"),
    )(a, *bs)


def _sum_cast_cols(groups, out_dtype, name, tr=256):
    first = groups[0][0]
    S, W = (first.shape if first.ndim == 2 else (first.shape[0] * first.shape[1], first.shape[2]))
    tr = _fit(S, tr)
    flat = [t for g in groups for t in g]

    def body(*refs):
        o_ref, tmp_ref = refs[-2], refs[-1]
        k = 0
        for gi, g in enumerate(groups):
            for j in range(W // LANES):
                cols = slice(j * LANES, (j + 1) * LANES)
                acc = _token_rows(refs[k], cols, tmp_ref)
                for r in refs[k + 1:k + len(g)]:
                    acc = acc + _token_rows(r, cols, tmp_ref)
                o_ref[:, gi * W + j * LANES:gi * W + (j + 1) * LANES] = acc.astype(out_dtype)
            k += len(g)

    return pl.pallas_call(
        body, name=name, grid=(S // tr,), in_specs=[_row_spec(t, tr) for t in flat],
        out_specs=pl.BlockSpec((tr, W * len(groups)), lambda i: (i, 0)),
        out_shape=jax.ShapeDtypeStruct((S, W * len(groups)), out_dtype),
        scratch_shapes=[pltpu.VMEM((tr, LANES), F32)],
        compiler_params=_params("parallel"),
    )(*flat)


FF_TILE = 256


def _ffn_up(h, w_gu, name, tm=1024):
    S, D = h.shape
    F2 = w_gu.shape[1]
    tm = _fit(S, tm)

    def body(h_ref, w_ref, gu_ref, act_ref):
        gu = _dot(h_ref[...], w_ref[...], _NN)
        gu_ref[...] = gu.astype(BF16)
        g, u = gu[:, :FF_TILE], gu[:, FF_TILE:]
        act_ref[...] = (g * (1.0 / (1.0 + jnp.exp(-g))) * u).astype(BF16)

    return pl.pallas_call(
        body, name=name, grid=(S // tm, F2 // (2 * FF_TILE)),
        in_specs=[pl.BlockSpec((tm, D), lambda i, j: (i, 0)), pl.BlockSpec((D, 2 * FF_TILE), lambda i, j: (0, j))],
        out_specs=[pl.BlockSpec((tm, 2 * FF_TILE), lambda i, j: (i, j)),
                   pl.BlockSpec((tm, FF_TILE), lambda i, j: (i, j))],
        out_shape=[jax.ShapeDtypeStruct((S, F2), BF16), jax.ShapeDtypeStruct((S, F2 // 2), BF16)],
        compiler_params=_params("parallel", "arbitrary"),
    )(h, w_gu)


def _ffn_dact(dy, w_down, gu, name, tm=1024):
    S, D = dy.shape
    F2 = gu.shape[1]
    tm = _fit(S, tm)

    def body(dy_ref, w_ref, gu_ref, dgu_ref):
        dact = _dot(dy_ref[...], w_ref[...], _NT)
        gu_v = gu_ref[...].astype(F32)
        g, u = gu_v[:, :FF_TILE], gu_v[:, FF_TILE:]
        sig = 1.0 / (1.0 + jnp.exp(-g))
        silu = g * sig
        dgu_ref[:, :FF_TILE] = (dact * u * (sig + silu * (1.0 - sig))).astype(BF16)
        dgu_ref[:, FF_TILE:] = (dact * silu).astype(BF16)

    return pl.pallas_call(
        body, name=name, grid=(S // tm, F2 // (2 * FF_TILE)),
        in_specs=[pl.BlockSpec((tm, D), lambda i, j: (i, 0)), pl.BlockSpec((FF_TILE, D), lambda i, j: (j, 0)),
                  pl.BlockSpec((tm, 2 * FF_TILE), lambda i, j: (i, j))],
        out_specs=pl.BlockSpec((tm, 2 * FF_TILE), lambda i, j: (i, j)),
        out_shape=jax.ShapeDtypeStruct((S, F2), BF16),
        compiler_params=_params("parallel", "arbitrary"),
    )(dy, w_down, gu)


def _fit_rows(n, cap):
    if n <= cap:
        return n
    t = (cap // 8) * 8
    while t >= 8:
        if n % t == 0:
            return t
        t -= 8
    raise ValueError(f"no sublane-aligned tile for {n} under {cap}")


def _add_n(arrs, name, tr=512):
    R, C = arrs[0].shape
    tr = _fit_rows(R, tr)

    def body(*refs):
        acc = refs[0][...]
        for r in refs[1:-1]:
            acc = acc + r[...]
        refs[-1][...] = acc

    row = pl.BlockSpec((tr, C), lambda i: (i, 0))
    return pl.pallas_call(
        body, name=name, grid=(R // tr,), in_specs=[row] * len(arrs), out_specs=row,
        out_shape=jax.ShapeDtypeStruct((R, C), F32), compiler_params=_params("parallel"),
    )(*arrs)


def _adamw(w, g, m, v, name, tr=512):
    R, C = w.shape
    tr = _fit_rows(R, tr)
    c1 = 1.0 / (1.0 - ADAM_B1 ** ADAM_STEP)
    c2 = 1.0 / (1.0 - ADAM_B2 ** ADAM_STEP)

    def body(w_ref, g_ref, m_ref, v_ref, d_ref, nm_ref, nv_ref):
        gv = g_ref[...]
        nm = ADAM_B1 * m_ref[...] + (1.0 - ADAM_B1) * gv
        nv = ADAM_B2 * v_ref[...] + (1.0 - ADAM_B2) * (gv * gv)
        nm_ref[...] = nm
        nv_ref[...] = nv
        d_ref[...] = -ADAM_LR * ((nm * c1) / (jnp.sqrt(nv * c2) + ADAM_EPS) + ADAM_WD * w_ref[...])

    row = pl.BlockSpec((tr, C), lambda i: (i, 0))
    return pl.pallas_call(
        body, name=name, grid=(R // tr,), in_specs=[row] * 4, out_specs=[row] * 3,
        out_shape=[jax.ShapeDtypeStruct((R, C), F32)] * 3, compiler_params=_params("parallel"),
    )(w, g, m, v)


def _place():
    return lax.axis_index("x"), lax.axis_index("y"), lax.axis_index("c")


_ANY = pl.BlockSpec(memory_space=pl.ANY)


def _chip_all_gather(shard, name):
    R, C = shard.shape
    half = R // 2

    def body(x_ref, out_ref, send_sems, recv_sems, local_sem):
        x, y, c = _place()
        chips = [(1 - x, y), (x, 1 - y), (1 - x, 1 - y)]
        sibling = (x, y, 1 - c)
        mine = pltpu.make_async_copy(x_ref, out_ref.at[2 * x + y], local_sem)
        mine.start()

        def rows(chip, core):
            return out_ref.at[chip, pl.ds(core * half, half)]

        def copy(k, chip, core, to, src=None):
            return pltpu.make_async_remote_copy(
                src_ref=rows(chip, core) if src is None else src, dst_ref=rows(chip, core),
                send_sem=send_sems.at[k], recv_sem=recv_sems.at[k], device_id=to, device_id_type=MESH_IDS)

        me = 2 * x + y
        first = [copy(k, me, c, (cx, cy, c), src=x_ref.at[pl.ds(c * half, half)]) for k, (cx, cy) in enumerate(chips)]
        for cp in first:
            cp.start()
        passed = [copy(3 + k, 2 * cx + cy, c, sibling) for k, (cx, cy) in enumerate(chips)]
        for k, (cx, cy) in enumerate(chips):
            copy(k, 2 * cx + cy, c, (cx, cy, c)).wait_recv()
            passed[k].start()
        for k, (cx, cy) in enumerate(chips):
            copy(3 + k, 2 * cx + cy, 1 - c, sibling).wait_recv()
        for cp in first + passed:
            cp.wait_send()
        mine.wait()

    return pl.pallas_call(
        body, name=name, in_specs=[_ANY], out_specs=_ANY,
        out_shape=jax.ShapeDtypeStruct((N_CHIPS, R, C), shard.dtype),
        scratch_shapes=[pltpu.SemaphoreType.DMA((6,)), pltpu.SemaphoreType.DMA((6,)), pltpu.SemaphoreType.DMA],
    )(shard)


def _sibling_exchange(buf, name):
    def body(x_ref, out_ref, send_sem, recv_sem):
        x, y, c = _place()
        cp = pltpu.make_async_remote_copy(
            src_ref=x_ref, dst_ref=out_ref, send_sem=send_sem, recv_sem=recv_sem,
            device_id=(x, y, 1 - c), device_id_type=MESH_IDS)
        cp.start()
        cp.wait()

    return pl.pallas_call(
        body, name=name, in_specs=[_ANY], out_specs=_ANY,
        out_shape=jax.ShapeDtypeStruct(buf.shape, buf.dtype),
        scratch_shapes=[pltpu.SemaphoreType.DMA, pltpu.SemaphoreType.DMA],
    )(buf)


def _chip_scatter(parts, name):
    _, R, C = parts.shape

    def body(p_ref, out_ref, send_sems, recv_sems):
        x, y, c = _place()
        chips = [(1 - x, y), (x, 1 - y), (1 - x, 1 - y)]

        def copy(k, slab, to):
            return pltpu.make_async_remote_copy(
                src_ref=p_ref.at[slab], dst_ref=out_ref.at[k], send_sem=send_sems.at[k], recv_sem=recv_sems.at[k],
                device_id=to, device_id_type=MESH_IDS)

        sends = [copy(k, 2 * cx + cy, (cx, cy, c)) for k, (cx, cy) in enumerate(chips)]
        for cp in sends:
            cp.start()
        for cp in sends:
            cp.wait_recv()
        for cp in sends:
            cp.wait_send()

    return pl.pallas_call(
        body, name=name, in_specs=[_ANY], out_specs=_ANY,
        out_shape=jax.ShapeDtypeStruct((3, R, C), parts.dtype),
        scratch_shapes=[pltpu.SemaphoreType.DMA((3,)), pltpu.SemaphoreType.DMA((3,))],
    )(parts)


def _all_to_all_small(vec, name):
    R, C = vec.shape

    def body(v_ref, out_ref, send_sems, recv_sems, local_sem):
        x, y, c = _place()
        me = 4 * x + 2 * y + c
        mine = pltpu.make_async_copy(v_ref, out_ref.at[me], local_sem)
        mine.start()
        flips = [(dx, dy, dc) for dx in (0, 1) for dy in (0, 1) for dc in (0, 1)][1:]

        def peer(f):
            return (x ^ f[0], y ^ f[1], c ^ f[2])

        def copy(k, slot, to):
            return pltpu.make_async_remote_copy(
                src_ref=v_ref, dst_ref=out_ref.at[slot], send_sem=send_sems.at[k], recv_sem=recv_sems.at[k],
                device_id=to, device_id_type=MESH_IDS)

        sends = [copy(k, me, peer(f)) for k, f in enumerate(flips)]
        for cp in sends:
            cp.start()
        for k, f in enumerate(flips):
            px, py, pc = peer(f)
            copy(k, 4 * px + 2 * py + pc, peer(f)).wait_recv()
        for cp in sends:
            cp.wait_send()
        mine.wait()

    return pl.pallas_call(
        body, name=name, in_specs=[_ANY], out_specs=_ANY,
        out_shape=jax.ShapeDtypeStruct((8, R, C), vec.dtype),
        scratch_shapes=[pltpu.SemaphoreType.DMA((7,)), pltpu.SemaphoreType.DMA((7,)), pltpu.SemaphoreType.DMA],
    )(vec)


def _to_heads(t, n):
    S = t.shape[0]
    return t.reshape(S, n, HEAD_DIM).transpose(1, 0, 2)


def _to_heads_t(t, n):
    S = t.shape[0]
    return t.T.reshape(n, HEAD_DIM, S)


def _from_heads_t(t):
    H, Dh, S = t.shape
    return t.reshape(H * Dh, S).T


def _rep(t):
    return jnp.broadcast_to(t[..., None], t.shape + (LANES,))


def _t5_bucket(dist):
    max_exact = N_BUCKETS // 2
    d = np.maximum(dist, 1).astype(np.float32)
    large = max_exact + (np.log(d / max_exact) / np.log(MAX_DISTANCE / max_exact)
                         * (N_BUCKETS - max_exact)).astype(np.int32)
    large = np.minimum(large, N_BUCKETS - 1)
    return np.where(dist < max_exact, dist, large).astype(np.int32)


def _band_tables():
    qi = np.arange(BAND)[:, None]
    kj = np.arange(2 * BAND)[None, :]
    sub = qi + BAND - kj
    band = (sub >= 0) & (sub <= BAND)
    out = []
    for d in DILATIONS:
        bucket = _t5_bucket(np.clip(sub, 0, BAND) * d)
        out.append(np.where(band, bucket, -1).astype(np.int32))
    return np.stack(out)


_PACK = (("w_in", 770), ("w_out", 256), ("w_xq", 64), ("w_xk", 64), ("w_xv", 64), ("w_xo", 64),
         ("w_gate", 704), ("w_up", 704), ("w_down", 704))


def _pack(shards):
    rows = [shards[n].reshape(-1, PACK_COLS) for n, _ in _PACK]
    total = sum(r.shape[0] for r in rows)
    pad = (-total) % 128
    if pad:
        rows.append(jnp.zeros((pad, PACK_COLS), rows[0].dtype))
    return jnp.concatenate(rows, axis=0)


def _unpack(pack, shapes):
    out, r = {}, 0
    for n, _ in _PACK:
        cnt = int(np.prod(shapes[n])) // PACK_COLS
        out[n] = pack[r:r + cnt].reshape(shapes[n])
        r += cnt
    return out


_COL_SHARDED = ("w_in", "w_xo", "w_gate", "w_up")


def _full_weight(gathered, name):
    return jnp.concatenate(gathered, axis=1 if name in _COL_SHARDED else 0)


def _split_weight(full, name):
    return jnp.split(full, N_CHIPS, axis=1 if name in _COL_SHARDED else 0)


_SMALL = ("g_mix_pre", "g_mix_post", "g_xattn_pre", "g_mem", "g_xattn_post", "g_ffn_pre", "g_ffn_post")


def _pack_small(vals):
    D = vals["g_mix_pre"].shape[1]
    rows = [vals[n].reshape(1, D) for n in _SMALL]
    misc = jnp.concatenate([vals["b_f"].reshape(-1), vals["rel_bias"].reshape(-1)])
    rows.append(jnp.pad(misc, (0, D - misc.shape[0])).reshape(1, D))
    rows.append(jnp.zeros((16 - len(rows), D), F32))
    return jnp.concatenate(rows, axis=0)


def _unpack_small(pack):
    out = {n: pack[i:i + 1] for i, n in enumerate(_SMALL)}
    out["b_f"] = pack[7, 0:N_FOX_HEADS].reshape(1, N_FOX_HEADS)
    out["rel_bias"] = pack[7, N_FOX_HEADS:N_FOX_HEADS + N_BUCKETS * N_DIL_HEADS].reshape(N_BUCKETS, N_DIL_HEADS)
    return out


def kernel(x, mem, g_mix_pre, w_in, b_f, rel_bias, w_out, g_mix_post, g_xattn_pre, g_mem, w_xq, w_xk, w_xv, w_xo, g_xattn_post, g_ffn_pre, w_gate, w_up, w_down, g_ffn_post, loss_target, m_g_mix_pre, m_w_in, m_b_f, m_rel_bias, m_w_out, m_g_mix_post, m_g_xattn_pre, m_g_mem, m_w_xq, m_w_xk, m_w_xv, m_w_xo, m_g_xattn_post, m_g_ffn_pre, m_w_gate, m_w_up, m_w_down, m_g_ffn_post, v_g_mix_pre, v_w_in, v_b_f, v_rel_bias, v_w_out, v_g_mix_post, v_g_xattn_pre, v_g_mem, v_w_xq, v_w_xk, v_w_xv, v_w_xo, v_g_xattn_post, v_g_ffn_pre, v_w_gate, v_w_up, v_w_down, v_g_ffn_post):
    args = dict(locals())
    big = [n for n, _ in _PACK]
    names = ["g_mix_pre", "w_in", "b_f", "rel_bias", "w_out", "g_mix_post", "g_xattn_pre", "g_mem", "w_xq",
             "w_xk", "w_xv", "w_xo", "g_xattn_post", "g_ffn_pre", "w_gate", "w_up", "w_down", "g_ffn_post"]
    xs = x[0]
    S, D = xs.shape
    assert S % (BAND * DILATIONS[-1]) == 0
    shard_shapes = {n: args[n].shape[1:] for n in big}
    my_x, my_y, my_c = lax.axis_index("x"), lax.axis_index("y"), lax.axis_index("c")

    w_pack = _pack({n: args[n][0] for n in big})
    gathered = _chip_all_gather(w_pack.astype(BF16), "weights_all_gather")
    per_chip = [_unpack(gathered[j], shard_shapes) for j in range(N_CHIPS)]
    W = {n: _full_weight([pc[n] for pc in per_chip], n) for n in big}
    w_fox, w_fg, w_dil = (W["w_in"][:, :3 * FOX_WIDTH], W["w_in"][:, 3 * FOX_WIDTH:3 * FOX_WIDTH + N_FOX_HEADS],
                          W["w_in"][:, 3 * FOX_WIDTH + N_FOX_HEADS:])
    w_qkv = jnp.concatenate([w_fox, w_dil], axis=1)
    w_fg_pad = jnp.pad(w_fg, ((0, 0), (0, LANES - N_FOX_HEADS)))
    F = W["w_gate"].shape[1]
    nft = F // FF_TILE
    w_gu = jnp.stack([W["w_gate"].reshape(D, nft, FF_TILE), W["w_up"].reshape(D, nft, FF_TILE)],
                     axis=2).reshape(D, 2 * F)

    h1 = _rms_fwd(xs, g_mix_pre, "rms_mix_pre")
    qkv = _mm(h1, w_qkv, "nn", BF16, "proj_qkv")
    fg = _mm(h1, w_fg_pad, "nn", F32, "proj_gate")
    fg_t = fg[:, :N_FOX_HEADS].T
    b_col = b_f.reshape(N_FOX_HEADS, 1)
    c_t = _forget_fwd(fg_t, b_col, "forget_cumsum")
    c_row = c_t.reshape(N_FOX_HEADS, 1, S)
    c_rep = _rep(c_t)
    fq_s, fk_s, fv_s = (qkv[:, i * FOX_WIDTH:(i + 1) * FOX_WIDTH] for i in range(3))
    fqt, fkt, fvt = (_to_heads_t(t, N_FOX_HEADS) for t in (fq_s, fk_s, fv_s))
    fk, fv = _to_heads(fk_s, N_FOX_HEADS), _to_heads(fv_s, N_FOX_HEADS)
    o_fox_t, lse_fox = _fox_fwd(fqt, fk, fvt, c_row, c_rep, "fox_fwd")

    bucket_map = _band_tables()
    onehot = (jnp.asarray(bucket_map)[..., None] == jnp.arange(N_BUCKETS)).astype(F32)
    bias_tab = jnp.einsum("pqkb,bh->phkq", onehot, rel_bias, precision=lax.Precision.HIGHEST)
    bias_tab = jnp.where(jnp.asarray(bucket_map.transpose(0, 2, 1) >= 0)[:, None], bias_tab, NEG)
    bias_t = bias_tab.reshape(3, HEAD_PAIRS, 2, 2 * BAND, BAND)
    views = [(qkv.reshape(1, S, qkv.shape[1]), DIL_Q_BLOCK)] + [
        (_to_residues(qkv, 1, 3 * DIL_WIDTH, d, f"dilated_qkv_residues_{d}"), 0) for d in DILATIONS[1:]]

    def to_tok(stat, d):
        return stat.reshape(N_DIL_HEADS, d, S // d).swapaxes(1, 2).reshape(N_DIL_HEADS, S)

    def to_perm(stat, d):
        return stat.reshape(N_DIL_HEADS, S // d, d).swapaxes(1, 2).reshape(HEAD_PAIRS, 2, S)

    def tok_or_res(t):
        return t.reshape(t.shape[1:]) if t.shape[0] == 1 else t

    lse_tok = jnp.stack([to_tok(_dil_lse(*views[p], bias_t[p], f"dilated_lse_{d}"), d)
                         for p, d in enumerate(DILATIONS)])
    lse_joint = _lse_join(lse_tok, "dilated_lse_join")
    lse_perm = [to_perm(lse_joint, d) for d in DILATIONS]
    o_dil = [tok_or_res(_dil_out(*views[p], bias_t[p], lse_perm[p], f"dilated_out_{d}"))
             for p, d in enumerate(DILATIONS)]
    o_cat = _sum_cast_cols([[_from_heads_t(o_fox_t)]] + [[o] for o in o_dil], BF16, "mixer_out_cat")
    w_out_b = W["w_out"]
    w_out_cat = jnp.concatenate([w_out_b[:FOX_WIDTH]] + [w_out_b[FOX_WIDTH:]] * 3, axis=0)
    a = _mm(o_cat, w_out_cat, "nn", F32, "proj_out")
    x1, h2 = _resid_norm(xs, a, g_mix_post, g_xattn_pre, "resid_mix")

    hm = _rms_fwd(mem[0], g_mem, "rms_mem")
    q2 = _mm(h2, W["w_xq"], "nn", BF16, "xattn_q")
    w_xkv = jnp.concatenate([W["w_xk"], W["w_xv"]], axis=1)
    kvm = _mm(hm, w_xkv, "nn", BF16, "xattn_kv")
    MW = N_MEM_HEADS * HEAD_DIM
    oc, lse_mem = _mem_fwd(q2, kvm, "xattn_fwd")
    y2 = _mm(oc, W["w_xo"], "nn", F32, "xattn_o")
    x2, h3 = _resid_norm(x1, y2, g_xattn_post, g_ffn_pre, "resid_xattn")

    gu, act = _ffn_up(h3, w_gu, "ffn_up")
    y3 = _mm(act, W["w_down"], "nn", F32, "ffn_down", tk=1536)
    dx3, loss_tile = _final_loss(x2, y3, g_ffn_post, loss_target[0], "final_loss")

    grads = {}
    small = {}
    _, dy3_b, dg = _rms_bwd(y3, g_ffn_post, dx3, None, "bwd_norm_ffn_post", want=("bf16",))
    small["g_ffn_post"] = dg[0:1]
    grads["w_down"] = _mm(act, dy3_b, "tn", F32, "grad_w_down", tm=1408)
    dgu = _ffn_dact(dy3_b, W["w_down"], gu, "ffn_dact")
    dw_gu = _mm(h3, dgu, "tn", F32, "grad_w_gu").reshape(D, nft, 2, FF_TILE)
    grads["w_gate"], grads["w_up"] = dw_gu[:, :, 0].reshape(D, F), dw_gu[:, :, 1].reshape(D, F)
    dh3 = _mm(dgu, w_gu, "nt", F32, "bwd_ffn_in", tk=1024)
    dx2, _, dg = _rms_bwd(x2, g_ffn_pre, dh3, dx3, "bwd_norm_ffn_pre", want=("f32",))
    small["g_ffn_pre"] = dg[0:1]

    _, dy2_b, dg = _rms_bwd(y2, g_xattn_post, dx2, None, "bwd_norm_xattn_post", want=("bf16",))
    small["g_xattn_post"] = dg[0:1]
    grads["w_xo"] = _mm(oc, dy2_b, "tn", F32, "grad_w_xo")
    doc = _mm(dy2_b, W["w_xo"], "nt", BF16, "bwd_xattn_o")
    delta_mem = _head_rowdot(doc, [oc], "xattn_delta")[:, :N_MEM_HEADS].T.reshape(N_MEM_HEADS // 2, 2, S)
    dq2, dkm, dvm = _mem_bwd(q2, kvm, doc, lse_mem, delta_mem, "xattn_bwd")
    dkvm = jnp.concatenate([dkm, dvm], axis=1).astype(BF16)
    grads["w_xq"] = _mm(h2, dq2, "tn", F32, "grad_w_xq")
    dw_xkv = _mm(hm, dkvm, "tn", F32, "grad_w_xkv")
    grads["w_xk"], grads["w_xv"] = dw_xkv[:, :MW], dw_xkv[:, MW:]
    dhm = _mm(dkvm, w_xkv, "nt", F32, "bwd_xattn_kv")
    _, _, dg = _rms_bwd(mem[0], g_mem, dhm, None, "bwd_norm_mem", want=())
    small["g_mem"] = dg[0:1]
    dh2 = _mm(dq2, W["w_xq"], "nt", F32, "bwd_xattn_q")
    dx1, _, dg = _rms_bwd(x1, g_xattn_pre, dh2, dx2, "bwd_norm_xattn_pre", want=("f32",))
    small["g_xattn_pre"] = dg[0:1]

    _, da_b, dg = _rms_bwd(a, g_mix_post, dx1, None, "bwd_norm_mix_post", want=("bf16",))
    small["g_mix_post"] = dg[0:1]
    dw_out_cat = _mm(o_cat, da_b, "tn", F32, "grad_w_out")
    dw_out_dil = _add_n([dw_out_cat[FOX_WIDTH + p * DIL_WIDTH:FOX_WIDTH + (p + 1) * DIL_WIDTH] for p in range(3)],
                        "grad_w_out_dil")
    grads["w_out"] = jnp.concatenate([dw_out_cat[:FOX_WIDTH], dw_out_dil], axis=0)
    do = _mm(da_b, w_out_b, "nt", BF16, "bwd_proj_out")
    do_fox, do_dil = do[:, :FOX_WIDTH], do[:, FOX_WIDTH:]

    delta_fox = _head_rowdot(do_fox, [o_cat[:, :FOX_WIDTH]], "fox_delta")[:, :N_FOX_HEADS].T
    dqf, dkf, dvf, dcs, drs = _fox_bwd(fqt, fk, fkt, fv, _to_heads_t(do_fox, N_FOX_HEADS), c_rep,
                                       lse_fox, delta_fox.reshape(N_FOX_HEADS, 1, S), "fox_bwd")
    dfg_t, db_f = _forget_bwd(fg_t, b_col, drs[:, 0], dcs[..., 0], "forget_bwd")

    delta_dil = _head_rowdot(do_dil, o_dil, "dilated_delta")[:, :N_DIL_HEADS].T
    do_res = [do_dil.reshape(1, S, DIL_WIDTH)] + [
        _to_residues(do, 1, DIL_WIDTH, d, f"dilated_do_residues_{d}") for d in DILATIONS[1:]]
    dil_grads = [_dil_bwd(*views[p], do_res[p], bias_t[p], lse_perm[p], to_perm(delta_dil, d), f"dilated_bwd_{d}")
                 for p, d in enumerate(DILATIONS)]
    dbias_t = jnp.stack([g[3].reshape(N_DIL_HEADS, 2 * BAND, BAND) for g in dil_grads])
    d_rel = _bucket_reduce(dbias_t, jnp.asarray(bucket_map.transpose(0, 2, 1)), "rel_bias_grad")[:, :N_DIL_HEADS]
    dqkv = _sum_cast_cols([[_from_heads_t(dqf)], [_from_heads_t(dkf)], [_from_heads_t(dvf)]]
                          + [[tok_or_res(g[j]) for g in dil_grads] for j in range(3)],
                          BF16, "dqkv_assemble")
    dfg_pad = jnp.pad(dfg_t.T, ((0, 0), (0, LANES - N_FOX_HEADS))).astype(BF16)
    dw_qkv = _mm(h1, dqkv, "tn", F32, "grad_w_qkv")
    dw_fg = _mm(h1, dfg_pad, "tn", F32, "grad_w_gate_cols")[:, :N_FOX_HEADS]
    grads["w_in"] = jnp.concatenate([dw_qkv[:, :3 * FOX_WIDTH], dw_fg, dw_qkv[:, 3 * FOX_WIDTH:]], axis=1)
    dcat = jnp.concatenate([dqkv, dfg_pad], axis=1)
    w_cat = jnp.concatenate([w_qkv, w_fg_pad], axis=1)
    dh1 = _mm(dcat, w_cat, "nt", F32, "bwd_proj_in", tk=640)
    grad_x, _, dg = _rms_bwd(xs, g_mix_pre, dh1, dx1, "bwd_norm_mix_pre", want=("f32",))
    small["g_mix_pre"] = dg[0:1]
    small["b_f"] = db_f[:, 0].reshape(1, N_FOX_HEADS)
    small["rel_bias"] = d_rel

    split = {n: _split_weight(grads[n], n) for n in big}
    parts = jnp.stack([_pack({n: split[n][j] for n in big}) for j in range(N_CHIPS)])
    R = parts.shape[1]
    half = R // 2
    keep = lax.dynamic_slice_in_dim(parts, my_c * half, half, axis=1)
    give = lax.dynamic_slice_in_dim(parts, (1 - my_c) * half, half, axis=1)
    got = _sibling_exchange(give, "grads_to_sibling")
    chip_sum = _add_n([keep.reshape(-1, PACK_COLS), got.reshape(-1, PACK_COLS)], "grads_add_sibling")
    chip_sum = chip_sum.reshape(N_CHIPS, half, PACK_COLS)
    my_chip = 2 * my_x + my_y
    from_chips = _chip_scatter(chip_sum.astype(BF16), "grads_to_chips")
    own = lax.dynamic_index_in_dim(chip_sum, my_chip, axis=0, keepdims=False)
    g_half = _add_n([own, from_chips[0], from_chips[1], from_chips[2]], "grads_add_chips")
    other_half = _sibling_exchange(g_half, "grads_share_sibling")
    g_pack = jnp.where(my_c == 0, jnp.concatenate([g_half, other_half]), jnp.concatenate([other_half, g_half]))

    small_pack = _pack_small(small)
    small_pack = small_pack.at[8, 0].set(loss_tile[0, 0])
    everyone = _all_to_all_small(small_pack, "small_all_gather")
    small_sum = _add_n([everyone[i] for i in range(8)], "small_sum")
    loss = small_sum[8, 0]
    g_small = _unpack_small(small_sum)

    m_pack = _pack({n: args["m_" + n][0] for n in big})
    v_pack = _pack({n: args["v_" + n][0] for n in big})
    d_pack, nm_pack, nv_pack = _adamw(w_pack, g_pack, m_pack, v_pack, "adamw_big")
    outs = {"grad": _unpack(g_pack, shard_shapes), "delta": _unpack(d_pack, shard_shapes),
            "new_m": _unpack(nm_pack, shard_shapes), "new_v": _unpack(nv_pack, shard_shapes)}
    sw = _pack_small({n: args[n] for n in _SMALL + ("b_f", "rel_bias")})
    sm = _pack_small({n: args["m_" + n] for n in _SMALL + ("b_f", "rel_bias")})
    sv = _pack_small({n: args["v_" + n] for n in _SMALL + ("b_f", "rel_bias")})
    sd, snm, snv = _adamw(sw, small_sum.at[8, 0].set(0.0), sm, sv, "adamw_small")
    souts = {"grad": g_small, "delta": _unpack_small(sd), "new_m": _unpack_small(snm), "new_v": _unpack_small(snv)}

    def leaf(kind, n):
        if n in souts[kind]:
            return souts[kind][n].reshape(args[n].shape)
        return outs[kind][n].reshape(args[n].shape)

    result = [loss, grad_x.reshape(x.shape)]
    for kind in ("grad", "delta", "new_m", "new_v"):
        result += [leaf(kind, n) for n in names]
    return tuple(result)
```

```python
import numpy as np
import jax
import jax.numpy as jnp
from jax import lax
from jax.experimental import pallas as pl
from jax.experimental.pallas import tpu as pltpu

F32 = jnp.float32
BF16 = jnp.bfloat16
MESH_IDS = pl.DeviceIdType.MESH

LANES = 128
HEAD_DIM = 64
N_FOX_HEADS = 8
N_DIL_HEADS = 8
N_MEM_HEADS = 4
FOX_WIDTH = N_FOX_HEADS * HEAD_DIM
DIL_WIDTH = N_DIL_HEADS * HEAD_DIM
DILATIONS = (1, 4, 16)
BAND = 128
BAND_CHUNK_MAX = 8 * BAND
N_BUCKETS = 32
MAX_DISTANCE = 2048
QK_SCALE = HEAD_DIM ** -0.5
RMS_EPS = 1e-6
NEG = -1e30
VMEM_LIMIT = 56 << 20

ADAM_LR = 0.001
ADAM_B1 = 0.9
ADAM_B2 = 0.999
ADAM_EPS = 1e-08
ADAM_WD = 0.01
ADAM_STEP = 10

N_CHIPS = 4
PACK_COLS = 1024


def _params(*sem):
    return pltpu.CompilerParams(dimension_semantics=sem, vmem_limit_bytes=VMEM_LIMIT)


def _fit(n, cap):
    if n <= cap:
        return n
    t = (cap // LANES) * LANES
    while t >= LANES:
        if n % t == 0:
            return t
        t -= LANES
    raise ValueError(f"no lane-aligned tile for {n} under {cap}")


def _dot(a, b, dims):
    return lax.dot_general(a, b, (dims, ((), ())), preferred_element_type=F32)


_NN = ((1,), (0,))
_NT = ((1,), (1,))
_TN = ((0,), (0,))


def _mm(a, b, mode, out_dtype, name, tm=1024, tn=1024, tk=512):
    if mode == "nn":
        (M, K), N = a.shape, b.shape[1]
    elif mode == "nt":
        (M, K), N = a.shape, b.shape[0]
    else:
        (K, M), N = a.shape, b.shape[1]
    tm, tn, tk = _fit(M, tm), _fit(N, tn), _fit(K, tk)
    nk = K // tk
    if mode == "tn":
        a_spec = pl.BlockSpec((tk, tm), lambda i, j, k: (k, i))
    else:
        a_spec = pl.BlockSpec((tm, tk), lambda i, j, k: (i, k))
    if mode == "nt":
        b_spec = pl.BlockSpec((tn, tk), lambda i, j, k: (j, k))
    else:
        b_spec = pl.BlockSpec((tk, tn), lambda i, j, k: (k, j))
    dims = {"nn": _NN, "nt": _NT, "tn": _TN}[mode]

    def body(a_ref, b_ref, o_ref, acc_ref):
        k = pl.program_id(2)

        @pl.when(k == 0)
        def _():
            acc_ref[...] = jnp.zeros_like(acc_ref)

        acc_ref[...] += _dot(a_ref[...].astype(BF16), b_ref[...].astype(BF16), dims)

        @pl.when(k == nk - 1)
        def _():
            o_ref[...] = acc_ref[...].astype(o_ref.dtype)

    return pl.pallas_call(
        body, name=name, grid=(M // tm, N // tn, nk),
        in_specs=[a_spec, b_spec],
        out_specs=pl.BlockSpec((tm, tn), lambda i, j, k: (i, j)),
        out_shape=jax.ShapeDtypeStruct((M, N), out_dtype),
        scratch_shapes=[pltpu.VMEM((tm, tn), F32)],
        compiler_params=_params("parallel", "parallel", "arbitrary"),
    )(a, b)


def _rms_rows(x):
    return lax.rsqrt(jnp.mean(x * x, axis=-1, keepdims=True) + RMS_EPS)


def _rms_fwd(x, g, name, tr=512):
    S, D = x.shape
    tr = _fit(S, tr)

    def body(x_ref, g_ref, h_ref):
        xv = x_ref[...]
        h_ref[...] = (xv * _rms_rows(xv) * g_ref[...]).astype(BF16)

    return pl.pallas_call(
        body, name=name, grid=(S // tr,),
        in_specs=[pl.BlockSpec((tr, D), lambda i: (i, 0)), pl.BlockSpec((1, D), lambda i: (0, 0))],
        out_specs=pl.BlockSpec((tr, D), lambda i: (i, 0)),
        out_shape=jax.ShapeDtypeStruct((S, D), BF16),
        compiler_params=_params("parallel"),
    )(x, g)


def _resid_norm(xres, y, g_post, g_next, name, tr=512):
    S, D = xres.shape
    tr = _fit(S, tr)

    def body(x_ref, y_ref, gp_ref, gn_ref, xn_ref, h_ref):
        yv = y_ref[...]
        xn = x_ref[...] + yv * _rms_rows(yv) * gp_ref[...]
        xn_ref[...] = xn
        h_ref[...] = (xn * _rms_rows(xn) * gn_ref[...]).astype(BF16)

    row = pl.BlockSpec((tr, D), lambda i: (i, 0))
    vec = pl.BlockSpec((1, D), lambda i: (0, 0))
    return pl.pallas_call(
        body, name=name, grid=(S // tr,),
        in_specs=[row, row, vec, vec], out_specs=[row, row],
        out_shape=[jax.ShapeDtypeStruct((S, D), F32), jax.ShapeDtypeStruct((S, D), BF16)],
        compiler_params=_params("parallel"),
    )(xres, y, g_post, g_next)


def _final_loss(xres, y, g_post, target, name, tr=512):
    S, D = xres.shape
    tr = _fit(S, tr)

    def body(x_ref, y_ref, gp_ref, t_ref, d_ref, loss_ref):
        i = pl.program_id(0)
        yv = y_ref[...]
        err = x_ref[...] + yv * _rms_rows(yv) * gp_ref[...] - t_ref[...]
        d_ref[...] = err * (1.0 / D)

        @pl.when(i == 0)
        def _():
            loss_ref[...] = jnp.zeros_like(loss_ref)

        part = jnp.sum(jnp.sum(err * err, axis=1, keepdims=True), axis=0, keepdims=True)
        loss_ref[...] += jnp.broadcast_to(part * (0.5 / D), loss_ref.shape)

    row = pl.BlockSpec((tr, D), lambda i: (i, 0))
    vec = pl.BlockSpec((1, D), lambda i: (0, 0))
    return pl.pallas_call(
        body, name=name, grid=(S // tr,),
        in_specs=[row, row, vec, row],
        out_specs=[row, pl.BlockSpec((8, LANES), lambda i: (0, 0))],
        out_shape=[jax.ShapeDtypeStruct((S, D), F32), jax.ShapeDtypeStruct((8, LANES), F32)],
        compiler_params=_params("arbitrary"),
    )(xres, y, g_post, target)


def _rms_bwd(xin, g, dy, dres, name, want=("f32", "bf16"), tr=512):
    S, D = xin.shape
    tr = _fit(S, tr)
    has_res = dres is not None

    def body(*refs):
        refs = list(refs)
        dg_ref = refs.pop()
        dxb_ref = refs.pop() if "bf16" in want else None
        dx_ref = refs.pop() if "f32" in want else None
        dr_ref = refs.pop() if has_res else None
        x_ref, g_ref, dy_ref = refs
        i = pl.program_id(0)
        xv = x_ref[...]
        dyv = dy_ref[...].astype(F32)
        xhat = xv * _rms_rows(xv)
        dxhat = dyv * g_ref[...]
        r = _rms_rows(xv)
        dx = r * (dxhat - xhat * jnp.mean(dxhat * xhat, axis=-1, keepdims=True))
        if has_res:
            dx = dx + dr_ref[...]
        if dx_ref is not None:
            dx_ref[...] = dx
        if dxb_ref is not None:
            dxb_ref[...] = dx.astype(BF16)

        @pl.when(i == 0)
        def _():
            dg_ref[...] = jnp.zeros_like(dg_ref)

        dg_ref[...] += jnp.broadcast_to(jnp.sum(dyv * xhat, axis=0, keepdims=True), dg_ref.shape)

    row = pl.BlockSpec((tr, D), lambda i: (i, 0))
    vec = pl.BlockSpec((1, D), lambda i: (0, 0))
    acc = pl.BlockSpec((8, D), lambda i: (0, 0))
    ins = [xin, g, dy] + ([dres] if has_res else [])
    dtypes = [dt for key, dt in (("f32", F32), ("bf16", BF16)) if key in want]
    outs = pl.pallas_call(
        body, name=name, grid=(S // tr,),
        in_specs=[row, vec, row] + ([row] if has_res else []),
        out_specs=[row] * len(dtypes) + [acc],
        out_shape=[jax.ShapeDtypeStruct((S, D), dt) for dt in dtypes] + [jax.ShapeDtypeStruct((8, D), F32)],
        compiler_params=_params("arbitrary"),
    )(*ins)
    by_key = dict(zip([key for key in ("f32", "bf16") if key in want], outs[:-1]))
    return by_key.get("f32"), by_key.get("bf16"), outs[-1]


def _tri(n, upper):
    r = lax.broadcasted_iota(jnp.int32, (n, n), 0)
    c = lax.broadcasted_iota(jnp.int32, (n, n), 1)
    return jnp.where((r <= c) if upper else (r >= c), 1.0, 0.0).astype(F32)


def _forget_fwd(fg_t, b_col, name, ts=512):
    H, S = fg_t.shape
    ts = _fit(S, ts)

    def body(f_ref, b_ref, c_ref, carry_ref):
        i = pl.program_id(0)

        @pl.when(i == 0)
        def _():
            carry_ref[...] = jnp.zeros_like(carry_ref)

        z = f_ref[...] + b_ref[...]
        logf = jnp.minimum(z, 0.0) - jnp.log(1.0 + jnp.exp(-jnp.abs(z)))
        run = lax.dot_general(logf, _tri(ts, True), (_NN, ((), ())), precision=lax.Precision.HIGHEST,
                              preferred_element_type=F32) + carry_ref[:, 0:1]
        c_ref[...] = run
        carry_ref[...] = jnp.broadcast_to(
            carry_ref[:, 0:1] + jnp.sum(logf, axis=1, keepdims=True), carry_ref.shape)

    return pl.pallas_call(
        body, name=name, grid=(S // ts,),
        in_specs=[pl.BlockSpec((H, ts), lambda i: (0, i)), pl.BlockSpec((H, 1), lambda i: (0, 0))],
        out_specs=pl.BlockSpec((H, ts), lambda i: (0, i)),
        out_shape=jax.ShapeDtypeStruct((H, S), F32),
        scratch_shapes=[pltpu.VMEM((H, LANES), F32)],
        compiler_params=_params("arbitrary"),
    )(fg_t, b_col)


def _forget_bwd(fg_t, b_col, dc_plus, dc_minus, name, ts=512):
    H, S = fg_t.shape
    ts = _fit(S, ts)
    nb = S // ts

    def body(f_ref, b_ref, dcp_ref, dcm_ref, df_ref, db_ref, carry_ref):
        i = pl.program_id(0)

        @pl.when(i == 0)
        def _():
            carry_ref[...] = jnp.zeros_like(carry_ref)
            db_ref[...] = jnp.zeros_like(db_ref)

        dc = dcp_ref[...] - dcm_ref[...]
        suffix = lax.dot_general(dc, _tri(ts, False), (_NN, ((), ())), precision=lax.Precision.HIGHEST,
                                 preferred_element_type=F32) + carry_ref[:, 0:1]
        z = f_ref[...] + b_ref[...]
        sig_neg = 1.0 / (1.0 + jnp.exp(z))
        df = suffix * sig_neg
        df_ref[...] = df
        carry_ref[...] = jnp.broadcast_to(
            carry_ref[:, 0:1] + jnp.sum(dc, axis=1, keepdims=True), carry_ref.shape)
        db_ref[...] += jnp.broadcast_to(jnp.sum(df, axis=1, keepdims=True), db_ref.shape)

    rev = pl.BlockSpec((H, ts), lambda i: (0, nb - 1 - i))
    return pl.pallas_call(
        body, name=name, grid=(nb,),
        in_specs=[rev, pl.BlockSpec((H, 1), lambda i: (0, 0)), rev, rev],
        out_specs=[rev, pl.BlockSpec((H, LANES), lambda i: (0, 0))],
        out_shape=[jax.ShapeDtypeStruct((H, S), F32), jax.ShapeDtypeStruct((H, LANES), F32)],
        scratch_shapes=[pltpu.VMEM((H, LANES), F32)],
        compiler_params=_params("arbitrary"),
    )(fg_t, b_col, dc_plus, dc_minus)


def _tile_lanes(x, n):
    return x if n == LANES else jnp.tile(x, (1, n // LANES))


ONES_ROWS = 16


def _with_ones(t):
    return jnp.concatenate([t, jnp.ones((t.shape[0], ONES_ROWS, t.shape[2]), t.dtype)], axis=1)


def _fox_fwd(qt, k, vt, c_row, c_rep, name, tq=512, tk=1024):
    H, Dh, S = qt.shape
    tk = _fit(S, tk)
    tq = _fit(tk, tq)
    ratio = tk // tq

    def body(qt_ref, k_ref, vt_ref, c_ref, crep_ref, o_ref, lse_ref, m_ref, acc_ref,
             sa_ref, sb_ref, ta_ref, tb_ref):
        i = pl.program_id(1)
        qv = qt_ref[...] * QK_SCALE
        cq0 = c_ref[:, pl.ds(pl.multiple_of(i * tq, LANES), LANES)][:, 0:1]
        m_ref[...] = jnp.full_like(m_ref, NEG)
        acc_ref[...] = jnp.zeros_like(acc_ref)
        n = i // ratio
        q_off = (i - n * ratio) * tq

        def scores(j, s_ref, t_ref, diagonal):
            off = pl.multiple_of(j * tk, LANES)
            s = _dot(k_ref[pl.ds(off, tk), :], qv, _NN) + _tile_lanes(cq0 - crep_ref[pl.ds(off, tk), :], tq)
            if diagonal:
                key = lax.broadcasted_iota(jnp.int32, (tk, tq), 0)
                qry = lax.broadcasted_iota(jnp.int32, (tk, tq), 1) + q_off
                s = jnp.where(key <= qry, s, NEG)
            s_ref[...] = s
            t_ref[...] = jnp.max(s, axis=0, keepdims=True)

        def absorb(j, s_ref, t_ref):
            off = pl.multiple_of(j * tk, LANES)
            m_old = m_ref[...]
            m_new = jnp.maximum(m_old, t_ref[...])
            p = jnp.exp(s_ref[...] - m_new)
            alpha = jnp.exp(m_old - m_new)
            acc_ref[...] = alpha * acc_ref[...] + _dot(vt_ref[:, pl.ds(off, tk)], p.astype(BF16), _NN)
            m_ref[...] = m_new

        scores(n, sa_ref, ta_ref, True)

        def loop_body(jj, carry):
            scores(2 * jj, sb_ref, tb_ref, False)
            absorb(jnp.where(jj == 0, n, 2 * jj - 1), sa_ref, ta_ref)
            scores(2 * jj + 1, sa_ref, ta_ref, False)
            absorb(2 * jj, sb_ref, tb_ref)
            return carry

        pairs = n // 2
        lax.fori_loop(0, pairs, loop_body, 0)
        held = jnp.where(pairs == 0, n, 2 * pairs - 1)

        @pl.when(n % 2 == 1)
        def _():
            scores(n - 1, sb_ref, tb_ref, False)
            absorb(held, sa_ref, ta_ref)
            absorb(n - 1, sb_ref, tb_ref)

        @pl.when(n % 2 == 0)
        def _():
            absorb(held, sa_ref, ta_ref)

        l = acc_ref[Dh:Dh + 1, :]
        o_ref[...] = acc_ref[0:Dh, :] / l
        lse_ref[...] = m_ref[...] + jnp.log(l) - cq0

    lanes_full = pl.BlockSpec((None, Dh + ONES_ROWS, S), lambda h, i: (h, 0, 0))
    lanes_tile = pl.BlockSpec((None, Dh, tq), lambda h, i: (h, 0, i))
    return pl.pallas_call(
        body, name=name, grid=(H, S // tq),
        in_specs=[lanes_tile, pl.BlockSpec((None, S, Dh), lambda h, i: (h, 0, 0)), lanes_full,
                  pl.BlockSpec((None, 1, S), lambda h, i: (h, 0, 0)),
                  pl.BlockSpec((None, S, LANES), lambda h, i: (h, 0, 0))],
        out_specs=[lanes_tile, pl.BlockSpec((None, 1, tq), lambda h, i: (h, 0, i))],
        out_shape=[jax.ShapeDtypeStruct((H, Dh, S), F32), jax.ShapeDtypeStruct((H, 1, S), F32)],
        scratch_shapes=[pltpu.VMEM((1, tq), F32), pltpu.VMEM((Dh + ONES_ROWS, tq), F32),
                        pltpu.VMEM((tk, tq), F32), pltpu.VMEM((tk, tq), F32),
                        pltpu.VMEM((1, tq), F32), pltpu.VMEM((1, tq), F32)],
        compiler_params=_params("parallel", "arbitrary"),
    )(qt, k, vt, c_row, c_rep)


def _fox_bwd(qt, k, kt, v, dot, c_rep, lse_row, delta_row, name, tq=1024, tk=512):
    H, Da, S = qt.shape
    Dh = Da - ONES_ROWS
    tq = _fit(S, tq)
    tk = _fit(tq, tk)
    ratio = tq // tk
    nq = S // tq
    nk = S // tk

    def body(k_ref, kt_ref, v_ref, crep_ref, qt_ref, dot_ref, lse_ref, dl_ref,
             dqt_ref, dkt_ref, dvt_ref, dka_ref, dva_ref):
        j = pl.program_id(1)

        @pl.when(j == 0)
        def _():
            dqt_ref[...] = jnp.zeros_like(dqt_ref)

        kv = k_ref[...]
        ktv = kt_ref[...]
        vv = v_ref[...]
        c_col = _tile_lanes(crep_ref[...], tq)
        dka_ref[...] = jnp.zeros_like(dka_ref)
        dva_ref[...] = jnp.zeros_like(dva_ref)
        i_diag = j // ratio
        k_off = (j - i_diag * ratio) * tk

        def step(i, diagonal):
            off = pl.multiple_of(i * tq, LANES)
            qv = qt_ref[:, pl.ds(off, tq)] * QK_SCALE
            dov = dot_ref[:, pl.ds(off, tq)]
            e = _dot(kv, qv[0:Dh, :], _NN) - lse_ref[:, pl.ds(off, tq)] - c_col
            if diagonal:
                key = lax.broadcasted_iota(jnp.int32, (tk, tq), 0) + k_off
                qry = lax.broadcasted_iota(jnp.int32, (tk, tq), 1)
                e = jnp.where(key <= qry, e, NEG)
            p_t = jnp.exp(e)
            dva_ref[...] += _dot(dov, p_t.astype(BF16), _NT)
            ds_t = p_t * (_dot(vv, dov, _NN) - dl_ref[:, pl.ds(off, tq)])
            ds_b = ds_t.astype(BF16)
            dka_ref[...] += _dot(qv, ds_b, _NT)
            dqt_ref[:, pl.ds(off, tq)] += _dot(ktv, ds_b, _NN)

        step(i_diag, True)

        def loop_body(i, carry):
            step(i, False)
            return carry

        lax.fori_loop(i_diag + 1, nq, loop_body, 0)
        dkt_ref[0:Dh, :] = dka_ref[0:Dh, :]
        dkt_ref[Dh:, :] = dka_ref[Dh:, :] * (1.0 / QK_SCALE)
        dvt_ref[...] = dva_ref[...]

        @pl.when(j == nk - 1)
        def _():
            dqt_ref[0:Dh, :] = dqt_ref[0:Dh, :] * QK_SCALE

    rows_tile = pl.BlockSpec((None, tk, Dh), lambda h, j: (h, j, 0))
    ones_tile = pl.BlockSpec((None, Da, tk), lambda h, j: (h, 0, j))
    lanes_tile = pl.BlockSpec((None, Dh, tk), lambda h, j: (h, 0, j))
    rep = pl.BlockSpec((None, tk, LANES), lambda h, j: (h, j, 0))
    ones_full = pl.BlockSpec((None, Da, S), lambda h, j: (h, 0, 0))
    lanes_full = pl.BlockSpec((None, Dh, S), lambda h, j: (h, 0, 0))
    rowv = pl.BlockSpec((None, 1, S), lambda h, j: (h, 0, 0))
    return pl.pallas_call(
        body, name=name, grid=(H, nk),
        in_specs=[rows_tile, ones_tile, rows_tile, rep, ones_full, lanes_full, rowv, rowv],
        out_specs=[ones_full, ones_tile, lanes_tile],
        out_shape=[jax.ShapeDtypeStruct((H, Da, S), F32), jax.ShapeDtypeStruct((H, Da, S), F32),
                   jax.ShapeDtypeStruct((H, Dh, S), F32)],
        scratch_shapes=[pltpu.VMEM((Da, tk), F32), pltpu.VMEM((Dh, tk), F32)],
        compiler_params=_params("parallel", "arbitrary"),
    )(k, kt, v, c_rep, qt, dot, lse_row, delta_row)


DIL_Q_BLOCK = 3 * FOX_WIDTH // LANES
HEAD_PAIRS = N_DIL_HEADS // 2
PAIR_BLOCKS = DIL_WIDTH // LANES


def _band_geometry(S, d):
    L = S // d
    chunk = min(BAND_CHUNK_MAX, L)
    assert L % chunk == 0 and chunk % BAND == 0
    return L, chunk, chunk // BAND, L // chunk


def _band_in_specs(S, d, base):
    L, chunk, nb, _ = _band_geometry(S, d)

    def col(kind):
        return lambda hp, r, i: (r, i, base + kind * PAIR_BLOCKS + hp)

    def col_prev(kind):
        return lambda hp, r, i: (r, jnp.maximum(i * nb - 1, 0), base + kind * PAIR_BLOCKS + hp)

    main = [pl.BlockSpec((None, chunk, LANES), col(kind)) for kind in range(3)]
    prev = [pl.BlockSpec((None, BAND, LANES), col_prev(kind)) for kind in range(3)]
    bias = pl.BlockSpec((None, 2, 2 * BAND, BAND), lambda hp, r, i: (hp, 0, 0, 0))
    stat = pl.BlockSpec((None, 2, chunk), lambda hp, r, i: (hp, 0, r * (L // chunk) + i))
    tok = pl.BlockSpec((None, chunk, LANES), lambda hp, r, i: (r, i, hp))
    return main, prev, bias, stat, tok


def _to_residues(x, col_block, width, d, name, tr=512):
    S = x.shape[0]
    tr = _fit(S, tr)

    def body(x_ref, o_ref, tmp_ref):
        for j in range(width // LANES):
            cols = slice(j * LANES, (j + 1) * LANES)
            tmp_ref[j] = x_ref[:, cols].astype(F32)
            for r in range(d):
                o_ref[r, :, cols] = tmp_ref[j, pl.ds(r, tr // d, stride=d), :].astype(o_ref.dtype)

    return pl.pallas_call(
        body, name=name, grid=(S // tr,),
        in_specs=[pl.BlockSpec((tr, width), lambda i: (i, col_block))],
        out_specs=pl.BlockSpec((d, tr // d, width), lambda i: (0, i, 0)),
        out_shape=jax.ShapeDtypeStruct((d, S // d, width), x.dtype),
        scratch_shapes=[pltpu.VMEM((width // LANES, tr, LANES), F32)],
        compiler_params=_params("parallel"),
    )(x)


def _token_rows(ref, cols, tmp_ref):
    if len(ref.shape) == 2:
        return ref[:, cols].astype(F32)
    d, rows = ref.shape[0], ref.shape[1]
    for r in range(d):
        tmp_ref[pl.ds(r, rows, stride=d), :] = ref[r, :, cols].astype(F32)
    return tmp_ref[...]


def _row_spec(t, tr):
    if t.ndim == 2:
        return pl.BlockSpec((tr, t.shape[1]), lambda i: (i, 0))
    d = t.shape[0]
    return pl.BlockSpec((d, tr // d, t.shape[2]), lambda i: (0, i, 0))


def _band_scores_t(kb, qb, bias_t, first):
    s = _dot(kb, qb, _NT) + bias_t
    if first is not None:
        key = lax.broadcasted_iota(jnp.int32, s.shape, 0)
        s = jnp.where(jnp.logical_and(first, key < BAND), NEG, s)
    return s


def _dil_lse(qkv_v, base, bias_t, name):
    d, L = qkv_v.shape[:2]
    S = L * d
    _, chunk, nb, nchunks = _band_geometry(S, d)
    main, prev, bias, stat, _ = _band_in_specs(S, d, base)

    def body(q_ref, k_ref, kp_ref, b_ref, lse_ref, kext_ref):
        first = pl.program_id(2) == 0
        kext_ref[0:BAND, :] = kp_ref[...]
        kext_ref[BAND:, :] = k_ref[...]
        for a in range(2):
            lanes = slice(a * HEAD_DIM, (a + 1) * HEAD_DIM)
            bias_v = b_ref[a]
            for b in range(nb):
                s = _band_scores_t(kext_ref[b * BAND:(b + 2) * BAND, lanes],
                                   q_ref[b * BAND:(b + 1) * BAND, lanes] * QK_SCALE, bias_v, first if b == 0 else None)
                m = jnp.max(s, axis=0, keepdims=True)
                lse_ref[a:a + 1, b * BAND:(b + 1) * BAND] = m + jnp.log(jnp.sum(jnp.exp(s - m), axis=0, keepdims=True))

    return pl.pallas_call(
        body, name=name, grid=(HEAD_PAIRS, d, nchunks),
        in_specs=[main[0], main[1], prev[1], bias], out_specs=stat,
        out_shape=jax.ShapeDtypeStruct((HEAD_PAIRS, 2, S), F32),
        scratch_shapes=[pltpu.VMEM((chunk + BAND, LANES), BF16)],
        compiler_params=_params("parallel", "parallel", "parallel"),
    )(qkv_v, qkv_v, qkv_v, bias_t)


def _dil_out(qkv_v, base, bias_t, lse_joint, name):
    d, L = qkv_v.shape[:2]
    S = L * d
    _, chunk, nb, nchunks = _band_geometry(S, d)
    main, prev, bias, stat, tok = _band_in_specs(S, d, base)

    def body(q_ref, k_ref, kp_ref, v_ref, vp_ref, b_ref, lse_ref, o_ref, kext_ref, vext_ref):
        first = pl.program_id(2) == 0
        kext_ref[0:BAND, :] = kp_ref[...]
        kext_ref[BAND:, :] = k_ref[...]
        vext_ref[0:BAND, :] = vp_ref[...]
        vext_ref[BAND:, :] = v_ref[...]
        for a in range(2):
            lanes = slice(a * HEAD_DIM, (a + 1) * HEAD_DIM)
            bias_v = b_ref[a]
            for b in range(nb):
                rows, ext = slice(b * BAND, (b + 1) * BAND), slice(b * BAND, (b + 2) * BAND)
                s = _band_scores_t(kext_ref[ext, lanes], q_ref[rows, lanes] * QK_SCALE, bias_v, first if b == 0 else None)
                p_t = jnp.exp(s - lse_ref[a:a + 1, rows])
                o_ref[rows, lanes] = _dot(p_t.astype(BF16), vext_ref[ext, lanes], _TN).astype(BF16)

    return pl.pallas_call(
        body, name=name, grid=(HEAD_PAIRS, d, nchunks),
        in_specs=[main[0], main[1], prev[1], main[2], prev[2], bias, stat], out_specs=tok,
        out_shape=jax.ShapeDtypeStruct((d, L, DIL_WIDTH), BF16),
        scratch_shapes=[pltpu.VMEM((chunk + BAND, LANES), BF16), pltpu.VMEM((chunk + BAND, LANES), BF16)],
        compiler_params=_params("parallel", "parallel", "parallel"),
    )(qkv_v, qkv_v, qkv_v, qkv_v, qkv_v, bias_t, lse_joint)


def _dil_bwd(qkv_v, base, do_v, bias_t, lse_joint, delta, name):
    d, L = qkv_v.shape[:2]
    S = L * d
    _, chunk, nb, nchunks = _band_geometry(S, d)
    main, prev, bias, stat, tok = _band_in_specs(S, d, base)
    nblocks = L // BAND

    def nxt_row(i):
        return jnp.minimum((i + 1) * nb, nblocks - 1)

    q_next = pl.BlockSpec((None, BAND, LANES), lambda hp, r, i: (r, nxt_row(i), base + hp))
    do_next = pl.BlockSpec((None, BAND, LANES), lambda hp, r, i: (r, nxt_row(i), hp))
    stat_next = pl.BlockSpec((None, 2, BAND), lambda hp, r, i: (hp, 0, r * nblocks + nxt_row(i)))

    def body(q_ref, k_ref, kp_ref, v_ref, vp_ref, do_ref, b_ref, lse_ref, dl_ref,
             qn_ref, don_ref, lsen_ref, dln_ref,
             dq_ref, dk_ref, dv_ref, db_ref, kext_ref, vext_ref, dkext_ref, dvext_ref):
        r, i = pl.program_id(1), pl.program_id(2)
        first = i == 0
        has_next = i + 1 < nchunks
        tail = slice(BAND + chunk, 2 * BAND + chunk)
        kext_ref[0:BAND, :] = kp_ref[...]
        kext_ref[BAND:BAND + chunk, :] = k_ref[...]
        kext_ref[tail, :] = jnp.zeros((BAND, LANES), BF16)
        vext_ref[0:BAND, :] = vp_ref[...]
        vext_ref[BAND:BAND + chunk, :] = v_ref[...]
        vext_ref[tail, :] = jnp.zeros((BAND, LANES), BF16)
        dkext_ref[...] = jnp.zeros_like(dkext_ref)
        dvext_ref[...] = jnp.zeros_like(dvext_ref)

        @pl.when(jnp.logical_and(r == 0, i == 0))
        def _():
            db_ref[...] = jnp.zeros_like(db_ref)

        def block(a, qb, dob, lse_row, dl_row, ext, mask_rows):
            lanes = slice(a * HEAD_DIM, (a + 1) * HEAD_DIM)
            kb, vb = kext_ref[ext, lanes], vext_ref[ext, lanes]
            s = _dot(kb, qb, _NT) + b_ref[a]
            if mask_rows is not None:
                s = jnp.where(mask_rows, NEG, s)
            p_t = jnp.exp(s - lse_row)
            ds_t = p_t * (_dot(vb, dob, _NT) - dl_row)
            ds_b = ds_t.astype(BF16)
            dkext_ref[ext, lanes] += _dot(ds_b, qb, _NN)
            dvext_ref[ext, lanes] += _dot(p_t.astype(BF16), dob, _NN)
            return ds_t, ds_b, kb

        key = lax.broadcasted_iota(jnp.int32, (2 * BAND, BAND), 0)
        for a in range(2):
            lanes = slice(a * HEAD_DIM, (a + 1) * HEAD_DIM)
            db_acc = jnp.zeros((2 * BAND, BAND), F32)
            for b in range(nb):
                rows, ext = slice(b * BAND, (b + 1) * BAND), slice(b * BAND, (b + 2) * BAND)
                mask = jnp.logical_and(first, key < BAND) if b == 0 else None
                ds_t, ds_b, kb = block(a, q_ref[rows, lanes] * QK_SCALE, do_ref[rows, lanes],
                                       lse_ref[a:a + 1, rows], dl_ref[a:a + 1, rows], ext, mask)
                dq_ref[rows, lanes] = _dot(ds_b, kb, _TN) * QK_SCALE
                db_acc = db_acc + ds_t
            db_ref[a] += db_acc
            block(a, qn_ref[:, lanes] * QK_SCALE, don_ref[:, lanes], lsen_ref[a:a + 1, :], dln_ref[a:a + 1, :],
                  slice(chunk, chunk + 2 * BAND), jnp.logical_or(jnp.logical_not(has_next), key >= BAND))
        dk_ref[...] = dkext_ref[BAND:BAND + chunk, :]
        dv_ref[...] = dvext_ref[BAND:BAND + chunk, :]

    ext_rows = chunk + 2 * BAND
    return pl.pallas_call(
        body, name=name, grid=(HEAD_PAIRS, d, nchunks),
        in_specs=[main[0], main[1], prev[1], main[2], prev[2], tok, bias, stat, stat,
                  q_next, do_next, stat_next, stat_next],
        out_specs=[tok, tok, tok, bias],
        out_shape=[jax.ShapeDtypeStruct((d, L, DIL_WIDTH), F32)] * 3
                  + [jax.ShapeDtypeStruct((HEAD_PAIRS, 2, 2 * BAND, BAND), F32)],
        scratch_shapes=[pltpu.VMEM((ext_rows, LANES), BF16), pltpu.VMEM((ext_rows, LANES), BF16),
                        pltpu.VMEM((ext_rows, LANES), F32), pltpu.VMEM((ext_rows, LANES), F32)],
        compiler_params=_params("arbitrary", "arbitrary", "arbitrary"),
    )(qkv_v, qkv_v, qkv_v, qkv_v, qkv_v, do_v, bias_t, lse_joint, delta, qkv_v, do_v, lse_joint, delta)


def _lse_join(lse3, name):
    P, H, S = lse3.shape

    def body(l_ref, o_ref):
        a, b, c = l_ref[0], l_ref[1], l_ref[2]
        m = jnp.maximum(jnp.maximum(a, b), c)
        o_ref[...] = m + jnp.log(jnp.exp(a - m) + jnp.exp(b - m) + jnp.exp(c - m))

    return pl.pallas_call(body, name=name, out_shape=jax.ShapeDtypeStruct((H, S), F32))(lse3)


def _bucket_reduce(dbias_t, bucket_map_t, name):
    P, H = dbias_t.shape[:2]

    def body(db_ref, bk_ref, o_ref):
        p, h = pl.program_id(0), pl.program_id(1)

        @pl.when(jnp.logical_and(p == 0, h == 0))
        def _():
            o_ref[...] = jnp.zeros_like(o_ref)

        db, bk = db_ref[...], bk_ref[...]
        row = lax.broadcasted_iota(jnp.int32, (N_BUCKETS, LANES), 0)
        lane = lax.broadcasted_iota(jnp.int32, (N_BUCKETS, LANES), 1)

        def one(b, acc):
            val = jnp.sum(jnp.sum(jnp.where(bk == b, db, 0.0), axis=1, keepdims=True), axis=0, keepdims=True)
            return acc + jnp.where(jnp.logical_and(row == b, lane == h), val, 0.0)

        o_ref[...] += lax.fori_loop(0, N_BUCKETS, one, jnp.zeros((N_BUCKETS, LANES), F32))

    return pl.pallas_call(
        body, name=name, grid=(P, H),
        in_specs=[pl.BlockSpec((None, None, 2 * BAND, BAND), lambda p, h: (p, h, 0, 0)),
                  pl.BlockSpec((None, 2 * BAND, BAND), lambda p, h: (p, 0, 0))],
        out_specs=pl.BlockSpec((N_BUCKETS, LANES), lambda p, h: (0, 0)),
        out_shape=jax.ShapeDtypeStruct((N_BUCKETS, LANES), F32),
        compiler_params=_params("arbitrary", "arbitrary"),
    )(dbias_t, bucket_map_t)


def _mem_fwd(q, kv, name, tq=1024):
    S, W = q.shape
    N = kv.shape[0]
    pairs = W // LANES
    tq = _fit(S, tq)

    def body(q_ref, k_ref, v_ref, o_ref, lse_ref):
        for a in range(2):
            lanes = slice(a * HEAD_DIM, (a + 1) * HEAD_DIM)
            s = _dot(k_ref[:, lanes], q_ref[:, lanes] * QK_SCALE, _NT)
            m = jnp.max(s, axis=0, keepdims=True)
            e = jnp.exp(s - m)
            l = jnp.sum(e, axis=0, keepdims=True)
            o_ref[:, lanes] = _dot((e / l).astype(BF16), v_ref[:, lanes], _TN).astype(BF16)
            lse_ref[a:a + 1, :] = m + jnp.log(l)

    return pl.pallas_call(
        body, name=name, grid=(pairs, S // tq),
        in_specs=[pl.BlockSpec((tq, LANES), lambda hp, i: (i, hp)),
                  pl.BlockSpec((N, LANES), lambda hp, i: (0, hp)),
                  pl.BlockSpec((N, LANES), lambda hp, i: (0, pairs + hp))],
        out_specs=[pl.BlockSpec((tq, LANES), lambda hp, i: (i, hp)),
                   pl.BlockSpec((None, 2, tq), lambda hp, i: (hp, 0, i))],
        out_shape=[jax.ShapeDtypeStruct((S, W), BF16), jax.ShapeDtypeStruct((pairs, 2, S), F32)],
        compiler_params=_params("parallel", "parallel"),
    )(q, kv, kv)


def _mem_bwd(q, kv, do, lse, delta, name, tq=1024):
    S, W = q.shape
    N = kv.shape[0]
    pairs = W // LANES
    tq = _fit(S, tq)

    def body(q_ref, k_ref, v_ref, do_ref, lse_ref, dl_ref, dq_ref, dk_ref, dv_ref):
        i = pl.program_id(1)

        @pl.when(i == 0)
        def _():
            dk_ref[...] = jnp.zeros_like(dk_ref)
            dv_ref[...] = jnp.zeros_like(dv_ref)

        for a in range(2):
            lanes = slice(a * HEAD_DIM, (a + 1) * HEAD_DIM)
            qv, dov = q_ref[:, lanes] * QK_SCALE, do_ref[:, lanes]
            kv_, vv = k_ref[:, lanes], v_ref[:, lanes]
            p_t = jnp.exp(_dot(kv_, qv, _NT) - lse_ref[a:a + 1, :])
            ds_t = p_t * (_dot(vv, dov, _NT) - dl_ref[a:a + 1, :])
            ds_b = ds_t.astype(BF16)
            dq_ref[:, lanes] = (_dot(ds_b, kv_, _TN) * QK_SCALE).astype(BF16)
            dk_ref[:, lanes] += _dot(ds_b, qv, _NN)
            dv_ref[:, lanes] += _dot(p_t.astype(BF16), dov, _NN)

    qs = pl.BlockSpec((tq, LANES), lambda hp, i: (i, hp))
    stat = pl.BlockSpec((None, 2, tq), lambda hp, i: (hp, 0, i))
    acc = pl.BlockSpec((N, LANES), lambda hp, i: (0, hp))
    return pl.pallas_call(
        body, name=name, grid=(pairs, S // tq),
        in_specs=[qs, acc, pl.BlockSpec((N, LANES), lambda hp, i: (0, pairs + hp)), qs, stat, stat],
        out_specs=[qs, acc, acc],
        out_shape=[jax.ShapeDtypeStruct((S, W), BF16), jax.ShapeDtypeStruct((N, W), F32),
                   jax.ShapeDtypeStruct((N, W), F32)],
        compiler_params=_params("parallel", "arbitrary"),
    )(q, kv, kv, do, lse, delta)


def _head_rowdot(a, bs, name, tr=512):
    S, W = a.shape
    tr = _fit(S, tr)

    def body(*refs):
        a_ref, b_refs, o_ref, tmp_ref = refs[0], refs[1:-2], refs[-2], refs[-1]
        col = lax.broadcasted_iota(jnp.int32, (LANES, LANES), 0)
        lane = lax.broadcasted_iota(jnp.int32, (LANES, LANES), 1)
        acc = jnp.zeros((tr, LANES), F32)
        for j in range(W // LANES):
            cols = slice(j * LANES, (j + 1) * LANES)
            tot = _token_rows(b_refs[0], cols, tmp_ref)
            for r in b_refs[1:]:
                tot = tot + _token_rows(r, cols, tmp_ref)
            sel = jnp.where(col // HEAD_DIM + j * (LANES // HEAD_DIM) == lane, 1.0, 0.0).astype(F32)
            acc = acc + lax.dot_general(a_ref[:, cols].astype(F32) * tot, sel, (_NN, ((), ())),
                                        precision=lax.Precision.HIGHEST, preferred_element_type=F32)
        o_ref[...] = acc

    return pl.pallas_call(
        body, name=name, grid=(S // tr,), in_specs=[_row_spec(t, tr) for t in [a] + list(bs)],
        out_specs=pl.BlockSpec((tr, LANES), lambda i: (i, 0)),
        out_shape=jax.ShapeDtypeStruct((S, LANES), F32),
        scratch_shapes=[pltpu.VMEM((tr, LANES), F32)],
        compiler_params=_params("parallel"),
    )(a, *bs)


def _sum_cast_cols(groups, out_dtype, name, tail=None, tr=256):
    first = groups[0][0]
    S, W = (first.shape if first.ndim == 2 else (first.shape[0] * first.shape[1], first.shape[2]))
    tr = _fit(S, tr)
    flat = [t for g in groups for t in g] + ([tail] if tail is not None else [])
    tail_w = 0 if tail is None else tail.shape[1]

    def body(*refs):
        o_ref, tmp_ref = refs[-2], refs[-1]
        if tail is not None:
            o_ref[:, W * len(groups):] = refs[-3][...].astype(out_dtype)
        k = 0
        for gi, g in enumerate(groups):
            for j in range(W // LANES):
                cols = slice(j * LANES, (j + 1) * LANES)
                acc = _token_rows(refs[k], cols, tmp_ref)
                for r in refs[k + 1:k + len(g)]:
                    acc = acc + _token_rows(r, cols, tmp_ref)
                o_ref[:, gi * W + j * LANES:gi * W + (j + 1) * LANES] = acc.astype(out_dtype)
            k += len(g)

    return pl.pallas_call(
        body, name=name, grid=(S // tr,), in_specs=[_row_spec(t, tr) for t in flat],
        out_specs=pl.BlockSpec((tr, W * len(groups) + tail_w), lambda i: (i, 0)),
        out_shape=jax.ShapeDtypeStruct((S, W * len(groups) + tail_w), out_dtype),
        scratch_shapes=[pltpu.VMEM((tr, LANES), F32)],
        compiler_params=_params("parallel"),
    )(*flat)


FF_TILE = 256


def _ffn_up(h, w_gu, name, tm=1024):
    S, D = h.shape
    F2 = w_gu.shape[1]
    tm = _fit(S, tm)

    def body(h_ref, w_ref, gu_ref, act_ref):
        gu = _dot(h_ref[...], w_ref[...], _NN)
        gu_ref[...] = gu.astype(BF16)
        g, u = gu[:, :FF_TILE], gu[:, FF_TILE:]
        act_ref[...] = (g * (1.0 / (1.0 + jnp.exp(-g))) * u).astype(BF16)

    return pl.pallas_call(
        body, name=name, grid=(S // tm, F2 // (2 * FF_TILE)),
        in_specs=[pl.BlockSpec((tm, D), lambda i, j: (i, 0)), pl.BlockSpec((D, 2 * FF_TILE), lambda i, j: (0, j))],
        out_specs=[pl.BlockSpec((tm, 2 * FF_TILE), lambda i, j: (i, j)),
                   pl.BlockSpec((tm, FF_TILE), lambda i, j: (i, j))],
        out_shape=[jax.ShapeDtypeStruct((S, F2), BF16), jax.ShapeDtypeStruct((S, F2 // 2), BF16)],
        compiler_params=_params("parallel", "arbitrary"),
    )(h, w_gu)


def _ffn_dact(dy, w_down, gu, name, tm=1024):
    S, D = dy.shape
    F2 = gu.shape[1]
    tm = _fit(S, tm)

    def body(dy_ref, w_ref, gu_ref, dgu_ref):
        dact = _dot(dy_ref[...], w_ref[...], _NT)
        gu_v = gu_ref[...].astype(F32)
        g, u = gu_v[:, :FF_TILE], gu_v[:, FF_TILE:]
        sig = 1.0 / (1.0 + jnp.exp(-g))
        silu = g * sig
        dgu_ref[:, :FF_TILE] = (dact * u * (sig + silu * (1.0 - sig))).astype(BF16)
        dgu_ref[:, FF_TILE:] = (dact * silu).astype(BF16)

    return pl.pallas_call(
        body, name=name, grid=(S // tm, F2 // (2 * FF_TILE)),
        in_specs=[pl.BlockSpec((tm, D), lambda i, j: (i, 0)), pl.BlockSpec((FF_TILE, D), lambda i, j: (j, 0)),
                  pl.BlockSpec((tm, 2 * FF_TILE), lambda i, j: (i, j))],
        out_specs=pl.BlockSpec((tm, 2 * FF_TILE), lambda i, j: (i, j)),
        out_shape=jax.ShapeDtypeStruct((S, F2), BF16),
        compiler_params=_params("parallel", "arbitrary"),
    )(dy, w_down, gu)


def _fit_rows(n, cap):
    if n <= cap:
        return n
    t = (cap // 8) * 8
    while t >= 8:
        if n % t == 0:
            return t
        t -= 8
    raise ValueError(f"no sublane-aligned tile for {n} under {cap}")


def _add_n(arrs, name, tr=512):
    R, C = arrs[0].shape
    tr = _fit_rows(R, tr)

    def body(*refs):
        acc = refs[0][...]
        for r in refs[1:-1]:
            acc = acc + r[...]
        refs[-1][...] = acc

    row = pl.BlockSpec((tr, C), lambda i: (i, 0))
    return pl.pallas_call(
        body, name=name, grid=(R // tr,), in_specs=[row] * len(arrs), out_specs=row,
        out_shape=jax.ShapeDtypeStruct((R, C), F32), compiler_params=_params("parallel"),
    )(*arrs)


def _adamw(w, g, m, v, name, tr=512):
    R, C = w.shape
    tr = _fit_rows(R, tr)
    c1 = 1.0 / (1.0 - ADAM_B1 ** ADAM_STEP)
    c2 = 1.0 / (1.0 - ADAM_B2 ** ADAM_STEP)

    def body(w_ref, g_ref, m_ref, v_ref, d_ref, nm_ref, nv_ref):
        gv = g_ref[...]
        nm = ADAM_B1 * m_ref[...] + (1.0 - ADAM_B1) * gv
        nv = ADAM_B2 * v_ref[...] + (1.0 - ADAM_B2) * (gv * gv)
        nm_ref[...] = nm
        nv_ref[...] = nv
        d_ref[...] = -ADAM_LR * ((nm * c1) / (jnp.sqrt(nv * c2) + ADAM_EPS) + ADAM_WD * w_ref[...])

    row = pl.BlockSpec((tr, C), lambda i: (i, 0))
    return pl.pallas_call(
        body, name=name, grid=(R // tr,), in_specs=[row] * 4, out_specs=[row] * 3,
        out_shape=[jax.ShapeDtypeStruct((R, C), F32)] * 3, compiler_params=_params("parallel"),
    )(w, g, m, v)


def _place():
    return lax.axis_index("x"), lax.axis_index("y"), lax.axis_index("c")


_ANY = pl.BlockSpec(memory_space=pl.ANY)


def _chip_all_gather(shard, name):
    R, C = shard.shape
    half = R // 2

    def body(x_ref, out_ref, send_sems, recv_sems, local_sem):
        x, y, c = _place()
        chips = [(1 - x, y), (x, 1 - y), (1 - x, 1 - y)]
        sibling = (x, y, 1 - c)
        mine = pltpu.make_async_copy(x_ref, out_ref.at[2 * x + y], local_sem)
        mine.start()

        def rows(chip, core):
            return out_ref.at[chip, pl.ds(core * half, half)]

        def copy(k, chip, core, to, src=None):
            return pltpu.make_async_remote_copy(
                src_ref=rows(chip, core) if src is None else src, dst_ref=rows(chip, core),
                send_sem=send_sems.at[k], recv_sem=recv_sems.at[k], device_id=to, device_id_type=MESH_IDS)

        me = 2 * x + y
        first = [copy(k, me, c, (cx, cy, c), src=x_ref.at[pl.ds(c * half, half)]) for k, (cx, cy) in enumerate(chips)]
        for cp in first:
            cp.start()
        passed = [copy(3 + k, 2 * cx + cy, c, sibling) for k, (cx, cy) in enumerate(chips)]
        for k, (cx, cy) in enumerate(chips):
            copy(k, 2 * cx + cy, c, (cx, cy, c)).wait_recv()
            passed[k].start()
        for k, (cx, cy) in enumerate(chips):
            copy(3 + k, 2 * cx + cy, 1 - c, sibling).wait_recv()
        for cp in first + passed:
            cp.wait_send()
        mine.wait()

    return pl.pallas_call(
        body, name=name, in_specs=[_ANY], out_specs=_ANY,
        out_shape=jax.ShapeDtypeStruct((N_CHIPS, R, C), shard.dtype),
        scratch_shapes=[pltpu.SemaphoreType.DMA((6,)), pltpu.SemaphoreType.DMA((6,)), pltpu.SemaphoreType.DMA],
    )(shard)


def _sibling_exchange(buf, name):
    def body(x_ref, out_ref, send_sem, recv_sem):
        x, y, c = _place()
        cp = pltpu.make_async_remote_copy(
            src_ref=x_ref, dst_ref=out_ref, send_sem=send_sem, recv_sem=recv_sem,
            device_id=(x, y, 1 - c), device_id_type=MESH_IDS)
        cp.start()
        cp.wait()

    return pl.pallas_call(
        body, name=name, in_specs=[_ANY], out_specs=_ANY,
        out_shape=jax.ShapeDtypeStruct(buf.shape, buf.dtype),
        scratch_shapes=[pltpu.SemaphoreType.DMA, pltpu.SemaphoreType.DMA],
    )(buf)


def _chip_scatter(parts, name):
    _, R, C = parts.shape

    def body(p_ref, out_ref, send_sems, recv_sems):
        x, y, c = _place()
        chips = [(1 - x, y), (x, 1 - y), (1 - x, 1 - y)]

        def copy(k, slab, to):
            return pltpu.make_async_remote_copy(
                src_ref=p_ref.at[slab], dst_ref=out_ref.at[k], send_sem=send_sems.at[k], recv_sem=recv_sems.at[k],
                device_id=to, device_id_type=MESH_IDS)

        sends = [copy(k, 2 * cx + cy, (cx, cy, c)) for k, (cx, cy) in enumerate(chips)]
        for cp in sends:
            cp.start()
        for cp in sends:
            cp.wait_recv()
        for cp in sends:
            cp.wait_send()

    return pl.pallas_call(
        body, name=name, in_specs=[_ANY], out_specs=_ANY,
        out_shape=jax.ShapeDtypeStruct((3, R, C), parts.dtype),
        scratch_shapes=[pltpu.SemaphoreType.DMA((3,)), pltpu.SemaphoreType.DMA((3,))],
    )(parts)


def _all_to_all_small(vec, name):
    R, C = vec.shape

    def body(v_ref, out_ref, send_sems, recv_sems, local_sem):
        x, y, c = _place()
        me = 4 * x + 2 * y + c
        mine = pltpu.make_async_copy(v_ref, out_ref.at[me], local_sem)
        mine.start()
        flips = [(dx, dy, dc) for dx in (0, 1) for dy in (0, 1) for dc in (0, 1)][1:]

        def peer(f):
            return (x ^ f[0], y ^ f[1], c ^ f[2])

        def copy(k, slot, to):
            return pltpu.make_async_remote_copy(
                src_ref=v_ref, dst_ref=out_ref.at[slot], send_sem=send_sems.at[k], recv_sem=recv_sems.at[k],
                device_id=to, device_id_type=MESH_IDS)

        sends = [copy(k, me, peer(f)) for k, f in enumerate(flips)]
        for cp in sends:
            cp.start()
        for k, f in enumerate(flips):
            px, py, pc = peer(f)
            copy(k, 4 * px + 2 * py + pc, peer(f)).wait_recv()
        for cp in sends:
            cp.wait_send()
        mine.wait()

    return pl.pallas_call(
        body, name=name, in_specs=[_ANY], out_specs=_ANY,
        out_shape=jax.ShapeDtypeStruct((8, R, C), vec.dtype),
        scratch_shapes=[pltpu.SemaphoreType.DMA((7,)), pltpu.SemaphoreType.DMA((7,)), pltpu.SemaphoreType.DMA],
    )(vec)


def _to_heads(t, n):
    S = t.shape[0]
    return t.reshape(S, n, HEAD_DIM).transpose(1, 0, 2)


def _to_heads_t(t, n):
    S = t.shape[0]
    return t.T.reshape(n, HEAD_DIM, S)


def _from_heads_t(t):
    H, Dh, S = t.shape
    return t.reshape(H * Dh, S).T


def _rep(t):
    return jnp.broadcast_to(t[..., None], t.shape + (LANES,))


def _t5_bucket(dist):
    max_exact = N_BUCKETS // 2
    d = np.maximum(dist, 1).astype(np.float32)
    large = max_exact + (np.log(d / max_exact) / np.log(MAX_DISTANCE / max_exact)
                         * (N_BUCKETS - max_exact)).astype(np.int32)
    large = np.minimum(large, N_BUCKETS - 1)
    return np.where(dist < max_exact, dist, large).astype(np.int32)


def _band_tables():
    qi = np.arange(BAND)[:, None]
    kj = np.arange(2 * BAND)[None, :]
    sub = qi + BAND - kj
    band = (sub >= 0) & (sub <= BAND)
    out = []
    for d in DILATIONS:
        bucket = _t5_bucket(np.clip(sub, 0, BAND) * d)
        out.append(np.where(band, bucket, -1).astype(np.int32))
    return np.stack(out)


_PACK = (("w_in", 770), ("w_out", 256), ("w_xq", 64), ("w_xk", 64), ("w_xv", 64), ("w_xo", 64),
         ("w_gate", 704), ("w_up", 704), ("w_down", 704))


def _pack(shards):
    rows = [shards[n].reshape(-1, PACK_COLS) for n, _ in _PACK]
    total = sum(r.shape[0] for r in rows)
    pad = (-total) % 128
    if pad:
        rows.append(jnp.zeros((pad, PACK_COLS), rows[0].dtype))
    return jnp.concatenate(rows, axis=0)


def _unpack(pack, shapes):
    out, r = {}, 0
    for n, _ in _PACK:
        cnt = int(np.prod(shapes[n])) // PACK_COLS
        out[n] = pack[r:r + cnt].reshape(shapes[n])
        r += cnt
    return out


_COL_SHARDED = ("w_in", "w_xo", "w_gate", "w_up")


def _full_weight(gathered, name):
    return jnp.concatenate(gathered, axis=1 if name in _COL_SHARDED else 0)


def _split_weight(full, name):
    return jnp.split(full, N_CHIPS, axis=1 if name in _COL_SHARDED else 0)


_SMALL = ("g_mix_pre", "g_mix_post", "g_xattn_pre", "g_mem", "g_xattn_post", "g_ffn_pre", "g_ffn_post")


def _pack_small(vals):
    D = vals["g_mix_pre"].shape[1]
    rows = [vals[n].reshape(1, D) for n in _SMALL]
    misc = jnp.concatenate([vals["b_f"].reshape(-1), vals["rel_bias"].reshape(-1)])
    rows.append(jnp.pad(misc, (0, D - misc.shape[0])).reshape(1, D))
    rows.append(jnp.zeros((16 - len(rows), D), F32))
    return jnp.concatenate(rows, axis=0)


def _unpack_small(pack):
    out = {n: pack[i:i + 1] for i, n in enumerate(_SMALL)}
    out["b_f"] = pack[7, 0:N_FOX_HEADS].reshape(1, N_FOX_HEADS)
    out["rel_bias"] = pack[7, N_FOX_HEADS:N_FOX_HEADS + N_BUCKETS * N_DIL_HEADS].reshape(N_BUCKETS, N_DIL_HEADS)
    return out


def kernel(x, mem, g_mix_pre, w_in, b_f, rel_bias, w_out, g_mix_post, g_xattn_pre, g_mem, w_xq, w_xk, w_xv, w_xo, g_xattn_post, g_ffn_pre, w_gate, w_up, w_down, g_ffn_post, loss_target, m_g_mix_pre, m_w_in, m_b_f, m_rel_bias, m_w_out, m_g_mix_post, m_g_xattn_pre, m_g_mem, m_w_xq, m_w_xk, m_w_xv, m_w_xo, m_g_xattn_post, m_g_ffn_pre, m_w_gate, m_w_up, m_w_down, m_g_ffn_post, v_g_mix_pre, v_w_in, v_b_f, v_rel_bias, v_w_out, v_g_mix_post, v_g_xattn_pre, v_g_mem, v_w_xq, v_w_xk, v_w_xv, v_w_xo, v_g_xattn_post, v_g_ffn_pre, v_w_gate, v_w_up, v_w_down, v_g_ffn_post):
    args = dict(locals())
    big = [n for n, _ in _PACK]
    names = ["g_mix_pre", "w_in", "b_f", "rel_bias", "w_out", "g_mix_post", "g_xattn_pre", "g_mem", "w_xq",
             "w_xk", "w_xv", "w_xo", "g_xattn_post", "g_ffn_pre", "w_gate", "w_up", "w_down", "g_ffn_post"]
    xs = x[0]
    S, D = xs.shape
    assert S % (BAND * DILATIONS[-1]) == 0
    shard_shapes = {n: args[n].shape[1:] for n in big}
    my_x, my_y, my_c = lax.axis_index("x"), lax.axis_index("y"), lax.axis_index("c")

    w_pack = _pack({n: args[n][0] for n in big})
    gathered = _chip_all_gather(w_pack.astype(BF16), "weights_all_gather")
    per_chip = [_unpack(gathered[j], shard_shapes) for j in range(N_CHIPS)]
    W = {n: _full_weight([pc[n] for pc in per_chip], n) for n in big}
    w_fox, w_fg, w_dil = (W["w_in"][:, :3 * FOX_WIDTH], W["w_in"][:, 3 * FOX_WIDTH:3 * FOX_WIDTH + N_FOX_HEADS],
                          W["w_in"][:, 3 * FOX_WIDTH + N_FOX_HEADS:])
    w_qkv = jnp.concatenate([w_fox, w_dil], axis=1)
    w_fg_pad = jnp.pad(w_fg, ((0, 0), (0, LANES - N_FOX_HEADS)))
    F = W["w_gate"].shape[1]
    nft = F // FF_TILE
    w_gu = jnp.stack([W["w_gate"].reshape(D, nft, FF_TILE), W["w_up"].reshape(D, nft, FF_TILE)],
                     axis=2).reshape(D, 2 * F)

    h1 = _rms_fwd(xs, g_mix_pre, "rms_mix_pre")
    qkv = _mm(h1, w_qkv, "nn", BF16, "proj_qkv")
    fg = _mm(h1, w_fg_pad, "nn", F32, "proj_gate")
    fg_t = fg[:, :N_FOX_HEADS].T
    b_col = b_f.reshape(N_FOX_HEADS, 1)
    c_t = _forget_fwd(fg_t, b_col, "forget_cumsum")
    c_row = c_t.reshape(N_FOX_HEADS, 1, S)
    c_rep = _rep(c_t)
    fq_s, fk_s, fv_s = (qkv[:, i * FOX_WIDTH:(i + 1) * FOX_WIDTH] for i in range(3))
    fqt, fkt, fvt = (_to_heads_t(t, N_FOX_HEADS) for t in (fq_s, fk_s, fv_s))
    fk, fv = _to_heads(fk_s, N_FOX_HEADS), _to_heads(fv_s, N_FOX_HEADS)
    o_fox_t, lse_fox = _fox_fwd(fqt, fk, _with_ones(fvt), c_row, c_rep, "fox_fwd")

    bucket_map = _band_tables()
    onehot = (jnp.asarray(bucket_map)[..., None] == jnp.arange(N_BUCKETS)).astype(F32)
    bias_tab = jnp.einsum("pqkb,bh->phkq", onehot, rel_bias, precision=lax.Precision.HIGHEST)
    bias_tab = jnp.where(jnp.asarray(bucket_map.transpose(0, 2, 1) >= 0)[:, None], bias_tab, NEG)
    bias_t = bias_tab.reshape(3, HEAD_PAIRS, 2, 2 * BAND, BAND)
    views = [(qkv.reshape(1, S, qkv.shape[1]), DIL_Q_BLOCK)] + [
        (_to_residues(qkv, 1, 3 * DIL_WIDTH, d, f"dilated_qkv_residues_{d}"), 0) for d in DILATIONS[1:]]

    def to_tok(stat, d):
        return stat.reshape(N_DIL_HEADS, d, S // d).swapaxes(1, 2).reshape(N_DIL_HEADS, S)

    def to_perm(stat, d):
        return stat.reshape(N_DIL_HEADS, S // d, d).swapaxes(1, 2).reshape(HEAD_PAIRS, 2, S)

    def tok_or_res(t):
        return t.reshape(t.shape[1:]) if t.shape[0] == 1 else t

    lse_tok = jnp.stack([to_tok(_dil_lse(*views[p], bias_t[p], f"dilated_lse_{d}"), d)
                         for p, d in enumerate(DILATIONS)])
    lse_joint = _lse_join(lse_tok, "dilated_lse_join")
    lse_perm = [to_perm(lse_joint, d) for d in DILATIONS]
    o_dil = [tok_or_res(_dil_out(*views[p], bias_t[p], lse_perm[p], f"dilated_out_{d}"))
             for p, d in enumerate(DILATIONS)]
    o_cat = _sum_cast_cols([[_from_heads_t(o_fox_t)]] + [[o] for o in o_dil], BF16, "mixer_out_cat")
    w_out_b = W["w_out"]
    w_out_cat = jnp.concatenate([w_out_b[:FOX_WIDTH]] + [w_out_b[FOX_WIDTH:]] * 3, axis=0)
    a = _mm(o_cat, w_out_cat, "nn", F32, "proj_out")
    x1, h2 = _resid_norm(xs, a, g_mix_post, g_xattn_pre, "resid_mix")

    hm = _rms_fwd(mem[0], g_mem, "rms_mem")
    q2 = _mm(h2, W["w_xq"], "nn", BF16, "xattn_q")
    w_xkv = jnp.concatenate([W["w_xk"], W["w_xv"]], axis=1)
    kvm = _mm(hm, w_xkv, "nn", BF16, "xattn_kv")
    MW = N_MEM_HEADS * HEAD_DIM
    oc, lse_mem = _mem_fwd(q2, kvm, "xattn_fwd")
    y2 = _mm(oc, W["w_xo"], "nn", F32, "xattn_o")
    x2, h3 = _resid_norm(x1, y2, g_xattn_post, g_ffn_pre, "resid_xattn")

    gu, act = _ffn_up(h3, w_gu, "ffn_up")
    y3 = _mm(act, W["w_down"], "nn", F32, "ffn_down", tk=1536)
    dx3, loss_tile = _final_loss(x2, y3, g_ffn_post, loss_target[0], "final_loss")

    grads = {}
    small = {}
    _, dy3_b, dg = _rms_bwd(y3, g_ffn_post, dx3, None, "bwd_norm_ffn_post", want=("bf16",))
    small["g_ffn_post"] = dg[0:1]
    grads["w_down"] = _mm(act, dy3_b, "tn", F32, "grad_w_down", tm=1408)
    dgu = _ffn_dact(dy3_b, W["w_down"], gu, "ffn_dact")
    dw_gu = _mm(h3, dgu, "tn", F32, "grad_w_gu").reshape(D, nft, 2, FF_TILE)
    grads["w_gate"], grads["w_up"] = dw_gu[:, :, 0].reshape(D, F), dw_gu[:, :, 1].reshape(D, F)
    dh3 = _mm(dgu, w_gu, "nt", F32, "bwd_ffn_in", tk=1024)
    dx2, _, dg = _rms_bwd(x2, g_ffn_pre, dh3, dx3, "bwd_norm_ffn_pre", want=("f32",))
    small["g_ffn_pre"] = dg[0:1]

    _, dy2_b, dg = _rms_bwd(y2, g_xattn_post, dx2, None, "bwd_norm_xattn_post", want=("bf16",))
    small["g_xattn_post"] = dg[0:1]
    grads["w_xo"] = _mm(oc, dy2_b, "tn", F32, "grad_w_xo")
    doc = _mm(dy2_b, W["w_xo"], "nt", BF16, "bwd_xattn_o")
    delta_mem = _head_rowdot(doc, [oc], "xattn_delta")[:, :N_MEM_HEADS].T.reshape(N_MEM_HEADS // 2, 2, S)
    dq2, dkm, dvm = _mem_bwd(q2, kvm, doc, lse_mem, delta_mem, "xattn_bwd")
    dkvm = jnp.concatenate([dkm, dvm], axis=1).astype(BF16)
    grads["w_xq"] = _mm(h2, dq2, "tn", F32, "grad_w_xq")
    dw_xkv = _mm(hm, dkvm, "tn", F32, "grad_w_xkv")
    grads["w_xk"], grads["w_xv"] = dw_xkv[:, :MW], dw_xkv[:, MW:]
    dhm = _mm(dkvm, w_xkv, "nt", F32, "bwd_xattn_kv")
    _, _, dg = _rms_bwd(mem[0], g_mem, dhm, None, "bwd_norm_mem", want=())
    small["g_mem"] = dg[0:1]
    dh2 = _mm(dq2, W["w_xq"], "nt", F32, "bwd_xattn_q")
    dx1, _, dg = _rms_bwd(x1, g_xattn_pre, dh2, dx2, "bwd_norm_xattn_pre", want=("f32",))
    small["g_xattn_pre"] = dg[0:1]

    _, da_b, dg = _rms_bwd(a, g_mix_post, dx1, None, "bwd_norm_mix_post", want=("bf16",))
    small["g_mix_post"] = dg[0:1]
    dw_out_cat = _mm(o_cat, da_b, "tn", F32, "grad_w_out")
    dw_out_dil = _add_n([dw_out_cat[FOX_WIDTH + p * DIL_WIDTH:FOX_WIDTH + (p + 1) * DIL_WIDTH] for p in range(3)],
                        "grad_w_out_dil")
    grads["w_out"] = jnp.concatenate([dw_out_cat[:FOX_WIDTH], dw_out_dil], axis=0)
    do = _mm(da_b, w_out_b, "nt", BF16, "bwd_proj_out")
    do_fox, do_dil = do[:, :FOX_WIDTH], do[:, FOX_WIDTH:]

    delta_fox = _head_rowdot(do_fox, [o_cat[:, :FOX_WIDTH]], "fox_delta")[:, :N_FOX_HEADS].T
    dq_aug, dk_aug, dvf = _fox_bwd(_with_ones(fqt), fk, _with_ones(fkt), fv, _to_heads_t(do_fox, N_FOX_HEADS), c_rep,
                                   lse_fox, delta_fox.reshape(N_FOX_HEADS, 1, S), "fox_bwd")
    dqf, dkf = dq_aug[:, :HEAD_DIM], dk_aug[:, :HEAD_DIM]
    dfg_t, db_f = _forget_bwd(fg_t, b_col, dq_aug[:, HEAD_DIM], dk_aug[:, HEAD_DIM], "forget_bwd")

    delta_dil = _head_rowdot(do_dil, o_dil, "dilated_delta")[:, :N_DIL_HEADS].T
    do_res = [do_dil.reshape(1, S, DIL_WIDTH)] + [
        _to_residues(do, 1, DIL_WIDTH, d, f"dilated_do_residues_{d}") for d in DILATIONS[1:]]
    dil_grads = [_dil_bwd(*views[p], do_res[p], bias_t[p], lse_perm[p], to_perm(delta_dil, d), f"dilated_bwd_{d}")
                 for p, d in enumerate(DILATIONS)]
    dbias_t = jnp.stack([g[3].reshape(N_DIL_HEADS, 2 * BAND, BAND) for g in dil_grads])
    d_rel = _bucket_reduce(dbias_t, jnp.asarray(bucket_map.transpose(0, 2, 1)), "rel_bias_grad")[:, :N_DIL_HEADS]
    dfg_pad = jnp.pad(dfg_t.T, ((0, 0), (0, LANES - N_FOX_HEADS))).astype(BF16)
    dcat = _sum_cast_cols([[_from_heads_t(dqf)], [_from_heads_t(dkf)], [_from_heads_t(dvf)]]
                          + [[tok_or_res(g[j]) for g in dil_grads] for j in range(3)],
                          BF16, "dqkv_assemble", tail=dfg_pad)
    dw_cat = _mm(h1, dcat, "tn", F32, "grad_w_qkv")
    n_qkv = 3 * (FOX_WIDTH + DIL_WIDTH)
    grads["w_in"] = jnp.concatenate([dw_cat[:, :3 * FOX_WIDTH], dw_cat[:, n_qkv:n_qkv + N_FOX_HEADS],
                                     dw_cat[:, 3 * FOX_WIDTH:n_qkv]], axis=1)
    w_cat = jnp.concatenate([w_qkv, w_fg_pad], axis=1)
    dh1 = _mm(dcat, w_cat, "nt", F32, "bwd_proj_in", tk=640)
    grad_x, _, dg = _rms_bwd(xs, g_mix_pre, dh1, dx1, "bwd_norm_mix_pre", want=("f32",))
    small["g_mix_pre"] = dg[0:1]
    small["b_f"] = db_f[:, 0].reshape(1, N_FOX_HEADS)
    small["rel_bias"] = d_rel

    split = {n: _split_weight(grads[n], n) for n in big}
    parts = jnp.stack([_pack({n: split[n][j] for n in big}) for j in range(N_CHIPS)])
    R = parts.shape[1]
    half = R // 2
    keep = lax.dynamic_slice_in_dim(parts, my_c * half, half, axis=1)
    give = lax.dynamic_slice_in_dim(parts, (1 - my_c) * half, half, axis=1)
    got = _sibling_exchange(give, "grads_to_sibling")
    chip_sum = _add_n([keep.reshape(-1, PACK_COLS), got.reshape(-1, PACK_COLS)], "grads_add_sibling")
    chip_sum = chip_sum.reshape(N_CHIPS, half, PACK_COLS)
    my_chip = 2 * my_x + my_y
    from_chips = _chip_scatter(chip_sum.astype(BF16), "grads_to_chips")
    own = lax.dynamic_index_in_dim(chip_sum, my_chip, axis=0, keepdims=False)
    g_half = _add_n([own, from_chips[0], from_chips[1], from_chips[2]], "grads_add_chips")
    other_half = _sibling_exchange(g_half, "grads_share_sibling")
    g_pack = jnp.where(my_c == 0, jnp.concatenate([g_half, other_half]), jnp.concatenate([other_half, g_half]))

    small_pack = _pack_small(small)
    small_pack = small_pack.at[8, 0].set(loss_tile[0, 0])
    everyone = _all_to_all_small(small_pack, "small_all_gather")
    small_sum = _add_n([everyone[i] for i in range(8)], "small_sum")
    loss = small_sum[8, 0]
    g_small = _unpack_small(small_sum)

    m_pack = _pack({n: args["m_" + n][0] for n in big})
    v_pack = _pack({n: args["v_" + n][0] for n in big})
    d_pack, nm_pack, nv_pack = _adamw(w_pack, g_pack, m_pack, v_pack, "adamw_big")
    outs = {"grad": _unpack(g_pack, shard_shapes), "delta": _unpack(d_pack, shard_shapes),
            "new_m": _unpack(nm_pack, shard_shapes), "new_v": _unpack(nv_pack, shard_shapes)}
    sw = _pack_small({n: args[n] for n in _SMALL + ("b_f", "rel_bias")})
    sm = _pack_small({n: args["m_" + n] for n in _SMALL + ("b_f", "rel_bias")})
    sv = _pack_small({n: args["v_" + n] for n in _SMALL + ("b_f", "rel_bias")})
    sd, snm, snv = _adamw(sw, small_sum.at[8, 0].set(0.0), sm, sv, "adamw_small")
    souts = {"grad": g_small, "delta": _unpack_small(sd), "new_m": _unpack_small(snm), "new_v": _unpack_small(snv)}

    def leaf(kind, n):
        if n in souts[kind]:
            return souts[kind][n].reshape(args[n].shape)
        return outs[kind][n].reshape(args[n].shape)

    result = [loss, grad_x.reshape(x.shape)]
    for kind in ("grad", "delta", "new_m", "new_v"):
        result += [leaf(kind, n) for n in names]
    return tuple(result)
```

```python
import numpy as np
import jax
import jax.numpy as jnp
from jax import lax
from jax.experimental import pallas as pl
from jax.experimental.pallas import tpu as pltpu

F32 = jnp.float32
BF16 = jnp.bfloat16
MESH_IDS = pl.DeviceIdType.MESH

LANES = 128
HEAD_DIM = 64
N_FOX_HEADS = 8
N_DIL_HEADS = 8
N_MEM_HEADS = 4
FOX_WIDTH = N_FOX_HEADS * HEAD_DIM
DIL_WIDTH = N_DIL_HEADS * HEAD_DIM
DILATIONS = (1, 4, 16)
BAND = 128
BAND_CHUNK_MAX = 8 * BAND
N_BUCKETS = 32
MAX_DISTANCE = 2048
QK_SCALE = HEAD_DIM ** -0.5
RMS_EPS = 1e-6
NEG = -1e30
VMEM_LIMIT = 56 << 20

ADAM_LR = 0.001
ADAM_B1 = 0.9
ADAM_B2 = 0.999
ADAM_EPS = 1e-08
ADAM_WD = 0.01
ADAM_STEP = 10

N_CHIPS = 4
PACK_COLS = 1024


def _params(*sem):
    return pltpu.CompilerParams(dimension_semantics=sem, vmem_limit_bytes=VMEM_LIMIT)


def _fit(n, cap):
    if n <= cap:
        return n
    t = (cap // LANES) * LANES
    while t >= LANES:
        if n % t == 0:
            return t
        t -= LANES
    raise ValueError(f"no lane-aligned tile for {n} under {cap}")


def _dot(a, b, dims):
    return lax.dot_general(a, b, (dims, ((), ())), preferred_element_type=F32)


_NN = ((1,), (0,))
_NT = ((1,), (1,))
_TN = ((0,), (0,))


def _mm(a, b, mode, out_dtype, name, tm=1024, tn=1024, tk=512):
    if mode == "nn":
        (M, K), N = a.shape, b.shape[1]
    elif mode == "nt":
        (M, K), N = a.shape, b.shape[0]
    else:
        (K, M), N = a.shape, b.shape[1]
    tm, tn, tk = _fit(M, tm), _fit(N, tn), _fit(K, tk)
    nk = K // tk
    if mode == "tn":
        a_spec = pl.BlockSpec((tk, tm), lambda i, j, k: (k, i))
    else:
        a_spec = pl.BlockSpec((tm, tk), lambda i, j, k: (i, k))
    if mode == "nt":
        b_spec = pl.BlockSpec((tn, tk), lambda i, j, k: (j, k))
    else:
        b_spec = pl.BlockSpec((tk, tn), lambda i, j, k: (k, j))
    dims = {"nn": _NN, "nt": _NT, "tn": _TN}[mode]

    def body(a_ref, b_ref, o_ref, acc_ref):
        k = pl.program_id(2)

        @pl.when(k == 0)
        def _():
            acc_ref[...] = jnp.zeros_like(acc_ref)

        acc_ref[...] += _dot(a_ref[...].astype(BF16), b_ref[...].astype(BF16), dims)

        @pl.when(k == nk - 1)
        def _():
            o_ref[...] = acc_ref[...].astype(o_ref.dtype)

    return pl.pallas_call(
        body, name=name, grid=(M // tm, N // tn, nk),
        in_specs=[a_spec, b_spec],
        out_specs=pl.BlockSpec((tm, tn), lambda i, j, k: (i, j)),
        out_shape=jax.ShapeDtypeStruct((M, N), out_dtype),
        scratch_shapes=[pltpu.VMEM((tm, tn), F32)],
        compiler_params=_params("parallel", "parallel", "arbitrary"),
    )(a, b)


def _rms_rows(x):
    return lax.rsqrt(jnp.mean(x * x, axis=-1, keepdims=True) + RMS_EPS)


def _rms_fwd(x, g, name, tr=512):
    S, D = x.shape
    tr = _fit(S, tr)

    def body(x_ref, g_ref, h_ref):
        xv = x_ref[...]
        h_ref[...] = (xv * _rms_rows(xv) * g_ref[...]).astype(BF16)

    return pl.pallas_call(
        body, name=name, grid=(S // tr,),
        in_specs=[pl.BlockSpec((tr, D), lambda i: (i, 0)), pl.BlockSpec((1, D), lambda i: (0, 0))],
        out_specs=pl.BlockSpec((tr, D), lambda i: (i, 0)),
        out_shape=jax.ShapeDtypeStruct((S, D), BF16),
        compiler_params=_params("parallel"),
    )(x, g)


def _resid_norm(xres, y, g_post, g_next, name, tr=512):
    S, D = xres.shape
    tr = _fit(S, tr)

    def body(x_ref, y_ref, gp_ref, gn_ref, xn_ref, h_ref):
        yv = y_ref[...]
        xn = x_ref[...] + yv * _rms_rows(yv) * gp_ref[...]
        xn_ref[...] = xn
        h_ref[...] = (xn * _rms_rows(xn) * gn_ref[...]).astype(BF16)

    row = pl.BlockSpec((tr, D), lambda i: (i, 0))
    vec = pl.BlockSpec((1, D), lambda i: (0, 0))
    return pl.pallas_call(
        body, name=name, grid=(S // tr,),
        in_specs=[row, row, vec, vec], out_specs=[row, row],
        out_shape=[jax.ShapeDtypeStruct((S, D), F32), jax.ShapeDtypeStruct((S, D), BF16)],
        compiler_params=_params("parallel"),
    )(xres, y, g_post, g_next)


def _final_loss(xres, y, g_post, target, name, tr=512):
    S, D = xres.shape
    tr = _fit(S, tr)

    def body(x_ref, y_ref, gp_ref, t_ref, d_ref, loss_ref):
        i = pl.program_id(0)
        yv = y_ref[...]
        err = x_ref[...] + yv * _rms_rows(yv) * gp_ref[...] - t_ref[...]
        d_ref[...] = err * (1.0 / D)

        @pl.when(i == 0)
        def _():
            loss_ref[...] = jnp.zeros_like(loss_ref)

        part = jnp.sum(jnp.sum(err * err, axis=1, keepdims=True), axis=0, keepdims=True)
        loss_ref[...] += jnp.broadcast_to(part * (0.5 / D), loss_ref.shape)

    row = pl.BlockSpec((tr, D), lambda i: (i, 0))
    vec = pl.BlockSpec((1, D), lambda i: (0, 0))
    return pl.pallas_call(
        body, name=name, grid=(S // tr,),
        in_specs=[row, row, vec, row],
        out_specs=[row, pl.BlockSpec((8, LANES), lambda i: (0, 0))],
        out_shape=[jax.ShapeDtypeStruct((S, D), F32), jax.ShapeDtypeStruct((8, LANES), F32)],
        compiler_params=_params("arbitrary"),
    )(xres, y, g_post, target)


def _rms_bwd(xin, g, dy, dres, name, want=("f32", "bf16"), tr=512):
    S, D = xin.shape
    tr = _fit(S, tr)
    has_res = dres is not None

    def body(*refs):
        refs = list(refs)
        dg_ref = refs.pop()
        dxb_ref = refs.pop() if "bf16" in want else None
        dx_ref = refs.pop() if "f32" in want else None
        dr_ref = refs.pop() if has_res else None
        x_ref, g_ref, dy_ref = refs
        i = pl.program_id(0)
        xv = x_ref[...]
        dyv = dy_ref[...].astype(F32)
        xhat = xv * _rms_rows(xv)
        dxhat = dyv * g_ref[...]
        r = _rms_rows(xv)
        dx = r * (dxhat - xhat * jnp.mean(dxhat * xhat, axis=-1, keepdims=True))
        if has_res:
            dx = dx + dr_ref[...]
        if dx_ref is not None:
            dx_ref[...] = dx
        if dxb_ref is not None:
            dxb_ref[...] = dx.astype(BF16)

        @pl.when(i == 0)
        def _():
            dg_ref[...] = jnp.zeros_like(dg_ref)

        dg_ref[...] += jnp.broadcast_to(jnp.sum(dyv * xhat, axis=0, keepdims=True), dg_ref.shape)

    row = pl.BlockSpec((tr, D), lambda i: (i, 0))
    vec = pl.BlockSpec((1, D), lambda i: (0, 0))
    acc = pl.BlockSpec((8, D), lambda i: (0, 0))
    ins = [xin, g, dy] + ([dres] if has_res else [])
    dtypes = [dt for key, dt in (("f32", F32), ("bf16", BF16)) if key in want]
    outs = pl.pallas_call(
        body, name=name, grid=(S // tr,),
        in_specs=[row, vec, row] + ([row] if has_res else []),
        out_specs=[row] * len(dtypes) + [acc],
        out_shape=[jax.ShapeDtypeStruct((S, D), dt) for dt in dtypes] + [jax.ShapeDtypeStruct((8, D), F32)],
        compiler_params=_params("arbitrary"),
    )(*ins)
    by_key = dict(zip([key for key in ("f32", "bf16") if key in want], outs[:-1]))
    return by_key.get("f32"), by_key.get("bf16"), outs[-1]


def _tri(n, upper):
    r = lax.broadcasted_iota(jnp.int32, (n, n), 0)
    c = lax.broadcasted_iota(jnp.int32, (n, n), 1)
    return jnp.where((r <= c) if upper else (r >= c), 1.0, 0.0).astype(F32)


def _forget_fwd(fg_t, b_col, name, ts=512):
    H, S = fg_t.shape
    ts = _fit(S, ts)

    def body(f_ref, b_ref, c_ref, carry_ref):
        i = pl.program_id(0)

        @pl.when(i == 0)
        def _():
            carry_ref[...] = jnp.zeros_like(carry_ref)

        z = f_ref[...] + b_ref[...]
        logf = jnp.minimum(z, 0.0) - jnp.log(1.0 + jnp.exp(-jnp.abs(z)))
        run = lax.dot_general(logf, _tri(ts, True), (_NN, ((), ())), precision=lax.Precision.HIGHEST,
                              preferred_element_type=F32) + carry_ref[:, 0:1]
        c_ref[...] = run
        carry_ref[...] = jnp.broadcast_to(
            carry_ref[:, 0:1] + jnp.sum(logf, axis=1, keepdims=True), carry_ref.shape)

    return pl.pallas_call(
        body, name=name, grid=(S // ts,),
        in_specs=[pl.BlockSpec((H, ts), lambda i: (0, i)), pl.BlockSpec((H, 1), lambda i: (0, 0))],
        out_specs=pl.BlockSpec((H, ts), lambda i: (0, i)),
        out_shape=jax.ShapeDtypeStruct((H, S), F32),
        scratch_shapes=[pltpu.VMEM((H, LANES), F32)],
        compiler_params=_params("arbitrary"),
    )(fg_t, b_col)


def _forget_bwd(fg_t, b_col, dc_plus, dc_minus, name, ts=512):
    H, S = fg_t.shape
    ts = _fit(S, ts)
    nb = S // ts

    def body(f_ref, b_ref, dcp_ref, dcm_ref, df_ref, db_ref, carry_ref):
        i = pl.program_id(0)

        @pl.when(i == 0)
        def _():
            carry_ref[...] = jnp.zeros_like(carry_ref)
            db_ref[...] = jnp.zeros_like(db_ref)

        dc = dcp_ref[...] - dcm_ref[...]
        suffix = lax.dot_general(dc, _tri(ts, False), (_NN, ((), ())), precision=lax.Precision.HIGHEST,
                                 preferred_element_type=F32) + carry_ref[:, 0:1]
        z = f_ref[...] + b_ref[...]
        sig_neg = 1.0 / (1.0 + jnp.exp(z))
        df = suffix * sig_neg
        df_ref[...] = df
        carry_ref[...] = jnp.broadcast_to(
            carry_ref[:, 0:1] + jnp.sum(dc, axis=1, keepdims=True), carry_ref.shape)
        db_ref[...] += jnp.broadcast_to(jnp.sum(df, axis=1, keepdims=True), db_ref.shape)

    rev = pl.BlockSpec((H, ts), lambda i: (0, nb - 1 - i))
    return pl.pallas_call(
        body, name=name, grid=(nb,),
        in_specs=[rev, pl.BlockSpec((H, 1), lambda i: (0, 0)), rev, rev],
        out_specs=[rev, pl.BlockSpec((H, LANES), lambda i: (0, 0))],
        out_shape=[jax.ShapeDtypeStruct((H, S), F32), jax.ShapeDtypeStruct((H, LANES), F32)],
        scratch_shapes=[pltpu.VMEM((H, LANES), F32)],
        compiler_params=_params("arbitrary"),
    )(fg_t, b_col, dc_plus, dc_minus)


def _tile_lanes(x, n):
    return x if n == LANES else jnp.tile(x, (1, n // LANES))


ONES_ROWS = 16


def _with_ones(t):
    return jnp.concatenate([t, jnp.ones((t.shape[0], ONES_ROWS, t.shape[2]), t.dtype)], axis=1)


def _fox_fwd(qt, k, vt, c_row, c_rep, name, tq=512, tk=1024):
    H, Dh, S = qt.shape
    tk = _fit(S, tk)
    tq = _fit(tk, tq)
    ratio = tk // tq

    def body(qt_ref, k_ref, vt_ref, c_ref, crep_ref, o_ref, lse_ref, m_ref, acc_ref,
             sa_ref, sb_ref, ta_ref, tb_ref):
        i = pl.program_id(1)
        qv = qt_ref[...] * QK_SCALE
        cq0 = c_ref[:, pl.ds(pl.multiple_of(i * tq, LANES), LANES)][:, 0:1]
        m_ref[...] = jnp.full_like(m_ref, NEG)
        acc_ref[...] = jnp.zeros_like(acc_ref)
        n = i // ratio
        q_off = (i - n * ratio) * tq

        def scores(j, s_ref, t_ref, diagonal):
            off = pl.multiple_of(j * tk, LANES)
            s = _dot(k_ref[pl.ds(off, tk), :], qv, _NN) + _tile_lanes(cq0 - crep_ref[pl.ds(off, tk), :], tq)
            if diagonal:
                key = lax.broadcasted_iota(jnp.int32, (tk, tq), 0)
                qry = lax.broadcasted_iota(jnp.int32, (tk, tq), 1) + q_off
                s = jnp.where(key <= qry, s, NEG)
            s_ref[...] = s
            t_ref[...] = jnp.max(s, axis=0, keepdims=True)

        def absorb(j, s_ref, t_ref):
            off = pl.multiple_of(j * tk, LANES)
            m_old = m_ref[...]
            m_new = jnp.maximum(m_old, t_ref[...])
            p = jnp.exp(s_ref[...] - m_new)
            alpha = jnp.exp(m_old - m_new)
            acc_ref[...] = alpha * acc_ref[...] + _dot(vt_ref[:, pl.ds(off, tk)], p.astype(BF16), _NN)
            m_ref[...] = m_new

        scores(n, sa_ref, ta_ref, True)

        def loop_body(jj, carry):
            scores(2 * jj, sb_ref, tb_ref, False)
            absorb(jnp.where(jj == 0, n, 2 * jj - 1), sa_ref, ta_ref)
            scores(2 * jj + 1, sa_ref, ta_ref, False)
            absorb(2 * jj, sb_ref, tb_ref)
            return carry

        pairs = n // 2
        lax.fori_loop(0, pairs, loop_body, 0)
        held = jnp.where(pairs == 0, n, 2 * pairs - 1)

        @pl.when(n % 2 == 1)
        def _():
            scores(n - 1, sb_ref, tb_ref, False)
            absorb(held, sa_ref, ta_ref)
            absorb(n - 1, sb_ref, tb_ref)

        @pl.when(n % 2 == 0)
        def _():
            absorb(held, sa_ref, ta_ref)

        l = acc_ref[Dh:Dh + 1, :]
        o_ref[...] = acc_ref[0:Dh, :] / l
        lse_ref[...] = m_ref[...] + jnp.log(l) - cq0

    lanes_full = pl.BlockSpec((None, Dh + ONES_ROWS, S), lambda h, i: (h, 0, 0))
    lanes_tile = pl.BlockSpec((None, Dh, tq), lambda h, i: (h, 0, i))
    return pl.pallas_call(
        body, name=name, grid=(H, S // tq),
        in_specs=[lanes_tile, pl.BlockSpec((None, S, Dh), lambda h, i: (h, 0, 0)), lanes_full,
                  pl.BlockSpec((None, 1, S), lambda h, i: (h, 0, 0)),
                  pl.BlockSpec((None, S, LANES), lambda h, i: (h, 0, 0))],
        out_specs=[lanes_tile, pl.BlockSpec((None, 1, tq), lambda h, i: (h, 0, i))],
        out_shape=[jax.ShapeDtypeStruct((H, Dh, S), F32), jax.ShapeDtypeStruct((H, 1, S), F32)],
        scratch_shapes=[pltpu.VMEM((1, tq), F32), pltpu.VMEM((Dh + ONES_ROWS, tq), F32),
                        pltpu.VMEM((tk, tq), F32), pltpu.VMEM((tk, tq), F32),
                        pltpu.VMEM((1, tq), F32), pltpu.VMEM((1, tq), F32)],
        compiler_params=_params("parallel", "arbitrary"),
    )(qt, k, vt, c_row, c_rep)


def _fox_bwd(qt, k, kt, v, dot, c_rep, lse_row, delta_row, name, tq=1024, tk=512):
    H, Da, S = qt.shape
    Dh = Da - ONES_ROWS
    tq = _fit(S, tq)
    tk = _fit(tq, tk)
    ratio = tq // tk
    nq = S // tq
    nk = S // tk

    def body(k_ref, kt_ref, v_ref, crep_ref, qt_ref, dot_ref, lse_ref, dl_ref,
             dqt_ref, dkt_ref, dvt_ref, dka_ref, dva_ref):
        j = pl.program_id(1)

        @pl.when(j == 0)
        def _():
            dqt_ref[...] = jnp.zeros_like(dqt_ref)

        kv = k_ref[...]
        ktv = kt_ref[...]
        vv = v_ref[...]
        c_col = _tile_lanes(crep_ref[...], tq)
        dka_ref[...] = jnp.zeros_like(dka_ref)
        dva_ref[...] = jnp.zeros_like(dva_ref)
        i_diag = j // ratio
        k_off = (j - i_diag * ratio) * tk

        def step(i, diagonal):
            off = pl.multiple_of(i * tq, LANES)
            qv = qt_ref[:, pl.ds(off, tq)] * QK_SCALE
            dov = dot_ref[:, pl.ds(off, tq)]
            e = _dot(kv, qv[0:Dh, :], _NN) - lse_ref[:, pl.ds(off, tq)] - c_col
            if diagonal:
                key = lax.broadcasted_iota(jnp.int32, (tk, tq), 0) + k_off
                qry = lax.broadcasted_iota(jnp.int32, (tk, tq), 1)
                e = jnp.where(key <= qry, e, NEG)
            p_t = jnp.exp(e)
            dva_ref[...] += _dot(dov, p_t.astype(BF16), _NT)
            ds_t = p_t * (_dot(vv, dov, _NN) - dl_ref[:, pl.ds(off, tq)])
            ds_b = ds_t.astype(BF16)
            dka_ref[...] += _dot(qv, ds_b, _NT)
            dqt_ref[:, pl.ds(off, tq)] += _dot(ktv, ds_b, _NN)

        step(i_diag, True)

        def loop_body(i, carry):
            step(i, False)
            return carry

        lax.fori_loop(i_diag + 1, nq, loop_body, 0)
        dkt_ref[0:Dh, :] = dka_ref[0:Dh, :]
        dkt_ref[Dh:, :] = dka_ref[Dh:, :] * (1.0 / QK_SCALE)
        dvt_ref[...] = dva_ref[...]

        @pl.when(j == nk - 1)
        def _():
            dqt_ref[0:Dh, :] = dqt_ref[0:Dh, :] * QK_SCALE

    rows_tile = pl.BlockSpec((None, tk, Dh), lambda h, j: (h, j, 0))
    ones_tile = pl.BlockSpec((None, Da, tk), lambda h, j: (h, 0, j))
    lanes_tile = pl.BlockSpec((None, Dh, tk), lambda h, j: (h, 0, j))
    rep = pl.BlockSpec((None, tk, LANES), lambda h, j: (h, j, 0))
    ones_full = pl.BlockSpec((None, Da, S), lambda h, j: (h, 0, 0))
    lanes_full = pl.BlockSpec((None, Dh, S), lambda h, j: (h, 0, 0))
    rowv = pl.BlockSpec((None, 1, S), lambda h, j: (h, 0, 0))
    return pl.pallas_call(
        body, name=name, grid=(H, nk),
        in_specs=[rows_tile, ones_tile, rows_tile, rep, ones_full, lanes_full, rowv, rowv],
        out_specs=[ones_full, ones_tile, lanes_tile],
        out_shape=[jax.ShapeDtypeStruct((H, Da, S), F32), jax.ShapeDtypeStruct((H, Da, S), F32),
                   jax.ShapeDtypeStruct((H, Dh, S), F32)],
        scratch_shapes=[pltpu.VMEM((Da, tk), F32), pltpu.VMEM((Dh, tk), F32)],
        compiler_params=_params("parallel", "arbitrary"),
    )(k, kt, v, c_rep, qt, dot, lse_row, delta_row)


DIL_Q_BLOCK = 3 * FOX_WIDTH // LANES
HEAD_PAIRS = N_DIL_HEADS // 2
PAIR_BLOCKS = DIL_WIDTH // LANES


def _band_geometry(S, d):
    L = S // d
    chunk = min(BAND_CHUNK_MAX, L)
    assert L % chunk == 0 and chunk % BAND == 0
    return L, chunk, chunk // BAND, L // chunk


def _band_in_specs(S, d, base):
    L, chunk, nb, _ = _band_geometry(S, d)

    def col(kind):
        return lambda hp, r, i: (r, i, base + kind * PAIR_BLOCKS + hp)

    def col_prev(kind):
        return lambda hp, r, i: (r, jnp.maximum(i * nb - 1, 0), base + kind * PAIR_BLOCKS + hp)

    main = [pl.BlockSpec((None, chunk, LANES), col(kind)) for kind in range(3)]
    prev = [pl.BlockSpec((None, BAND, LANES), col_prev(kind)) for kind in range(3)]
    bias = pl.BlockSpec((None, 2 * BAND, 2 * BAND), lambda hp, r, i: (hp, 0, 0))
    stat = pl.BlockSpec((None, 2, chunk), lambda hp, r, i: (hp, 0, r * (L // chunk) + i))
    tok = pl.BlockSpec((None, chunk, LANES), lambda hp, r, i: (r, i, hp))
    return main, prev, bias, stat, tok


def _to_residues(x, col_block, width, d, name, tr=512):
    S = x.shape[0]
    tr = _fit(S, tr)

    def body(x_ref, o_ref, tmp_ref):
        for j in range(width // LANES):
            cols = slice(j * LANES, (j + 1) * LANES)
            tmp_ref[j] = x_ref[:, cols].astype(F32)
            for r in range(d):
                o_ref[r, :, cols] = tmp_ref[j, pl.ds(r, tr // d, stride=d), :].astype(o_ref.dtype)

    return pl.pallas_call(
        body, name=name, grid=(S // tr,),
        in_specs=[pl.BlockSpec((tr, width), lambda i: (i, col_block))],
        out_specs=pl.BlockSpec((d, tr // d, width), lambda i: (0, i, 0)),
        out_shape=jax.ShapeDtypeStruct((d, S // d, width), x.dtype),
        scratch_shapes=[pltpu.VMEM((width // LANES, tr, LANES), F32)],
        compiler_params=_params("parallel"),
    )(x)


def _token_rows(ref, cols, tmp_ref):
    if len(ref.shape) == 2:
        return ref[:, cols].astype(F32)
    d, rows = ref.shape[0], ref.shape[1]
    for r in range(d):
        tmp_ref[pl.ds(r, rows, stride=d), :] = ref[r, :, cols].astype(F32)
    return tmp_ref[...]


def _row_spec(t, tr):
    if t.ndim == 2:
        return pl.BlockSpec((tr, t.shape[1]), lambda i: (i, 0))
    d = t.shape[0]
    return pl.BlockSpec((d, tr // d, t.shape[2]), lambda i: (0, i, 0))


def _head_lanes(a):
    return lax.broadcasted_iota(jnp.int32, (1, LANES), 1) // HEAD_DIM == a


def _one_head(x, a):
    return jnp.where(_head_lanes(a), x, jnp.zeros_like(x))


def _head_stack(x):
    return jnp.concatenate([_one_head(x, 0), _one_head(x, 1)], axis=0)


def _pair_rows(ref, rows):
    return jnp.concatenate([ref[0:1, rows], ref[1:2, rows]], axis=1)


def _band_scores_t(kb, q_stack, bias_t, first):
    s = _dot(kb, q_stack, _NT) + bias_t
    if first is not None:
        key = lax.broadcasted_iota(jnp.int32, s.shape, 0)
        s = jnp.where(jnp.logical_and(first, key < BAND), NEG, s)
    return s


def _pair_select(stacked):
    return jnp.where(_head_lanes(0), stacked[0:BAND, :], stacked[BAND:, :])


def _dil_lse(qkv_v, base, bias_t, name):
    d, L = qkv_v.shape[:2]
    S = L * d
    _, chunk, nb, nchunks = _band_geometry(S, d)
    main, prev, bias, stat, _ = _band_in_specs(S, d, base)

    def body(q_ref, k_ref, kp_ref, b_ref, lse_ref, kext_ref):
        first = pl.program_id(2) == 0
        kext_ref[0:BAND, :] = kp_ref[...]
        kext_ref[BAND:, :] = k_ref[...]
        for b in range(nb):
            rows, ext = slice(b * BAND, (b + 1) * BAND), slice(b * BAND, (b + 2) * BAND)
            s = _band_scores_t(kext_ref[ext, :], _head_stack(q_ref[rows, :] * QK_SCALE), b_ref[...],
                               first if b == 0 else None)
            m = jnp.max(s, axis=0, keepdims=True)
            lse = m + jnp.log(jnp.sum(jnp.exp(s - m), axis=0, keepdims=True))
            lse_ref[0:1, rows] = lse[:, 0:BAND]
            lse_ref[1:2, rows] = lse[:, BAND:]

    return pl.pallas_call(
        body, name=name, grid=(HEAD_PAIRS, d, nchunks),
        in_specs=[main[0], main[1], prev[1], bias], out_specs=stat,
        out_shape=jax.ShapeDtypeStruct((HEAD_PAIRS, 2, S), F32),
        scratch_shapes=[pltpu.VMEM((chunk + BAND, LANES), BF16)],
        compiler_params=_params("parallel", "parallel", "parallel"),
    )(qkv_v, qkv_v, qkv_v, bias_t)


def _dil_out(qkv_v, base, bias_t, lse_joint, name):
    d, L = qkv_v.shape[:2]
    S = L * d
    _, chunk, nb, nchunks = _band_geometry(S, d)
    main, prev, bias, stat, tok = _band_in_specs(S, d, base)

    def body(q_ref, k_ref, kp_ref, v_ref, vp_ref, b_ref, lse_ref, o_ref, kext_ref, vext_ref):
        first = pl.program_id(2) == 0
        kext_ref[0:BAND, :] = kp_ref[...]
        kext_ref[BAND:, :] = k_ref[...]
        vext_ref[0:BAND, :] = vp_ref[...]
        vext_ref[BAND:, :] = v_ref[...]
        for b in range(nb):
            rows, ext = slice(b * BAND, (b + 1) * BAND), slice(b * BAND, (b + 2) * BAND)
            s = _band_scores_t(kext_ref[ext, :], _head_stack(q_ref[rows, :] * QK_SCALE), b_ref[...],
                               first if b == 0 else None)
            p_t = jnp.exp(s - _pair_rows(lse_ref, rows))
            o_ref[rows, :] = _pair_select(_dot(p_t.astype(BF16), vext_ref[ext, :], _TN)).astype(BF16)

    return pl.pallas_call(
        body, name=name, grid=(HEAD_PAIRS, d, nchunks),
        in_specs=[main[0], main[1], prev[1], main[2], prev[2], bias, stat], out_specs=tok,
        out_shape=jax.ShapeDtypeStruct((d, L, DIL_WIDTH), BF16),
        scratch_shapes=[pltpu.VMEM((chunk + BAND, LANES), BF16), pltpu.VMEM((chunk + BAND, LANES), BF16)],
        compiler_params=_params("parallel", "parallel", "parallel"),
    )(qkv_v, qkv_v, qkv_v, qkv_v, qkv_v, bias_t, lse_joint)


def _dil_bwd(qkv_v, base, do_v, bias_t, lse_joint, delta, name):
    d, L = qkv_v.shape[:2]
    S = L * d
    _, chunk, nb, nchunks = _band_geometry(S, d)
    main, prev, bias, stat, tok = _band_in_specs(S, d, base)
    nblocks = L // BAND

    def nxt_row(i):
        return jnp.minimum((i + 1) * nb, nblocks - 1)

    q_next = pl.BlockSpec((None, BAND, LANES), lambda hp, r, i: (r, nxt_row(i), base + hp))
    do_next = pl.BlockSpec((None, BAND, LANES), lambda hp, r, i: (r, nxt_row(i), hp))
    stat_next = pl.BlockSpec((None, 2, BAND), lambda hp, r, i: (hp, 0, r * nblocks + nxt_row(i)))

    def body(q_ref, k_ref, kp_ref, v_ref, vp_ref, do_ref, b_ref, lse_ref, dl_ref,
             qn_ref, don_ref, lsen_ref, dln_ref,
             dq_ref, dk_ref, dv_ref, db_ref, kext_ref, vext_ref, dkext_ref, dvext_ref):
        r, i = pl.program_id(1), pl.program_id(2)
        first = i == 0
        has_next = i + 1 < nchunks
        tail = slice(BAND + chunk, 2 * BAND + chunk)
        kext_ref[0:BAND, :] = kp_ref[...]
        kext_ref[BAND:BAND + chunk, :] = k_ref[...]
        kext_ref[tail, :] = jnp.zeros((BAND, LANES), BF16)
        vext_ref[0:BAND, :] = vp_ref[...]
        vext_ref[BAND:BAND + chunk, :] = v_ref[...]
        vext_ref[tail, :] = jnp.zeros((BAND, LANES), BF16)
        dkext_ref[...] = jnp.zeros_like(dkext_ref)
        dvext_ref[...] = jnp.zeros_like(dvext_ref)

        @pl.when(jnp.logical_and(r == 0, i == 0))
        def _():
            db_ref[...] = jnp.zeros_like(db_ref)

        def block(q2, do2, lse_row, dl_row, ext, mask_rows):
            q_stack, do_stack = _head_stack(q2), _head_stack(do2)
            s = _dot(kext_ref[ext, :], q_stack, _NT) + b_ref[...]
            if mask_rows is not None:
                s = jnp.where(mask_rows, NEG, s)
            p_t = jnp.exp(s - lse_row)
            ds_t = p_t * (_dot(vext_ref[ext, :], do_stack, _NT) - dl_row)
            ds_b = ds_t.astype(BF16)
            dkext_ref[ext, :] += _dot(ds_b, q_stack, _NN)
            dvext_ref[ext, :] += _dot(p_t.astype(BF16), do_stack, _NN)
            return ds_t, ds_b

        key = lax.broadcasted_iota(jnp.int32, (2 * BAND, 2 * BAND), 0)
        all_lanes = slice(0, BAND)
        for b in range(nb):
            rows, ext = slice(b * BAND, (b + 1) * BAND), slice(b * BAND, (b + 2) * BAND)
            mask = jnp.logical_and(first, key < BAND) if b == 0 else None
            ds_t, ds_b = block(q_ref[rows, :] * QK_SCALE, do_ref[rows, :], _pair_rows(lse_ref, rows),
                               _pair_rows(dl_ref, rows), ext, mask)
            dq_ref[rows, :] = _pair_select(_dot(ds_b, kext_ref[ext, :], _TN)) * QK_SCALE
            db_ref[...] += ds_t
        block(qn_ref[...] * QK_SCALE, don_ref[...], _pair_rows(lsen_ref, all_lanes), _pair_rows(dln_ref, all_lanes),
              slice(chunk, chunk + 2 * BAND), jnp.logical_or(jnp.logical_not(has_next), key >= BAND))
        dk_ref[...] = dkext_ref[BAND:BAND + chunk, :]
        dv_ref[...] = dvext_ref[BAND:BAND + chunk, :]

    ext_rows = chunk + 2 * BAND
    return pl.pallas_call(
        body, name=name, grid=(HEAD_PAIRS, d, nchunks),
        in_specs=[main[0], main[1], prev[1], main[2], prev[2], tok, bias, stat, stat,
                  q_next, do_next, stat_next, stat_next],
        out_specs=[tok, tok, tok, bias],
        out_shape=[jax.ShapeDtypeStruct((d, L, DIL_WIDTH), F32)] * 3
                  + [jax.ShapeDtypeStruct((HEAD_PAIRS, 2 * BAND, 2 * BAND), F32)],
        scratch_shapes=[pltpu.VMEM((ext_rows, LANES), BF16), pltpu.VMEM((ext_rows, LANES), BF16),
                        pltpu.VMEM((ext_rows, LANES), F32), pltpu.VMEM((ext_rows, LANES), F32)],
        compiler_params=_params("arbitrary", "arbitrary", "arbitrary"),
    )(qkv_v, qkv_v, qkv_v, qkv_v, qkv_v, do_v, bias_t, lse_joint, delta, qkv_v, do_v, lse_joint, delta)


def _lse_join(lse3, name):
    P, H, S = lse3.shape

    def body(l_ref, o_ref):
        a, b, c = l_ref[0], l_ref[1], l_ref[2]
        m = jnp.maximum(jnp.maximum(a, b), c)
        o_ref[...] = m + jnp.log(jnp.exp(a - m) + jnp.exp(b - m) + jnp.exp(c - m))

    return pl.pallas_call(body, name=name, out_shape=jax.ShapeDtypeStruct((H, S), F32))(lse3)


def _bucket_reduce(dbias_t, bucket_map_t, name):
    P, H = dbias_t.shape[:2]

    def body(db_ref, bk_ref, o_ref):
        p, h = pl.program_id(0), pl.program_id(1)

        @pl.when(jnp.logical_and(p == 0, h == 0))
        def _():
            o_ref[...] = jnp.zeros_like(o_ref)

        db, bk = db_ref[...], bk_ref[...]
        row = lax.broadcasted_iota(jnp.int32, (N_BUCKETS, LANES), 0)
        lane = lax.broadcasted_iota(jnp.int32, (N_BUCKETS, LANES), 1)

        def one(b, acc):
            val = jnp.sum(jnp.sum(jnp.where(bk == b, db, 0.0), axis=1, keepdims=True), axis=0, keepdims=True)
            return acc + jnp.where(jnp.logical_and(row == b, lane == h), val, 0.0)

        o_ref[...] += lax.fori_loop(0, N_BUCKETS, one, jnp.zeros((N_BUCKETS, LANES), F32))

    return pl.pallas_call(
        body, name=name, grid=(P, H),
        in_specs=[pl.BlockSpec((None, None, 2 * BAND, BAND), lambda p, h: (p, h, 0, 0)),
                  pl.BlockSpec((None, 2 * BAND, BAND), lambda p, h: (p, 0, 0))],
        out_specs=pl.BlockSpec((N_BUCKETS, LANES), lambda p, h: (0, 0)),
        out_shape=jax.ShapeDtypeStruct((N_BUCKETS, LANES), F32),
        compiler_params=_params("arbitrary", "arbitrary"),
    )(dbias_t, bucket_map_t)


def _mem_fwd(q, kv, name, tq=1024):
    S, W = q.shape
    N = kv.shape[0]
    pairs = W // LANES
    tq = _fit(S, tq)

    def body(q_ref, k_ref, v_ref, o_ref, lse_ref):
        for a in range(2):
            lanes = slice(a * HEAD_DIM, (a + 1) * HEAD_DIM)
            s = _dot(k_ref[:, lanes], q_ref[:, lanes] * QK_SCALE, _NT)
            m = jnp.max(s, axis=0, keepdims=True)
            e = jnp.exp(s - m)
            l = jnp.sum(e, axis=0, keepdims=True)
            o_ref[:, lanes] = _dot((e / l).astype(BF16), v_ref[:, lanes], _TN).astype(BF16)
            lse_ref[a:a + 1, :] = m + jnp.log(l)

    return pl.pallas_call(
        body, name=name, grid=(pairs, S // tq),
        in_specs=[pl.BlockSpec((tq, LANES), lambda hp, i: (i, hp)),
                  pl.BlockSpec((N, LANES), lambda hp, i: (0, hp)),
                  pl.BlockSpec((N, LANES), lambda hp, i: (0, pairs + hp))],
        out_specs=[pl.BlockSpec((tq, LANES), lambda hp, i: (i, hp)),
                   pl.BlockSpec((None, 2, tq), lambda hp, i: (hp, 0, i))],
        out_shape=[jax.ShapeDtypeStruct((S, W), BF16), jax.ShapeDtypeStruct((pairs, 2, S), F32)],
        compiler_params=_params("parallel", "parallel"),
    )(q, kv, kv)


def _mem_bwd(q, kv, do, lse, delta, name, tq=1024):
    S, W = q.shape
    N = kv.shape[0]
    pairs = W // LANES
    tq = _fit(S, tq)

    def body(q_ref, k_ref, v_ref, do_ref, lse_ref, dl_ref, dq_ref, dk_ref, dv_ref):
        i = pl.program_id(1)

        @pl.when(i == 0)
        def _():
            dk_ref[...] = jnp.zeros_like(dk_ref)
            dv_ref[...] = jnp.zeros_like(dv_ref)

        for a in range(2):
            lanes = slice(a * HEAD_DIM, (a + 1) * HEAD_DIM)
            qv, dov = q_ref[:, lanes] * QK_SCALE, do_ref[:, lanes]
            kv_, vv = k_ref[:, lanes], v_ref[:, lanes]
            p_t = jnp.exp(_dot(kv_, qv, _NT) - lse_ref[a:a + 1, :])
            ds_t = p_t * (_dot(vv, dov, _NT) - dl_ref[a:a + 1, :])
            ds_b = ds_t.astype(BF16)
            dq_ref[:, lanes] = (_dot(ds_b, kv_, _TN) * QK_SCALE).astype(BF16)
            dk_ref[:, lanes] += _dot(ds_b, qv, _NN)
            dv_ref[:, lanes] += _dot(p_t.astype(BF16), dov, _NN)

    qs = pl.BlockSpec((tq, LANES), lambda hp, i: (i, hp))
    stat = pl.BlockSpec((None, 2, tq), lambda hp, i: (hp, 0, i))
    acc = pl.BlockSpec((N, LANES), lambda hp, i: (0, hp))
    return pl.pallas_call(
        body, name=name, grid=(pairs, S // tq),
        in_specs=[qs, acc, pl.BlockSpec((N, LANES), lambda hp, i: (0, pairs + hp)), qs, stat, stat],
        out_specs=[qs, acc, acc],
        out_shape=[jax.ShapeDtypeStruct((S, W), BF16), jax.ShapeDtypeStruct((N, W), F32),
                   jax.ShapeDtypeStruct((N, W), F32)],
        compiler_params=_params("parallel", "arbitrary"),
    )(q, kv, kv, do, lse, delta)


def _head_rowdot(a, bs, name, tr=512):
    S, W = a.shape
    tr = _fit(S, tr)

    def body(*refs):
        a_ref, b_refs, o_ref, tmp_ref = refs[0], refs[1:-2], refs[-2], refs[-1]
        col = lax.broadcasted_iota(jnp.int32, (LANES, LANES), 0)
        lane = lax.broadcasted_iota(jnp.int32, (LANES, LANES), 1)
        acc = jnp.zeros((tr, LANES), F32)
        for j in range(W // LANES):
            cols = slice(j * LANES, (j + 1) * LANES)
            tot = _token_rows(b_refs[0], cols, tmp_ref)
            for r in b_refs[1:]:
                tot = tot + _token_rows(r, cols, tmp_ref)
            sel = jnp.where(col // HEAD_DIM + j * (LANES // HEAD_DIM) == lane, 1.0, 0.0).astype(F32)
            acc = acc + lax.dot_general(a_ref[:, cols].astype(F32) * tot, sel, (_NN, ((), ())),
                                        precision=lax.Precision.HIGHEST, preferred_element_type=F32)
        o_ref[...] = acc

    return pl.pallas_call(
        body, name=name, grid=(S // tr,), in_specs=[_row_spec(t, tr) for t in [a] + list(bs)],
        out_specs=pl.BlockSpec((tr, LANES), lambda i: (i, 0)),
        out_shape=jax.ShapeDtypeStruct((S, LANES), F32),
        scratch_shapes=[pltpu.VMEM((tr, LANES), F32)],
        compiler_params=_params("parallel"),
    )(a, *bs)


def _sum_cast_cols(groups, out_dtype, name, tail=None, tr=256):
    first = groups[0][0]
    S, W = (first.shape if first.ndim == 2 else (first.shape[0] * first.shape[1], first.shape[2]))
    tr = _fit(S, tr)
    flat = [t for g in groups for t in g] + ([tail] if tail is not None else [])
    tail_w = 0 if tail is None else tail.shape[1]

    def body(*refs):
        o_ref, tmp_ref = refs[-2], refs[-1]
        if tail is not None:
            o_ref[:, W * len(groups):] = refs[-3][...].astype(out_dtype)
        k = 0
        for gi, g in enumerate(groups):
            for j in range(W // LANES):
                cols = slice(j * LANES, (j + 1) * LANES)
                acc = _token_rows(refs[k], cols, tmp_ref)
                for r in refs[k + 1:k + len(g)]:
                    acc = acc + _token_rows(r, cols, tmp_ref)
                o_ref[:, gi * W + j * LANES:gi * W + (j + 1) * LANES] = acc.astype(out_dtype)
            k += len(g)

    return pl.pallas_call(
        body, name=name, grid=(S // tr,), in_specs=[_row_spec(t, tr) for t in flat],
        out_specs=pl.BlockSpec((tr, W * len(groups) + tail_w), lambda i: (i, 0)),
        out_shape=jax.ShapeDtypeStruct((S, W * len(groups) + tail_w), out_dtype),
        scratch_shapes=[pltpu.VMEM((tr, LANES), F32)],
        compiler_params=_params("parallel"),
    )(*flat)


FF_TILE = 256


def _ffn_up(h, w_gu, name, tm=1024):
    S, D = h.shape
    F2 = w_gu.shape[1]
    tm = _fit(S, tm)

    def body(h_ref, w_ref, gu_ref, act_ref):
        gu = _dot(h_ref[...], w_ref[...], _NN)
        gu_ref[...] = gu.astype(BF16)
        g, u = gu[:, :FF_TILE], gu[:, FF_TILE:]
        act_ref[...] = (g * (1.0 / (1.0 + jnp.exp(-g))) * u).astype(BF16)

    return pl.pallas_call(
        body, name=name, grid=(S // tm, F2 // (2 * FF_TILE)),
        in_specs=[pl.BlockSpec((tm, D), lambda i, j: (i, 0)), pl.BlockSpec((D, 2 * FF_TILE), lambda i, j: (0, j))],
        out_specs=[pl.BlockSpec((tm, 2 * FF_TILE), lambda i, j: (i, j)),
                   pl.BlockSpec((tm, FF_TILE), lambda i, j: (i, j))],
        out_shape=[jax.ShapeDtypeStruct((S, F2), BF16), jax.ShapeDtypeStruct((S, F2 // 2), BF16)],
        compiler_params=_params("parallel", "arbitrary"),
    )(h, w_gu)


def _ffn_dact(dy, w_down, gu, name, tm=1024):
    S, D = dy.shape
    F2 = gu.shape[1]
    tm = _fit(S, tm)

    def body(dy_ref, w_ref, gu_ref, dgu_ref):
        dact = _dot(dy_ref[...], w_ref[...], _NT)
        gu_v = gu_ref[...].astype(F32)
        g, u = gu_v[:, :FF_TILE], gu_v[:, FF_TILE:]
        sig = 1.0 / (1.0 + jnp.exp(-g))
        silu = g * sig
        dgu_ref[:, :FF_TILE] = (dact * u * (sig + silu * (1.0 - sig))).astype(BF16)
        dgu_ref[:, FF_TILE:] = (dact * silu).astype(BF16)

    return pl.pallas_call(
        body, name=name, grid=(S // tm, F2 // (2 * FF_TILE)),
        in_specs=[pl.BlockSpec((tm, D), lambda i, j: (i, 0)), pl.BlockSpec((FF_TILE, D), lambda i, j: (j, 0)),
                  pl.BlockSpec((tm, 2 * FF_TILE), lambda i, j: (i, j))],
        out_specs=pl.BlockSpec((tm, 2 * FF_TILE), lambda i, j: (i, j)),
        out_shape=jax.ShapeDtypeStruct((S, F2), BF16),
        compiler_params=_params("parallel", "arbitrary"),
    )(dy, w_down, gu)


def _fit_rows(n, cap):
    if n <= cap:
        return n
    t = (cap // 8) * 8
    while t >= 8:
        if n % t == 0:
            return t
        t -= 8
    raise ValueError(f"no sublane-aligned tile for {n} under {cap}")


def _add_n(arrs, name, tr=512):
    R, C = arrs[0].shape
    tr = _fit_rows(R, tr)

    def body(*refs):
        acc = refs[0][...]
        for r in refs[1:-1]:
            acc = acc + r[...]
        refs[-1][...] = acc

    row = pl.BlockSpec((tr, C), lambda i: (i, 0))
    return pl.pallas_call(
        body, name=name, grid=(R // tr,), in_specs=[row] * len(arrs), out_specs=row,
        out_shape=jax.ShapeDtypeStruct((R, C), F32), compiler_params=_params("parallel"),
    )(*arrs)


def _adamw(w, g, m, v, name, tr=512):
    R, C = w.shape
    tr = _fit_rows(R, tr)
    c1 = 1.0 / (1.0 - ADAM_B1 ** ADAM_STEP)
    c2 = 1.0 / (1.0 - ADAM_B2 ** ADAM_STEP)

    def body(w_ref, g_ref, m_ref, v_ref, d_ref, nm_ref, nv_ref):
        gv = g_ref[...]
        nm = ADAM_B1 * m_ref[...] + (1.0 - ADAM_B1) * gv
        nv = ADAM_B2 * v_ref[...] + (1.0 - ADAM_B2) * (gv * gv)
        nm_ref[...] = nm
        nv_ref[...] = nv
        d_ref[...] = -ADAM_LR * ((nm * c1) / (jnp.sqrt(nv * c2) + ADAM_EPS) + ADAM_WD * w_ref[...])

    row = pl.BlockSpec((tr, C), lambda i: (i, 0))
    return pl.pallas_call(
        body, name=name, grid=(R // tr,), in_specs=[row] * 4, out_specs=[row] * 3,
        out_shape=[jax.ShapeDtypeStruct((R, C), F32)] * 3, compiler_params=_params("parallel"),
    )(w, g, m, v)


def _place():
    return lax.axis_index("x"), lax.axis_index("y"), lax.axis_index("c")


_ANY = pl.BlockSpec(memory_space=pl.ANY)


def _chip_all_gather(shard, name):
    R, C = shard.shape
    half = R // 2

    def body(x_ref, out_ref, send_sems, recv_sems, local_sem):
        x, y, c = _place()
        chips = [(1 - x, y), (x, 1 - y), (1 - x, 1 - y)]
        sibling = (x, y, 1 - c)
        mine = pltpu.make_async_copy(x_ref, out_ref.at[2 * x + y], local_sem)
        mine.start()

        def rows(chip, core):
            return out_ref.at[chip, pl.ds(core * half, half)]

        def copy(k, chip, core, to, src=None):
            return pltpu.make_async_remote_copy(
                src_ref=rows(chip, core) if src is None else src, dst_ref=rows(chip, core),
                send_sem=send_sems.at[k], recv_sem=recv_sems.at[k], device_id=to, device_id_type=MESH_IDS)

        me = 2 * x + y
        first = [copy(k, me, c, (cx, cy, c), src=x_ref.at[pl.ds(c * half, half)]) for k, (cx, cy) in enumerate(chips)]
        for cp in first:
            cp.start()
        passed = [copy(3 + k, 2 * cx + cy, c, sibling) for k, (cx, cy) in enumerate(chips)]
        for k, (cx, cy) in enumerate(chips):
            copy(k, 2 * cx + cy, c, (cx, cy, c)).wait_recv()
            passed[k].start()
        for k, (cx, cy) in enumerate(chips):
            copy(3 + k, 2 * cx + cy, 1 - c, sibling).wait_recv()
        for cp in first + passed:
            cp.wait_send()
        mine.wait()

    return pl.pallas_call(
        body, name=name, in_specs=[_ANY], out_specs=_ANY,
        out_shape=jax.ShapeDtypeStruct((N_CHIPS, R, C), shard.dtype),
        scratch_shapes=[pltpu.SemaphoreType.DMA((6,)), pltpu.SemaphoreType.DMA((6,)), pltpu.SemaphoreType.DMA],
    )(shard)


def _sibling_exchange(buf, name):
    def body(x_ref, out_ref, send_sem, recv_sem):
        x, y, c = _place()
        cp = pltpu.make_async_remote_copy(
            src_ref=x_ref, dst_ref=out_ref, send_sem=send_sem, recv_sem=recv_sem,
            device_id=(x, y, 1 - c), device_id_type=MESH_IDS)
        cp.start()
        cp.wait()

    return pl.pallas_call(
        body, name=name, in_specs=[_ANY], out_specs=_ANY,
        out_shape=jax.ShapeDtypeStruct(buf.shape, buf.dtype),
        scratch_shapes=[pltpu.SemaphoreType.DMA, pltpu.SemaphoreType.DMA],
    )(buf)


def _chip_scatter(parts, name):
    _, R, C = parts.shape

    def body(p_ref, out_ref, send_sems, recv_sems):
        x, y, c = _place()
        chips = [(1 - x, y), (x, 1 - y), (1 - x, 1 - y)]

        def copy(k, slab, to):
            return pltpu.make_async_remote_copy(
                src_ref=p_ref.at[slab], dst_ref=out_ref.at[k], send_sem=send_sems.at[k], recv_sem=recv_sems.at[k],
                device_id=to, device_id_type=MESH_IDS)

        sends = [copy(k, 2 * cx + cy, (cx, cy, c)) for k, (cx, cy) in enumerate(chips)]
        for cp in sends:
            cp.start()
        for cp in sends:
            cp.wait_recv()
        for cp in sends:
            cp.wait_send()

    return pl.pallas_call(
        body, name=name, in_specs=[_ANY], out_specs=_ANY,
        out_shape=jax.ShapeDtypeStruct((3, R, C), parts.dtype),
        scratch_shapes=[pltpu.SemaphoreType.DMA((3,)), pltpu.SemaphoreType.DMA((3,))],
    )(parts)


def _all_to_all_small(vec, name):
    R, C = vec.shape

    def body(v_ref, out_ref, send_sems, recv_sems, local_sem):
        x, y, c = _place()
        me = 4 * x + 2 * y + c
        mine = pltpu.make_async_copy(v_ref, out_ref.at[me], local_sem)
        mine.start()
        flips = [(dx, dy, dc) for dx in (0, 1) for dy in (0, 1) for dc in (0, 1)][1:]

        def peer(f):
            return (x ^ f[0], y ^ f[1], c ^ f[2])

        def copy(k, slot, to):
            return pltpu.make_async_remote_copy(
                src_ref=v_ref, dst_ref=out_ref.at[slot], send_sem=send_sems.at[k], recv_sem=recv_sems.at[k],
                device_id=to, device_id_type=MESH_IDS)

        sends = [copy(k, me, peer(f)) for k, f in enumerate(flips)]
        for cp in sends:
            cp.start()
        for k, f in enumerate(flips):
            px, py, pc = peer(f)
            copy(k, 4 * px + 2 * py + pc, peer(f)).wait_recv()
        for cp in sends:
            cp.wait_send()
        mine.wait()

    return pl.pallas_call(
        body, name=name, in_specs=[_ANY], out_specs=_ANY,
        out_shape=jax.ShapeDtypeStruct((8, R, C), vec.dtype),
        scratch_shapes=[pltpu.SemaphoreType.DMA((7,)), pltpu.SemaphoreType.DMA((7,)), pltpu.SemaphoreType.DMA],
    )(vec)


def _to_heads(t, n):
    S = t.shape[0]
    return t.reshape(S, n, HEAD_DIM).transpose(1, 0, 2)


def _to_heads_t(t, n):
    S = t.shape[0]
    return t.T.reshape(n, HEAD_DIM, S)


def _from_heads_t(t):
    H, Dh, S = t.shape
    return t.reshape(H * Dh, S).T


def _rep(t):
    return jnp.broadcast_to(t[..., None], t.shape + (LANES,))


def _t5_bucket(dist):
    max_exact = N_BUCKETS // 2
    d = np.maximum(dist, 1).astype(np.float32)
    large = max_exact + (np.log(d / max_exact) / np.log(MAX_DISTANCE / max_exact)
                         * (N_BUCKETS - max_exact)).astype(np.int32)
    large = np.minimum(large, N_BUCKETS - 1)
    return np.where(dist < max_exact, dist, large).astype(np.int32)


def _band_tables():
    qi = np.arange(BAND)[:, None]
    kj = np.arange(2 * BAND)[None, :]
    sub = qi + BAND - kj
    band = (sub >= 0) & (sub <= BAND)
    out = []
    for d in DILATIONS:
        bucket = _t5_bucket(np.clip(sub, 0, BAND) * d)
        out.append(np.where(band, bucket, -1).astype(np.int32))
    return np.stack(out)


_PACK = (("w_in", 770), ("w_out", 256), ("w_xq", 64), ("w_xk", 64), ("w_xv", 64), ("w_xo", 64),
         ("w_gate", 704), ("w_up", 704), ("w_down", 704))


def _pack(shards):
    rows = [shards[n].reshape(-1, PACK_COLS) for n, _ in _PACK]
    total = sum(r.shape[0] for r in rows)
    pad = (-total) % 128
    if pad:
        rows.append(jnp.zeros((pad, PACK_COLS), rows[0].dtype))
    return jnp.concatenate(rows, axis=0)


def _unpack(pack, shapes):
    out, r = {}, 0
    for n, _ in _PACK:
        cnt = int(np.prod(shapes[n])) // PACK_COLS
        out[n] = pack[r:r + cnt].reshape(shapes[n])
        r += cnt
    return out


_COL_SHARDED = ("w_in", "w_xo", "w_gate", "w_up")


def _full_weight(gathered, name):
    return jnp.concatenate(gathered, axis=1 if name in _COL_SHARDED else 0)


def _split_weight(full, name):
    return jnp.split(full, N_CHIPS, axis=1 if name in _COL_SHARDED else 0)


_SMALL = ("g_mix_pre", "g_mix_post", "g_xattn_pre", "g_mem", "g_xattn_post", "g_ffn_pre", "g_ffn_post")


def _pack_small(vals):
    D = vals["g_mix_pre"].shape[1]
    rows = [vals[n].reshape(1, D) for n in _SMALL]
    misc = jnp.concatenate([vals["b_f"].reshape(-1), vals["rel_bias"].reshape(-1)])
    rows.append(jnp.pad(misc, (0, D - misc.shape[0])).reshape(1, D))
    rows.append(jnp.zeros((16 - len(rows), D), F32))
    return jnp.concatenate(rows, axis=0)


def _unpack_small(pack):
    out = {n: pack[i:i + 1] for i, n in enumerate(_SMALL)}
    out["b_f"] = pack[7, 0:N_FOX_HEADS].reshape(1, N_FOX_HEADS)
    out["rel_bias"] = pack[7, N_FOX_HEADS:N_FOX_HEADS + N_BUCKETS * N_DIL_HEADS].reshape(N_BUCKETS, N_DIL_HEADS)
    return out


def kernel(x, mem, g_mix_pre, w_in, b_f, rel_bias, w_out, g_mix_post, g_xattn_pre, g_mem, w_xq, w_xk, w_xv, w_xo, g_xattn_post, g_ffn_pre, w_gate, w_up, w_down, g_ffn_post, loss_target, m_g_mix_pre, m_w_in, m_b_f, m_rel_bias, m_w_out, m_g_mix_post, m_g_xattn_pre, m_g_mem, m_w_xq, m_w_xk, m_w_xv, m_w_xo, m_g_xattn_post, m_g_ffn_pre, m_w_gate, m_w_up, m_w_down, m_g_ffn_post, v_g_mix_pre, v_w_in, v_b_f, v_rel_bias, v_w_out, v_g_mix_post, v_g_xattn_pre, v_g_mem, v_w_xq, v_w_xk, v_w_xv, v_w_xo, v_g_xattn_post, v_g_ffn_pre, v_w_gate, v_w_up, v_w_down, v_g_ffn_post):
    args = dict(locals())
    big = [n for n, _ in _PACK]
    names = ["g_mix_pre", "w_in", "b_f", "rel_bias", "w_out", "g_mix_post", "g_xattn_pre", "g_mem", "w_xq",
             "w_xk", "w_xv", "w_xo", "g_xattn_post", "g_ffn_pre", "w_gate", "w_up", "w_down", "g_ffn_post"]
    xs = x[0]
    S, D = xs.shape
    assert S % (BAND * DILATIONS[-1]) == 0
    shard_shapes = {n: args[n].shape[1:] for n in big}
    my_x, my_y, my_c = lax.axis_index("x"), lax.axis_index("y"), lax.axis_index("c")

    w_pack = _pack({n: args[n][0] for n in big})
    gathered = _chip_all_gather(w_pack.astype(BF16), "weights_all_gather")
    per_chip = [_unpack(gathered[j], shard_shapes) for j in range(N_CHIPS)]
    W = {n: _full_weight([pc[n] for pc in per_chip], n) for n in big}
    w_fox, w_fg, w_dil = (W["w_in"][:, :3 * FOX_WIDTH], W["w_in"][:, 3 * FOX_WIDTH:3 * FOX_WIDTH + N_FOX_HEADS],
                          W["w_in"][:, 3 * FOX_WIDTH + N_FOX_HEADS:])
    w_qkv = jnp.concatenate([w_fox, w_dil], axis=1)
    w_fg_pad = jnp.pad(w_fg, ((0, 0), (0, LANES - N_FOX_HEADS)))
    F = W["w_gate"].shape[1]
    nft = F // FF_TILE
    w_gu = jnp.stack([W["w_gate"].reshape(D, nft, FF_TILE), W["w_up"].reshape(D, nft, FF_TILE)],
                     axis=2).reshape(D, 2 * F)

    h1 = _rms_fwd(xs, g_mix_pre, "rms_mix_pre")
    qkv = _mm(h1, w_qkv, "nn", BF16, "proj_qkv")
    fg = _mm(h1, w_fg_pad, "nn", F32, "proj_gate")
    fg_t = fg[:, :N_FOX_HEADS].T
    b_col = b_f.reshape(N_FOX_HEADS, 1)
    c_t = _forget_fwd(fg_t, b_col, "forget_cumsum")
    c_row = c_t.reshape(N_FOX_HEADS, 1, S)
    c_rep = _rep(c_t)
    fq_s, fk_s, fv_s = (qkv[:, i * FOX_WIDTH:(i + 1) * FOX_WIDTH] for i in range(3))
    fqt, fkt, fvt = (_to_heads_t(t, N_FOX_HEADS) for t in (fq_s, fk_s, fv_s))
    fk, fv = _to_heads(fk_s, N_FOX_HEADS), _to_heads(fv_s, N_FOX_HEADS)
    o_fox_t, lse_fox = _fox_fwd(fqt, fk, _with_ones(fvt), c_row, c_rep, "fox_fwd")

    bucket_map = _band_tables()
    onehot = (jnp.asarray(bucket_map)[..., None] == jnp.arange(N_BUCKETS)).astype(F32)
    bias_tab = jnp.einsum("pqkb,bh->phkq", onehot, rel_bias, precision=lax.Precision.HIGHEST)
    bias_tab = jnp.where(jnp.asarray(bucket_map.transpose(0, 2, 1) >= 0)[:, None], bias_tab, NEG)
    bias_t = bias_tab.reshape(3, HEAD_PAIRS, 2, 2 * BAND, BAND).transpose(0, 1, 3, 2, 4).reshape(
        3, HEAD_PAIRS, 2 * BAND, 2 * BAND)
    views = [(qkv.reshape(1, S, qkv.shape[1]), DIL_Q_BLOCK)] + [
        (_to_residues(qkv, 1, 3 * DIL_WIDTH, d, f"dilated_qkv_residues_{d}"), 0) for d in DILATIONS[1:]]

    def to_tok(stat, d):
        return stat.reshape(N_DIL_HEADS, d, S // d).swapaxes(1, 2).reshape(N_DIL_HEADS, S)

    def to_perm(stat, d):
        return stat.reshape(N_DIL_HEADS, S // d, d).swapaxes(1, 2).reshape(HEAD_PAIRS, 2, S)

    def tok_or_res(t):
        return t.reshape(t.shape[1:]) if t.shape[0] == 1 else t

    lse_tok = jnp.stack([to_tok(_dil_lse(*views[p], bias_t[p], f"dilated_lse_{d}"), d)
                         for p, d in enumerate(DILATIONS)])
    lse_joint = _lse_join(lse_tok, "dilated_lse_join")
    lse_perm = [to_perm(lse_joint, d) for d in DILATIONS]
    o_dil = [tok_or_res(_dil_out(*views[p], bias_t[p], lse_perm[p], f"dilated_out_{d}"))
             for p, d in enumerate(DILATIONS)]
    o_cat = _sum_cast_cols([[_from_heads_t(o_fox_t)]] + [[o] for o in o_dil], BF16, "mixer_out_cat")
    w_out_b = W["w_out"]
    w_out_cat = jnp.concatenate([w_out_b[:FOX_WIDTH]] + [w_out_b[FOX_WIDTH:]] * 3, axis=0)
    a = _mm(o_cat, w_out_cat, "nn", F32, "proj_out")
    x1, h2 = _resid_norm(xs, a, g_mix_post, g_xattn_pre, "resid_mix")

    hm = _rms_fwd(mem[0], g_mem, "rms_mem")
    q2 = _mm(h2, W["w_xq"], "nn", BF16, "xattn_q")
    w_xkv = jnp.concatenate([W["w_xk"], W["w_xv"]], axis=1)
    kvm = _mm(hm, w_xkv, "nn", BF16, "xattn_kv")
    MW = N_MEM_HEADS * HEAD_DIM
    oc, lse_mem = _mem_fwd(q2, kvm, "xattn_fwd")
    y2 = _mm(oc, W["w_xo"], "nn", F32, "xattn_o")
    x2, h3 = _resid_norm(x1, y2, g_xattn_post, g_ffn_pre, "resid_xattn")

    gu, act = _ffn_up(h3, w_gu, "ffn_up")
    y3 = _mm(act, W["w_down"], "nn", F32, "ffn_down", tk=1536)
    dx3, loss_tile = _final_loss(x2, y3, g_ffn_post, loss_target[0], "final_loss")

    grads = {}
    small = {}
    _, dy3_b, dg = _rms_bwd(y3, g_ffn_post, dx3, None, "bwd_norm_ffn_post", want=("bf16",))
    small["g_ffn_post"] = dg[0:1]
    grads["w_down"] = _mm(act, dy3_b, "tn", F32, "grad_w_down", tm=1408)
    dgu = _ffn_dact(dy3_b, W["w_down"], gu, "ffn_dact")
    dw_gu = _mm(h3, dgu, "tn", F32, "grad_w_gu", tn=1408).reshape(D, nft, 2, FF_TILE)
    grads["w_gate"], grads["w_up"] = dw_gu[:, :, 0].reshape(D, F), dw_gu[:, :, 1].reshape(D, F)
    dh3 = _mm(dgu, w_gu, "nt", F32, "bwd_ffn_in", tk=1024)
    dx2, _, dg = _rms_bwd(x2, g_ffn_pre, dh3, dx3, "bwd_norm_ffn_pre", want=("f32",))
    small["g_ffn_pre"] = dg[0:1]

    _, dy2_b, dg = _rms_bwd(y2, g_xattn_post, dx2, None, "bwd_norm_xattn_post", want=("bf16",))
    small["g_xattn_post"] = dg[0:1]
    grads["w_xo"] = _mm(oc, dy2_b, "tn", F32, "grad_w_xo")
    doc = _mm(dy2_b, W["w_xo"], "nt", BF16, "bwd_xattn_o")
    delta_mem = _head_rowdot(doc, [oc], "xattn_delta")[:, :N_MEM_HEADS].T.reshape(N_MEM_HEADS // 2, 2, S)
    dq2, dkm, dvm = _mem_bwd(q2, kvm, doc, lse_mem, delta_mem, "xattn_bwd")
    dkvm = jnp.concatenate([dkm, dvm], axis=1).astype(BF16)
    grads["w_xq"] = _mm(h2, dq2, "tn", F32, "grad_w_xq")
    dw_xkv = _mm(hm, dkvm, "tn", F32, "grad_w_xkv")
    grads["w_xk"], grads["w_xv"] = dw_xkv[:, :MW], dw_xkv[:, MW:]
    dhm = _mm(dkvm, w_xkv, "nt", F32, "bwd_xattn_kv")
    _, _, dg = _rms_bwd(mem[0], g_mem, dhm, None, "bwd_norm_mem", want=())
    small["g_mem"] = dg[0:1]
    dh2 = _mm(dq2, W["w_xq"], "nt", F32, "bwd_xattn_q")
    dx1, _, dg = _rms_bwd(x1, g_xattn_pre, dh2, dx2, "bwd_norm_xattn_pre", want=("f32",))
    small["g_xattn_pre"] = dg[0:1]

    _, da_b, dg = _rms_bwd(a, g_mix_post, dx1, None, "bwd_norm_mix_post", want=("bf16",))
    small["g_mix_post"] = dg[0:1]
    dw_out_cat = _mm(o_cat, da_b, "tn", F32, "grad_w_out")
    dw_out_dil = _add_n([dw_out_cat[FOX_WIDTH + p * DIL_WIDTH:FOX_WIDTH + (p + 1) * DIL_WIDTH] for p in range(3)],
                        "grad_w_out_dil")
    grads["w_out"] = jnp.concatenate([dw_out_cat[:FOX_WIDTH], dw_out_dil], axis=0)
    do = _mm(da_b, w_out_b, "nt", BF16, "bwd_proj_out")
    do_fox, do_dil = do[:, :FOX_WIDTH], do[:, FOX_WIDTH:]

    delta_fox = _head_rowdot(do_fox, [o_cat[:, :FOX_WIDTH]], "fox_delta")[:, :N_FOX_HEADS].T
    dq_aug, dk_aug, dvf = _fox_bwd(_with_ones(fqt), fk, _with_ones(fkt), fv, _to_heads_t(do_fox, N_FOX_HEADS), c_rep,
                                   lse_fox, delta_fox.reshape(N_FOX_HEADS, 1, S), "fox_bwd")
    dqf, dkf = dq_aug[:, :HEAD_DIM], dk_aug[:, :HEAD_DIM]
    dfg_t, db_f = _forget_bwd(fg_t, b_col, dq_aug[:, HEAD_DIM], dk_aug[:, HEAD_DIM], "forget_bwd")

    delta_dil = _head_rowdot(do_dil, o_dil, "dilated_delta")[:, :N_DIL_HEADS].T
    do_res = [do_dil.reshape(1, S, DIL_WIDTH)] + [
        _to_residues(do, 1, DIL_WIDTH, d, f"dilated_do_residues_{d}") for d in DILATIONS[1:]]
    dil_grads = [_dil_bwd(*views[p], do_res[p], bias_t[p], lse_perm[p], to_perm(delta_dil, d), f"dilated_bwd_{d}")
                 for p, d in enumerate(DILATIONS)]
    dbias_t = jnp.stack([g[3].reshape(HEAD_PAIRS, 2 * BAND, 2, BAND).transpose(0, 2, 1, 3).reshape(
        N_DIL_HEADS, 2 * BAND, BAND) for g in dil_grads])
    d_rel = _bucket_reduce(dbias_t, jnp.asarray(bucket_map.transpose(0, 2, 1)), "rel_bias_grad")[:, :N_DIL_HEADS]
    dfg_pad = jnp.pad(dfg_t.T, ((0, 0), (0, LANES - N_FOX_HEADS))).astype(BF16)
    dcat = _sum_cast_cols([[_from_heads_t(dqf)], [_from_heads_t(dkf)], [_from_heads_t(dvf)]]
                          + [[tok_or_res(g[j]) for g in dil_grads] for j in range(3)],
                          BF16, "dqkv_assemble", tail=dfg_pad)
    dw_cat = _mm(h1, dcat, "tn", F32, "grad_w_qkv", tm=512, tn=3200)
    n_qkv = 3 * (FOX_WIDTH + DIL_WIDTH)
    grads["w_in"] = jnp.concatenate([dw_cat[:, :3 * FOX_WIDTH], dw_cat[:, n_qkv:n_qkv + N_FOX_HEADS],
                                     dw_cat[:, 3 * FOX_WIDTH:n_qkv]], axis=1)
    w_cat = jnp.concatenate([w_qkv, w_fg_pad], axis=1)
    dh1 = _mm(dcat, w_cat, "nt", F32, "bwd_proj_in", tk=640)
    grad_x, _, dg = _rms_bwd(xs, g_mix_pre, dh1, dx1, "bwd_norm_mix_pre", want=("f32",))
    small["g_mix_pre"] = dg[0:1]
    small["b_f"] = db_f[:, 0].reshape(1, N_FOX_HEADS)
    small["rel_bias"] = d_rel

    split = {n: _split_weight(grads[n], n) for n in big}
    parts = jnp.stack([_pack({n: split[n][j] for n in big}) for j in range(N_CHIPS)])
    R = parts.shape[1]
    half = R // 2
    keep = lax.dynamic_slice_in_dim(parts, my_c * half, half, axis=1)
    give = lax.dynamic_slice_in_dim(parts, (1 - my_c) * half, half, axis=1)
    got = _sibling_exchange(give, "grads_to_sibling")
    chip_sum = _add_n([keep.reshape(-1, PACK_COLS), got.reshape(-1, PACK_COLS)], "grads_add_sibling")
    chip_sum = chip_sum.reshape(N_CHIPS, half, PACK_COLS)
    my_chip = 2 * my_x + my_y
    from_chips = _chip_scatter(chip_sum.astype(BF16), "grads_to_chips")
    own = lax.dynamic_index_in_dim(chip_sum, my_chip, axis=0, keepdims=False)
    g_half = _add_n([own, from_chips[0], from_chips[1], from_chips[2]], "grads_add_chips")
    other_half = _sibling_exchange(g_half, "grads_share_sibling")
    g_pack = jnp.where(my_c == 0, jnp.concatenate([g_half, other_half]), jnp.concatenate([other_half, g_half]))

    small_pack = _pack_small(small)
    small_pack = small_pack.at[8, 0].set(loss_tile[0, 0])
    everyone = _all_to_all_small(small_pack, "small_all_gather")
    small_sum = _add_n([everyone[i] for i in range(8)], "small_sum")
    loss = small_sum[8, 0]
    g_small = _unpack_small(small_sum)

    m_pack = _pack({n: args["m_" + n][0] for n in big})
    v_pack = _pack({n: args["v_" + n][0] for n in big})
    d_pack, nm_pack, nv_pack = _adamw(w_pack, g_pack, m_pack, v_pack, "adamw_big")
    outs = {"grad": _unpack(g_pack, shard_shapes), "delta": _unpack(d_pack, shard_shapes),
            "new_m": _unpack(nm_pack, shard_shapes), "new_v": _unpack(nv_pack, shard_shapes)}
    sw = _pack_small({n: args[n] for n in _SMALL + ("b_f", "rel_bias")})
    sm = _pack_small({n: args["m_" + n] for n in _SMALL + ("b_f", "rel_bias")})
    sv = _pack_small({n: args["v_" + n] for n in _SMALL + ("b_f", "rel_bias")})
    sd, snm, snv = _adamw(sw, small_sum.at[8, 0].set(0.0), sm, sv, "adamw_small")
    souts = {"grad": g_small, "delta": _unpack_small(sd), "new_m": _unpack_small(snm), "new_v": _unpack_small(snv)}

    def leaf(kind, n):
        if n in souts[kind]:
            return souts[kind][n].reshape(args[n].shape)
        return outs[kind][n].reshape(args[n].shape)

    result = [loss, grad_x.reshape(x.shape)]
    for kind in ("grad", "delta", "new_m", "new_v"):
        result += [leaf(kind, n) for n in names]
    return tuple(result)
```

```python
import numpy as np
import jax
import jax.numpy as jnp
from jax import lax
from jax.experimental import pallas as pl
from jax.experimental.pallas import tpu as pltpu

F32 = jnp.float32
BF16 = jnp.bfloat16
MESH_IDS = pl.DeviceIdType.MESH

LANES = 128
HEAD_DIM = 64
N_FOX_HEADS = 8
N_DIL_HEADS = 8
N_MEM_HEADS = 4
FOX_WIDTH = N_FOX_HEADS * HEAD_DIM
DIL_WIDTH = N_DIL_HEADS * HEAD_DIM
DILATIONS = (1, 4, 16)
BAND = 128
BAND_CHUNK_MAX = 8 * BAND
N_BUCKETS = 32
MAX_DISTANCE = 2048
QK_SCALE = HEAD_DIM ** -0.5
RMS_EPS = 1e-6
NEG = -1e30
VMEM_LIMIT = 56 << 20

ADAM_LR = 0.001
ADAM_B1 = 0.9
ADAM_B2 = 0.999
ADAM_EPS = 1e-08
ADAM_WD = 0.01
ADAM_STEP = 10

N_CHIPS = 4
PACK_COLS = 1024


def _params(*sem):
    return pltpu.CompilerParams(dimension_semantics=sem, vmem_limit_bytes=VMEM_LIMIT)


def _fit(n, cap):
    if n <= cap:
        return n
    t = (cap // LANES) * LANES
    while t >= LANES:
        if n % t == 0:
            return t
        t -= LANES
    raise ValueError(f"no lane-aligned tile for {n} under {cap}")


def _dot(a, b, dims):
    return lax.dot_general(a, b, (dims, ((), ())), preferred_element_type=F32)


_NN = ((1,), (0,))
_NT = ((1,), (1,))
_TN = ((0,), (0,))


def _mm(a, b, mode, out_dtype, name, tm=1024, tn=1024, tk=1024):
    if mode == "nn":
        (M, K), N = a.shape, b.shape[1]
    elif mode == "nt":
        (M, K), N = a.shape, b.shape[0]
    else:
        (K, M), N = a.shape, b.shape[1]
    tm, tn, tk = _fit(M, tm), _fit(N, tn), _fit(K, tk)
    nk = K // tk
    if mode == "tn":
        a_spec = pl.BlockSpec((tk, tm), lambda i, j, k: (k, i))
    else:
        a_spec = pl.BlockSpec((tm, tk), lambda i, j, k: (i, k))
    if mode == "nt":
        b_spec = pl.BlockSpec((tn, tk), lambda i, j, k: (j, k))
    else:
        b_spec = pl.BlockSpec((tk, tn), lambda i, j, k: (k, j))
    dims = {"nn": _NN, "nt": _NT, "tn": _TN}[mode]

    def body(a_ref, b_ref, o_ref, *acc):
        prod = _dot(a_ref[...].astype(BF16), b_ref[...].astype(BF16), dims)
        if nk == 1:
            o_ref[...] = prod.astype(o_ref.dtype)
            return
        acc_ref, k = acc[0], pl.program_id(2)

        @pl.when(k == 0)
        def _():
            acc_ref[...] = prod

        @pl.when(k > 0)
        def _():
            acc_ref[...] += prod

        @pl.when(k == nk - 1)
        def _():
            o_ref[...] = acc_ref[...].astype(o_ref.dtype)

    return pl.pallas_call(
        body, name=name, grid=(M // tm, N // tn, nk),
        in_specs=[a_spec, b_spec],
        out_specs=pl.BlockSpec((tm, tn), lambda i, j, k: (i, j)),
        out_shape=jax.ShapeDtypeStruct((M, N), out_dtype),
        scratch_shapes=[pltpu.VMEM((tm, tn), F32)] if nk > 1 else [],
        compiler_params=_params("parallel", "parallel", "arbitrary"),
    )(a, b)


def _rms_rows(x):
    return lax.rsqrt(jnp.mean(x * x, axis=-1, keepdims=True) + RMS_EPS)


def _rms_fwd(x, g, name, tr=512):
    S, D = x.shape
    tr = _fit(S, tr)

    def body(x_ref, g_ref, h_ref):
        xv = x_ref[...]
        h_ref[...] = (xv * _rms_rows(xv) * g_ref[...]).astype(BF16)

    return pl.pallas_call(
        body, name=name, grid=(S // tr,),
        in_specs=[pl.BlockSpec((tr, D), lambda i: (i, 0)), pl.BlockSpec((1, D), lambda i: (0, 0))],
        out_specs=pl.BlockSpec((tr, D), lambda i: (i, 0)),
        out_shape=jax.ShapeDtypeStruct((S, D), BF16),
        compiler_params=_params("parallel"),
    )(x, g)


def _resid_norm(xres, y, g_post, g_next, name, tr=512):
    S, D = xres.shape
    tr = _fit(S, tr)

    def body(x_ref, y_ref, gp_ref, gn_ref, xn_ref, h_ref):
        yv = y_ref[...]
        xn = x_ref[...] + yv * _rms_rows(yv) * gp_ref[...]
        xn_ref[...] = xn
        h_ref[...] = (xn * _rms_rows(xn) * gn_ref[...]).astype(BF16)

    row = pl.BlockSpec((tr, D), lambda i: (i, 0))
    vec = pl.BlockSpec((1, D), lambda i: (0, 0))
    return pl.pallas_call(
        body, name=name, grid=(S // tr,),
        in_specs=[row, row, vec, vec], out_specs=[row, row],
        out_shape=[jax.ShapeDtypeStruct((S, D), F32), jax.ShapeDtypeStruct((S, D), BF16)],
        compiler_params=_params("parallel"),
    )(xres, y, g_post, g_next)


def _final_loss(xres, y, g_post, target, name, tr=512):
    S, D = xres.shape
    tr = _fit(S, tr)

    def body(x_ref, y_ref, gp_ref, t_ref, d_ref, loss_ref):
        i = pl.program_id(0)
        yv = y_ref[...]
        err = x_ref[...] + yv * _rms_rows(yv) * gp_ref[...] - t_ref[...]
        d_ref[...] = err * (1.0 / D)

        @pl.when(i == 0)
        def _():
            loss_ref[...] = jnp.zeros_like(loss_ref)

        part = jnp.sum(jnp.sum(err * err, axis=1, keepdims=True), axis=0, keepdims=True)
        loss_ref[...] += jnp.broadcast_to(part * (0.5 / D), loss_ref.shape)

    row = pl.BlockSpec((tr, D), lambda i: (i, 0))
    vec = pl.BlockSpec((1, D), lambda i: (0, 0))
    return pl.pallas_call(
        body, name=name, grid=(S // tr,),
        in_specs=[row, row, vec, row],
        out_specs=[row, pl.BlockSpec((8, LANES), lambda i: (0, 0))],
        out_shape=[jax.ShapeDtypeStruct((S, D), F32), jax.ShapeDtypeStruct((8, LANES), F32)],
        compiler_params=_params("arbitrary"),
    )(xres, y, g_post, target)


def _rms_bwd(xin, g, dy, dres, name, want=("f32", "bf16"), tr=512):
    S, D = xin.shape
    tr = _fit(S, tr)
    has_res = dres is not None

    def body(*refs):
        refs = list(refs)
        dg_ref = refs.pop()
        dxb_ref = refs.pop() if "bf16" in want else None
        dx_ref = refs.pop() if "f32" in want else None
        dr_ref = refs.pop() if has_res else None
        x_ref, g_ref, dy_ref = refs
        i = pl.program_id(0)
        xv = x_ref[...]
        dyv = dy_ref[...].astype(F32)
        xhat = xv * _rms_rows(xv)
        dxhat = dyv * g_ref[...]
        r = _rms_rows(xv)
        dx = r * (dxhat - xhat * jnp.mean(dxhat * xhat, axis=-1, keepdims=True))
        if has_res:
            dx = dx + dr_ref[...]
        if dx_ref is not None:
            dx_ref[...] = dx
        if dxb_ref is not None:
            dxb_ref[...] = dx.astype(BF16)

        @pl.when(i == 0)
        def _():
            dg_ref[...] = jnp.zeros_like(dg_ref)

        dg_ref[...] += jnp.broadcast_to(jnp.sum(dyv * xhat, axis=0, keepdims=True), dg_ref.shape)

    row = pl.BlockSpec((tr, D), lambda i: (i, 0))
    vec = pl.BlockSpec((1, D), lambda i: (0, 0))
    acc = pl.BlockSpec((8, D), lambda i: (0, 0))
    ins = [xin, g, dy] + ([dres] if has_res else [])
    dtypes = [dt for key, dt in (("f32", F32), ("bf16", BF16)) if key in want]
    outs = pl.pallas_call(
        body, name=name, grid=(S // tr,),
        in_specs=[row, vec, row] + ([row] if has_res else []),
        out_specs=[row] * len(dtypes) + [acc],
        out_shape=[jax.ShapeDtypeStruct((S, D), dt) for dt in dtypes] + [jax.ShapeDtypeStruct((8, D), F32)],
        compiler_params=_params("arbitrary"),
    )(*ins)
    by_key = dict(zip([key for key in ("f32", "bf16") if key in want], outs[:-1]))
    return by_key.get("f32"), by_key.get("bf16"), outs[-1]


def _tri(n, upper):
    r = lax.broadcasted_iota(jnp.int32, (n, n), 0)
    c = lax.broadcasted_iota(jnp.int32, (n, n), 1)
    return jnp.where((r <= c) if upper else (r >= c), 1.0, 0.0).astype(F32)


def _forget_fwd(fg_t, b_col, name, ts=512):
    H, S = fg_t.shape
    ts = _fit(S, ts)

    def body(f_ref, b_ref, c_ref, carry_ref):
        i = pl.program_id(0)

        @pl.when(i == 0)
        def _():
            carry_ref[...] = jnp.zeros_like(carry_ref)

        z = f_ref[...] + b_ref[...]
        logf = jnp.minimum(z, 0.0) - jnp.log(1.0 + jnp.exp(-jnp.abs(z)))
        run = lax.dot_general(logf, _tri(ts, True), (_NN, ((), ())), precision=lax.Precision.HIGHEST,
                              preferred_element_type=F32) + carry_ref[:, 0:1]
        c_ref[...] = run
        carry_ref[...] = jnp.broadcast_to(
            carry_ref[:, 0:1] + jnp.sum(logf, axis=1, keepdims=True), carry_ref.shape)

    return pl.pallas_call(
        body, name=name, grid=(S // ts,),
        in_specs=[pl.BlockSpec((H, ts), lambda i: (0, i)), pl.BlockSpec((H, 1), lambda i: (0, 0))],
        out_specs=pl.BlockSpec((H, ts), lambda i: (0, i)),
        out_shape=jax.ShapeDtypeStruct((H, S), F32),
        scratch_shapes=[pltpu.VMEM((H, LANES), F32)],
        compiler_params=_params("arbitrary"),
    )(fg_t, b_col)


def _forget_bwd(fg_t, b_col, dc_plus, dc_minus, name, ts=512):
    H, S = fg_t.shape
    ts = _fit(S, ts)
    nb = S // ts

    def body(f_ref, b_ref, dcp_ref, dcm_ref, df_ref, db_ref, carry_ref):
        i = pl.program_id(0)

        @pl.when(i == 0)
        def _():
            carry_ref[...] = jnp.zeros_like(carry_ref)
            db_ref[...] = jnp.zeros_like(db_ref)

        dc = dcp_ref[...] - dcm_ref[...]
        suffix = lax.dot_general(dc, _tri(ts, False), (_NN, ((), ())), precision=lax.Precision.HIGHEST,
                                 preferred_element_type=F32) + carry_ref[:, 0:1]
        z = f_ref[...] + b_ref[...]
        sig_neg = 1.0 / (1.0 + jnp.exp(z))
        df = suffix * sig_neg
        df_ref[...] = df
        carry_ref[...] = jnp.broadcast_to(
            carry_ref[:, 0:1] + jnp.sum(dc, axis=1, keepdims=True), carry_ref.shape)
        db_ref[...] += jnp.broadcast_to(jnp.sum(df, axis=1, keepdims=True), db_ref.shape)

    rev = pl.BlockSpec((H, ts), lambda i: (0, nb - 1 - i))
    return pl.pallas_call(
        body, name=name, grid=(nb,),
        in_specs=[rev, pl.BlockSpec((H, 1), lambda i: (0, 0)), rev, rev],
        out_specs=[rev, pl.BlockSpec((H, LANES), lambda i: (0, 0))],
        out_shape=[jax.ShapeDtypeStruct((H, S), F32), jax.ShapeDtypeStruct((H, LANES), F32)],
        scratch_shapes=[pltpu.VMEM((H, LANES), F32)],
        compiler_params=_params("arbitrary"),
    )(fg_t, b_col, dc_plus, dc_minus)


def _tile_lanes(x, n):
    return x if n == LANES else jnp.tile(x, (1, n // LANES))


ONES_ROWS = 16


def _with_ones(t):
    return jnp.concatenate([t, jnp.ones((t.shape[0], ONES_ROWS, t.shape[2]), t.dtype)], axis=1)


def _fox_fwd(qt, k, vt, c_row, c_rep, name, tq=512, tk=1024):
    H, Dh, S = qt.shape
    tk = _fit(S, tk)
    tq = _fit(tk, tq)
    ratio = tk // tq

    def body(qt_ref, k_ref, vt_ref, c_ref, crep_ref, o_ref, lse_ref, m_ref, acc_ref,
             sa_ref, sb_ref, ta_ref, tb_ref):
        i = pl.program_id(1)
        qv = qt_ref[...] * QK_SCALE
        cq0 = c_ref[:, pl.ds(pl.multiple_of(i * tq, LANES), LANES)][:, 0:1]
        m_ref[...] = jnp.full_like(m_ref, NEG)
        acc_ref[...] = jnp.zeros_like(acc_ref)
        n = i // ratio
        q_off = (i - n * ratio) * tq

        def scores(j, s_ref, t_ref, diagonal):
            off = pl.multiple_of(j * tk, LANES)
            s = _dot(k_ref[pl.ds(off, tk), :], qv, _NN) + _tile_lanes(cq0 - crep_ref[pl.ds(off, tk), :], tq)
            if diagonal:
                key = lax.broadcasted_iota(jnp.int32, (tk, tq), 0)
                qry = lax.broadcasted_iota(jnp.int32, (tk, tq), 1) + q_off
                s = jnp.where(key <= qry, s, NEG)
            s_ref[...] = s
            t_ref[...] = jnp.max(s, axis=0, keepdims=True)

        def absorb(j, s_ref, t_ref):
            off = pl.multiple_of(j * tk, LANES)
            m_old = m_ref[...]
            m_new = jnp.maximum(m_old, t_ref[...])
            p = jnp.exp(s_ref[...] - m_new)
            alpha = jnp.exp(m_old - m_new)
            acc_ref[...] = alpha * acc_ref[...] + _dot(vt_ref[:, pl.ds(off, tk)], p.astype(BF16), _NN)
            m_ref[...] = m_new

        scores(n, sa_ref, ta_ref, True)

        def loop_body(jj, carry):
            scores(2 * jj, sb_ref, tb_ref, False)
            absorb(jnp.where(jj == 0, n, 2 * jj - 1), sa_ref, ta_ref)
            scores(2 * jj + 1, sa_ref, ta_ref, False)
            absorb(2 * jj, sb_ref, tb_ref)
            return carry

        pairs = n // 2
        lax.fori_loop(0, pairs, loop_body, 0)
        held = jnp.where(pairs == 0, n, 2 * pairs - 1)

        @pl.when(n % 2 == 1)
        def _():
            scores(n - 1, sb_ref, tb_ref, False)
            absorb(held, sa_ref, ta_ref)
            absorb(n - 1, sb_ref, tb_ref)

        @pl.when(n % 2 == 0)
        def _():
            absorb(held, sa_ref, ta_ref)

        l = acc_ref[Dh:Dh + 1, :]
        o_ref[...] = acc_ref[0:Dh, :] / l
        lse_ref[...] = m_ref[...] + jnp.log(l) - cq0

    lanes_full = pl.BlockSpec((None, Dh + ONES_ROWS, S), lambda h, i: (h, 0, 0))
    lanes_tile = pl.BlockSpec((None, Dh, tq), lambda h, i: (h, 0, i))
    return pl.pallas_call(
        body, name=name, grid=(H, S // tq),
        in_specs=[lanes_tile, pl.BlockSpec((None, S, Dh), lambda h, i: (h, 0, 0)), lanes_full,
                  pl.BlockSpec((None, 1, S), lambda h, i: (h, 0, 0)),
                  pl.BlockSpec((None, S, LANES), lambda h, i: (h, 0, 0))],
        out_specs=[lanes_tile, pl.BlockSpec((None, 1, tq), lambda h, i: (h, 0, i))],
        out_shape=[jax.ShapeDtypeStruct((H, Dh, S), F32), jax.ShapeDtypeStruct((H, 1, S), F32)],
        scratch_shapes=[pltpu.VMEM((1, tq), F32), pltpu.VMEM((Dh + ONES_ROWS, tq), F32),
                        pltpu.VMEM((tk, tq), F32), pltpu.VMEM((tk, tq), F32),
                        pltpu.VMEM((1, tq), F32), pltpu.VMEM((1, tq), F32)],
        compiler_params=_params("parallel", "arbitrary"),
    )(qt, k, vt, c_row, c_rep)


def _fox_bwd(qt, k, kt, v, dot, c_rep, lse_row, delta_row, name, tq=1024, tk=512):
    H, Da, S = qt.shape
    Dh = Da - ONES_ROWS
    tq = _fit(S, tq)
    tk = _fit(tq, tk)
    ratio = tq // tk
    nq = S // tq
    nk = S // tk

    def body(k_ref, kt_ref, v_ref, crep_ref, qt_ref, dot_ref, lse_ref, dl_ref,
             dqt_ref, dkt_ref, dvt_ref, dka_ref, dva_ref):
        j = pl.program_id(1)

        @pl.when(j == 0)
        def _():
            dqt_ref[...] = jnp.zeros_like(dqt_ref)

        kv = k_ref[...]
        ktv = kt_ref[...]
        vv = v_ref[...]
        c_col = _tile_lanes(crep_ref[...], tq)
        dka_ref[...] = jnp.zeros_like(dka_ref)
        dva_ref[...] = jnp.zeros_like(dva_ref)
        i_diag = j // ratio
        k_off = (j - i_diag * ratio) * tk

        def step(i, diagonal):
            off = pl.multiple_of(i * tq, LANES)
            qv = qt_ref[:, pl.ds(off, tq)] * QK_SCALE
            dov = dot_ref[:, pl.ds(off, tq)]
            e = _dot(kv, qv[0:Dh, :], _NN) - lse_ref[:, pl.ds(off, tq)] - c_col
            if diagonal:
                key = lax.broadcasted_iota(jnp.int32, (tk, tq), 0) + k_off
                qry = lax.broadcasted_iota(jnp.int32, (tk, tq), 1)
                e = jnp.where(key <= qry, e, NEG)
            p_t = jnp.exp(e)
            dva_ref[...] += _dot(dov, p_t.astype(BF16), _NT)
            ds_t = p_t * (_dot(vv, dov, _NN) - dl_ref[:, pl.ds(off, tq)])
            ds_b = ds_t.astype(BF16)
            dka_ref[...] += _dot(qv, ds_b, _NT)
            dqt_ref[:, pl.ds(off, tq)] += _dot(ktv, ds_b, _NN)

        step(i_diag, True)

        def loop_body(i, carry):
            step(i, False)
            return carry

        lax.fori_loop(i_diag + 1, nq, loop_body, 0)
        dkt_ref[0:Dh, :] = dka_ref[0:Dh, :]
        dkt_ref[Dh:, :] = dka_ref[Dh:, :] * (1.0 / QK_SCALE)
        dvt_ref[...] = dva_ref[...]

        @pl.when(j == nk - 1)
        def _():
            dqt_ref[0:Dh, :] = dqt_ref[0:Dh, :] * QK_SCALE

    rows_tile = pl.BlockSpec((None, tk, Dh), lambda h, j: (h, j, 0))
    ones_tile = pl.BlockSpec((None, Da, tk), lambda h, j: (h, 0, j))
    lanes_tile = pl.BlockSpec((None, Dh, tk), lambda h, j: (h, 0, j))
    rep = pl.BlockSpec((None, tk, LANES), lambda h, j: (h, j, 0))
    ones_full = pl.BlockSpec((None, Da, S), lambda h, j: (h, 0, 0))
    lanes_full = pl.BlockSpec((None, Dh, S), lambda h, j: (h, 0, 0))
    rowv = pl.BlockSpec((None, 1, S), lambda h, j: (h, 0, 0))
    return pl.pallas_call(
        body, name=name, grid=(H, nk),
        in_specs=[rows_tile, ones_tile, rows_tile, rep, ones_full, lanes_full, rowv, rowv],
        out_specs=[ones_full, ones_tile, lanes_tile],
        out_shape=[jax.ShapeDtypeStruct((H, Da, S), F32), jax.ShapeDtypeStruct((H, Da, S), F32),
                   jax.ShapeDtypeStruct((H, Dh, S), F32)],
        scratch_shapes=[pltpu.VMEM((Da, tk), F32), pltpu.VMEM((Dh, tk), F32)],
        compiler_params=_params("parallel", "arbitrary"),
    )(k, kt, v, c_rep, qt, dot, lse_row, delta_row)


DIL_Q_BLOCK = 3 * FOX_WIDTH // LANES
HEAD_PAIRS = N_DIL_HEADS // 2
PAIR_BLOCKS = DIL_WIDTH // LANES


def _band_geometry(S, d):
    L = S // d
    chunk = min(BAND_CHUNK_MAX, L)
    assert L % chunk == 0 and chunk % BAND == 0
    return L, chunk, chunk // BAND, L // chunk


def _band_in_specs(S, d, base):
    L, chunk, nb, _ = _band_geometry(S, d)

    def col(kind):
        return lambda hp, r, i: (r, i, base + kind * PAIR_BLOCKS + hp)

    def col_prev(kind):
        return lambda hp, r, i: (r, jnp.maximum(i * nb - 1, 0), base + kind * PAIR_BLOCKS + hp)

    main = [pl.BlockSpec((None, chunk, LANES), col(kind)) for kind in range(3)]
    prev = [pl.BlockSpec((None, BAND, LANES), col_prev(kind)) for kind in range(3)]
    bias = pl.BlockSpec((None, 2 * BAND, 2 * BAND), lambda hp, r, i: (hp, 0, 0))
    stat = pl.BlockSpec((None, 2, chunk), lambda hp, r, i: (hp, 0, r * (L // chunk) + i))
    tok = pl.BlockSpec((None, chunk, LANES), lambda hp, r, i: (r, i, hp))
    return main, prev, bias, stat, tok


def _to_residues(x, col_block, width, d, name, tr=512):
    S = x.shape[0]
    tr = _fit(S, tr)

    def body(x_ref, o_ref, tmp_ref):
        for j in range(width // LANES):
            cols = slice(j * LANES, (j + 1) * LANES)
            tmp_ref[j] = x_ref[:, cols].astype(F32)
            for r in range(d):
                o_ref[r, :, cols] = tmp_ref[j, pl.ds(r, tr // d, stride=d), :].astype(o_ref.dtype)

    return pl.pallas_call(
        body, name=name, grid=(S // tr,),
        in_specs=[pl.BlockSpec((tr, width), lambda i: (i, col_block))],
        out_specs=pl.BlockSpec((d, tr // d, width), lambda i: (0, i, 0)),
        out_shape=jax.ShapeDtypeStruct((d, S // d, width), x.dtype),
        scratch_shapes=[pltpu.VMEM((width // LANES, tr, LANES), F32)],
        compiler_params=_params("parallel"),
    )(x)


def _token_rows(ref, cols, tmp_ref):
    if len(ref.shape) == 2:
        return ref[:, cols].astype(F32)
    d, rows = ref.shape[0], ref.shape[1]
    for r in range(d):
        tmp_ref[pl.ds(r, rows, stride=d), :] = ref[r, :, cols].astype(F32)
    return tmp_ref[...]


def _row_spec(t, tr):
    if t.ndim == 2:
        return pl.BlockSpec((tr, t.shape[1]), lambda i: (i, 0))
    d = t.shape[0]
    return pl.BlockSpec((d, tr // d, t.shape[2]), lambda i: (0, i, 0))


def _head_lanes(a):
    return lax.broadcasted_iota(jnp.int32, (1, LANES), 1) // HEAD_DIM == a


def _one_head(x, a):
    return jnp.where(_head_lanes(a), x, jnp.zeros_like(x))


def _head_stack(x):
    return jnp.concatenate([_one_head(x, 0), _one_head(x, 1)], axis=0)


def _pair_rows(ref, rows):
    return jnp.concatenate([ref[0:1, rows], ref[1:2, rows]], axis=1)


def _band_scores_t(kb, q_stack, bias_t, first):
    s = _dot(kb, q_stack, _NT) + bias_t
    if first is not None:
        key = lax.broadcasted_iota(jnp.int32, s.shape, 0)
        s = jnp.where(jnp.logical_and(first, key < BAND), NEG, s)
    return s


def _pair_select(stacked):
    return jnp.where(_head_lanes(0), stacked[0:BAND, :], stacked[BAND:, :])


def _dil_lse(qkv_v, base, bias_t, name):
    d, L = qkv_v.shape[:2]
    S = L * d
    _, chunk, nb, nchunks = _band_geometry(S, d)
    main, prev, bias, stat, _ = _band_in_specs(S, d, base)

    def body(q_ref, k_ref, kp_ref, b_ref, lse_ref, kext_ref):
        first = pl.program_id(2) == 0
        kext_ref[0:BAND, :] = kp_ref[...]
        kext_ref[BAND:, :] = k_ref[...]
        for b in range(nb):
            rows, ext = slice(b * BAND, (b + 1) * BAND), slice(b * BAND, (b + 2) * BAND)
            s = _band_scores_t(kext_ref[ext, :], _head_stack(q_ref[rows, :] * QK_SCALE), b_ref[...],
                               first if b == 0 else None)
            m = jnp.max(s, axis=0, keepdims=True)
            lse = m + jnp.log(jnp.sum(jnp.exp(s - m), axis=0, keepdims=True))
            lse_ref[0:1, rows] = lse[:, 0:BAND]
            lse_ref[1:2, rows] = lse[:, BAND:]

    return pl.pallas_call(
        body, name=name, grid=(HEAD_PAIRS, d, nchunks),
        in_specs=[main[0], main[1], prev[1], bias], out_specs=stat,
        out_shape=jax.ShapeDtypeStruct((HEAD_PAIRS, 2, S), F32),
        scratch_shapes=[pltpu.VMEM((chunk + BAND, LANES), BF16)],
        compiler_params=_params("parallel", "parallel", "parallel"),
    )(qkv_v, qkv_v, qkv_v, bias_t)


def _dil_out(qkv_v, base, bias_t, lse_joint, name):
    d, L = qkv_v.shape[:2]
    S = L * d
    _, chunk, nb, nchunks = _band_geometry(S, d)
    main, prev, bias, stat, tok = _band_in_specs(S, d, base)

    def body(q_ref, k_ref, kp_ref, v_ref, vp_ref, b_ref, lse_ref, o_ref, kext_ref, vext_ref):
        first = pl.program_id(2) == 0
        kext_ref[0:BAND, :] = kp_ref[...]
        kext_ref[BAND:, :] = k_ref[...]
        vext_ref[0:BAND, :] = vp_ref[...]
        vext_ref[BAND:, :] = v_ref[...]
        for b in range(nb):
            rows, ext = slice(b * BAND, (b + 1) * BAND), slice(b * BAND, (b + 2) * BAND)
            s = _band_scores_t(kext_ref[ext, :], _head_stack(q_ref[rows, :] * QK_SCALE), b_ref[...],
                               first if b == 0 else None)
            p_t = jnp.exp(s - _pair_rows(lse_ref, rows))
            o_ref[rows, :] = _pair_select(_dot(p_t.astype(BF16), vext_ref[ext, :], _TN)).astype(BF16)

    return pl.pallas_call(
        body, name=name, grid=(HEAD_PAIRS, d, nchunks),
        in_specs=[main[0], main[1], prev[1], main[2], prev[2], bias, stat], out_specs=tok,
        out_shape=jax.ShapeDtypeStruct((d, L, DIL_WIDTH), BF16),
        scratch_shapes=[pltpu.VMEM((chunk + BAND, LANES), BF16), pltpu.VMEM((chunk + BAND, LANES), BF16)],
        compiler_params=_params("parallel", "parallel", "parallel"),
    )(qkv_v, qkv_v, qkv_v, qkv_v, qkv_v, bias_t, lse_joint)


def _dil_bwd(qkv_v, base, do_v, bias_t, lse_joint, delta, name):
    d, L = qkv_v.shape[:2]
    S = L * d
    _, chunk, nb, nchunks = _band_geometry(S, d)
    main, prev, bias, stat, tok = _band_in_specs(S, d, base)
    nblocks = L // BAND

    def nxt_row(i):
        return jnp.minimum((i + 1) * nb, nblocks - 1)

    q_next = pl.BlockSpec((None, BAND, LANES), lambda hp, r, i: (r, nxt_row(i), base + hp))
    do_next = pl.BlockSpec((None, BAND, LANES), lambda hp, r, i: (r, nxt_row(i), hp))
    stat_next = pl.BlockSpec((None, 2, BAND), lambda hp, r, i: (hp, 0, r * nblocks + nxt_row(i)))

    def body(q_ref, k_ref, kp_ref, v_ref, vp_ref, do_ref, b_ref, lse_ref, dl_ref,
             qn_ref, don_ref, lsen_ref, dln_ref,
             dq_ref, dk_ref, dv_ref, db_ref, kext_ref, vext_ref, dkext_ref, dvext_ref):
        r, i = pl.program_id(1), pl.program_id(2)
        first = i == 0
        has_next = i + 1 < nchunks
        tail = slice(BAND + chunk, 2 * BAND + chunk)
        kext_ref[0:BAND, :] = kp_ref[...]
        kext_ref[BAND:BAND + chunk, :] = k_ref[...]
        kext_ref[tail, :] = jnp.zeros((BAND, LANES), BF16)
        vext_ref[0:BAND, :] = vp_ref[...]
        vext_ref[BAND:BAND + chunk, :] = v_ref[...]
        vext_ref[tail, :] = jnp.zeros((BAND, LANES), BF16)
        dkext_ref[...] = jnp.zeros_like(dkext_ref)
        dvext_ref[...] = jnp.zeros_like(dvext_ref)

        @pl.when(jnp.logical_and(r == 0, i == 0))
        def _():
            db_ref[...] = jnp.zeros_like(db_ref)

        def block(q2, do2, lse_row, dl_row, ext, mask_rows):
            q_stack, do_stack = _head_stack(q2), _head_stack(do2)
            s = _dot(kext_ref[ext, :], q_stack, _NT) + b_ref[...]
            if mask_rows is not None:
                s = jnp.where(mask_rows, NEG, s)
            p_t = jnp.exp(s - lse_row)
            ds_t = p_t * (_dot(vext_ref[ext, :], do_stack, _NT) - dl_row)
            ds_b = ds_t.astype(BF16)
            dkext_ref[ext, :] += _dot(ds_b, q_stack, _NN)
            dvext_ref[ext, :] += _dot(p_t.astype(BF16), do_stack, _NN)
            return ds_t, ds_b

        key = lax.broadcasted_iota(jnp.int32, (2 * BAND, 2 * BAND), 0)
        all_lanes = slice(0, BAND)
        for b in range(nb):
            rows, ext = slice(b * BAND, (b + 1) * BAND), slice(b * BAND, (b + 2) * BAND)
            mask = jnp.logical_and(first, key < BAND) if b == 0 else None
            ds_t, ds_b = block(q_ref[rows, :] * QK_SCALE, do_ref[rows, :], _pair_rows(lse_ref, rows),
                               _pair_rows(dl_ref, rows), ext, mask)
            dq_ref[rows, :] = _pair_select(_dot(ds_b, kext_ref[ext, :], _TN)) * QK_SCALE
            db_ref[...] += ds_t
        block(qn_ref[...] * QK_SCALE, don_ref[...], _pair_rows(lsen_ref, all_lanes), _pair_rows(dln_ref, all_lanes),
              slice(chunk, chunk + 2 * BAND), jnp.logical_or(jnp.logical_not(has_next), key >= BAND))
        dk_ref[...] = dkext_ref[BAND:BAND + chunk, :]
        dv_ref[...] = dvext_ref[BAND:BAND + chunk, :]

    ext_rows = chunk + 2 * BAND
    return pl.pallas_call(
        body, name=name, grid=(HEAD_PAIRS, d, nchunks),
        in_specs=[main[0], main[1], prev[1], main[2], prev[2], tok, bias, stat, stat,
                  q_next, do_next, stat_next, stat_next],
        out_specs=[tok, tok, tok, bias],
        out_shape=[jax.ShapeDtypeStruct((d, L, DIL_WIDTH), F32)] * 3
                  + [jax.ShapeDtypeStruct((HEAD_PAIRS, 2 * BAND, 2 * BAND), F32)],
        scratch_shapes=[pltpu.VMEM((ext_rows, LANES), BF16), pltpu.VMEM((ext_rows, LANES), BF16),
                        pltpu.VMEM((ext_rows, LANES), F32), pltpu.VMEM((ext_rows, LANES), F32)],
        compiler_params=_params("arbitrary", "arbitrary", "arbitrary"),
    )(qkv_v, qkv_v, qkv_v, qkv_v, qkv_v, do_v, bias_t, lse_joint, delta, qkv_v, do_v, lse_joint, delta)


def _lse_join(lse3, name):
    P, H, S = lse3.shape

    def body(l_ref, o_ref):
        a, b, c = l_ref[0], l_ref[1], l_ref[2]
        m = jnp.maximum(jnp.maximum(a, b), c)
        o_ref[...] = m + jnp.log(jnp.exp(a - m) + jnp.exp(b - m) + jnp.exp(c - m))

    return pl.pallas_call(body, name=name, out_shape=jax.ShapeDtypeStruct((H, S), F32))(lse3)


def _bucket_reduce(dbias_t, bucket_map_t, name):
    P, H = dbias_t.shape[:2]

    def body(db_ref, bk_ref, o_ref):
        p, h = pl.program_id(0), pl.program_id(1)

        @pl.when(jnp.logical_and(p == 0, h == 0))
        def _():
            o_ref[...] = jnp.zeros_like(o_ref)

        db, bk = db_ref[...], bk_ref[...]
        row = lax.broadcasted_iota(jnp.int32, (N_BUCKETS, LANES), 0)
        lane = lax.broadcasted_iota(jnp.int32, (N_BUCKETS, LANES), 1)

        def one(b, acc):
            val = jnp.sum(jnp.sum(jnp.where(bk == b, db, 0.0), axis=1, keepdims=True), axis=0, keepdims=True)
            return acc + jnp.where(jnp.logical_and(row == b, lane == h), val, 0.0)

        o_ref[...] += lax.fori_loop(0, N_BUCKETS, one, jnp.zeros((N_BUCKETS, LANES), F32))

    return pl.pallas_call(
        body, name=name, grid=(P, H),
        in_specs=[pl.BlockSpec((None, None, 2 * BAND, BAND), lambda p, h: (p, h, 0, 0)),
                  pl.BlockSpec((None, 2 * BAND, BAND), lambda p, h: (p, 0, 0))],
        out_specs=pl.BlockSpec((N_BUCKETS, LANES), lambda p, h: (0, 0)),
        out_shape=jax.ShapeDtypeStruct((N_BUCKETS, LANES), F32),
        compiler_params=_params("arbitrary", "arbitrary"),
    )(dbias_t, bucket_map_t)


def _mem_fwd(q, kv, name, tq=1024):
    S, W = q.shape
    N = kv.shape[0]
    pairs = W // LANES
    tq = _fit(S, tq)

    def body(q_ref, k_ref, v_ref, o_ref, lse_ref):
        for a in range(2):
            lanes = slice(a * HEAD_DIM, (a + 1) * HEAD_DIM)
            s = _dot(k_ref[:, lanes], q_ref[:, lanes] * QK_SCALE, _NT)
            m = jnp.max(s, axis=0, keepdims=True)
            e = jnp.exp(s - m)
            l = jnp.sum(e, axis=0, keepdims=True)
            o_ref[:, lanes] = _dot((e / l).astype(BF16), v_ref[:, lanes], _TN).astype(BF16)
            lse_ref[a:a + 1, :] = m + jnp.log(l)

    return pl.pallas_call(
        body, name=name, grid=(pairs, S // tq),
        in_specs=[pl.BlockSpec((tq, LANES), lambda hp, i: (i, hp)),
                  pl.BlockSpec((N, LANES), lambda hp, i: (0, hp)),
                  pl.BlockSpec((N, LANES), lambda hp, i: (0, pairs + hp))],
        out_specs=[pl.BlockSpec((tq, LANES), lambda hp, i: (i, hp)),
                   pl.BlockSpec((None, 2, tq), lambda hp, i: (hp, 0, i))],
        out_shape=[jax.ShapeDtypeStruct((S, W), BF16), jax.ShapeDtypeStruct((pairs, 2, S), F32)],
        compiler_params=_params("parallel", "parallel"),
    )(q, kv, kv)


def _mem_bwd(q, kv, do, lse, delta, name, tq=1024):
    S, W = q.shape
    N = kv.shape[0]
    pairs = W // LANES
    tq = _fit(S, tq)

    def body(q_ref, k_ref, v_ref, do_ref, lse_ref, dl_ref, dq_ref, dk_ref, dv_ref):
        i = pl.program_id(1)

        @pl.when(i == 0)
        def _():
            dk_ref[...] = jnp.zeros_like(dk_ref)
            dv_ref[...] = jnp.zeros_like(dv_ref)

        for a in range(2):
            lanes = slice(a * HEAD_DIM, (a + 1) * HEAD_DIM)
            qv, dov = q_ref[:, lanes] * QK_SCALE, do_ref[:, lanes]
            kv_, vv = k_ref[:, lanes], v_ref[:, lanes]
            p_t = jnp.exp(_dot(kv_, qv, _NT) - lse_ref[a:a + 1, :])
            ds_t = p_t * (_dot(vv, dov, _NT) - dl_ref[a:a + 1, :])
            ds_b = ds_t.astype(BF16)
            dq_ref[:, lanes] = (_dot(ds_b, kv_, _TN) * QK_SCALE).astype(BF16)
            dk_ref[:, lanes] += _dot(ds_b, qv, _NN)
            dv_ref[:, lanes] += _dot(p_t.astype(BF16), dov, _NN)

    qs = pl.BlockSpec((tq, LANES), lambda hp, i: (i, hp))
    stat = pl.BlockSpec((None, 2, tq), lambda hp, i: (hp, 0, i))
    acc = pl.BlockSpec((N, LANES), lambda hp, i: (0, hp))
    return pl.pallas_call(
        body, name=name, grid=(pairs, S // tq),
        in_specs=[qs, acc, pl.BlockSpec((N, LANES), lambda hp, i: (0, pairs + hp)), qs, stat, stat],
        out_specs=[qs, acc, acc],
        out_shape=[jax.ShapeDtypeStruct((S, W), BF16), jax.ShapeDtypeStruct((N, W), F32),
                   jax.ShapeDtypeStruct((N, W), F32)],
        compiler_params=_params("parallel", "arbitrary"),
    )(q, kv, kv, do, lse, delta)


def _head_rowdot(a, bs, name, tr=512):
    S, W = a.shape
    tr = _fit(S, tr)

    def body(*refs):
        a_ref, b_refs, o_ref, tmp_ref = refs[0], refs[1:-2], refs[-2], refs[-1]
        col = lax.broadcasted_iota(jnp.int32, (LANES, LANES), 0)
        lane = lax.broadcasted_iota(jnp.int32, (LANES, LANES), 1)
        acc = jnp.zeros((tr, LANES), F32)
        for j in range(W // LANES):
            cols = slice(j * LANES, (j + 1) * LANES)
            tot = _token_rows(b_refs[0], cols, tmp_ref)
            for r in b_refs[1:]:
                tot = tot + _token_rows(r, cols, tmp_ref)
            sel = jnp.where(col // HEAD_DIM + j * (LANES // HEAD_DIM) == lane, 1.0, 0.0).astype(F32)
            acc = acc + lax.dot_general(a_ref[:, cols].astype(F32) * tot, sel, (_NN, ((), ())),
                                        precision=lax.Precision.HIGHEST, preferred_element_type=F32)
        o_ref[...] = acc

    return pl.pallas_call(
        body, name=name, grid=(S // tr,), in_specs=[_row_spec(t, tr) for t in [a] + list(bs)],
        out_specs=pl.BlockSpec((tr, LANES), lambda i: (i, 0)),
        out_shape=jax.ShapeDtypeStruct((S, LANES), F32),
        scratch_shapes=[pltpu.VMEM((tr, LANES), F32)],
        compiler_params=_params("parallel"),
    )(a, *bs)


def _sum_cast_cols(groups, out_dtype, name, tail=None, tr=256):
    first = groups[0][0]
    S, W = (first.shape if first.ndim == 2 else (first.shape[0] * first.shape[1], first.shape[2]))
    tr = _fit(S, tr)
    flat = [t for g in groups for t in g] + ([tail] if tail is not None else [])
    tail_w = 0 if tail is None else tail.shape[1]

    def body(*refs):
        o_ref, tmp_ref = refs[-2], refs[-1]
        if tail is not None:
            o_ref[:, W * len(groups):] = refs[-3][...].astype(out_dtype)
        k = 0
        for gi, g in enumerate(groups):
            for j in range(W // LANES):
                cols = slice(j * LANES, (j + 1) * LANES)
                acc = _token_rows(refs[k], cols, tmp_ref)
                for r in refs[k + 1:k + len(g)]:
                    acc = acc + _token_rows(r, cols, tmp_ref)
                o_ref[:, gi * W + j * LANES:gi * W + (j + 1) * LANES] = acc.astype(out_dtype)
            k += len(g)

    return pl.pallas_call(
        body, name=name, grid=(S // tr,), in_specs=[_row_spec(t, tr) for t in flat],
        out_specs=pl.BlockSpec((tr, W * len(groups) + tail_w), lambda i: (i, 0)),
        out_shape=jax.ShapeDtypeStruct((S, W * len(groups) + tail_w), out_dtype),
        scratch_shapes=[pltpu.VMEM((tr, LANES), F32)],
        compiler_params=_params("parallel"),
    )(*flat)


FF_TILE = 256


def _ffn_up(h, w_gu, name, tm=1024):
    S, D = h.shape
    F2 = w_gu.shape[1]
    tm = _fit(S, tm)

    def body(h_ref, w_ref, gu_ref, act_ref):
        gu = _dot(h_ref[...], w_ref[...], _NN)
        gu_ref[...] = gu.astype(BF16)
        g, u = gu[:, :FF_TILE], gu[:, FF_TILE:]
        act_ref[...] = (g * (1.0 / (1.0 + jnp.exp(-g))) * u).astype(BF16)

    return pl.pallas_call(
        body, name=name, grid=(S // tm, F2 // (2 * FF_TILE)),
        in_specs=[pl.BlockSpec((tm, D), lambda i, j: (i, 0)), pl.BlockSpec((D, 2 * FF_TILE), lambda i, j: (0, j))],
        out_specs=[pl.BlockSpec((tm, 2 * FF_TILE), lambda i, j: (i, j)),
                   pl.BlockSpec((tm, FF_TILE), lambda i, j: (i, j))],
        out_shape=[jax.ShapeDtypeStruct((S, F2), BF16), jax.ShapeDtypeStruct((S, F2 // 2), BF16)],
        compiler_params=_params("parallel", "arbitrary"),
    )(h, w_gu)


def _ffn_dact(dy, w_down, gu, name, tm=1024):
    S, D = dy.shape
    F2 = gu.shape[1]
    tm = _fit(S, tm)

    def body(dy_ref, w_ref, gu_ref, dgu_ref):
        dact = _dot(dy_ref[...], w_ref[...], _NT)
        gu_v = gu_ref[...].astype(F32)
        g, u = gu_v[:, :FF_TILE], gu_v[:, FF_TILE:]
        sig = 1.0 / (1.0 + jnp.exp(-g))
        silu = g * sig
        dgu_ref[:, :FF_TILE] = (dact * u * (sig + silu * (1.0 - sig))).astype(BF16)
        dgu_ref[:, FF_TILE:] = (dact * silu).astype(BF16)

    return pl.pallas_call(
        body, name=name, grid=(S // tm, F2 // (2 * FF_TILE)),
        in_specs=[pl.BlockSpec((tm, D), lambda i, j: (i, 0)), pl.BlockSpec((FF_TILE, D), lambda i, j: (j, 0)),
                  pl.BlockSpec((tm, 2 * FF_TILE), lambda i, j: (i, j))],
        out_specs=pl.BlockSpec((tm, 2 * FF_TILE), lambda i, j: (i, j)),
        out_shape=jax.ShapeDtypeStruct((S, F2), BF16),
        compiler_params=_params("parallel", "arbitrary"),
    )(dy, w_down, gu)


def _fit_rows(n, cap):
    if n <= cap:
        return n
    t = (cap // 8) * 8
    while t >= 8:
        if n % t == 0:
            return t
        t -= 8
    raise ValueError(f"no sublane-aligned tile for {n} under {cap}")


def _add_n(arrs, name, tr=512):
    R, C = arrs[0].shape
    tr = _fit_rows(R, tr)

    def body(*refs):
        acc = refs[0][...]
        for r in refs[1:-1]:
            acc = acc + r[...]
        refs[-1][...] = acc

    row = pl.BlockSpec((tr, C), lambda i: (i, 0))
    return pl.pallas_call(
        body, name=name, grid=(R // tr,), in_specs=[row] * len(arrs), out_specs=row,
        out_shape=jax.ShapeDtypeStruct((R, C), F32), compiler_params=_params("parallel"),
    )(*arrs)


def _adamw(w, g, m, v, name, tr=512):
    R, C = w.shape
    tr = _fit_rows(R, tr)
    c1 = 1.0 / (1.0 - ADAM_B1 ** ADAM_STEP)
    c2 = 1.0 / (1.0 - ADAM_B2 ** ADAM_STEP)

    def body(w_ref, g_ref, m_ref, v_ref, d_ref, nm_ref, nv_ref):
        gv = g_ref[...]
        nm = ADAM_B1 * m_ref[...] + (1.0 - ADAM_B1) * gv
        nv = ADAM_B2 * v_ref[...] + (1.0 - ADAM_B2) * (gv * gv)
        nm_ref[...] = nm
        nv_ref[...] = nv
        d_ref[...] = -ADAM_LR * ((nm * c1) / (jnp.sqrt(nv * c2) + ADAM_EPS) + ADAM_WD * w_ref[...])

    row = pl.BlockSpec((tr, C), lambda i: (i, 0))
    return pl.pallas_call(
        body, name=name, grid=(R // tr,), in_specs=[row] * 4, out_specs=[row] * 3,
        out_shape=[jax.ShapeDtypeStruct((R, C), F32)] * 3, compiler_params=_params("parallel"),
    )(w, g, m, v)


def _place():
    return lax.axis_index("x"), lax.axis_index("y"), lax.axis_index("c")


_ANY = pl.BlockSpec(memory_space=pl.ANY)


def _chip_all_gather(shard, name):
    R, C = shard.shape
    half = R // 2

    def body(x_ref, out_ref, send_sems, recv_sems, local_sem):
        x, y, c = _place()
        chips = [(1 - x, y), (x, 1 - y), (1 - x, 1 - y)]
        sibling = (x, y, 1 - c)
        mine = pltpu.make_async_copy(x_ref, out_ref.at[2 * x + y], local_sem)
        mine.start()

        def rows(chip, core):
            return out_ref.at[chip, pl.ds(core * half, half)]

        def copy(k, chip, core, to, src=None):
            return pltpu.make_async_remote_copy(
                src_ref=rows(chip, core) if src is None else src, dst_ref=rows(chip, core),
                send_sem=send_sems.at[k], recv_sem=recv_sems.at[k], device_id=to, device_id_type=MESH_IDS)

        me = 2 * x + y
        first = [copy(k, me, c, (cx, cy, c), src=x_ref.at[pl.ds(c * half, half)]) for k, (cx, cy) in enumerate(chips)]
        for cp in first:
            cp.start()
        passed = [copy(3 + k, 2 * cx + cy, c, sibling) for k, (cx, cy) in enumerate(chips)]
        for k, (cx, cy) in enumerate(chips):
            copy(k, 2 * cx + cy, c, (cx, cy, c)).wait_recv()
            passed[k].start()
        for k, (cx, cy) in enumerate(chips):
            copy(3 + k, 2 * cx + cy, 1 - c, sibling).wait_recv()
        for cp in first + passed:
            cp.wait_send()
        mine.wait()

    return pl.pallas_call(
        body, name=name, in_specs=[_ANY], out_specs=_ANY,
        out_shape=jax.ShapeDtypeStruct((N_CHIPS, R, C), shard.dtype),
        scratch_shapes=[pltpu.SemaphoreType.DMA((6,)), pltpu.SemaphoreType.DMA((6,)), pltpu.SemaphoreType.DMA],
    )(shard)


def _sibling_exchange(buf, name):
    def body(x_ref, out_ref, send_sem, recv_sem):
        x, y, c = _place()
        cp = pltpu.make_async_remote_copy(
            src_ref=x_ref, dst_ref=out_ref, send_sem=send_sem, recv_sem=recv_sem,
            device_id=(x, y, 1 - c), device_id_type=MESH_IDS)
        cp.start()
        cp.wait()

    return pl.pallas_call(
        body, name=name, in_specs=[_ANY], out_specs=_ANY,
        out_shape=jax.ShapeDtypeStruct(buf.shape, buf.dtype),
        scratch_shapes=[pltpu.SemaphoreType.DMA, pltpu.SemaphoreType.DMA],
    )(buf)


def _chip_scatter(parts, name):
    _, R, C = parts.shape

    def body(p_ref, out_ref, send_sems, recv_sems):
        x, y, c = _place()
        chips = [(1 - x, y), (x, 1 - y), (1 - x, 1 - y)]

        def copy(k, slab, to):
            return pltpu.make_async_remote_copy(
                src_ref=p_ref.at[slab], dst_ref=out_ref.at[k], send_sem=send_sems.at[k], recv_sem=recv_sems.at[k],
                device_id=to, device_id_type=MESH_IDS)

        sends = [copy(k, 2 * cx + cy, (cx, cy, c)) for k, (cx, cy) in enumerate(chips)]
        for cp in sends:
            cp.start()
        for cp in sends:
            cp.wait_recv()
        for cp in sends:
            cp.wait_send()

    return pl.pallas_call(
        body, name=name, in_specs=[_ANY], out_specs=_ANY,
        out_shape=jax.ShapeDtypeStruct((3, R, C), parts.dtype),
        scratch_shapes=[pltpu.SemaphoreType.DMA((3,)), pltpu.SemaphoreType.DMA((3,))],
    )(parts)


def _all_to_all_small(vec, name):
    R, C = vec.shape

    def body(v_ref, out_ref, send_sems, recv_sems, local_sem):
        x, y, c = _place()
        me = 4 * x + 2 * y + c
        mine = pltpu.make_async_copy(v_ref, out_ref.at[me], local_sem)
        mine.start()
        flips = [(dx, dy, dc) for dx in (0, 1) for dy in (0, 1) for dc in (0, 1)][1:]

        def peer(f):
            return (x ^ f[0], y ^ f[1], c ^ f[2])

        def copy(k, slot, to):
            return pltpu.make_async_remote_copy(
                src_ref=v_ref, dst_ref=out_ref.at[slot], send_sem=send_sems.at[k], recv_sem=recv_sems.at[k],
                device_id=to, device_id_type=MESH_IDS)

        sends = [copy(k, me, peer(f)) for k, f in enumerate(flips)]
        for cp in sends:
            cp.start()
        for k, f in enumerate(flips):
            px, py, pc = peer(f)
            copy(k, 4 * px + 2 * py + pc, peer(f)).wait_recv()
        for cp in sends:
            cp.wait_send()
        mine.wait()

    return pl.pallas_call(
        body, name=name, in_specs=[_ANY], out_specs=_ANY,
        out_shape=jax.ShapeDtypeStruct((8, R, C), vec.dtype),
        scratch_shapes=[pltpu.SemaphoreType.DMA((7,)), pltpu.SemaphoreType.DMA((7,)), pltpu.SemaphoreType.DMA],
    )(vec)


def _to_heads(t, n):
    S = t.shape[0]
    return t.reshape(S, n, HEAD_DIM).transpose(1, 0, 2)


def _to_heads_t(t, n):
    S = t.shape[0]
    return t.T.reshape(n, HEAD_DIM, S)


def _from_heads_t(t):
    H, Dh, S = t.shape
    return t.reshape(H * Dh, S).T


def _rep(t):
    return jnp.broadcast_to(t[..., None], t.shape + (LANES,))


def _t5_bucket(dist):
    max_exact = N_BUCKETS // 2
    d = np.maximum(dist, 1).astype(np.float32)
    large = max_exact + (np.log(d / max_exact) / np.log(MAX_DISTANCE / max_exact)
                         * (N_BUCKETS - max_exact)).astype(np.int32)
    large = np.minimum(large, N_BUCKETS - 1)
    return np.where(dist < max_exact, dist, large).astype(np.int32)


def _band_tables():
    qi = np.arange(BAND)[:, None]
    kj = np.arange(2 * BAND)[None, :]
    sub = qi + BAND - kj
    band = (sub >= 0) & (sub <= BAND)
    out = []
    for d in DILATIONS:
        bucket = _t5_bucket(np.clip(sub, 0, BAND) * d)
        out.append(np.where(band, bucket, -1).astype(np.int32))
    return np.stack(out)


_PACK = (("w_in", 770), ("w_out", 256), ("w_xq", 64), ("w_xk", 64), ("w_xv", 64), ("w_xo", 64),
         ("w_gate", 704), ("w_up", 704), ("w_down", 704))


def _pack(shards):
    rows = [shards[n].reshape(-1, PACK_COLS) for n, _ in _PACK]
    total = sum(r.shape[0] for r in rows)
    pad = (-total) % 128
    if pad:
        rows.append(jnp.zeros((pad, PACK_COLS), rows[0].dtype))
    return jnp.concatenate(rows, axis=0)


def _unpack(pack, shapes):
    out, r = {}, 0
    for n, _ in _PACK:
        cnt = int(np.prod(shapes[n])) // PACK_COLS
        out[n] = pack[r:r + cnt].reshape(shapes[n])
        r += cnt
    return out


_COL_SHARDED = ("w_in", "w_xo", "w_gate", "w_up")


def _full_weight(gathered, name):
    return jnp.concatenate(gathered, axis=1 if name in _COL_SHARDED else 0)


def _split_weight(full, name):
    return jnp.split(full, N_CHIPS, axis=1 if name in _COL_SHARDED else 0)


_SMALL = ("g_mix_pre", "g_mix_post", "g_xattn_pre", "g_mem", "g_xattn_post", "g_ffn_pre", "g_ffn_post")


def _pack_small(vals):
    D = vals["g_mix_pre"].shape[1]
    rows = [vals[n].reshape(1, D) for n in _SMALL]
    misc = jnp.concatenate([vals["b_f"].reshape(-1), vals["rel_bias"].reshape(-1)])
    rows.append(jnp.pad(misc, (0, D - misc.shape[0])).reshape(1, D))
    rows.append(jnp.zeros((16 - len(rows), D), F32))
    return jnp.concatenate(rows, axis=0)


def _unpack_small(pack):
    out = {n: pack[i:i + 1] for i, n in enumerate(_SMALL)}
    out["b_f"] = pack[7, 0:N_FOX_HEADS].reshape(1, N_FOX_HEADS)
    out["rel_bias"] = pack[7, N_FOX_HEADS:N_FOX_HEADS + N_BUCKETS * N_DIL_HEADS].reshape(N_BUCKETS, N_DIL_HEADS)
    return out


def kernel(x, mem, g_mix_pre, w_in, b_f, rel_bias, w_out, g_mix_post, g_xattn_pre, g_mem, w_xq, w_xk, w_xv, w_xo, g_xattn_post, g_ffn_pre, w_gate, w_up, w_down, g_ffn_post, loss_target, m_g_mix_pre, m_w_in, m_b_f, m_rel_bias, m_w_out, m_g_mix_post, m_g_xattn_pre, m_g_mem, m_w_xq, m_w_xk, m_w_xv, m_w_xo, m_g_xattn_post, m_g_ffn_pre, m_w_gate, m_w_up, m_w_down, m_g_ffn_post, v_g_mix_pre, v_w_in, v_b_f, v_rel_bias, v_w_out, v_g_mix_post, v_g_xattn_pre, v_g_mem, v_w_xq, v_w_xk, v_w_xv, v_w_xo, v_g_xattn_post, v_g_ffn_pre, v_w_gate, v_w_up, v_w_down, v_g_ffn_post):
    args = dict(locals())
    big = [n for n, _ in _PACK]
    names = ["g_mix_pre", "w_in", "b_f", "rel_bias", "w_out", "g_mix_post", "g_xattn_pre", "g_mem", "w_xq",
             "w_xk", "w_xv", "w_xo", "g_xattn_post", "g_ffn_pre", "w_gate", "w_up", "w_down", "g_ffn_post"]
    xs = x[0]
    S, D = xs.shape
    assert S % (BAND * DILATIONS[-1]) == 0
    shard_shapes = {n: args[n].shape[1:] for n in big}
    my_x, my_y, my_c = lax.axis_index("x"), lax.axis_index("y"), lax.axis_index("c")

    gathered = _chip_all_gather(_pack({n: args[n][0].astype(BF16) for n in big}), "weights_all_gather")
    per_chip = [_unpack(gathered[j], shard_shapes) for j in range(N_CHIPS)]
    W = {n: _full_weight([pc[n] for pc in per_chip], n) for n in big}
    w_fox, w_fg, w_dil = (W["w_in"][:, :3 * FOX_WIDTH], W["w_in"][:, 3 * FOX_WIDTH:3 * FOX_WIDTH + N_FOX_HEADS],
                          W["w_in"][:, 3 * FOX_WIDTH + N_FOX_HEADS:])
    w_qkv = jnp.concatenate([w_fox, w_dil], axis=1)
    w_fg_pad = jnp.pad(w_fg, ((0, 0), (0, LANES - N_FOX_HEADS)))
    F = W["w_gate"].shape[1]
    nft = F // FF_TILE
    w_gu = jnp.stack([W["w_gate"].reshape(D, nft, FF_TILE), W["w_up"].reshape(D, nft, FF_TILE)],
                     axis=2).reshape(D, 2 * F)

    h1 = _rms_fwd(xs, g_mix_pre, "rms_mix_pre")
    qkv = _mm(h1, w_qkv, "nn", BF16, "proj_qkv")
    fg = _mm(h1, w_fg_pad, "nn", F32, "proj_gate")
    fg_t = fg[:, :N_FOX_HEADS].T
    b_col = b_f.reshape(N_FOX_HEADS, 1)
    c_t = _forget_fwd(fg_t, b_col, "forget_cumsum")
    c_row = c_t.reshape(N_FOX_HEADS, 1, S)
    c_rep = _rep(c_t)
    fq_s, fk_s, fv_s = (qkv[:, i * FOX_WIDTH:(i + 1) * FOX_WIDTH] for i in range(3))
    fqt, fkt, fvt = (_to_heads_t(t, N_FOX_HEADS) for t in (fq_s, fk_s, fv_s))
    fk, fv = _to_heads(fk_s, N_FOX_HEADS), _to_heads(fv_s, N_FOX_HEADS)
    o_fox_t, lse_fox = _fox_fwd(fqt, fk, _with_ones(fvt), c_row, c_rep, "fox_fwd")

    bucket_map = _band_tables()
    onehot = (jnp.asarray(bucket_map)[..., None] == jnp.arange(N_BUCKETS)).astype(F32)
    bias_tab = jnp.einsum("pqkb,bh->phkq", onehot, rel_bias, precision=lax.Precision.HIGHEST)
    bias_tab = jnp.where(jnp.asarray(bucket_map.transpose(0, 2, 1) >= 0)[:, None], bias_tab, NEG)
    bias_t = bias_tab.reshape(3, HEAD_PAIRS, 2, 2 * BAND, BAND).transpose(0, 1, 3, 2, 4).reshape(
        3, HEAD_PAIRS, 2 * BAND, 2 * BAND)
    views = [(qkv.reshape(1, S, qkv.shape[1]), DIL_Q_BLOCK)] + [
        (_to_residues(qkv, 1, 3 * DIL_WIDTH, d, f"dilated_qkv_residues_{d}"), 0) for d in DILATIONS[1:]]

    def to_tok(stat, d):
        return stat.reshape(N_DIL_HEADS, d, S // d).swapaxes(1, 2).reshape(N_DIL_HEADS, S)

    def to_perm(stat, d):
        return stat.reshape(N_DIL_HEADS, S // d, d).swapaxes(1, 2).reshape(HEAD_PAIRS, 2, S)

    def tok_or_res(t):
        return t.reshape(t.shape[1:]) if t.shape[0] == 1 else t

    lse_tok = jnp.stack([to_tok(_dil_lse(*views[p], bias_t[p], f"dilated_lse_{d}"), d)
                         for p, d in enumerate(DILATIONS)])
    lse_joint = _lse_join(lse_tok, "dilated_lse_join")
    lse_perm = [to_perm(lse_joint, d) for d in DILATIONS]
    o_dil = [tok_or_res(_dil_out(*views[p], bias_t[p], lse_perm[p], f"dilated_out_{d}"))
             for p, d in enumerate(DILATIONS)]
    o_cat = _sum_cast_cols([[_from_heads_t(o_fox_t)]] + [[o] for o in o_dil], BF16, "mixer_out_cat")
    w_out_b = W["w_out"]
    w_out_cat = jnp.concatenate([w_out_b[:FOX_WIDTH]] + [w_out_b[FOX_WIDTH:]] * 3, axis=0)
    a = _mm(o_cat, w_out_cat, "nn", F32, "proj_out", tk=2048)
    x1, h2 = _resid_norm(xs, a, g_mix_post, g_xattn_pre, "resid_mix")

    hm = _rms_fwd(mem[0], g_mem, "rms_mem")
    q2 = _mm(h2, W["w_xq"], "nn", BF16, "xattn_q")
    w_xkv = jnp.concatenate([W["w_xk"], W["w_xv"]], axis=1)
    kvm = _mm(hm, w_xkv, "nn", BF16, "xattn_kv")
    MW = N_MEM_HEADS * HEAD_DIM
    oc, lse_mem = _mem_fwd(q2, kvm, "xattn_fwd")
    y2 = _mm(oc, W["w_xo"], "nn", F32, "xattn_o")
    x2, h3 = _resid_norm(x1, y2, g_xattn_post, g_ffn_pre, "resid_xattn")

    gu, act = _ffn_up(h3, w_gu, "ffn_up")
    y3 = _mm(act, W["w_down"], "nn", F32, "ffn_down", tk=2816)
    dx3, loss_tile = _final_loss(x2, y3, g_ffn_post, loss_target[0], "final_loss")

    grads = {}
    small = {}
    _, dy3_b, dg = _rms_bwd(y3, g_ffn_post, dx3, None, "bwd_norm_ffn_post", want=("bf16",))
    small["g_ffn_post"] = dg[0:1]
    grads["w_down"] = _mm(act, dy3_b, "tn", F32, "grad_w_down", tm=1408)
    dgu = _ffn_dact(dy3_b, W["w_down"], gu, "ffn_dact")
    dw_gu = _mm(h3, dgu, "tn", F32, "grad_w_gu", tn=1408).reshape(D, nft, 2, FF_TILE)
    grads["w_gate"], grads["w_up"] = dw_gu[:, :, 0].reshape(D, F), dw_gu[:, :, 1].reshape(D, F)
    dh3 = _mm(dgu, w_gu, "nt", F32, "bwd_ffn_in", tk=1408)
    dx2, _, dg = _rms_bwd(x2, g_ffn_pre, dh3, dx3, "bwd_norm_ffn_pre", want=("f32",))
    small["g_ffn_pre"] = dg[0:1]

    _, dy2_b, dg = _rms_bwd(y2, g_xattn_post, dx2, None, "bwd_norm_xattn_post", want=("bf16",))
    small["g_xattn_post"] = dg[0:1]
    grads["w_xo"] = _mm(oc, dy2_b, "tn", F32, "grad_w_xo")
    doc = _mm(dy2_b, W["w_xo"], "nt", BF16, "bwd_xattn_o")
    delta_mem = _head_rowdot(doc, [oc], "xattn_delta")[:, :N_MEM_HEADS].T.reshape(N_MEM_HEADS // 2, 2, S)
    dq2, dkm, dvm = _mem_bwd(q2, kvm, doc, lse_mem, delta_mem, "xattn_bwd")
    dkvm = jnp.concatenate([dkm, dvm], axis=1).astype(BF16)
    grads["w_xq"] = _mm(h2, dq2, "tn", F32, "grad_w_xq")
    dw_xkv = _mm(hm, dkvm, "tn", F32, "grad_w_xkv")
    grads["w_xk"], grads["w_xv"] = dw_xkv[:, :MW], dw_xkv[:, MW:]
    dhm = _mm(dkvm, w_xkv, "nt", F32, "bwd_xattn_kv")
    _, _, dg = _rms_bwd(mem[0], g_mem, dhm, None, "bwd_norm_mem", want=())
    small["g_mem"] = dg[0:1]
    dh2 = _mm(dq2, W["w_xq"], "nt", F32, "bwd_xattn_q")
    dx1, _, dg = _rms_bwd(x1, g_xattn_pre, dh2, dx2, "bwd_norm_xattn_pre", want=("f32",))
    small["g_xattn_pre"] = dg[0:1]

    _, da_b, dg = _rms_bwd(a, g_mix_post, dx1, None, "bwd_norm_mix_post", want=("bf16",))
    small["g_mix_post"] = dg[0:1]
    dw_out_cat = _mm(o_cat, da_b, "tn", F32, "grad_w_out")
    dw_out_dil = _add_n([dw_out_cat[FOX_WIDTH + p * DIL_WIDTH:FOX_WIDTH + (p + 1) * DIL_WIDTH] for p in range(3)],
                        "grad_w_out_dil")
    grads["w_out"] = jnp.concatenate([dw_out_cat[:FOX_WIDTH], dw_out_dil], axis=0)
    do = _mm(da_b, w_out_b, "nt", BF16, "bwd_proj_out")
    do_fox, do_dil = do[:, :FOX_WIDTH], do[:, FOX_WIDTH:]

    delta_fox = _head_rowdot(do_fox, [o_cat[:, :FOX_WIDTH]], "fox_delta")[:, :N_FOX_HEADS].T
    dq_aug, dk_aug, dvf = _fox_bwd(_with_ones(fqt), fk, _with_ones(fkt), fv, _to_heads_t(do_fox, N_FOX_HEADS), c_rep,
                                   lse_fox, delta_fox.reshape(N_FOX_HEADS, 1, S), "fox_bwd")
    dqf, dkf = dq_aug[:, :HEAD_DIM], dk_aug[:, :HEAD_DIM]
    dfg_t, db_f = _forget_bwd(fg_t, b_col, dq_aug[:, HEAD_DIM], dk_aug[:, HEAD_DIM], "forget_bwd")

    delta_dil = _head_rowdot(do_dil, o_dil, "dilated_delta")[:, :N_DIL_HEADS].T
    do_res = [do_dil.reshape(1, S, DIL_WIDTH)] + [
        _to_residues(do, 1, DIL_WIDTH, d, f"dilated_do_residues_{d}") for d in DILATIONS[1:]]
    dil_grads = [_dil_bwd(*views[p], do_res[p], bias_t[p], lse_perm[p], to_perm(delta_dil, d), f"dilated_bwd_{d}")
                 for p, d in enumerate(DILATIONS)]
    dbias_t = jnp.stack([g[3].reshape(HEAD_PAIRS, 2 * BAND, 2, BAND).transpose(0, 2, 1, 3).reshape(
        N_DIL_HEADS, 2 * BAND, BAND) for g in dil_grads])
    d_rel = _bucket_reduce(dbias_t, jnp.asarray(bucket_map.transpose(0, 2, 1)), "rel_bias_grad")[:, :N_DIL_HEADS]
    dfg_pad = jnp.pad(dfg_t.T, ((0, 0), (0, LANES - N_FOX_HEADS))).astype(BF16)
    dcat = _sum_cast_cols([[_from_heads_t(dqf)], [_from_heads_t(dkf)], [_from_heads_t(dvf)]]
                          + [[tok_or_res(g[j]) for g in dil_grads] for j in range(3)],
                          BF16, "dqkv_assemble", tail=dfg_pad)
    dw_cat = _mm(h1, dcat, "tn", F32, "grad_w_qkv", tm=512, tn=3200)
    n_qkv = 3 * (FOX_WIDTH + DIL_WIDTH)
    grads["w_in"] = jnp.concatenate([dw_cat[:, :3 * FOX_WIDTH], dw_cat[:, n_qkv:n_qkv + N_FOX_HEADS],
                                     dw_cat[:, 3 * FOX_WIDTH:n_qkv]], axis=1)
    w_cat = jnp.concatenate([w_qkv, w_fg_pad], axis=1)
    dh1 = _mm(dcat, w_cat, "nt", F32, "bwd_proj_in", tk=3200)
    grad_x, _, dg = _rms_bwd(xs, g_mix_pre, dh1, dx1, "bwd_norm_mix_pre", want=("f32",))
    small["g_mix_pre"] = dg[0:1]
    small["b_f"] = db_f[:, 0].reshape(1, N_FOX_HEADS)
    small["rel_bias"] = d_rel

    split = {n: _split_weight(grads[n], n) for n in big}
    parts = jnp.stack([_pack({n: split[n][j] for n in big}) for j in range(N_CHIPS)])
    R = parts.shape[1]
    half = R // 2
    keep = lax.dynamic_slice_in_dim(parts, my_c * half, half, axis=1)
    give = lax.dynamic_slice_in_dim(parts, (1 - my_c) * half, half, axis=1)
    got = _sibling_exchange(give, "grads_to_sibling")
    chip_sum = _add_n([keep.reshape(-1, PACK_COLS), got.reshape(-1, PACK_COLS)], "grads_add_sibling")
    chip_sum = chip_sum.reshape(N_CHIPS, half, PACK_COLS)
    my_chip = 2 * my_x + my_y
    from_chips = _chip_scatter(chip_sum.astype(BF16), "grads_to_chips")
    own = lax.dynamic_index_in_dim(chip_sum, my_chip, axis=0, keepdims=False)
    g_half = _add_n([own, from_chips[0], from_chips[1], from_chips[2]], "grads_add_chips")
    other_half = _sibling_exchange(g_half, "grads_share_sibling")
    g_pack = jnp.where(my_c == 0, jnp.concatenate([g_half, other_half]), jnp.concatenate([other_half, g_half]))

    small_pack = _pack_small(small)
    small_pack = small_pack.at[8, 0].set(loss_tile[0, 0])
    everyone = _all_to_all_small(small_pack, "small_all_gather")
    small_sum = _add_n([everyone[i] for i in range(8)], "small_sum")
    loss = small_sum[8, 0]
    g_small = _unpack_small(small_sum)

    outs = {"grad": _unpack(g_pack, shard_shapes), "delta": {}, "new_m": {}, "new_v": {}}
    for n in big:
        outs["delta"][n], outs["new_m"][n], outs["new_v"][n] = _adamw(
            args[n][0], outs["grad"][n], args["m_" + n][0], args["v_" + n][0], f"adamw_{n}")
    sw = _pack_small({n: args[n] for n in _SMALL + ("b_f", "rel_bias")})
    sm = _pack_small({n: args["m_" + n] for n in _SMALL + ("b_f", "rel_bias")})
    sv = _pack_small({n: args["v_" + n] for n in _SMALL + ("b_f", "rel_bias")})
    sd, snm, snv = _adamw(sw, small_sum.at[8, 0].set(0.0), sm, sv, "adamw_small")
    souts = {"grad": g_small, "delta": _unpack_small(sd), "new_m": _unpack_small(snm), "new_v": _unpack_small(snv)}

    def leaf(kind, n):
        if n in souts[kind]:
            return souts[kind][n].reshape(args[n].shape)
        return outs[kind][n].reshape(args[n].shape)

    result = [loss, grad_x.reshape(x.shape)]
    for kind in ("grad", "delta", "new_m", "new_v"):
        result += [leaf(kind, n) for n in names]
    return tuple(result)
```

```python
import numpy as np
import jax
import jax.numpy as jnp
from jax import lax
from jax.experimental import pallas as pl
from jax.experimental.pallas import tpu as pltpu

F32 = jnp.float32
BF16 = jnp.bfloat16
MESH_IDS = pl.DeviceIdType.MESH

LANES = 128
HEAD_DIM = 64
N_FOX_HEADS = 8
N_DIL_HEADS = 8
N_MEM_HEADS = 4
FOX_WIDTH = N_FOX_HEADS * HEAD_DIM
DIL_WIDTH = N_DIL_HEADS * HEAD_DIM
DILATIONS = (1, 4, 16)
BAND = 128
BAND_CHUNK_MAX = 8 * BAND
N_BUCKETS = 32
MAX_DISTANCE = 2048
QK_SCALE = HEAD_DIM ** -0.5
RMS_EPS = 1e-6
NEG = -1e30
VMEM_LIMIT = 56 << 20

ADAM_LR = 0.001
ADAM_B1 = 0.9
ADAM_B2 = 0.999
ADAM_EPS = 1e-08
ADAM_WD = 0.01
ADAM_STEP = 10

N_CHIPS = 4
PACK_COLS = 1024


def _params(*sem):
    return pltpu.CompilerParams(dimension_semantics=sem, vmem_limit_bytes=VMEM_LIMIT)


def _fit(n, cap):
    if n <= cap:
        return n
    t = (cap // LANES) * LANES
    while t >= LANES:
        if n % t == 0:
            return t
        t -= LANES
    raise ValueError(f"no lane-aligned tile for {n} under {cap}")


def _dot(a, b, dims):
    return lax.dot_general(a, b, (dims, ((), ())), preferred_element_type=F32)


_NN = ((1,), (0,))
_NT = ((1,), (1,))
_TN = ((0,), (0,))


def _mm(a, b, mode, out_dtype, name, tm=1024, tn=1024, tk=1024):
    if mode == "nn":
        (M, K), N = a.shape, b.shape[1]
    elif mode == "nt":
        (M, K), N = a.shape, b.shape[0]
    else:
        (K, M), N = a.shape, b.shape[1]
    tm, tn, tk = _fit(M, tm), _fit(N, tn), _fit(K, tk)
    nk = K // tk
    if mode == "tn":
        a_spec = pl.BlockSpec((tk, tm), lambda i, j, k: (k, i))
    else:
        a_spec = pl.BlockSpec((tm, tk), lambda i, j, k: (i, k))
    if mode == "nt":
        b_spec = pl.BlockSpec((tn, tk), lambda i, j, k: (j, k))
    else:
        b_spec = pl.BlockSpec((tk, tn), lambda i, j, k: (k, j))
    dims = {"nn": _NN, "nt": _NT, "tn": _TN}[mode]

    def body(a_ref, b_ref, o_ref, *acc):
        prod = _dot(a_ref[...].astype(BF16), b_ref[...].astype(BF16), dims)
        if nk == 1:
            o_ref[...] = prod.astype(o_ref.dtype)
            return
        acc_ref, k = acc[0], pl.program_id(2)

        @pl.when(k == 0)
        def _():
            acc_ref[...] = prod

        @pl.when(k > 0)
        def _():
            acc_ref[...] += prod

        @pl.when(k == nk - 1)
        def _():
            o_ref[...] = acc_ref[...].astype(o_ref.dtype)

    return pl.pallas_call(
        body, name=name, grid=(M // tm, N // tn, nk),
        in_specs=[a_spec, b_spec],
        out_specs=pl.BlockSpec((tm, tn), lambda i, j, k: (i, j)),
        out_shape=jax.ShapeDtypeStruct((M, N), out_dtype),
        scratch_shapes=[pltpu.VMEM((tm, tn), F32)] if nk > 1 else [],
        compiler_params=_params("parallel", "parallel", "arbitrary"),
    )(a, b)


def _rms_rows(x):
    return lax.rsqrt(jnp.mean(x * x, axis=-1, keepdims=True) + RMS_EPS)


def _rms_fwd(x, g, name, tr=512):
    S, D = x.shape
    tr = _fit(S, tr)

    def body(x_ref, g_ref, h_ref):
        xv = x_ref[...]
        h_ref[...] = (xv * _rms_rows(xv) * g_ref[...]).astype(BF16)

    return pl.pallas_call(
        body, name=name, grid=(S // tr,),
        in_specs=[pl.BlockSpec((tr, D), lambda i: (i, 0)), pl.BlockSpec((1, D), lambda i: (0, 0))],
        out_specs=pl.BlockSpec((tr, D), lambda i: (i, 0)),
        out_shape=jax.ShapeDtypeStruct((S, D), BF16),
        compiler_params=_params("parallel"),
    )(x, g)


def _resid_norm(xres, y, g_post, g_next, name, tr=512):
    S, D = xres.shape
    tr = _fit(S, tr)

    def body(x_ref, y_ref, gp_ref, gn_ref, xn_ref, h_ref):
        yv = y_ref[...]
        xn = x_ref[...] + yv * _rms_rows(yv) * gp_ref[...]
        xn_ref[...] = xn
        h_ref[...] = (xn * _rms_rows(xn) * gn_ref[...]).astype(BF16)

    row = pl.BlockSpec((tr, D), lambda i: (i, 0))
    vec = pl.BlockSpec((1, D), lambda i: (0, 0))
    return pl.pallas_call(
        body, name=name, grid=(S // tr,),
        in_specs=[row, row, vec, vec], out_specs=[row, row],
        out_shape=[jax.ShapeDtypeStruct((S, D), F32), jax.ShapeDtypeStruct((S, D), BF16)],
        compiler_params=_params("parallel"),
    )(xres, y, g_post, g_next)


def _final_loss(xres, y, g_post, target, name, tr=512):
    S, D = xres.shape
    tr = _fit(S, tr)

    def body(x_ref, y_ref, gp_ref, t_ref, d_ref, loss_ref):
        i = pl.program_id(0)
        yv = y_ref[...]
        err = x_ref[...] + yv * _rms_rows(yv) * gp_ref[...] - t_ref[...]
        d_ref[...] = err * (1.0 / D)

        @pl.when(i == 0)
        def _():
            loss_ref[...] = jnp.zeros_like(loss_ref)

        part = jnp.sum(jnp.sum(err * err, axis=1, keepdims=True), axis=0, keepdims=True)
        loss_ref[...] += jnp.broadcast_to(part * (0.5 / D), loss_ref.shape)

    row = pl.BlockSpec((tr, D), lambda i: (i, 0))
    vec = pl.BlockSpec((1, D), lambda i: (0, 0))
    return pl.pallas_call(
        body, name=name, grid=(S // tr,),
        in_specs=[row, row, vec, row],
        out_specs=[row, pl.BlockSpec((8, LANES), lambda i: (0, 0))],
        out_shape=[jax.ShapeDtypeStruct((S, D), F32), jax.ShapeDtypeStruct((8, LANES), F32)],
        compiler_params=_params("arbitrary"),
    )(xres, y, g_post, target)


def _rms_bwd(xin, g, dy, dres, name, want=("f32", "bf16"), tr=512):
    S, D = xin.shape
    tr = _fit(S, tr)
    has_res = dres is not None

    def body(*refs):
        refs = list(refs)
        dg_ref = refs.pop()
        dxb_ref = refs.pop() if "bf16" in want else None
        dx_ref = refs.pop() if "f32" in want else None
        dr_ref = refs.pop() if has_res else None
        x_ref, g_ref, dy_ref = refs
        i = pl.program_id(0)
        xv = x_ref[...]
        dyv = dy_ref[...].astype(F32)
        xhat = xv * _rms_rows(xv)
        dxhat = dyv * g_ref[...]
        r = _rms_rows(xv)
        dx = r * (dxhat - xhat * jnp.mean(dxhat * xhat, axis=-1, keepdims=True))
        if has_res:
            dx = dx + dr_ref[...]
        if dx_ref is not None:
            dx_ref[...] = dx
        if dxb_ref is not None:
            dxb_ref[...] = dx.astype(BF16)

        @pl.when(i == 0)
        def _():
            dg_ref[...] = jnp.zeros_like(dg_ref)

        dg_ref[...] += jnp.broadcast_to(jnp.sum(dyv * xhat, axis=0, keepdims=True), dg_ref.shape)

    row = pl.BlockSpec((tr, D), lambda i: (i, 0))
    vec = pl.BlockSpec((1, D), lambda i: (0, 0))
    acc = pl.BlockSpec((8, D), lambda i: (0, 0))
    ins = [xin, g, dy] + ([dres] if has_res else [])
    dtypes = [dt for key, dt in (("f32", F32), ("bf16", BF16)) if key in want]
    outs = pl.pallas_call(
        body, name=name, grid=(S // tr,),
        in_specs=[row, vec, row] + ([row] if has_res else []),
        out_specs=[row] * len(dtypes) + [acc],
        out_shape=[jax.ShapeDtypeStruct((S, D), dt) for dt in dtypes] + [jax.ShapeDtypeStruct((8, D), F32)],
        compiler_params=_params("arbitrary"),
    )(*ins)
    by_key = dict(zip([key for key in ("f32", "bf16") if key in want], outs[:-1]))
    return by_key.get("f32"), by_key.get("bf16"), outs[-1]


def _tri(n, upper):
    r = lax.broadcasted_iota(jnp.int32, (n, n), 0)
    c = lax.broadcasted_iota(jnp.int32, (n, n), 1)
    return jnp.where((r <= c) if upper else (r >= c), 1.0, 0.0).astype(F32)


def _forget_fwd(fg_t, b_col, name, ts=512):
    H, S = fg_t.shape
    ts = _fit(S, ts)

    def body(f_ref, b_ref, c_ref, carry_ref):
        i = pl.program_id(0)

        @pl.when(i == 0)
        def _():
            carry_ref[...] = jnp.zeros_like(carry_ref)

        z = f_ref[...] + b_ref[...]
        logf = jnp.minimum(z, 0.0) - jnp.log(1.0 + jnp.exp(-jnp.abs(z)))
        run = lax.dot_general(logf, _tri(ts, True), (_NN, ((), ())), precision=lax.Precision.HIGHEST,
                              preferred_element_type=F32) + carry_ref[:, 0:1]
        c_ref[...] = run
        carry_ref[...] = jnp.broadcast_to(
            carry_ref[:, 0:1] + jnp.sum(logf, axis=1, keepdims=True), carry_ref.shape)

    return pl.pallas_call(
        body, name=name, grid=(S // ts,),
        in_specs=[pl.BlockSpec((H, ts), lambda i: (0, i)), pl.BlockSpec((H, 1), lambda i: (0, 0))],
        out_specs=pl.BlockSpec((H, ts), lambda i: (0, i)),
        out_shape=jax.ShapeDtypeStruct((H, S), F32),
        scratch_shapes=[pltpu.VMEM((H, LANES), F32)],
        compiler_params=_params("arbitrary"),
    )(fg_t, b_col)


def _forget_bwd(fg_t, b_col, dc_plus, dc_minus, name, ts=512):
    H, S = fg_t.shape
    ts = _fit(S, ts)
    nb = S // ts

    def body(f_ref, b_ref, dcp_ref, dcm_ref, df_ref, db_ref, carry_ref):
        i = pl.program_id(0)

        @pl.when(i == 0)
        def _():
            carry_ref[...] = jnp.zeros_like(carry_ref)
            db_ref[...] = jnp.zeros_like(db_ref)

        dc = dcp_ref[...] - dcm_ref[...]
        suffix = lax.dot_general(dc, _tri(ts, False), (_NN, ((), ())), precision=lax.Precision.HIGHEST,
                                 preferred_element_type=F32) + carry_ref[:, 0:1]
        z = f_ref[...] + b_ref[...]
        sig_neg = 1.0 / (1.0 + jnp.exp(z))
        df = suffix * sig_neg
        df_ref[...] = df
        carry_ref[...] = jnp.broadcast_to(
            carry_ref[:, 0:1] + jnp.sum(dc, axis=1, keepdims=True), carry_ref.shape)
        db_ref[...] += jnp.broadcast_to(jnp.sum(df, axis=1, keepdims=True), db_ref.shape)

    rev = pl.BlockSpec((H, ts), lambda i: (0, nb - 1 - i))
    return pl.pallas_call(
        body, name=name, grid=(nb,),
        in_specs=[rev, pl.BlockSpec((H, 1), lambda i: (0, 0)), rev, rev],
        out_specs=[rev, pl.BlockSpec((H, LANES), lambda i: (0, 0))],
        out_shape=[jax.ShapeDtypeStruct((H, S), F32), jax.ShapeDtypeStruct((H, LANES), F32)],
        scratch_shapes=[pltpu.VMEM((H, LANES), F32)],
        compiler_params=_params("arbitrary"),
    )(fg_t, b_col, dc_plus, dc_minus)


ONES_ROWS = 16
EXTRA = 3


def _split3(x):
    hi = lax.reduce_precision(x, 8, 7)
    mid = lax.reduce_precision(x - hi, 8, 7)
    lo = lax.reduce_precision(x - hi - mid, 8, 7)
    return hi.astype(BF16), mid.astype(BF16), lo.astype(BF16)


def _lanes_operand(t, extras):
    cols = [t] + [e[..., None] for e in extras]
    pad = LANES - HEAD_DIM - len(extras)
    return jnp.concatenate(cols + [jnp.zeros(t.shape[:2] + (pad,), BF16)], axis=-1)


def _rows_operand(t, extras):
    rows = [t] + [e[:, None, :] for e in extras]
    pad = LANES - HEAD_DIM - len(extras)
    return jnp.concatenate(rows + [jnp.zeros((t.shape[0], pad, t.shape[2]), BF16)], axis=1)


def _with_ones(t):
    return jnp.concatenate([t, jnp.ones((t.shape[0], ONES_ROWS, t.shape[2]), t.dtype)], axis=1)


def _fox_fwd(qa, ka, vt, name, tq=512, tk=1024):
    H, _, S = qa.shape
    Dh = HEAD_DIM
    tk = _fit(S, tk)
    tq = _fit(tk, tq)
    ratio = tk // tq

    def body(qa_ref, ka_ref, vt_ref, o_ref, lse_ref, m_ref, acc_ref, sa_ref, sb_ref, ta_ref, tb_ref):
        i = pl.program_id(1)
        qv = qa_ref[...] * QK_SCALE
        m_ref[...] = jnp.full_like(m_ref, NEG)
        acc_ref[...] = jnp.zeros_like(acc_ref)
        n = i // ratio
        q_off = (i - n * ratio) * tq

        def scores(j, s_ref, t_ref, diagonal):
            off = pl.multiple_of(j * tk, LANES)
            s = _dot(ka_ref[pl.ds(off, tk), :], qv, _NN)
            if diagonal:
                key = lax.broadcasted_iota(jnp.int32, (tk, tq), 0)
                qry = lax.broadcasted_iota(jnp.int32, (tk, tq), 1) + q_off
                s = jnp.where(key <= qry, s, NEG)
            s_ref[...] = s
            t_ref[...] = jnp.max(s, axis=0, keepdims=True)

        def absorb(j, s_ref, t_ref):
            off = pl.multiple_of(j * tk, LANES)
            m_old = m_ref[...]
            m_new = jnp.maximum(m_old, t_ref[...])
            p = jnp.exp(s_ref[...] - m_new)
            alpha = jnp.exp(m_old - m_new)
            acc_ref[...] = alpha * acc_ref[...] + _dot(vt_ref[:, pl.ds(off, tk)], p.astype(BF16), _NN)
            m_ref[...] = m_new

        scores(n, sa_ref, ta_ref, True)

        def loop_body(jj, carry):
            scores(2 * jj, sb_ref, tb_ref, False)
            absorb(jnp.where(jj == 0, n, 2 * jj - 1), sa_ref, ta_ref)
            scores(2 * jj + 1, sa_ref, ta_ref, False)
            absorb(2 * jj, sb_ref, tb_ref)
            return carry

        pairs = n // 2
        lax.fori_loop(0, pairs, loop_body, 0)
        held = jnp.where(pairs == 0, n, 2 * pairs - 1)

        @pl.when(n % 2 == 1)
        def _():
            scores(n - 1, sb_ref, tb_ref, False)
            absorb(held, sa_ref, ta_ref)
            absorb(n - 1, sb_ref, tb_ref)

        @pl.when(n % 2 == 0)
        def _():
            absorb(held, sa_ref, ta_ref)

        l = acc_ref[Dh:Dh + 1, :]
        o_ref[...] = acc_ref[0:Dh, :] / l
        lse_ref[...] = m_ref[...] + jnp.log(l)

    return pl.pallas_call(
        body, name=name, grid=(H, S // tq),
        in_specs=[pl.BlockSpec((None, LANES, tq), lambda h, i: (h, 0, i)),
                  pl.BlockSpec((None, S, LANES), lambda h, i: (h, 0, 0)),
                  pl.BlockSpec((None, Dh + ONES_ROWS, S), lambda h, i: (h, 0, 0))],
        out_specs=[pl.BlockSpec((None, Dh, tq), lambda h, i: (h, 0, i)),
                   pl.BlockSpec((None, 1, tq), lambda h, i: (h, 0, i))],
        out_shape=[jax.ShapeDtypeStruct((H, Dh, S), F32), jax.ShapeDtypeStruct((H, 1, S), F32)],
        scratch_shapes=[pltpu.VMEM((1, tq), F32), pltpu.VMEM((Dh + ONES_ROWS, tq), F32),
                        pltpu.VMEM((tk, tq), F32), pltpu.VMEM((tk, tq), F32),
                        pltpu.VMEM((1, tq), F32), pltpu.VMEM((1, tq), F32)],
        compiler_params=_params("parallel", "arbitrary"),
    )(qa, ka, vt)


def _fox_bwd(qa, ka, kta, va, doa, name, tq=1024, tk=512):
    H, _, S = qa.shape
    Dh, Da = HEAD_DIM, HEAD_DIM + ONES_ROWS
    tq = _fit(S, tq)
    tk = _fit(tq, tk)
    ratio = tq // tk
    nq = S // tq
    nk = S // tk

    def body(ka_ref, kta_ref, va_ref, qa_ref, doa_ref, dqt_ref, dkt_ref, dvt_ref, dka_ref, dva_ref):
        j = pl.program_id(1)

        @pl.when(j == 0)
        def _():
            dqt_ref[...] = jnp.zeros_like(dqt_ref)

        kv = ka_ref[...]
        ktv = kta_ref[0:Da, :]
        vv = va_ref[...]
        dka_ref[...] = jnp.zeros_like(dka_ref)
        dva_ref[...] = jnp.zeros_like(dva_ref)
        i_diag = j // ratio
        k_off = (j - i_diag * ratio) * tk

        def step(i, diagonal):
            off = pl.multiple_of(i * tq, LANES)
            qv = qa_ref[:, pl.ds(off, tq)] * QK_SCALE
            dov = doa_ref[:, pl.ds(off, tq)]
            e = _dot(kv, qv, _NN)
            if diagonal:
                key = lax.broadcasted_iota(jnp.int32, (tk, tq), 0) + k_off
                qry = lax.broadcasted_iota(jnp.int32, (tk, tq), 1)
                e = jnp.where(key <= qry, e, NEG)
            p_t = jnp.exp(e)
            dva_ref[...] += _dot(dov[0:Dh, :], p_t.astype(BF16), _NT)
            ds_b = (p_t * _dot(vv, dov, _NN)).astype(BF16)
            dka_ref[...] += _dot(qv[0:Da, :], ds_b, _NT)
            dqt_ref[:, pl.ds(off, tq)] += _dot(ktv, ds_b, _NN)

        step(i_diag, True)

        def loop_body(i, carry):
            step(i, False)
            return carry

        lax.fori_loop(i_diag + 1, nq, loop_body, 0)
        dkt_ref[...] = dka_ref[...]
        dvt_ref[...] = dva_ref[...]

        @pl.when(j == nk - 1)
        def _():
            dqt_ref[0:Dh, :] = dqt_ref[0:Dh, :] * QK_SCALE

    lanes_tile = pl.BlockSpec((None, tk, LANES), lambda h, j: (h, j, 0))
    rows_tile = pl.BlockSpec((None, LANES, tk), lambda h, j: (h, 0, j))
    rows_full = pl.BlockSpec((None, LANES, S), lambda h, j: (h, 0, 0))
    return pl.pallas_call(
        body, name=name, grid=(H, nk),
        in_specs=[lanes_tile, rows_tile, lanes_tile, rows_full, rows_full],
        out_specs=[pl.BlockSpec((None, Da, S), lambda h, j: (h, 0, 0)),
                   pl.BlockSpec((None, Da, tk), lambda h, j: (h, 0, j)),
                   pl.BlockSpec((None, Dh, tk), lambda h, j: (h, 0, j))],
        out_shape=[jax.ShapeDtypeStruct((H, Da, S), F32), jax.ShapeDtypeStruct((H, Da, S), F32),
                   jax.ShapeDtypeStruct((H, Dh, S), F32)],
        scratch_shapes=[pltpu.VMEM((Da, tk), F32), pltpu.VMEM((Dh, tk), F32)],
        compiler_params=_params("parallel", "arbitrary"),
    )(ka, kta, va, qa, doa)


DIL_Q_BLOCK = 3 * FOX_WIDTH // LANES
HEAD_PAIRS = N_DIL_HEADS // 2
PAIR_BLOCKS = DIL_WIDTH // LANES


def _band_geometry(S, d):
    L = S // d
    chunk = min(BAND_CHUNK_MAX, L)
    assert L % chunk == 0 and chunk % BAND == 0
    return L, chunk, chunk // BAND, L // chunk


def _band_in_specs(S, d, base):
    L, chunk, nb, _ = _band_geometry(S, d)

    def col(kind):
        return lambda hp, r, i: (r, i, base + kind * PAIR_BLOCKS + hp)

    def col_prev(kind):
        return lambda hp, r, i: (r, jnp.maximum(i * nb - 1, 0), base + kind * PAIR_BLOCKS + hp)

    main = [pl.BlockSpec((None, chunk, LANES), col(kind)) for kind in range(3)]
    prev = [pl.BlockSpec((None, BAND, LANES), col_prev(kind)) for kind in range(3)]
    bias = pl.BlockSpec((None, 2 * BAND, 2 * BAND), lambda hp, r, i: (hp, 0, 0))
    stat = pl.BlockSpec((None, 2, chunk), lambda hp, r, i: (hp, 0, r * (L // chunk) + i))
    tok = pl.BlockSpec((None, chunk, LANES), lambda hp, r, i: (r, i, hp))
    return main, prev, bias, stat, tok


def _to_residues(x, col_block, width, d, name, tr=512):
    S = x.shape[0]
    tr = _fit(S, tr)

    def body(x_ref, o_ref, tmp_ref):
        for j in range(width // LANES):
            cols = slice(j * LANES, (j + 1) * LANES)
            tmp_ref[j] = x_ref[:, cols].astype(F32)
            for r in range(d):
                o_ref[r, :, cols] = tmp_ref[j, pl.ds(r, tr // d, stride=d), :].astype(o_ref.dtype)

    return pl.pallas_call(
        body, name=name, grid=(S // tr,),
        in_specs=[pl.BlockSpec((tr, width), lambda i: (i, col_block))],
        out_specs=pl.BlockSpec((d, tr // d, width), lambda i: (0, i, 0)),
        out_shape=jax.ShapeDtypeStruct((d, S // d, width), x.dtype),
        scratch_shapes=[pltpu.VMEM((width // LANES, tr, LANES), F32)],
        compiler_params=_params("parallel"),
    )(x)


def _token_rows(ref, cols, tmp_ref):
    if len(ref.shape) == 2:
        return ref[:, cols].astype(F32)
    d, rows = ref.shape[0], ref.shape[1]
    for r in range(d):
        tmp_ref[pl.ds(r, rows, stride=d), :] = ref[r, :, cols].astype(F32)
    return tmp_ref[...]


def _row_spec(t, tr):
    if t.ndim == 2:
        return pl.BlockSpec((tr, t.shape[1]), lambda i: (i, 0))
    d = t.shape[0]
    return pl.BlockSpec((d, tr // d, t.shape[2]), lambda i: (0, i, 0))


def _head_lanes(a):
    return lax.broadcasted_iota(jnp.int32, (1, LANES), 1) // HEAD_DIM == a


def _one_head(x, a):
    return jnp.where(_head_lanes(a), x, jnp.zeros_like(x))


def _head_stack(x):
    return jnp.concatenate([_one_head(x, 0), _one_head(x, 1)], axis=0)


def _pair_rows(ref, rows):
    return jnp.concatenate([ref[0:1, rows], ref[1:2, rows]], axis=1)


def _band_scores_t(kb, q_stack, bias_t, first):
    s = _dot(kb, q_stack, _NT) + bias_t
    if first is not None:
        key = lax.broadcasted_iota(jnp.int32, s.shape, 0)
        s = jnp.where(jnp.logical_and(first, key < BAND), NEG, s)
    return s


def _pair_select(stacked):
    return jnp.where(_head_lanes(0), stacked[0:BAND, :], stacked[BAND:, :])


def _dil_lse(qkv_v, base, bias_t, name):
    d, L = qkv_v.shape[:2]
    S = L * d
    _, chunk, nb, nchunks = _band_geometry(S, d)
    main, prev, bias, stat, _ = _band_in_specs(S, d, base)

    def body(q_ref, k_ref, kp_ref, b_ref, lse_ref, kext_ref):
        first = pl.program_id(2) == 0
        kext_ref[0:BAND, :] = kp_ref[...]
        kext_ref[BAND:, :] = k_ref[...]
        for b in range(nb):
            rows, ext = slice(b * BAND, (b + 1) * BAND), slice(b * BAND, (b + 2) * BAND)
            s = _band_scores_t(kext_ref[ext, :], _head_stack(q_ref[rows, :] * QK_SCALE), b_ref[...],
                               first if b == 0 else None)
            m = jnp.max(s, axis=0, keepdims=True)
            lse = m + jnp.log(jnp.sum(jnp.exp(s - m), axis=0, keepdims=True))
            lse_ref[0:1, rows] = lse[:, 0:BAND]
            lse_ref[1:2, rows] = lse[:, BAND:]

    return pl.pallas_call(
        body, name=name, grid=(HEAD_PAIRS, d, nchunks),
        in_specs=[main[0], main[1], prev[1], bias], out_specs=stat,
        out_shape=jax.ShapeDtypeStruct((HEAD_PAIRS, 2, S), F32),
        scratch_shapes=[pltpu.VMEM((chunk + BAND, LANES), BF16)],
        compiler_params=_params("parallel", "parallel", "parallel"),
    )(qkv_v, qkv_v, qkv_v, bias_t)


def _dil_out(qkv_v, base, bias_t, lse_joint, name):
    d, L = qkv_v.shape[:2]
    S = L * d
    _, chunk, nb, nchunks = _band_geometry(S, d)
    main, prev, bias, stat, tok = _band_in_specs(S, d, base)

    def body(q_ref, k_ref, kp_ref, v_ref, vp_ref, b_ref, lse_ref, o_ref, kext_ref, vext_ref):
        first = pl.program_id(2) == 0
        kext_ref[0:BAND, :] = kp_ref[...]
        kext_ref[BAND:, :] = k_ref[...]
        vext_ref[0:BAND, :] = vp_ref[...]
        vext_ref[BAND:, :] = v_ref[...]
        for b in range(nb):
            rows, ext = slice(b * BAND, (b + 1) * BAND), slice(b * BAND, (b + 2) * BAND)
            s = _band_scores_t(kext_ref[ext, :], _head_stack(q_ref[rows, :] * QK_SCALE), b_ref[...],
                               first if b == 0 else None)
            p_t = jnp.exp(s - _pair_rows(lse_ref, rows))
            o_ref[rows, :] = _pair_select(_dot(p_t.astype(BF16), vext_ref[ext, :], _TN)).astype(BF16)

    return pl.pallas_call(
        body, name=name, grid=(HEAD_PAIRS, d, nchunks),
        in_specs=[main[0], main[1], prev[1], main[2], prev[2], bias, stat], out_specs=tok,
        out_shape=jax.ShapeDtypeStruct((d, L, DIL_WIDTH), BF16),
        scratch_shapes=[pltpu.VMEM((chunk + BAND, LANES), BF16), pltpu.VMEM((chunk + BAND, LANES), BF16)],
        compiler_params=_params("parallel", "parallel", "parallel"),
    )(qkv_v, qkv_v, qkv_v, qkv_v, qkv_v, bias_t, lse_joint)


def _dil_bwd(qkv_v, base, do_v, bias_t, lse_joint, delta, name):
    d, L = qkv_v.shape[:2]
    S = L * d
    _, chunk, nb, nchunks = _band_geometry(S, d)
    main, prev, bias, stat, tok = _band_in_specs(S, d, base)
    nblocks = L // BAND

    def nxt_row(i):
        return jnp.minimum((i + 1) * nb, nblocks - 1)

    q_next = pl.BlockSpec((None, BAND, LANES), lambda hp, r, i: (r, nxt_row(i), base + hp))
    do_next = pl.BlockSpec((None, BAND, LANES), lambda hp, r, i: (r, nxt_row(i), hp))
    stat_next = pl.BlockSpec((None, 2, BAND), lambda hp, r, i: (hp, 0, r * nblocks + nxt_row(i)))

    def body(q_ref, k_ref, kp_ref, v_ref, vp_ref, do_ref, b_ref, lse_ref, dl_ref,
             qn_ref, don_ref, lsen_ref, dln_ref,
             dq_ref, dk_ref, dv_ref, db_ref, kext_ref, vext_ref, dkext_ref, dvext_ref):
        r, i = pl.program_id(1), pl.program_id(2)
        first = i == 0
        has_next = i + 1 < nchunks
        tail = slice(BAND + chunk, 2 * BAND + chunk)
        kext_ref[0:BAND, :] = kp_ref[...]
        kext_ref[BAND:BAND + chunk, :] = k_ref[...]
        kext_ref[tail, :] = jnp.zeros((BAND, LANES), BF16)
        vext_ref[0:BAND, :] = vp_ref[...]
        vext_ref[BAND:BAND + chunk, :] = v_ref[...]
        vext_ref[tail, :] = jnp.zeros((BAND, LANES), BF16)
        dkext_ref[...] = jnp.zeros_like(dkext_ref)
        dvext_ref[...] = jnp.zeros_like(dvext_ref)

        @pl.when(jnp.logical_and(r == 0, i == 0))
        def _():
            db_ref[...] = jnp.zeros_like(db_ref)

        def block(q2, do2, lse_row, dl_row, ext, mask_rows):
            q_stack, do_stack = _head_stack(q2), _head_stack(do2)
            s = _dot(kext_ref[ext, :], q_stack, _NT) + b_ref[...]
            if mask_rows is not None:
                s = jnp.where(mask_rows, NEG, s)
            p_t = jnp.exp(s - lse_row)
            ds_t = p_t * (_dot(vext_ref[ext, :], do_stack, _NT) - dl_row)
            ds_b = ds_t.astype(BF16)
            dkext_ref[ext, :] += _dot(ds_b, q_stack, _NN)
            dvext_ref[ext, :] += _dot(p_t.astype(BF16), do_stack, _NN)
            return ds_t, ds_b

        key = lax.broadcasted_iota(jnp.int32, (2 * BAND, 2 * BAND), 0)
        all_lanes = slice(0, BAND)
        for b in range(nb):
            rows, ext = slice(b * BAND, (b + 1) * BAND), slice(b * BAND, (b + 2) * BAND)
            mask = jnp.logical_and(first, key < BAND) if b == 0 else None
            ds_t, ds_b = block(q_ref[rows, :] * QK_SCALE, do_ref[rows, :], _pair_rows(lse_ref, rows),
                               _pair_rows(dl_ref, rows), ext, mask)
            dq_ref[rows, :] = _pair_select(_dot(ds_b, kext_ref[ext, :], _TN)) * QK_SCALE
            db_ref[...] += ds_t
        block(qn_ref[...] * QK_SCALE, don_ref[...], _pair_rows(lsen_ref, all_lanes), _pair_rows(dln_ref, all_lanes),
              slice(chunk, chunk + 2 * BAND), jnp.logical_or(jnp.logical_not(has_next), key >= BAND))
        dk_ref[...] = dkext_ref[BAND:BAND + chunk, :]
        dv_ref[...] = dvext_ref[BAND:BAND + chunk, :]

    ext_rows = chunk + 2 * BAND
    return pl.pallas_call(
        body, name=name, grid=(HEAD_PAIRS, d, nchunks),
        in_specs=[main[0], main[1], prev[1], main[2], prev[2], tok, bias, stat, stat,
                  q_next, do_next, stat_next, stat_next],
        out_specs=[tok, tok, tok, bias],
        out_shape=[jax.ShapeDtypeStruct((d, L, DIL_WIDTH), F32)] * 3
                  + [jax.ShapeDtypeStruct((HEAD_PAIRS, 2 * BAND, 2 * BAND), F32)],
        scratch_shapes=[pltpu.VMEM((ext_rows, LANES), BF16), pltpu.VMEM((ext_rows, LANES), BF16),
                        pltpu.VMEM((ext_rows, LANES), F32), pltpu.VMEM((ext_rows, LANES), F32)],
        compiler_params=_params("arbitrary", "arbitrary", "arbitrary"),
    )(qkv_v, qkv_v, qkv_v, qkv_v, qkv_v, do_v, bias_t, lse_joint, delta, qkv_v, do_v, lse_joint, delta)


def _lse_join(lse3, name):
    P, H, S = lse3.shape

    def body(l_ref, o_ref):
        a, b, c = l_ref[0], l_ref[1], l_ref[2]
        m = jnp.maximum(jnp.maximum(a, b), c)
        o_ref[...] = m + jnp.log(jnp.exp(a - m) + jnp.exp(b - m) + jnp.exp(c - m))

    return pl.pallas_call(body, name=name, out_shape=jax.ShapeDtypeStruct((H, S), F32))(lse3)


def _bucket_reduce(dbias_t, bucket_map_t, name):
    P, H = dbias_t.shape[:2]

    def body(db_ref, bk_ref, o_ref):
        p, h = pl.program_id(0), pl.program_id(1)

        @pl.when(jnp.logical_and(p == 0, h == 0))
        def _():
            o_ref[...] = jnp.zeros_like(o_ref)

        db, bk = db_ref[...], bk_ref[...]
        row = lax.broadcasted_iota(jnp.int32, (N_BUCKETS, LANES), 0)
        lane = lax.broadcasted_iota(jnp.int32, (N_BUCKETS, LANES), 1)

        def one(b, acc):
            val = jnp.sum(jnp.sum(jnp.where(bk == b, db, 0.0), axis=1, keepdims=True), axis=0, keepdims=True)
            return acc + jnp.where(jnp.logical_and(row == b, lane == h), val, 0.0)

        o_ref[...] += lax.fori_loop(0, N_BUCKETS, one, jnp.zeros((N_BUCKETS, LANES), F32))

    return pl.pallas_call(
        body, name=name, grid=(P, H),
        in_specs=[pl.BlockSpec((None, None, 2 * BAND, BAND), lambda p, h: (p, h, 0, 0)),
                  pl.BlockSpec((None, 2 * BAND, BAND), lambda p, h: (p, 0, 0))],
        out_specs=pl.BlockSpec((N_BUCKETS, LANES), lambda p, h: (0, 0)),
        out_shape=jax.ShapeDtypeStruct((N_BUCKETS, LANES), F32),
        compiler_params=_params("arbitrary", "arbitrary"),
    )(dbias_t, bucket_map_t)


def _mem_fwd(q, kv, name, tq=1024):
    S, W = q.shape
    N = kv.shape[0]
    pairs = W // LANES
    tq = _fit(S, tq)

    def body(q_ref, k_ref, v_ref, o_ref, lse_ref):
        for a in range(2):
            lanes = slice(a * HEAD_DIM, (a + 1) * HEAD_DIM)
            s = _dot(k_ref[:, lanes], q_ref[:, lanes] * QK_SCALE, _NT)
            m = jnp.max(s, axis=0, keepdims=True)
            e = jnp.exp(s - m)
            l = jnp.sum(e, axis=0, keepdims=True)
            o_ref[:, lanes] = _dot((e / l).astype(BF16), v_ref[:, lanes], _TN).astype(BF16)
            lse_ref[a:a + 1, :] = m + jnp.log(l)

    return pl.pallas_call(
        body, name=name, grid=(pairs, S // tq),
        in_specs=[pl.BlockSpec((tq, LANES), lambda hp, i: (i, hp)),
                  pl.BlockSpec((N, LANES), lambda hp, i: (0, hp)),
                  pl.BlockSpec((N, LANES), lambda hp, i: (0, pairs + hp))],
        out_specs=[pl.BlockSpec((tq, LANES), lambda hp, i: (i, hp)),
                   pl.BlockSpec((None, 2, tq), lambda hp, i: (hp, 0, i))],
        out_shape=[jax.ShapeDtypeStruct((S, W), BF16), jax.ShapeDtypeStruct((pairs, 2, S), F32)],
        compiler_params=_params("parallel", "parallel"),
    )(q, kv, kv)


def _mem_bwd(q, kv, do, lse, delta, name, tq=1024):
    S, W = q.shape
    N = kv.shape[0]
    pairs = W // LANES
    tq = _fit(S, tq)

    def body(q_ref, k_ref, v_ref, do_ref, lse_ref, dl_ref, dq_ref, dk_ref, dv_ref):
        i = pl.program_id(1)

        @pl.when(i == 0)
        def _():
            dk_ref[...] = jnp.zeros_like(dk_ref)
            dv_ref[...] = jnp.zeros_like(dv_ref)

        for a in range(2):
            lanes = slice(a * HEAD_DIM, (a + 1) * HEAD_DIM)
            qv, dov = q_ref[:, lanes] * QK_SCALE, do_ref[:, lanes]
            kv_, vv = k_ref[:, lanes], v_ref[:, lanes]
            p_t = jnp.exp(_dot(kv_, qv, _NT) - lse_ref[a:a + 1, :])
            ds_t = p_t * (_dot(vv, dov, _NT) - dl_ref[a:a + 1, :])
            ds_b = ds_t.astype(BF16)
            dq_ref[:, lanes] = (_dot(ds_b, kv_, _TN) * QK_SCALE).astype(BF16)
            dk_ref[:, lanes] += _dot(ds_b, qv, _NN)
            dv_ref[:, lanes] += _dot(p_t.astype(BF16), dov, _NN)

    qs = pl.BlockSpec((tq, LANES), lambda hp, i: (i, hp))
    stat = pl.BlockSpec((None, 2, tq), lambda hp, i: (hp, 0, i))
    acc = pl.BlockSpec((N, LANES), lambda hp, i: (0, hp))
    return pl.pallas_call(
        body, name=name, grid=(pairs, S // tq),
        in_specs=[qs, acc, pl.BlockSpec((N, LANES), lambda hp, i: (0, pairs + hp)), qs, stat, stat],
        out_specs=[qs, acc, acc],
        out_shape=[jax.ShapeDtypeStruct((S, W), BF16), jax.ShapeDtypeStruct((N, W), F32),
                   jax.ShapeDtypeStruct((N, W), F32)],
        compiler_params=_params("parallel", "arbitrary"),
    )(q, kv, kv, do, lse, delta)


def _head_rowdot(a, bs, name, tr=512):
    S, W = a.shape
    tr = _fit(S, tr)

    def body(*refs):
        a_ref, b_refs, o_ref, tmp_ref = refs[0], refs[1:-2], refs[-2], refs[-1]
        col = lax.broadcasted_iota(jnp.int32, (LANES, LANES), 0)
        lane = lax.broadcasted_iota(jnp.int32, (LANES, LANES), 1)
        acc = jnp.zeros((tr, LANES), F32)
        for j in range(W // LANES):
            cols = slice(j * LANES, (j + 1) * LANES)
            tot = _token_rows(b_refs[0], cols, tmp_ref)
            for r in b_refs[1:]:
                tot = tot + _token_rows(r, cols, tmp_ref)
            sel = jnp.where(col // HEAD_DIM + j * (LANES // HEAD_DIM) == lane, 1.0, 0.0).astype(F32)
            acc = acc + lax.dot_general(a_ref[:, cols].astype(F32) * tot, sel, (_NN, ((), ())),
                                        precision=lax.Precision.HIGHEST, preferred_element_type=F32)
        o_ref[...] = acc

    return pl.pallas_call(
        body, name=name, grid=(S // tr,), in_specs=[_row_spec(t, tr) for t in [a] + list(bs)],
        out_specs=pl.BlockSpec((tr, LANES), lambda i: (i, 0)),
        out_shape=jax.ShapeDtypeStruct((S, LANES), F32),
        scratch_shapes=[pltpu.VMEM((tr, LANES), F32)],
        compiler_params=_params("parallel"),
    )(a, *bs)


def _sum_cast_cols(groups, out_dtype, name, tail=None, tr=256):
    first = groups[0][0]
    S, W = (first.shape if first.ndim == 2 else (first.shape[0] * first.shape[1], first.shape[2]))
    tr = _fit(S, tr)
    flat = [t for g in groups for t in g] + ([tail] if tail is not None else [])
    tail_w = 0 if tail is None else tail.shape[1]

    def body(*refs):
        o_ref, tmp_ref = refs[-2], refs[-1]
        if tail is not None:
            o_ref[:, W * len(groups):] = refs[-3][...].astype(out_dtype)
        k = 0
        for gi, g in enumerate(groups):
            for j in range(W // LANES):
                cols = slice(j * LANES, (j + 1) * LANES)
                acc = _token_rows(refs[k], cols, tmp_ref)
                for r in refs[k + 1:k + len(g)]:
                    acc = acc + _token_rows(r, cols, tmp_ref)
                o_ref[:, gi * W + j * LANES:gi * W + (j + 1) * LANES] = acc.astype(out_dtype)
            k += len(g)

    return pl.pallas_call(
        body, name=name, grid=(S // tr,), in_specs=[_row_spec(t, tr) for t in flat],
        out_specs=pl.BlockSpec((tr, W * len(groups) + tail_w), lambda i: (i, 0)),
        out_shape=jax.ShapeDtypeStruct((S, W * len(groups) + tail_w), out_dtype),
        scratch_shapes=[pltpu.VMEM((tr, LANES), F32)],
        compiler_params=_params("parallel"),
    )(*flat)


FF_TILE = 256


def _ffn_up(h, w_gu, name, tm=1024):
    S, D = h.shape
    F2 = w_gu.shape[1]
    tm = _fit(S, tm)

    def body(h_ref, w_ref, gu_ref, act_ref):
        gu = _dot(h_ref[...], w_ref[...], _NN)
        gu_ref[...] = gu.astype(BF16)
        g, u = gu[:, :FF_TILE], gu[:, FF_TILE:]
        act_ref[...] = (g * (1.0 / (1.0 + jnp.exp(-g))) * u).astype(BF16)

    return pl.pallas_call(
        body, name=name, grid=(S // tm, F2 // (2 * FF_TILE)),
        in_specs=[pl.BlockSpec((tm, D), lambda i, j: (i, 0)), pl.BlockSpec((D, 2 * FF_TILE), lambda i, j: (0, j))],
        out_specs=[pl.BlockSpec((tm, 2 * FF_TILE), lambda i, j: (i, j)),
                   pl.BlockSpec((tm, FF_TILE), lambda i, j: (i, j))],
        out_shape=[jax.ShapeDtypeStruct((S, F2), BF16), jax.ShapeDtypeStruct((S, F2 // 2), BF16)],
        compiler_params=_params("parallel", "arbitrary"),
    )(h, w_gu)


def _ffn_dact(dy, w_down, gu, name, tm=1024):
    S, D = dy.shape
    F2 = gu.shape[1]
    tm = _fit(S, tm)

    def body(dy_ref, w_ref, gu_ref, dgu_ref):
        dact = _dot(dy_ref[...], w_ref[...], _NT)
        gu_v = gu_ref[...].astype(F32)
        g, u = gu_v[:, :FF_TILE], gu_v[:, FF_TILE:]
        sig = 1.0 / (1.0 + jnp.exp(-g))
        silu = g * sig
        dgu_ref[:, :FF_TILE] = (dact * u * (sig + silu * (1.0 - sig))).astype(BF16)
        dgu_ref[:, FF_TILE:] = (dact * silu).astype(BF16)

    return pl.pallas_call(
        body, name=name, grid=(S // tm, F2 // (2 * FF_TILE)),
        in_specs=[pl.BlockSpec((tm, D), lambda i, j: (i, 0)), pl.BlockSpec((FF_TILE, D), lambda i, j: (j, 0)),
                  pl.BlockSpec((tm, 2 * FF_TILE), lambda i, j: (i, j))],
        out_specs=pl.BlockSpec((tm, 2 * FF_TILE), lambda i, j: (i, j)),
        out_shape=jax.ShapeDtypeStruct((S, F2), BF16),
        compiler_params=_params("parallel", "arbitrary"),
    )(dy, w_down, gu)


def _fit_rows(n, cap):
    if n <= cap:
        return n
    t = (cap // 8) * 8
    while t >= 8:
        if n % t == 0:
            return t
        t -= 8
    raise ValueError(f"no sublane-aligned tile for {n} under {cap}")


def _add_n(arrs, name, tr=512):
    R, C = arrs[0].shape
    tr = _fit_rows(R, tr)

    def body(*refs):
        acc = refs[0][...]
        for r in refs[1:-1]:
            acc = acc + r[...]
        refs[-1][...] = acc

    row = pl.BlockSpec((tr, C), lambda i: (i, 0))
    return pl.pallas_call(
        body, name=name, grid=(R // tr,), in_specs=[row] * len(arrs), out_specs=row,
        out_shape=jax.ShapeDtypeStruct((R, C), F32), compiler_params=_params("parallel"),
    )(*arrs)


def _adamw(w, g, m, v, name, tr=512):
    R, C = w.shape
    tr = _fit_rows(R, tr)
    c1 = 1.0 / (1.0 - ADAM_B1 ** ADAM_STEP)
    c2 = 1.0 / (1.0 - ADAM_B2 ** ADAM_STEP)

    def body(w_ref, g_ref, m_ref, v_ref, d_ref, nm_ref, nv_ref):
        gv = g_ref[...]
        nm = ADAM_B1 * m_ref[...] + (1.0 - ADAM_B1) * gv
        nv = ADAM_B2 * v_ref[...] + (1.0 - ADAM_B2) * (gv * gv)
        nm_ref[...] = nm
        nv_ref[...] = nv
        d_ref[...] = -ADAM_LR * ((nm * c1) / (jnp.sqrt(nv * c2) + ADAM_EPS) + ADAM_WD * w_ref[...])

    row = pl.BlockSpec((tr, C), lambda i: (i, 0))
    return pl.pallas_call(
        body, name=name, grid=(R // tr,), in_specs=[row] * 4, out_specs=[row] * 3,
        out_shape=[jax.ShapeDtypeStruct((R, C), F32)] * 3, compiler_params=_params("parallel"),
    )(w, g, m, v)


def _place():
    return lax.axis_index("x"), lax.axis_index("y"), lax.axis_index("c")


_ANY = pl.BlockSpec(memory_space=pl.ANY)


def _chip_all_gather(shard, name):
    R, C = shard.shape
    half = R // 2

    def body(x_ref, out_ref, send_sems, recv_sems, local_sem):
        x, y, c = _place()
        chips = [(1 - x, y), (x, 1 - y), (1 - x, 1 - y)]
        sibling = (x, y, 1 - c)
        mine = pltpu.make_async_copy(x_ref, out_ref.at[2 * x + y], local_sem)
        mine.start()

        def rows(chip, core):
            return out_ref.at[chip, pl.ds(core * half, half)]

        def copy(k, chip, core, to, src=None):
            return pltpu.make_async_remote_copy(
                src_ref=rows(chip, core) if src is None else src, dst_ref=rows(chip, core),
                send_sem=send_sems.at[k], recv_sem=recv_sems.at[k], device_id=to, device_id_type=MESH_IDS)

        me = 2 * x + y
        first = [copy(k, me, c, (cx, cy, c), src=x_ref.at[pl.ds(c * half, half)]) for k, (cx, cy) in enumerate(chips)]
        for cp in first:
            cp.start()
        passed = [copy(3 + k, 2 * cx + cy, c, sibling) for k, (cx, cy) in enumerate(chips)]
        for k, (cx, cy) in enumerate(chips):
            copy(k, 2 * cx + cy, c, (cx, cy, c)).wait_recv()
            passed[k].start()
        for k, (cx, cy) in enumerate(chips):
            copy(3 + k, 2 * cx + cy, 1 - c, sibling).wait_recv()
        for cp in first + passed:
            cp.wait_send()
        mine.wait()

    return pl.pallas_call(
        body, name=name, in_specs=[_ANY], out_specs=_ANY,
        out_shape=jax.ShapeDtypeStruct((N_CHIPS, R, C), shard.dtype),
        scratch_shapes=[pltpu.SemaphoreType.DMA((6,)), pltpu.SemaphoreType.DMA((6,)), pltpu.SemaphoreType.DMA],
    )(shard)


def _sibling_exchange(buf, name):
    def body(x_ref, out_ref, send_sem, recv_sem):
        x, y, c = _place()
        cp = pltpu.make_async_remote_copy(
            src_ref=x_ref, dst_ref=out_ref, send_sem=send_sem, recv_sem=recv_sem,
            device_id=(x, y, 1 - c), device_id_type=MESH_IDS)
        cp.start()
        cp.wait()

    return pl.pallas_call(
        body, name=name, in_specs=[_ANY], out_specs=_ANY,
        out_shape=jax.ShapeDtypeStruct(buf.shape, buf.dtype),
        scratch_shapes=[pltpu.SemaphoreType.DMA, pltpu.SemaphoreType.DMA],
    )(buf)


def _chip_scatter(parts, name):
    _, R, C = parts.shape

    def body(p_ref, out_ref, send_sems, recv_sems):
        x, y, c = _place()
        chips = [(1 - x, y), (x, 1 - y), (1 - x, 1 - y)]

        def copy(k, slab, to):
            return pltpu.make_async_remote_copy(
                src_ref=p_ref.at[slab], dst_ref=out_ref.at[k], send_sem=send_sems.at[k], recv_sem=recv_sems.at[k],
                device_id=to, device_id_type=MESH_IDS)

        sends = [copy(k, 2 * cx + cy, (cx, cy, c)) for k, (cx, cy) in enumerate(chips)]
        for cp in sends:
            cp.start()
        for cp in sends:
            cp.wait_recv()
        for cp in sends:
            cp.wait_send()

    return pl.pallas_call(
        body, name=name, in_specs=[_ANY], out_specs=_ANY,
        out_shape=jax.ShapeDtypeStruct((3, R, C), parts.dtype),
        scratch_shapes=[pltpu.SemaphoreType.DMA((3,)), pltpu.SemaphoreType.DMA((3,))],
    )(parts)


def _all_to_all_small(vec, name):
    R, C = vec.shape

    def body(v_ref, out_ref, send_sems, recv_sems, local_sem):
        x, y, c = _place()
        me = 4 * x + 2 * y + c
        mine = pltpu.make_async_copy(v_ref, out_ref.at[me], local_sem)
        mine.start()
        flips = [(dx, dy, dc) for dx in (0, 1) for dy in (0, 1) for dc in (0, 1)][1:]

        def peer(f):
            return (x ^ f[0], y ^ f[1], c ^ f[2])

        def copy(k, slot, to):
            return pltpu.make_async_remote_copy(
                src_ref=v_ref, dst_ref=out_ref.at[slot], send_sem=send_sems.at[k], recv_sem=recv_sems.at[k],
                device_id=to, device_id_type=MESH_IDS)

        sends = [copy(k, me, peer(f)) for k, f in enumerate(flips)]
        for cp in sends:
            cp.start()
        for k, f in enumerate(flips):
            px, py, pc = peer(f)
            copy(k, 4 * px + 2 * py + pc, peer(f)).wait_recv()
        for cp in sends:
            cp.wait_send()
        mine.wait()

    return pl.pallas_call(
        body, name=name, in_specs=[_ANY], out_specs=_ANY,
        out_shape=jax.ShapeDtypeStruct((8, R, C), vec.dtype),
        scratch_shapes=[pltpu.SemaphoreType.DMA((7,)), pltpu.SemaphoreType.DMA((7,)), pltpu.SemaphoreType.DMA],
    )(vec)


def _to_heads(t, n):
    S = t.shape[0]
    return t.reshape(S, n, HEAD_DIM).transpose(1, 0, 2)


def _to_heads_t(t, n):
    S = t.shape[0]
    return t.T.reshape(n, HEAD_DIM, S)


def _from_heads_t(t):
    H, Dh, S = t.shape
    return t.reshape(H * Dh, S).T


def _t5_bucket(dist):
    max_exact = N_BUCKETS // 2
    d = np.maximum(dist, 1).astype(np.float32)
    large = max_exact + (np.log(d / max_exact) / np.log(MAX_DISTANCE / max_exact)
                         * (N_BUCKETS - max_exact)).astype(np.int32)
    large = np.minimum(large, N_BUCKETS - 1)
    return np.where(dist < max_exact, dist, large).astype(np.int32)


def _band_tables():
    qi = np.arange(BAND)[:, None]
    kj = np.arange(2 * BAND)[None, :]
    sub = qi + BAND - kj
    band = (sub >= 0) & (sub <= BAND)
    out = []
    for d in DILATIONS:
        bucket = _t5_bucket(np.clip(sub, 0, BAND) * d)
        out.append(np.where(band, bucket, -1).astype(np.int32))
    return np.stack(out)


_PACK = (("w_in", 770), ("w_out", 256), ("w_xq", 64), ("w_xk", 64), ("w_xv", 64), ("w_xo", 64),
         ("w_gate", 704), ("w_up", 704), ("w_down", 704))


def _pack(shards):
    rows = [shards[n].reshape(-1, PACK_COLS) for n, _ in _PACK]
    total = sum(r.shape[0] for r in rows)
    pad = (-total) % 128
    if pad:
        rows.append(jnp.zeros((pad, PACK_COLS), rows[0].dtype))
    return jnp.concatenate(rows, axis=0)


def _unpack(pack, shapes):
    out, r = {}, 0
    for n, _ in _PACK:
        cnt = int(np.prod(shapes[n])) // PACK_COLS
        out[n] = pack[r:r + cnt].reshape(shapes[n])
        r += cnt
    return out


_COL_SHARDED = ("w_in", "w_xo", "w_gate", "w_up")


def _full_weight(gathered, name):
    return jnp.concatenate(gathered, axis=1 if name in _COL_SHARDED else 0)


def _split_weight(full, name):
    return jnp.split(full, N_CHIPS, axis=1 if name in _COL_SHARDED else 0)


_SMALL = ("g_mix_pre", "g_mix_post", "g_xattn_pre", "g_mem", "g_xattn_post", "g_ffn_pre", "g_ffn_post")


def _pack_small(vals):
    D = vals["g_mix_pre"].shape[1]
    rows = [vals[n].reshape(1, D) for n in _SMALL]
    misc = jnp.concatenate([vals["b_f"].reshape(-1), vals["rel_bias"].reshape(-1)])
    rows.append(jnp.pad(misc, (0, D - misc.shape[0])).reshape(1, D))
    rows.append(jnp.zeros((16 - len(rows), D), F32))
    return jnp.concatenate(rows, axis=0)


def _unpack_small(pack):
    out = {n: pack[i:i + 1] for i, n in enumerate(_SMALL)}
    out["b_f"] = pack[7, 0:N_FOX_HEADS].reshape(1, N_FOX_HEADS)
    out["rel_bias"] = pack[7, N_FOX_HEADS:N_FOX_HEADS + N_BUCKETS * N_DIL_HEADS].reshape(N_BUCKETS, N_DIL_HEADS)
    return out


def kernel(x, mem, g_mix_pre, w_in, b_f, rel_bias, w_out, g_mix_post, g_xattn_pre, g_mem, w_xq, w_xk, w_xv, w_xo, g_xattn_post, g_ffn_pre, w_gate, w_up, w_down, g_ffn_post, loss_target, m_g_mix_pre, m_w_in, m_b_f, m_rel_bias, m_w_out, m_g_mix_post, m_g_xattn_pre, m_g_mem, m_w_xq, m_w_xk, m_w_xv, m_w_xo, m_g_xattn_post, m_g_ffn_pre, m_w_gate, m_w_up, m_w_down, m_g_ffn_post, v_g_mix_pre, v_w_in, v_b_f, v_rel_bias, v_w_out, v_g_mix_post, v_g_xattn_pre, v_g_mem, v_w_xq, v_w_xk, v_w_xv, v_w_xo, v_g_xattn_post, v_g_ffn_pre, v_w_gate, v_w_up, v_w_down, v_g_ffn_post):
    args = dict(locals())
    big = [n for n, _ in _PACK]
    names = ["g_mix_pre", "w_in", "b_f", "rel_bias", "w_out", "g_mix_post", "g_xattn_pre", "g_mem", "w_xq",
             "w_xk", "w_xv", "w_xo", "g_xattn_post", "g_ffn_pre", "w_gate", "w_up", "w_down", "g_ffn_post"]
    xs = x[0]
    S, D = xs.shape
    assert S % (BAND * DILATIONS[-1]) == 0
    shard_shapes = {n: args[n].shape[1:] for n in big}
    my_x, my_y, my_c = lax.axis_index("x"), lax.axis_index("y"), lax.axis_index("c")

    gathered = _chip_all_gather(_pack({n: args[n][0].astype(BF16) for n in big}), "weights_all_gather")
    per_chip = [_unpack(gathered[j], shard_shapes) for j in range(N_CHIPS)]
    W = {n: _full_weight([pc[n] for pc in per_chip], n) for n in big}
    w_fox, w_fg, w_dil = (W["w_in"][:, :3 * FOX_WIDTH], W["w_in"][:, 3 * FOX_WIDTH:3 * FOX_WIDTH + N_FOX_HEADS],
                          W["w_in"][:, 3 * FOX_WIDTH + N_FOX_HEADS:])
    w_qkv = jnp.concatenate([w_fox, w_dil], axis=1)
    w_fg_pad = jnp.pad(w_fg, ((0, 0), (0, LANES - N_FOX_HEADS)))
    F = W["w_gate"].shape[1]
    nft = F // FF_TILE
    w_gu = jnp.stack([W["w_gate"].reshape(D, nft, FF_TILE), W["w_up"].reshape(D, nft, FF_TILE)],
                     axis=2).reshape(D, 2 * F)

    h1 = _rms_fwd(xs, g_mix_pre, "rms_mix_pre")
    qkv = _mm(h1, w_qkv, "nn", BF16, "proj_qkv")
    fg = _mm(h1, w_fg_pad, "nn", F32, "proj_gate")
    fg_t = fg[:, :N_FOX_HEADS].T
    b_col = b_f.reshape(N_FOX_HEADS, 1)
    c_t = _forget_fwd(fg_t, b_col, "forget_cumsum")
    fq_s, fk_s, fv_s = (qkv[:, i * FOX_WIDTH:(i + 1) * FOX_WIDTH] for i in range(3))
    fqt, fvt = _to_heads_t(fq_s, N_FOX_HEADS), _to_heads_t(fv_s, N_FOX_HEADS)
    unit = jnp.full((N_FOX_HEADS, S), 1.0, BF16)
    inv_scale = jnp.full((N_FOX_HEADS, S), 1.0 / QK_SCALE, BF16)
    ka = _lanes_operand(_to_heads(fk_s, N_FOX_HEADS), list(_split3(-c_t)) + [unit] * EXTRA)
    o_fox_t, lse_fox = _fox_fwd(_rows_operand(fqt, [inv_scale] * EXTRA), ka, _with_ones(fvt), "fox_fwd")

    bucket_map = _band_tables()
    onehot = (jnp.asarray(bucket_map)[..., None] == jnp.arange(N_BUCKETS)).astype(F32)
    bias_tab = jnp.einsum("pqkb,bh->phkq", onehot, rel_bias, precision=lax.Precision.HIGHEST)
    bias_tab = jnp.where(jnp.asarray(bucket_map.transpose(0, 2, 1) >= 0)[:, None], bias_tab, NEG)
    bias_t = bias_tab.reshape(3, HEAD_PAIRS, 2, 2 * BAND, BAND).transpose(0, 1, 3, 2, 4).reshape(
        3, HEAD_PAIRS, 2 * BAND, 2 * BAND)
    views = [(qkv.reshape(1, S, qkv.shape[1]), DIL_Q_BLOCK)] + [
        (_to_residues(qkv, 1, 3 * DIL_WIDTH, d, f"dilated_qkv_residues_{d}"), 0) for d in DILATIONS[1:]]

    def to_tok(stat, d):
        return stat.reshape(N_DIL_HEADS, d, S // d).swapaxes(1, 2).reshape(N_DIL_HEADS, S)

    def to_perm(stat, d):
        return stat.reshape(N_DIL_HEADS, S // d, d).swapaxes(1, 2).reshape(HEAD_PAIRS, 2, S)

    def tok_or_res(t):
        return t.reshape(t.shape[1:]) if t.shape[0] == 1 else t

    lse_tok = jnp.stack([to_tok(_dil_lse(*views[p], bias_t[p], f"dilated_lse_{d}"), d)
                         for p, d in enumerate(DILATIONS)])
    lse_joint = _lse_join(lse_tok, "dilated_lse_join")
    lse_perm = [to_perm(lse_joint, d) for d in DILATIONS]
    o_dil = [tok_or_res(_dil_out(*views[p], bias_t[p], lse_perm[p], f"dilated_out_{d}"))
             for p, d in enumerate(DILATIONS)]
    o_cat = _sum_cast_cols([[_from_heads_t(o_fox_t)]] + [[o] for o in o_dil], BF16, "mixer_out_cat")
    w_out_b = W["w_out"]
    w_out_cat = jnp.concatenate([w_out_b[:FOX_WIDTH]] + [w_out_b[FOX_WIDTH:]] * 3, axis=0)
    a = _mm(o_cat, w_out_cat, "nn", F32, "proj_out", tk=2048)
    x1, h2 = _resid_norm(xs, a, g_mix_post, g_xattn_pre, "resid_mix")

    hm = _rms_fwd(mem[0], g_mem, "rms_mem")
    q2 = _mm(h2, W["w_xq"], "nn", BF16, "xattn_q")
    w_xkv = jnp.concatenate([W["w_xk"], W["w_xv"]], axis=1)
    kvm = _mm(hm, w_xkv, "nn", BF16, "xattn_kv")
    MW = N_MEM_HEADS * HEAD_DIM
    oc, lse_mem = _mem_fwd(q2, kvm, "xattn_fwd")
    y2 = _mm(oc, W["w_xo"], "nn", F32, "xattn_o")
    x2, h3 = _resid_norm(x1, y2, g_xattn_post, g_ffn_pre, "resid_xattn")

    gu, act = _ffn_up(h3, w_gu, "ffn_up")
    y3 = _mm(act, W["w_down"], "nn", F32, "ffn_down", tk=2816)
    dx3, loss_tile = _final_loss(x2, y3, g_ffn_post, loss_target[0], "final_loss")

    grads = {}
    small = {}
    _, dy3_b, dg = _rms_bwd(y3, g_ffn_post, dx3, None, "bwd_norm_ffn_post", want=("bf16",))
    small["g_ffn_post"] = dg[0:1]
    grads["w_down"] = _mm(act, dy3_b, "tn", F32, "grad_w_down", tm=1408)
    dgu = _ffn_dact(dy3_b, W["w_down"], gu, "ffn_dact")
    dw_gu = _mm(h3, dgu, "tn", F32, "grad_w_gu", tn=1408).reshape(D, nft, 2, FF_TILE)
    grads["w_gate"], grads["w_up"] = dw_gu[:, :, 0].reshape(D, F), dw_gu[:, :, 1].reshape(D, F)
    dh3 = _mm(dgu, w_gu, "nt", F32, "bwd_ffn_in", tk=1408)
    dx2, _, dg = _rms_bwd(x2, g_ffn_pre, dh3, dx3, "bwd_norm_ffn_pre", want=("f32",))
    small["g_ffn_pre"] = dg[0:1]

    _, dy2_b, dg = _rms_bwd(y2, g_xattn_post, dx2, None, "bwd_norm_xattn_post", want=("bf16",))
    small["g_xattn_post"] = dg[0:1]
    grads["w_xo"] = _mm(oc, dy2_b, "tn", F32, "grad_w_xo")
    doc = _mm(dy2_b, W["w_xo"], "nt", BF16, "bwd_xattn_o")
    delta_mem = _head_rowdot(doc, [oc], "xattn_delta")[:, :N_MEM_HEADS].T.reshape(N_MEM_HEADS // 2, 2, S)
    dq2, dkm, dvm = _mem_bwd(q2, kvm, doc, lse_mem, delta_mem, "xattn_bwd")
    dkvm = jnp.concatenate([dkm, dvm], axis=1).astype(BF16)
    grads["w_xq"] = _mm(h2, dq2, "tn", F32, "grad_w_xq")
    dw_xkv = _mm(hm, dkvm, "tn", F32, "grad_w_xkv")
    grads["w_xk"], grads["w_xv"] = dw_xkv[:, :MW], dw_xkv[:, MW:]
    dhm = _mm(dkvm, w_xkv, "nt", F32, "bwd_xattn_kv")
    _, _, dg = _rms_bwd(mem[0], g_mem, dhm, None, "bwd_norm_mem", want=())
    small["g_mem"] = dg[0:1]
    dh2 = _mm(dq2, W["w_xq"], "nt", F32, "bwd_xattn_q")
    dx1, _, dg = _rms_bwd(x1, g_xattn_pre, dh2, dx2, "bwd_norm_xattn_pre", want=("f32",))
    small["g_xattn_pre"] = dg[0:1]

    _, da_b, dg = _rms_bwd(a, g_mix_post, dx1, None, "bwd_norm_mix_post", want=("bf16",))
    small["g_mix_post"] = dg[0:1]
    dw_out_cat = _mm(o_cat, da_b, "tn", F32, "grad_w_out")
    dw_out_dil = _add_n([dw_out_cat[FOX_WIDTH + p * DIL_WIDTH:FOX_WIDTH + (p + 1) * DIL_WIDTH] for p in range(3)],
                        "grad_w_out_dil")
    grads["w_out"] = jnp.concatenate([dw_out_cat[:FOX_WIDTH], dw_out_dil], axis=0)
    do = _mm(da_b, w_out_b, "nt", BF16, "bwd_proj_out")
    do_fox, do_dil = do[:, :FOX_WIDTH], do[:, FOX_WIDTH:]

    delta_fox = _head_rowdot(do_fox, [o_cat[:, :FOX_WIDTH]], "fox_delta")[:, :N_FOX_HEADS].T
    qa_b = _rows_operand(fqt, [inv_scale] * EXTRA + list(_split3(lse_fox[:, 0] * (-1.0 / QK_SCALE))))
    va = _lanes_operand(_to_heads(fv_s, N_FOX_HEADS), [unit] * EXTRA)
    doa = _rows_operand(_to_heads_t(do_fox, N_FOX_HEADS), list(_split3(-delta_fox)))
    dq_aug, dk_aug, dvf = _fox_bwd(qa_b, ka, ka.transpose(0, 2, 1), va, doa, "fox_bwd")
    dqf, dkf = dq_aug[:, :HEAD_DIM], dk_aug[:, :HEAD_DIM]
    dfg_t, db_f = _forget_bwd(fg_t, b_col, dq_aug[:, HEAD_DIM + EXTRA], dk_aug[:, HEAD_DIM], "forget_bwd")

    delta_dil = _head_rowdot(do_dil, o_dil, "dilated_delta")[:, :N_DIL_HEADS].T
    do_res = [do_dil.reshape(1, S, DIL_WIDTH)] + [
        _to_residues(do, 1, DIL_WIDTH, d, f"dilated_do_residues_{d}") for d in DILATIONS[1:]]
    dil_grads = [_dil_bwd(*views[p], do_res[p], bias_t[p], lse_perm[p], to_perm(delta_dil, d), f"dilated_bwd_{d}")
                 for p, d in enumerate(DILATIONS)]
    dbias_t = jnp.stack([g[3].reshape(HEAD_PAIRS, 2 * BAND, 2, BAND).transpose(0, 2, 1, 3).reshape(
        N_DIL_HEADS, 2 * BAND, BAND) for g in dil_grads])
    d_rel = _bucket_reduce(dbias_t, jnp.asarray(bucket_map.transpose(0, 2, 1)), "rel_bias_grad")[:, :N_DIL_HEADS]
    dfg_pad = jnp.pad(dfg_t.T, ((0, 0), (0, LANES - N_FOX_HEADS))).astype(BF16)
    dcat = _sum_cast_cols([[_from_heads_t(dqf)], [_from_heads_t(dkf)], [_from_heads_t(dvf)]]
                          + [[tok_or_res(g[j]) for g in dil_grads] for j in range(3)],
                          BF16, "dqkv_assemble", tail=dfg_pad)
    dw_cat = _mm(h1, dcat, "tn", F32, "grad_w_qkv", tm=512, tn=3200)
    n_qkv = 3 * (FOX_WIDTH + DIL_WIDTH)
    grads["w_in"] = jnp.concatenate([dw_cat[:, :3 * FOX_WIDTH], dw_cat[:, n_qkv:n_qkv + N_FOX_HEADS],
                                     dw_cat[:, 3 * FOX_WIDTH:n_qkv]], axis=1)
    w_cat = jnp.concatenate([w_qkv, w_fg_pad], axis=1)
    dh1 = _mm(dcat, w_cat, "nt", F32, "bwd_proj_in", tk=3200)
    grad_x, _, dg = _rms_bwd(xs, g_mix_pre, dh1, dx1, "bwd_norm_mix_pre", want=("f32",))
    small["g_mix_pre"] = dg[0:1]
    small["b_f"] = db_f[:, 0].reshape(1, N_FOX_HEADS)
    small["rel_bias"] = d_rel

    split = {n: _split_weight(grads[n], n) for n in big}
    parts = jnp.stack([_pack({n: split[n][j] for n in big}) for j in range(N_CHIPS)])
    R = parts.shape[1]
    half = R // 2
    keep = lax.dynamic_slice_in_dim(parts, my_c * half, half, axis=1)
    give = lax.dynamic_slice_in_dim(parts, (1 - my_c) * half, half, axis=1)
    got = _sibling_exchange(give, "grads_to_sibling")
    chip_sum = _add_n([keep.reshape(-1, PACK_COLS), got.reshape(-1, PACK_COLS)], "grads_add_sibling")
    chip_sum = chip_sum.reshape(N_CHIPS, half, PACK_COLS)
    my_chip = 2 * my_x + my_y
    from_chips = _chip_scatter(chip_sum.astype(BF16), "grads_to_chips")
    own = lax.dynamic_index_in_dim(chip_sum, my_chip, axis=0, keepdims=False)
    g_half = _add_n([own, from_chips[0], from_chips[1], from_chips[2]], "grads_add_chips")
    other_half = _sibling_exchange(g_half, "grads_share_sibling")
    g_pack = jnp.where(my_c == 0, jnp.concatenate([g_half, other_half]), jnp.concatenate([other_half, g_half]))

    small_pack = _pack_small(small)
    small_pack = small_pack.at[8, 0].set(loss_tile[0, 0])
    everyone = _all_to_all_small(small_pack, "small_all_gather")
    small_sum = _add_n([everyone[i] for i in range(8)], "small_sum")
    loss = small_sum[8, 0]
    g_small = _unpack_small(small_sum)

    outs = {"grad": _unpack(g_pack, shard_shapes), "delta": {}, "new_m": {}, "new_v": {}}
    for n in big:
        outs["delta"][n], outs["new_m"][n], outs["new_v"][n] = _adamw(
            args[n][0], outs["grad"][n], args["m_" + n][0], args["v_" + n][0], f"adamw_{n}")
    sw = _pack_small({n: args[n] for n in _SMALL + ("b_f", "rel_bias")})
    sm = _pack_small({n: args["m_" + n] for n in _SMALL + ("b_f", "rel_bias")})
    sv = _pack_small({n: args["v_" + n] for n in _SMALL + ("b_f", "rel_bias")})
    sd, snm, snv = _adamw(sw, small_sum.at[8, 0].set(0.0), sm, sv, "adamw_small")
    souts = {"grad": g_small, "delta": _unpack_small(sd), "new_m": _unpack_small(snm), "new_v": _unpack_small(snv)}

    def leaf(kind, n):
        if n in souts[kind]:
            return souts[kind][n].reshape(args[n].shape)
        return outs[kind][n].reshape(args[n].shape)

    result = [loss, grad_x.reshape(x.shape)]
    for kind in ("grad", "delta", "new_m", "new_v"):
        result += [leaf(kind, n) for n in names]
    return tuple(result)
```

```python
import numpy as np
import jax
import jax.numpy as jnp
from jax import lax
from jax.experimental import pallas as pl
from jax.experimental.pallas import tpu as pltpu

F32 = jnp.float32
BF16 = jnp.bfloat16
MESH_IDS = pl.DeviceIdType.MESH

LANES = 128
HEAD_DIM = 64
N_FOX_HEADS = 8
N_DIL_HEADS = 8
N_MEM_HEADS = 4
FOX_WIDTH = N_FOX_HEADS * HEAD_DIM
DIL_WIDTH = N_DIL_HEADS * HEAD_DIM
DILATIONS = (1, 4, 16)
BAND = 128
BAND_CHUNK_MAX = 8 * BAND
N_BUCKETS = 32
MAX_DISTANCE = 2048
QK_SCALE = HEAD_DIM ** -0.5
RMS_EPS = 1e-6
NEG = -1e30
VMEM_LIMIT = 56 << 20

ADAM_LR = 0.001
ADAM_B1 = 0.9
ADAM_B2 = 0.999
ADAM_EPS = 1e-08
ADAM_WD = 0.01
ADAM_STEP = 10

N_CHIPS = 4
PACK_COLS = 1024


def _params(*sem):
    return pltpu.CompilerParams(dimension_semantics=sem, vmem_limit_bytes=VMEM_LIMIT)


def _fit(n, cap):
    if n <= cap:
        return n
    t = (cap // LANES) * LANES
    while t >= LANES:
        if n % t == 0:
            return t
        t -= LANES
    raise ValueError(f"no lane-aligned tile for {n} under {cap}")


def _dot(a, b, dims):
    return lax.dot_general(a, b, (dims, ((), ())), preferred_element_type=F32)


_NN = ((1,), (0,))
_NT = ((1,), (1,))
_TN = ((0,), (0,))


def _mm(a, b, mode, out_dtype, name, tm=1024, tn=1024, tk=1024):
    if mode == "nn":
        (M, K), N = a.shape, b.shape[1]
    elif mode == "nt":
        (M, K), N = a.shape, b.shape[0]
    else:
        (K, M), N = a.shape, b.shape[1]
    tm, tn, tk = _fit(M, tm), _fit(N, tn), _fit(K, tk)
    nk = K // tk
    if mode == "tn":
        a_spec = pl.BlockSpec((tk, tm), lambda i, j, k: (k, i))
    else:
        a_spec = pl.BlockSpec((tm, tk), lambda i, j, k: (i, k))
    if mode == "nt":
        b_spec = pl.BlockSpec((tn, tk), lambda i, j, k: (j, k))
    else:
        b_spec = pl.BlockSpec((tk, tn), lambda i, j, k: (k, j))
    dims = {"nn": _NN, "nt": _NT, "tn": _TN}[mode]

    def body(a_ref, b_ref, o_ref, *acc):
        prod = _dot(a_ref[...].astype(BF16), b_ref[...].astype(BF16), dims)
        if nk == 1:
            o_ref[...] = prod.astype(o_ref.dtype)
            return
        acc_ref, k = acc[0], pl.program_id(2)

        @pl.when(k == 0)
        def _():
            acc_ref[...] = prod

        @pl.when(k > 0)
        def _():
            acc_ref[...] += prod

        @pl.when(k == nk - 1)
        def _():
            o_ref[...] = acc_ref[...].astype(o_ref.dtype)

    return pl.pallas_call(
        body, name=name, grid=(M // tm, N // tn, nk),
        in_specs=[a_spec, b_spec],
        out_specs=pl.BlockSpec((tm, tn), lambda i, j, k: (i, j)),
        out_shape=jax.ShapeDtypeStruct((M, N), out_dtype),
        scratch_shapes=[pltpu.VMEM((tm, tn), F32)] if nk > 1 else [],
        compiler_params=_params("parallel", "parallel", "arbitrary"),
    )(a, b)


def _rms_rows(x):
    return lax.rsqrt(jnp.mean(x * x, axis=-1, keepdims=True) + RMS_EPS)


def _rms_fwd(x, g, name, tr=512):
    S, D = x.shape
    tr = _fit(S, tr)

    def body(x_ref, g_ref, h_ref):
        xv = x_ref[...]
        h_ref[...] = (xv * _rms_rows(xv) * g_ref[...]).astype(BF16)

    return pl.pallas_call(
        body, name=name, grid=(S // tr,),
        in_specs=[pl.BlockSpec((tr, D), lambda i: (i, 0)), pl.BlockSpec((1, D), lambda i: (0, 0))],
        out_specs=pl.BlockSpec((tr, D), lambda i: (i, 0)),
        out_shape=jax.ShapeDtypeStruct((S, D), BF16),
        compiler_params=_params("parallel"),
    )(x, g)


def _resid_norm(xres, y, g_post, g_next, name, tr=512):
    S, D = xres.shape
    tr = _fit(S, tr)

    def body(x_ref, y_ref, gp_ref, gn_ref, xn_ref, h_ref):
        yv = y_ref[...]
        xn = x_ref[...] + yv * _rms_rows(yv) * gp_ref[...]
        xn_ref[...] = xn
        h_ref[...] = (xn * _rms_rows(xn) * gn_ref[...]).astype(BF16)

    row = pl.BlockSpec((tr, D), lambda i: (i, 0))
    vec = pl.BlockSpec((1, D), lambda i: (0, 0))
    return pl.pallas_call(
        body, name=name, grid=(S // tr,),
        in_specs=[row, row, vec, vec], out_specs=[row, row],
        out_shape=[jax.ShapeDtypeStruct((S, D), F32), jax.ShapeDtypeStruct((S, D), BF16)],
        compiler_params=_params("parallel"),
    )(xres, y, g_post, g_next)


def _final_loss(xres, y, g_post, target, name, tr=512):
    S, D = xres.shape
    tr = _fit(S, tr)

    def body(x_ref, y_ref, gp_ref, t_ref, d_ref, loss_ref):
        i = pl.program_id(0)
        yv = y_ref[...]
        err = x_ref[...] + yv * _rms_rows(yv) * gp_ref[...] - t_ref[...]
        d_ref[...] = err * (1.0 / D)

        @pl.when(i == 0)
        def _():
            loss_ref[...] = jnp.zeros_like(loss_ref)

        part = jnp.sum(jnp.sum(err * err, axis=1, keepdims=True), axis=0, keepdims=True)
        loss_ref[...] += jnp.broadcast_to(part * (0.5 / D), loss_ref.shape)

    row = pl.BlockSpec((tr, D), lambda i: (i, 0))
    vec = pl.BlockSpec((1, D), lambda i: (0, 0))
    return pl.pallas_call(
        body, name=name, grid=(S // tr,),
        in_specs=[row, row, vec, row],
        out_specs=[row, pl.BlockSpec((8, LANES), lambda i: (0, 0))],
        out_shape=[jax.ShapeDtypeStruct((S, D), F32), jax.ShapeDtypeStruct((8, LANES), F32)],
        compiler_params=_params("arbitrary"),
    )(xres, y, g_post, target)


def _rms_bwd(xin, g, dy, dres, name, want=("f32", "bf16"), tr=512):
    S, D = xin.shape
    tr = _fit(S, tr)
    has_res = dres is not None

    def body(*refs):
        refs = list(refs)
        dg_ref = refs.pop()
        dxb_ref = refs.pop() if "bf16" in want else None
        dx_ref = refs.pop() if "f32" in want else None
        dr_ref = refs.pop() if has_res else None
        x_ref, g_ref, dy_ref = refs
        i = pl.program_id(0)
        xv = x_ref[...]
        dyv = dy_ref[...].astype(F32)
        xhat = xv * _rms_rows(xv)
        dxhat = dyv * g_ref[...]
        r = _rms_rows(xv)
        dx = r * (dxhat - xhat * jnp.mean(dxhat * xhat, axis=-1, keepdims=True))
        if has_res:
            dx = dx + dr_ref[...]
        if dx_ref is not None:
            dx_ref[...] = dx
        if dxb_ref is not None:
            dxb_ref[...] = dx.astype(BF16)

        @pl.when(i == 0)
        def _():
            dg_ref[...] = jnp.zeros_like(dg_ref)

        dg_ref[...] += jnp.broadcast_to(jnp.sum(dyv * xhat, axis=0, keepdims=True), dg_ref.shape)

    row = pl.BlockSpec((tr, D), lambda i: (i, 0))
    vec = pl.BlockSpec((1, D), lambda i: (0, 0))
    acc = pl.BlockSpec((8, D), lambda i: (0, 0))
    ins = [xin, g, dy] + ([dres] if has_res else [])
    dtypes = [dt for key, dt in (("f32", F32), ("bf16", BF16)) if key in want]
    outs = pl.pallas_call(
        body, name=name, grid=(S // tr,),
        in_specs=[row, vec, row] + ([row] if has_res else []),
        out_specs=[row] * len(dtypes) + [acc],
        out_shape=[jax.ShapeDtypeStruct((S, D), dt) for dt in dtypes] + [jax.ShapeDtypeStruct((8, D), F32)],
        compiler_params=_params("arbitrary"),
    )(*ins)
    by_key = dict(zip([key for key in ("f32", "bf16") if key in want], outs[:-1]))
    return by_key.get("f32"), by_key.get("bf16"), outs[-1]


def _tri(n, upper):
    r = lax.broadcasted_iota(jnp.int32, (n, n), 0)
    c = lax.broadcasted_iota(jnp.int32, (n, n), 1)
    return jnp.where((r <= c) if upper else (r >= c), 1.0, 0.0).astype(F32)


def _forget_fwd(fg_t, b_col, name, ts=512):
    H, S = fg_t.shape
    ts = _fit(S, ts)

    def body(f_ref, b_ref, c_ref, carry_ref):
        i = pl.program_id(0)

        @pl.when(i == 0)
        def _():
            carry_ref[...] = jnp.zeros_like(carry_ref)

        z = f_ref[...] + b_ref[...]
        logf = jnp.minimum(z, 0.0) - jnp.log(1.0 + jnp.exp(-jnp.abs(z)))
        run = lax.dot_general(logf, _tri(ts, True), (_NN, ((), ())), precision=lax.Precision.HIGHEST,
                              preferred_element_type=F32) + carry_ref[:, 0:1]
        c_ref[...] = run
        carry_ref[...] = jnp.broadcast_to(
            carry_ref[:, 0:1] + jnp.sum(logf, axis=1, keepdims=True), carry_ref.shape)

    return pl.pallas_call(
        body, name=name, grid=(S // ts,),
        in_specs=[pl.BlockSpec((H, ts), lambda i: (0, i)), pl.BlockSpec((H, 1), lambda i: (0, 0))],
        out_specs=pl.BlockSpec((H, ts), lambda i: (0, i)),
        out_shape=jax.ShapeDtypeStruct((H, S), F32),
        scratch_shapes=[pltpu.VMEM((H, LANES), F32)],
        compiler_params=_params("arbitrary"),
    )(fg_t, b_col)


def _forget_bwd(fg_t, b_col, dc_plus, dc_minus, name, ts=512):
    H, S = fg_t.shape
    ts = _fit(S, ts)
    nb = S // ts

    def body(f_ref, b_ref, dcp_ref, dcm_ref, df_ref, db_ref, carry_ref):
        i = pl.program_id(0)

        @pl.when(i == 0)
        def _():
            carry_ref[...] = jnp.zeros_like(carry_ref)
            db_ref[...] = jnp.zeros_like(db_ref)

        dc = dcp_ref[...] - dcm_ref[...]
        suffix = lax.dot_general(dc, _tri(ts, False), (_NN, ((), ())), precision=lax.Precision.HIGHEST,
                                 preferred_element_type=F32) + carry_ref[:, 0:1]
        z = f_ref[...] + b_ref[...]
        sig_neg = 1.0 / (1.0 + jnp.exp(z))
        df = suffix * sig_neg
        df_ref[...] = df
        carry_ref[...] = jnp.broadcast_to(
            carry_ref[:, 0:1] + jnp.sum(dc, axis=1, keepdims=True), carry_ref.shape)
        db_ref[...] += jnp.broadcast_to(jnp.sum(df, axis=1, keepdims=True), db_ref.shape)

    rev = pl.BlockSpec((H, ts), lambda i: (0, nb - 1 - i))
    return pl.pallas_call(
        body, name=name, grid=(nb,),
        in_specs=[rev, pl.BlockSpec((H, 1), lambda i: (0, 0)), rev, rev],
        out_specs=[rev, pl.BlockSpec((H, LANES), lambda i: (0, 0))],
        out_shape=[jax.ShapeDtypeStruct((H, S), F32), jax.ShapeDtypeStruct((H, LANES), F32)],
        scratch_shapes=[pltpu.VMEM((H, LANES), F32)],
        compiler_params=_params("arbitrary"),
    )(fg_t, b_col, dc_plus, dc_minus)


ONES_ROWS = 16
EXTRA = 3


def _split3(x):
    hi = lax.reduce_precision(x, 8, 7)
    mid = lax.reduce_precision(x - hi, 8, 7)
    lo = lax.reduce_precision(x - hi - mid, 8, 7)
    return hi.astype(BF16), mid.astype(BF16), lo.astype(BF16)


def _lanes_operand(t, extras):
    block = jnp.pad(jnp.stack(extras, axis=-1), ((0, 0), (0, 0), (0, LANES - HEAD_DIM - len(extras))))
    return jnp.concatenate([t, block], axis=-1)


def _rows_operand(t, extras):
    block = jnp.pad(jnp.stack(extras, axis=1), ((0, 0), (0, LANES - HEAD_DIM - len(extras)), (0, 0)))
    return jnp.concatenate([t, block], axis=1)


def _with_ones(t):
    return jnp.concatenate([t, jnp.ones((t.shape[0], ONES_ROWS, t.shape[2]), t.dtype)], axis=1)


def _fox_fwd(qa, ka, vt, name, tq=512, tk=1024):
    H, _, S = qa.shape
    Dh = HEAD_DIM
    tk = _fit(S, tk)
    tq = _fit(tk, tq)
    ratio = tk // tq

    def body(qa_ref, ka_ref, vt_ref, o_ref, lse_ref, m_ref, acc_ref, sa_ref, sb_ref, ta_ref, tb_ref):
        i = pl.program_id(1)
        qv = qa_ref[...] * QK_SCALE
        m_ref[...] = jnp.full_like(m_ref, NEG)
        acc_ref[...] = jnp.zeros_like(acc_ref)
        n = i // ratio
        q_off = (i - n * ratio) * tq

        def scores(j, s_ref, t_ref, diagonal):
            off = pl.multiple_of(j * tk, LANES)
            s = _dot(ka_ref[pl.ds(off, tk), :], qv, _NN)
            if diagonal:
                key = lax.broadcasted_iota(jnp.int32, (tk, tq), 0)
                qry = lax.broadcasted_iota(jnp.int32, (tk, tq), 1) + q_off
                s = jnp.where(key <= qry, s, NEG)
            s_ref[...] = s
            t_ref[...] = jnp.max(s, axis=0, keepdims=True)

        def absorb(j, s_ref, t_ref):
            off = pl.multiple_of(j * tk, LANES)
            m_old = m_ref[...]
            m_new = jnp.maximum(m_old, t_ref[...])
            p = jnp.exp(s_ref[...] - m_new)
            alpha = jnp.exp(m_old - m_new)
            acc_ref[...] = alpha * acc_ref[...] + _dot(vt_ref[:, pl.ds(off, tk)], p.astype(BF16), _NN)
            m_ref[...] = m_new

        scores(n, sa_ref, ta_ref, True)

        def loop_body(jj, carry):
            scores(2 * jj, sb_ref, tb_ref, False)
            absorb(jnp.where(jj == 0, n, 2 * jj - 1), sa_ref, ta_ref)
            scores(2 * jj + 1, sa_ref, ta_ref, False)
            absorb(2 * jj, sb_ref, tb_ref)
            return carry

        pairs = n // 2
        lax.fori_loop(0, pairs, loop_body, 0)
        held = jnp.where(pairs == 0, n, 2 * pairs - 1)

        @pl.when(n % 2 == 1)
        def _():
            scores(n - 1, sb_ref, tb_ref, False)
            absorb(held, sa_ref, ta_ref)
            absorb(n - 1, sb_ref, tb_ref)

        @pl.when(n % 2 == 0)
        def _():
            absorb(held, sa_ref, ta_ref)

        l = acc_ref[Dh:Dh + 1, :]
        o_ref[...] = acc_ref[0:Dh, :] / l
        lse_ref[...] = m_ref[...] + jnp.log(l)

    return pl.pallas_call(
        body, name=name, grid=(H, S // tq),
        in_specs=[pl.BlockSpec((None, LANES, tq), lambda h, i: (h, 0, i)),
                  pl.BlockSpec((None, S, LANES), lambda h, i: (h, 0, 0)),
                  pl.BlockSpec((None, Dh + ONES_ROWS, S), lambda h, i: (h, 0, 0))],
        out_specs=[pl.BlockSpec((None, Dh, tq), lambda h, i: (h, 0, i)),
                   pl.BlockSpec((None, 1, tq), lambda h, i: (h, 0, i))],
        out_shape=[jax.ShapeDtypeStruct((H, Dh, S), F32), jax.ShapeDtypeStruct((H, 1, S), F32)],
        scratch_shapes=[pltpu.VMEM((1, tq), F32), pltpu.VMEM((Dh + ONES_ROWS, tq), F32),
                        pltpu.VMEM((tk, tq), F32), pltpu.VMEM((tk, tq), F32),
                        pltpu.VMEM((1, tq), F32), pltpu.VMEM((1, tq), F32)],
        compiler_params=_params("parallel", "arbitrary"),
    )(qa, ka, vt)


def _fox_bwd(qa, ka, kta, va, doa, name, tq=1024, tk=512):
    H, _, S = qa.shape
    Dh, Da = HEAD_DIM, HEAD_DIM + ONES_ROWS
    tq = _fit(S, tq)
    tk = _fit(tq, tk)
    ratio = tq // tk
    nq = S // tq
    nk = S // tk

    def body(ka_ref, kta_ref, va_ref, qa_ref, doa_ref, dqt_ref, dkt_ref, dvt_ref, dka_ref, dva_ref):
        j = pl.program_id(1)

        @pl.when(j == 0)
        def _():
            dqt_ref[...] = jnp.zeros_like(dqt_ref)

        kv = ka_ref[...]
        ktv = kta_ref[0:Da, :]
        vv = va_ref[...]
        dka_ref[...] = jnp.zeros_like(dka_ref)
        dva_ref[...] = jnp.zeros_like(dva_ref)
        i_diag = j // ratio
        k_off = (j - i_diag * ratio) * tk

        def step(i, diagonal):
            off = pl.multiple_of(i * tq, LANES)
            qv = qa_ref[:, pl.ds(off, tq)] * QK_SCALE
            dov = doa_ref[:, pl.ds(off, tq)]
            e = _dot(kv, qv, _NN)
            if diagonal:
                key = lax.broadcasted_iota(jnp.int32, (tk, tq), 0) + k_off
                qry = lax.broadcasted_iota(jnp.int32, (tk, tq), 1)
                e = jnp.where(key <= qry, e, NEG)
            p_t = jnp.exp(e)
            dva_ref[...] += _dot(dov[0:Dh, :], p_t.astype(BF16), _NT)
            ds_b = (p_t * _dot(vv, dov, _NN)).astype(BF16)
            dka_ref[...] += _dot(qv[0:Da, :], ds_b, _NT)
            dqt_ref[:, pl.ds(off, tq)] += _dot(ktv, ds_b, _NN)

        step(i_diag, True)

        def loop_body(i, carry):
            step(i, False)
            return carry

        lax.fori_loop(i_diag + 1, nq, loop_body, 0)
        dkt_ref[...] = dka_ref[...]
        dvt_ref[...] = dva_ref[...]

        @pl.when(j == nk - 1)
        def _():
            dqt_ref[0:Dh, :] = dqt_ref[0:Dh, :] * QK_SCALE

    lanes_tile = pl.BlockSpec((None, tk, LANES), lambda h, j: (h, j, 0))
    rows_tile = pl.BlockSpec((None, LANES, tk), lambda h, j: (h, 0, j))
    rows_full = pl.BlockSpec((None, LANES, S), lambda h, j: (h, 0, 0))
    return pl.pallas_call(
        body, name=name, grid=(H, nk),
        in_specs=[lanes_tile, rows_tile, lanes_tile, rows_full, rows_full],
        out_specs=[pl.BlockSpec((None, Da, S), lambda h, j: (h, 0, 0)),
                   pl.BlockSpec((None, Da, tk), lambda h, j: (h, 0, j)),
                   pl.BlockSpec((None, Dh, tk), lambda h, j: (h, 0, j))],
        out_shape=[jax.ShapeDtypeStruct((H, Da, S), F32), jax.ShapeDtypeStruct((H, Da, S), F32),
                   jax.ShapeDtypeStruct((H, Dh, S), F32)],
        scratch_shapes=[pltpu.VMEM((Da, tk), F32), pltpu.VMEM((Dh, tk), F32)],
        compiler_params=_params("parallel", "arbitrary"),
    )(ka, kta, va, qa, doa)


DIL_Q_BLOCK = 3 * FOX_WIDTH // LANES
HEAD_PAIRS = N_DIL_HEADS // 2
PAIR_BLOCKS = DIL_WIDTH // LANES


def _band_geometry(S, d):
    L = S // d
    chunk = min(BAND_CHUNK_MAX, L)
    assert L % chunk == 0 and chunk % BAND == 0
    return L, chunk, chunk // BAND, L // chunk


def _band_in_specs(S, d, base):
    L, chunk, nb, _ = _band_geometry(S, d)

    def col(kind):
        return lambda hp, r, i: (r, i, base + kind * PAIR_BLOCKS + hp)

    def col_prev(kind):
        return lambda hp, r, i: (r, jnp.maximum(i * nb - 1, 0), base + kind * PAIR_BLOCKS + hp)

    main = [pl.BlockSpec((None, chunk, LANES), col(kind)) for kind in range(3)]
    prev = [pl.BlockSpec((None, BAND, LANES), col_prev(kind)) for kind in range(3)]
    bias = pl.BlockSpec((None, 2 * BAND, 2 * BAND), lambda hp, r, i: (hp, 0, 0))
    stat = pl.BlockSpec((None, 2, chunk), lambda hp, r, i: (hp, 0, r * (L // chunk) + i))
    tok = pl.BlockSpec((None, chunk, LANES), lambda hp, r, i: (r, i, hp))
    return main, prev, bias, stat, tok


def _to_residues(x, col_block, width, d, name, tr=512):
    S = x.shape[0]
    tr = _fit(S, tr)

    def body(x_ref, o_ref, tmp_ref):
        for j in range(width // LANES):
            cols = slice(j * LANES, (j + 1) * LANES)
            tmp_ref[j] = x_ref[:, cols].astype(F32)
            for r in range(d):
                o_ref[r, :, cols] = tmp_ref[j, pl.ds(r, tr // d, stride=d), :].astype(o_ref.dtype)

    return pl.pallas_call(
        body, name=name, grid=(S // tr,),
        in_specs=[pl.BlockSpec((tr, width), lambda i: (i, col_block))],
        out_specs=pl.BlockSpec((d, tr // d, width), lambda i: (0, i, 0)),
        out_shape=jax.ShapeDtypeStruct((d, S // d, width), x.dtype),
        scratch_shapes=[pltpu.VMEM((width // LANES, tr, LANES), F32)],
        compiler_params=_params("parallel"),
    )(x)


def _token_rows(ref, cols, tmp_ref):
    if len(ref.shape) == 2:
        return ref[:, cols].astype(F32)
    d, rows = ref.shape[0], ref.shape[1]
    for r in range(d):
        tmp_ref[pl.ds(r, rows, stride=d), :] = ref[r, :, cols].astype(F32)
    return tmp_ref[...]


def _row_spec(t, tr):
    if t.ndim == 2:
        return pl.BlockSpec((tr, t.shape[1]), lambda i: (i, 0))
    d = t.shape[0]
    return pl.BlockSpec((d, tr // d, t.shape[2]), lambda i: (0, i, 0))


def _head_lanes(a):
    return lax.broadcasted_iota(jnp.int32, (1, LANES), 1) // HEAD_DIM == a


def _one_head(x, a):
    return jnp.where(_head_lanes(a), x, jnp.zeros_like(x))


def _head_stack(x):
    return jnp.concatenate([_one_head(x, 0), _one_head(x, 1)], axis=0)


def _pair_rows(ref, rows):
    return jnp.concatenate([ref[0:1, rows], ref[1:2, rows]], axis=1)


def _band_scores_t(kb, q_stack, bias_t, first):
    s = _dot(kb, q_stack, _NT) + bias_t
    if first is not None:
        key = lax.broadcasted_iota(jnp.int32, s.shape, 0)
        s = jnp.where(jnp.logical_and(first, key < BAND), NEG, s)
    return s


def _pair_select(stacked):
    return jnp.where(_head_lanes(0), stacked[0:BAND, :], stacked[BAND:, :])


def _dil_lse(qkv_v, base, bias_t, name):
    d, L = qkv_v.shape[:2]
    S = L * d
    _, chunk, nb, nchunks = _band_geometry(S, d)
    main, prev, bias, stat, _ = _band_in_specs(S, d, base)

    def body(q_ref, k_ref, kp_ref, b_ref, lse_ref, kext_ref):
        first = pl.program_id(2) == 0
        kext_ref[0:BAND, :] = kp_ref[...]
        kext_ref[BAND:, :] = k_ref[...]
        for b in range(nb):
            rows, ext = slice(b * BAND, (b + 1) * BAND), slice(b * BAND, (b + 2) * BAND)
            s = _band_scores_t(kext_ref[ext, :], _head_stack(q_ref[rows, :] * QK_SCALE), b_ref[...],
                               first if b == 0 else None)
            m = jnp.max(s, axis=0, keepdims=True)
            lse = m + jnp.log(jnp.sum(jnp.exp(s - m), axis=0, keepdims=True))
            lse_ref[0:1, rows] = lse[:, 0:BAND]
            lse_ref[1:2, rows] = lse[:, BAND:]

    return pl.pallas_call(
        body, name=name, grid=(HEAD_PAIRS, d, nchunks),
        in_specs=[main[0], main[1], prev[1], bias], out_specs=stat,
        out_shape=jax.ShapeDtypeStruct((HEAD_PAIRS, 2, S), F32),
        scratch_shapes=[pltpu.VMEM((chunk + BAND, LANES), BF16)],
        compiler_params=_params("parallel", "parallel", "parallel"),
    )(qkv_v, qkv_v, qkv_v, bias_t)


def _dil_out(qkv_v, base, bias_t, lse_joint, name):
    d, L = qkv_v.shape[:2]
    S = L * d
    _, chunk, nb, nchunks = _band_geometry(S, d)
    main, prev, bias, stat, tok = _band_in_specs(S, d, base)

    def body(q_ref, k_ref, kp_ref, v_ref, vp_ref, b_ref, lse_ref, o_ref, kext_ref, vext_ref):
        first = pl.program_id(2) == 0
        kext_ref[0:BAND, :] = kp_ref[...]
        kext_ref[BAND:, :] = k_ref[...]
        vext_ref[0:BAND, :] = vp_ref[...]
        vext_ref[BAND:, :] = v_ref[...]
        for b in range(nb):
            rows, ext = slice(b * BAND, (b + 1) * BAND), slice(b * BAND, (b + 2) * BAND)
            s = _band_scores_t(kext_ref[ext, :], _head_stack(q_ref[rows, :] * QK_SCALE), b_ref[...],
                               first if b == 0 else None)
            p_t = jnp.exp(s - _pair_rows(lse_ref, rows))
            o_ref[rows, :] = _pair_select(_dot(p_t.astype(BF16), vext_ref[ext, :], _TN)).astype(BF16)

    return pl.pallas_call(
        body, name=name, grid=(HEAD_PAIRS, d, nchunks),
        in_specs=[main[0], main[1], prev[1], main[2], prev[2], bias, stat], out_specs=tok,
        out_shape=jax.ShapeDtypeStruct((d, L, DIL_WIDTH), BF16),
        scratch_shapes=[pltpu.VMEM((chunk + BAND, LANES), BF16), pltpu.VMEM((chunk + BAND, LANES), BF16)],
        compiler_params=_params("parallel", "parallel", "parallel"),
    )(qkv_v, qkv_v, qkv_v, qkv_v, qkv_v, bias_t, lse_joint)


def _dil_bwd(qkv_v, base, do_v, bias_t, lse_joint, delta, name):
    d, L = qkv_v.shape[:2]
    S = L * d
    _, chunk, nb, nchunks = _band_geometry(S, d)
    main, prev, bias, stat, tok = _band_in_specs(S, d, base)
    nblocks = L // BAND

    def nxt_row(i):
        return jnp.minimum((i + 1) * nb, nblocks - 1)

    q_next = pl.BlockSpec((None, BAND, LANES), lambda hp, r, i: (r, nxt_row(i), base + hp))
    do_next = pl.BlockSpec((None, BAND, LANES), lambda hp, r, i: (r, nxt_row(i), hp))
    stat_next = pl.BlockSpec((None, 2, BAND), lambda hp, r, i: (hp, 0, r * nblocks + nxt_row(i)))

    def body(q_ref, k_ref, kp_ref, v_ref, vp_ref, do_ref, b_ref, lse_ref, dl_ref,
             qn_ref, don_ref, lsen_ref, dln_ref,
             dq_ref, dk_ref, dv_ref, db_ref, kext_ref, vext_ref, dkext_ref, dvext_ref):
        r, i = pl.program_id(1), pl.program_id(2)
        first = i == 0
        has_next = i + 1 < nchunks
        tail = slice(BAND + chunk, 2 * BAND + chunk)
        kext_ref[0:BAND, :] = kp_ref[...]
        kext_ref[BAND:BAND + chunk, :] = k_ref[...]
        kext_ref[tail, :] = jnp.zeros((BAND, LANES), BF16)
        vext_ref[0:BAND, :] = vp_ref[...]
        vext_ref[BAND:BAND + chunk, :] = v_ref[...]
        vext_ref[tail, :] = jnp.zeros((BAND, LANES), BF16)
        dkext_ref[...] = jnp.zeros_like(dkext_ref)
        dvext_ref[...] = jnp.zeros_like(dvext_ref)

        @pl.when(jnp.logical_and(r == 0, i == 0))
        def _():
            db_ref[...] = jnp.zeros_like(db_ref)

        def block(q2, do2, lse_row, dl_row, ext, mask_rows):
            q_stack, do_stack = _head_stack(q2), _head_stack(do2)
            s = _dot(kext_ref[ext, :], q_stack, _NT) + b_ref[...]
            if mask_rows is not None:
                s = jnp.where(mask_rows, NEG, s)
            p_t = jnp.exp(s - lse_row)
            ds_t = p_t * (_dot(vext_ref[ext, :], do_stack, _NT) - dl_row)
            ds_b = ds_t.astype(BF16)
            dkext_ref[ext, :] += _dot(ds_b, q_stack, _NN)
            dvext_ref[ext, :] += _dot(p_t.astype(BF16), do_stack, _NN)
            return ds_t, ds_b

        key = lax.broadcasted_iota(jnp.int32, (2 * BAND, 2 * BAND), 0)
        all_lanes = slice(0, BAND)
        for b in range(nb):
            rows, ext = slice(b * BAND, (b + 1) * BAND), slice(b * BAND, (b + 2) * BAND)
            mask = jnp.logical_and(first, key < BAND) if b == 0 else None
            ds_t, ds_b = block(q_ref[rows, :] * QK_SCALE, do_ref[rows, :], _pair_rows(lse_ref, rows),
                               _pair_rows(dl_ref, rows), ext, mask)
            dq_ref[rows, :] = _pair_select(_dot(ds_b, kext_ref[ext, :], _TN)) * QK_SCALE
            db_ref[...] += ds_t
        block(qn_ref[...] * QK_SCALE, don_ref[...], _pair_rows(lsen_ref, all_lanes), _pair_rows(dln_ref, all_lanes),
              slice(chunk, chunk + 2 * BAND), jnp.logical_or(jnp.logical_not(has_next), key >= BAND))
        dk_ref[...] = dkext_ref[BAND:BAND + chunk, :]
        dv_ref[...] = dvext_ref[BAND:BAND + chunk, :]

    ext_rows = chunk + 2 * BAND
    return pl.pallas_call(
        body, name=name, grid=(HEAD_PAIRS, d, nchunks),
        in_specs=[main[0], main[1], prev[1], main[2], prev[2], tok, bias, stat, stat,
                  q_next, do_next, stat_next, stat_next],
        out_specs=[tok, tok, tok, bias],
        out_shape=[jax.ShapeDtypeStruct((d, L, DIL_WIDTH), F32)] * 3
                  + [jax.ShapeDtypeStruct((HEAD_PAIRS, 2 * BAND, 2 * BAND), F32)],
        scratch_shapes=[pltpu.VMEM((ext_rows, LANES), BF16), pltpu.VMEM((ext_rows, LANES), BF16),
                        pltpu.VMEM((ext_rows, LANES), F32), pltpu.VMEM((ext_rows, LANES), F32)],
        compiler_params=_params("arbitrary", "arbitrary", "arbitrary"),
    )(qkv_v, qkv_v, qkv_v, qkv_v, qkv_v, do_v, bias_t, lse_joint, delta, qkv_v, do_v, lse_joint, delta)


def _lse_join(lse3, name):
    P, H, S = lse3.shape

    def body(l_ref, o_ref):
        a, b, c = l_ref[0], l_ref[1], l_ref[2]
        m = jnp.maximum(jnp.maximum(a, b), c)
        o_ref[...] = m + jnp.log(jnp.exp(a - m) + jnp.exp(b - m) + jnp.exp(c - m))

    return pl.pallas_call(body, name=name, out_shape=jax.ShapeDtypeStruct((H, S), F32))(lse3)


def _bucket_reduce(dbias_t, bucket_map_t, name):
    P, H = dbias_t.shape[:2]

    def body(db_ref, bk_ref, o_ref):
        p, h = pl.program_id(0), pl.program_id(1)

        @pl.when(jnp.logical_and(p == 0, h == 0))
        def _():
            o_ref[...] = jnp.zeros_like(o_ref)

        db, bk = db_ref[...], bk_ref[...]
        row = lax.broadcasted_iota(jnp.int32, (N_BUCKETS, LANES), 0)
        lane = lax.broadcasted_iota(jnp.int32, (N_BUCKETS, LANES), 1)

        def one(b, acc):
            val = jnp.sum(jnp.sum(jnp.where(bk == b, db, 0.0), axis=1, keepdims=True), axis=0, keepdims=True)
            return acc + jnp.where(jnp.logical_and(row == b, lane == h), val, 0.0)

        o_ref[...] += lax.fori_loop(0, N_BUCKETS, one, jnp.zeros((N_BUCKETS, LANES), F32))

    return pl.pallas_call(
        body, name=name, grid=(P, H),
        in_specs=[pl.BlockSpec((None, None, 2 * BAND, BAND), lambda p, h: (p, h, 0, 0)),
                  pl.BlockSpec((None, 2 * BAND, BAND), lambda p, h: (p, 0, 0))],
        out_specs=pl.BlockSpec((N_BUCKETS, LANES), lambda p, h: (0, 0)),
        out_shape=jax.ShapeDtypeStruct((N_BUCKETS, LANES), F32),
        compiler_params=_params("arbitrary", "arbitrary"),
    )(dbias_t, bucket_map_t)


def _mem_fwd(q, kv, name, tq=1024):
    S, W = q.shape
    N = kv.shape[0]
    pairs = W // LANES
    tq = _fit(S, tq)

    def body(q_ref, k_ref, v_ref, o_ref, lse_ref):
        for a in range(2):
            lanes = slice(a * HEAD_DIM, (a + 1) * HEAD_DIM)
            s = _dot(k_ref[:, lanes], q_ref[:, lanes] * QK_SCALE, _NT)
            m = jnp.max(s, axis=0, keepdims=True)
            e = jnp.exp(s - m)
            l = jnp.sum(e, axis=0, keepdims=True)
            o_ref[:, lanes] = _dot((e / l).astype(BF16), v_ref[:, lanes], _TN).astype(BF16)
            lse_ref[a:a + 1, :] = m + jnp.log(l)

    return pl.pallas_call(
        body, name=name, grid=(pairs, S // tq),
        in_specs=[pl.BlockSpec((tq, LANES), lambda hp, i: (i, hp)),
                  pl.BlockSpec((N, LANES), lambda hp, i: (0, hp)),
                  pl.BlockSpec((N, LANES), lambda hp, i: (0, pairs + hp))],
        out_specs=[pl.BlockSpec((tq, LANES), lambda hp, i: (i, hp)),
                   pl.BlockSpec((None, 2, tq), lambda hp, i: (hp, 0, i))],
        out_shape=[jax.ShapeDtypeStruct((S, W), BF16), jax.ShapeDtypeStruct((pairs, 2, S), F32)],
        compiler_params=_params("parallel", "parallel"),
    )(q, kv, kv)


def _mem_bwd(q, kv, do, lse, delta, name, tq=1024):
    S, W = q.shape
    N = kv.shape[0]
    pairs = W // LANES
    tq = _fit(S, tq)

    def body(q_ref, k_ref, v_ref, do_ref, lse_ref, dl_ref, dq_ref, dk_ref, dv_ref):
        i = pl.program_id(1)

        @pl.when(i == 0)
        def _():
            dk_ref[...] = jnp.zeros_like(dk_ref)
            dv_ref[...] = jnp.zeros_like(dv_ref)

        for a in range(2):
            lanes = slice(a * HEAD_DIM, (a + 1) * HEAD_DIM)
            qv, dov = q_ref[:, lanes] * QK_SCALE, do_ref[:, lanes]
            kv_, vv = k_ref[:, lanes], v_ref[:, lanes]
            p_t = jnp.exp(_dot(kv_, qv, _NT) - lse_ref[a:a + 1, :])
            ds_t = p_t * (_dot(vv, dov, _NT) - dl_ref[a:a + 1, :])
            ds_b = ds_t.astype(BF16)
            dq_ref[:, lanes] = (_dot(ds_b, kv_, _TN) * QK_SCALE).astype(BF16)
            dk_ref[:, lanes] += _dot(ds_b, qv, _NN)
            dv_ref[:, lanes] += _dot(p_t.astype(BF16), dov, _NN)

    qs = pl.BlockSpec((tq, LANES), lambda hp, i: (i, hp))
    stat = pl.BlockSpec((None, 2, tq), lambda hp, i: (hp, 0, i))
    acc = pl.BlockSpec((N, LANES), lambda hp, i: (0, hp))
    return pl.pallas_call(
        body, name=name, grid=(pairs, S // tq),
        in_specs=[qs, acc, pl.BlockSpec((N, LANES), lambda hp, i: (0, pairs + hp)), qs, stat, stat],
        out_specs=[qs, acc, acc],
        out_shape=[jax.ShapeDtypeStruct((S, W), BF16), jax.ShapeDtypeStruct((N, W), F32),
                   jax.ShapeDtypeStruct((N, W), F32)],
        compiler_params=_params("parallel", "arbitrary"),
    )(q, kv, kv, do, lse, delta)


def _head_rowdot(a, bs, name, tr=512):
    S, W = a.shape
    tr = _fit(S, tr)

    def body(*refs):
        a_ref, b_refs, o_ref, tmp_ref = refs[0], refs[1:-2], refs[-2], refs[-1]
        col = lax.broadcasted_iota(jnp.int32, (LANES, LANES), 0)
        lane = lax.broadcasted_iota(jnp.int32, (LANES, LANES), 1)
        acc = jnp.zeros((tr, LANES), F32)
        for j in range(W // LANES):
            cols = slice(j * LANES, (j + 1) * LANES)
            tot = _token_rows(b_refs[0], cols, tmp_ref)
            for r in b_refs[1:]:
                tot = tot + _token_rows(r, cols, tmp_ref)
            sel = jnp.where(col // HEAD_DIM + j * (LANES // HEAD_DIM) == lane, 1.0, 0.0).astype(F32)
            acc = acc + lax.dot_general(a_ref[:, cols].astype(F32) * tot, sel, (_NN, ((), ())),
                                        precision=lax.Precision.HIGHEST, preferred_element_type=F32)
        o_ref[...] = acc

    return pl.pallas_call(
        body, name=name, grid=(S // tr,), in_specs=[_row_spec(t, tr) for t in [a] + list(bs)],
        out_specs=pl.BlockSpec((tr, LANES), lambda i: (i, 0)),
        out_shape=jax.ShapeDtypeStruct((S, LANES), F32),
        scratch_shapes=[pltpu.VMEM((tr, LANES), F32)],
        compiler_params=_params("parallel"),
    )(a, *bs)


def _sum_cast_cols(groups, out_dtype, name, tail=None, tr=256):
    first = groups[0][0]
    S, W = (first.shape if first.ndim == 2 else (first.shape[0] * first.shape[1], first.shape[2]))
    tr = _fit(S, tr)
    flat = [t for g in groups for t in g] + ([tail] if tail is not None else [])
    tail_w = 0 if tail is None else tail.shape[1]

    def body(*refs):
        o_ref, tmp_ref = refs[-2], refs[-1]
        if tail is not None:
            o_ref[:, W * len(groups):] = refs[-3][...].astype(out_dtype)
        k = 0
        for gi, g in enumerate(groups):
            for j in range(W // LANES):
                cols = slice(j * LANES, (j + 1) * LANES)
                acc = _token_rows(refs[k], cols, tmp_ref)
                for r in refs[k + 1:k + len(g)]:
                    acc = acc + _token_rows(r, cols, tmp_ref)
                o_ref[:, gi * W + j * LANES:gi * W + (j + 1) * LANES] = acc.astype(out_dtype)
            k += len(g)

    return pl.pallas_call(
        body, name=name, grid=(S // tr,), in_specs=[_row_spec(t, tr) for t in flat],
        out_specs=pl.BlockSpec((tr, W * len(groups) + tail_w), lambda i: (i, 0)),
        out_shape=jax.ShapeDtypeStruct((S, W * len(groups) + tail_w), out_dtype),
        scratch_shapes=[pltpu.VMEM((tr, LANES), F32)],
        compiler_params=_params("parallel"),
    )(*flat)


FF_TILE = 256


def _ffn_up(h, w_gu, name, tm=1024):
    S, D = h.shape
    F2 = w_gu.shape[1]
    tm = _fit(S, tm)

    def body(h_ref, w_ref, gu_ref, act_ref):
        gu = _dot(h_ref[...], w_ref[...], _NN)
        gu_ref[...] = gu.astype(BF16)
        g, u = gu[:, :FF_TILE], gu[:, FF_TILE:]
        act_ref[...] = (g * (1.0 / (1.0 + jnp.exp(-g))) * u).astype(BF16)

    return pl.pallas_call(
        body, name=name, grid=(S // tm, F2 // (2 * FF_TILE)),
        in_specs=[pl.BlockSpec((tm, D), lambda i, j: (i, 0)), pl.BlockSpec((D, 2 * FF_TILE), lambda i, j: (0, j))],
        out_specs=[pl.BlockSpec((tm, 2 * FF_TILE), lambda i, j: (i, j)),
                   pl.BlockSpec((tm, FF_TILE), lambda i, j: (i, j))],
        out_shape=[jax.ShapeDtypeStruct((S, F2), BF16), jax.ShapeDtypeStruct((S, F2 // 2), BF16)],
        compiler_params=_params("parallel", "arbitrary"),
    )(h, w_gu)


def _ffn_dact(dy, w_down, gu, name, tm=1024):
    S, D = dy.shape
    F2 = gu.shape[1]
    tm = _fit(S, tm)

    def body(dy_ref, w_ref, gu_ref, dgu_ref):
        dact = _dot(dy_ref[...], w_ref[...], _NT)
        gu_v = gu_ref[...].astype(F32)
        g, u = gu_v[:, :FF_TILE], gu_v[:, FF_TILE:]
        sig = 1.0 / (1.0 + jnp.exp(-g))
        silu = g * sig
        dgu_ref[:, :FF_TILE] = (dact * u * (sig + silu * (1.0 - sig))).astype(BF16)
        dgu_ref[:, FF_TILE:] = (dact * silu).astype(BF16)

    return pl.pallas_call(
        body, name=name, grid=(S // tm, F2 // (2 * FF_TILE)),
        in_specs=[pl.BlockSpec((tm, D), lambda i, j: (i, 0)), pl.BlockSpec((FF_TILE, D), lambda i, j: (j, 0)),
                  pl.BlockSpec((tm, 2 * FF_TILE), lambda i, j: (i, j))],
        out_specs=pl.BlockSpec((tm, 2 * FF_TILE), lambda i, j: (i, j)),
        out_shape=jax.ShapeDtypeStruct((S, F2), BF16),
        compiler_params=_params("parallel", "arbitrary"),
    )(dy, w_down, gu)


def _fit_rows(n, cap):
    if n <= cap:
        return n
    t = (cap // 8) * 8
    while t >= 8:
        if n % t == 0:
            return t
        t -= 8
    raise ValueError(f"no sublane-aligned tile for {n} under {cap}")


def _add_n(arrs, name, tr=512):
    R, C = arrs[0].shape
    tr = _fit_rows(R, tr)

    def body(*refs):
        acc = refs[0][...]
        for r in refs[1:-1]:
            acc = acc + r[...]
        refs[-1][...] = acc

    row = pl.BlockSpec((tr, C), lambda i: (i, 0))
    return pl.pallas_call(
        body, name=name, grid=(R // tr,), in_specs=[row] * len(arrs), out_specs=row,
        out_shape=jax.ShapeDtypeStruct((R, C), F32), compiler_params=_params("parallel"),
    )(*arrs)


def _adamw(w, g, m, v, name, tr=512):
    R, C = w.shape
    tr = _fit_rows(R, tr)
    c1 = 1.0 / (1.0 - ADAM_B1 ** ADAM_STEP)
    c2 = 1.0 / (1.0 - ADAM_B2 ** ADAM_STEP)

    def body(w_ref, g_ref, m_ref, v_ref, d_ref, nm_ref, nv_ref):
        gv = g_ref[...]
        nm = ADAM_B1 * m_ref[...] + (1.0 - ADAM_B1) * gv
        nv = ADAM_B2 * v_ref[...] + (1.0 - ADAM_B2) * (gv * gv)
        nm_ref[...] = nm
        nv_ref[...] = nv
        d_ref[...] = -ADAM_LR * ((nm * c1) / (jnp.sqrt(nv * c2) + ADAM_EPS) + ADAM_WD * w_ref[...])

    row = pl.BlockSpec((tr, C), lambda i: (i, 0))
    return pl.pallas_call(
        body, name=name, grid=(R // tr,), in_specs=[row] * 4, out_specs=[row] * 3,
        out_shape=[jax.ShapeDtypeStruct((R, C), F32)] * 3, compiler_params=_params("parallel"),
    )(w, g, m, v)


def _place():
    return lax.axis_index("x"), lax.axis_index("y"), lax.axis_index("c")


_ANY = pl.BlockSpec(memory_space=pl.ANY)


def _chip_all_gather(shard, name):
    R, C = shard.shape
    half = R // 2

    def body(x_ref, out_ref, send_sems, recv_sems, local_sem):
        x, y, c = _place()
        chips = [(1 - x, y), (x, 1 - y), (1 - x, 1 - y)]
        sibling = (x, y, 1 - c)
        mine = pltpu.make_async_copy(x_ref, out_ref.at[2 * x + y], local_sem)
        mine.start()

        def rows(chip, core):
            return out_ref.at[chip, pl.ds(core * half, half)]

        def copy(k, chip, core, to, src=None):
            return pltpu.make_async_remote_copy(
                src_ref=rows(chip, core) if src is None else src, dst_ref=rows(chip, core),
                send_sem=send_sems.at[k], recv_sem=recv_sems.at[k], device_id=to, device_id_type=MESH_IDS)

        me = 2 * x + y
        first = [copy(k, me, c, (cx, cy, c), src=x_ref.at[pl.ds(c * half, half)]) for k, (cx, cy) in enumerate(chips)]
        for cp in first:
            cp.start()
        passed = [copy(3 + k, 2 * cx + cy, c, sibling) for k, (cx, cy) in enumerate(chips)]
        for k, (cx, cy) in enumerate(chips):
            copy(k, 2 * cx + cy, c, (cx, cy, c)).wait_recv()
            passed[k].start()
        for k, (cx, cy) in enumerate(chips):
            copy(3 + k, 2 * cx + cy, 1 - c, sibling).wait_recv()
        for cp in first + passed:
            cp.wait_send()
        mine.wait()

    return pl.pallas_call(
        body, name=name, in_specs=[_ANY], out_specs=_ANY,
        out_shape=jax.ShapeDtypeStruct((N_CHIPS, R, C), shard.dtype),
        scratch_shapes=[pltpu.SemaphoreType.DMA((6,)), pltpu.SemaphoreType.DMA((6,)), pltpu.SemaphoreType.DMA],
    )(shard)


def _sibling_exchange(buf, name):
    def body(x_ref, out_ref, send_sem, recv_sem):
        x, y, c = _place()
        cp = pltpu.make_async_remote_copy(
            src_ref=x_ref, dst_ref=out_ref, send_sem=send_sem, recv_sem=recv_sem,
            device_id=(x, y, 1 - c), device_id_type=MESH_IDS)
        cp.start()
        cp.wait()

    return pl.pallas_call(
        body, name=name, in_specs=[_ANY], out_specs=_ANY,
        out_shape=jax.ShapeDtypeStruct(buf.shape, buf.dtype),
        scratch_shapes=[pltpu.SemaphoreType.DMA, pltpu.SemaphoreType.DMA],
    )(buf)


def _chip_scatter(parts, name):
    _, R, C = parts.shape

    def body(p_ref, out_ref, send_sems, recv_sems):
        x, y, c = _place()
        chips = [(1 - x, y), (x, 1 - y), (1 - x, 1 - y)]

        def copy(k, slab, to):
            return pltpu.make_async_remote_copy(
                src_ref=p_ref.at[slab], dst_ref=out_ref.at[k], send_sem=send_sems.at[k], recv_sem=recv_sems.at[k],
                device_id=to, device_id_type=MESH_IDS)

        sends = [copy(k, 2 * cx + cy, (cx, cy, c)) for k, (cx, cy) in enumerate(chips)]
        for cp in sends:
            cp.start()
        for cp in sends:
            cp.wait_recv()
        for cp in sends:
            cp.wait_send()

    return pl.pallas_call(
        body, name=name, in_specs=[_ANY], out_specs=_ANY,
        out_shape=jax.ShapeDtypeStruct((3, R, C), parts.dtype),
        scratch_shapes=[pltpu.SemaphoreType.DMA((3,)), pltpu.SemaphoreType.DMA((3,))],
    )(parts)


def _all_to_all_small(vec, name):
    R, C = vec.shape

    def body(v_ref, out_ref, send_sems, recv_sems, local_sem):
        x, y, c = _place()
        me = 4 * x + 2 * y + c
        mine = pltpu.make_async_copy(v_ref, out_ref.at[me], local_sem)
        mine.start()
        flips = [(dx, dy, dc) for dx in (0, 1) for dy in (0, 1) for dc in (0, 1)][1:]

        def peer(f):
            return (x ^ f[0], y ^ f[1], c ^ f[2])

        def copy(k, slot, to):
            return pltpu.make_async_remote_copy(
                src_ref=v_ref, dst_ref=out_ref.at[slot], send_sem=send_sems.at[k], recv_sem=recv_sems.at[k],
                device_id=to, device_id_type=MESH_IDS)

        sends = [copy(k, me, peer(f)) for k, f in enumerate(flips)]
        for cp in sends:
            cp.start()
        for k, f in enumerate(flips):
            px, py, pc = peer(f)
            copy(k, 4 * px + 2 * py + pc, peer(f)).wait_recv()
        for cp in sends:
            cp.wait_send()
        mine.wait()

    return pl.pallas_call(
        body, name=name, in_specs=[_ANY], out_specs=_ANY,
        out_shape=jax.ShapeDtypeStruct((8, R, C), vec.dtype),
        scratch_shapes=[pltpu.SemaphoreType.DMA((7,)), pltpu.SemaphoreType.DMA((7,)), pltpu.SemaphoreType.DMA],
    )(vec)


def _to_heads(t, n):
    S = t.shape[0]
    return t.reshape(S, n, HEAD_DIM).transpose(1, 0, 2)


def _to_heads_t(t, n):
    S = t.shape[0]
    return t.T.reshape(n, HEAD_DIM, S)


def _from_heads_t(t):
    H, Dh, S = t.shape
    return t.reshape(H * Dh, S).T


def _t5_bucket(dist):
    max_exact = N_BUCKETS // 2
    d = np.maximum(dist, 1).astype(np.float32)
    large = max_exact + (np.log(d / max_exact) / np.log(MAX_DISTANCE / max_exact)
                         * (N_BUCKETS - max_exact)).astype(np.int32)
    large = np.minimum(large, N_BUCKETS - 1)
    return np.where(dist < max_exact, dist, large).astype(np.int32)


def _band_tables():
    qi = np.arange(BAND)[:, None]
    kj = np.arange(2 * BAND)[None, :]
    sub = qi + BAND - kj
    band = (sub >= 0) & (sub <= BAND)
    out = []
    for d in DILATIONS:
        bucket = _t5_bucket(np.clip(sub, 0, BAND) * d)
        out.append(np.where(band, bucket, -1).astype(np.int32))
    return np.stack(out)


_PACK = (("w_in", 770), ("w_out", 256), ("w_xq", 64), ("w_xk", 64), ("w_xv", 64), ("w_xo", 64),
         ("w_gate", 704), ("w_up", 704), ("w_down", 704))


def _pack(shards):
    rows = [shards[n].reshape(-1, PACK_COLS) for n, _ in _PACK]
    total = sum(r.shape[0] for r in rows)
    pad = (-total) % 128
    if pad:
        rows.append(jnp.zeros((pad, PACK_COLS), rows[0].dtype))
    return jnp.concatenate(rows, axis=0)


def _unpack(pack, shapes):
    out, r = {}, 0
    for n, _ in _PACK:
        cnt = int(np.prod(shapes[n])) // PACK_COLS
        out[n] = pack[r:r + cnt].reshape(shapes[n])
        r += cnt
    return out


_COL_SHARDED = ("w_in", "w_xo", "w_gate", "w_up")


def _full_weight(gathered, name):
    return jnp.concatenate(gathered, axis=1 if name in _COL_SHARDED else 0)


def _split_weight(full, name):
    return jnp.split(full, N_CHIPS, axis=1 if name in _COL_SHARDED else 0)


_SMALL = ("g_mix_pre", "g_mix_post", "g_xattn_pre", "g_mem", "g_xattn_post", "g_ffn_pre", "g_ffn_post")


def _pack_small(vals):
    D = vals["g_mix_pre"].shape[1]
    rows = [vals[n].reshape(1, D) for n in _SMALL]
    misc = jnp.concatenate([vals["b_f"].reshape(-1), vals["rel_bias"].reshape(-1)])
    rows.append(jnp.pad(misc, (0, D - misc.shape[0])).reshape(1, D))
    rows.append(jnp.zeros((16 - len(rows), D), F32))
    return jnp.concatenate(rows, axis=0)


def _unpack_small(pack):
    out = {n: pack[i:i + 1] for i, n in enumerate(_SMALL)}
    out["b_f"] = pack[7, 0:N_FOX_HEADS].reshape(1, N_FOX_HEADS)
    out["rel_bias"] = pack[7, N_FOX_HEADS:N_FOX_HEADS + N_BUCKETS * N_DIL_HEADS].reshape(N_BUCKETS, N_DIL_HEADS)
    return out


def kernel(x, mem, g_mix_pre, w_in, b_f, rel_bias, w_out, g_mix_post, g_xattn_pre, g_mem, w_xq, w_xk, w_xv, w_xo, g_xattn_post, g_ffn_pre, w_gate, w_up, w_down, g_ffn_post, loss_target, m_g_mix_pre, m_w_in, m_b_f, m_rel_bias, m_w_out, m_g_mix_post, m_g_xattn_pre, m_g_mem, m_w_xq, m_w_xk, m_w_xv, m_w_xo, m_g_xattn_post, m_g_ffn_pre, m_w_gate, m_w_up, m_w_down, m_g_ffn_post, v_g_mix_pre, v_w_in, v_b_f, v_rel_bias, v_w_out, v_g_mix_post, v_g_xattn_pre, v_g_mem, v_w_xq, v_w_xk, v_w_xv, v_w_xo, v_g_xattn_post, v_g_ffn_pre, v_w_gate, v_w_up, v_w_down, v_g_ffn_post):
    args = dict(locals())
    big = [n for n, _ in _PACK]
    names = ["g_mix_pre", "w_in", "b_f", "rel_bias", "w_out", "g_mix_post", "g_xattn_pre", "g_mem", "w_xq",
             "w_xk", "w_xv", "w_xo", "g_xattn_post", "g_ffn_pre", "w_gate", "w_up", "w_down", "g_ffn_post"]
    xs = x[0]
    S, D = xs.shape
    assert S % (BAND * DILATIONS[-1]) == 0
    shard_shapes = {n: args[n].shape[1:] for n in big}
    my_x, my_y, my_c = lax.axis_index("x"), lax.axis_index("y"), lax.axis_index("c")

    gathered = _chip_all_gather(_pack({n: args[n][0].astype(BF16) for n in big}), "weights_all_gather")
    per_chip = [_unpack(gathered[j], shard_shapes) for j in range(N_CHIPS)]
    W = {n: _full_weight([pc[n] for pc in per_chip], n) for n in big}
    w_fox, w_fg, w_dil = (W["w_in"][:, :3 * FOX_WIDTH], W["w_in"][:, 3 * FOX_WIDTH:3 * FOX_WIDTH + N_FOX_HEADS],
                          W["w_in"][:, 3 * FOX_WIDTH + N_FOX_HEADS:])
    w_qkv = jnp.concatenate([w_fox, w_dil], axis=1)
    w_fg_pad = jnp.pad(w_fg, ((0, 0), (0, LANES - N_FOX_HEADS)))
    F = W["w_gate"].shape[1]
    nft = F // FF_TILE
    w_gu = jnp.stack([W["w_gate"].reshape(D, nft, FF_TILE), W["w_up"].reshape(D, nft, FF_TILE)],
                     axis=2).reshape(D, 2 * F)

    h1 = _rms_fwd(xs, g_mix_pre, "rms_mix_pre")
    qkv = _mm(h1, w_qkv, "nn", BF16, "proj_qkv")
    fg = _mm(h1, w_fg_pad, "nn", F32, "proj_gate")
    fg_t = fg[:, :N_FOX_HEADS].T
    b_col = b_f.reshape(N_FOX_HEADS, 1)
    c_t = _forget_fwd(fg_t, b_col, "forget_cumsum")
    fq_s, fk_s, fv_s = (qkv[:, i * FOX_WIDTH:(i + 1) * FOX_WIDTH] for i in range(3))
    fqt, fvt = _to_heads_t(fq_s, N_FOX_HEADS), _to_heads_t(fv_s, N_FOX_HEADS)
    unit = jnp.full((N_FOX_HEADS, S), 1.0, BF16)
    inv_scale = jnp.full((N_FOX_HEADS, S), 1.0 / QK_SCALE, BF16)
    ka = _lanes_operand(_to_heads(fk_s, N_FOX_HEADS), list(_split3(-c_t)) + [unit] * EXTRA)
    qa_f = _rows_operand(fqt, [inv_scale] * EXTRA)
    o_fox_t, lse_fox = _fox_fwd(qa_f, ka, _with_ones(fvt), "fox_fwd")

    bucket_map = _band_tables()
    onehot = (jnp.asarray(bucket_map)[..., None] == jnp.arange(N_BUCKETS)).astype(F32)
    bias_tab = jnp.einsum("pqkb,bh->phkq", onehot, rel_bias, precision=lax.Precision.HIGHEST)
    bias_tab = jnp.where(jnp.asarray(bucket_map.transpose(0, 2, 1) >= 0)[:, None], bias_tab, NEG)
    bias_t = bias_tab.reshape(3, HEAD_PAIRS, 2, 2 * BAND, BAND).transpose(0, 1, 3, 2, 4).reshape(
        3, HEAD_PAIRS, 2 * BAND, 2 * BAND)
    views = [(qkv.reshape(1, S, qkv.shape[1]), DIL_Q_BLOCK)] + [
        (_to_residues(qkv, 1, 3 * DIL_WIDTH, d, f"dilated_qkv_residues_{d}"), 0) for d in DILATIONS[1:]]

    def to_tok(stat, d):
        return stat.reshape(N_DIL_HEADS, d, S // d).swapaxes(1, 2).reshape(N_DIL_HEADS, S)

    def to_perm(stat, d):
        return stat.reshape(N_DIL_HEADS, S // d, d).swapaxes(1, 2).reshape(HEAD_PAIRS, 2, S)

    def tok_or_res(t):
        return t.reshape(t.shape[1:]) if t.shape[0] == 1 else t

    lse_tok = jnp.stack([to_tok(_dil_lse(*views[p], bias_t[p], f"dilated_lse_{d}"), d)
                         for p, d in enumerate(DILATIONS)])
    lse_joint = _lse_join(lse_tok, "dilated_lse_join")
    lse_perm = [to_perm(lse_joint, d) for d in DILATIONS]
    o_dil = [tok_or_res(_dil_out(*views[p], bias_t[p], lse_perm[p], f"dilated_out_{d}"))
             for p, d in enumerate(DILATIONS)]
    o_cat = _sum_cast_cols([[_from_heads_t(o_fox_t)]] + [[o] for o in o_dil], BF16, "mixer_out_cat")
    w_out_b = W["w_out"]
    w_out_cat = jnp.concatenate([w_out_b[:FOX_WIDTH]] + [w_out_b[FOX_WIDTH:]] * 3, axis=0)
    a = _mm(o_cat, w_out_cat, "nn", F32, "proj_out", tk=2048)
    x1, h2 = _resid_norm(xs, a, g_mix_post, g_xattn_pre, "resid_mix")

    hm = _rms_fwd(mem[0], g_mem, "rms_mem")
    q2 = _mm(h2, W["w_xq"], "nn", BF16, "xattn_q")
    w_xkv = jnp.concatenate([W["w_xk"], W["w_xv"]], axis=1)
    kvm = _mm(hm, w_xkv, "nn", BF16, "xattn_kv")
    MW = N_MEM_HEADS * HEAD_DIM
    oc, lse_mem = _mem_fwd(q2, kvm, "xattn_fwd")
    y2 = _mm(oc, W["w_xo"], "nn", F32, "xattn_o")
    x2, h3 = _resid_norm(x1, y2, g_xattn_post, g_ffn_pre, "resid_xattn")

    gu, act = _ffn_up(h3, w_gu, "ffn_up")
    y3 = _mm(act, W["w_down"], "nn", F32, "ffn_down", tk=2816)
    dx3, loss_tile = _final_loss(x2, y3, g_ffn_post, loss_target[0], "final_loss")

    grads = {}
    small = {}
    _, dy3_b, dg = _rms_bwd(y3, g_ffn_post, dx3, None, "bwd_norm_ffn_post", want=("bf16",))
    small["g_ffn_post"] = dg[0:1]
    grads["w_down"] = _mm(act, dy3_b, "tn", F32, "grad_w_down", tm=1408)
    dgu = _ffn_dact(dy3_b, W["w_down"], gu, "ffn_dact")
    dw_gu = _mm(h3, dgu, "tn", F32, "grad_w_gu", tn=1408).reshape(D, nft, 2, FF_TILE)
    grads["w_gate"], grads["w_up"] = dw_gu[:, :, 0].reshape(D, F), dw_gu[:, :, 1].reshape(D, F)
    dh3 = _mm(dgu, w_gu, "nt", F32, "bwd_ffn_in", tk=1408)
    dx2, _, dg = _rms_bwd(x2, g_ffn_pre, dh3, dx3, "bwd_norm_ffn_pre", want=("f32",))
    small["g_ffn_pre"] = dg[0:1]

    _, dy2_b, dg = _rms_bwd(y2, g_xattn_post, dx2, None, "bwd_norm_xattn_post", want=("bf16",))
    small["g_xattn_post"] = dg[0:1]
    grads["w_xo"] = _mm(oc, dy2_b, "tn", F32, "grad_w_xo")
    doc = _mm(dy2_b, W["w_xo"], "nt", BF16, "bwd_xattn_o")
    delta_mem = _head_rowdot(doc, [oc], "xattn_delta")[:, :N_MEM_HEADS].T.reshape(N_MEM_HEADS // 2, 2, S)
    dq2, dkm, dvm = _mem_bwd(q2, kvm, doc, lse_mem, delta_mem, "xattn_bwd")
    dkvm = jnp.concatenate([dkm, dvm], axis=1).astype(BF16)
    grads["w_xq"] = _mm(h2, dq2, "tn", F32, "grad_w_xq")
    dw_xkv = _mm(hm, dkvm, "tn", F32, "grad_w_xkv")
    grads["w_xk"], grads["w_xv"] = dw_xkv[:, :MW], dw_xkv[:, MW:]
    dhm = _mm(dkvm, w_xkv, "nt", F32, "bwd_xattn_kv")
    _, _, dg = _rms_bwd(mem[0], g_mem, dhm, None, "bwd_norm_mem", want=())
    small["g_mem"] = dg[0:1]
    dh2 = _mm(dq2, W["w_xq"], "nt", F32, "bwd_xattn_q")
    dx1, _, dg = _rms_bwd(x1, g_xattn_pre, dh2, dx2, "bwd_norm_xattn_pre", want=("f32",))
    small["g_xattn_pre"] = dg[0:1]

    _, da_b, dg = _rms_bwd(a, g_mix_post, dx1, None, "bwd_norm_mix_post", want=("bf16",))
    small["g_mix_post"] = dg[0:1]
    dw_out_cat = _mm(o_cat, da_b, "tn", F32, "grad_w_out")
    dw_out_dil = _add_n([dw_out_cat[FOX_WIDTH + p * DIL_WIDTH:FOX_WIDTH + (p + 1) * DIL_WIDTH] for p in range(3)],
                        "grad_w_out_dil")
    grads["w_out"] = jnp.concatenate([dw_out_cat[:FOX_WIDTH], dw_out_dil], axis=0)
    do = _mm(da_b, w_out_b, "nt", BF16, "bwd_proj_out")
    do_fox, do_dil = do[:, :FOX_WIDTH], do[:, FOX_WIDTH:]

    delta_fox = _head_rowdot(do_fox, [o_cat[:, :FOX_WIDTH]], "fox_delta")[:, :N_FOX_HEADS].T
    qa_b = lax.dynamic_update_slice(qa_f, jnp.stack(_split3(lse_fox[:, 0] * (-1.0 / QK_SCALE)), axis=1),
                                    (0, HEAD_DIM + EXTRA, 0))
    va = _lanes_operand(_to_heads(fv_s, N_FOX_HEADS), [unit] * EXTRA)
    doa = _rows_operand(_to_heads_t(do_fox, N_FOX_HEADS), list(_split3(-delta_fox)))
    dq_aug, dk_aug, dvf = _fox_bwd(qa_b, ka, ka.transpose(0, 2, 1), va, doa, "fox_bwd")
    dqf, dkf = dq_aug[:, :HEAD_DIM], dk_aug[:, :HEAD_DIM]
    dfg_t, db_f = _forget_bwd(fg_t, b_col, dq_aug[:, HEAD_DIM + EXTRA], dk_aug[:, HEAD_DIM], "forget_bwd")

    delta_dil = _head_rowdot(do_dil, o_dil, "dilated_delta")[:, :N_DIL_HEADS].T
    do_res = [do_dil.reshape(1, S, DIL_WIDTH)] + [
        _to_residues(do, 1, DIL_WIDTH, d, f"dilated_do_residues_{d}") for d in DILATIONS[1:]]
    dil_grads = [_dil_bwd(*views[p], do_res[p], bias_t[p], lse_perm[p], to_perm(delta_dil, d), f"dilated_bwd_{d}")
                 for p, d in enumerate(DILATIONS)]
    dbias_t = jnp.stack([g[3].reshape(HEAD_PAIRS, 2 * BAND, 2, BAND).transpose(0, 2, 1, 3).reshape(
        N_DIL_HEADS, 2 * BAND, BAND) for g in dil_grads])
    d_rel = _bucket_reduce(dbias_t, jnp.asarray(bucket_map.transpose(0, 2, 1)), "rel_bias_grad")[:, :N_DIL_HEADS]
    dfg_pad = jnp.pad(dfg_t.T, ((0, 0), (0, LANES - N_FOX_HEADS))).astype(BF16)
    dcat = _sum_cast_cols([[_from_heads_t(dqf)], [_from_heads_t(dkf)], [_from_heads_t(dvf)]]
                          + [[tok_or_res(g[j]) for g in dil_grads] for j in range(3)],
                          BF16, "dqkv_assemble", tail=dfg_pad)
    dw_cat = _mm(h1, dcat, "tn", F32, "grad_w_qkv", tm=512, tn=3200)
    n_qkv = 3 * (FOX_WIDTH + DIL_WIDTH)
    grads["w_in"] = jnp.concatenate([dw_cat[:, :3 * FOX_WIDTH], dw_cat[:, n_qkv:n_qkv + N_FOX_HEADS],
                                     dw_cat[:, 3 * FOX_WIDTH:n_qkv]], axis=1)
    w_cat = jnp.concatenate([w_qkv, w_fg_pad], axis=1)
    dh1 = _mm(dcat, w_cat, "nt", F32, "bwd_proj_in", tk=3200)
    grad_x, _, dg = _rms_bwd(xs, g_mix_pre, dh1, dx1, "bwd_norm_mix_pre", want=("f32",))
    small["g_mix_pre"] = dg[0:1]
    small["b_f"] = db_f[:, 0].reshape(1, N_FOX_HEADS)
    small["rel_bias"] = d_rel

    split = {n: _split_weight(grads[n], n) for n in big}
    parts = jnp.stack([_pack({n: split[n][j] for n in big}) for j in range(N_CHIPS)])
    R = parts.shape[1]
    half = R // 2
    keep = lax.dynamic_slice_in_dim(parts, my_c * half, half, axis=1)
    give = lax.dynamic_slice_in_dim(parts, (1 - my_c) * half, half, axis=1)
    got = _sibling_exchange(give, "grads_to_sibling")
    chip_sum = _add_n([keep.reshape(-1, PACK_COLS), got.reshape(-1, PACK_COLS)], "grads_add_sibling")
    chip_sum = chip_sum.reshape(N_CHIPS, half, PACK_COLS)
    my_chip = 2 * my_x + my_y
    from_chips = _chip_scatter(chip_sum.astype(BF16), "grads_to_chips")
    own = lax.dynamic_index_in_dim(chip_sum, my_chip, axis=0, keepdims=False)
    g_half = _add_n([own, from_chips[0], from_chips[1], from_chips[2]], "grads_add_chips")
    other_half = _sibling_exchange(g_half, "grads_share_sibling")
    g_pack = jnp.where(my_c == 0, jnp.concatenate([g_half, other_half]), jnp.concatenate([other_half, g_half]))

    small_pack = _pack_small(small)
    small_pack = small_pack.at[8, 0].set(loss_tile[0, 0])
    everyone = _all_to_all_small(small_pack, "small_all_gather")
    small_sum = _add_n([everyone[i] for i in range(8)], "small_sum")
    loss = small_sum[8, 0]
    g_small = _unpack_small(small_sum)

    outs = {"grad": _unpack(g_pack, shard_shapes), "delta": {}, "new_m": {}, "new_v": {}}
    for n in big:
        outs["delta"][n], outs["new_m"][n], outs["new_v"][n] = _adamw(
            args[n][0], outs["grad"][n], args["m_" + n][0], args["v_" + n][0], f"adamw_{n}")
    sw = _pack_small({n: args[n] for n in _SMALL + ("b_f", "rel_bias")})
    sm = _pack_small({n: args["m_" + n] for n in _SMALL + ("b_f", "rel_bias")})
    sv = _pack_small({n: args["v_" + n] for n in _SMALL + ("b_f", "rel_bias")})
    sd, snm, snv = _adamw(sw, small_sum.at[8, 0].set(0.0), sm, sv, "adamw_small")
    souts = {"grad": g_small, "delta": _unpack_small(sd), "new_m": _unpack_small(snm), "new_v": _unpack_small(snv)}

    def leaf(kind, n):
        if n in souts[kind]:
            return souts[kind][n].reshape(args[n].shape)
        return outs[kind][n].reshape(args[n].shape)

    result = [loss, grad_x.reshape(x.shape)]
    for kind in ("grad", "delta", "new_m", "new_v"):
        result += [leaf(kind, n) for n in names]
    return tuple(result)
```

```python
import numpy as np
import jax
import jax.numpy as jnp
from jax import lax
from jax.experimental import pallas as pl
from jax.experimental.pallas import tpu as pltpu

F32 = jnp.float32
BF16 = jnp.bfloat16
MESH_IDS = pl.DeviceIdType.MESH

LANES = 128
HEAD_DIM = 64
N_FOX_HEADS = 8
N_DIL_HEADS = 8
N_MEM_HEADS = 4
FOX_WIDTH = N_FOX_HEADS * HEAD_DIM
DIL_WIDTH = N_DIL_HEADS * HEAD_DIM
DILATIONS = (1, 4, 16)
BAND = 128
BAND_CHUNK_MAX = 8 * BAND
N_BUCKETS = 32
MAX_DISTANCE = 2048
QK_SCALE = HEAD_DIM ** -0.5
RMS_EPS = 1e-6
NEG = -1e30
VMEM_LIMIT = 56 << 20

ADAM_LR = 0.001
ADAM_B1 = 0.9
ADAM_B2 = 0.999
ADAM_EPS = 1e-08
ADAM_WD = 0.01
ADAM_STEP = 10

N_CHIPS = 4
PACK_COLS = 1024


def _params(*sem):
    return pltpu.CompilerParams(dimension_semantics=sem, vmem_limit_bytes=VMEM_LIMIT)


def _fit(n, cap):
    if n <= cap:
        return n
    t = (cap // LANES) * LANES
    while t >= LANES:
        if n % t == 0:
            return t
        t -= LANES
    raise ValueError(f"no lane-aligned tile for {n} under {cap}")


def _dot(a, b, dims):
    return lax.dot_general(a, b, (dims, ((), ())), preferred_element_type=F32)


_NN = ((1,), (0,))
_NT = ((1,), (1,))
_TN = ((0,), (0,))


def _mm(a, b, mode, out_dtype, name, tm=1024, tn=1024, tk=1024):
    if mode == "nn":
        (M, K), N = a.shape, b.shape[1]
    elif mode == "nt":
        (M, K), N = a.shape, b.shape[0]
    else:
        (K, M), N = a.shape, b.shape[1]
    tm, tn, tk = _fit(M, tm), _fit(N, tn), _fit(K, tk)
    nk = K // tk
    if mode == "tn":
        a_spec = pl.BlockSpec((tk, tm), lambda i, j, k: (k, i))
    else:
        a_spec = pl.BlockSpec((tm, tk), lambda i, j, k: (i, k))
    if mode == "nt":
        b_spec = pl.BlockSpec((tn, tk), lambda i, j, k: (j, k))
    else:
        b_spec = pl.BlockSpec((tk, tn), lambda i, j, k: (k, j))
    dims = {"nn": _NN, "nt": _NT, "tn": _TN}[mode]

    def body(a_ref, b_ref, o_ref, *acc):
        prod = _dot(a_ref[...].astype(BF16), b_ref[...].astype(BF16), dims)
        if nk == 1:
            o_ref[...] = prod.astype(o_ref.dtype)
            return
        acc_ref, k = acc[0], pl.program_id(2)

        @pl.when(k == 0)
        def _():
            acc_ref[...] = prod

        @pl.when(k > 0)
        def _():
            acc_ref[...] += prod

        @pl.when(k == nk - 1)
        def _():
            o_ref[...] = acc_ref[...].astype(o_ref.dtype)

    return pl.pallas_call(
        body, name=name, grid=(M // tm, N // tn, nk),
        in_specs=[a_spec, b_spec],
        out_specs=pl.BlockSpec((tm, tn), lambda i, j, k: (i, j)),
        out_shape=jax.ShapeDtypeStruct((M, N), out_dtype),
        scratch_shapes=[pltpu.VMEM((tm, tn), F32)] if nk > 1 else [],
        compiler_params=_params("parallel", "parallel", "arbitrary"),
    )(a, b)


def _rms_rows(x):
    return lax.rsqrt(jnp.mean(x * x, axis=-1, keepdims=True) + RMS_EPS)


def _rms_fwd(x, g, name, tr=512):
    S, D = x.shape
    tr = _fit(S, tr)

    def body(x_ref, g_ref, h_ref):
        xv = x_ref[...]
        h_ref[...] = (xv * _rms_rows(xv) * g_ref[...]).astype(BF16)

    return pl.pallas_call(
        body, name=name, grid=(S // tr,),
        in_specs=[pl.BlockSpec((tr, D), lambda i: (i, 0)), pl.BlockSpec((1, D), lambda i: (0, 0))],
        out_specs=pl.BlockSpec((tr, D), lambda i: (i, 0)),
        out_shape=jax.ShapeDtypeStruct((S, D), BF16),
        compiler_params=_params("parallel"),
    )(x, g)


def _resid_norm(xres, y, g_post, g_next, name, tr=512):
    S, D = xres.shape
    tr = _fit(S, tr)

    def body(x_ref, y_ref, gp_ref, gn_ref, xn_ref, h_ref):
        yv = y_ref[...]
        xn = x_ref[...] + yv * _rms_rows(yv) * gp_ref[...]
        xn_ref[...] = xn
        h_ref[...] = (xn * _rms_rows(xn) * gn_ref[...]).astype(BF16)

    row = pl.BlockSpec((tr, D), lambda i: (i, 0))
    vec = pl.BlockSpec((1, D), lambda i: (0, 0))
    return pl.pallas_call(
        body, name=name, grid=(S // tr,),
        in_specs=[row, row, vec, vec], out_specs=[row, row],
        out_shape=[jax.ShapeDtypeStruct((S, D), F32), jax.ShapeDtypeStruct((S, D), BF16)],
        compiler_params=_params("parallel"),
    )(xres, y, g_post, g_next)


def _final_loss(xres, y, g_post, target, name, tr=512):
    S, D = xres.shape
    tr = _fit(S, tr)

    def body(x_ref, y_ref, gp_ref, t_ref, d_ref, loss_ref):
        i = pl.program_id(0)
        yv = y_ref[...]
        err = x_ref[...] + yv * _rms_rows(yv) * gp_ref[...] - t_ref[...]
        d_ref[...] = err * (1.0 / D)

        @pl.when(i == 0)
        def _():
            loss_ref[...] = jnp.zeros_like(loss_ref)

        part = jnp.sum(jnp.sum(err * err, axis=1, keepdims=True), axis=0, keepdims=True)
        loss_ref[...] += jnp.broadcast_to(part * (0.5 / D), loss_ref.shape)

    row = pl.BlockSpec((tr, D), lambda i: (i, 0))
    vec = pl.BlockSpec((1, D), lambda i: (0, 0))
    return pl.pallas_call(
        body, name=name, grid=(S // tr,),
        in_specs=[row, row, vec, row],
        out_specs=[row, pl.BlockSpec((8, LANES), lambda i: (0, 0))],
        out_shape=[jax.ShapeDtypeStruct((S, D), F32), jax.ShapeDtypeStruct((8, LANES), F32)],
        compiler_params=_params("arbitrary"),
    )(xres, y, g_post, target)


def _rms_bwd(xin, g, dy, dres, name, want=("f32", "bf16"), tr=512):
    S, D = xin.shape
    tr = _fit(S, tr)
    has_res = dres is not None

    def body(*refs):
        refs = list(refs)
        dg_ref = refs.pop()
        dxb_ref = refs.pop() if "bf16" in want else None
        dx_ref = refs.pop() if "f32" in want else None
        dr_ref = refs.pop() if has_res else None
        x_ref, g_ref, dy_ref = refs
        i = pl.program_id(0)
        xv = x_ref[...]
        dyv = dy_ref[...].astype(F32)
        xhat = xv * _rms_rows(xv)
        dxhat = dyv * g_ref[...]
        r = _rms_rows(xv)
        dx = r * (dxhat - xhat * jnp.mean(dxhat * xhat, axis=-1, keepdims=True))
        if has_res:
            dx = dx + dr_ref[...]
        if dx_ref is not None:
            dx_ref[...] = dx
        if dxb_ref is not None:
            dxb_ref[...] = dx.astype(BF16)

        @pl.when(i == 0)
        def _():
            dg_ref[...] = jnp.zeros_like(dg_ref)

        dg_ref[...] += jnp.broadcast_to(jnp.sum(dyv * xhat, axis=0, keepdims=True), dg_ref.shape)

    row = pl.BlockSpec((tr, D), lambda i: (i, 0))
    vec = pl.BlockSpec((1, D), lambda i: (0, 0))
    acc = pl.BlockSpec((8, D), lambda i: (0, 0))
    ins = [xin, g, dy] + ([dres] if has_res else [])
    dtypes = [dt for key, dt in (("f32", F32), ("bf16", BF16)) if key in want]
    outs = pl.pallas_call(
        body, name=name, grid=(S // tr,),
        in_specs=[row, vec, row] + ([row] if has_res else []),
        out_specs=[row] * len(dtypes) + [acc],
        out_shape=[jax.ShapeDtypeStruct((S, D), dt) for dt in dtypes] + [jax.ShapeDtypeStruct((8, D), F32)],
        compiler_params=_params("arbitrary"),
    )(*ins)
    by_key = dict(zip([key for key in ("f32", "bf16") if key in want], outs[:-1]))
    return by_key.get("f32"), by_key.get("bf16"), outs[-1]


def _tri(n, upper):
    r = lax.broadcasted_iota(jnp.int32, (n, n), 0)
    c = lax.broadcasted_iota(jnp.int32, (n, n), 1)
    return jnp.where((r <= c) if upper else (r >= c), 1.0, 0.0).astype(F32)


def _forget_fwd(fg_t, b_col, name, ts=512):
    H, S = fg_t.shape
    ts = _fit(S, ts)

    def body(f_ref, b_ref, c_ref, carry_ref):
        i = pl.program_id(0)

        @pl.when(i == 0)
        def _():
            carry_ref[...] = jnp.zeros_like(carry_ref)

        z = f_ref[...] + b_ref[...]
        logf = jnp.minimum(z, 0.0) - jnp.log(1.0 + jnp.exp(-jnp.abs(z)))
        run = lax.dot_general(logf, _tri(ts, True), (_NN, ((), ())), precision=lax.Precision.HIGHEST,
                              preferred_element_type=F32) + carry_ref[:, 0:1]
        c_ref[...] = run
        carry_ref[...] = jnp.broadcast_to(
            carry_ref[:, 0:1] + jnp.sum(logf, axis=1, keepdims=True), carry_ref.shape)

    return pl.pallas_call(
        body, name=name, grid=(S // ts,),
        in_specs=[pl.BlockSpec((H, ts), lambda i: (0, i)), pl.BlockSpec((H, 1), lambda i: (0, 0))],
        out_specs=pl.BlockSpec((H, ts), lambda i: (0, i)),
        out_shape=jax.ShapeDtypeStruct((H, S), F32),
        scratch_shapes=[pltpu.VMEM((H, LANES), F32)],
        compiler_params=_params("arbitrary"),
    )(fg_t, b_col)


def _forget_bwd(fg_t, b_col, dc_plus, dc_minus, name, ts=512):
    H, S = fg_t.shape
    ts = _fit(S, ts)
    nb = S // ts

    def body(f_ref, b_ref, dcp_ref, dcm_ref, df_ref, db_ref, carry_ref):
        i = pl.program_id(0)

        @pl.when(i == 0)
        def _():
            carry_ref[...] = jnp.zeros_like(carry_ref)
            db_ref[...] = jnp.zeros_like(db_ref)

        dc = dcp_ref[...] - dcm_ref[...]
        suffix = lax.dot_general(dc, _tri(ts, False), (_NN, ((), ())), precision=lax.Precision.HIGHEST,
                                 preferred_element_type=F32) + carry_ref[:, 0:1]
        z = f_ref[...] + b_ref[...]
        sig_neg = 1.0 / (1.0 + jnp.exp(z))
        df = suffix * sig_neg
        df_ref[...] = df
        carry_ref[...] = jnp.broadcast_to(
            carry_ref[:, 0:1] + jnp.sum(dc, axis=1, keepdims=True), carry_ref.shape)
        db_ref[...] += jnp.broadcast_to(jnp.sum(df, axis=1, keepdims=True), db_ref.shape)

    rev = pl.BlockSpec((H, ts), lambda i: (0, nb - 1 - i))
    return pl.pallas_call(
        body, name=name, grid=(nb,),
        in_specs=[rev, pl.BlockSpec((H, 1), lambda i: (0, 0)), rev, rev],
        out_specs=[rev, pl.BlockSpec((H, LANES), lambda i: (0, 0))],
        out_shape=[jax.ShapeDtypeStruct((H, S), F32), jax.ShapeDtypeStruct((H, LANES), F32)],
        scratch_shapes=[pltpu.VMEM((H, LANES), F32)],
        compiler_params=_params("arbitrary"),
    )(fg_t, b_col, dc_plus, dc_minus)


ONES_ROWS = 16
EXTRA = 3


def _split3(x):
    hi = lax.reduce_precision(x, 8, 7)
    mid = lax.reduce_precision(x - hi, 8, 7)
    lo = lax.reduce_precision(x - hi - mid, 8, 7)
    return hi.astype(BF16), mid.astype(BF16), lo.astype(BF16)


def _lanes_operand(t, extras):
    block = jnp.pad(jnp.stack(extras, axis=-1), ((0, 0), (0, 0), (0, LANES - HEAD_DIM - len(extras))))
    return jnp.concatenate([t, block], axis=-1)


def _rows_operand(t, extras):
    block = jnp.pad(jnp.stack(extras, axis=1), ((0, 0), (0, LANES - HEAD_DIM - len(extras)), (0, 0)))
    return jnp.concatenate([t, block], axis=1)


def _with_ones(t):
    return jnp.concatenate([t, jnp.ones((t.shape[0], ONES_ROWS, t.shape[2]), t.dtype)], axis=1)


def _fox_fwd(qa, ka, vt, name, tq=512, tk=1024):
    H, _, S = qa.shape
    Dh = HEAD_DIM
    tk = _fit(S, tk)
    tq = _fit(tk, tq)
    ratio = tk // tq

    def body(qa_ref, ka_ref, vt_ref, o_ref, lse_ref, m_ref, acc_ref, sa_ref, sb_ref, ta_ref, tb_ref):
        i = pl.program_id(1)
        qv = qa_ref[...] * QK_SCALE
        m_ref[...] = jnp.full_like(m_ref, NEG)
        acc_ref[...] = jnp.zeros_like(acc_ref)
        n = i // ratio
        q_off = (i - n * ratio) * tq

        def scores(j, s_ref, t_ref, diagonal):
            off = pl.multiple_of(j * tk, LANES)
            s = _dot(ka_ref[pl.ds(off, tk), :], qv, _NN)
            if diagonal:
                key = lax.broadcasted_iota(jnp.int32, (tk, tq), 0)
                qry = lax.broadcasted_iota(jnp.int32, (tk, tq), 1) + q_off
                s = jnp.where(key <= qry, s, NEG)
            s_ref[...] = s
            t_ref[...] = jnp.max(s, axis=0, keepdims=True)

        def absorb(j, s_ref, t_ref):
            off = pl.multiple_of(j * tk, LANES)
            m_old = m_ref[...]
            m_new = jnp.maximum(m_old, t_ref[...])
            p = jnp.exp(s_ref[...] - m_new)
            alpha = jnp.exp(m_old - m_new)
            acc_ref[...] = alpha * acc_ref[...] + _dot(vt_ref[:, pl.ds(off, tk)], p.astype(BF16), _NN)
            m_ref[...] = m_new

        scores(n, sa_ref, ta_ref, True)

        def loop_body(jj, carry):
            scores(2 * jj, sb_ref, tb_ref, False)
            absorb(jnp.where(jj == 0, n, 2 * jj - 1), sa_ref, ta_ref)
            scores(2 * jj + 1, sa_ref, ta_ref, False)
            absorb(2 * jj, sb_ref, tb_ref)
            return carry

        pairs = n // 2
        lax.fori_loop(0, pairs, loop_body, 0)
        held = jnp.where(pairs == 0, n, 2 * pairs - 1)

        @pl.when(n % 2 == 1)
        def _():
            scores(n - 1, sb_ref, tb_ref, False)
            absorb(held, sa_ref, ta_ref)
            absorb(n - 1, sb_ref, tb_ref)

        @pl.when(n % 2 == 0)
        def _():
            absorb(held, sa_ref, ta_ref)

        l = acc_ref[Dh:Dh + 1, :]
        o_ref[...] = acc_ref[0:Dh, :] / l
        lse_ref[...] = m_ref[...] + jnp.log(l)

    return pl.pallas_call(
        body, name=name, grid=(H, S // tq),
        in_specs=[pl.BlockSpec((None, LANES, tq), lambda h, i: (h, 0, i)),
                  pl.BlockSpec((None, S, LANES), lambda h, i: (h, 0, 0)),
                  pl.BlockSpec((None, Dh + ONES_ROWS, S), lambda h, i: (h, 0, 0))],
        out_specs=[pl.BlockSpec((None, Dh, tq), lambda h, i: (h, 0, i)),
                   pl.BlockSpec((None, 1, tq), lambda h, i: (h, 0, i))],
        out_shape=[jax.ShapeDtypeStruct((H, Dh, S), F32), jax.ShapeDtypeStruct((H, 1, S), F32)],
        scratch_shapes=[pltpu.VMEM((1, tq), F32), pltpu.VMEM((Dh + ONES_ROWS, tq), F32),
                        pltpu.VMEM((tk, tq), F32), pltpu.VMEM((tk, tq), F32),
                        pltpu.VMEM((1, tq), F32), pltpu.VMEM((1, tq), F32)],
        compiler_params=_params("parallel", "arbitrary"),
    )(qa, ka, vt)


def _fox_bwd(qa, ka, kta, va, doa, name, tq=1024, tk=512):
    H, _, S = qa.shape
    Dh, Da = HEAD_DIM, HEAD_DIM + ONES_ROWS
    tq = _fit(S, tq)
    tk = _fit(tq, tk)
    ratio = tq // tk
    nq = S // tq
    nk = S // tk

    def body(ka_ref, kta_ref, va_ref, qa_ref, doa_ref, dqt_ref, dkt_ref, dvt_ref, dka_ref, dva_ref):
        j = pl.program_id(1)

        @pl.when(j == 0)
        def _():
            dqt_ref[...] = jnp.zeros_like(dqt_ref)

        kv = ka_ref[...]
        ktv = kta_ref[0:Da, :]
        vv = va_ref[...]
        dka_ref[...] = jnp.zeros_like(dka_ref)
        dva_ref[...] = jnp.zeros_like(dva_ref)
        i_diag = j // ratio
        k_off = (j - i_diag * ratio) * tk

        def step(i, diagonal):
            off = pl.multiple_of(i * tq, LANES)
            qv = qa_ref[:, pl.ds(off, tq)] * QK_SCALE
            dov = doa_ref[:, pl.ds(off, tq)]
            e = _dot(kv, qv, _NN)
            if diagonal:
                key = lax.broadcasted_iota(jnp.int32, (tk, tq), 0) + k_off
                qry = lax.broadcasted_iota(jnp.int32, (tk, tq), 1)
                e = jnp.where(key <= qry, e, NEG)
            p_t = jnp.exp(e)
            dva_ref[...] += _dot(dov[0:Dh, :], p_t.astype(BF16), _NT)
            ds_b = (p_t * _dot(vv, dov, _NN)).astype(BF16)
            dka_ref[...] += _dot(qv[0:Da, :], ds_b, _NT)
            dqt_ref[:, pl.ds(off, tq)] += _dot(ktv, ds_b, _NN)

        step(i_diag, True)

        def loop_body(i, carry):
            step(i, False)
            return carry

        lax.fori_loop(i_diag + 1, nq, loop_body, 0)
        dkt_ref[...] = dka_ref[...]
        dvt_ref[...] = dva_ref[...]

        @pl.when(j == nk - 1)
        def _():
            dqt_ref[0:Dh, :] = dqt_ref[0:Dh, :] * QK_SCALE

    lanes_tile = pl.BlockSpec((None, tk, LANES), lambda h, j: (h, j, 0))
    rows_tile = pl.BlockSpec((None, LANES, tk), lambda h, j: (h, 0, j))
    rows_full = pl.BlockSpec((None, LANES, S), lambda h, j: (h, 0, 0))
    return pl.pallas_call(
        body, name=name, grid=(H, nk),
        in_specs=[lanes_tile, rows_tile, lanes_tile, rows_full, rows_full],
        out_specs=[pl.BlockSpec((None, Da, S), lambda h, j: (h, 0, 0)),
                   pl.BlockSpec((None, Da, tk), lambda h, j: (h, 0, j)),
                   pl.BlockSpec((None, Dh, tk), lambda h, j: (h, 0, j))],
        out_shape=[jax.ShapeDtypeStruct((H, Da, S), F32), jax.ShapeDtypeStruct((H, Da, S), F32),
                   jax.ShapeDtypeStruct((H, Dh, S), F32)],
        scratch_shapes=[pltpu.VMEM((Da, tk), F32), pltpu.VMEM((Dh, tk), F32)],
        compiler_params=_params("parallel", "arbitrary"),
    )(ka, kta, va, qa, doa)


DIL_Q_BLOCK = 3 * FOX_WIDTH // LANES
HEAD_PAIRS = N_DIL_HEADS // 2
PAIR_BLOCKS = DIL_WIDTH // LANES


def _band_geometry(S, d):
    L = S // d
    chunk = min(BAND_CHUNK_MAX, L)
    assert L % chunk == 0 and chunk % BAND == 0
    return L, chunk, chunk // BAND, L // chunk


def _band_in_specs(S, d, base):
    L, chunk, nb, _ = _band_geometry(S, d)

    def col(kind):
        return lambda hp, r, i: (r, i, base + kind * PAIR_BLOCKS + hp)

    def col_prev(kind):
        return lambda hp, r, i: (r, jnp.maximum(i * nb - 1, 0), base + kind * PAIR_BLOCKS + hp)

    main = [pl.BlockSpec((None, chunk, LANES), col(kind)) for kind in range(3)]
    prev = [pl.BlockSpec((None, BAND, LANES), col_prev(kind)) for kind in range(3)]
    bias = pl.BlockSpec((None, 2 * BAND, 2 * BAND), lambda hp, r, i: (hp, 0, 0))
    stat = pl.BlockSpec((None, 2, chunk), lambda hp, r, i: (hp, 0, r * (L // chunk) + i))
    tok = pl.BlockSpec((None, chunk, LANES), lambda hp, r, i: (r, i, hp))
    return main, prev, bias, stat, tok


def _to_residues(x, col_block, width, d, name, tr=512):
    S = x.shape[0]
    tr = _fit(S, tr)

    def body(x_ref, o_ref, tmp_ref):
        for j in range(width // LANES):
            cols = slice(j * LANES, (j + 1) * LANES)
            tmp_ref[j] = x_ref[:, cols].astype(F32)
            for r in range(d):
                o_ref[r, :, cols] = tmp_ref[j, pl.ds(r, tr // d, stride=d), :].astype(o_ref.dtype)

    return pl.pallas_call(
        body, name=name, grid=(S // tr,),
        in_specs=[pl.BlockSpec((tr, width), lambda i: (i, col_block))],
        out_specs=pl.BlockSpec((d, tr // d, width), lambda i: (0, i, 0)),
        out_shape=jax.ShapeDtypeStruct((d, S // d, width), x.dtype),
        scratch_shapes=[pltpu.VMEM((width // LANES, tr, LANES), F32)],
        compiler_params=_params("parallel"),
    )(x)


def _token_rows(ref, cols, tmp_ref):
    if len(ref.shape) == 2:
        return ref[:, cols].astype(F32)
    d, rows = ref.shape[0], ref.shape[1]
    for r in range(d):
        tmp_ref[pl.ds(r, rows, stride=d), :] = ref[r, :, cols].astype(F32)
    return tmp_ref[...]


def _row_spec(t, tr):
    if t.ndim == 2:
        return pl.BlockSpec((tr, t.shape[1]), lambda i: (i, 0))
    d = t.shape[0]
    return pl.BlockSpec((d, tr // d, t.shape[2]), lambda i: (0, i, 0))


def _head_lanes(a):
    return lax.broadcasted_iota(jnp.int32, (1, LANES), 1) // HEAD_DIM == a


def _one_head(x, a):
    return jnp.where(_head_lanes(a), x, jnp.zeros_like(x))


def _head_stack(x):
    return jnp.concatenate([_one_head(x, 0), _one_head(x, 1)], axis=0)


def _pair_rows(ref, rows):
    return jnp.concatenate([ref[0:1, rows], ref[1:2, rows]], axis=1)


def _band_scores_t(kb, q_stack, bias_t, first):
    s = _dot(kb, q_stack, _NT) + bias_t
    if first is not None:
        key = lax.broadcasted_iota(jnp.int32, s.shape, 0)
        s = jnp.where(jnp.logical_and(first, key < BAND), NEG, s)
    return s


def _pair_select(stacked):
    return jnp.where(_head_lanes(0), stacked[0:BAND, :], stacked[BAND:, :])


def _dil_lse(qkv_v, base, bias_t, name):
    d, L = qkv_v.shape[:2]
    S = L * d
    _, chunk, nb, nchunks = _band_geometry(S, d)
    main, prev, bias, stat, _ = _band_in_specs(S, d, base)

    def body(q_ref, k_ref, kp_ref, b_ref, lse_ref, kext_ref):
        first = pl.program_id(2) == 0
        kext_ref[0:BAND, :] = kp_ref[...]
        kext_ref[BAND:, :] = k_ref[...]
        for b in range(nb):
            rows, ext = slice(b * BAND, (b + 1) * BAND), slice(b * BAND, (b + 2) * BAND)
            s = _band_scores_t(kext_ref[ext, :], _head_stack(q_ref[rows, :] * QK_SCALE), b_ref[...],
                               first if b == 0 else None)
            m = jnp.max(s, axis=0, keepdims=True)
            lse = m + jnp.log(jnp.sum(jnp.exp(s - m), axis=0, keepdims=True))
            lse_ref[0:1, rows] = lse[:, 0:BAND]
            lse_ref[1:2, rows] = lse[:, BAND:]

    return pl.pallas_call(
        body, name=name, grid=(HEAD_PAIRS, d, nchunks),
        in_specs=[main[0], main[1], prev[1], bias], out_specs=stat,
        out_shape=jax.ShapeDtypeStruct((HEAD_PAIRS, 2, S), F32),
        scratch_shapes=[pltpu.VMEM((chunk + BAND, LANES), BF16)],
        compiler_params=_params("parallel", "parallel", "parallel"),
    )(qkv_v, qkv_v, qkv_v, bias_t)


def _dil_out(qkv_v, base, bias_t, lse_joint, name):
    d, L = qkv_v.shape[:2]
    S = L * d
    _, chunk, nb, nchunks = _band_geometry(S, d)
    main, prev, bias, stat, tok = _band_in_specs(S, d, base)

    def body(q_ref, k_ref, kp_ref, v_ref, vp_ref, b_ref, lse_ref, o_ref, kext_ref, vext_ref):
        first = pl.program_id(2) == 0
        kext_ref[0:BAND, :] = kp_ref[...]
        kext_ref[BAND:, :] = k_ref[...]
        vext_ref[0:BAND, :] = vp_ref[...]
        vext_ref[BAND:, :] = v_ref[...]
        for b in range(nb):
            rows, ext = slice(b * BAND, (b + 1) * BAND), slice(b * BAND, (b + 2) * BAND)
            s = _band_scores_t(kext_ref[ext, :], _head_stack(q_ref[rows, :] * QK_SCALE), b_ref[...],
                               first if b == 0 else None)
            p_t = jnp.exp(s - _pair_rows(lse_ref, rows))
            o_ref[rows, :] = _pair_select(_dot(p_t.astype(BF16), vext_ref[ext, :], _TN)).astype(BF16)

    return pl.pallas_call(
        body, name=name, grid=(HEAD_PAIRS, d, nchunks),
        in_specs=[main[0], main[1], prev[1], main[2], prev[2], bias, stat], out_specs=tok,
        out_shape=jax.ShapeDtypeStruct((d, L, DIL_WIDTH), BF16),
        scratch_shapes=[pltpu.VMEM((chunk + BAND, LANES), BF16), pltpu.VMEM((chunk + BAND, LANES), BF16)],
        compiler_params=_params("parallel", "parallel", "parallel"),
    )(qkv_v, qkv_v, qkv_v, qkv_v, qkv_v, bias_t, lse_joint)


def _dil_bwd(qkv_v, base, do_v, bias_t, lse_joint, delta, name):
    d, L = qkv_v.shape[:2]
    S = L * d
    _, chunk, nb, nchunks = _band_geometry(S, d)
    main, prev, bias, stat, tok = _band_in_specs(S, d, base)
    nblocks = L // BAND

    def nxt_row(i):
        return jnp.minimum((i + 1) * nb, nblocks - 1)

    q_next = pl.BlockSpec((None, BAND, LANES), lambda hp, r, i: (r, nxt_row(i), base + hp))
    do_next = pl.BlockSpec((None, BAND, LANES), lambda hp, r, i: (r, nxt_row(i), hp))
    stat_next = pl.BlockSpec((None, 2, BAND), lambda hp, r, i: (hp, 0, r * nblocks + nxt_row(i)))

    def body(q_ref, k_ref, kp_ref, v_ref, vp_ref, do_ref, b_ref, lse_ref, dl_ref,
             qn_ref, don_ref, lsen_ref, dln_ref,
             dq_ref, dk_ref, dv_ref, db_ref, kext_ref, vext_ref, dkext_ref, dvext_ref):
        r, i = pl.program_id(1), pl.program_id(2)
        first = i == 0
        has_next = i + 1 < nchunks
        tail = slice(BAND + chunk, 2 * BAND + chunk)
        kext_ref[0:BAND, :] = kp_ref[...]
        kext_ref[BAND:BAND + chunk, :] = k_ref[...]
        kext_ref[tail, :] = jnp.zeros((BAND, LANES), BF16)
        vext_ref[0:BAND, :] = vp_ref[...]
        vext_ref[BAND:BAND + chunk, :] = v_ref[...]
        vext_ref[tail, :] = jnp.zeros((BAND, LANES), BF16)
        dkext_ref[...] = jnp.zeros_like(dkext_ref)
        dvext_ref[...] = jnp.zeros_like(dvext_ref)

        @pl.when(jnp.logical_and(r == 0, i == 0))
        def _():
            db_ref[...] = jnp.zeros_like(db_ref)

        def block(q2, do2, lse_row, dl_row, ext, mask_rows):
            q_stack, do_stack = _head_stack(q2), _head_stack(do2)
            s = _dot(kext_ref[ext, :], q_stack, _NT) + b_ref[...]
            if mask_rows is not None:
                s = jnp.where(mask_rows, NEG, s)
            p_t = jnp.exp(s - lse_row)
            ds_t = p_t * (_dot(vext_ref[ext, :], do_stack, _NT) - dl_row)
            ds_b = ds_t.astype(BF16)
            dkext_ref[ext, :] += _dot(ds_b, q_stack, _NN)
            dvext_ref[ext, :] += _dot(p_t.astype(BF16), do_stack, _NN)
            return ds_t, ds_b

        key = lax.broadcasted_iota(jnp.int32, (2 * BAND, 2 * BAND), 0)
        all_lanes = slice(0, BAND)
        for b in range(nb):
            rows, ext = slice(b * BAND, (b + 1) * BAND), slice(b * BAND, (b + 2) * BAND)
            mask = jnp.logical_and(first, key < BAND) if b == 0 else None
            ds_t, ds_b = block(q_ref[rows, :] * QK_SCALE, do_ref[rows, :], _pair_rows(lse_ref, rows),
                               _pair_rows(dl_ref, rows), ext, mask)
            dq_ref[rows, :] = _pair_select(_dot(ds_b, kext_ref[ext, :], _TN)) * QK_SCALE
            db_ref[...] += ds_t
        block(qn_ref[...] * QK_SCALE, don_ref[...], _pair_rows(lsen_ref, all_lanes), _pair_rows(dln_ref, all_lanes),
              slice(chunk, chunk + 2 * BAND), jnp.logical_or(jnp.logical_not(has_next), key >= BAND))
        dk_ref[...] = dkext_ref[BAND:BAND + chunk, :]
        dv_ref[...] = dvext_ref[BAND:BAND + chunk, :]

    ext_rows = chunk + 2 * BAND
    return pl.pallas_call(
        body, name=name, grid=(HEAD_PAIRS, d, nchunks),
        in_specs=[main[0], main[1], prev[1], main[2], prev[2], tok, bias, stat, stat,
                  q_next, do_next, stat_next, stat_next],
        out_specs=[tok, tok, tok, bias],
        out_shape=[jax.ShapeDtypeStruct((d, L, DIL_WIDTH), F32)] * 3
                  + [jax.ShapeDtypeStruct((HEAD_PAIRS, 2 * BAND, 2 * BAND), F32)],
        scratch_shapes=[pltpu.VMEM((ext_rows, LANES), BF16), pltpu.VMEM((ext_rows, LANES), BF16),
                        pltpu.VMEM((ext_rows, LANES), F32), pltpu.VMEM((ext_rows, LANES), F32)],
        compiler_params=_params("arbitrary", "arbitrary", "arbitrary"),
    )(qkv_v, qkv_v, qkv_v, qkv_v, qkv_v, do_v, bias_t, lse_joint, delta, qkv_v, do_v, lse_joint, delta)


def _lse_join(lse3, name):
    P, H, S = lse3.shape

    def body(l_ref, o_ref):
        a, b, c = l_ref[0], l_ref[1], l_ref[2]
        m = jnp.maximum(jnp.maximum(a, b), c)
        o_ref[...] = m + jnp.log(jnp.exp(a - m) + jnp.exp(b - m) + jnp.exp(c - m))

    return pl.pallas_call(body, name=name, out_shape=jax.ShapeDtypeStruct((H, S), F32))(lse3)


def _bucket_reduce(dbias_t, bucket_map_t, name):
    P, H = dbias_t.shape[:2]

    def body(db_ref, bk_ref, o_ref):
        p, h = pl.program_id(0), pl.program_id(1)

        @pl.when(jnp.logical_and(p == 0, h == 0))
        def _():
            o_ref[...] = jnp.zeros_like(o_ref)

        db, bk = db_ref[...], bk_ref[...]
        row = lax.broadcasted_iota(jnp.int32, (N_BUCKETS, LANES), 0)
        lane = lax.broadcasted_iota(jnp.int32, (N_BUCKETS, LANES), 1)

        def one(b, acc):
            val = jnp.sum(jnp.sum(jnp.where(bk == b, db, 0.0), axis=1, keepdims=True), axis=0, keepdims=True)
            return acc + jnp.where(jnp.logical_and(row == b, lane == h), val, 0.0)

        o_ref[...] += lax.fori_loop(0, N_BUCKETS, one, jnp.zeros((N_BUCKETS, LANES), F32))

    return pl.pallas_call(
        body, name=name, grid=(P, H),
        in_specs=[pl.BlockSpec((None, None, 2 * BAND, BAND), lambda p, h: (p, h, 0, 0)),
                  pl.BlockSpec((None, 2 * BAND, BAND), lambda p, h: (p, 0, 0))],
        out_specs=pl.BlockSpec((N_BUCKETS, LANES), lambda p, h: (0, 0)),
        out_shape=jax.ShapeDtypeStruct((N_BUCKETS, LANES), F32),
        compiler_params=_params("arbitrary", "arbitrary"),
    )(dbias_t, bucket_map_t)


def _mem_fwd(q, kv, name, tq=1024):
    S, W = q.shape
    N = kv.shape[0]
    pairs = W // LANES
    tq = _fit(S, tq)

    def body(q_ref, k_ref, v_ref, o_ref, lse_ref):
        for a in range(2):
            lanes = slice(a * HEAD_DIM, (a + 1) * HEAD_DIM)
            s = _dot(k_ref[:, lanes], q_ref[:, lanes] * QK_SCALE, _NT)
            m = jnp.max(s, axis=0, keepdims=True)
            e = jnp.exp(s - m)
            l = jnp.sum(e, axis=0, keepdims=True)
            o_ref[:, lanes] = _dot((e / l).astype(BF16), v_ref[:, lanes], _TN).astype(BF16)
            lse_ref[a:a + 1, :] = m + jnp.log(l)

    return pl.pallas_call(
        body, name=name, grid=(pairs, S // tq),
        in_specs=[pl.BlockSpec((tq, LANES), lambda hp, i: (i, hp)),
                  pl.BlockSpec((N, LANES), lambda hp, i: (0, hp)),
                  pl.BlockSpec((N, LANES), lambda hp, i: (0, pairs + hp))],
        out_specs=[pl.BlockSpec((tq, LANES), lambda hp, i: (i, hp)),
                   pl.BlockSpec((None, 2, tq), lambda hp, i: (hp, 0, i))],
        out_shape=[jax.ShapeDtypeStruct((S, W), BF16), jax.ShapeDtypeStruct((pairs, 2, S), F32)],
        compiler_params=_params("parallel", "parallel"),
    )(q, kv, kv)


def _mem_bwd(q, kv, do, lse, delta, name, tq=1024):
    S, W = q.shape
    N = kv.shape[0]
    pairs = W // LANES
    tq = _fit(S, tq)

    def body(q_ref, k_ref, v_ref, do_ref, lse_ref, dl_ref, dq_ref, dk_ref, dv_ref):
        i = pl.program_id(1)

        @pl.when(i == 0)
        def _():
            dk_ref[...] = jnp.zeros_like(dk_ref)
            dv_ref[...] = jnp.zeros_like(dv_ref)

        for a in range(2):
            lanes = slice(a * HEAD_DIM, (a + 1) * HEAD_DIM)
            qv, dov = q_ref[:, lanes] * QK_SCALE, do_ref[:, lanes]
            kv_, vv = k_ref[:, lanes], v_ref[:, lanes]
            p_t = jnp.exp(_dot(kv_, qv, _NT) - lse_ref[a:a + 1, :])
            ds_t = p_t * (_dot(vv, dov, _NT) - dl_ref[a:a + 1, :])
            ds_b = ds_t.astype(BF16)
            dq_ref[:, lanes] = (_dot(ds_b, kv_, _TN) * QK_SCALE).astype(BF16)
            dk_ref[:, lanes] += _dot(ds_b, qv, _NN)
            dv_ref[:, lanes] += _dot(p_t.astype(BF16), dov, _NN)

    qs = pl.BlockSpec((tq, LANES), lambda hp, i: (i, hp))
    stat = pl.BlockSpec((None, 2, tq), lambda hp, i: (hp, 0, i))
    acc = pl.BlockSpec((N, LANES), lambda hp, i: (0, hp))
    return pl.pallas_call(
        body, name=name, grid=(pairs, S // tq),
        in_specs=[qs, acc, pl.BlockSpec((N, LANES), lambda hp, i: (0, pairs + hp)), qs, stat, stat],
        out_specs=[qs, acc, acc],
        out_shape=[jax.ShapeDtypeStruct((S, W), BF16), jax.ShapeDtypeStruct((N, W), F32),
                   jax.ShapeDtypeStruct((N, W), F32)],
        compiler_params=_params("parallel", "arbitrary"),
    )(q, kv, kv, do, lse, delta)


def _head_rowdot(a, bs, name, tr=512):
    S, W = a.shape
    tr = _fit(S, tr)

    def body(*refs):
        a_ref, b_refs, o_ref, tmp_ref = refs[0], refs[1:-2], refs[-2], refs[-1]
        col = lax.broadcasted_iota(jnp.int32, (LANES, LANES), 0)
        lane = lax.broadcasted_iota(jnp.int32, (LANES, LANES), 1)
        acc = jnp.zeros((tr, LANES), F32)
        for j in range(W // LANES):
            cols = slice(j * LANES, (j + 1) * LANES)
            tot = _token_rows(b_refs[0], cols, tmp_ref)
            for r in b_refs[1:]:
                tot = tot + _token_rows(r, cols, tmp_ref)
            sel = jnp.where(col // HEAD_DIM + j * (LANES // HEAD_DIM) == lane, 1.0, 0.0).astype(F32)
            acc = acc + lax.dot_general(a_ref[:, cols].astype(F32) * tot, sel, (_NN, ((), ())),
                                        precision=lax.Precision.HIGHEST, preferred_element_type=F32)
        o_ref[...] = acc

    return pl.pallas_call(
        body, name=name, grid=(S // tr,), in_specs=[_row_spec(t, tr) for t in [a] + list(bs)],
        out_specs=pl.BlockSpec((tr, LANES), lambda i: (i, 0)),
        out_shape=jax.ShapeDtypeStruct((S, LANES), F32),
        scratch_shapes=[pltpu.VMEM((tr, LANES), F32)],
        compiler_params=_params("parallel"),
    )(a, *bs)


def _sum_cast_cols(groups, out_dtype, name, tail=None, tr=256):
    first = groups[0][0]
    S, W = (first.shape if first.ndim == 2 else (first.shape[0] * first.shape[1], first.shape[2]))
    tr = _fit(S, tr)
    flat = [t for g in groups for t in g] + ([tail] if tail is not None else [])
    tail_w = 0 if tail is None else tail.shape[1]

    def body(*refs):
        o_ref, tmp_ref = refs[-2], refs[-1]
        if tail is not None:
            o_ref[:, W * len(groups):] = refs[-3][...].astype(out_dtype)
        k = 0
        for gi, g in enumerate(groups):
            for j in range(W // LANES):
                cols = slice(j * LANES, (j + 1) * LANES)
                acc = _token_rows(refs[k], cols, tmp_ref)
                for r in refs[k + 1:k + len(g)]:
                    acc = acc + _token_rows(r, cols, tmp_ref)
                o_ref[:, gi * W + j * LANES:gi * W + (j + 1) * LANES] = acc.astype(out_dtype)
            k += len(g)

    return pl.pallas_call(
        body, name=name, grid=(S // tr,), in_specs=[_row_spec(t, tr) for t in flat],
        out_specs=pl.BlockSpec((tr, W * len(groups) + tail_w), lambda i: (i, 0)),
        out_shape=jax.ShapeDtypeStruct((S, W * len(groups) + tail_w), out_dtype),
        scratch_shapes=[pltpu.VMEM((tr, LANES), F32)],
        compiler_params=_params("parallel"),
    )(*flat)


FF_TILE = 256


def _ffn_up(h, w_gu, name, tm=2048):
    S, D = h.shape
    F2 = w_gu.shape[1]
    tm = _fit(S, tm)

    def body(h_ref, w_ref, gu_ref, act_ref):
        gu = _dot(h_ref[...], w_ref[...], _NN)
        gu_ref[...] = gu.astype(BF16)
        g, u = gu[:, :FF_TILE], gu[:, FF_TILE:]
        act_ref[...] = (g * (1.0 / (1.0 + jnp.exp(-g))) * u).astype(BF16)

    return pl.pallas_call(
        body, name=name, grid=(S // tm, F2 // (2 * FF_TILE)),
        in_specs=[pl.BlockSpec((tm, D), lambda i, j: (i, 0)), pl.BlockSpec((D, 2 * FF_TILE), lambda i, j: (0, j))],
        out_specs=[pl.BlockSpec((tm, 2 * FF_TILE), lambda i, j: (i, j)),
                   pl.BlockSpec((tm, FF_TILE), lambda i, j: (i, j))],
        out_shape=[jax.ShapeDtypeStruct((S, F2), BF16), jax.ShapeDtypeStruct((S, F2 // 2), BF16)],
        compiler_params=_params("parallel", "arbitrary"),
    )(h, w_gu)


def _ffn_dact(dy, w_down, gu, name, tm=2048):
    S, D = dy.shape
    F2 = gu.shape[1]
    tm = _fit(S, tm)

    def body(dy_ref, w_ref, gu_ref, dgu_ref):
        dact = _dot(dy_ref[...], w_ref[...], _NT)
        gu_v = gu_ref[...].astype(F32)
        g, u = gu_v[:, :FF_TILE], gu_v[:, FF_TILE:]
        sig = 1.0 / (1.0 + jnp.exp(-g))
        silu = g * sig
        dgu_ref[:, :FF_TILE] = (dact * u * (sig + silu * (1.0 - sig))).astype(BF16)
        dgu_ref[:, FF_TILE:] = (dact * silu).astype(BF16)

    return pl.pallas_call(
        body, name=name, grid=(S // tm, F2 // (2 * FF_TILE)),
        in_specs=[pl.BlockSpec((tm, D), lambda i, j: (i, 0)), pl.BlockSpec((FF_TILE, D), lambda i, j: (j, 0)),
                  pl.BlockSpec((tm, 2 * FF_TILE), lambda i, j: (i, j))],
        out_specs=pl.BlockSpec((tm, 2 * FF_TILE), lambda i, j: (i, j)),
        out_shape=jax.ShapeDtypeStruct((S, F2), BF16),
        compiler_params=_params("parallel", "arbitrary"),
    )(dy, w_down, gu)


def _fit_rows(n, cap):
    if n <= cap:
        return n
    t = (cap // 8) * 8
    while t >= 8:
        if n % t == 0:
            return t
        t -= 8
    raise ValueError(f"no sublane-aligned tile for {n} under {cap}")


def _add_n(arrs, name, tr=512):
    R, C = arrs[0].shape
    tr = _fit_rows(R, tr)

    def body(*refs):
        acc = refs[0][...].astype(F32)
        for r in refs[1:-1]:
            acc = acc + r[...].astype(F32)
        refs[-1][...] = acc

    row = pl.BlockSpec((tr, C), lambda i: (i, 0))
    return pl.pallas_call(
        body, name=name, grid=(R // tr,), in_specs=[row] * len(arrs), out_specs=row,
        out_shape=jax.ShapeDtypeStruct((R, C), F32), compiler_params=_params("parallel"),
    )(*arrs)


def _adamw(w, g, m, v, name, tr=512):
    R, C = w.shape
    tr = _fit_rows(R, tr)
    c1 = 1.0 / (1.0 - ADAM_B1 ** ADAM_STEP)
    c2 = 1.0 / (1.0 - ADAM_B2 ** ADAM_STEP)

    def body(w_ref, g_ref, m_ref, v_ref, d_ref, nm_ref, nv_ref):
        gv = g_ref[...]
        nm = ADAM_B1 * m_ref[...] + (1.0 - ADAM_B1) * gv
        nv = ADAM_B2 * v_ref[...] + (1.0 - ADAM_B2) * (gv * gv)
        nm_ref[...] = nm
        nv_ref[...] = nv
        d_ref[...] = -ADAM_LR * ((nm * c1) / (jnp.sqrt(nv * c2) + ADAM_EPS) + ADAM_WD * w_ref[...])

    row = pl.BlockSpec((tr, C), lambda i: (i, 0))
    return pl.pallas_call(
        body, name=name, grid=(R // tr,), in_specs=[row] * 4, out_specs=[row] * 3,
        out_shape=[jax.ShapeDtypeStruct((R, C), F32)] * 3, compiler_params=_params("parallel"),
    )(w, g, m, v)


def _place():
    return lax.axis_index("x"), lax.axis_index("y"), lax.axis_index("c")


_ANY = pl.BlockSpec(memory_space=pl.ANY)


def _chip_all_gather(shard, name):
    R, C = shard.shape
    half = R // 2

    def body(x_ref, out_ref, send_sems, recv_sems, local_sem):
        x, y, c = _place()
        chips = [(1 - x, y), (x, 1 - y), (1 - x, 1 - y)]
        sibling = (x, y, 1 - c)
        mine = pltpu.make_async_copy(x_ref, out_ref.at[2 * x + y], local_sem)
        mine.start()

        def rows(chip, core):
            return out_ref.at[chip, pl.ds(core * half, half)]

        def copy(k, chip, core, to, src=None):
            return pltpu.make_async_remote_copy(
                src_ref=rows(chip, core) if src is None else src, dst_ref=rows(chip, core),
                send_sem=send_sems.at[k], recv_sem=recv_sems.at[k], device_id=to, device_id_type=MESH_IDS)

        me = 2 * x + y
        first = [copy(k, me, c, (cx, cy, c), src=x_ref.at[pl.ds(c * half, half)]) for k, (cx, cy) in enumerate(chips)]
        for cp in first:
            cp.start()
        passed = [copy(3 + k, 2 * cx + cy, c, sibling) for k, (cx, cy) in enumerate(chips)]
        for k, (cx, cy) in enumerate(chips):
            copy(k, 2 * cx + cy, c, (cx, cy, c)).wait_recv()
            passed[k].start()
        for k, (cx, cy) in enumerate(chips):
            copy(3 + k, 2 * cx + cy, 1 - c, sibling).wait_recv()
        for cp in first + passed:
            cp.wait_send()
        mine.wait()

    return pl.pallas_call(
        body, name=name, in_specs=[_ANY], out_specs=_ANY,
        out_shape=jax.ShapeDtypeStruct((N_CHIPS, R, C), shard.dtype),
        scratch_shapes=[pltpu.SemaphoreType.DMA((6,)), pltpu.SemaphoreType.DMA((6,)), pltpu.SemaphoreType.DMA],
    )(shard)


def _sibling_exchange(buf, name):
    def body(x_ref, out_ref, send_sem, recv_sem):
        x, y, c = _place()
        cp = pltpu.make_async_remote_copy(
            src_ref=x_ref, dst_ref=out_ref, send_sem=send_sem, recv_sem=recv_sem,
            device_id=(x, y, 1 - c), device_id_type=MESH_IDS)
        cp.start()
        cp.wait()

    return pl.pallas_call(
        body, name=name, in_specs=[_ANY], out_specs=_ANY,
        out_shape=jax.ShapeDtypeStruct(buf.shape, buf.dtype),
        scratch_shapes=[pltpu.SemaphoreType.DMA, pltpu.SemaphoreType.DMA],
    )(buf)


def _chip_scatter(parts, name):
    _, R, C = parts.shape

    def body(p_ref, out_ref, send_sems, recv_sems):
        x, y, c = _place()
        chips = [(1 - x, y), (x, 1 - y), (1 - x, 1 - y)]

        def copy(k, slab, to):
            return pltpu.make_async_remote_copy(
                src_ref=p_ref.at[slab], dst_ref=out_ref.at[k], send_sem=send_sems.at[k], recv_sem=recv_sems.at[k],
                device_id=to, device_id_type=MESH_IDS)

        sends = [copy(k, 2 * cx + cy, (cx, cy, c)) for k, (cx, cy) in enumerate(chips)]
        for cp in sends:
            cp.start()
        for cp in sends:
            cp.wait_recv()
        for cp in sends:
            cp.wait_send()

    return pl.pallas_call(
        body, name=name, in_specs=[_ANY], out_specs=_ANY,
        out_shape=jax.ShapeDtypeStruct((3, R, C), parts.dtype),
        scratch_shapes=[pltpu.SemaphoreType.DMA((3,)), pltpu.SemaphoreType.DMA((3,))],
    )(parts)


def _all_to_all_small(vec, name):
    R, C = vec.shape

    def body(v_ref, out_ref, send_sems, recv_sems, local_sem):
        x, y, c = _place()
        me = 4 * x + 2 * y + c
        mine = pltpu.make_async_copy(v_ref, out_ref.at[me], local_sem)
        mine.start()
        flips = [(dx, dy, dc) for dx in (0, 1) for dy in (0, 1) for dc in (0, 1)][1:]

        def peer(f):
            return (x ^ f[0], y ^ f[1], c ^ f[2])

        def copy(k, slot, to):
            return pltpu.make_async_remote_copy(
                src_ref=v_ref, dst_ref=out_ref.at[slot], send_sem=send_sems.at[k], recv_sem=recv_sems.at[k],
                device_id=to, device_id_type=MESH_IDS)

        sends = [copy(k, me, peer(f)) for k, f in enumerate(flips)]
        for cp in sends:
            cp.start()
        for k, f in enumerate(flips):
            px, py, pc = peer(f)
            copy(k, 4 * px + 2 * py + pc, peer(f)).wait_recv()
        for cp in sends:
            cp.wait_send()
        mine.wait()

    return pl.pallas_call(
        body, name=name, in_specs=[_ANY], out_specs=_ANY,
        out_shape=jax.ShapeDtypeStruct((8, R, C), vec.dtype),
        scratch_shapes=[pltpu.SemaphoreType.DMA((7,)), pltpu.SemaphoreType.DMA((7,)), pltpu.SemaphoreType.DMA],
    )(vec)


def _to_heads(t, n):
    S = t.shape[0]
    return t.reshape(S, n, HEAD_DIM).transpose(1, 0, 2)


def _to_heads_t(t, n):
    S = t.shape[0]
    return t.T.reshape(n, HEAD_DIM, S)


def _from_heads_t(t):
    H, Dh, S = t.shape
    return t.reshape(H * Dh, S).T


def _t5_bucket(dist):
    max_exact = N_BUCKETS // 2
    d = np.maximum(dist, 1).astype(np.float32)
    large = max_exact + (np.log(d / max_exact) / np.log(MAX_DISTANCE / max_exact)
                         * (N_BUCKETS - max_exact)).astype(np.int32)
    large = np.minimum(large, N_BUCKETS - 1)
    return np.where(dist < max_exact, dist, large).astype(np.int32)


def _band_tables():
    qi = np.arange(BAND)[:, None]
    kj = np.arange(2 * BAND)[None, :]
    sub = qi + BAND - kj
    band = (sub >= 0) & (sub <= BAND)
    out = []
    for d in DILATIONS:
        bucket = _t5_bucket(np.clip(sub, 0, BAND) * d)
        out.append(np.where(band, bucket, -1).astype(np.int32))
    return np.stack(out)


_PACK = (("w_in", 770), ("w_out", 256), ("w_xq", 64), ("w_xk", 64), ("w_xv", 64), ("w_xo", 64),
         ("w_gate", 704), ("w_up", 704), ("w_down", 704))


def _pack(shards):
    rows = [shards[n].reshape(-1, PACK_COLS) for n, _ in _PACK]
    total = sum(r.shape[0] for r in rows)
    pad = (-total) % 128
    if pad:
        rows.append(jnp.zeros((pad, PACK_COLS), rows[0].dtype))
    return jnp.concatenate(rows, axis=0)


def _unpack(pack, shapes):
    out, r = {}, 0
    for n, _ in _PACK:
        cnt = int(np.prod(shapes[n])) // PACK_COLS
        out[n] = pack[r:r + cnt].reshape(shapes[n])
        r += cnt
    return out


_COL_SHARDED = ("w_in", "w_xo", "w_gate", "w_up")


def _full_weight(gathered, name):
    return jnp.concatenate(gathered, axis=1 if name in _COL_SHARDED else 0)


def _split_weight(full, name):
    return jnp.split(full, N_CHIPS, axis=1 if name in _COL_SHARDED else 0)


_SMALL = ("g_mix_pre", "g_mix_post", "g_xattn_pre", "g_mem", "g_xattn_post", "g_ffn_pre", "g_ffn_post")


def _pack_small(vals):
    D = vals["g_mix_pre"].shape[1]
    rows = [vals[n].reshape(1, D) for n in _SMALL]
    misc = jnp.concatenate([vals["b_f"].reshape(-1), vals["rel_bias"].reshape(-1)])
    rows.append(jnp.pad(misc, (0, D - misc.shape[0])).reshape(1, D))
    rows.append(jnp.zeros((16 - len(rows), D), F32))
    return jnp.concatenate(rows, axis=0)


def _unpack_small(pack):
    out = {n: pack[i:i + 1] for i, n in enumerate(_SMALL)}
    out["b_f"] = pack[7, 0:N_FOX_HEADS].reshape(1, N_FOX_HEADS)
    out["rel_bias"] = pack[7, N_FOX_HEADS:N_FOX_HEADS + N_BUCKETS * N_DIL_HEADS].reshape(N_BUCKETS, N_DIL_HEADS)
    return out


def kernel(x, mem, g_mix_pre, w_in, b_f, rel_bias, w_out, g_mix_post, g_xattn_pre, g_mem, w_xq, w_xk, w_xv, w_xo, g_xattn_post, g_ffn_pre, w_gate, w_up, w_down, g_ffn_post, loss_target, m_g_mix_pre, m_w_in, m_b_f, m_rel_bias, m_w_out, m_g_mix_post, m_g_xattn_pre, m_g_mem, m_w_xq, m_w_xk, m_w_xv, m_w_xo, m_g_xattn_post, m_g_ffn_pre, m_w_gate, m_w_up, m_w_down, m_g_ffn_post, v_g_mix_pre, v_w_in, v_b_f, v_rel_bias, v_w_out, v_g_mix_post, v_g_xattn_pre, v_g_mem, v_w_xq, v_w_xk, v_w_xv, v_w_xo, v_g_xattn_post, v_g_ffn_pre, v_w_gate, v_w_up, v_w_down, v_g_ffn_post):
    args = dict(locals())
    big = [n for n, _ in _PACK]
    names = ["g_mix_pre", "w_in", "b_f", "rel_bias", "w_out", "g_mix_post", "g_xattn_pre", "g_mem", "w_xq",
             "w_xk", "w_xv", "w_xo", "g_xattn_post", "g_ffn_pre", "w_gate", "w_up", "w_down", "g_ffn_post"]
    xs = x[0]
    S, D = xs.shape
    assert S % (BAND * DILATIONS[-1]) == 0
    shard_shapes = {n: args[n].shape[1:] for n in big}
    my_x, my_y, my_c = lax.axis_index("x"), lax.axis_index("y"), lax.axis_index("c")

    gathered = _chip_all_gather(_pack({n: args[n][0].astype(BF16) for n in big}), "weights_all_gather")
    per_chip = [_unpack(gathered[j], shard_shapes) for j in range(N_CHIPS)]
    W = {n: _full_weight([pc[n] for pc in per_chip], n) for n in big}
    w_fox, w_fg, w_dil = (W["w_in"][:, :3 * FOX_WIDTH], W["w_in"][:, 3 * FOX_WIDTH:3 * FOX_WIDTH + N_FOX_HEADS],
                          W["w_in"][:, 3 * FOX_WIDTH + N_FOX_HEADS:])
    w_qkv = jnp.concatenate([w_fox, w_dil], axis=1)
    w_fg_pad = jnp.pad(w_fg, ((0, 0), (0, LANES - N_FOX_HEADS)))
    F = W["w_gate"].shape[1]
    nft = F // FF_TILE
    w_gu = jnp.stack([W["w_gate"].reshape(D, nft, FF_TILE), W["w_up"].reshape(D, nft, FF_TILE)],
                     axis=2).reshape(D, 2 * F)

    h1 = _rms_fwd(xs, g_mix_pre, "rms_mix_pre")
    qkv = _mm(h1, w_qkv, "nn", BF16, "proj_qkv")
    fg = _mm(h1, w_fg_pad, "nn", F32, "proj_gate")
    fg_t = fg[:, :N_FOX_HEADS].T
    b_col = b_f.reshape(N_FOX_HEADS, 1)
    c_t = _forget_fwd(fg_t, b_col, "forget_cumsum")
    fq_s, fk_s, fv_s = (qkv[:, i * FOX_WIDTH:(i + 1) * FOX_WIDTH] for i in range(3))
    fqt, fvt = _to_heads_t(fq_s, N_FOX_HEADS), _to_heads_t(fv_s, N_FOX_HEADS)
    unit = jnp.full((N_FOX_HEADS, S), 1.0, BF16)
    inv_scale = jnp.full((N_FOX_HEADS, S), 1.0 / QK_SCALE, BF16)
    ka = _lanes_operand(_to_heads(fk_s, N_FOX_HEADS), list(_split3(-c_t)) + [unit] * EXTRA)
    qa_f = _rows_operand(fqt, [inv_scale] * EXTRA)
    o_fox_t, lse_fox = _fox_fwd(qa_f, ka, _with_ones(fvt), "fox_fwd")

    bucket_map = _band_tables()
    onehot = (jnp.asarray(bucket_map)[..., None] == jnp.arange(N_BUCKETS)).astype(F32)
    bias_tab = jnp.einsum("pqkb,bh->phkq", onehot, rel_bias, precision=lax.Precision.HIGHEST)
    bias_tab = jnp.where(jnp.asarray(bucket_map.transpose(0, 2, 1) >= 0)[:, None], bias_tab, NEG)
    bias_t = bias_tab.reshape(3, HEAD_PAIRS, 2, 2 * BAND, BAND).transpose(0, 1, 3, 2, 4).reshape(
        3, HEAD_PAIRS, 2 * BAND, 2 * BAND)
    views = [(qkv.reshape(1, S, qkv.shape[1]), DIL_Q_BLOCK)] + [
        (_to_residues(qkv, 1, 3 * DIL_WIDTH, d, f"dilated_qkv_residues_{d}"), 0) for d in DILATIONS[1:]]

    def to_tok(stat, d):
        return stat.reshape(N_DIL_HEADS, d, S // d).swapaxes(1, 2).reshape(N_DIL_HEADS, S)

    def to_perm(stat, d):
        return stat.reshape(N_DIL_HEADS, S // d, d).swapaxes(1, 2).reshape(HEAD_PAIRS, 2, S)

    def tok_or_res(t):
        return t.reshape(t.shape[1:]) if t.shape[0] == 1 else t

    lse_tok = jnp.stack([to_tok(_dil_lse(*views[p], bias_t[p], f"dilated_lse_{d}"), d)
                         for p, d in enumerate(DILATIONS)])
    lse_joint = _lse_join(lse_tok, "dilated_lse_join")
    lse_perm = [to_perm(lse_joint, d) for d in DILATIONS]
    o_dil = [tok_or_res(_dil_out(*views[p], bias_t[p], lse_perm[p], f"dilated_out_{d}"))
             for p, d in enumerate(DILATIONS)]
    o_cat = _sum_cast_cols([[_from_heads_t(o_fox_t)]] + [[o] for o in o_dil], BF16, "mixer_out_cat")
    w_out_b = W["w_out"]
    w_out_cat = jnp.concatenate([w_out_b[:FOX_WIDTH]] + [w_out_b[FOX_WIDTH:]] * 3, axis=0)
    a = _mm(o_cat, w_out_cat, "nn", F32, "proj_out", tk=2048)
    x1, h2 = _resid_norm(xs, a, g_mix_post, g_xattn_pre, "resid_mix")

    hm = _rms_fwd(mem[0], g_mem, "rms_mem")
    q2 = _mm(h2, W["w_xq"], "nn", BF16, "xattn_q")
    w_xkv = jnp.concatenate([W["w_xk"], W["w_xv"]], axis=1)
    kvm = _mm(hm, w_xkv, "nn", BF16, "xattn_kv")
    MW = N_MEM_HEADS * HEAD_DIM
    oc, lse_mem = _mem_fwd(q2, kvm, "xattn_fwd")
    y2 = _mm(oc, W["w_xo"], "nn", F32, "xattn_o")
    x2, h3 = _resid_norm(x1, y2, g_xattn_post, g_ffn_pre, "resid_xattn")

    gu, act = _ffn_up(h3, w_gu, "ffn_up")
    y3 = _mm(act, W["w_down"], "nn", F32, "ffn_down", tk=2816)
    dx3, loss_tile = _final_loss(x2, y3, g_ffn_post, loss_target[0], "final_loss")

    grads = {}
    small = {}
    _, dy3_b, dg = _rms_bwd(y3, g_ffn_post, dx3, None, "bwd_norm_ffn_post", want=("bf16",))
    small["g_ffn_post"] = dg[0:1]
    grads["w_down"] = _mm(act, dy3_b, "tn", F32, "grad_w_down", tm=1408)
    dgu = _ffn_dact(dy3_b, W["w_down"], gu, "ffn_dact")
    dw_gu = _mm(h3, dgu, "tn", F32, "grad_w_gu", tn=1408).reshape(D, nft, 2, FF_TILE)
    grads["w_gate"], grads["w_up"] = dw_gu[:, :, 0].reshape(D, F), dw_gu[:, :, 1].reshape(D, F)
    dh3 = _mm(dgu, w_gu, "nt", F32, "bwd_ffn_in", tk=1408)
    dx2, _, dg = _rms_bwd(x2, g_ffn_pre, dh3, dx3, "bwd_norm_ffn_pre", want=("f32",))
    small["g_ffn_pre"] = dg[0:1]

    _, dy2_b, dg = _rms_bwd(y2, g_xattn_post, dx2, None, "bwd_norm_xattn_post", want=("bf16",))
    small["g_xattn_post"] = dg[0:1]
    grads["w_xo"] = _mm(oc, dy2_b, "tn", F32, "grad_w_xo")
    doc = _mm(dy2_b, W["w_xo"], "nt", BF16, "bwd_xattn_o")
    delta_mem = _head_rowdot(doc, [oc], "xattn_delta")[:, :N_MEM_HEADS].T.reshape(N_MEM_HEADS // 2, 2, S)
    dq2, dkm, dvm = _mem_bwd(q2, kvm, doc, lse_mem, delta_mem, "xattn_bwd")
    dkvm = jnp.concatenate([dkm, dvm], axis=1).astype(BF16)
    grads["w_xq"] = _mm(h2, dq2, "tn", F32, "grad_w_xq")
    dw_xkv = _mm(hm, dkvm, "tn", F32, "grad_w_xkv")
    grads["w_xk"], grads["w_xv"] = dw_xkv[:, :MW], dw_xkv[:, MW:]
    dhm = _mm(dkvm, w_xkv, "nt", F32, "bwd_xattn_kv")
    _, _, dg = _rms_bwd(mem[0], g_mem, dhm, None, "bwd_norm_mem", want=())
    small["g_mem"] = dg[0:1]
    dh2 = _mm(dq2, W["w_xq"], "nt", F32, "bwd_xattn_q")
    dx1, _, dg = _rms_bwd(x1, g_xattn_pre, dh2, dx2, "bwd_norm_xattn_pre", want=("f32",))
    small["g_xattn_pre"] = dg[0:1]

    _, da_b, dg = _rms_bwd(a, g_mix_post, dx1, None, "bwd_norm_mix_post", want=("bf16",))
    small["g_mix_post"] = dg[0:1]
    dw_out_cat = _mm(o_cat, da_b, "tn", F32, "grad_w_out")
    dw_out_dil = _add_n([dw_out_cat[FOX_WIDTH + p * DIL_WIDTH:FOX_WIDTH + (p + 1) * DIL_WIDTH] for p in range(3)],
                        "grad_w_out_dil")
    grads["w_out"] = jnp.concatenate([dw_out_cat[:FOX_WIDTH], dw_out_dil], axis=0)
    do = _mm(da_b, w_out_b, "nt", BF16, "bwd_proj_out")
    do_fox, do_dil = do[:, :FOX_WIDTH], do[:, FOX_WIDTH:]

    delta_fox = _head_rowdot(do_fox, [o_cat[:, :FOX_WIDTH]], "fox_delta")[:, :N_FOX_HEADS].T
    qa_b = lax.dynamic_update_slice(qa_f, jnp.stack(_split3(lse_fox[:, 0] * (-1.0 / QK_SCALE)), axis=1),
                                    (0, HEAD_DIM + EXTRA, 0))
    va = _lanes_operand(_to_heads(fv_s, N_FOX_HEADS), [unit] * EXTRA)
    doa = _rows_operand(_to_heads_t(do_fox, N_FOX_HEADS), list(_split3(-delta_fox)))
    dq_aug, dk_aug, dvf = _fox_bwd(qa_b, ka, ka.transpose(0, 2, 1), va, doa, "fox_bwd")
    dqf, dkf = dq_aug[:, :HEAD_DIM], dk_aug[:, :HEAD_DIM]
    dfg_t, db_f = _forget_bwd(fg_t, b_col, dq_aug[:, HEAD_DIM + EXTRA], dk_aug[:, HEAD_DIM], "forget_bwd")

    delta_dil = _head_rowdot(do_dil, o_dil, "dilated_delta")[:, :N_DIL_HEADS].T
    do_res = [do_dil.reshape(1, S, DIL_WIDTH)] + [
        _to_residues(do, 1, DIL_WIDTH, d, f"dilated_do_residues_{d}") for d in DILATIONS[1:]]
    dil_grads = [_dil_bwd(*views[p], do_res[p], bias_t[p], lse_perm[p], to_perm(delta_dil, d), f"dilated_bwd_{d}")
                 for p, d in enumerate(DILATIONS)]
    dbias_t = jnp.stack([g[3].reshape(HEAD_PAIRS, 2 * BAND, 2, BAND).transpose(0, 2, 1, 3).reshape(
        N_DIL_HEADS, 2 * BAND, BAND) for g in dil_grads])
    d_rel = _bucket_reduce(dbias_t, jnp.asarray(bucket_map.transpose(0, 2, 1)), "rel_bias_grad")[:, :N_DIL_HEADS]
    dfg_pad = jnp.pad(dfg_t.T, ((0, 0), (0, LANES - N_FOX_HEADS))).astype(BF16)
    dcat = _sum_cast_cols([[_from_heads_t(dqf)], [_from_heads_t(dkf)], [_from_heads_t(dvf)]]
                          + [[tok_or_res(g[j]) for g in dil_grads] for j in range(3)],
                          BF16, "dqkv_assemble", tail=dfg_pad)
    dw_cat = _mm(h1, dcat, "tn", F32, "grad_w_qkv", tm=512, tn=3200)
    n_qkv = 3 * (FOX_WIDTH + DIL_WIDTH)
    grads["w_in"] = jnp.concatenate([dw_cat[:, :3 * FOX_WIDTH], dw_cat[:, n_qkv:n_qkv + N_FOX_HEADS],
                                     dw_cat[:, 3 * FOX_WIDTH:n_qkv]], axis=1)
    w_cat = jnp.concatenate([w_qkv, w_fg_pad], axis=1)
    dh1 = _mm(dcat, w_cat, "nt", F32, "bwd_proj_in", tk=3200)
    grad_x, _, dg = _rms_bwd(xs, g_mix_pre, dh1, dx1, "bwd_norm_mix_pre", want=("f32",))
    small["g_mix_pre"] = dg[0:1]
    small["b_f"] = db_f[:, 0].reshape(1, N_FOX_HEADS)
    small["rel_bias"] = d_rel

    split = {n: _split_weight(grads[n], n) for n in big}
    parts = jnp.stack([_pack({n: split[n][j].astype(BF16) for n in big}) for j in range(N_CHIPS)])
    R = parts.shape[1]
    half = R // 2
    keep = lax.dynamic_slice_in_dim(parts, my_c * half, half, axis=1)
    give = lax.dynamic_slice_in_dim(parts, (1 - my_c) * half, half, axis=1)
    got = _sibling_exchange(give, "grads_to_sibling")
    chip_sum = _add_n([keep.reshape(-1, PACK_COLS), got.reshape(-1, PACK_COLS)], "grads_add_sibling")
    chip_sum = chip_sum.reshape(N_CHIPS, half, PACK_COLS)
    my_chip = 2 * my_x + my_y
    from_chips = _chip_scatter(chip_sum.astype(BF16), "grads_to_chips")
    own = lax.dynamic_index_in_dim(chip_sum, my_chip, axis=0, keepdims=False)
    g_half = _add_n([own, from_chips[0], from_chips[1], from_chips[2]], "grads_add_chips")
    other_half = _sibling_exchange(g_half, "grads_share_sibling")
    g_pack = jnp.where(my_c == 0, jnp.concatenate([g_half, other_half]), jnp.concatenate([other_half, g_half]))

    small_pack = _pack_small(small)
    small_pack = small_pack.at[8, 0].set(loss_tile[0, 0])
    everyone = _all_to_all_small(small_pack, "small_all_gather")
    small_sum = _add_n([everyone[i] for i in range(8)], "small_sum")
    loss = small_sum[8, 0]
    g_small = _unpack_small(small_sum)

    outs = {"grad": _unpack(g_pack, shard_shapes), "delta": {}, "new_m": {}, "new_v": {}}
    for n in big:
        outs["delta"][n], outs["new_m"][n], outs["new_v"][n] = _adamw(
            args[n][0], outs["grad"][n], args["m_" + n][0], args["v_" + n][0], f"adamw_{n}")
    sw = _pack_small({n: args[n] for n in _SMALL + ("b_f", "rel_bias")})
    sm = _pack_small({n: args["m_" + n] for n in _SMALL + ("b_f", "rel_bias")})
    sv = _pack_small({n: args["v_" + n] for n in _SMALL + ("b_f", "rel_bias")})
    sd, snm, snv = _adamw(sw, small_sum.at[8, 0].set(0.0), sm, sv, "adamw_small")
    souts = {"grad": g_small, "delta": _unpack_small(sd), "new_m": _unpack_small(snm), "new_v": _unpack_small(snv)}

    def leaf(kind, n):
        if n in souts[kind]:
            return souts[kind][n].reshape(args[n].shape)
        return outs[kind][n].reshape(args[n].shape)

    result = [loss, grad_x.reshape(x.shape)]
    for kind in ("grad", "delta", "new_m", "new_v"):
        result += [leaf(kind, n) for n in names]
    return tuple(result)
```

```python
import numpy as np
import jax
import jax.numpy as jnp
from jax import lax
from jax.experimental import pallas as pl
from jax.experimental.pallas import tpu as pltpu

F32 = jnp.float32
BF16 = jnp.bfloat16
MESH_IDS = pl.DeviceIdType.MESH

LANES = 128
HEAD_DIM = 64
N_FOX_HEADS = 8
N_DIL_HEADS = 8
N_MEM_HEADS = 4
FOX_WIDTH = N_FOX_HEADS * HEAD_DIM
DIL_WIDTH = N_DIL_HEADS * HEAD_DIM
DILATIONS = (1, 4, 16)
BAND = 128
BAND_CHUNK_MAX = 8 * BAND
N_BUCKETS = 32
MAX_DISTANCE = 2048
QK_SCALE = HEAD_DIM ** -0.5
RMS_EPS = 1e-6
NEG = -1e30
VMEM_LIMIT = 56 << 20

ADAM_LR = 0.001
ADAM_B1 = 0.9
ADAM_B2 = 0.999
ADAM_EPS = 1e-08
ADAM_WD = 0.01
ADAM_STEP = 10

N_CHIPS = 4
PACK_COLS = 1024


def _params(*sem):
    return pltpu.CompilerParams(dimension_semantics=sem, vmem_limit_bytes=VMEM_LIMIT)


def _fit(n, cap):
    if n <= cap:
        return n
    t = (cap // LANES) * LANES
    while t >= LANES:
        if n % t == 0:
            return t
        t -= LANES
    raise ValueError(f"no lane-aligned tile for {n} under {cap}")


def _dot(a, b, dims):
    return lax.dot_general(a, b, (dims, ((), ())), preferred_element_type=F32)


_NN = ((1,), (0,))
_NT = ((1,), (1,))
_TN = ((0,), (0,))


def _mm(a, b, mode, out_dtype, name, tm=1024, tn=1024, tk=1024):
    if mode == "nn":
        (M, K), N = a.shape, b.shape[1]
    elif mode == "nt":
        (M, K), N = a.shape, b.shape[0]
    else:
        (K, M), N = a.shape, b.shape[1]
    tm, tn, tk = _fit(M, tm), _fit(N, tn), _fit(K, tk)
    nk = K // tk
    if mode == "tn":
        a_spec = pl.BlockSpec((tk, tm), lambda i, j, k: (k, i))
    else:
        a_spec = pl.BlockSpec((tm, tk), lambda i, j, k: (i, k))
    if mode == "nt":
        b_spec = pl.BlockSpec((tn, tk), lambda i, j, k: (j, k))
    else:
        b_spec = pl.BlockSpec((tk, tn), lambda i, j, k: (k, j))
    dims = {"nn": _NN, "nt": _NT, "tn": _TN}[mode]

    def body(a_ref, b_ref, o_ref, *acc):
        prod = _dot(a_ref[...].astype(BF16), b_ref[...].astype(BF16), dims)
        if nk == 1:
            o_ref[...] = prod.astype(o_ref.dtype)
            return
        acc_ref, k = acc[0], pl.program_id(2)

        @pl.when(k == 0)
        def _():
            acc_ref[...] = prod

        @pl.when(k > 0)
        def _():
            acc_ref[...] += prod

        @pl.when(k == nk - 1)
        def _():
            o_ref[...] = acc_ref[...].astype(o_ref.dtype)

    return pl.pallas_call(
        body, name=name, grid=(M // tm, N // tn, nk),
        in_specs=[a_spec, b_spec],
        out_specs=pl.BlockSpec((tm, tn), lambda i, j, k: (i, j)),
        out_shape=jax.ShapeDtypeStruct((M, N), out_dtype),
        scratch_shapes=[pltpu.VMEM((tm, tn), F32)] if nk > 1 else [],
        compiler_params=_params("parallel", "parallel", "arbitrary"),
    )(a, b)


def _rms_rows(x):
    return lax.rsqrt(jnp.mean(x * x, axis=-1, keepdims=True) + RMS_EPS)


def _rms_fwd(x, g, name, tr=512):
    S, D = x.shape
    tr = _fit(S, tr)

    def body(x_ref, g_ref, h_ref):
        xv = x_ref[...]
        h_ref[...] = (xv * _rms_rows(xv) * g_ref[...]).astype(BF16)

    return pl.pallas_call(
        body, name=name, grid=(S // tr,),
        in_specs=[pl.BlockSpec((tr, D), lambda i: (i, 0)), pl.BlockSpec((1, D), lambda i: (0, 0))],
        out_specs=pl.BlockSpec((tr, D), lambda i: (i, 0)),
        out_shape=jax.ShapeDtypeStruct((S, D), BF16),
        compiler_params=_params("parallel"),
    )(x, g)


def _resid_norm(xres, y, g_post, g_next, name, tr=512):
    S, D = xres.shape
    tr = _fit(S, tr)

    def body(x_ref, y_ref, gp_ref, gn_ref, xn_ref, h_ref):
        yv = y_ref[...]
        xn = x_ref[...] + yv * _rms_rows(yv) * gp_ref[...]
        xn_ref[...] = xn
        h_ref[...] = (xn * _rms_rows(xn) * gn_ref[...]).astype(BF16)

    row = pl.BlockSpec((tr, D), lambda i: (i, 0))
    vec = pl.BlockSpec((1, D), lambda i: (0, 0))
    return pl.pallas_call(
        body, name=name, grid=(S // tr,),
        in_specs=[row, row, vec, vec], out_specs=[row, row],
        out_shape=[jax.ShapeDtypeStruct((S, D), F32), jax.ShapeDtypeStruct((S, D), BF16)],
        compiler_params=_params("parallel"),
    )(xres, y, g_post, g_next)


def _final_loss(xres, y, g_post, target, name, tr=512):
    S, D = xres.shape
    tr = _fit(S, tr)

    def body(x_ref, y_ref, gp_ref, t_ref, d_ref, loss_ref):
        i = pl.program_id(0)
        yv = y_ref[...]
        err = x_ref[...] + yv * _rms_rows(yv) * gp_ref[...] - t_ref[...]
        d_ref[...] = err * (1.0 / D)

        @pl.when(i == 0)
        def _():
            loss_ref[...] = jnp.zeros_like(loss_ref)

        part = jnp.sum(jnp.sum(err * err, axis=1, keepdims=True), axis=0, keepdims=True)
        loss_ref[...] += jnp.broadcast_to(part * (0.5 / D), loss_ref.shape)

    row = pl.BlockSpec((tr, D), lambda i: (i, 0))
    vec = pl.BlockSpec((1, D), lambda i: (0, 0))
    return pl.pallas_call(
        body, name=name, grid=(S // tr,),
        in_specs=[row, row, vec, row],
        out_specs=[row, pl.BlockSpec((8, LANES), lambda i: (0, 0))],
        out_shape=[jax.ShapeDtypeStruct((S, D), F32), jax.ShapeDtypeStruct((8, LANES), F32)],
        compiler_params=_params("arbitrary"),
    )(xres, y, g_post, target)


def _rms_bwd(xin, g, dy, dres, name, want=("f32", "bf16"), tr=512):
    S, D = xin.shape
    tr = _fit(S, tr)
    has_res = dres is not None

    def body(*refs):
        refs = list(refs)
        dg_ref = refs.pop()
        dxb_ref = refs.pop() if "bf16" in want else None
        dx_ref = refs.pop() if "f32" in want else None
        dr_ref = refs.pop() if has_res else None
        x_ref, g_ref, dy_ref = refs
        i = pl.program_id(0)
        xv = x_ref[...]
        dyv = dy_ref[...].astype(F32)
        xhat = xv * _rms_rows(xv)
        dxhat = dyv * g_ref[...]
        r = _rms_rows(xv)
        dx = r * (dxhat - xhat * jnp.mean(dxhat * xhat, axis=-1, keepdims=True))
        if has_res:
            dx = dx + dr_ref[...]
        if dx_ref is not None:
            dx_ref[...] = dx
        if dxb_ref is not None:
            dxb_ref[...] = dx.astype(BF16)

        @pl.when(i == 0)
        def _():
            dg_ref[...] = jnp.zeros_like(dg_ref)

        dg_ref[...] += jnp.broadcast_to(jnp.sum(dyv * xhat, axis=0, keepdims=True), dg_ref.shape)

    row = pl.BlockSpec((tr, D), lambda i: (i, 0))
    vec = pl.BlockSpec((1, D), lambda i: (0, 0))
    acc = pl.BlockSpec((8, D), lambda i: (0, 0))
    ins = [xin, g, dy] + ([dres] if has_res else [])
    dtypes = [dt for key, dt in (("f32", F32), ("bf16", BF16)) if key in want]
    outs = pl.pallas_call(
        body, name=name, grid=(S // tr,),
        in_specs=[row, vec, row] + ([row] if has_res else []),
        out_specs=[row] * len(dtypes) + [acc],
        out_shape=[jax.ShapeDtypeStruct((S, D), dt) for dt in dtypes] + [jax.ShapeDtypeStruct((8, D), F32)],
        compiler_params=_params("arbitrary"),
    )(*ins)
    by_key = dict(zip([key for key in ("f32", "bf16") if key in want], outs[:-1]))
    return by_key.get("f32"), by_key.get("bf16"), outs[-1]


def _tri(n, upper):
    r = lax.broadcasted_iota(jnp.int32, (n, n), 0)
    c = lax.broadcasted_iota(jnp.int32, (n, n), 1)
    return jnp.where((r <= c) if upper else (r >= c), 1.0, 0.0).astype(F32)


def _forget_fwd(fg_t, b_col, name, ts=512):
    H, S = fg_t.shape
    ts = _fit(S, ts)

    def body(f_ref, b_ref, c_ref, carry_ref):
        i = pl.program_id(0)

        @pl.when(i == 0)
        def _():
            carry_ref[...] = jnp.zeros_like(carry_ref)

        z = f_ref[...] + b_ref[...]
        logf = jnp.minimum(z, 0.0) - jnp.log(1.0 + jnp.exp(-jnp.abs(z)))
        run = lax.dot_general(logf, _tri(ts, True), (_NN, ((), ())), precision=lax.Precision.HIGHEST,
                              preferred_element_type=F32) + carry_ref[:, 0:1]
        c_ref[...] = run
        carry_ref[...] = jnp.broadcast_to(
            carry_ref[:, 0:1] + jnp.sum(logf, axis=1, keepdims=True), carry_ref.shape)

    return pl.pallas_call(
        body, name=name, grid=(S // ts,),
        in_specs=[pl.BlockSpec((H, ts), lambda i: (0, i)), pl.BlockSpec((H, 1), lambda i: (0, 0))],
        out_specs=pl.BlockSpec((H, ts), lambda i: (0, i)),
        out_shape=jax.ShapeDtypeStruct((H, S), F32),
        scratch_shapes=[pltpu.VMEM((H, LANES), F32)],
        compiler_params=_params("arbitrary"),
    )(fg_t, b_col)


def _forget_bwd(fg_t, b_col, dc_plus, dc_minus, name, ts=512):
    H, S = fg_t.shape
    ts = _fit(S, ts)
    nb = S // ts

    def body(f_ref, b_ref, dcp_ref, dcm_ref, df_ref, db_ref, carry_ref):
        i = pl.program_id(0)

        @pl.when(i == 0)
        def _():
            carry_ref[...] = jnp.zeros_like(carry_ref)
            db_ref[...] = jnp.zeros_like(db_ref)

        dc = dcp_ref[...] - dcm_ref[...]
        suffix = lax.dot_general(dc, _tri(ts, False), (_NN, ((), ())), precision=lax.Precision.HIGHEST,
                                 preferred_element_type=F32) + carry_ref[:, 0:1]
        z = f_ref[...] + b_ref[...]
        sig_neg = 1.0 / (1.0 + jnp.exp(z))
        df = suffix * sig_neg
        df_ref[...] = df
        carry_ref[...] = jnp.broadcast_to(
            carry_ref[:, 0:1] + jnp.sum(dc, axis=1, keepdims=True), carry_ref.shape)
        db_ref[...] += jnp.broadcast_to(jnp.sum(df, axis=1, keepdims=True), db_ref.shape)

    rev = pl.BlockSpec((H, ts), lambda i: (0, nb - 1 - i))
    return pl.pallas_call(
        body, name=name, grid=(nb,),
        in_specs=[rev, pl.BlockSpec((H, 1), lambda i: (0, 0)), rev, rev],
        out_specs=[rev, pl.BlockSpec((H, LANES), lambda i: (0, 0))],
        out_shape=[jax.ShapeDtypeStruct((H, S), F32), jax.ShapeDtypeStruct((H, LANES), F32)],
        scratch_shapes=[pltpu.VMEM((H, LANES), F32)],
        compiler_params=_params("arbitrary"),
    )(fg_t, b_col, dc_plus, dc_minus)


ONES_ROWS = 16
EXTRA = 3


def _split3(x):
    hi = lax.reduce_precision(x, 8, 7)
    mid = lax.reduce_precision(x - hi, 8, 7)
    lo = lax.reduce_precision(x - hi - mid, 8, 7)
    return hi.astype(BF16), mid.astype(BF16), lo.astype(BF16)


def _lanes_operand(t, extras):
    block = jnp.pad(jnp.stack(extras, axis=-1), ((0, 0), (0, 0), (0, LANES - HEAD_DIM - len(extras))))
    return jnp.concatenate([t, block], axis=-1)


def _rows_operand(t, extras):
    block = jnp.pad(jnp.stack(extras, axis=1), ((0, 0), (0, LANES - HEAD_DIM - len(extras)), (0, 0)))
    return jnp.concatenate([t, block], axis=1)


def _with_ones(t):
    return jnp.concatenate([t, jnp.ones((t.shape[0], ONES_ROWS, t.shape[2]), t.dtype)], axis=1)


def _fox_fwd(qa, ka, vt, name, tq=512, tk=1024):
    H, _, S = qa.shape
    Dh = HEAD_DIM
    tk = _fit(S, tk)
    tq = _fit(tk, tq)
    ratio = tk // tq

    def body(qa_ref, ka_ref, vt_ref, o_ref, lse_ref, m_ref, acc_ref, sa_ref, sb_ref, ta_ref, tb_ref):
        i = pl.program_id(1)
        qv = qa_ref[...] * QK_SCALE
        m_ref[...] = jnp.full_like(m_ref, NEG)
        acc_ref[...] = jnp.zeros_like(acc_ref)
        n = i // ratio
        q_off = (i - n * ratio) * tq

        def scores(j, s_ref, t_ref, diagonal):
            off = pl.multiple_of(j * tk, LANES)
            s = _dot(ka_ref[pl.ds(off, tk), :], qv, _NN)
            if diagonal:
                key = lax.broadcasted_iota(jnp.int32, (tk, tq), 0)
                qry = lax.broadcasted_iota(jnp.int32, (tk, tq), 1) + q_off
                s = jnp.where(key <= qry, s, NEG)
            s_ref[...] = s
            t_ref[...] = jnp.max(s, axis=0, keepdims=True)

        def absorb(j, s_ref, t_ref):
            off = pl.multiple_of(j * tk, LANES)
            m_old = m_ref[...]
            m_new = jnp.maximum(m_old, t_ref[...])
            p = jnp.exp(s_ref[...] - m_new)
            alpha = jnp.exp(m_old - m_new)
            acc_ref[...] = alpha * acc_ref[...] + _dot(vt_ref[:, pl.ds(off, tk)], p.astype(BF16), _NN)
            m_ref[...] = m_new

        scores(n, sa_ref, ta_ref, True)

        def loop_body(jj, carry):
            scores(2 * jj, sb_ref, tb_ref, False)
            absorb(jnp.where(jj == 0, n, 2 * jj - 1), sa_ref, ta_ref)
            scores(2 * jj + 1, sa_ref, ta_ref, False)
            absorb(2 * jj, sb_ref, tb_ref)
            return carry

        pairs = n // 2
        lax.fori_loop(0, pairs, loop_body, 0)
        held = jnp.where(pairs == 0, n, 2 * pairs - 1)

        @pl.when(n % 2 == 1)
        def _():
            scores(n - 1, sb_ref, tb_ref, False)
            absorb(held, sa_ref, ta_ref)
            absorb(n - 1, sb_ref, tb_ref)

        @pl.when(n % 2 == 0)
        def _():
            absorb(held, sa_ref, ta_ref)

        l = acc_ref[Dh:Dh + 1, :]
        o_ref[...] = acc_ref[0:Dh, :] / l
        lse_ref[...] = m_ref[...] + jnp.log(l)

    return pl.pallas_call(
        body, name=name, grid=(H, S // tq),
        in_specs=[pl.BlockSpec((None, LANES, tq), lambda h, i: (h, 0, i)),
                  pl.BlockSpec((None, S, LANES), lambda h, i: (h, 0, 0)),
                  pl.BlockSpec((None, Dh + ONES_ROWS, S), lambda h, i: (h, 0, 0))],
        out_specs=[pl.BlockSpec((None, Dh, tq), lambda h, i: (h, 0, i)),
                   pl.BlockSpec((None, 1, tq), lambda h, i: (h, 0, i))],
        out_shape=[jax.ShapeDtypeStruct((H, Dh, S), F32), jax.ShapeDtypeStruct((H, 1, S), F32)],
        scratch_shapes=[pltpu.VMEM((1, tq), F32), pltpu.VMEM((Dh + ONES_ROWS, tq), F32),
                        pltpu.VMEM((tk, tq), F32), pltpu.VMEM((tk, tq), F32),
                        pltpu.VMEM((1, tq), F32), pltpu.VMEM((1, tq), F32)],
        compiler_params=_params("parallel", "arbitrary"),
    )(qa, ka, vt)


def _fox_bwd(qa, ka, kta, va, doa, name, tq=1024, tk=512):
    H, _, S = qa.shape
    Dh, Da = HEAD_DIM, HEAD_DIM + ONES_ROWS
    tq = _fit(S, tq)
    tk = _fit(tq, tk)
    ratio = tq // tk
    nq = S // tq
    nk = S // tk

    def body(ka_ref, kta_ref, va_ref, qa_ref, doa_ref, dqt_ref, dkt_ref, dvt_ref, dka_ref, dva_ref):
        j = pl.program_id(1)

        @pl.when(j == 0)
        def _():
            dqt_ref[...] = jnp.zeros_like(dqt_ref)

        kv = ka_ref[...]
        ktv = kta_ref[0:Da, :]
        vv = va_ref[...]
        dka_ref[...] = jnp.zeros_like(dka_ref)
        dva_ref[...] = jnp.zeros_like(dva_ref)
        i_diag = j // ratio
        k_off = (j - i_diag * ratio) * tk

        def step(i, diagonal):
            off = pl.multiple_of(i * tq, LANES)
            qv = qa_ref[:, pl.ds(off, tq)] * QK_SCALE
            dov = doa_ref[:, pl.ds(off, tq)]
            e = _dot(kv, qv, _NN)
            if diagonal:
                key = lax.broadcasted_iota(jnp.int32, (tk, tq), 0) + k_off
                qry = lax.broadcasted_iota(jnp.int32, (tk, tq), 1)
                e = jnp.where(key <= qry, e, NEG)
            p_t = jnp.exp(e)
            dva_ref[...] += _dot(dov[0:Dh, :], p_t.astype(BF16), _NT)
            ds_b = (p_t * _dot(vv, dov, _NN)).astype(BF16)
            dka_ref[...] += _dot(qv[0:Da, :], ds_b, _NT)
            dqt_ref[:, pl.ds(off, tq)] += _dot(ktv, ds_b, _NN)

        step(i_diag, True)

        def loop_body(i, carry):
            step(i, False)
            return carry

        lax.fori_loop(i_diag + 1, nq, loop_body, 0)
        dkt_ref[...] = dka_ref[...]
        dvt_ref[...] = dva_ref[...]

        @pl.when(j == nk - 1)
        def _():
            dqt_ref[0:Dh, :] = dqt_ref[0:Dh, :] * QK_SCALE

    lanes_tile = pl.BlockSpec((None, tk, LANES), lambda h, j: (h, j, 0))
    rows_tile = pl.BlockSpec((None, LANES, tk), lambda h, j: (h, 0, j))
    rows_full = pl.BlockSpec((None, LANES, S), lambda h, j: (h, 0, 0))
    return pl.pallas_call(
        body, name=name, grid=(H, nk),
        in_specs=[lanes_tile, rows_tile, lanes_tile, rows_full, rows_full],
        out_specs=[pl.BlockSpec((None, Da, S), lambda h, j: (h, 0, 0)),
                   pl.BlockSpec((None, Da, tk), lambda h, j: (h, 0, j)),
                   pl.BlockSpec((None, Dh, tk), lambda h, j: (h, 0, j))],
        out_shape=[jax.ShapeDtypeStruct((H, Da, S), F32), jax.ShapeDtypeStruct((H, Da, S), F32),
                   jax.ShapeDtypeStruct((H, Dh, S), F32)],
        scratch_shapes=[pltpu.VMEM((Da, tk), F32), pltpu.VMEM((Dh, tk), F32)],
        compiler_params=_params("parallel", "arbitrary"),
    )(ka, kta, va, qa, doa)


DIL_Q_BLOCK = 3 * FOX_WIDTH // LANES
HEAD_PAIRS = N_DIL_HEADS // 2
PAIR_BLOCKS = DIL_WIDTH // LANES


def _band_geometry(S, d):
    L = S // d
    chunk = min(BAND_CHUNK_MAX, L)
    assert L % chunk == 0 and chunk % BAND == 0
    return L, chunk, chunk // BAND, L // chunk


def _band_in_specs(S, d, base):
    L, chunk, nb, _ = _band_geometry(S, d)

    def col(kind):
        return lambda hp, r, i: (r, i, base + kind * PAIR_BLOCKS + hp)

    def col_prev(kind):
        return lambda hp, r, i: (r, jnp.maximum(i * nb - 1, 0), base + kind * PAIR_BLOCKS + hp)

    main = [pl.BlockSpec((None, chunk, LANES), col(kind)) for kind in range(3)]
    prev = [pl.BlockSpec((None, BAND, LANES), col_prev(kind)) for kind in range(3)]
    bias = pl.BlockSpec((None, 2 * BAND, 2 * BAND), lambda hp, r, i: (hp, 0, 0))
    stat = pl.BlockSpec((None, 2, chunk), lambda hp, r, i: (hp, 0, r * (L // chunk) + i))
    tok = pl.BlockSpec((None, chunk, LANES), lambda hp, r, i: (r, i, hp))
    return main, prev, bias, stat, tok


def _to_residues(x, col_block, width, d, name, tr=512):
    S = x.shape[0]
    tr = _fit(S, tr)

    def body(x_ref, o_ref, tmp_ref):
        for j in range(width // LANES):
            cols = slice(j * LANES, (j + 1) * LANES)
            tmp_ref[j] = x_ref[:, cols].astype(F32)
            for r in range(d):
                o_ref[r, :, cols] = tmp_ref[j, pl.ds(r, tr // d, stride=d), :].astype(o_ref.dtype)

    return pl.pallas_call(
        body, name=name, grid=(S // tr,),
        in_specs=[pl.BlockSpec((tr, width), lambda i: (i, col_block))],
        out_specs=pl.BlockSpec((d, tr // d, width), lambda i: (0, i, 0)),
        out_shape=jax.ShapeDtypeStruct((d, S // d, width), x.dtype),
        scratch_shapes=[pltpu.VMEM((width // LANES, tr, LANES), F32)],
        compiler_params=_params("parallel"),
    )(x)


def _token_rows(ref, cols, tmp_ref):
    if len(ref.shape) == 2:
        return ref[:, cols].astype(F32)
    d, rows = ref.shape[0], ref.shape[1]
    for r in range(d):
        tmp_ref[pl.ds(r, rows, stride=d), :] = ref[r, :, cols].astype(F32)
    return tmp_ref[...]


def _row_spec(t, tr):
    if t.ndim == 2:
        return pl.BlockSpec((tr, t.shape[1]), lambda i: (i, 0))
    d = t.shape[0]
    return pl.BlockSpec((d, tr // d, t.shape[2]), lambda i: (0, i, 0))


def _head_lanes(a):
    return lax.broadcasted_iota(jnp.int32, (1, LANES), 1) // HEAD_DIM == a


def _one_head(x, a):
    return jnp.where(_head_lanes(a), x, jnp.zeros_like(x))


def _head_stack(x):
    return jnp.concatenate([_one_head(x, 0), _one_head(x, 1)], axis=0)


def _pair_rows(ref, rows):
    return jnp.concatenate([ref[0:1, rows], ref[1:2, rows]], axis=1)


def _band_scores_t(kb, q_stack, bias_t, first):
    s = _dot(kb, q_stack, _NT) + bias_t
    if first is not None:
        key = lax.broadcasted_iota(jnp.int32, s.shape, 0)
        s = jnp.where(jnp.logical_and(first, key < BAND), NEG, s)
    return s


def _pair_select(stacked):
    return jnp.where(_head_lanes(0), stacked[0:BAND, :], stacked[BAND:, :])


def _dil_lse(qkv_v, base, bias_t, name):
    d, L = qkv_v.shape[:2]
    S = L * d
    _, chunk, nb, nchunks = _band_geometry(S, d)
    main, prev, bias, stat, _ = _band_in_specs(S, d, base)

    def body(q_ref, k_ref, kp_ref, b_ref, lse_ref, kext_ref):
        first = pl.program_id(2) == 0
        kext_ref[0:BAND, :] = kp_ref[...]
        kext_ref[BAND:, :] = k_ref[...]
        for b in range(nb):
            rows, ext = slice(b * BAND, (b + 1) * BAND), slice(b * BAND, (b + 2) * BAND)
            s = _band_scores_t(kext_ref[ext, :], _head_stack(q_ref[rows, :] * QK_SCALE), b_ref[...],
                               first if b == 0 else None)
            m = jnp.max(s, axis=0, keepdims=True)
            lse = m + jnp.log(jnp.sum(jnp.exp(s - m), axis=0, keepdims=True))
            lse_ref[0:1, rows] = lse[:, 0:BAND]
            lse_ref[1:2, rows] = lse[:, BAND:]

    return pl.pallas_call(
        body, name=name, grid=(HEAD_PAIRS, d, nchunks),
        in_specs=[main[0], main[1], prev[1], bias], out_specs=stat,
        out_shape=jax.ShapeDtypeStruct((HEAD_PAIRS, 2, S), F32),
        scratch_shapes=[pltpu.VMEM((chunk + BAND, LANES), BF16)],
        compiler_params=_params("parallel", "parallel", "parallel"),
    )(qkv_v, qkv_v, qkv_v, bias_t)


def _dil_out(qkv_v, base, bias_t, lse_joint, name):
    d, L = qkv_v.shape[:2]
    S = L * d
    _, chunk, nb, nchunks = _band_geometry(S, d)
    main, prev, bias, stat, tok = _band_in_specs(S, d, base)

    def body(q_ref, k_ref, kp_ref, v_ref, vp_ref, b_ref, lse_ref, o_ref, kext_ref, vext_ref):
        first = pl.program_id(2) == 0
        kext_ref[0:BAND, :] = kp_ref[...]
        kext_ref[BAND:, :] = k_ref[...]
        vext_ref[0:BAND, :] = vp_ref[...]
        vext_ref[BAND:, :] = v_ref[...]
        for b in range(nb):
            rows, ext = slice(b * BAND, (b + 1) * BAND), slice(b * BAND, (b + 2) * BAND)
            s = _band_scores_t(kext_ref[ext, :], _head_stack(q_ref[rows, :] * QK_SCALE), b_ref[...],
                               first if b == 0 else None)
            p_t = jnp.exp(s - _pair_rows(lse_ref, rows))
            o_ref[rows, :] = _pair_select(_dot(p_t.astype(BF16), vext_ref[ext, :], _TN)).astype(BF16)

    return pl.pallas_call(
        body, name=name, grid=(HEAD_PAIRS, d, nchunks),
        in_specs=[main[0], main[1], prev[1], main[2], prev[2], bias, stat], out_specs=tok,
        out_shape=jax.ShapeDtypeStruct((d, L, DIL_WIDTH), BF16),
        scratch_shapes=[pltpu.VMEM((chunk + BAND, LANES), BF16), pltpu.VMEM((chunk + BAND, LANES), BF16)],
        compiler_params=_params("parallel", "parallel", "parallel"),
    )(qkv_v, qkv_v, qkv_v, qkv_v, qkv_v, bias_t, lse_joint)


def _dil_bwd(qkv_v, base, do_v, bias_t, lse_joint, delta, name):
    d, L = qkv_v.shape[:2]
    S = L * d
    _, chunk, nb, nchunks = _band_geometry(S, d)
    main, prev, bias, stat, tok = _band_in_specs(S, d, base)
    nblocks = L // BAND

    def nxt_row(i):
        return jnp.minimum((i + 1) * nb, nblocks - 1)

    q_next = pl.BlockSpec((None, BAND, LANES), lambda hp, r, i: (r, nxt_row(i), base + hp))
    do_next = pl.BlockSpec((None, BAND, LANES), lambda hp, r, i: (r, nxt_row(i), hp))
    stat_next = pl.BlockSpec((None, 2, BAND), lambda hp, r, i: (hp, 0, r * nblocks + nxt_row(i)))

    def body(q_ref, k_ref, kp_ref, v_ref, vp_ref, do_ref, b_ref, lse_ref, dl_ref,
             qn_ref, don_ref, lsen_ref, dln_ref,
             dq_ref, dk_ref, dv_ref, db_ref, kext_ref, vext_ref, dkext_ref, dvext_ref):
        r, i = pl.program_id(1), pl.program_id(2)
        first = i == 0
        has_next = i + 1 < nchunks
        tail = slice(BAND + chunk, 2 * BAND + chunk)
        kext_ref[0:BAND, :] = kp_ref[...]
        kext_ref[BAND:BAND + chunk, :] = k_ref[...]
        kext_ref[tail, :] = jnp.zeros((BAND, LANES), BF16)
        vext_ref[0:BAND, :] = vp_ref[...]
        vext_ref[BAND:BAND + chunk, :] = v_ref[...]
        vext_ref[tail, :] = jnp.zeros((BAND, LANES), BF16)
        dkext_ref[...] = jnp.zeros_like(dkext_ref)
        dvext_ref[...] = jnp.zeros_like(dvext_ref)

        @pl.when(jnp.logical_and(r == 0, i == 0))
        def _():
            db_ref[...] = jnp.zeros_like(db_ref)

        def block(q2, do2, lse_row, dl_row, ext, mask_rows):
            q_stack, do_stack = _head_stack(q2), _head_stack(do2)
            s = _dot(kext_ref[ext, :], q_stack, _NT) + b_ref[...]
            if mask_rows is not None:
                s = jnp.where(mask_rows, NEG, s)
            p_t = jnp.exp(s - lse_row)
            ds_t = p_t * (_dot(vext_ref[ext, :], do_stack, _NT) - dl_row)
            ds_b = ds_t.astype(BF16)
            dkext_ref[ext, :] += _dot(ds_b, q_stack, _NN)
            dvext_ref[ext, :] += _dot(p_t.astype(BF16), do_stack, _NN)
            return ds_t, ds_b

        key = lax.broadcasted_iota(jnp.int32, (2 * BAND, 2 * BAND), 0)
        all_lanes = slice(0, BAND)
        for b in range(nb):
            rows, ext = slice(b * BAND, (b + 1) * BAND), slice(b * BAND, (b + 2) * BAND)
            mask = jnp.logical_and(first, key < BAND) if b == 0 else None
            ds_t, ds_b = block(q_ref[rows, :] * QK_SCALE, do_ref[rows, :], _pair_rows(lse_ref, rows),
                               _pair_rows(dl_ref, rows), ext, mask)
            dq_ref[rows, :] = _pair_select(_dot(ds_b, kext_ref[ext, :], _TN)) * QK_SCALE
            db_ref[...] += ds_t
        block(qn_ref[...] * QK_SCALE, don_ref[...], _pair_rows(lsen_ref, all_lanes), _pair_rows(dln_ref, all_lanes),
              slice(chunk, chunk + 2 * BAND), jnp.logical_or(jnp.logical_not(has_next), key >= BAND))
        dk_ref[...] = dkext_ref[BAND:BAND + chunk, :]
        dv_ref[...] = dvext_ref[BAND:BAND + chunk, :]

    ext_rows = chunk + 2 * BAND
    return pl.pallas_call(
        body, name=name, grid=(HEAD_PAIRS, d, nchunks),
        in_specs=[main[0], main[1], prev[1], main[2], prev[2], tok, bias, stat, stat,
                  q_next, do_next, stat_next, stat_next],
        out_specs=[tok, tok, tok, bias],
        out_shape=[jax.ShapeDtypeStruct((d, L, DIL_WIDTH), F32)] * 3
                  + [jax.ShapeDtypeStruct((HEAD_PAIRS, 2 * BAND, 2 * BAND), F32)],
        scratch_shapes=[pltpu.VMEM((ext_rows, LANES), BF16), pltpu.VMEM((ext_rows, LANES), BF16),
                        pltpu.VMEM((ext_rows, LANES), F32), pltpu.VMEM((ext_rows, LANES), F32)],
        compiler_params=_params("arbitrary", "arbitrary", "arbitrary"),
    )(qkv_v, qkv_v, qkv_v, qkv_v, qkv_v, do_v, bias_t, lse_joint, delta, qkv_v, do_v, lse_joint, delta)


def _lse_join(lse3, name):
    P, H, S = lse3.shape

    def body(l_ref, o_ref):
        a, b, c = l_ref[0], l_ref[1], l_ref[2]
        m = jnp.maximum(jnp.maximum(a, b), c)
        o_ref[...] = m + jnp.log(jnp.exp(a - m) + jnp.exp(b - m) + jnp.exp(c - m))

    return pl.pallas_call(body, name=name, out_shape=jax.ShapeDtypeStruct((H, S), F32))(lse3)


def _bucket_reduce(dbias_t, bucket_map_t, name):
    P, H = dbias_t.shape[:2]

    def body(db_ref, bk_ref, o_ref):
        p, h = pl.program_id(0), pl.program_id(1)

        @pl.when(jnp.logical_and(p == 0, h == 0))
        def _():
            o_ref[...] = jnp.zeros_like(o_ref)

        db, bk = db_ref[...], bk_ref[...]
        row = lax.broadcasted_iota(jnp.int32, (N_BUCKETS, LANES), 0)
        lane = lax.broadcasted_iota(jnp.int32, (N_BUCKETS, LANES), 1)

        def one(b, acc):
            val = jnp.sum(jnp.sum(jnp.where(bk == b, db, 0.0), axis=1, keepdims=True), axis=0, keepdims=True)
            return acc + jnp.where(jnp.logical_and(row == b, lane == h), val, 0.0)

        o_ref[...] += lax.fori_loop(0, N_BUCKETS, one, jnp.zeros((N_BUCKETS, LANES), F32))

    return pl.pallas_call(
        body, name=name, grid=(P, H),
        in_specs=[pl.BlockSpec((None, None, 2 * BAND, BAND), lambda p, h: (p, h, 0, 0)),
                  pl.BlockSpec((None, 2 * BAND, BAND), lambda p, h: (p, 0, 0))],
        out_specs=pl.BlockSpec((N_BUCKETS, LANES), lambda p, h: (0, 0)),
        out_shape=jax.ShapeDtypeStruct((N_BUCKETS, LANES), F32),
        compiler_params=_params("arbitrary", "arbitrary"),
    )(dbias_t, bucket_map_t)


def _mem_fwd(q, kv, name, tq=1024):
    S, W = q.shape
    N = kv.shape[0]
    pairs = W // LANES
    tq = _fit(S, tq)

    def body(q_ref, k_ref, v_ref, o_ref, lse_ref):
        for a in range(2):
            lanes = slice(a * HEAD_DIM, (a + 1) * HEAD_DIM)
            s = _dot(k_ref[:, lanes], q_ref[:, lanes] * QK_SCALE, _NT)
            m = jnp.max(s, axis=0, keepdims=True)
            e = jnp.exp(s - m)
            l = jnp.sum(e, axis=0, keepdims=True)
            o_ref[:, lanes] = _dot((e / l).astype(BF16), v_ref[:, lanes], _TN).astype(BF16)
            lse_ref[a:a + 1, :] = m + jnp.log(l)

    return pl.pallas_call(
        body, name=name, grid=(pairs, S // tq),
        in_specs=[pl.BlockSpec((tq, LANES), lambda hp, i: (i, hp)),
                  pl.BlockSpec((N, LANES), lambda hp, i: (0, hp)),
                  pl.BlockSpec((N, LANES), lambda hp, i: (0, pairs + hp))],
        out_specs=[pl.BlockSpec((tq, LANES), lambda hp, i: (i, hp)),
                   pl.BlockSpec((None, 2, tq), lambda hp, i: (hp, 0, i))],
        out_shape=[jax.ShapeDtypeStruct((S, W), BF16), jax.ShapeDtypeStruct((pairs, 2, S), F32)],
        compiler_params=_params("parallel", "parallel"),
    )(q, kv, kv)


def _mem_bwd(q, kv, do, lse, delta, name, tq=1024):
    S, W = q.shape
    N = kv.shape[0]
    pairs = W // LANES
    tq = _fit(S, tq)

    def body(q_ref, k_ref, v_ref, do_ref, lse_ref, dl_ref, dq_ref, dk_ref, dv_ref):
        i = pl.program_id(1)

        @pl.when(i == 0)
        def _():
            dk_ref[...] = jnp.zeros_like(dk_ref)
            dv_ref[...] = jnp.zeros_like(dv_ref)

        for a in range(2):
            lanes = slice(a * HEAD_DIM, (a + 1) * HEAD_DIM)
            qv, dov = q_ref[:, lanes] * QK_SCALE, do_ref[:, lanes]
            kv_, vv = k_ref[:, lanes], v_ref[:, lanes]
            p_t = jnp.exp(_dot(kv_, qv, _NT) - lse_ref[a:a + 1, :])
            ds_t = p_t * (_dot(vv, dov, _NT) - dl_ref[a:a + 1, :])
            ds_b = ds_t.astype(BF16)
            dq_ref[:, lanes] = (_dot(ds_b, kv_, _TN) * QK_SCALE).astype(BF16)
            dk_ref[:, lanes] += _dot(ds_b, qv, _NN)
            dv_ref[:, lanes] += _dot(p_t.astype(BF16), dov, _NN)

    qs = pl.BlockSpec((tq, LANES), lambda hp, i: (i, hp))
    stat = pl.BlockSpec((None, 2, tq), lambda hp, i: (hp, 0, i))
    acc = pl.BlockSpec((N, LANES), lambda hp, i: (0, hp))
    return pl.pallas_call(
        body, name=name, grid=(pairs, S // tq),
        in_specs=[qs, acc, pl.BlockSpec((N, LANES), lambda hp, i: (0, pairs + hp)), qs, stat, stat],
        out_specs=[qs, acc, acc],
        out_shape=[jax.ShapeDtypeStruct((S, W), BF16), jax.ShapeDtypeStruct((N, W), F32),
                   jax.ShapeDtypeStruct((N, W), F32)],
        compiler_params=_params("parallel", "arbitrary"),
    )(q, kv, kv, do, lse, delta)


def _head_rowdot(a, bs, name, tr=512):
    S, W = a.shape
    tr = _fit(S, tr)

    def body(*refs):
        a_ref, b_refs, o_ref, tmp_ref = refs[0], refs[1:-2], refs[-2], refs[-1]
        col = lax.broadcasted_iota(jnp.int32, (LANES, LANES), 0)
        lane = lax.broadcasted_iota(jnp.int32, (LANES, LANES), 1)
        acc = jnp.zeros((tr, LANES), F32)
        for j in range(W // LANES):
            cols = slice(j * LANES, (j + 1) * LANES)
            tot = _token_rows(b_refs[0], cols, tmp_ref)
            for r in b_refs[1:]:
                tot = tot + _token_rows(r, cols, tmp_ref)
            sel = jnp.where(col // HEAD_DIM + j * (LANES // HEAD_DIM) == lane, 1.0, 0.0).astype(F32)
            acc = acc + lax.dot_general(a_ref[:, cols].astype(F32) * tot, sel, (_NN, ((), ())),
                                        precision=lax.Precision.HIGHEST, preferred_element_type=F32)
        o_ref[...] = acc

    return pl.pallas_call(
        body, name=name, grid=(S // tr,), in_specs=[_row_spec(t, tr) for t in [a] + list(bs)],
        out_specs=pl.BlockSpec((tr, LANES), lambda i: (i, 0)),
        out_shape=jax.ShapeDtypeStruct((S, LANES), F32),
        scratch_shapes=[pltpu.VMEM((tr, LANES), F32)],
        compiler_params=_params("parallel"),
    )(a, *bs)


def _sum_cast_cols(groups, out_dtype, name, tail=None, tr=256):
    first = groups[0][0]
    S, W = (first.shape if first.ndim == 2 else (first.shape[0] * first.shape[1], first.shape[2]))
    tr = _fit(S, tr)
    flat = [t for g in groups for t in g] + ([tail] if tail is not None else [])
    tail_w = 0 if tail is None else tail.shape[1]

    def body(*refs):
        o_ref, tmp_ref = refs[-2], refs[-1]
        if tail is not None:
            o_ref[:, W * len(groups):] = refs[-3][...].astype(out_dtype)
        k = 0
        for gi, g in enumerate(groups):
            for j in range(W // LANES):
                cols = slice(j * LANES, (j + 1) * LANES)
                acc = _token_rows(refs[k], cols, tmp_ref)
                for r in refs[k + 1:k + len(g)]:
                    acc = acc + _token_rows(r, cols, tmp_ref)
                o_ref[:, gi * W + j * LANES:gi * W + (j + 1) * LANES] = acc.astype(out_dtype)
            k += len(g)

    return pl.pallas_call(
        body, name=name, grid=(S // tr,), in_specs=[_row_spec(t, tr) for t in flat],
        out_specs=pl.BlockSpec((tr, W * len(groups) + tail_w), lambda i: (i, 0)),
        out_shape=jax.ShapeDtypeStruct((S, W * len(groups) + tail_w), out_dtype),
        scratch_shapes=[pltpu.VMEM((tr, LANES), F32)],
        compiler_params=_params("parallel"),
    )(*flat)


FF_TILE = 256


def _ffn_up(h, w_gu, name, tm=4096):
    S, D = h.shape
    F2 = w_gu.shape[1]
    tm = _fit(S, tm)

    def body(h_ref, w_ref, gu_ref, act_ref):
        gu = _dot(h_ref[...], w_ref[...], _NN)
        gu_ref[...] = gu.astype(BF16)
        g, u = gu[:, :FF_TILE], gu[:, FF_TILE:]
        act_ref[...] = (g * (1.0 / (1.0 + jnp.exp(-g))) * u).astype(BF16)

    return pl.pallas_call(
        body, name=name, grid=(S // tm, F2 // (2 * FF_TILE)),
        in_specs=[pl.BlockSpec((tm, D), lambda i, j: (i, 0)), pl.BlockSpec((D, 2 * FF_TILE), lambda i, j: (0, j))],
        out_specs=[pl.BlockSpec((tm, 2 * FF_TILE), lambda i, j: (i, j)),
                   pl.BlockSpec((tm, FF_TILE), lambda i, j: (i, j))],
        out_shape=[jax.ShapeDtypeStruct((S, F2), BF16), jax.ShapeDtypeStruct((S, F2 // 2), BF16)],
        compiler_params=_params("parallel", "arbitrary"),
    )(h, w_gu)


def _ffn_dact(dy, w_down, gu, name, tm=4096):
    S, D = dy.shape
    F2 = gu.shape[1]
    tm = _fit(S, tm)

    def body(dy_ref, w_ref, gu_ref, dgu_ref):
        dact = _dot(dy_ref[...], w_ref[...], _NT)
        gu_v = gu_ref[...].astype(F32)
        g, u = gu_v[:, :FF_TILE], gu_v[:, FF_TILE:]
        sig = 1.0 / (1.0 + jnp.exp(-g))
        silu = g * sig
        dgu_ref[:, :FF_TILE] = (dact * u * (sig + silu * (1.0 - sig))).astype(BF16)
        dgu_ref[:, FF_TILE:] = (dact * silu).astype(BF16)

    return pl.pallas_call(
        body, name=name, grid=(S // tm, F2 // (2 * FF_TILE)),
        in_specs=[pl.BlockSpec((tm, D), lambda i, j: (i, 0)), pl.BlockSpec((FF_TILE, D), lambda i, j: (j, 0)),
                  pl.BlockSpec((tm, 2 * FF_TILE), lambda i, j: (i, j))],
        out_specs=pl.BlockSpec((tm, 2 * FF_TILE), lambda i, j: (i, j)),
        out_shape=jax.ShapeDtypeStruct((S, F2), BF16),
        compiler_params=_params("parallel", "arbitrary"),
    )(dy, w_down, gu)


def _fit_rows(n, cap):
    if n <= cap:
        return n
    t = (cap // 8) * 8
    while t >= 8:
        if n % t == 0:
            return t
        t -= 8
    raise ValueError(f"no sublane-aligned tile for {n} under {cap}")


def _add_n(arrs, name, tr=512):
    R, C = arrs[0].shape
    tr = _fit_rows(R, tr)

    def body(*refs):
        acc = refs[0][...].astype(F32)
        for r in refs[1:-1]:
            acc = acc + r[...].astype(F32)
        refs[-1][...] = acc

    row = pl.BlockSpec((tr, C), lambda i: (i, 0))
    return pl.pallas_call(
        body, name=name, grid=(R // tr,), in_specs=[row] * len(arrs), out_specs=row,
        out_shape=jax.ShapeDtypeStruct((R, C), F32), compiler_params=_params("parallel"),
    )(*arrs)


def _adamw(w, g, m, v, name, tr=512):
    R, C = w.shape
    tr = _fit_rows(R, tr)
    c1 = 1.0 / (1.0 - ADAM_B1 ** ADAM_STEP)
    c2 = 1.0 / (1.0 - ADAM_B2 ** ADAM_STEP)

    def body(w_ref, g_ref, m_ref, v_ref, d_ref, nm_ref, nv_ref):
        gv = g_ref[...]
        nm = ADAM_B1 * m_ref[...] + (1.0 - ADAM_B1) * gv
        nv = ADAM_B2 * v_ref[...] + (1.0 - ADAM_B2) * (gv * gv)
        nm_ref[...] = nm
        nv_ref[...] = nv
        d_ref[...] = -ADAM_LR * ((nm * c1) / (jnp.sqrt(nv * c2) + ADAM_EPS) + ADAM_WD * w_ref[...])

    row = pl.BlockSpec((tr, C), lambda i: (i, 0))
    return pl.pallas_call(
        body, name=name, grid=(R // tr,), in_specs=[row] * 4, out_specs=[row] * 3,
        out_shape=[jax.ShapeDtypeStruct((R, C), F32)] * 3, compiler_params=_params("parallel"),
    )(w, g, m, v)


def _place():
    return lax.axis_index("x"), lax.axis_index("y"), lax.axis_index("c")


_ANY = pl.BlockSpec(memory_space=pl.ANY)


def _chip_all_gather(shard, name):
    R, C = shard.shape
    half = R // 2

    def body(x_ref, out_ref, send_sems, recv_sems, local_sem):
        x, y, c = _place()
        chips = [(1 - x, y), (x, 1 - y), (1 - x, 1 - y)]
        sibling = (x, y, 1 - c)
        mine = pltpu.make_async_copy(x_ref, out_ref.at[2 * x + y], local_sem)
        mine.start()

        def rows(chip, core):
            return out_ref.at[chip, pl.ds(core * half, half)]

        def copy(k, chip, core, to, src=None):
            return pltpu.make_async_remote_copy(
                src_ref=rows(chip, core) if src is None else src, dst_ref=rows(chip, core),
                send_sem=send_sems.at[k], recv_sem=recv_sems.at[k], device_id=to, device_id_type=MESH_IDS)

        me = 2 * x + y
        first = [copy(k, me, c, (cx, cy, c), src=x_ref.at[pl.ds(c * half, half)]) for k, (cx, cy) in enumerate(chips)]
        for cp in first:
            cp.start()
        passed = [copy(3 + k, 2 * cx + cy, c, sibling) for k, (cx, cy) in enumerate(chips)]
        for k, (cx, cy) in enumerate(chips):
            copy(k, 2 * cx + cy, c, (cx, cy, c)).wait_recv()
            passed[k].start()
        for k, (cx, cy) in enumerate(chips):
            copy(3 + k, 2 * cx + cy, 1 - c, sibling).wait_recv()
        for cp in first + passed:
            cp.wait_send()
        mine.wait()

    return pl.pallas_call(
        body, name=name, in_specs=[_ANY], out_specs=_ANY,
        out_shape=jax.ShapeDtypeStruct((N_CHIPS, R, C), shard.dtype),
        scratch_shapes=[pltpu.SemaphoreType.DMA((6,)), pltpu.SemaphoreType.DMA((6,)), pltpu.SemaphoreType.DMA],
    )(shard)


def _sibling_exchange(buf, name):
    def body(x_ref, out_ref, send_sem, recv_sem):
        x, y, c = _place()
        cp = pltpu.make_async_remote_copy(
            src_ref=x_ref, dst_ref=out_ref, send_sem=send_sem, recv_sem=recv_sem,
            device_id=(x, y, 1 - c), device_id_type=MESH_IDS)
        cp.start()
        cp.wait()

    return pl.pallas_call(
        body, name=name, in_specs=[_ANY], out_specs=_ANY,
        out_shape=jax.ShapeDtypeStruct(buf.shape, buf.dtype),
        scratch_shapes=[pltpu.SemaphoreType.DMA, pltpu.SemaphoreType.DMA],
    )(buf)


def _chip_scatter(parts, name):
    _, R, C = parts.shape

    def body(p_ref, out_ref, send_sems, recv_sems):
        x, y, c = _place()
        chips = [(1 - x, y), (x, 1 - y), (1 - x, 1 - y)]

        def copy(k, slab, to):
            return pltpu.make_async_remote_copy(
                src_ref=p_ref.at[slab], dst_ref=out_ref.at[k], send_sem=send_sems.at[k], recv_sem=recv_sems.at[k],
                device_id=to, device_id_type=MESH_IDS)

        sends = [copy(k, 2 * cx + cy, (cx, cy, c)) for k, (cx, cy) in enumerate(chips)]
        for cp in sends:
            cp.start()
        for cp in sends:
            cp.wait_recv()
        for cp in sends:
            cp.wait_send()

    return pl.pallas_call(
        body, name=name, in_specs=[_ANY], out_specs=_ANY,
        out_shape=jax.ShapeDtypeStruct((3, R, C), parts.dtype),
        scratch_shapes=[pltpu.SemaphoreType.DMA((3,)), pltpu.SemaphoreType.DMA((3,))],
    )(parts)


def _all_to_all_small(vec, name):
    R, C = vec.shape

    def body(v_ref, out_ref, send_sems, recv_sems, local_sem):
        x, y, c = _place()
        me = 4 * x + 2 * y + c
        mine = pltpu.make_async_copy(v_ref, out_ref.at[me], local_sem)
        mine.start()
        flips = [(dx, dy, dc) for dx in (0, 1) for dy in (0, 1) for dc in (0, 1)][1:]

        def peer(f):
            return (x ^ f[0], y ^ f[1], c ^ f[2])

        def copy(k, slot, to):
            return pltpu.make_async_remote_copy(
                src_ref=v_ref, dst_ref=out_ref.at[slot], send_sem=send_sems.at[k], recv_sem=recv_sems.at[k],
                device_id=to, device_id_type=MESH_IDS)

        sends = [copy(k, me, peer(f)) for k, f in enumerate(flips)]
        for cp in sends:
            cp.start()
        for k, f in enumerate(flips):
            px, py, pc = peer(f)
            copy(k, 4 * px + 2 * py + pc, peer(f)).wait_recv()
        for cp in sends:
            cp.wait_send()
        mine.wait()

    return pl.pallas_call(
        body, name=name, in_specs=[_ANY], out_specs=_ANY,
        out_shape=jax.ShapeDtypeStruct((8, R, C), vec.dtype),
        scratch_shapes=[pltpu.SemaphoreType.DMA((7,)), pltpu.SemaphoreType.DMA((7,)), pltpu.SemaphoreType.DMA],
    )(vec)


def _to_heads(t, n):
    S = t.shape[0]
    return t.reshape(S, n, HEAD_DIM).transpose(1, 0, 2)


def _to_heads_t(t, n):
    S = t.shape[0]
    return t.T.reshape(n, HEAD_DIM, S)


def _from_heads_t(t):
    H, Dh, S = t.shape
    return t.reshape(H * Dh, S).T


def _t5_bucket(dist):
    max_exact = N_BUCKETS // 2
    d = np.maximum(dist, 1).astype(np.float32)
    large = max_exact + (np.log(d / max_exact) / np.log(MAX_DISTANCE / max_exact)
                         * (N_BUCKETS - max_exact)).astype(np.int32)
    large = np.minimum(large, N_BUCKETS - 1)
    return np.where(dist < max_exact, dist, large).astype(np.int32)


def _band_tables():
    qi = np.arange(BAND)[:, None]
    kj = np.arange(2 * BAND)[None, :]
    sub = qi + BAND - kj
    band = (sub >= 0) & (sub <= BAND)
    out = []
    for d in DILATIONS:
        bucket = _t5_bucket(np.clip(sub, 0, BAND) * d)
        out.append(np.where(band, bucket, -1).astype(np.int32))
    return np.stack(out)


_PACK = (("w_in", 770), ("w_out", 256), ("w_xq", 64), ("w_xk", 64), ("w_xv", 64), ("w_xo", 64),
         ("w_gate", 704), ("w_up", 704), ("w_down", 704))


def _pack(shards):
    rows = [shards[n].reshape(-1, PACK_COLS) for n, _ in _PACK]
    total = sum(r.shape[0] for r in rows)
    pad = (-total) % 128
    if pad:
        rows.append(jnp.zeros((pad, PACK_COLS), rows[0].dtype))
    return jnp.concatenate(rows, axis=0)


def _unpack(pack, shapes):
    out, r = {}, 0
    for n, _ in _PACK:
        cnt = int(np.prod(shapes[n])) // PACK_COLS
        out[n] = pack[r:r + cnt].reshape(shapes[n])
        r += cnt
    return out


_COL_SHARDED = ("w_in", "w_xo", "w_gate", "w_up")


def _full_weight(gathered, name):
    return jnp.concatenate(gathered, axis=1 if name in _COL_SHARDED else 0)


def _split_weight(full, name):
    return jnp.split(full, N_CHIPS, axis=1 if name in _COL_SHARDED else 0)


_SMALL = ("g_mix_pre", "g_mix_post", "g_xattn_pre", "g_mem", "g_xattn_post", "g_ffn_pre", "g_ffn_post")


def _pack_small(vals):
    D = vals["g_mix_pre"].shape[1]
    rows = [vals[n].reshape(1, D) for n in _SMALL]
    misc = jnp.concatenate([vals["b_f"].reshape(-1), vals["rel_bias"].reshape(-1)])
    rows.append(jnp.pad(misc, (0, D - misc.shape[0])).reshape(1, D))
    rows.append(jnp.zeros((16 - len(rows), D), F32))
    return jnp.concatenate(rows, axis=0)


def _unpack_small(pack):
    out = {n: pack[i:i + 1] for i, n in enumerate(_SMALL)}
    out["b_f"] = pack[7, 0:N_FOX_HEADS].reshape(1, N_FOX_HEADS)
    out["rel_bias"] = pack[7, N_FOX_HEADS:N_FOX_HEADS + N_BUCKETS * N_DIL_HEADS].reshape(N_BUCKETS, N_DIL_HEADS)
    return out


def kernel(x, mem, g_mix_pre, w_in, b_f, rel_bias, w_out, g_mix_post, g_xattn_pre, g_mem, w_xq, w_xk, w_xv, w_xo, g_xattn_post, g_ffn_pre, w_gate, w_up, w_down, g_ffn_post, loss_target, m_g_mix_pre, m_w_in, m_b_f, m_rel_bias, m_w_out, m_g_mix_post, m_g_xattn_pre, m_g_mem, m_w_xq, m_w_xk, m_w_xv, m_w_xo, m_g_xattn_post, m_g_ffn_pre, m_w_gate, m_w_up, m_w_down, m_g_ffn_post, v_g_mix_pre, v_w_in, v_b_f, v_rel_bias, v_w_out, v_g_mix_post, v_g_xattn_pre, v_g_mem, v_w_xq, v_w_xk, v_w_xv, v_w_xo, v_g_xattn_post, v_g_ffn_pre, v_w_gate, v_w_up, v_w_down, v_g_ffn_post):
    args = dict(locals())
    big = [n for n, _ in _PACK]
    names = ["g_mix_pre", "w_in", "b_f", "rel_bias", "w_out", "g_mix_post", "g_xattn_pre", "g_mem", "w_xq",
             "w_xk", "w_xv", "w_xo", "g_xattn_post", "g_ffn_pre", "w_gate", "w_up", "w_down", "g_ffn_post"]
    xs = x[0]
    S, D = xs.shape
    assert S % (BAND * DILATIONS[-1]) == 0
    shard_shapes = {n: args[n].shape[1:] for n in big}
    my_x, my_y, my_c = lax.axis_index("x"), lax.axis_index("y"), lax.axis_index("c")

    gathered = _chip_all_gather(_pack({n: args[n][0].astype(BF16) for n in big}), "weights_all_gather")
    per_chip = [_unpack(gathered[j], shard_shapes) for j in range(N_CHIPS)]
    W = {n: _full_weight([pc[n] for pc in per_chip], n) for n in big}
    w_fox, w_fg, w_dil = (W["w_in"][:, :3 * FOX_WIDTH], W["w_in"][:, 3 * FOX_WIDTH:3 * FOX_WIDTH + N_FOX_HEADS],
                          W["w_in"][:, 3 * FOX_WIDTH + N_FOX_HEADS:])
    w_qkv = jnp.concatenate([w_fox, w_dil], axis=1)
    w_fg_pad = jnp.pad(w_fg, ((0, 0), (0, LANES - N_FOX_HEADS)))
    F = W["w_gate"].shape[1]
    nft = F // FF_TILE
    w_gu = jnp.stack([W["w_gate"].reshape(D, nft, FF_TILE), W["w_up"].reshape(D, nft, FF_TILE)],
                     axis=2).reshape(D, 2 * F)

    h1 = _rms_fwd(xs, g_mix_pre, "rms_mix_pre")
    qkv = _mm(h1, w_qkv, "nn", BF16, "proj_qkv", tm=2048)
    fg = _mm(h1, w_fg_pad, "nn", F32, "proj_gate")
    fg_t = fg[:, :N_FOX_HEADS].T
    b_col = b_f.reshape(N_FOX_HEADS, 1)
    c_t = _forget_fwd(fg_t, b_col, "forget_cumsum")
    fq_s, fk_s, fv_s = (qkv[:, i * FOX_WIDTH:(i + 1) * FOX_WIDTH] for i in range(3))
    fqt, fvt = _to_heads_t(fq_s, N_FOX_HEADS), _to_heads_t(fv_s, N_FOX_HEADS)
    unit = jnp.full((N_FOX_HEADS, S), 1.0, BF16)
    inv_scale = jnp.full((N_FOX_HEADS, S), 1.0 / QK_SCALE, BF16)
    ka = _lanes_operand(_to_heads(fk_s, N_FOX_HEADS), list(_split3(-c_t)) + [unit] * EXTRA)
    qa_f = _rows_operand(fqt, [inv_scale] * EXTRA)
    o_fox_t, lse_fox = _fox_fwd(qa_f, ka, _with_ones(fvt), "fox_fwd")

    bucket_map = _band_tables()
    onehot = (jnp.asarray(bucket_map)[..., None] == jnp.arange(N_BUCKETS)).astype(F32)
    bias_tab = jnp.einsum("pqkb,bh->phkq", onehot, rel_bias, precision=lax.Precision.HIGHEST)
    bias_tab = jnp.where(jnp.asarray(bucket_map.transpose(0, 2, 1) >= 0)[:, None], bias_tab, NEG)
    bias_t = bias_tab.reshape(3, HEAD_PAIRS, 2, 2 * BAND, BAND).transpose(0, 1, 3, 2, 4).reshape(
        3, HEAD_PAIRS, 2 * BAND, 2 * BAND)
    views = [(qkv.reshape(1, S, qkv.shape[1]), DIL_Q_BLOCK)] + [
        (_to_residues(qkv, 1, 3 * DIL_WIDTH, d, f"dilated_qkv_residues_{d}"), 0) for d in DILATIONS[1:]]

    def to_tok(stat, d):
        return stat.reshape(N_DIL_HEADS, d, S // d).swapaxes(1, 2).reshape(N_DIL_HEADS, S)

    def to_perm(stat, d):
        return stat.reshape(N_DIL_HEADS, S // d, d).swapaxes(1, 2).reshape(HEAD_PAIRS, 2, S)

    def tok_or_res(t):
        return t.reshape(t.shape[1:]) if t.shape[0] == 1 else t

    lse_tok = jnp.stack([to_tok(_dil_lse(*views[p], bias_t[p], f"dilated_lse_{d}"), d)
                         for p, d in enumerate(DILATIONS)])
    lse_joint = _lse_join(lse_tok, "dilated_lse_join")
    lse_perm = [to_perm(lse_joint, d) for d in DILATIONS]
    o_dil = [tok_or_res(_dil_out(*views[p], bias_t[p], lse_perm[p], f"dilated_out_{d}"))
             for p, d in enumerate(DILATIONS)]
    o_cat = _sum_cast_cols([[_from_heads_t(o_fox_t)]] + [[o] for o in o_dil], BF16, "mixer_out_cat")
    w_out_b = W["w_out"]
    w_out_cat = jnp.concatenate([w_out_b[:FOX_WIDTH]] + [w_out_b[FOX_WIDTH:]] * 3, axis=0)
    a = _mm(o_cat, w_out_cat, "nn", F32, "proj_out", tm=2048, tk=2048)
    x1, h2 = _resid_norm(xs, a, g_mix_post, g_xattn_pre, "resid_mix")

    hm = _rms_fwd(mem[0], g_mem, "rms_mem")
    q2 = _mm(h2, W["w_xq"], "nn", BF16, "xattn_q")
    w_xkv = jnp.concatenate([W["w_xk"], W["w_xv"]], axis=1)
    kvm = _mm(hm, w_xkv, "nn", BF16, "xattn_kv")
    MW = N_MEM_HEADS * HEAD_DIM
    oc, lse_mem = _mem_fwd(q2, kvm, "xattn_fwd")
    y2 = _mm(oc, W["w_xo"], "nn", F32, "xattn_o")
    x2, h3 = _resid_norm(x1, y2, g_xattn_post, g_ffn_pre, "resid_xattn")

    gu, act = _ffn_up(h3, w_gu, "ffn_up")
    y3 = _mm(act, W["w_down"], "nn", F32, "ffn_down", tk=2816)
    dx3, loss_tile = _final_loss(x2, y3, g_ffn_post, loss_target[0], "final_loss")

    grads = {}
    small = {}
    _, dy3_b, dg = _rms_bwd(y3, g_ffn_post, dx3, None, "bwd_norm_ffn_post", want=("bf16",))
    small["g_ffn_post"] = dg[0:1]
    grads["w_down"] = _mm(act, dy3_b, "tn", F32, "grad_w_down", tm=1408)
    dgu = _ffn_dact(dy3_b, W["w_down"], gu, "ffn_dact")
    dw_gu = _mm(h3, dgu, "tn", F32, "grad_w_gu", tn=1408).reshape(D, nft, 2, FF_TILE)
    grads["w_gate"], grads["w_up"] = dw_gu[:, :, 0].reshape(D, F), dw_gu[:, :, 1].reshape(D, F)
    dh3 = _mm(dgu, w_gu, "nt", F32, "bwd_ffn_in", tm=2048, tk=1408)
    dx2, _, dg = _rms_bwd(x2, g_ffn_pre, dh3, dx3, "bwd_norm_ffn_pre", want=("f32",))
    small["g_ffn_pre"] = dg[0:1]

    _, dy2_b, dg = _rms_bwd(y2, g_xattn_post, dx2, None, "bwd_norm_xattn_post", want=("bf16",))
    small["g_xattn_post"] = dg[0:1]
    grads["w_xo"] = _mm(oc, dy2_b, "tn", F32, "grad_w_xo")
    doc = _mm(dy2_b, W["w_xo"], "nt", BF16, "bwd_xattn_o")
    delta_mem = _head_rowdot(doc, [oc], "xattn_delta")[:, :N_MEM_HEADS].T.reshape(N_MEM_HEADS // 2, 2, S)
    dq2, dkm, dvm = _mem_bwd(q2, kvm, doc, lse_mem, delta_mem, "xattn_bwd")
    dkvm = jnp.concatenate([dkm, dvm], axis=1).astype(BF16)
    grads["w_xq"] = _mm(h2, dq2, "tn", F32, "grad_w_xq")
    dw_xkv = _mm(hm, dkvm, "tn", F32, "grad_w_xkv")
    grads["w_xk"], grads["w_xv"] = dw_xkv[:, :MW], dw_xkv[:, MW:]
    dhm = _mm(dkvm, w_xkv, "nt", F32, "bwd_xattn_kv")
    _, _, dg = _rms_bwd(mem[0], g_mem, dhm, None, "bwd_norm_mem", want=())
    small["g_mem"] = dg[0:1]
    dh2 = _mm(dq2, W["w_xq"], "nt", F32, "bwd_xattn_q")
    dx1, _, dg = _rms_bwd(x1, g_xattn_pre, dh2, dx2, "bwd_norm_xattn_pre", want=("f32",))
    small["g_xattn_pre"] = dg[0:1]

    _, da_b, dg = _rms_bwd(a, g_mix_post, dx1, None, "bwd_norm_mix_post", want=("bf16",))
    small["g_mix_post"] = dg[0:1]
    dw_out_cat = _mm(o_cat, da_b, "tn", F32, "grad_w_out")
    dw_out_dil = _add_n([dw_out_cat[FOX_WIDTH + p * DIL_WIDTH:FOX_WIDTH + (p + 1) * DIL_WIDTH] for p in range(3)],
                        "grad_w_out_dil")
    grads["w_out"] = jnp.concatenate([dw_out_cat[:FOX_WIDTH], dw_out_dil], axis=0)
    do = _mm(da_b, w_out_b, "nt", BF16, "bwd_proj_out")
    do_fox, do_dil = do[:, :FOX_WIDTH], do[:, FOX_WIDTH:]

    delta_fox = _head_rowdot(do_fox, [o_cat[:, :FOX_WIDTH]], "fox_delta")[:, :N_FOX_HEADS].T
    qa_b = lax.dynamic_update_slice(qa_f, jnp.stack(_split3(lse_fox[:, 0] * (-1.0 / QK_SCALE)), axis=1),
                                    (0, HEAD_DIM + EXTRA, 0))
    va = _lanes_operand(_to_heads(fv_s, N_FOX_HEADS), [unit] * EXTRA)
    doa = _rows_operand(_to_heads_t(do_fox, N_FOX_HEADS), list(_split3(-delta_fox)))
    dq_aug, dk_aug, dvf = _fox_bwd(qa_b, ka, ka.transpose(0, 2, 1), va, doa, "fox_bwd")
    dqf, dkf = dq_aug[:, :HEAD_DIM], dk_aug[:, :HEAD_DIM]
    dfg_t, db_f = _forget_bwd(fg_t, b_col, dq_aug[:, HEAD_DIM + EXTRA], dk_aug[:, HEAD_DIM], "forget_bwd")

    delta_dil = _head_rowdot(do_dil, o_dil, "dilated_delta")[:, :N_DIL_HEADS].T
    do_res = [do_dil.reshape(1, S, DIL_WIDTH)] + [
        _to_residues(do, 1, DIL_WIDTH, d, f"dilated_do_residues_{d}") for d in DILATIONS[1:]]
    dil_grads = [_dil_bwd(*views[p], do_res[p], bias_t[p], lse_perm[p], to_perm(delta_dil, d), f"dilated_bwd_{d}")
                 for p, d in enumerate(DILATIONS)]
    dbias_t = jnp.stack([g[3].reshape(HEAD_PAIRS, 2 * BAND, 2, BAND).transpose(0, 2, 1, 3).reshape(
        N_DIL_HEADS, 2 * BAND, BAND) for g in dil_grads])
    d_rel = _bucket_reduce(dbias_t, jnp.asarray(bucket_map.transpose(0, 2, 1)), "rel_bias_grad")[:, :N_DIL_HEADS]
    dfg_pad = jnp.pad(dfg_t.T, ((0, 0), (0, LANES - N_FOX_HEADS))).astype(BF16)
    dcat = _sum_cast_cols([[_from_heads_t(dqf)], [_from_heads_t(dkf)], [_from_heads_t(dvf)]]
                          + [[tok_or_res(g[j]) for g in dil_grads] for j in range(3)],
                          BF16, "dqkv_assemble", tail=dfg_pad)
    dw_cat = _mm(h1, dcat, "tn", F32, "grad_w_qkv", tm=512, tn=3200)
    n_qkv = 3 * (FOX_WIDTH + DIL_WIDTH)
    grads["w_in"] = jnp.concatenate([dw_cat[:, :3 * FOX_WIDTH], dw_cat[:, n_qkv:n_qkv + N_FOX_HEADS],
                                     dw_cat[:, 3 * FOX_WIDTH:n_qkv]], axis=1)
    w_cat = jnp.concatenate([w_qkv, w_fg_pad], axis=1)
    dh1 = _mm(dcat, w_cat, "nt", F32, "bwd_proj_in", tk=3200)
    grad_x, _, dg = _rms_bwd(xs, g_mix_pre, dh1, dx1, "bwd_norm_mix_pre", want=("f32",))
    small["g_mix_pre"] = dg[0:1]
    small["b_f"] = db_f[:, 0].reshape(1, N_FOX_HEADS)
    small["rel_bias"] = d_rel

    split = {n: _split_weight(grads[n], n) for n in big}
    parts = jnp.stack([_pack({n: split[n][j].astype(BF16) for n in big}) for j in range(N_CHIPS)])
    R = parts.shape[1]
    half = R // 2
    keep = lax.dynamic_slice_in_dim(parts, my_c * half, half, axis=1)
    give = lax.dynamic_slice_in_dim(parts, (1 - my_c) * half, half, axis=1)
    got = _sibling_exchange(give, "grads_to_sibling")
    chip_sum = _add_n([keep.reshape(-1, PACK_COLS), got.reshape(-1, PACK_COLS)], "grads_add_sibling")
    chip_sum = chip_sum.reshape(N_CHIPS, half, PACK_COLS)
    my_chip = 2 * my_x + my_y
    from_chips = _chip_scatter(chip_sum.astype(BF16), "grads_to_chips")
    own = lax.dynamic_index_in_dim(chip_sum, my_chip, axis=0, keepdims=False)
    g_half = _add_n([own, from_chips[0], from_chips[1], from_chips[2]], "grads_add_chips")
    other_half = _sibling_exchange(g_half, "grads_share_sibling")
    g_pack = jnp.where(my_c == 0, jnp.concatenate([g_half, other_half]), jnp.concatenate([other_half, g_half]))

    small_pack = _pack_small(small)
    small_pack = small_pack.at[8, 0].set(loss_tile[0, 0])
    everyone = _all_to_all_small(small_pack, "small_all_gather")
    small_sum = _add_n([everyone[i] for i in range(8)], "small_sum")
    loss = small_sum[8, 0]
    g_small = _unpack_small(small_sum)

    outs = {"grad": _unpack(g_pack, shard_shapes), "delta": {}, "new_m": {}, "new_v": {}}
    for n in big:
        outs["delta"][n], outs["new_m"][n], outs["new_v"][n] = _adamw(
            args[n][0], outs["grad"][n], args["m_" + n][0], args["v_" + n][0], f"adamw_{n}")
    sw = _pack_small({n: args[n] for n in _SMALL + ("b_f", "rel_bias")})
    sm = _pack_small({n: args["m_" + n] for n in _SMALL + ("b_f", "rel_bias")})
    sv = _pack_small({n: args["v_" + n] for n in _SMALL + ("b_f", "rel_bias")})
    sd, snm, snv = _adamw(sw, small_sum.at[8, 0].set(0.0), sm, sv, "adamw_small")
    souts = {"grad": g_small, "delta": _unpack_small(sd), "new_m": _unpack_small(snm), "new_v": _unpack_small(snv)}

    def leaf(kind, n):
        if n in souts[kind]:
            return souts[kind][n].reshape(args[n].shape)
        return outs[kind][n].reshape(args[n].shape)

    result = [loss, grad_x.reshape(x.shape)]
    for kind in ("grad", "delta", "new_m", "new_v"):
        result += [leaf(kind, n) for n in names]
    return tuple(result)
```

```python
import numpy as np
import jax
import jax.numpy as jnp
from jax import lax
from jax.experimental import pallas as pl
from jax.experimental.pallas import tpu as pltpu

F32 = jnp.float32
BF16 = jnp.bfloat16
MESH_IDS = pl.DeviceIdType.MESH

LANES = 128
HEAD_DIM = 64
N_FOX_HEADS = 8
N_DIL_HEADS = 8
N_MEM_HEADS = 4
FOX_WIDTH = N_FOX_HEADS * HEAD_DIM
DIL_WIDTH = N_DIL_HEADS * HEAD_DIM
DILATIONS = (1, 4, 16)
BAND = 128
BAND_CHUNK_MAX = 8 * BAND
N_BUCKETS = 32
MAX_DISTANCE = 2048
QK_SCALE = HEAD_DIM ** -0.5
RMS_EPS = 1e-6
NEG = -1e30
VMEM_LIMIT = 56 << 20

ADAM_LR = 0.001
ADAM_B1 = 0.9
ADAM_B2 = 0.999
ADAM_EPS = 1e-08
ADAM_WD = 0.01
ADAM_STEP = 10

N_CHIPS = 4
PACK_COLS = 1024


def _params(*sem):
    return pltpu.CompilerParams(dimension_semantics=sem, vmem_limit_bytes=VMEM_LIMIT)


def _fit(n, cap):
    if n <= cap:
        return n
    t = (cap // LANES) * LANES
    while t >= LANES:
        if n % t == 0:
            return t
        t -= LANES
    raise ValueError(f"no lane-aligned tile for {n} under {cap}")


def _dot(a, b, dims):
    return lax.dot_general(a, b, (dims, ((), ())), preferred_element_type=F32)


_NN = ((1,), (0,))
_NT = ((1,), (1,))
_TN = ((0,), (0,))


def _mm(a, b, mode, out_dtype, name, tm=1024, tn=1024, tk=1024):
    if mode == "nn":
        (M, K), N = a.shape, b.shape[1]
    elif mode == "nt":
        (M, K), N = a.shape, b.shape[0]
    else:
        (K, M), N = a.shape, b.shape[1]
    tm, tn, tk = _fit(M, tm), _fit(N, tn), _fit(K, tk)
    nk = K // tk
    if mode == "tn":
        a_spec = pl.BlockSpec((tk, tm), lambda i, j, k: (k, i))
    else:
        a_spec = pl.BlockSpec((tm, tk), lambda i, j, k: (i, k))
    if mode == "nt":
        b_spec = pl.BlockSpec((tn, tk), lambda i, j, k: (j, k))
    else:
        b_spec = pl.BlockSpec((tk, tn), lambda i, j, k: (k, j))
    dims = {"nn": _NN, "nt": _NT, "tn": _TN}[mode]

    def body(a_ref, b_ref, o_ref, *acc):
        prod = _dot(a_ref[...].astype(BF16), b_ref[...].astype(BF16), dims)
        if nk == 1:
            o_ref[...] = prod.astype(o_ref.dtype)
            return
        acc_ref, k = acc[0], pl.program_id(2)

        @pl.when(k == 0)
        def _():
            acc_ref[...] = prod

        @pl.when(k > 0)
        def _():
            acc_ref[...] += prod

        @pl.when(k == nk - 1)
        def _():
            o_ref[...] = acc_ref[...].astype(o_ref.dtype)

    return pl.pallas_call(
        body, name=name, grid=(M // tm, N // tn, nk),
        in_specs=[a_spec, b_spec],
        out_specs=pl.BlockSpec((tm, tn), lambda i, j, k: (i, j)),
        out_shape=jax.ShapeDtypeStruct((M, N), out_dtype),
        scratch_shapes=[pltpu.VMEM((tm, tn), F32)] if nk > 1 else [],
        compiler_params=_params("parallel", "parallel", "arbitrary"),
    )(a, b)


def _rms_rows(x):
    return lax.rsqrt(jnp.mean(x * x, axis=-1, keepdims=True) + RMS_EPS)


def _rms_fwd(x, g, name, tr=512):
    S, D = x.shape
    tr = _fit(S, tr)

    def body(x_ref, g_ref, h_ref):
        xv = x_ref[...]
        h_ref[...] = (xv * _rms_rows(xv) * g_ref[...]).astype(BF16)

    return pl.pallas_call(
        body, name=name, grid=(S // tr,),
        in_specs=[pl.BlockSpec((tr, D), lambda i: (i, 0)), pl.BlockSpec((1, D), lambda i: (0, 0))],
        out_specs=pl.BlockSpec((tr, D), lambda i: (i, 0)),
        out_shape=jax.ShapeDtypeStruct((S, D), BF16),
        compiler_params=_params("parallel"),
    )(x, g)


def _resid_norm(xres, y, g_post, g_next, name, tr=512):
    S, D = xres.shape
    tr = _fit(S, tr)

    def body(x_ref, y_ref, gp_ref, gn_ref, xn_ref, h_ref):
        yv = y_ref[...]
        xn = x_ref[...] + yv * _rms_rows(yv) * gp_ref[...]
        xn_ref[...] = xn
        h_ref[...] = (xn * _rms_rows(xn) * gn_ref[...]).astype(BF16)

    row = pl.BlockSpec((tr, D), lambda i: (i, 0))
    vec = pl.BlockSpec((1, D), lambda i: (0, 0))
    return pl.pallas_call(
        body, name=name, grid=(S // tr,),
        in_specs=[row, row, vec, vec], out_specs=[row, row],
        out_shape=[jax.ShapeDtypeStruct((S, D), F32), jax.ShapeDtypeStruct((S, D), BF16)],
        compiler_params=_params("parallel"),
    )(xres, y, g_post, g_next)


def _final_loss(xres, y, g_post, target, name, tr=512):
    S, D = xres.shape
    tr = _fit(S, tr)

    def body(x_ref, y_ref, gp_ref, t_ref, d_ref, loss_ref):
        i = pl.program_id(0)
        yv = y_ref[...]
        err = x_ref[...] + yv * _rms_rows(yv) * gp_ref[...] - t_ref[...]
        d_ref[...] = err * (1.0 / D)

        @pl.when(i == 0)
        def _():
            loss_ref[...] = jnp.zeros_like(loss_ref)

        part = jnp.sum(jnp.sum(err * err, axis=1, keepdims=True), axis=0, keepdims=True)
        loss_ref[...] += jnp.broadcast_to(part * (0.5 / D), loss_ref.shape)

    row = pl.BlockSpec((tr, D), lambda i: (i, 0))
    vec = pl.BlockSpec((1, D), lambda i: (0, 0))
    return pl.pallas_call(
        body, name=name, grid=(S // tr,),
        in_specs=[row, row, vec, row],
        out_specs=[row, pl.BlockSpec((8, LANES), lambda i: (0, 0))],
        out_shape=[jax.ShapeDtypeStruct((S, D), F32), jax.ShapeDtypeStruct((8, LANES), F32)],
        compiler_params=_params("arbitrary"),
    )(xres, y, g_post, target)


def _rms_bwd(xin, g, dy, dres, name, want=("f32", "bf16"), tr=512):
    S, D = xin.shape
    tr = _fit(S, tr)
    has_res = dres is not None

    def body(*refs):
        refs = list(refs)
        dg_ref = refs.pop()
        dxb_ref = refs.pop() if "bf16" in want else None
        dx_ref = refs.pop() if "f32" in want else None
        dr_ref = refs.pop() if has_res else None
        x_ref, g_ref, dy_ref = refs
        i = pl.program_id(0)
        xv = x_ref[...]
        dyv = dy_ref[...].astype(F32)
        xhat = xv * _rms_rows(xv)
        dxhat = dyv * g_ref[...]
        r = _rms_rows(xv)
        dx = r * (dxhat - xhat * jnp.mean(dxhat * xhat, axis=-1, keepdims=True))
        if has_res:
            dx = dx + dr_ref[...]
        if dx_ref is not None:
            dx_ref[...] = dx
        if dxb_ref is not None:
            dxb_ref[...] = dx.astype(BF16)

        @pl.when(i == 0)
        def _():
            dg_ref[...] = jnp.zeros_like(dg_ref)

        dg_ref[...] += jnp.broadcast_to(jnp.sum(dyv * xhat, axis=0, keepdims=True), dg_ref.shape)

    row = pl.BlockSpec((tr, D), lambda i: (i, 0))
    vec = pl.BlockSpec((1, D), lambda i: (0, 0))
    acc = pl.BlockSpec((8, D), lambda i: (0, 0))
    ins = [xin, g, dy] + ([dres] if has_res else [])
    dtypes = [dt for key, dt in (("f32", F32), ("bf16", BF16)) if key in want]
    outs = pl.pallas_call(
        body, name=name, grid=(S // tr,),
        in_specs=[row, vec, row] + ([row] if has_res else []),
        out_specs=[row] * len(dtypes) + [acc],
        out_shape=[jax.ShapeDtypeStruct((S, D), dt) for dt in dtypes] + [jax.ShapeDtypeStruct((8, D), F32)],
        compiler_params=_params("arbitrary"),
    )(*ins)
    by_key = dict(zip([key for key in ("f32", "bf16") if key in want], outs[:-1]))
    return by_key.get("f32"), by_key.get("bf16"), outs[-1]


def _tri(n, upper):
    r = lax.broadcasted_iota(jnp.int32, (n, n), 0)
    c = lax.broadcasted_iota(jnp.int32, (n, n), 1)
    return jnp.where((r <= c) if upper else (r >= c), 1.0, 0.0).astype(F32)


def _forget_fwd(fg_t, b_col, name, ts=512):
    H, S = fg_t.shape
    ts = _fit(S, ts)

    def body(f_ref, b_ref, c_ref, carry_ref):
        i = pl.program_id(0)

        @pl.when(i == 0)
        def _():
            carry_ref[...] = jnp.zeros_like(carry_ref)

        z = f_ref[...] + b_ref[...]
        logf = jnp.minimum(z, 0.0) - jnp.log(1.0 + jnp.exp(-jnp.abs(z)))
        run = lax.dot_general(logf, _tri(ts, True), (_NN, ((), ())), precision=lax.Precision.HIGHEST,
                              preferred_element_type=F32) + carry_ref[:, 0:1]
        c_ref[...] = run
        carry_ref[...] = jnp.broadcast_to(
            carry_ref[:, 0:1] + jnp.sum(logf, axis=1, keepdims=True), carry_ref.shape)

    return pl.pallas_call(
        body, name=name, grid=(S // ts,),
        in_specs=[pl.BlockSpec((H, ts), lambda i: (0, i)), pl.BlockSpec((H, 1), lambda i: (0, 0))],
        out_specs=pl.BlockSpec((H, ts), lambda i: (0, i)),
        out_shape=jax.ShapeDtypeStruct((H, S), F32),
        scratch_shapes=[pltpu.VMEM((H, LANES), F32)],
        compiler_params=_params("arbitrary"),
    )(fg_t, b_col)


def _forget_bwd(fg_t, b_col, dc_plus, dc_minus, name, ts=512):
    H, S = fg_t.shape
    ts = _fit(S, ts)
    nb = S // ts

    def body(f_ref, b_ref, dcp_ref, dcm_ref, df_ref, db_ref, carry_ref):
        i = pl.program_id(0)

        @pl.when(i == 0)
        def _():
            carry_ref[...] = jnp.zeros_like(carry_ref)
            db_ref[...] = jnp.zeros_like(db_ref)

        dc = dcp_ref[...] - dcm_ref[...]
        suffix = lax.dot_general(dc, _tri(ts, False), (_NN, ((), ())), precision=lax.Precision.HIGHEST,
                                 preferred_element_type=F32) + carry_ref[:, 0:1]
        z = f_ref[...] + b_ref[...]
        sig_neg = 1.0 / (1.0 + jnp.exp(z))
        df = suffix * sig_neg
        df_ref[...] = df
        carry_ref[...] = jnp.broadcast_to(
            carry_ref[:, 0:1] + jnp.sum(dc, axis=1, keepdims=True), carry_ref.shape)
        db_ref[...] += jnp.broadcast_to(jnp.sum(df, axis=1, keepdims=True), db_ref.shape)

    rev = pl.BlockSpec((H, ts), lambda i: (0, nb - 1 - i))
    return pl.pallas_call(
        body, name=name, grid=(nb,),
        in_specs=[rev, pl.BlockSpec((H, 1), lambda i: (0, 0)), rev, rev],
        out_specs=[rev, pl.BlockSpec((H, LANES), lambda i: (0, 0))],
        out_shape=[jax.ShapeDtypeStruct((H, S), F32), jax.ShapeDtypeStruct((H, LANES), F32)],
        scratch_shapes=[pltpu.VMEM((H, LANES), F32)],
        compiler_params=_params("arbitrary"),
    )(fg_t, b_col, dc_plus, dc_minus)


ONES_ROWS = 16
EXTRA = 3


def _split3(x):
    hi = lax.reduce_precision(x, 8, 7)
    mid = lax.reduce_precision(x - hi, 8, 7)
    lo = lax.reduce_precision(x - hi - mid, 8, 7)
    return hi.astype(BF16), mid.astype(BF16), lo.astype(BF16)


def _lanes_operand(t, extras):
    block = jnp.pad(jnp.stack(extras, axis=-1), ((0, 0), (0, 0), (0, LANES - HEAD_DIM - len(extras))))
    return jnp.concatenate([t, block], axis=-1)


def _rows_operand(t, extras):
    block = jnp.pad(jnp.stack(extras, axis=1), ((0, 0), (0, LANES - HEAD_DIM - len(extras)), (0, 0)))
    return jnp.concatenate([t, block], axis=1)


def _with_ones(t):
    return jnp.concatenate([t, jnp.ones((t.shape[0], ONES_ROWS, t.shape[2]), t.dtype)], axis=1)


def _fox_fwd(qa, ka, vt, name, tq=512, tk=1024):
    H, _, S = qa.shape
    Dh = HEAD_DIM
    tk = _fit(S, tk)
    tq = _fit(tk, tq)
    ratio = tk // tq

    def body(qa_ref, ka_ref, vt_ref, o_ref, lse_ref, m_ref, acc_ref, sa_ref, sb_ref, ta_ref, tb_ref):
        i = pl.program_id(1)
        qv = qa_ref[...] * QK_SCALE
        m_ref[...] = jnp.full_like(m_ref, NEG)
        acc_ref[...] = jnp.zeros_like(acc_ref)
        n = i // ratio
        q_off = (i - n * ratio) * tq

        def scores(j, s_ref, t_ref, diagonal):
            off = pl.multiple_of(j * tk, LANES)
            s = _dot(ka_ref[pl.ds(off, tk), :], qv, _NN)
            if diagonal:
                key = lax.broadcasted_iota(jnp.int32, (tk, tq), 0)
                qry = lax.broadcasted_iota(jnp.int32, (tk, tq), 1) + q_off
                s = jnp.where(key <= qry, s, NEG)
            s_ref[...] = s
            t_ref[...] = jnp.max(s, axis=0, keepdims=True)

        def absorb(j, s_ref, t_ref):
            off = pl.multiple_of(j * tk, LANES)
            m_old = m_ref[...]
            m_new = jnp.maximum(m_old, t_ref[...])
            p = jnp.exp(s_ref[...] - m_new)
            alpha = jnp.exp(m_old - m_new)
            acc_ref[...] = alpha * acc_ref[...] + _dot(vt_ref[:, pl.ds(off, tk)], p.astype(BF16), _NN)
            m_ref[...] = m_new

        scores(n, sa_ref, ta_ref, True)

        def loop_body(jj, carry):
            scores(2 * jj, sb_ref, tb_ref, False)
            absorb(jnp.where(jj == 0, n, 2 * jj - 1), sa_ref, ta_ref)
            scores(2 * jj + 1, sa_ref, ta_ref, False)
            absorb(2 * jj, sb_ref, tb_ref)
            return carry

        pairs = n // 2
        lax.fori_loop(0, pairs, loop_body, 0)
        held = jnp.where(pairs == 0, n, 2 * pairs - 1)

        @pl.when(n % 2 == 1)
        def _():
            scores(n - 1, sb_ref, tb_ref, False)
            absorb(held, sa_ref, ta_ref)
            absorb(n - 1, sb_ref, tb_ref)

        @pl.when(n % 2 == 0)
        def _():
            absorb(held, sa_ref, ta_ref)

        l = acc_ref[Dh:Dh + 1, :]
        o_ref[...] = acc_ref[0:Dh, :] / l
        lse_ref[...] = m_ref[...] + jnp.log(l)

    return pl.pallas_call(
        body, name=name, grid=(H, S // tq),
        in_specs=[pl.BlockSpec((None, LANES, tq), lambda h, i: (h, 0, i)),
                  pl.BlockSpec((None, S, LANES), lambda h, i: (h, 0, 0)),
                  pl.BlockSpec((None, Dh + ONES_ROWS, S), lambda h, i: (h, 0, 0))],
        out_specs=[pl.BlockSpec((None, Dh, tq), lambda h, i: (h, 0, i)),
                   pl.BlockSpec((None, 1, tq), lambda h, i: (h, 0, i))],
        out_shape=[jax.ShapeDtypeStruct((H, Dh, S), F32), jax.ShapeDtypeStruct((H, 1, S), F32)],
        scratch_shapes=[pltpu.VMEM((1, tq), F32), pltpu.VMEM((Dh + ONES_ROWS, tq), F32),
                        pltpu.VMEM((tk, tq), F32), pltpu.VMEM((tk, tq), F32),
                        pltpu.VMEM((1, tq), F32), pltpu.VMEM((1, tq), F32)],
        compiler_params=_params("parallel", "arbitrary"),
    )(qa, ka, vt)


def _fox_bwd(qa, ka, kta, va, doa, name, tq=1024, tk=512):
    H, _, S = qa.shape
    Dh, Da = HEAD_DIM, HEAD_DIM + ONES_ROWS
    tq = _fit(S, tq)
    tk = _fit(tq, tk)
    ratio = tq // tk
    nq = S // tq
    nk = S // tk

    def body(ka_ref, kta_ref, va_ref, qa_ref, doa_ref, dqt_ref, dkt_ref, dvt_ref, dka_ref, dva_ref):
        j = pl.program_id(1)

        @pl.when(j == 0)
        def _():
            dqt_ref[...] = jnp.zeros_like(dqt_ref)

        kv = ka_ref[...]
        ktv = kta_ref[0:Da, :]
        vv = va_ref[...]
        dka_ref[...] = jnp.zeros_like(dka_ref)
        dva_ref[...] = jnp.zeros_like(dva_ref)
        i_diag = j // ratio
        k_off = (j - i_diag * ratio) * tk

        def step(i, diagonal):
            off = pl.multiple_of(i * tq, LANES)
            qv = qa_ref[:, pl.ds(off, tq)] * QK_SCALE
            dov = doa_ref[:, pl.ds(off, tq)]
            e = _dot(kv, qv, _NN)
            if diagonal:
                key = lax.broadcasted_iota(jnp.int32, (tk, tq), 0) + k_off
                qry = lax.broadcasted_iota(jnp.int32, (tk, tq), 1)
                e = jnp.where(key <= qry, e, NEG)
            p_t = jnp.exp(e)
            dva_ref[...] += _dot(dov[0:Dh, :], p_t.astype(BF16), _NT)
            ds_b = (p_t * _dot(vv, dov, _NN)).astype(BF16)
            dka_ref[...] += _dot(qv[0:Da, :], ds_b, _NT)
            dqt_ref[:, pl.ds(off, tq)] += _dot(ktv, ds_b, _NN)

        step(i_diag, True)

        def loop_body(i, carry):
            step(i, False)
            return carry

        lax.fori_loop(i_diag + 1, nq, loop_body, 0)
        dkt_ref[...] = dka_ref[...]
        dvt_ref[...] = dva_ref[...]

        @pl.when(j == nk - 1)
        def _():
            dqt_ref[0:Dh, :] = dqt_ref[0:Dh, :] * QK_SCALE

    lanes_tile = pl.BlockSpec((None, tk, LANES), lambda h, j: (h, j, 0))
    rows_tile = pl.BlockSpec((None, LANES, tk), lambda h, j: (h, 0, j))
    rows_full = pl.BlockSpec((None, LANES, S), lambda h, j: (h, 0, 0))
    return pl.pallas_call(
        body, name=name, grid=(H, nk),
        in_specs=[lanes_tile, rows_tile, lanes_tile, rows_full, rows_full],
        out_specs=[pl.BlockSpec((None, Da, S), lambda h, j: (h, 0, 0)),
                   pl.BlockSpec((None, Da, tk), lambda h, j: (h, 0, j)),
                   pl.BlockSpec((None, Dh, tk), lambda h, j: (h, 0, j))],
        out_shape=[jax.ShapeDtypeStruct((H, Da, S), F32), jax.ShapeDtypeStruct((H, Da, S), F32),
                   jax.ShapeDtypeStruct((H, Dh, S), F32)],
        scratch_shapes=[pltpu.VMEM((Da, tk), F32), pltpu.VMEM((Dh, tk), F32)],
        compiler_params=_params("parallel", "arbitrary"),
    )(ka, kta, va, qa, doa)


DIL_Q_BLOCK = 3 * FOX_WIDTH // LANES
HEAD_PAIRS = N_DIL_HEADS // 2
PAIR_BLOCKS = DIL_WIDTH // LANES


def _band_geometry(S, d):
    L = S // d
    chunk = min(BAND_CHUNK_MAX, L)
    assert L % chunk == 0 and chunk % BAND == 0
    return L, chunk, chunk // BAND, L // chunk


def _band_in_specs(S, d, base):
    L, chunk, nb, _ = _band_geometry(S, d)

    def col(kind):
        return lambda hp, r, i: (r, i, base + kind * PAIR_BLOCKS + hp)

    def col_prev(kind):
        return lambda hp, r, i: (r, jnp.maximum(i * nb - 1, 0), base + kind * PAIR_BLOCKS + hp)

    main = [pl.BlockSpec((None, chunk, LANES), col(kind)) for kind in range(3)]
    prev = [pl.BlockSpec((None, BAND, LANES), col_prev(kind)) for kind in range(3)]
    bias = pl.BlockSpec((None, 2 * BAND, 2 * BAND), lambda hp, r, i: (hp, 0, 0))
    stat = pl.BlockSpec((None, 2, chunk), lambda hp, r, i: (hp, 0, r * (L // chunk) + i))
    tok = pl.BlockSpec((None, chunk, LANES), lambda hp, r, i: (r, i, hp))
    return main, prev, bias, stat, tok


def _to_residues(x, col_block, width, d, name, tr=512):
    S = x.shape[0]
    tr = _fit(S, tr)

    def body(x_ref, o_ref, tmp_ref):
        for j in range(width // LANES):
            cols = slice(j * LANES, (j + 1) * LANES)
            tmp_ref[j] = x_ref[:, cols].astype(F32)
            for r in range(d):
                o_ref[r, :, cols] = tmp_ref[j, pl.ds(r, tr // d, stride=d), :].astype(o_ref.dtype)

    return pl.pallas_call(
        body, name=name, grid=(S // tr,),
        in_specs=[pl.BlockSpec((tr, width), lambda i: (i, col_block))],
        out_specs=pl.BlockSpec((d, tr // d, width), lambda i: (0, i, 0)),
        out_shape=jax.ShapeDtypeStruct((d, S // d, width), x.dtype),
        scratch_shapes=[pltpu.VMEM((width // LANES, tr, LANES), F32)],
        compiler_params=_params("parallel"),
    )(x)


def _token_rows(ref, cols, tmp_ref):
    if len(ref.shape) == 2:
        return ref[:, cols].astype(F32)
    d, rows = ref.shape[0], ref.shape[1]
    for r in range(d):
        tmp_ref[pl.ds(r, rows, stride=d), :] = ref[r, :, cols].astype(F32)
    return tmp_ref[...]


def _row_spec(t, tr):
    if t.ndim == 2:
        return pl.BlockSpec((tr, t.shape[1]), lambda i: (i, 0))
    d = t.shape[0]
    return pl.BlockSpec((d, tr // d, t.shape[2]), lambda i: (0, i, 0))


def _head_lanes(a):
    return lax.broadcasted_iota(jnp.int32, (1, LANES), 1) // HEAD_DIM == a


def _one_head(x, a):
    return jnp.where(_head_lanes(a), x, jnp.zeros_like(x))


def _head_stack(x):
    return jnp.concatenate([_one_head(x, 0), _one_head(x, 1)], axis=0)


def _pair_rows(ref, rows):
    return jnp.concatenate([ref[0:1, rows], ref[1:2, rows]], axis=1)


def _band_scores_t(kb, q_stack, bias_t, first):
    s = _dot(kb, q_stack, _NT) + bias_t
    if first is not None:
        key = lax.broadcasted_iota(jnp.int32, s.shape, 0)
        s = jnp.where(jnp.logical_and(first, key < BAND), NEG, s)
    return s


def _pair_select(stacked):
    return jnp.where(_head_lanes(0), stacked[0:BAND, :], stacked[BAND:, :])


def _dil_lse(qkv_v, base, bias_t, name):
    d, L = qkv_v.shape[:2]
    S = L * d
    _, chunk, nb, nchunks = _band_geometry(S, d)
    main, prev, bias, stat, _ = _band_in_specs(S, d, base)

    def body(q_ref, k_ref, kp_ref, b_ref, lse_ref, kext_ref):
        first = pl.program_id(2) == 0
        kext_ref[0:BAND, :] = kp_ref[...]
        kext_ref[BAND:, :] = k_ref[...]
        for b in range(nb):
            rows, ext = slice(b * BAND, (b + 1) * BAND), slice(b * BAND, (b + 2) * BAND)
            s = _band_scores_t(kext_ref[ext, :], _head_stack(q_ref[rows, :] * QK_SCALE), b_ref[...],
                               first if b == 0 else None)
            m = jnp.max(s, axis=0, keepdims=True)
            lse = m + jnp.log(jnp.sum(jnp.exp(s - m), axis=0, keepdims=True))
            lse_ref[0:1, rows] = lse[:, 0:BAND]
            lse_ref[1:2, rows] = lse[:, BAND:]

    return pl.pallas_call(
        body, name=name, grid=(HEAD_PAIRS, d, nchunks),
        in_specs=[main[0], main[1], prev[1], bias], out_specs=stat,
        out_shape=jax.ShapeDtypeStruct((HEAD_PAIRS, 2, S), F32),
        scratch_shapes=[pltpu.VMEM((chunk + BAND, LANES), BF16)],
        compiler_params=_params("parallel", "parallel", "parallel"),
    )(qkv_v, qkv_v, qkv_v, bias_t)


def _dil_out(qkv_v, base, bias_t, lse_joint, name):
    d, L = qkv_v.shape[:2]
    S = L * d
    _, chunk, nb, nchunks = _band_geometry(S, d)
    main, prev, bias, stat, tok = _band_in_specs(S, d, base)

    def body(q_ref, k_ref, kp_ref, v_ref, vp_ref, b_ref, lse_ref, o_ref, kext_ref, vext_ref):
        first = pl.program_id(2) == 0
        kext_ref[0:BAND, :] = kp_ref[...]
        kext_ref[BAND:, :] = k_ref[...]
        vext_ref[0:BAND, :] = vp_ref[...]
        vext_ref[BAND:, :] = v_ref[...]
        for b in range(nb):
            rows, ext = slice(b * BAND, (b + 1) * BAND), slice(b * BAND, (b + 2) * BAND)
            s = _band_scores_t(kext_ref[ext, :], _head_stack(q_ref[rows, :] * QK_SCALE), b_ref[...],
                               first if b == 0 else None)
            p_t = jnp.exp(s - _pair_rows(lse_ref, rows))
            o_ref[rows, :] = _pair_select(_dot(p_t.astype(BF16), vext_ref[ext, :], _TN)).astype(BF16)

    return pl.pallas_call(
        body, name=name, grid=(HEAD_PAIRS, d, nchunks),
        in_specs=[main[0], main[1], prev[1], main[2], prev[2], bias, stat], out_specs=tok,
        out_shape=jax.ShapeDtypeStruct((d, L, DIL_WIDTH), BF16),
        scratch_shapes=[pltpu.VMEM((chunk + BAND, LANES), BF16), pltpu.VMEM((chunk + BAND, LANES), BF16)],
        compiler_params=_params("parallel", "parallel", "parallel"),
    )(qkv_v, qkv_v, qkv_v, qkv_v, qkv_v, bias_t, lse_joint)


def _dil_bwd(qkv_v, base, do_v, bias_t, lse_joint, delta, name):
    d, L = qkv_v.shape[:2]
    S = L * d
    _, chunk, nb, nchunks = _band_geometry(S, d)
    main, prev, bias, stat, tok = _band_in_specs(S, d, base)
    nblocks = L // BAND

    def nxt_row(i):
        return jnp.minimum((i + 1) * nb, nblocks - 1)

    q_next = pl.BlockSpec((None, BAND, LANES), lambda hp, r, i: (r, nxt_row(i), base + hp))
    do_next = pl.BlockSpec((None, BAND, LANES), lambda hp, r, i: (r, nxt_row(i), hp))
    stat_next = pl.BlockSpec((None, 2, BAND), lambda hp, r, i: (hp, 0, r * nblocks + nxt_row(i)))

    def body(q_ref, k_ref, kp_ref, v_ref, vp_ref, do_ref, b_ref, lse_ref, dl_ref,
             qn_ref, don_ref, lsen_ref, dln_ref,
             dq_ref, dk_ref, dv_ref, db_ref, kext_ref, vext_ref, dkext_ref, dvext_ref):
        r, i = pl.program_id(1), pl.program_id(2)
        first = i == 0
        has_next = i + 1 < nchunks
        tail = slice(BAND + chunk, 2 * BAND + chunk)
        kext_ref[0:BAND, :] = kp_ref[...]
        kext_ref[BAND:BAND + chunk, :] = k_ref[...]
        kext_ref[tail, :] = jnp.zeros((BAND, LANES), BF16)
        vext_ref[0:BAND, :] = vp_ref[...]
        vext_ref[BAND:BAND + chunk, :] = v_ref[...]
        vext_ref[tail, :] = jnp.zeros((BAND, LANES), BF16)
        dkext_ref[...] = jnp.zeros_like(dkext_ref)
        dvext_ref[...] = jnp.zeros_like(dvext_ref)

        @pl.when(jnp.logical_and(r == 0, i == 0))
        def _():
            db_ref[...] = jnp.zeros_like(db_ref)

        def block(q2, do2, lse_row, dl_row, ext, mask_rows):
            q_stack, do_stack = _head_stack(q2), _head_stack(do2)
            s = _dot(kext_ref[ext, :], q_stack, _NT) + b_ref[...]
            if mask_rows is not None:
                s = jnp.where(mask_rows, NEG, s)
            p_t = jnp.exp(s - lse_row)
            ds_t = p_t * (_dot(vext_ref[ext, :], do_stack, _NT) - dl_row)
            ds_b = ds_t.astype(BF16)
            dkext_ref[ext, :] += _dot(ds_b, q_stack, _NN)
            dvext_ref[ext, :] += _dot(p_t.astype(BF16), do_stack, _NN)
            return ds_t, ds_b

        key = lax.broadcasted_iota(jnp.int32, (2 * BAND, 2 * BAND), 0)
        all_lanes = slice(0, BAND)
        for b in range(nb):
            rows, ext = slice(b * BAND, (b + 1) * BAND), slice(b * BAND, (b + 2) * BAND)
            mask = jnp.logical_and(first, key < BAND) if b == 0 else None
            ds_t, ds_b = block(q_ref[rows, :] * QK_SCALE, do_ref[rows, :], _pair_rows(lse_ref, rows),
                               _pair_rows(dl_ref, rows), ext, mask)
            dq_ref[rows, :] = (_pair_select(_dot(ds_b, kext_ref[ext, :], _TN)) * QK_SCALE).astype(BF16)
            db_ref[...] += ds_t
        block(qn_ref[...] * QK_SCALE, don_ref[...], _pair_rows(lsen_ref, all_lanes), _pair_rows(dln_ref, all_lanes),
              slice(chunk, chunk + 2 * BAND), jnp.logical_or(jnp.logical_not(has_next), key >= BAND))
        dk_ref[...] = dkext_ref[BAND:BAND + chunk, :].astype(BF16)
        dv_ref[...] = dvext_ref[BAND:BAND + chunk, :].astype(BF16)

    ext_rows = chunk + 2 * BAND
    return pl.pallas_call(
        body, name=name, grid=(HEAD_PAIRS, d, nchunks),
        in_specs=[main[0], main[1], prev[1], main[2], prev[2], tok, bias, stat, stat,
                  q_next, do_next, stat_next, stat_next],
        out_specs=[tok, tok, tok, bias],
        out_shape=[jax.ShapeDtypeStruct((d, L, DIL_WIDTH), BF16)] * 3
                  + [jax.ShapeDtypeStruct((HEAD_PAIRS, 2 * BAND, 2 * BAND), F32)],
        scratch_shapes=[pltpu.VMEM((ext_rows, LANES), BF16), pltpu.VMEM((ext_rows, LANES), BF16),
                        pltpu.VMEM((ext_rows, LANES), F32), pltpu.VMEM((ext_rows, LANES), F32)],
        compiler_params=_params("arbitrary", "arbitrary", "arbitrary"),
    )(qkv_v, qkv_v, qkv_v, qkv_v, qkv_v, do_v, bias_t, lse_joint, delta, qkv_v, do_v, lse_joint, delta)


def _lse_join(lse3, name):
    P, H, S = lse3.shape

    def body(l_ref, o_ref):
        a, b, c = l_ref[0], l_ref[1], l_ref[2]
        m = jnp.maximum(jnp.maximum(a, b), c)
        o_ref[...] = m + jnp.log(jnp.exp(a - m) + jnp.exp(b - m) + jnp.exp(c - m))

    return pl.pallas_call(body, name=name, out_shape=jax.ShapeDtypeStruct((H, S), F32))(lse3)


def _bucket_reduce(dbias_t, bucket_map_t, name):
    P, H = dbias_t.shape[:2]

    def body(db_ref, bk_ref, o_ref):
        p, h = pl.program_id(0), pl.program_id(1)

        @pl.when(jnp.logical_and(p == 0, h == 0))
        def _():
            o_ref[...] = jnp.zeros_like(o_ref)

        db, bk = db_ref[...], bk_ref[...]
        row = lax.broadcasted_iota(jnp.int32, (N_BUCKETS, LANES), 0)
        lane = lax.broadcasted_iota(jnp.int32, (N_BUCKETS, LANES), 1)

        def one(b, acc):
            val = jnp.sum(jnp.sum(jnp.where(bk == b, db, 0.0), axis=1, keepdims=True), axis=0, keepdims=True)
            return acc + jnp.where(jnp.logical_and(row == b, lane == h), val, 0.0)

        acc = jnp.zeros((N_BUCKETS, LANES), F32)
        for b in range(N_BUCKETS):
            acc = one(b, acc)
        o_ref[...] += acc

    return pl.pallas_call(
        body, name=name, grid=(P, H),
        in_specs=[pl.BlockSpec((None, None, 2 * BAND, BAND), lambda p, h: (p, h, 0, 0)),
                  pl.BlockSpec((None, 2 * BAND, BAND), lambda p, h: (p, 0, 0))],
        out_specs=pl.BlockSpec((N_BUCKETS, LANES), lambda p, h: (0, 0)),
        out_shape=jax.ShapeDtypeStruct((N_BUCKETS, LANES), F32),
        compiler_params=_params("arbitrary", "arbitrary"),
    )(dbias_t, bucket_map_t)


def _mem_fwd(q, kv, name, tq=1024):
    S, W = q.shape
    N = kv.shape[0]
    pairs = W // LANES
    tq = _fit(S, tq)

    def body(q_ref, k_ref, v_ref, o_ref, lse_ref):
        for a in range(2):
            lanes = slice(a * HEAD_DIM, (a + 1) * HEAD_DIM)
            s = _dot(k_ref[:, lanes], q_ref[:, lanes] * QK_SCALE, _NT)
            m = jnp.max(s, axis=0, keepdims=True)
            e = jnp.exp(s - m)
            l = jnp.sum(e, axis=0, keepdims=True)
            o_ref[:, lanes] = _dot((e / l).astype(BF16), v_ref[:, lanes], _TN).astype(BF16)
            lse_ref[a:a + 1, :] = m + jnp.log(l)

    return pl.pallas_call(
        body, name=name, grid=(pairs, S // tq),
        in_specs=[pl.BlockSpec((tq, LANES), lambda hp, i: (i, hp)),
                  pl.BlockSpec((N, LANES), lambda hp, i: (0, hp)),
                  pl.BlockSpec((N, LANES), lambda hp, i: (0, pairs + hp))],
        out_specs=[pl.BlockSpec((tq, LANES), lambda hp, i: (i, hp)),
                   pl.BlockSpec((None, 2, tq), lambda hp, i: (hp, 0, i))],
        out_shape=[jax.ShapeDtypeStruct((S, W), BF16), jax.ShapeDtypeStruct((pairs, 2, S), F32)],
        compiler_params=_params("parallel", "parallel"),
    )(q, kv, kv)


def _mem_bwd(q, kv, do, lse, delta, name, tq=1024):
    S, W = q.shape
    N = kv.shape[0]
    pairs = W // LANES
    tq = _fit(S, tq)

    def body(q_ref, k_ref, v_ref, do_ref, lse_ref, dl_ref, dq_ref, dk_ref, dv_ref):
        i = pl.program_id(1)

        @pl.when(i == 0)
        def _():
            dk_ref[...] = jnp.zeros_like(dk_ref)
            dv_ref[...] = jnp.zeros_like(dv_ref)

        for a in range(2):
            lanes = slice(a * HEAD_DIM, (a + 1) * HEAD_DIM)
            qv, dov = q_ref[:, lanes] * QK_SCALE, do_ref[:, lanes]
            kv_, vv = k_ref[:, lanes], v_ref[:, lanes]
            p_t = jnp.exp(_dot(kv_, qv, _NT) - lse_ref[a:a + 1, :])
            ds_t = p_t * (_dot(vv, dov, _NT) - dl_ref[a:a + 1, :])
            ds_b = ds_t.astype(BF16)
            dq_ref[:, lanes] = (_dot(ds_b, kv_, _TN) * QK_SCALE).astype(BF16)
            dk_ref[:, lanes] += _dot(ds_b, qv, _NN)
            dv_ref[:, lanes] += _dot(p_t.astype(BF16), dov, _NN)

    qs = pl.BlockSpec((tq, LANES), lambda hp, i: (i, hp))
    stat = pl.BlockSpec((None, 2, tq), lambda hp, i: (hp, 0, i))
    acc = pl.BlockSpec((N, LANES), lambda hp, i: (0, hp))
    return pl.pallas_call(
        body, name=name, grid=(pairs, S // tq),
        in_specs=[qs, acc, pl.BlockSpec((N, LANES), lambda hp, i: (0, pairs + hp)), qs, stat, stat],
        out_specs=[qs, acc, acc],
        out_shape=[jax.ShapeDtypeStruct((S, W), BF16), jax.ShapeDtypeStruct((N, W), F32),
                   jax.ShapeDtypeStruct((N, W), F32)],
        compiler_params=_params("parallel", "arbitrary"),
    )(q, kv, kv, do, lse, delta)


def _head_rowdot(a, bs, name, tr=512):
    S, W = a.shape
    tr = _fit(S, tr)

    def body(*refs):
        a_ref, b_refs, o_ref, tmp_ref = refs[0], refs[1:-2], refs[-2], refs[-1]
        col = lax.broadcasted_iota(jnp.int32, (LANES, LANES), 0)
        lane = lax.broadcasted_iota(jnp.int32, (LANES, LANES), 1)
        acc = jnp.zeros((tr, LANES), F32)
        for j in range(W // LANES):
            cols = slice(j * LANES, (j + 1) * LANES)
            tot = _token_rows(b_refs[0], cols, tmp_ref)
            for r in b_refs[1:]:
                tot = tot + _token_rows(r, cols, tmp_ref)
            sel = jnp.where(col // HEAD_DIM + j * (LANES // HEAD_DIM) == lane, 1.0, 0.0).astype(F32)
            acc = acc + lax.dot_general(a_ref[:, cols].astype(F32) * tot, sel, (_NN, ((), ())),
                                        precision=lax.Precision.HIGHEST, preferred_element_type=F32)
        o_ref[...] = acc

    return pl.pallas_call(
        body, name=name, grid=(S // tr,), in_specs=[_row_spec(t, tr) for t in [a] + list(bs)],
        out_specs=pl.BlockSpec((tr, LANES), lambda i: (i, 0)),
        out_shape=jax.ShapeDtypeStruct((S, LANES), F32),
        scratch_shapes=[pltpu.VMEM((tr, LANES), F32)],
        compiler_params=_params("parallel"),
    )(a, *bs)


def _sum_cast_cols(groups, out_dtype, name, tail=None, tr=256):
    first = groups[0][0]
    S, W = (first.shape if first.ndim == 2 else (first.shape[0] * first.shape[1], first.shape[2]))
    tr = _fit(S, tr)
    flat = [t for g in groups for t in g] + ([tail] if tail is not None else [])
    tail_w = 0 if tail is None else tail.shape[1]

    def body(*refs):
        o_ref, tmp_ref = refs[-2], refs[-1]
        if tail is not None:
            o_ref[:, W * len(groups):] = refs[-3][...].astype(out_dtype)
        k = 0
        for gi, g in enumerate(groups):
            for j in range(W // LANES):
                cols = slice(j * LANES, (j + 1) * LANES)
                acc = _token_rows(refs[k], cols, tmp_ref)
                for r in refs[k + 1:k + len(g)]:
                    acc = acc + _token_rows(r, cols, tmp_ref)
                o_ref[:, gi * W + j * LANES:gi * W + (j + 1) * LANES] = acc.astype(out_dtype)
            k += len(g)

    return pl.pallas_call(
        body, name=name, grid=(S // tr,), in_specs=[_row_spec(t, tr) for t in flat],
        out_specs=pl.BlockSpec((tr, W * len(groups) + tail_w), lambda i: (i, 0)),
        out_shape=jax.ShapeDtypeStruct((S, W * len(groups) + tail_w), out_dtype),
        scratch_shapes=[pltpu.VMEM((tr, LANES), F32)],
        compiler_params=_params("parallel"),
    )(*flat)


FF_TILE = 256


def _ffn_up(h, w_gu, name, tm=4096):
    S, D = h.shape
    F2 = w_gu.shape[1]
    tm = _fit(S, tm)

    def body(h_ref, w_ref, gu_ref, act_ref):
        gu = _dot(h_ref[...], w_ref[...], _NN)
        gu_ref[...] = gu.astype(BF16)
        g, u = gu[:, :FF_TILE], gu[:, FF_TILE:]
        act_ref[...] = (g * (1.0 / (1.0 + jnp.exp(-g))) * u).astype(BF16)

    return pl.pallas_call(
        body, name=name, grid=(S // tm, F2 // (2 * FF_TILE)),
        in_specs=[pl.BlockSpec((tm, D), lambda i, j: (i, 0)), pl.BlockSpec((D, 2 * FF_TILE), lambda i, j: (0, j))],
        out_specs=[pl.BlockSpec((tm, 2 * FF_TILE), lambda i, j: (i, j)),
                   pl.BlockSpec((tm, FF_TILE), lambda i, j: (i, j))],
        out_shape=[jax.ShapeDtypeStruct((S, F2), BF16), jax.ShapeDtypeStruct((S, F2 // 2), BF16)],
        compiler_params=_params("parallel", "arbitrary"),
    )(h, w_gu)


def _ffn_dact(dy, w_down, gu, name, tm=4096):
    S, D = dy.shape
    F2 = gu.shape[1]
    tm = _fit(S, tm)

    def body(dy_ref, w_ref, gu_ref, dgu_ref):
        dact = _dot(dy_ref[...], w_ref[...], _NT)
        gu_v = gu_ref[...].astype(F32)
        g, u = gu_v[:, :FF_TILE], gu_v[:, FF_TILE:]
        sig = 1.0 / (1.0 + jnp.exp(-g))
        silu = g * sig
        dgu_ref[:, :FF_TILE] = (dact * u * (sig + silu * (1.0 - sig))).astype(BF16)
        dgu_ref[:, FF_TILE:] = (dact * silu).astype(BF16)

    return pl.pallas_call(
        body, name=name, grid=(S // tm, F2 // (2 * FF_TILE)),
        in_specs=[pl.BlockSpec((tm, D), lambda i, j: (i, 0)), pl.BlockSpec((FF_TILE, D), lambda i, j: (j, 0)),
                  pl.BlockSpec((tm, 2 * FF_TILE), lambda i, j: (i, j))],
        out_specs=pl.BlockSpec((tm, 2 * FF_TILE), lambda i, j: (i, j)),
        out_shape=jax.ShapeDtypeStruct((S, F2), BF16),
        compiler_params=_params("parallel", "arbitrary"),
    )(dy, w_down, gu)


def _fit_rows(n, cap):
    if n <= cap:
        return n
    t = (cap // 8) * 8
    while t >= 8:
        if n % t == 0:
            return t
        t -= 8
    raise ValueError(f"no sublane-aligned tile for {n} under {cap}")


def _add_n(arrs, name, tr=512):
    R, C = arrs[0].shape
    tr = _fit_rows(R, tr)

    def body(*refs):
        acc = refs[0][...].astype(F32)
        for r in refs[1:-1]:
            acc = acc + r[...].astype(F32)
        refs[-1][...] = acc

    row = pl.BlockSpec((tr, C), lambda i: (i, 0))
    return pl.pallas_call(
        body, name=name, grid=(R // tr,), in_specs=[row] * len(arrs), out_specs=row,
        out_shape=jax.ShapeDtypeStruct((R, C), F32), compiler_params=_params("parallel"),
    )(*arrs)


def _adamw(w, g, m, v, name, tr=512):
    R, C = w.shape
    tr = _fit_rows(R, tr)
    c1 = 1.0 / (1.0 - ADAM_B1 ** ADAM_STEP)
    c2 = 1.0 / (1.0 - ADAM_B2 ** ADAM_STEP)

    def body(w_ref, g_ref, m_ref, v_ref, d_ref, nm_ref, nv_ref):
        gv = g_ref[...]
        nm = ADAM_B1 * m_ref[...] + (1.0 - ADAM_B1) * gv
        nv = ADAM_B2 * v_ref[...] + (1.0 - ADAM_B2) * (gv * gv)
        nm_ref[...] = nm
        nv_ref[...] = nv
        d_ref[...] = -ADAM_LR * ((nm * c1) / (jnp.sqrt(nv * c2) + ADAM_EPS) + ADAM_WD * w_ref[...])

    row = pl.BlockSpec((tr, C), lambda i: (i, 0))
    return pl.pallas_call(
        body, name=name, grid=(R // tr,), in_specs=[row] * 4, out_specs=[row] * 3,
        out_shape=[jax.ShapeDtypeStruct((R, C), F32)] * 3, compiler_params=_params("parallel"),
    )(w, g, m, v)


def _place():
    return lax.axis_index("x"), lax.axis_index("y"), lax.axis_index("c")


_ANY = pl.BlockSpec(memory_space=pl.ANY)


def _chip_all_gather(shard, name):
    R, C = shard.shape
    half = R // 2

    def body(x_ref, out_ref, send_sems, recv_sems, local_sem):
        x, y, c = _place()
        chips = [(1 - x, y), (x, 1 - y), (1 - x, 1 - y)]
        sibling = (x, y, 1 - c)
        mine = pltpu.make_async_copy(x_ref, out_ref.at[2 * x + y], local_sem)
        mine.start()

        def rows(chip, core):
            return out_ref.at[chip, pl.ds(core * half, half)]

        def copy(k, chip, core, to, src=None):
            return pltpu.make_async_remote_copy(
                src_ref=rows(chip, core) if src is None else src, dst_ref=rows(chip, core),
                send_sem=send_sems.at[k], recv_sem=recv_sems.at[k], device_id=to, device_id_type=MESH_IDS)

        me = 2 * x + y
        first = [copy(k, me, c, (cx, cy, c), src=x_ref.at[pl.ds(c * half, half)]) for k, (cx, cy) in enumerate(chips)]
        for cp in first:
            cp.start()
        passed = [copy(3 + k, 2 * cx + cy, c, sibling) for k, (cx, cy) in enumerate(chips)]
        for k, (cx, cy) in enumerate(chips):
            copy(k, 2 * cx + cy, c, (cx, cy, c)).wait_recv()
            passed[k].start()
        for k, (cx, cy) in enumerate(chips):
            copy(3 + k, 2 * cx + cy, 1 - c, sibling).wait_recv()
        for cp in first + passed:
            cp.wait_send()
        mine.wait()

    return pl.pallas_call(
        body, name=name, in_specs=[_ANY], out_specs=_ANY,
        out_shape=jax.ShapeDtypeStruct((N_CHIPS, R, C), shard.dtype),
        scratch_shapes=[pltpu.SemaphoreType.DMA((6,)), pltpu.SemaphoreType.DMA((6,)), pltpu.SemaphoreType.DMA],
    )(shard)


def _sibling_exchange(buf, name):
    def body(x_ref, out_ref, send_sem, recv_sem):
        x, y, c = _place()
        cp = pltpu.make_async_remote_copy(
            src_ref=x_ref, dst_ref=out_ref, send_sem=send_sem, recv_sem=recv_sem,
            device_id=(x, y, 1 - c), device_id_type=MESH_IDS)
        cp.start()
        cp.wait()

    return pl.pallas_call(
        body, name=name, in_specs=[_ANY], out_specs=_ANY,
        out_shape=jax.ShapeDtypeStruct(buf.shape, buf.dtype),
        scratch_shapes=[pltpu.SemaphoreType.DMA, pltpu.SemaphoreType.DMA],
    )(buf)


def _chip_scatter(parts, name):
    _, R, C = parts.shape

    def body(p_ref, out_ref, send_sems, recv_sems):
        x, y, c = _place()
        chips = [(1 - x, y), (x, 1 - y), (1 - x, 1 - y)]

        def copy(k, slab, to):
            return pltpu.make_async_remote_copy(
                src_ref=p_ref.at[slab], dst_ref=out_ref.at[k], send_sem=send_sems.at[k], recv_sem=recv_sems.at[k],
                device_id=to, device_id_type=MESH_IDS)

        sends = [copy(k, 2 * cx + cy, (cx, cy, c)) for k, (cx, cy) in enumerate(chips)]
        for cp in sends:
            cp.start()
        for cp in sends:
            cp.wait_recv()
        for cp in sends:
            cp.wait_send()

    return pl.pallas_call(
        body, name=name, in_specs=[_ANY], out_specs=_ANY,
        out_shape=jax.ShapeDtypeStruct((3, R, C), parts.dtype),
        scratch_shapes=[pltpu.SemaphoreType.DMA((3,)), pltpu.SemaphoreType.DMA((3,))],
    )(parts)


def _all_to_all_small(vec, name):
    R, C = vec.shape

    def body(v_ref, out_ref, send_sems, recv_sems, local_sem):
        x, y, c = _place()
        me = 4 * x + 2 * y + c
        mine = pltpu.make_async_copy(v_ref, out_ref.at[me], local_sem)
        mine.start()
        flips = [(dx, dy, dc) for dx in (0, 1) for dy in (0, 1) for dc in (0, 1)][1:]

        def peer(f):
            return (x ^ f[0], y ^ f[1], c ^ f[2])

        def copy(k, slot, to):
            return pltpu.make_async_remote_copy(
                src_ref=v_ref, dst_ref=out_ref.at[slot], send_sem=send_sems.at[k], recv_sem=recv_sems.at[k],
                device_id=to, device_id_type=MESH_IDS)

        sends = [copy(k, me, peer(f)) for k, f in enumerate(flips)]
        for cp in sends:
            cp.start()
        for k, f in enumerate(flips):
            px, py, pc = peer(f)
            copy(k, 4 * px + 2 * py + pc, peer(f)).wait_recv()
        for cp in sends:
            cp.wait_send()
        mine.wait()

    return pl.pallas_call(
        body, name=name, in_specs=[_ANY], out_specs=_ANY,
        out_shape=jax.ShapeDtypeStruct((8, R, C), vec.dtype),
        scratch_shapes=[pltpu.SemaphoreType.DMA((7,)), pltpu.SemaphoreType.DMA((7,)), pltpu.SemaphoreType.DMA],
    )(vec)


def _to_heads(t, n):
    S = t.shape[0]
    return t.reshape(S, n, HEAD_DIM).transpose(1, 0, 2)


def _to_heads_t(t, n):
    S = t.shape[0]
    return t.T.reshape(n, HEAD_DIM, S)


def _from_heads_t(t):
    H, Dh, S = t.shape
    return t.reshape(H * Dh, S).T


def _t5_bucket(dist):
    max_exact = N_BUCKETS // 2
    d = np.maximum(dist, 1).astype(np.float32)
    large = max_exact + (np.log(d / max_exact) / np.log(MAX_DISTANCE / max_exact)
                         * (N_BUCKETS - max_exact)).astype(np.int32)
    large = np.minimum(large, N_BUCKETS - 1)
    return np.where(dist < max_exact, dist, large).astype(np.int32)


def _band_tables():
    qi = np.arange(BAND)[:, None]
    kj = np.arange(2 * BAND)[None, :]
    sub = qi + BAND - kj
    band = (sub >= 0) & (sub <= BAND)
    out = []
    for d in DILATIONS:
        bucket = _t5_bucket(np.clip(sub, 0, BAND) * d)
        out.append(np.where(band, bucket, -1).astype(np.int32))
    return np.stack(out)


_PACK = (("w_in", 770), ("w_out", 256), ("w_xq", 64), ("w_xk", 64), ("w_xv", 64), ("w_xo", 64),
         ("w_gate", 704), ("w_up", 704), ("w_down", 704))


def _pack(shards):
    rows = [shards[n].reshape(-1, PACK_COLS) for n, _ in _PACK]
    total = sum(r.shape[0] for r in rows)
    pad = (-total) % 128
    if pad:
        rows.append(jnp.zeros((pad, PACK_COLS), rows[0].dtype))
    return jnp.concatenate(rows, axis=0)


def _unpack(pack, shapes):
    out, r = {}, 0
    for n, _ in _PACK:
        cnt = int(np.prod(shapes[n])) // PACK_COLS
        out[n] = pack[r:r + cnt].reshape(shapes[n])
        r += cnt
    return out


_COL_SHARDED = ("w_in", "w_xo", "w_gate", "w_up")


def _full_weight(gathered, name):
    return jnp.concatenate(gathered, axis=1 if name in _COL_SHARDED else 0)


def _split_weight(full, name):
    return jnp.split(full, N_CHIPS, axis=1 if name in _COL_SHARDED else 0)


_SMALL = ("g_mix_pre", "g_mix_post", "g_xattn_pre", "g_mem", "g_xattn_post", "g_ffn_pre", "g_ffn_post")


def _pack_small(vals):
    D = vals["g_mix_pre"].shape[1]
    rows = [vals[n].reshape(1, D) for n in _SMALL]
    misc = jnp.concatenate([vals["b_f"].reshape(-1), vals["rel_bias"].reshape(-1)])
    rows.append(jnp.pad(misc, (0, D - misc.shape[0])).reshape(1, D))
    rows.append(jnp.zeros((16 - len(rows), D), F32))
    return jnp.concatenate(rows, axis=0)


def _unpack_small(pack):
    out = {n: pack[i:i + 1] for i, n in enumerate(_SMALL)}
    out["b_f"] = pack[7, 0:N_FOX_HEADS].reshape(1, N_FOX_HEADS)
    out["rel_bias"] = pack[7, N_FOX_HEADS:N_FOX_HEADS + N_BUCKETS * N_DIL_HEADS].reshape(N_BUCKETS, N_DIL_HEADS)
    return out


def kernel(x, mem, g_mix_pre, w_in, b_f, rel_bias, w_out, g_mix_post, g_xattn_pre, g_mem, w_xq, w_xk, w_xv, w_xo, g_xattn_post, g_ffn_pre, w_gate, w_up, w_down, g_ffn_post, loss_target, m_g_mix_pre, m_w_in, m_b_f, m_rel_bias, m_w_out, m_g_mix_post, m_g_xattn_pre, m_g_mem, m_w_xq, m_w_xk, m_w_xv, m_w_xo, m_g_xattn_post, m_g_ffn_pre, m_w_gate, m_w_up, m_w_down, m_g_ffn_post, v_g_mix_pre, v_w_in, v_b_f, v_rel_bias, v_w_out, v_g_mix_post, v_g_xattn_pre, v_g_mem, v_w_xq, v_w_xk, v_w_xv, v_w_xo, v_g_xattn_post, v_g_ffn_pre, v_w_gate, v_w_up, v_w_down, v_g_ffn_post):
    args = dict(locals())
    big = [n for n, _ in _PACK]
    names = ["g_mix_pre", "w_in", "b_f", "rel_bias", "w_out", "g_mix_post", "g_xattn_pre", "g_mem", "w_xq",
             "w_xk", "w_xv", "w_xo", "g_xattn_post", "g_ffn_pre", "w_gate", "w_up", "w_down", "g_ffn_post"]
    xs = x[0]
    S, D = xs.shape
    assert S % (BAND * DILATIONS[-1]) == 0
    shard_shapes = {n: args[n].shape[1:] for n in big}
    my_x, my_y, my_c = lax.axis_index("x"), lax.axis_index("y"), lax.axis_index("c")

    gathered = _chip_all_gather(_pack({n: args[n][0].astype(BF16) for n in big}), "weights_all_gather")
    per_chip = [_unpack(gathered[j], shard_shapes) for j in range(N_CHIPS)]
    W = {n: _full_weight([pc[n] for pc in per_chip], n) for n in big}
    w_fox, w_fg, w_dil = (W["w_in"][:, :3 * FOX_WIDTH], W["w_in"][:, 3 * FOX_WIDTH:3 * FOX_WIDTH + N_FOX_HEADS],
                          W["w_in"][:, 3 * FOX_WIDTH + N_FOX_HEADS:])
    w_qkv = jnp.concatenate([w_fox, w_dil], axis=1)
    w_fg_pad = jnp.pad(w_fg, ((0, 0), (0, LANES - N_FOX_HEADS)))
    F = W["w_gate"].shape[1]
    nft = F // FF_TILE
    w_gu = jnp.stack([W["w_gate"].reshape(D, nft, FF_TILE), W["w_up"].reshape(D, nft, FF_TILE)],
                     axis=2).reshape(D, 2 * F)

    h1 = _rms_fwd(xs, g_mix_pre, "rms_mix_pre")
    qkv = _mm(h1, w_qkv, "nn", BF16, "proj_qkv", tm=2048)
    fg = _mm(h1, w_fg_pad, "nn", F32, "proj_gate")
    fg_t = fg[:, :N_FOX_HEADS].T
    b_col = b_f.reshape(N_FOX_HEADS, 1)
    c_t = _forget_fwd(fg_t, b_col, "forget_cumsum")
    fq_s, fk_s, fv_s = (qkv[:, i * FOX_WIDTH:(i + 1) * FOX_WIDTH] for i in range(3))
    fqt, fvt = _to_heads_t(fq_s, N_FOX_HEADS), _to_heads_t(fv_s, N_FOX_HEADS)
    unit = jnp.full((N_FOX_HEADS, S), 1.0, BF16)
    inv_scale = jnp.full((N_FOX_HEADS, S), 1.0 / QK_SCALE, BF16)
    ka = _lanes_operand(_to_heads(fk_s, N_FOX_HEADS), list(_split3(-c_t)) + [unit] * EXTRA)
    qa_f = _rows_operand(fqt, [inv_scale] * EXTRA)
    o_fox_t, lse_fox = _fox_fwd(qa_f, ka, _with_ones(fvt), "fox_fwd")

    bucket_map = _band_tables()
    onehot = (jnp.asarray(bucket_map)[..., None] == jnp.arange(N_BUCKETS)).astype(F32)
    bias_tab = jnp.einsum("pqkb,bh->phkq", onehot, rel_bias, precision=lax.Precision.HIGHEST)
    bias_tab = jnp.where(jnp.asarray(bucket_map.transpose(0, 2, 1) >= 0)[:, None], bias_tab, NEG)
    bias_t = bias_tab.reshape(3, HEAD_PAIRS, 2, 2 * BAND, BAND).transpose(0, 1, 3, 2, 4).reshape(
        3, HEAD_PAIRS, 2 * BAND, 2 * BAND)
    views = [(qkv.reshape(1, S, qkv.shape[1]), DIL_Q_BLOCK)] + [
        (_to_residues(qkv, 1, 3 * DIL_WIDTH, d, f"dilated_qkv_residues_{d}"), 0) for d in DILATIONS[1:]]

    def to_tok(stat, d):
        return stat.reshape(N_DIL_HEADS, d, S // d).swapaxes(1, 2).reshape(N_DIL_HEADS, S)

    def to_perm(stat, d):
        return stat.reshape(N_DIL_HEADS, S // d, d).swapaxes(1, 2).reshape(HEAD_PAIRS, 2, S)

    def tok_or_res(t):
        return t.reshape(t.shape[1:]) if t.shape[0] == 1 else t

    lse_tok = jnp.stack([to_tok(_dil_lse(*views[p], bias_t[p], f"dilated_lse_{d}"), d)
                         for p, d in enumerate(DILATIONS)])
    lse_joint = _lse_join(lse_tok, "dilated_lse_join")
    lse_perm = [to_perm(lse_joint, d) for d in DILATIONS]
    o_dil = [tok_or_res(_dil_out(*views[p], bias_t[p], lse_perm[p], f"dilated_out_{d}"))
             for p, d in enumerate(DILATIONS)]
    o_cat = _sum_cast_cols([[_from_heads_t(o_fox_t)]] + [[o] for o in o_dil], BF16, "mixer_out_cat")
    w_out_b = W["w_out"]
    w_out_cat = jnp.concatenate([w_out_b[:FOX_WIDTH]] + [w_out_b[FOX_WIDTH:]] * 3, axis=0)
    a = _mm(o_cat, w_out_cat, "nn", F32, "proj_out", tm=2048, tk=2048)
    x1, h2 = _resid_norm(xs, a, g_mix_post, g_xattn_pre, "resid_mix")

    hm = _rms_fwd(mem[0], g_mem, "rms_mem")
    q2 = _mm(h2, W["w_xq"], "nn", BF16, "xattn_q")
    w_xkv = jnp.concatenate([W["w_xk"], W["w_xv"]], axis=1)
    kvm = _mm(hm, w_xkv, "nn", BF16, "xattn_kv")
    MW = N_MEM_HEADS * HEAD_DIM
    oc, lse_mem = _mem_fwd(q2, kvm, "xattn_fwd")
    y2 = _mm(oc, W["w_xo"], "nn", F32, "xattn_o")
    x2, h3 = _resid_norm(x1, y2, g_xattn_post, g_ffn_pre, "resid_xattn")

    gu, act = _ffn_up(h3, w_gu, "ffn_up")
    y3 = _mm(act, W["w_down"], "nn", F32, "ffn_down", tk=2816)
    dx3, loss_tile = _final_loss(x2, y3, g_ffn_post, loss_target[0], "final_loss")

    grads = {}
    small = {}
    _, dy3_b, dg = _rms_bwd(y3, g_ffn_post, dx3, None, "bwd_norm_ffn_post", want=("bf16",))
    small["g_ffn_post"] = dg[0:1]
    grads["w_down"] = _mm(act, dy3_b, "tn", F32, "grad_w_down", tm=1408)
    dgu = _ffn_dact(dy3_b, W["w_down"], gu, "ffn_dact")
    dw_gu = _mm(h3, dgu, "tn", F32, "grad_w_gu", tn=1408).reshape(D, nft, 2, FF_TILE)
    grads["w_gate"], grads["w_up"] = dw_gu[:, :, 0].reshape(D, F), dw_gu[:, :, 1].reshape(D, F)
    dh3 = _mm(dgu, w_gu, "nt", F32, "bwd_ffn_in", tm=2048, tk=1408)
    dx2, _, dg = _rms_bwd(x2, g_ffn_pre, dh3, dx3, "bwd_norm_ffn_pre", want=("f32",))
    small["g_ffn_pre"] = dg[0:1]

    _, dy2_b, dg = _rms_bwd(y2, g_xattn_post, dx2, None, "bwd_norm_xattn_post", want=("bf16",))
    small["g_xattn_post"] = dg[0:1]
    grads["w_xo"] = _mm(oc, dy2_b, "tn", F32, "grad_w_xo")
    doc = _mm(dy2_b, W["w_xo"], "nt", BF16, "bwd_xattn_o")
    delta_mem = _head_rowdot(doc, [oc], "xattn_delta")[:, :N_MEM_HEADS].T.reshape(N_MEM_HEADS // 2, 2, S)
    dq2, dkm, dvm = _mem_bwd(q2, kvm, doc, lse_mem, delta_mem, "xattn_bwd")
    dkvm = jnp.concatenate([dkm, dvm], axis=1).astype(BF16)
    grads["w_xq"] = _mm(h2, dq2, "tn", F32, "grad_w_xq")
    dw_xkv = _mm(hm, dkvm, "tn", F32, "grad_w_xkv")
    grads["w_xk"], grads["w_xv"] = dw_xkv[:, :MW], dw_xkv[:, MW:]
    dhm = _mm(dkvm, w_xkv, "nt", F32, "bwd_xattn_kv")
    _, _, dg = _rms_bwd(mem[0], g_mem, dhm, None, "bwd_norm_mem", want=())
    small["g_mem"] = dg[0:1]
    dh2 = _mm(dq2, W["w_xq"], "nt", F32, "bwd_xattn_q")
    dx1, _, dg = _rms_bwd(x1, g_xattn_pre, dh2, dx2, "bwd_norm_xattn_pre", want=("f32",))
    small["g_xattn_pre"] = dg[0:1]

    _, da_b, dg = _rms_bwd(a, g_mix_post, dx1, None, "bwd_norm_mix_post", want=("bf16",))
    small["g_mix_post"] = dg[0:1]
    dw_out_cat = _mm(o_cat, da_b, "tn", F32, "grad_w_out")
    dw_out_dil = _add_n([dw_out_cat[FOX_WIDTH + p * DIL_WIDTH:FOX_WIDTH + (p + 1) * DIL_WIDTH] for p in range(3)],
                        "grad_w_out_dil")
    grads["w_out"] = jnp.concatenate([dw_out_cat[:FOX_WIDTH], dw_out_dil], axis=0)
    do = _mm(da_b, w_out_b, "nt", BF16, "bwd_proj_out")
    do_fox, do_dil = do[:, :FOX_WIDTH], do[:, FOX_WIDTH:]

    delta_fox = _head_rowdot(do_fox, [o_cat[:, :FOX_WIDTH]], "fox_delta")[:, :N_FOX_HEADS].T
    qa_b = lax.dynamic_update_slice(qa_f, jnp.stack(_split3(lse_fox[:, 0] * (-1.0 / QK_SCALE)), axis=1),
                                    (0, HEAD_DIM + EXTRA, 0))
    va = _lanes_operand(_to_heads(fv_s, N_FOX_HEADS), [unit] * EXTRA)
    doa = _rows_operand(_to_heads_t(do_fox, N_FOX_HEADS), list(_split3(-delta_fox)))
    dq_aug, dk_aug, dvf = _fox_bwd(qa_b, ka, ka.transpose(0, 2, 1), va, doa, "fox_bwd")
    dqf, dkf = dq_aug[:, :HEAD_DIM], dk_aug[:, :HEAD_DIM]
    dfg_t, db_f = _forget_bwd(fg_t, b_col, dq_aug[:, HEAD_DIM + EXTRA], dk_aug[:, HEAD_DIM], "forget_bwd")

    delta_dil = _head_rowdot(do_dil, o_dil, "dilated_delta")[:, :N_DIL_HEADS].T
    do_res = [do_dil.reshape(1, S, DIL_WIDTH)] + [
        _to_residues(do, 1, DIL_WIDTH, d, f"dilated_do_residues_{d}") for d in DILATIONS[1:]]
    dil_grads = [_dil_bwd(*views[p], do_res[p], bias_t[p], lse_perm[p], to_perm(delta_dil, d), f"dilated_bwd_{d}")
                 for p, d in enumerate(DILATIONS)]
    dbias_t = jnp.stack([g[3].reshape(HEAD_PAIRS, 2 * BAND, 2, BAND).transpose(0, 2, 1, 3).reshape(
        N_DIL_HEADS, 2 * BAND, BAND) for g in dil_grads])
    d_rel = _bucket_reduce(dbias_t, jnp.asarray(bucket_map.transpose(0, 2, 1)), "rel_bias_grad")[:, :N_DIL_HEADS]
    dfg_pad = jnp.pad(dfg_t.T, ((0, 0), (0, LANES - N_FOX_HEADS))).astype(BF16)
    dcat = _sum_cast_cols([[_from_heads_t(dqf)], [_from_heads_t(dkf)], [_from_heads_t(dvf)]]
                          + [[tok_or_res(g[j]) for g in dil_grads] for j in range(3)],
                          BF16, "dqkv_assemble", tail=dfg_pad)
    dw_cat = _mm(h1, dcat, "tn", F32, "grad_w_qkv", tm=512, tn=3200)
    n_qkv = 3 * (FOX_WIDTH + DIL_WIDTH)
    grads["w_in"] = jnp.concatenate([dw_cat[:, :3 * FOX_WIDTH], dw_cat[:, n_qkv:n_qkv + N_FOX_HEADS],
                                     dw_cat[:, 3 * FOX_WIDTH:n_qkv]], axis=1)
    w_cat = jnp.concatenate([w_qkv, w_fg_pad], axis=1)
    dh1 = _mm(dcat, w_cat, "nt", F32, "bwd_proj_in", tk=3200)
    grad_x, _, dg = _rms_bwd(xs, g_mix_pre, dh1, dx1, "bwd_norm_mix_pre", want=("f32",))
    small["g_mix_pre"] = dg[0:1]
    small["b_f"] = db_f[:, 0].reshape(1, N_FOX_HEADS)
    small["rel_bias"] = d_rel

    split = {n: _split_weight(grads[n], n) for n in big}
    parts = jnp.stack([_pack({n: split[n][j].astype(BF16) for n in big}) for j in range(N_CHIPS)])
    R = parts.shape[1]
    half = R // 2
    keep = lax.dynamic_slice_in_dim(parts, my_c * half, half, axis=1)
    give = lax.dynamic_slice_in_dim(parts, (1 - my_c) * half, half, axis=1)
    got = _sibling_exchange(give, "grads_to_sibling")
    chip_sum = _add_n([keep.reshape(-1, PACK_COLS), got.reshape(-1, PACK_COLS)], "grads_add_sibling")
    chip_sum = chip_sum.reshape(N_CHIPS, half, PACK_COLS)
    my_chip = 2 * my_x + my_y
    from_chips = _chip_scatter(chip_sum.astype(BF16), "grads_to_chips")
    own = lax.dynamic_index_in_dim(chip_sum, my_chip, axis=0, keepdims=False)
    g_half = _add_n([own, from_chips[0], from_chips[1], from_chips[2]], "grads_add_chips")
    other_half = _sibling_exchange(g_half, "grads_share_sibling")
    g_pack = jnp.where(my_c == 0, jnp.concatenate([g_half, other_half]), jnp.concatenate([other_half, g_half]))

    small_pack = _pack_small(small)
    small_pack = small_pack.at[8, 0].set(loss_tile[0, 0])
    everyone = _all_to_all_small(small_pack, "small_all_gather")
    small_sum = _add_n([everyone[i] for i in range(8)], "small_sum")
    loss = small_sum[8, 0]
    g_small = _unpack_small(small_sum)

    outs = {"grad": _unpack(g_pack, shard_shapes), "delta": {}, "new_m": {}, "new_v": {}}
    for n in big:
        outs["delta"][n], outs["new_m"][n], outs["new_v"][n] = _adamw(
            args[n][0], outs["grad"][n], args["m_" + n][0], args["v_" + n][0], f"adamw_{n}")
    sw = _pack_small({n: args[n] for n in _SMALL + ("b_f", "rel_bias")})
    sm = _pack_small({n: args["m_" + n] for n in _SMALL + ("b_f", "rel_bias")})
    sv = _pack_small({n: args["v_" + n] for n in _SMALL + ("b_f", "rel_bias")})
    sd, snm, snv = _adamw(sw, small_sum.at[8, 0].set(0.0), sm, sv, "adamw_small")
    souts = {"grad": g_small, "delta": _unpack_small(sd), "new_m": _unpack_small(snm), "new_v": _unpack_small(snv)}

    def leaf(kind, n):
        if n in souts[kind]:
            return souts[kind][n].reshape(args[n].shape)
        return outs[kind][n].reshape(args[n].shape)

    result = [loss, grad_x.reshape(x.shape)]
    for kind in ("grad", "delta", "new_m", "new_v"):
        result += [leaf(kind, n) for n in names]
    return tuple(result)
```

```python
import numpy as np
import jax
import jax.numpy as jnp
from jax import lax
from jax.experimental import pallas as pl
from jax.experimental.pallas import tpu as pltpu

F32 = jnp.float32
BF16 = jnp.bfloat16
MESH_IDS = pl.DeviceIdType.MESH

LANES = 128
HEAD_DIM = 64
N_FOX_HEADS = 8
N_DIL_HEADS = 8
N_MEM_HEADS = 4
FOX_WIDTH = N_FOX_HEADS * HEAD_DIM
DIL_WIDTH = N_DIL_HEADS * HEAD_DIM
DILATIONS = (1, 4, 16)
BAND = 128
BAND_CHUNK_MAX = 8 * BAND
N_BUCKETS = 32
MAX_DISTANCE = 2048
QK_SCALE = HEAD_DIM ** -0.5
RMS_EPS = 1e-6
NEG = -1e30
VMEM_LIMIT = 56 << 20

ADAM_LR = 0.001
ADAM_B1 = 0.9
ADAM_B2 = 0.999
ADAM_EPS = 1e-08
ADAM_WD = 0.01
ADAM_STEP = 10

N_CHIPS = 4
PACK_COLS = 1024


def _params(*sem):
    return pltpu.CompilerParams(dimension_semantics=sem, vmem_limit_bytes=VMEM_LIMIT)


def _fit(n, cap):
    if n <= cap:
        return n
    t = (cap // LANES) * LANES
    while t >= LANES:
        if n % t == 0:
            return t
        t -= LANES
    raise ValueError(f"no lane-aligned tile for {n} under {cap}")


def _dot(a, b, dims):
    return lax.dot_general(a, b, (dims, ((), ())), preferred_element_type=F32)


_NN = ((1,), (0,))
_NT = ((1,), (1,))
_TN = ((0,), (0,))


def _mm(a, b, mode, out_dtype, name, tm=1024, tn=1024, tk=1024):
    if mode == "nn":
        (M, K), N = a.shape, b.shape[1]
    elif mode == "nt":
        (M, K), N = a.shape, b.shape[0]
    else:
        (K, M), N = a.shape, b.shape[1]
    tm, tn, tk = _fit(M, tm), _fit(N, tn), _fit(K, tk)
    nk = K // tk
    if mode == "tn":
        a_spec = pl.BlockSpec((tk, tm), lambda i, j, k: (k, i))
    else:
        a_spec = pl.BlockSpec((tm, tk), lambda i, j, k: (i, k))
    if mode == "nt":
        b_spec = pl.BlockSpec((tn, tk), lambda i, j, k: (j, k))
    else:
        b_spec = pl.BlockSpec((tk, tn), lambda i, j, k: (k, j))
    dims = {"nn": _NN, "nt": _NT, "tn": _TN}[mode]

    def body(a_ref, b_ref, o_ref, *acc):
        prod = _dot(a_ref[...].astype(BF16), b_ref[...].astype(BF16), dims)
        if nk == 1:
            o_ref[...] = prod.astype(o_ref.dtype)
            return
        acc_ref, k = acc[0], pl.program_id(2)

        @pl.when(k == 0)
        def _():
            acc_ref[...] = prod

        @pl.when(k > 0)
        def _():
            acc_ref[...] += prod

        @pl.when(k == nk - 1)
        def _():
            o_ref[...] = acc_ref[...].astype(o_ref.dtype)

    return pl.pallas_call(
        body, name=name, grid=(M // tm, N // tn, nk),
        in_specs=[a_spec, b_spec],
        out_specs=pl.BlockSpec((tm, tn), lambda i, j, k: (i, j)),
        out_shape=jax.ShapeDtypeStruct((M, N), out_dtype),
        scratch_shapes=[pltpu.VMEM((tm, tn), F32)] if nk > 1 else [],
        compiler_params=_params("parallel", "parallel", "arbitrary"),
    )(a, b)


def _rms_rows(x):
    return lax.rsqrt(jnp.mean(x * x, axis=-1, keepdims=True) + RMS_EPS)


def _rms_fwd(x, g, name, tr=512):
    S, D = x.shape
    tr = _fit(S, tr)

    def body(x_ref, g_ref, h_ref):
        xv = x_ref[...]
        h_ref[...] = (xv * _rms_rows(xv) * g_ref[...]).astype(BF16)

    return pl.pallas_call(
        body, name=name, grid=(S // tr,),
        in_specs=[pl.BlockSpec((tr, D), lambda i: (i, 0)), pl.BlockSpec((1, D), lambda i: (0, 0))],
        out_specs=pl.BlockSpec((tr, D), lambda i: (i, 0)),
        out_shape=jax.ShapeDtypeStruct((S, D), BF16),
        compiler_params=_params("parallel"),
    )(x, g)


def _resid_norm(xres, y, g_post, g_next, name, tr=512):
    S, D = xres.shape
    tr = _fit(S, tr)

    def body(x_ref, y_ref, gp_ref, gn_ref, xn_ref, h_ref):
        yv = y_ref[...]
        xn = x_ref[...] + yv * _rms_rows(yv) * gp_ref[...]
        xn_ref[...] = xn
        h_ref[...] = (xn * _rms_rows(xn) * gn_ref[...]).astype(BF16)

    row = pl.BlockSpec((tr, D), lambda i: (i, 0))
    vec = pl.BlockSpec((1, D), lambda i: (0, 0))
    return pl.pallas_call(
        body, name=name, grid=(S // tr,),
        in_specs=[row, row, vec, vec], out_specs=[row, row],
        out_shape=[jax.ShapeDtypeStruct((S, D), F32), jax.ShapeDtypeStruct((S, D), BF16)],
        compiler_params=_params("parallel"),
    )(xres, y, g_post, g_next)


def _final_loss(xres, y, g_post, target, name, tr=512):
    S, D = xres.shape
    tr = _fit(S, tr)

    def body(x_ref, y_ref, gp_ref, t_ref, d_ref, loss_ref):
        i = pl.program_id(0)
        yv = y_ref[...]
        err = x_ref[...] + yv * _rms_rows(yv) * gp_ref[...] - t_ref[...]
        d_ref[...] = err * (1.0 / D)

        @pl.when(i == 0)
        def _():
            loss_ref[...] = jnp.zeros_like(loss_ref)

        part = jnp.sum(jnp.sum(err * err, axis=1, keepdims=True), axis=0, keepdims=True)
        loss_ref[...] += jnp.broadcast_to(part * (0.5 / D), loss_ref.shape)

    row = pl.BlockSpec((tr, D), lambda i: (i, 0))
    vec = pl.BlockSpec((1, D), lambda i: (0, 0))
    return pl.pallas_call(
        body, name=name, grid=(S // tr,),
        in_specs=[row, row, vec, row],
        out_specs=[row, pl.BlockSpec((8, LANES), lambda i: (0, 0))],
        out_shape=[jax.ShapeDtypeStruct((S, D), F32), jax.ShapeDtypeStruct((8, LANES), F32)],
        compiler_params=_params("arbitrary"),
    )(xres, y, g_post, target)


def _rms_bwd(xin, g, dy, dres, name, want=("f32", "bf16"), tr=512):
    S, D = xin.shape
    tr = _fit(S, tr)
    has_res = dres is not None

    def body(*refs):
        refs = list(refs)
        dg_ref = refs.pop()
        dxb_ref = refs.pop() if "bf16" in want else None
        dx_ref = refs.pop() if "f32" in want else None
        dr_ref = refs.pop() if has_res else None
        x_ref, g_ref, dy_ref = refs
        i = pl.program_id(0)
        xv = x_ref[...]
        dyv = dy_ref[...].astype(F32)
        xhat = xv * _rms_rows(xv)
        dxhat = dyv * g_ref[...]
        r = _rms_rows(xv)
        dx = r * (dxhat - xhat * jnp.mean(dxhat * xhat, axis=-1, keepdims=True))
        if has_res:
            dx = dx + dr_ref[...]
        if dx_ref is not None:
            dx_ref[...] = dx
        if dxb_ref is not None:
            dxb_ref[...] = dx.astype(BF16)

        @pl.when(i == 0)
        def _():
            dg_ref[...] = jnp.zeros_like(dg_ref)

        dg_ref[...] += jnp.broadcast_to(jnp.sum(dyv * xhat, axis=0, keepdims=True), dg_ref.shape)

    row = pl.BlockSpec((tr, D), lambda i: (i, 0))
    vec = pl.BlockSpec((1, D), lambda i: (0, 0))
    acc = pl.BlockSpec((8, D), lambda i: (0, 0))
    ins = [xin, g, dy] + ([dres] if has_res else [])
    dtypes = [dt for key, dt in (("f32", F32), ("bf16", BF16)) if key in want]
    outs = pl.pallas_call(
        body, name=name, grid=(S // tr,),
        in_specs=[row, vec, row] + ([row] if has_res else []),
        out_specs=[row] * len(dtypes) + [acc],
        out_shape=[jax.ShapeDtypeStruct((S, D), dt) for dt in dtypes] + [jax.ShapeDtypeStruct((8, D), F32)],
        compiler_params=_params("arbitrary"),
    )(*ins)
    by_key = dict(zip([key for key in ("f32", "bf16") if key in want], outs[:-1]))
    return by_key.get("f32"), by_key.get("bf16"), outs[-1]


def _tri(n, upper):
    r = lax.broadcasted_iota(jnp.int32, (n, n), 0)
    c = lax.broadcasted_iota(jnp.int32, (n, n), 1)
    return jnp.where((r <= c) if upper else (r >= c), 1.0, 0.0).astype(F32)


def _forget_fwd(fg_t, b_col, name, ts=512):
    H, S = fg_t.shape
    ts = _fit(S, ts)

    def body(f_ref, b_ref, c_ref, carry_ref):
        i = pl.program_id(0)

        @pl.when(i == 0)
        def _():
            carry_ref[...] = jnp.zeros_like(carry_ref)

        z = f_ref[...] + b_ref[...]
        logf = jnp.minimum(z, 0.0) - jnp.log(1.0 + jnp.exp(-jnp.abs(z)))
        run = lax.dot_general(logf, _tri(ts, True), (_NN, ((), ())), precision=lax.Precision.HIGHEST,
                              preferred_element_type=F32) + carry_ref[:, 0:1]
        c_ref[...] = run
        carry_ref[...] = jnp.broadcast_to(
            carry_ref[:, 0:1] + jnp.sum(logf, axis=1, keepdims=True), carry_ref.shape)

    return pl.pallas_call(
        body, name=name, grid=(S // ts,),
        in_specs=[pl.BlockSpec((H, ts), lambda i: (0, i)), pl.BlockSpec((H, 1), lambda i: (0, 0))],
        out_specs=pl.BlockSpec((H, ts), lambda i: (0, i)),
        out_shape=jax.ShapeDtypeStruct((H, S), F32),
        scratch_shapes=[pltpu.VMEM((H, LANES), F32)],
        compiler_params=_params("arbitrary"),
    )(fg_t, b_col)


def _forget_bwd(fg_t, b_col, dc_plus, dc_minus, name, ts=512):
    H, S = fg_t.shape
    ts = _fit(S, ts)
    nb = S // ts

    def body(f_ref, b_ref, dcp_ref, dcm_ref, df_ref, db_ref, carry_ref):
        i = pl.program_id(0)

        @pl.when(i == 0)
        def _():
            carry_ref[...] = jnp.zeros_like(carry_ref)
            db_ref[...] = jnp.zeros_like(db_ref)

        dc = dcp_ref[...] - dcm_ref[...]
        suffix = lax.dot_general(dc, _tri(ts, False), (_NN, ((), ())), precision=lax.Precision.HIGHEST,
                                 preferred_element_type=F32) + carry_ref[:, 0:1]
        z = f_ref[...] + b_ref[...]
        sig_neg = 1.0 / (1.0 + jnp.exp(z))
        df = suffix * sig_neg
        df_ref[...] = df
        carry_ref[...] = jnp.broadcast_to(
            carry_ref[:, 0:1] + jnp.sum(dc, axis=1, keepdims=True), carry_ref.shape)
        db_ref[...] += jnp.broadcast_to(jnp.sum(df, axis=1, keepdims=True), db_ref.shape)

    rev = pl.BlockSpec((H, ts), lambda i: (0, nb - 1 - i))
    return pl.pallas_call(
        body, name=name, grid=(nb,),
        in_specs=[rev, pl.BlockSpec((H, 1), lambda i: (0, 0)), rev, rev],
        out_specs=[rev, pl.BlockSpec((H, LANES), lambda i: (0, 0))],
        out_shape=[jax.ShapeDtypeStruct((H, S), F32), jax.ShapeDtypeStruct((H, LANES), F32)],
        scratch_shapes=[pltpu.VMEM((H, LANES), F32)],
        compiler_params=_params("arbitrary"),
    )(fg_t, b_col, dc_plus, dc_minus)


ONES_ROWS = 16
EXTRA = 3


def _split3(x):
    hi = lax.reduce_precision(x, 8, 7)
    mid = lax.reduce_precision(x - hi, 8, 7)
    lo = lax.reduce_precision(x - hi - mid, 8, 7)
    return hi.astype(BF16), mid.astype(BF16), lo.astype(BF16)


def _lanes_operand(t, extras):
    block = jnp.pad(jnp.stack(extras, axis=-1), ((0, 0), (0, 0), (0, LANES - HEAD_DIM - len(extras))))
    return jnp.concatenate([t, block], axis=-1)


def _rows_operand(t, extras):
    block = jnp.pad(jnp.stack(extras, axis=1), ((0, 0), (0, LANES - HEAD_DIM - len(extras)), (0, 0)))
    return jnp.concatenate([t, block], axis=1)


def _with_ones(t):
    return jnp.concatenate([t, jnp.ones((t.shape[0], ONES_ROWS, t.shape[2]), t.dtype)], axis=1)


def _fox_fwd(qa, ka, vt, name, tq=512, tk=1024):
    H, _, S = qa.shape
    Dh = HEAD_DIM
    tk = _fit(S, tk)
    tq = _fit(tk, tq)
    ratio = tk // tq

    def body(qa_ref, ka_ref, vt_ref, o_ref, lse_ref, m_ref, acc_ref, sa_ref, sb_ref, ta_ref, tb_ref):
        i = pl.program_id(1)
        qv = qa_ref[...] * QK_SCALE
        m_ref[...] = jnp.full_like(m_ref, NEG)
        acc_ref[...] = jnp.zeros_like(acc_ref)
        n = i // ratio
        q_off = (i - n * ratio) * tq

        def scores(j, s_ref, t_ref, diagonal):
            off = pl.multiple_of(j * tk, LANES)
            s = _dot(ka_ref[pl.ds(off, tk), :], qv, _NN)
            if diagonal:
                key = lax.broadcasted_iota(jnp.int32, (tk, tq), 0)
                qry = lax.broadcasted_iota(jnp.int32, (tk, tq), 1) + q_off
                s = jnp.where(key <= qry, s, NEG)
            s_ref[...] = s
            t_ref[...] = jnp.max(s, axis=0, keepdims=True)

        def absorb(j, s_ref, t_ref):
            off = pl.multiple_of(j * tk, LANES)
            m_old = m_ref[...]
            m_new = jnp.maximum(m_old, t_ref[...])
            p = jnp.exp(s_ref[...] - m_new)
            alpha = jnp.exp(m_old - m_new)
            acc_ref[...] = alpha * acc_ref[...] + _dot(vt_ref[:, pl.ds(off, tk)], p.astype(BF16), _NN)
            m_ref[...] = m_new

        scores(n, sa_ref, ta_ref, True)

        def loop_body(jj, carry):
            scores(2 * jj, sb_ref, tb_ref, False)
            absorb(jnp.where(jj == 0, n, 2 * jj - 1), sa_ref, ta_ref)
            scores(2 * jj + 1, sa_ref, ta_ref, False)
            absorb(2 * jj, sb_ref, tb_ref)
            return carry

        pairs = n // 2
        lax.fori_loop(0, pairs, loop_body, 0)
        held = jnp.where(pairs == 0, n, 2 * pairs - 1)

        @pl.when(n % 2 == 1)
        def _():
            scores(n - 1, sb_ref, tb_ref, False)
            absorb(held, sa_ref, ta_ref)
            absorb(n - 1, sb_ref, tb_ref)

        @pl.when(n % 2 == 0)
        def _():
            absorb(held, sa_ref, ta_ref)

        l = acc_ref[Dh:Dh + 1, :]
        o_ref[...] = acc_ref[0:Dh, :] / l
        lse_ref[...] = m_ref[...] + jnp.log(l)

    return pl.pallas_call(
        body, name=name, grid=(H, S // tq),
        in_specs=[pl.BlockSpec((None, LANES, tq), lambda h, i: (h, 0, i)),
                  pl.BlockSpec((None, S, LANES), lambda h, i: (h, 0, 0)),
                  pl.BlockSpec((None, Dh + ONES_ROWS, S), lambda h, i: (h, 0, 0))],
        out_specs=[pl.BlockSpec((None, Dh, tq), lambda h, i: (h, 0, i)),
                   pl.BlockSpec((None, 1, tq), lambda h, i: (h, 0, i))],
        out_shape=[jax.ShapeDtypeStruct((H, Dh, S), F32), jax.ShapeDtypeStruct((H, 1, S), F32)],
        scratch_shapes=[pltpu.VMEM((1, tq), F32), pltpu.VMEM((Dh + ONES_ROWS, tq), F32),
                        pltpu.VMEM((tk, tq), F32), pltpu.VMEM((tk, tq), F32),
                        pltpu.VMEM((1, tq), F32), pltpu.VMEM((1, tq), F32)],
        compiler_params=_params("parallel", "arbitrary"),
    )(qa, ka, vt)


def _fox_bwd(qa, ka, kta, va, doa, name, tq=1024, tk=512):
    H, _, S = qa.shape
    Dh, Da = HEAD_DIM, HEAD_DIM + ONES_ROWS
    tq = _fit(S, tq)
    tk = _fit(tq, tk)
    ratio = tq // tk
    nq = S // tq
    nk = S // tk

    def body(ka_ref, kta_ref, va_ref, qa_ref, doa_ref, dqt_ref, dkt_ref, dvt_ref, dka_ref, dva_ref):
        j = pl.program_id(1)

        @pl.when(j == 0)
        def _():
            dqt_ref[...] = jnp.zeros_like(dqt_ref)

        kv = ka_ref[...]
        ktv = kta_ref[0:Da, :]
        vv = va_ref[...]
        dka_ref[...] = jnp.zeros_like(dka_ref)
        dva_ref[...] = jnp.zeros_like(dva_ref)
        i_diag = j // ratio
        k_off = (j - i_diag * ratio) * tk

        def step(i, diagonal):
            off = pl.multiple_of(i * tq, LANES)
            qv = qa_ref[:, pl.ds(off, tq)] * QK_SCALE
            dov = doa_ref[:, pl.ds(off, tq)]
            e = _dot(kv, qv, _NN)
            if diagonal:
                key = lax.broadcasted_iota(jnp.int32, (tk, tq), 0) + k_off
                qry = lax.broadcasted_iota(jnp.int32, (tk, tq), 1)
                e = jnp.where(key <= qry, e, NEG)
            p_t = jnp.exp(e)
            dva_ref[...] += _dot(dov[0:Dh, :], p_t.astype(BF16), _NT)
            ds_b = (p_t * _dot(vv, dov, _NN)).astype(BF16)
            dka_ref[...] += _dot(qv[0:Da, :], ds_b, _NT)
            dqt_ref[:, pl.ds(off, tq)] += _dot(ktv, ds_b, _NN)

        step(i_diag, True)

        def loop_body(i, carry):
            step(i, False)
            return carry

        lax.fori_loop(i_diag + 1, nq, loop_body, 0)
        dkt_ref[...] = dka_ref[...]
        dvt_ref[...] = dva_ref[...]

        @pl.when(j == nk - 1)
        def _():
            dqt_ref[0:Dh, :] = dqt_ref[0:Dh, :] * QK_SCALE

    lanes_tile = pl.BlockSpec((None, tk, LANES), lambda h, j: (h, j, 0))
    rows_tile = pl.BlockSpec((None, LANES, tk), lambda h, j: (h, 0, j))
    rows_full = pl.BlockSpec((None, LANES, S), lambda h, j: (h, 0, 0))
    return pl.pallas_call(
        body, name=name, grid=(H, nk),
        in_specs=[lanes_tile, rows_tile, lanes_tile, rows_full, rows_full],
        out_specs=[pl.BlockSpec((None, Da, S), lambda h, j: (h, 0, 0)),
                   pl.BlockSpec((None, Da, tk), lambda h, j: (h, 0, j)),
                   pl.BlockSpec((None, Dh, tk), lambda h, j: (h, 0, j))],
        out_shape=[jax.ShapeDtypeStruct((H, Da, S), F32), jax.ShapeDtypeStruct((H, Da, S), F32),
                   jax.ShapeDtypeStruct((H, Dh, S), F32)],
        scratch_shapes=[pltpu.VMEM((Da, tk), F32), pltpu.VMEM((Dh, tk), F32)],
        compiler_params=_params("parallel", "arbitrary"),
    )(ka, kta, va, qa, doa)


DIL_Q_BLOCK = 3 * FOX_WIDTH // LANES
HEAD_PAIRS = N_DIL_HEADS // 2
PAIR_BLOCKS = DIL_WIDTH // LANES


def _band_geometry(S, d):
    L = S // d
    chunk = min(BAND_CHUNK_MAX, L)
    assert L % chunk == 0 and chunk % BAND == 0
    return L, chunk, chunk // BAND, L // chunk


def _band_in_specs(S, d, base):
    L, chunk, nb, _ = _band_geometry(S, d)

    def col(kind):
        return lambda hp, r, i: (r, i, base + kind * PAIR_BLOCKS + hp)

    def col_prev(kind):
        return lambda hp, r, i: (r, jnp.maximum(i * nb - 1, 0), base + kind * PAIR_BLOCKS + hp)

    main = [pl.BlockSpec((None, chunk, LANES), col(kind)) for kind in range(3)]
    prev = [pl.BlockSpec((None, BAND, LANES), col_prev(kind)) for kind in range(3)]
    bias = pl.BlockSpec((None, 2 * BAND, 2 * BAND), lambda hp, r, i: (hp, 0, 0))
    stat = pl.BlockSpec((None, 2, chunk), lambda hp, r, i: (hp, 0, r * (L // chunk) + i))
    tok = pl.BlockSpec((None, chunk, LANES), lambda hp, r, i: (r, i, hp))
    return main, prev, bias, stat, tok


def _to_residues(x, col_block, width, d, name, tr=1024):
    S = x.shape[0]
    tr = _fit(S, tr)

    def body(x_ref, o_ref, tmp_ref):
        for j in range(width // LANES):
            cols = slice(j * LANES, (j + 1) * LANES)
            tmp_ref[j] = x_ref[:, cols].astype(F32)
            for r in range(d):
                o_ref[r, :, cols] = tmp_ref[j, pl.ds(r, tr // d, stride=d), :].astype(o_ref.dtype)

    return pl.pallas_call(
        body, name=name, grid=(S // tr,),
        in_specs=[pl.BlockSpec((tr, width), lambda i: (i, col_block))],
        out_specs=pl.BlockSpec((d, tr // d, width), lambda i: (0, i, 0)),
        out_shape=jax.ShapeDtypeStruct((d, S // d, width), x.dtype),
        scratch_shapes=[pltpu.VMEM((width // LANES, tr, LANES), F32)],
        compiler_params=_params("parallel"),
    )(x)


def _token_rows(ref, cols, tmp_ref):
    if len(ref.shape) == 2:
        return ref[:, cols].astype(F32)
    d, rows = ref.shape[0], ref.shape[1]
    for r in range(d):
        tmp_ref[pl.ds(r, rows, stride=d), :] = ref[r, :, cols].astype(F32)
    return tmp_ref[...]


def _row_spec(t, tr):
    if t.ndim == 2:
        return pl.BlockSpec((tr, t.shape[1]), lambda i: (i, 0))
    d = t.shape[0]
    return pl.BlockSpec((d, tr // d, t.shape[2]), lambda i: (0, i, 0))


def _head_lanes(a):
    return lax.broadcasted_iota(jnp.int32, (1, LANES), 1) // HEAD_DIM == a


def _one_head(x, a):
    return jnp.where(_head_lanes(a), x, jnp.zeros_like(x))


def _head_stack(x):
    return jnp.concatenate([_one_head(x, 0), _one_head(x, 1)], axis=0)


def _pair_rows(ref, rows):
    return jnp.concatenate([ref[0:1, rows], ref[1:2, rows]], axis=1)


def _band_scores_t(kb, q_stack, bias_t, first):
    s = _dot(kb, q_stack, _NT) + bias_t
    if first is not None:
        key = lax.broadcasted_iota(jnp.int32, s.shape, 0)
        s = jnp.where(jnp.logical_and(first, key < BAND), NEG, s)
    return s


def _pair_select(stacked):
    return jnp.where(_head_lanes(0), stacked[0:BAND, :], stacked[BAND:, :])


def _dil_lse(qkv_v, base, bias_t, name):
    d, L = qkv_v.shape[:2]
    S = L * d
    _, chunk, nb, nchunks = _band_geometry(S, d)
    main, prev, bias, stat, _ = _band_in_specs(S, d, base)

    def body(q_ref, k_ref, kp_ref, b_ref, lse_ref, kext_ref):
        first = pl.program_id(2) == 0
        kext_ref[0:BAND, :] = kp_ref[...]
        kext_ref[BAND:, :] = k_ref[...]
        for b in range(nb):
            rows, ext = slice(b * BAND, (b + 1) * BAND), slice(b * BAND, (b + 2) * BAND)
            s = _band_scores_t(kext_ref[ext, :], _head_stack(q_ref[rows, :] * QK_SCALE), b_ref[...],
                               first if b == 0 else None)
            m = jnp.max(s, axis=0, keepdims=True)
            lse = m + jnp.log(jnp.sum(jnp.exp(s - m), axis=0, keepdims=True))
            lse_ref[0:1, rows] = lse[:, 0:BAND]
            lse_ref[1:2, rows] = lse[:, BAND:]

    return pl.pallas_call(
        body, name=name, grid=(HEAD_PAIRS, d, nchunks),
        in_specs=[main[0], main[1], prev[1], bias], out_specs=stat,
        out_shape=jax.ShapeDtypeStruct((HEAD_PAIRS, 2, S), F32),
        scratch_shapes=[pltpu.VMEM((chunk + BAND, LANES), BF16)],
        compiler_params=_params("parallel", "parallel", "parallel"),
    )(qkv_v, qkv_v, qkv_v, bias_t)


def _dil_out(qkv_v, base, bias_t, lse_joint, name):
    d, L = qkv_v.shape[:2]
    S = L * d
    _, chunk, nb, nchunks = _band_geometry(S, d)
    main, prev, bias, stat, tok = _band_in_specs(S, d, base)

    def body(q_ref, k_ref, kp_ref, v_ref, vp_ref, b_ref, lse_ref, o_ref, kext_ref, vext_ref):
        first = pl.program_id(2) == 0
        kext_ref[0:BAND, :] = kp_ref[...]
        kext_ref[BAND:, :] = k_ref[...]
        vext_ref[0:BAND, :] = vp_ref[...]
        vext_ref[BAND:, :] = v_ref[...]
        for b in range(nb):
            rows, ext = slice(b * BAND, (b + 1) * BAND), slice(b * BAND, (b + 2) * BAND)
            s = _band_scores_t(kext_ref[ext, :], _head_stack(q_ref[rows, :] * QK_SCALE), b_ref[...],
                               first if b == 0 else None)
            p_t = jnp.exp(s - _pair_rows(lse_ref, rows))
            o_ref[rows, :] = _pair_select(_dot(p_t.astype(BF16), vext_ref[ext, :], _TN)).astype(BF16)

    return pl.pallas_call(
        body, name=name, grid=(HEAD_PAIRS, d, nchunks),
        in_specs=[main[0], main[1], prev[1], main[2], prev[2], bias, stat], out_specs=tok,
        out_shape=jax.ShapeDtypeStruct((d, L, DIL_WIDTH), BF16),
        scratch_shapes=[pltpu.VMEM((chunk + BAND, LANES), BF16), pltpu.VMEM((chunk + BAND, LANES), BF16)],
        compiler_params=_params("parallel", "parallel", "parallel"),
    )(qkv_v, qkv_v, qkv_v, qkv_v, qkv_v, bias_t, lse_joint)


def _dil_bwd(qkv_v, base, do_v, bias_t, lse_joint, delta, name):
    d, L = qkv_v.shape[:2]
    S = L * d
    _, chunk, nb, nchunks = _band_geometry(S, d)
    main, prev, bias, stat, tok = _band_in_specs(S, d, base)
    nblocks = L // BAND

    def nxt_row(i):
        return jnp.minimum((i + 1) * nb, nblocks - 1)

    q_next = pl.BlockSpec((None, BAND, LANES), lambda hp, r, i: (r, nxt_row(i), base + hp))
    do_next = pl.BlockSpec((None, BAND, LANES), lambda hp, r, i: (r, nxt_row(i), hp))
    stat_next = pl.BlockSpec((None, 2, BAND), lambda hp, r, i: (hp, 0, r * nblocks + nxt_row(i)))

    def body(q_ref, k_ref, kp_ref, v_ref, vp_ref, do_ref, b_ref, lse_ref, dl_ref,
             qn_ref, don_ref, lsen_ref, dln_ref,
             dq_ref, dk_ref, dv_ref, db_ref, kext_ref, vext_ref, dkext_ref, dvext_ref):
        r, i = pl.program_id(1), pl.program_id(2)
        first = i == 0
        has_next = i + 1 < nchunks
        tail = slice(BAND + chunk, 2 * BAND + chunk)
        kext_ref[0:BAND, :] = kp_ref[...]
        kext_ref[BAND:BAND + chunk, :] = k_ref[...]
        kext_ref[tail, :] = jnp.zeros((BAND, LANES), BF16)
        vext_ref[0:BAND, :] = vp_ref[...]
        vext_ref[BAND:BAND + chunk, :] = v_ref[...]
        vext_ref[tail, :] = jnp.zeros((BAND, LANES), BF16)
        dkext_ref[...] = jnp.zeros_like(dkext_ref)
        dvext_ref[...] = jnp.zeros_like(dvext_ref)

        @pl.when(jnp.logical_and(r == 0, i == 0))
        def _():
            db_ref[...] = jnp.zeros_like(db_ref)

        def block(q2, do2, lse_row, dl_row, ext, mask_rows):
            q_stack, do_stack = _head_stack(q2), _head_stack(do2)
            s = _dot(kext_ref[ext, :], q_stack, _NT) + b_ref[...]
            if mask_rows is not None:
                s = jnp.where(mask_rows, NEG, s)
            p_t = jnp.exp(s - lse_row)
            ds_t = p_t * (_dot(vext_ref[ext, :], do_stack, _NT) - dl_row)
            ds_b = ds_t.astype(BF16)
            dkext_ref[ext, :] += _dot(ds_b, q_stack, _NN)
            dvext_ref[ext, :] += _dot(p_t.astype(BF16), do_stack, _NN)
            return ds_t, ds_b

        key = lax.broadcasted_iota(jnp.int32, (2 * BAND, 2 * BAND), 0)
        all_lanes = slice(0, BAND)
        for b in range(nb):
            rows, ext = slice(b * BAND, (b + 1) * BAND), slice(b * BAND, (b + 2) * BAND)
            mask = jnp.logical_and(first, key < BAND) if b == 0 else None
            ds_t, ds_b = block(q_ref[rows, :] * QK_SCALE, do_ref[rows, :], _pair_rows(lse_ref, rows),
                               _pair_rows(dl_ref, rows), ext, mask)
            dq_ref[rows, :] = (_pair_select(_dot(ds_b, kext_ref[ext, :], _TN)) * QK_SCALE).astype(BF16)
            db_ref[...] += ds_t
        block(qn_ref[...] * QK_SCALE, don_ref[...], _pair_rows(lsen_ref, all_lanes), _pair_rows(dln_ref, all_lanes),
              slice(chunk, chunk + 2 * BAND), jnp.logical_or(jnp.logical_not(has_next), key >= BAND))
        dk_ref[...] = dkext_ref[BAND:BAND + chunk, :].astype(BF16)
        dv_ref[...] = dvext_ref[BAND:BAND + chunk, :].astype(BF16)

    ext_rows = chunk + 2 * BAND
    return pl.pallas_call(
        body, name=name, grid=(HEAD_PAIRS, d, nchunks),
        in_specs=[main[0], main[1], prev[1], main[2], prev[2], tok, bias, stat, stat,
                  q_next, do_next, stat_next, stat_next],
        out_specs=[tok, tok, tok, bias],
        out_shape=[jax.ShapeDtypeStruct((d, L, DIL_WIDTH), BF16)] * 3
                  + [jax.ShapeDtypeStruct((HEAD_PAIRS, 2 * BAND, 2 * BAND), F32)],
        scratch_shapes=[pltpu.VMEM((ext_rows, LANES), BF16), pltpu.VMEM((ext_rows, LANES), BF16),
                        pltpu.VMEM((ext_rows, LANES), F32), pltpu.VMEM((ext_rows, LANES), F32)],
        compiler_params=_params("arbitrary", "arbitrary", "arbitrary"),
    )(qkv_v, qkv_v, qkv_v, qkv_v, qkv_v, do_v, bias_t, lse_joint, delta, qkv_v, do_v, lse_joint, delta)


def _lse_join(lse3, name):
    P, H, S = lse3.shape

    def body(l_ref, o_ref):
        a, b, c = l_ref[0], l_ref[1], l_ref[2]
        m = jnp.maximum(jnp.maximum(a, b), c)
        o_ref[...] = m + jnp.log(jnp.exp(a - m) + jnp.exp(b - m) + jnp.exp(c - m))

    return pl.pallas_call(body, name=name, out_shape=jax.ShapeDtypeStruct((H, S), F32))(lse3)


def _bucket_reduce(dbias_t, bucket_map_t, name):
    P, H = dbias_t.shape[:2]

    def body(db_ref, bk_ref, o_ref):
        p, h = pl.program_id(0), pl.program_id(1)

        @pl.when(jnp.logical_and(p == 0, h == 0))
        def _():
            o_ref[...] = jnp.zeros_like(o_ref)

        db, bk = db_ref[...], bk_ref[...]
        row = lax.broadcasted_iota(jnp.int32, (N_BUCKETS, LANES), 0)
        lane = lax.broadcasted_iota(jnp.int32, (N_BUCKETS, LANES), 1)

        def one(b, acc):
            val = jnp.sum(jnp.sum(jnp.where(bk == b, db, 0.0), axis=1, keepdims=True), axis=0, keepdims=True)
            return acc + jnp.where(jnp.logical_and(row == b, lane == h), val, 0.0)

        acc = jnp.zeros((N_BUCKETS, LANES), F32)
        for b in range(N_BUCKETS):
            acc = one(b, acc)
        o_ref[...] += acc

    return pl.pallas_call(
        body, name=name, grid=(P, H),
        in_specs=[pl.BlockSpec((None, None, 2 * BAND, BAND), lambda p, h: (p, h, 0, 0)),
                  pl.BlockSpec((None, 2 * BAND, BAND), lambda p, h: (p, 0, 0))],
        out_specs=pl.BlockSpec((N_BUCKETS, LANES), lambda p, h: (0, 0)),
        out_shape=jax.ShapeDtypeStruct((N_BUCKETS, LANES), F32),
        compiler_params=_params("arbitrary", "arbitrary"),
    )(dbias_t, bucket_map_t)


def _mem_fwd(q, kv, name, tq=1024):
    S, W = q.shape
    N = kv.shape[0]
    pairs = W // LANES
    tq = _fit(S, tq)

    def body(q_ref, k_ref, v_ref, o_ref, lse_ref):
        for a in range(2):
            lanes = slice(a * HEAD_DIM, (a + 1) * HEAD_DIM)
            s = _dot(k_ref[:, lanes], q_ref[:, lanes] * QK_SCALE, _NT)
            m = jnp.max(s, axis=0, keepdims=True)
            e = jnp.exp(s - m)
            l = jnp.sum(e, axis=0, keepdims=True)
            o_ref[:, lanes] = _dot((e / l).astype(BF16), v_ref[:, lanes], _TN).astype(BF16)
            lse_ref[a:a + 1, :] = m + jnp.log(l)

    return pl.pallas_call(
        body, name=name, grid=(pairs, S // tq),
        in_specs=[pl.BlockSpec((tq, LANES), lambda hp, i: (i, hp)),
                  pl.BlockSpec((N, LANES), lambda hp, i: (0, hp)),
                  pl.BlockSpec((N, LANES), lambda hp, i: (0, pairs + hp))],
        out_specs=[pl.BlockSpec((tq, LANES), lambda hp, i: (i, hp)),
                   pl.BlockSpec((None, 2, tq), lambda hp, i: (hp, 0, i))],
        out_shape=[jax.ShapeDtypeStruct((S, W), BF16), jax.ShapeDtypeStruct((pairs, 2, S), F32)],
        compiler_params=_params("parallel", "parallel"),
    )(q, kv, kv)


def _mem_bwd(q, kv, do, lse, delta, name, tq=1024):
    S, W = q.shape
    N = kv.shape[0]
    pairs = W // LANES
    tq = _fit(S, tq)

    def body(q_ref, k_ref, v_ref, do_ref, lse_ref, dl_ref, dq_ref, dk_ref, dv_ref):
        i = pl.program_id(1)

        @pl.when(i == 0)
        def _():
            dk_ref[...] = jnp.zeros_like(dk_ref)
            dv_ref[...] = jnp.zeros_like(dv_ref)

        for a in range(2):
            lanes = slice(a * HEAD_DIM, (a + 1) * HEAD_DIM)
            qv, dov = q_ref[:, lanes] * QK_SCALE, do_ref[:, lanes]
            kv_, vv = k_ref[:, lanes], v_ref[:, lanes]
            p_t = jnp.exp(_dot(kv_, qv, _NT) - lse_ref[a:a + 1, :])
            ds_t = p_t * (_dot(vv, dov, _NT) - dl_ref[a:a + 1, :])
            ds_b = ds_t.astype(BF16)
            dq_ref[:, lanes] = (_dot(ds_b, kv_, _TN) * QK_SCALE).astype(BF16)
            dk_ref[:, lanes] += _dot(ds_b, qv, _NN)
            dv_ref[:, lanes] += _dot(p_t.astype(BF16), dov, _NN)

    qs = pl.BlockSpec((tq, LANES), lambda hp, i: (i, hp))
    stat = pl.BlockSpec((None, 2, tq), lambda hp, i: (hp, 0, i))
    acc = pl.BlockSpec((N, LANES), lambda hp, i: (0, hp))
    return pl.pallas_call(
        body, name=name, grid=(pairs, S // tq),
        in_specs=[qs, acc, pl.BlockSpec((N, LANES), lambda hp, i: (0, pairs + hp)), qs, stat, stat],
        out_specs=[qs, acc, acc],
        out_shape=[jax.ShapeDtypeStruct((S, W), BF16), jax.ShapeDtypeStruct((N, W), F32),
                   jax.ShapeDtypeStruct((N, W), F32)],
        compiler_params=_params("parallel", "arbitrary"),
    )(q, kv, kv, do, lse, delta)


def _head_rowdot(a, bs, name, tr=512):
    S, W = a.shape
    tr = _fit(S, tr)

    def body(*refs):
        a_ref, b_refs, o_ref, tmp_ref = refs[0], refs[1:-2], refs[-2], refs[-1]
        col = lax.broadcasted_iota(jnp.int32, (LANES, LANES), 0)
        lane = lax.broadcasted_iota(jnp.int32, (LANES, LANES), 1)
        acc = jnp.zeros((tr, LANES), F32)
        for j in range(W // LANES):
            cols = slice(j * LANES, (j + 1) * LANES)
            tot = _token_rows(b_refs[0], cols, tmp_ref)
            for r in b_refs[1:]:
                tot = tot + _token_rows(r, cols, tmp_ref)
            sel = jnp.where(col // HEAD_DIM + j * (LANES // HEAD_DIM) == lane, 1.0, 0.0).astype(F32)
            acc = acc + lax.dot_general(a_ref[:, cols].astype(F32) * tot, sel, (_NN, ((), ())),
                                        precision=lax.Precision.HIGHEST, preferred_element_type=F32)
        o_ref[...] = acc

    return pl.pallas_call(
        body, name=name, grid=(S // tr,), in_specs=[_row_spec(t, tr) for t in [a] + list(bs)],
        out_specs=pl.BlockSpec((tr, LANES), lambda i: (i, 0)),
        out_shape=jax.ShapeDtypeStruct((S, LANES), F32),
        scratch_shapes=[pltpu.VMEM((tr, LANES), F32)],
        compiler_params=_params("parallel"),
    )(a, *bs)


def _sum_cast_cols(groups, out_dtype, name, tail=None, tr=512):
    first = groups[0][0]
    S, W = (first.shape if first.ndim == 2 else (first.shape[0] * first.shape[1], first.shape[2]))
    tr = _fit(S, tr)
    flat = [t for g in groups for t in g] + ([tail] if tail is not None else [])
    tail_w = 0 if tail is None else tail.shape[1]

    def body(*refs):
        o_ref, tmp_ref = refs[-2], refs[-1]
        if tail is not None:
            o_ref[:, W * len(groups):] = refs[-3][...].astype(out_dtype)
        k = 0
        for gi, g in enumerate(groups):
            for j in range(W // LANES):
                cols = slice(j * LANES, (j + 1) * LANES)
                acc = _token_rows(refs[k], cols, tmp_ref)
                for r in refs[k + 1:k + len(g)]:
                    acc = acc + _token_rows(r, cols, tmp_ref)
                o_ref[:, gi * W + j * LANES:gi * W + (j + 1) * LANES] = acc.astype(out_dtype)
            k += len(g)

    return pl.pallas_call(
        body, name=name, grid=(S // tr,), in_specs=[_row_spec(t, tr) for t in flat],
        out_specs=pl.BlockSpec((tr, W * len(groups) + tail_w), lambda i: (i, 0)),
        out_shape=jax.ShapeDtypeStruct((S, W * len(groups) + tail_w), out_dtype),
        scratch_shapes=[pltpu.VMEM((tr, LANES), F32)],
        compiler_params=_params("parallel"),
    )(*flat)


FF_TILE = 256


def _ffn_up(h, w_gu, name, tm=4096):
    S, D = h.shape
    F2 = w_gu.shape[1]
    tm = _fit(S, tm)

    def body(h_ref, w_ref, gu_ref, act_ref):
        gu = _dot(h_ref[...], w_ref[...], _NN)
        gu_ref[...] = gu.astype(BF16)
        g, u = gu[:, :FF_TILE], gu[:, FF_TILE:]
        act_ref[...] = (g * (1.0 / (1.0 + jnp.exp(-g))) * u).astype(BF16)

    return pl.pallas_call(
        body, name=name, grid=(S // tm, F2 // (2 * FF_TILE)),
        in_specs=[pl.BlockSpec((tm, D), lambda i, j: (i, 0)), pl.BlockSpec((D, 2 * FF_TILE), lambda i, j: (0, j))],
        out_specs=[pl.BlockSpec((tm, 2 * FF_TILE), lambda i, j: (i, j)),
                   pl.BlockSpec((tm, FF_TILE), lambda i, j: (i, j))],
        out_shape=[jax.ShapeDtypeStruct((S, F2), BF16), jax.ShapeDtypeStruct((S, F2 // 2), BF16)],
        compiler_params=_params("parallel", "arbitrary"),
    )(h, w_gu)


def _ffn_dact(dy, w_down, gu, name, tm=4096):
    S, D = dy.shape
    F2 = gu.shape[1]
    tm = _fit(S, tm)

    def body(dy_ref, w_ref, gu_ref, dgu_ref):
        dact = _dot(dy_ref[...], w_ref[...], _NT)
        gu_v = gu_ref[...].astype(F32)
        g, u = gu_v[:, :FF_TILE], gu_v[:, FF_TILE:]
        sig = 1.0 / (1.0 + jnp.exp(-g))
        silu = g * sig
        dgu_ref[:, :FF_TILE] = (dact * u * (sig + silu * (1.0 - sig))).astype(BF16)
        dgu_ref[:, FF_TILE:] = (dact * silu).astype(BF16)

    return pl.pallas_call(
        body, name=name, grid=(S // tm, F2 // (2 * FF_TILE)),
        in_specs=[pl.BlockSpec((tm, D), lambda i, j: (i, 0)), pl.BlockSpec((FF_TILE, D), lambda i, j: (j, 0)),
                  pl.BlockSpec((tm, 2 * FF_TILE), lambda i, j: (i, j))],
        out_specs=pl.BlockSpec((tm, 2 * FF_TILE), lambda i, j: (i, j)),
        out_shape=jax.ShapeDtypeStruct((S, F2), BF16),
        compiler_params=_params("parallel", "arbitrary"),
    )(dy, w_down, gu)


def _fit_rows(n, cap):
    if n <= cap:
        return n
    t = (cap // 8) * 8
    while t >= 8:
        if n % t == 0:
            return t
        t -= 8
    raise ValueError(f"no sublane-aligned tile for {n} under {cap}")


def _add_n(arrs, name, tr=512):
    R, C = arrs[0].shape
    tr = _fit_rows(R, tr)

    def body(*refs):
        acc = refs[0][...].astype(F32)
        for r in refs[1:-1]:
            acc = acc + r[...].astype(F32)
        refs[-1][...] = acc

    row = pl.BlockSpec((tr, C), lambda i: (i, 0))
    return pl.pallas_call(
        body, name=name, grid=(R // tr,), in_specs=[row] * len(arrs), out_specs=row,
        out_shape=jax.ShapeDtypeStruct((R, C), F32), compiler_params=_params("parallel"),
    )(*arrs)


def _adamw(w, g, m, v, name, tr=512):
    R, C = w.shape
    tr = _fit_rows(R, tr)
    c1 = 1.0 / (1.0 - ADAM_B1 ** ADAM_STEP)
    c2 = 1.0 / (1.0 - ADAM_B2 ** ADAM_STEP)

    def body(w_ref, g_ref, m_ref, v_ref, d_ref, nm_ref, nv_ref):
        gv = g_ref[...]
        nm = ADAM_B1 * m_ref[...] + (1.0 - ADAM_B1) * gv
        nv = ADAM_B2 * v_ref[...] + (1.0 - ADAM_B2) * (gv * gv)
        nm_ref[...] = nm
        nv_ref[...] = nv
        d_ref[...] = -ADAM_LR * ((nm * c1) / (jnp.sqrt(nv * c2) + ADAM_EPS) + ADAM_WD * w_ref[...])

    row = pl.BlockSpec((tr, C), lambda i: (i, 0))
    return pl.pallas_call(
        body, name=name, grid=(R // tr,), in_specs=[row] * 4, out_specs=[row] * 3,
        out_shape=[jax.ShapeDtypeStruct((R, C), F32)] * 3, compiler_params=_params("parallel"),
    )(w, g, m, v)


def _place():
    return lax.axis_index("x"), lax.axis_index("y"), lax.axis_index("c")


_ANY = pl.BlockSpec(memory_space=pl.ANY)


def _chip_all_gather(shard, name):
    R, C = shard.shape
    half = R // 2

    def body(x_ref, out_ref, send_sems, recv_sems, local_sem):
        x, y, c = _place()
        chips = [(1 - x, y), (x, 1 - y), (1 - x, 1 - y)]
        sibling = (x, y, 1 - c)
        mine = pltpu.make_async_copy(x_ref, out_ref.at[2 * x + y], local_sem)
        mine.start()

        def rows(chip, core):
            return out_ref.at[chip, pl.ds(core * half, half)]

        def copy(k, chip, core, to, src=None):
            return pltpu.make_async_remote_copy(
                src_ref=rows(chip, core) if src is None else src, dst_ref=rows(chip, core),
                send_sem=send_sems.at[k], recv_sem=recv_sems.at[k], device_id=to, device_id_type=MESH_IDS)

        me = 2 * x + y
        first = [copy(k, me, c, (cx, cy, c), src=x_ref.at[pl.ds(c * half, half)]) for k, (cx, cy) in enumerate(chips)]
        for cp in first:
            cp.start()
        passed = [copy(3 + k, 2 * cx + cy, c, sibling) for k, (cx, cy) in enumerate(chips)]
        for k, (cx, cy) in enumerate(chips):
            copy(k, 2 * cx + cy, c, (cx, cy, c)).wait_recv()
            passed[k].start()
        for k, (cx, cy) in enumerate(chips):
            copy(3 + k, 2 * cx + cy, 1 - c, sibling).wait_recv()
        for cp in first + passed:
            cp.wait_send()
        mine.wait()

    return pl.pallas_call(
        body, name=name, in_specs=[_ANY], out_specs=_ANY,
        out_shape=jax.ShapeDtypeStruct((N_CHIPS, R, C), shard.dtype),
        scratch_shapes=[pltpu.SemaphoreType.DMA((6,)), pltpu.SemaphoreType.DMA((6,)), pltpu.SemaphoreType.DMA],
    )(shard)


def _sibling_exchange(buf, name):
    def body(x_ref, out_ref, send_sem, recv_sem):
        x, y, c = _place()
        cp = pltpu.make_async_remote_copy(
            src_ref=x_ref, dst_ref=out_ref, send_sem=send_sem, recv_sem=recv_sem,
            device_id=(x, y, 1 - c), device_id_type=MESH_IDS)
        cp.start()
        cp.wait()

    return pl.pallas_call(
        body, name=name, in_specs=[_ANY], out_specs=_ANY,
        out_shape=jax.ShapeDtypeStruct(buf.shape, buf.dtype),
        scratch_shapes=[pltpu.SemaphoreType.DMA, pltpu.SemaphoreType.DMA],
    )(buf)


def _chip_scatter(parts, name):
    _, R, C = parts.shape

    def body(p_ref, out_ref, send_sems, recv_sems):
        x, y, c = _place()
        chips = [(1 - x, y), (x, 1 - y), (1 - x, 1 - y)]

        def copy(k, slab, to):
            return pltpu.make_async_remote_copy(
                src_ref=p_ref.at[slab], dst_ref=out_ref.at[k], send_sem=send_sems.at[k], recv_sem=recv_sems.at[k],
                device_id=to, device_id_type=MESH_IDS)

        sends = [copy(k, 2 * cx + cy, (cx, cy, c)) for k, (cx, cy) in enumerate(chips)]
        for cp in sends:
            cp.start()
        for cp in sends:
            cp.wait_recv()
        for cp in sends:
            cp.wait_send()

    return pl.pallas_call(
        body, name=name, in_specs=[_ANY], out_specs=_ANY,
        out_shape=jax.ShapeDtypeStruct((3, R, C), parts.dtype),
        scratch_shapes=[pltpu.SemaphoreType.DMA((3,)), pltpu.SemaphoreType.DMA((3,))],
    )(parts)


def _all_to_all_small(vec, name):
    R, C = vec.shape

    def body(v_ref, out_ref, send_sems, recv_sems, local_sem):
        x, y, c = _place()
        me = 4 * x + 2 * y + c
        mine = pltpu.make_async_copy(v_ref, out_ref.at[me], local_sem)
        mine.start()
        flips = [(dx, dy, dc) for dx in (0, 1) for dy in (0, 1) for dc in (0, 1)][1:]

        def peer(f):
            return (x ^ f[0], y ^ f[1], c ^ f[2])

        def copy(k, slot, to):
            return pltpu.make_async_remote_copy(
                src_ref=v_ref, dst_ref=out_ref.at[slot], send_sem=send_sems.at[k], recv_sem=recv_sems.at[k],
                device_id=to, device_id_type=MESH_IDS)

        sends = [copy(k, me, peer(f)) for k, f in enumerate(flips)]
        for cp in sends:
            cp.start()
        for k, f in enumerate(flips):
            px, py, pc = peer(f)
            copy(k, 4 * px + 2 * py + pc, peer(f)).wait_recv()
        for cp in sends:
            cp.wait_send()
        mine.wait()

    return pl.pallas_call(
        body, name=name, in_specs=[_ANY], out_specs=_ANY,
        out_shape=jax.ShapeDtypeStruct((8, R, C), vec.dtype),
        scratch_shapes=[pltpu.SemaphoreType.DMA((7,)), pltpu.SemaphoreType.DMA((7,)), pltpu.SemaphoreType.DMA],
    )(vec)


def _to_heads(t, n):
    S = t.shape[0]
    return t.reshape(S, n, HEAD_DIM).transpose(1, 0, 2)


def _to_heads_t(t, n):
    S = t.shape[0]
    return t.T.reshape(n, HEAD_DIM, S)


def _from_heads_t(t):
    H, Dh, S = t.shape
    return t.reshape(H * Dh, S).T


def _t5_bucket(dist):
    max_exact = N_BUCKETS // 2
    d = np.maximum(dist, 1).astype(np.float32)
    large = max_exact + (np.log(d / max_exact) / np.log(MAX_DISTANCE / max_exact)
                         * (N_BUCKETS - max_exact)).astype(np.int32)
    large = np.minimum(large, N_BUCKETS - 1)
    return np.where(dist < max_exact, dist, large).astype(np.int32)


def _band_tables():
    qi = np.arange(BAND)[:, None]
    kj = np.arange(2 * BAND)[None, :]
    sub = qi + BAND - kj
    band = (sub >= 0) & (sub <= BAND)
    out = []
    for d in DILATIONS:
        bucket = _t5_bucket(np.clip(sub, 0, BAND) * d)
        out.append(np.where(band, bucket, -1).astype(np.int32))
    return np.stack(out)


_PACK = (("w_in", 770), ("w_out", 256), ("w_xq", 64), ("w_xk", 64), ("w_xv", 64), ("w_xo", 64),
         ("w_gate", 704), ("w_up", 704), ("w_down", 704))


def _pack(shards):
    rows = [shards[n].reshape(-1, PACK_COLS) for n, _ in _PACK]
    total = sum(r.shape[0] for r in rows)
    pad = (-total) % 128
    if pad:
        rows.append(jnp.zeros((pad, PACK_COLS), rows[0].dtype))
    return jnp.concatenate(rows, axis=0)


def _unpack(pack, shapes):
    out, r = {}, 0
    for n, _ in _PACK:
        cnt = int(np.prod(shapes[n])) // PACK_COLS
        out[n] = pack[r:r + cnt].reshape(shapes[n])
        r += cnt
    return out


_COL_SHARDED = ("w_in", "w_xo", "w_gate", "w_up")


def _full_weight(gathered, name):
    return jnp.concatenate(gathered, axis=1 if name in _COL_SHARDED else 0)


def _split_weight(full, name):
    return jnp.split(full, N_CHIPS, axis=1 if name in _COL_SHARDED else 0)


_SMALL = ("g_mix_pre", "g_mix_post", "g_xattn_pre", "g_mem", "g_xattn_post", "g_ffn_pre", "g_ffn_post")


def _pack_small(vals):
    D = vals["g_mix_pre"].shape[1]
    rows = [vals[n].reshape(1, D) for n in _SMALL]
    misc = jnp.concatenate([vals["b_f"].reshape(-1), vals["rel_bias"].reshape(-1)])
    rows.append(jnp.pad(misc, (0, D - misc.shape[0])).reshape(1, D))
    rows.append(jnp.zeros((16 - len(rows), D), F32))
    return jnp.concatenate(rows, axis=0)


def _unpack_small(pack):
    out = {n: pack[i:i + 1] for i, n in enumerate(_SMALL)}
    out["b_f"] = pack[7, 0:N_FOX_HEADS].reshape(1, N_FOX_HEADS)
    out["rel_bias"] = pack[7, N_FOX_HEADS:N_FOX_HEADS + N_BUCKETS * N_DIL_HEADS].reshape(N_BUCKETS, N_DIL_HEADS)
    return out


def kernel(x, mem, g_mix_pre, w_in, b_f, rel_bias, w_out, g_mix_post, g_xattn_pre, g_mem, w_xq, w_xk, w_xv, w_xo, g_xattn_post, g_ffn_pre, w_gate, w_up, w_down, g_ffn_post, loss_target, m_g_mix_pre, m_w_in, m_b_f, m_rel_bias, m_w_out, m_g_mix_post, m_g_xattn_pre, m_g_mem, m_w_xq, m_w_xk, m_w_xv, m_w_xo, m_g_xattn_post, m_g_ffn_pre, m_w_gate, m_w_up, m_w_down, m_g_ffn_post, v_g_mix_pre, v_w_in, v_b_f, v_rel_bias, v_w_out, v_g_mix_post, v_g_xattn_pre, v_g_mem, v_w_xq, v_w_xk, v_w_xv, v_w_xo, v_g_xattn_post, v_g_ffn_pre, v_w_gate, v_w_up, v_w_down, v_g_ffn_post):
    args = dict(locals())
    big = [n for n, _ in _PACK]
    names = ["g_mix_pre", "w_in", "b_f", "rel_bias", "w_out", "g_mix_post", "g_xattn_pre", "g_mem", "w_xq",
             "w_xk", "w_xv", "w_xo", "g_xattn_post", "g_ffn_pre", "w_gate", "w_up", "w_down", "g_ffn_post"]
    xs = x[0]
    S, D = xs.shape
    assert S % (BAND * DILATIONS[-1]) == 0
    shard_shapes = {n: args[n].shape[1:] for n in big}
    my_x, my_y, my_c = lax.axis_index("x"), lax.axis_index("y"), lax.axis_index("c")

    gathered = _chip_all_gather(_pack({n: args[n][0].astype(BF16) for n in big}), "weights_all_gather")
    per_chip = [_unpack(gathered[j], shard_shapes) for j in range(N_CHIPS)]
    W = {n: _full_weight([pc[n] for pc in per_chip], n) for n in big}
    w_fox, w_fg, w_dil = (W["w_in"][:, :3 * FOX_WIDTH], W["w_in"][:, 3 * FOX_WIDTH:3 * FOX_WIDTH + N_FOX_HEADS],
                          W["w_in"][:, 3 * FOX_WIDTH + N_FOX_HEADS:])
    w_qkv = jnp.concatenate([w_fox, w_dil], axis=1)
    w_fg_pad = jnp.pad(w_fg, ((0, 0), (0, LANES - N_FOX_HEADS)))
    F = W["w_gate"].shape[1]
    nft = F // FF_TILE
    w_gu = jnp.stack([W["w_gate"].reshape(D, nft, FF_TILE), W["w_up"].reshape(D, nft, FF_TILE)],
                     axis=2).reshape(D, 2 * F)

    h1 = _rms_fwd(xs, g_mix_pre, "rms_mix_pre")
    qkv = _mm(h1, w_qkv, "nn", BF16, "proj_qkv", tm=2048)
    fg = _mm(h1, w_fg_pad, "nn", F32, "proj_gate")
    fg_t = fg[:, :N_FOX_HEADS].T
    b_col = b_f.reshape(N_FOX_HEADS, 1)
    c_t = _forget_fwd(fg_t, b_col, "forget_cumsum")
    fq_s, fk_s, fv_s = (qkv[:, i * FOX_WIDTH:(i + 1) * FOX_WIDTH] for i in range(3))
    fqt, fvt = _to_heads_t(fq_s, N_FOX_HEADS), _to_heads_t(fv_s, N_FOX_HEADS)
    unit = jnp.full((N_FOX_HEADS, S), 1.0, BF16)
    inv_scale = jnp.full((N_FOX_HEADS, S), 1.0 / QK_SCALE, BF16)
    ka = _lanes_operand(_to_heads(fk_s, N_FOX_HEADS), list(_split3(-c_t)) + [unit] * EXTRA)
    qa_f = _rows_operand(fqt, [inv_scale] * EXTRA)
    o_fox_t, lse_fox = _fox_fwd(qa_f, ka, _with_ones(fvt), "fox_fwd")

    bucket_map = _band_tables()
    onehot = (jnp.asarray(bucket_map)[..., None] == jnp.arange(N_BUCKETS)).astype(F32)
    bias_tab = jnp.einsum("pqkb,bh->phkq", onehot, rel_bias, precision=lax.Precision.HIGHEST)
    bias_tab = jnp.where(jnp.asarray(bucket_map.transpose(0, 2, 1) >= 0)[:, None], bias_tab, NEG)
    bias_t = bias_tab.reshape(3, HEAD_PAIRS, 2, 2 * BAND, BAND).transpose(0, 1, 3, 2, 4).reshape(
        3, HEAD_PAIRS, 2 * BAND, 2 * BAND)
    views = [(qkv.reshape(1, S, qkv.shape[1]), DIL_Q_BLOCK)] + [
        (_to_residues(qkv, 1, 3 * DIL_WIDTH, d, f"dilated_qkv_residues_{d}"), 0) for d in DILATIONS[1:]]

    def to_tok(stat, d):
        return stat.reshape(N_DIL_HEADS, d, S // d).swapaxes(1, 2).reshape(N_DIL_HEADS, S)

    def to_perm(stat, d):
        return stat.reshape(N_DIL_HEADS, S // d, d).swapaxes(1, 2).reshape(HEAD_PAIRS, 2, S)

    def tok_or_res(t):
        return t.reshape(t.shape[1:]) if t.shape[0] == 1 else t

    lse_tok = jnp.stack([to_tok(_dil_lse(*views[p], bias_t[p], f"dilated_lse_{d}"), d)
                         for p, d in enumerate(DILATIONS)])
    lse_joint = _lse_join(lse_tok, "dilated_lse_join")
    lse_perm = [to_perm(lse_joint, d) for d in DILATIONS]
    o_dil = [tok_or_res(_dil_out(*views[p], bias_t[p], lse_perm[p], f"dilated_out_{d}"))
             for p, d in enumerate(DILATIONS)]
    o_cat = _sum_cast_cols([[_from_heads_t(o_fox_t)]] + [[o] for o in o_dil], BF16, "mixer_out_cat")
    w_out_b = W["w_out"]
    w_out_cat = jnp.concatenate([w_out_b[:FOX_WIDTH]] + [w_out_b[FOX_WIDTH:]] * 3, axis=0)
    a = _mm(o_cat, w_out_cat, "nn", F32, "proj_out", tm=2048, tk=2048)
    x1, h2 = _resid_norm(xs, a, g_mix_post, g_xattn_pre, "resid_mix")

    hm = _rms_fwd(mem[0], g_mem, "rms_mem")
    q2 = _mm(h2, W["w_xq"], "nn", BF16, "xattn_q")
    w_xkv = jnp.concatenate([W["w_xk"], W["w_xv"]], axis=1)
    kvm = _mm(hm, w_xkv, "nn", BF16, "xattn_kv")
    MW = N_MEM_HEADS * HEAD_DIM
    oc, lse_mem = _mem_fwd(q2, kvm, "xattn_fwd")
    y2 = _mm(oc, W["w_xo"], "nn", F32, "xattn_o")
    x2, h3 = _resid_norm(x1, y2, g_xattn_post, g_ffn_pre, "resid_xattn")

    gu, act = _ffn_up(h3, w_gu, "ffn_up")
    y3 = _mm(act, W["w_down"], "nn", F32, "ffn_down", tk=2816)
    dx3, loss_tile = _final_loss(x2, y3, g_ffn_post, loss_target[0], "final_loss")

    grads = {}
    small = {}
    _, dy3_b, dg = _rms_bwd(y3, g_ffn_post, dx3, None, "bwd_norm_ffn_post", want=("bf16",))
    small["g_ffn_post"] = dg[0:1]
    grads["w_down"] = _mm(act, dy3_b, "tn", F32, "grad_w_down", tm=1408)
    dgu = _ffn_dact(dy3_b, W["w_down"], gu, "ffn_dact")
    dw_gu = _mm(h3, dgu, "tn", F32, "grad_w_gu", tn=1408).reshape(D, nft, 2, FF_TILE)
    grads["w_gate"], grads["w_up"] = dw_gu[:, :, 0].reshape(D, F), dw_gu[:, :, 1].reshape(D, F)
    dh3 = _mm(dgu, w_gu, "nt", F32, "bwd_ffn_in", tm=2048, tk=1408)
    dx2, _, dg = _rms_bwd(x2, g_ffn_pre, dh3, dx3, "bwd_norm_ffn_pre", want=("f32",))
    small["g_ffn_pre"] = dg[0:1]

    _, dy2_b, dg = _rms_bwd(y2, g_xattn_post, dx2, None, "bwd_norm_xattn_post", want=("bf16",))
    small["g_xattn_post"] = dg[0:1]
    grads["w_xo"] = _mm(oc, dy2_b, "tn", F32, "grad_w_xo")
    doc = _mm(dy2_b, W["w_xo"], "nt", BF16, "bwd_xattn_o")
    delta_mem = _head_rowdot(doc, [oc], "xattn_delta")[:, :N_MEM_HEADS].T.reshape(N_MEM_HEADS // 2, 2, S)
    dq2, dkm, dvm = _mem_bwd(q2, kvm, doc, lse_mem, delta_mem, "xattn_bwd")
    dkvm = jnp.concatenate([dkm, dvm], axis=1).astype(BF16)
    grads["w_xq"] = _mm(h2, dq2, "tn", F32, "grad_w_xq")
    dw_xkv = _mm(hm, dkvm, "tn", F32, "grad_w_xkv")
    grads["w_xk"], grads["w_xv"] = dw_xkv[:, :MW], dw_xkv[:, MW:]
    dhm = _mm(dkvm, w_xkv, "nt", F32, "bwd_xattn_kv")
    _, _, dg = _rms_bwd(mem[0], g_mem, dhm, None, "bwd_norm_mem", want=())
    small["g_mem"] = dg[0:1]
    dh2 = _mm(dq2, W["w_xq"], "nt", F32, "bwd_xattn_q")
    dx1, _, dg = _rms_bwd(x1, g_xattn_pre, dh2, dx2, "bwd_norm_xattn_pre", want=("f32",))
    small["g_xattn_pre"] = dg[0:1]

    _, da_b, dg = _rms_bwd(a, g_mix_post, dx1, None, "bwd_norm_mix_post", want=("bf16",))
    small["g_mix_post"] = dg[0:1]
    dw_out_cat = _mm(o_cat, da_b, "tn", F32, "grad_w_out")
    dw_out_dil = _add_n([dw_out_cat[FOX_WIDTH + p * DIL_WIDTH:FOX_WIDTH + (p + 1) * DIL_WIDTH] for p in range(3)],
                        "grad_w_out_dil")
    grads["w_out"] = jnp.concatenate([dw_out_cat[:FOX_WIDTH], dw_out_dil], axis=0)
    do = _mm(da_b, w_out_b, "nt", BF16, "bwd_proj_out")
    do_fox, do_dil = do[:, :FOX_WIDTH], do[:, FOX_WIDTH:]

    delta_fox = _head_rowdot(do_fox, [o_cat[:, :FOX_WIDTH]], "fox_delta")[:, :N_FOX_HEADS].T
    qa_b = lax.dynamic_update_slice(qa_f, jnp.stack(_split3(lse_fox[:, 0] * (-1.0 / QK_SCALE)), axis=1),
                                    (0, HEAD_DIM + EXTRA, 0))
    va = _lanes_operand(_to_heads(fv_s, N_FOX_HEADS), [unit] * EXTRA)
    doa = _rows_operand(_to_heads_t(do_fox, N_FOX_HEADS), list(_split3(-delta_fox)))
    dq_aug, dk_aug, dvf = _fox_bwd(qa_b, ka, ka.transpose(0, 2, 1), va, doa, "fox_bwd")
    dqf, dkf = dq_aug[:, :HEAD_DIM], dk_aug[:, :HEAD_DIM]
    dfg_t, db_f = _forget_bwd(fg_t, b_col, dq_aug[:, HEAD_DIM + EXTRA], dk_aug[:, HEAD_DIM], "forget_bwd")

    delta_dil = _head_rowdot(do_dil, o_dil, "dilated_delta")[:, :N_DIL_HEADS].T
    do_res = [do_dil.reshape(1, S, DIL_WIDTH)] + [
        _to_residues(do, 1, DIL_WIDTH, d, f"dilated_do_residues_{d}") for d in DILATIONS[1:]]
    dil_grads = [_dil_bwd(*views[p], do_res[p], bias_t[p], lse_perm[p], to_perm(delta_dil, d), f"dilated_bwd_{d}")
                 for p, d in enumerate(DILATIONS)]
    dbias_t = jnp.stack([g[3].reshape(HEAD_PAIRS, 2 * BAND, 2, BAND).transpose(0, 2, 1, 3).reshape(
        N_DIL_HEADS, 2 * BAND, BAND) for g in dil_grads])
    d_rel = _bucket_reduce(dbias_t, jnp.asarray(bucket_map.transpose(0, 2, 1)), "rel_bias_grad")[:, :N_DIL_HEADS]
    dfg_pad = jnp.pad(dfg_t.T, ((0, 0), (0, LANES - N_FOX_HEADS))).astype(BF16)
    dcat = _sum_cast_cols([[_from_heads_t(dqf)], [_from_heads_t(dkf)], [_from_heads_t(dvf)]]
                          + [[tok_or_res(g[j]) for g in dil_grads] for j in range(3)],
                          BF16, "dqkv_assemble", tail=dfg_pad)
    dw_cat = _mm(h1, dcat, "tn", F32, "grad_w_qkv", tm=512, tn=3200)
    n_qkv = 3 * (FOX_WIDTH + DIL_WIDTH)
    grads["w_in"] = jnp.concatenate([dw_cat[:, :3 * FOX_WIDTH], dw_cat[:, n_qkv:n_qkv + N_FOX_HEADS],
                                     dw_cat[:, 3 * FOX_WIDTH:n_qkv]], axis=1)
    w_cat = jnp.concatenate([w_qkv, w_fg_pad], axis=1)
    dh1 = _mm(dcat, w_cat, "nt", F32, "bwd_proj_in", tk=3200)
    grad_x, _, dg = _rms_bwd(xs, g_mix_pre, dh1, dx1, "bwd_norm_mix_pre", want=("f32",))
    small["g_mix_pre"] = dg[0:1]
    small["b_f"] = db_f[:, 0].reshape(1, N_FOX_HEADS)
    small["rel_bias"] = d_rel

    split = {n: _split_weight(grads[n], n) for n in big}
    parts = jnp.stack([_pack({n: split[n][j].astype(BF16) for n in big}) for j in range(N_CHIPS)])
    R = parts.shape[1]
    half = R // 2
    keep = lax.dynamic_slice_in_dim(parts, my_c * half, half, axis=1)
    give = lax.dynamic_slice_in_dim(parts, (1 - my_c) * half, half, axis=1)
    got = _sibling_exchange(give, "grads_to_sibling")
    chip_sum = _add_n([keep.reshape(-1, PACK_COLS), got.reshape(-1, PACK_COLS)], "grads_add_sibling")
    chip_sum = chip_sum.reshape(N_CHIPS, half, PACK_COLS)
    my_chip = 2 * my_x + my_y
    from_chips = _chip_scatter(chip_sum.astype(BF16), "grads_to_chips")
    own = lax.dynamic_index_in_dim(chip_sum, my_chip, axis=0, keepdims=False)
    g_half = _add_n([own, from_chips[0], from_chips[1], from_chips[2]], "grads_add_chips")
    other_half = _sibling_exchange(g_half, "grads_share_sibling")
    g_pack = jnp.where(my_c == 0, jnp.concatenate([g_half, other_half]), jnp.concatenate([other_half, g_half]))

    small_pack = _pack_small(small)
    small_pack = small_pack.at[8, 0].set(loss_tile[0, 0])
    everyone = _all_to_all_small(small_pack, "small_all_gather")
    small_sum = _add_n([everyone[i] for i in range(8)], "small_sum")
    loss = small_sum[8, 0]
    g_small = _unpack_small(small_sum)

    outs = {"grad": _unpack(g_pack, shard_shapes), "delta": {}, "new_m": {}, "new_v": {}}
    for n in big:
        outs["delta"][n], outs["new_m"][n], outs["new_v"][n] = _adamw(
            args[n][0], outs["grad"][n], args["m_" + n][0], args["v_" + n][0], f"adamw_{n}")
    sw = _pack_small({n: args[n] for n in _SMALL + ("b_f", "rel_bias")})
    sm = _pack_small({n: args["m_" + n] for n in _SMALL + ("b_f", "rel_bias")})
    sv = _pack_small({n: args["v_" + n] for n in _SMALL + ("b_f", "rel_bias")})
    sd, snm, snv = _adamw(sw, small_sum.at[8, 0].set(0.0), sm, sv, "adamw_small")
    souts = {"grad": g_small, "delta": _unpack_small(sd), "new_m": _unpack_small(snm), "new_v": _unpack_small(snv)}

    def leaf(kind, n):
        if n in souts[kind]:
            return souts[kind][n].reshape(args[n].shape)
        return outs[kind][n].reshape(args[n].shape)

    result = [loss, grad_x.reshape(x.shape)]
    for kind in ("grad", "delta", "new_m", "new_v"):
        result += [leaf(kind, n) for n in names]
    return tuple(result)
```

```python
import numpy as np
import jax
import jax.numpy as jnp
from jax import lax
from jax.experimental import pallas as pl
from jax.experimental.pallas import tpu as pltpu

F32 = jnp.float32
BF16 = jnp.bfloat16
MESH_IDS = pl.DeviceIdType.MESH

LANES = 128
HEAD_DIM = 64
N_FOX_HEADS = 8
N_DIL_HEADS = 8
N_MEM_HEADS = 4
FOX_WIDTH = N_FOX_HEADS * HEAD_DIM
DIL_WIDTH = N_DIL_HEADS * HEAD_DIM
DILATIONS = (1, 4, 16)
BAND = 128
BAND_CHUNK_MAX = 16 * BAND
N_BUCKETS = 32
MAX_DISTANCE = 2048
QK_SCALE = HEAD_DIM ** -0.5
RMS_EPS = 1e-6
NEG = -1e30
VMEM_LIMIT = 56 << 20

ADAM_LR = 0.001
ADAM_B1 = 0.9
ADAM_B2 = 0.999
ADAM_EPS = 1e-08
ADAM_WD = 0.01
ADAM_STEP = 10

N_CHIPS = 4
PACK_COLS = 1024


def _params(*sem):
    return pltpu.CompilerParams(dimension_semantics=sem, vmem_limit_bytes=VMEM_LIMIT)


def _fit(n, cap):
    if n <= cap:
        return n
    t = (cap // LANES) * LANES
    while t >= LANES:
        if n % t == 0:
            return t
        t -= LANES
    raise ValueError(f"no lane-aligned tile for {n} under {cap}")


def _dot(a, b, dims):
    return lax.dot_general(a, b, (dims, ((), ())), preferred_element_type=F32)


_NN = ((1,), (0,))
_NT = ((1,), (1,))
_TN = ((0,), (0,))


def _mm(a, b, mode, out_dtype, name, tm=1024, tn=1024, tk=1024):
    if mode == "nn":
        (M, K), N = a.shape, b.shape[1]
    elif mode == "nt":
        (M, K), N = a.shape, b.shape[0]
    else:
        (K, M), N = a.shape, b.shape[1]
    tm, tn, tk = _fit(M, tm), _fit(N, tn), _fit(K, tk)
    nk = K // tk
    if mode == "tn":
        a_spec = pl.BlockSpec((tk, tm), lambda i, j, k: (k, i))
    else:
        a_spec = pl.BlockSpec((tm, tk), lambda i, j, k: (i, k))
    if mode == "nt":
        b_spec = pl.BlockSpec((tn, tk), lambda i, j, k: (j, k))
    else:
        b_spec = pl.BlockSpec((tk, tn), lambda i, j, k: (k, j))
    dims = {"nn": _NN, "nt": _NT, "tn": _TN}[mode]

    def body(a_ref, b_ref, o_ref, *acc):
        prod = _dot(a_ref[...].astype(BF16), b_ref[...].astype(BF16), dims)
        if nk == 1:
            o_ref[...] = prod.astype(o_ref.dtype)
            return
        acc_ref, k = acc[0], pl.program_id(2)

        @pl.when(k == 0)
        def _():
            acc_ref[...] = prod

        @pl.when(k > 0)
        def _():
            acc_ref[...] += prod

        @pl.when(k == nk - 1)
        def _():
            o_ref[...] = acc_ref[...].astype(o_ref.dtype)

    return pl.pallas_call(
        body, name=name, grid=(M // tm, N // tn, nk),
        in_specs=[a_spec, b_spec],
        out_specs=pl.BlockSpec((tm, tn), lambda i, j, k: (i, j)),
        out_shape=jax.ShapeDtypeStruct((M, N), out_dtype),
        scratch_shapes=[pltpu.VMEM((tm, tn), F32)] if nk > 1 else [],
        compiler_params=_params("parallel", "parallel", "arbitrary"),
    )(a, b)


def _rms_rows(x):
    return lax.rsqrt(jnp.mean(x * x, axis=-1, keepdims=True) + RMS_EPS)


def _rms_fwd(x, g, name, tr=512):
    S, D = x.shape
    tr = _fit(S, tr)

    def body(x_ref, g_ref, h_ref):
        xv = x_ref[...]
        h_ref[...] = (xv * _rms_rows(xv) * g_ref[...]).astype(BF16)

    return pl.pallas_call(
        body, name=name, grid=(S // tr,),
        in_specs=[pl.BlockSpec((tr, D), lambda i: (i, 0)), pl.BlockSpec((1, D), lambda i: (0, 0))],
        out_specs=pl.BlockSpec((tr, D), lambda i: (i, 0)),
        out_shape=jax.ShapeDtypeStruct((S, D), BF16),
        compiler_params=_params("parallel"),
    )(x, g)


def _resid_norm(xres, y, g_post, g_next, name, tr=1024):
    S, D = xres.shape
    tr = _fit(S, tr)

    def body(x_ref, y_ref, gp_ref, gn_ref, xn_ref, h_ref):
        yv = y_ref[...]
        xn = x_ref[...] + yv * _rms_rows(yv) * gp_ref[...]
        xn_ref[...] = xn
        h_ref[...] = (xn * _rms_rows(xn) * gn_ref[...]).astype(BF16)

    row = pl.BlockSpec((tr, D), lambda i: (i, 0))
    vec = pl.BlockSpec((1, D), lambda i: (0, 0))
    return pl.pallas_call(
        body, name=name, grid=(S // tr,),
        in_specs=[row, row, vec, vec], out_specs=[row, row],
        out_shape=[jax.ShapeDtypeStruct((S, D), F32), jax.ShapeDtypeStruct((S, D), BF16)],
        compiler_params=_params("parallel"),
    )(xres, y, g_post, g_next)


def _final_loss(xres, y, g_post, target, name, tr=1024):
    S, D = xres.shape
    tr = _fit(S, tr)

    def body(x_ref, y_ref, gp_ref, t_ref, d_ref, loss_ref):
        i = pl.program_id(0)
        yv = y_ref[...]
        err = x_ref[...] + yv * _rms_rows(yv) * gp_ref[...] - t_ref[...]
        d_ref[...] = err * (1.0 / D)

        @pl.when(i == 0)
        def _():
            loss_ref[...] = jnp.zeros_like(loss_ref)

        part = jnp.sum(jnp.sum(err * err, axis=1, keepdims=True), axis=0, keepdims=True)
        loss_ref[...] += jnp.broadcast_to(part * (0.5 / D), loss_ref.shape)

    row = pl.BlockSpec((tr, D), lambda i: (i, 0))
    vec = pl.BlockSpec((1, D), lambda i: (0, 0))
    return pl.pallas_call(
        body, name=name, grid=(S // tr,),
        in_specs=[row, row, vec, row],
        out_specs=[row, pl.BlockSpec((8, LANES), lambda i: (0, 0))],
        out_shape=[jax.ShapeDtypeStruct((S, D), F32), jax.ShapeDtypeStruct((8, LANES), F32)],
        compiler_params=_params("arbitrary"),
    )(xres, y, g_post, target)


def _rms_bwd(xin, g, dy, dres, name, want=("f32", "bf16"), tr=1024):
    S, D = xin.shape
    tr = _fit(S, tr)
    has_res = dres is not None

    def body(*refs):
        refs = list(refs)
        dg_ref = refs.pop()
        dxb_ref = refs.pop() if "bf16" in want else None
        dx_ref = refs.pop() if "f32" in want else None
        dr_ref = refs.pop() if has_res else None
        x_ref, g_ref, dy_ref = refs
        i = pl.program_id(0)
        xv = x_ref[...]
        dyv = dy_ref[...].astype(F32)
        xhat = xv * _rms_rows(xv)
        dxhat = dyv * g_ref[...]
        r = _rms_rows(xv)
        dx = r * (dxhat - xhat * jnp.mean(dxhat * xhat, axis=-1, keepdims=True))
        if has_res:
            dx = dx + dr_ref[...]
        if dx_ref is not None:
            dx_ref[...] = dx
        if dxb_ref is not None:
            dxb_ref[...] = dx.astype(BF16)

        @pl.when(i == 0)
        def _():
            dg_ref[...] = jnp.zeros_like(dg_ref)

        dg_ref[...] += jnp.broadcast_to(jnp.sum(dyv * xhat, axis=0, keepdims=True), dg_ref.shape)

    row = pl.BlockSpec((tr, D), lambda i: (i, 0))
    vec = pl.BlockSpec((1, D), lambda i: (0, 0))
    acc = pl.BlockSpec((8, D), lambda i: (0, 0))
    ins = [xin, g, dy] + ([dres] if has_res else [])
    dtypes = [dt for key, dt in (("f32", F32), ("bf16", BF16)) if key in want]
    outs = pl.pallas_call(
        body, name=name, grid=(S // tr,),
        in_specs=[row, vec, row] + ([row] if has_res else []),
        out_specs=[row] * len(dtypes) + [acc],
        out_shape=[jax.ShapeDtypeStruct((S, D), dt) for dt in dtypes] + [jax.ShapeDtypeStruct((8, D), F32)],
        compiler_params=_params("arbitrary"),
    )(*ins)
    by_key = dict(zip([key for key in ("f32", "bf16") if key in want], outs[:-1]))
    return by_key.get("f32"), by_key.get("bf16"), outs[-1]


def _tri(n, upper):
    r = lax.broadcasted_iota(jnp.int32, (n, n), 0)
    c = lax.broadcasted_iota(jnp.int32, (n, n), 1)
    return jnp.where((r <= c) if upper else (r >= c), 1.0, 0.0).astype(F32)


def _forget_fwd(fg_t, b_col, name, ts=512):
    H, S = fg_t.shape
    ts = _fit(S, ts)

    def body(f_ref, b_ref, c_ref, carry_ref):
        i = pl.program_id(0)

        @pl.when(i == 0)
        def _():
            carry_ref[...] = jnp.zeros_like(carry_ref)

        z = f_ref[...] + b_ref[...]
        logf = jnp.minimum(z, 0.0) - jnp.log(1.0 + jnp.exp(-jnp.abs(z)))
        run = lax.dot_general(logf, _tri(ts, True), (_NN, ((), ())), precision=lax.Precision.HIGHEST,
                              preferred_element_type=F32) + carry_ref[:, 0:1]
        c_ref[...] = run
        carry_ref[...] = jnp.broadcast_to(
            carry_ref[:, 0:1] + jnp.sum(logf, axis=1, keepdims=True), carry_ref.shape)

    return pl.pallas_call(
        body, name=name, grid=(S // ts,),
        in_specs=[pl.BlockSpec((H, ts), lambda i: (0, i)), pl.BlockSpec((H, 1), lambda i: (0, 0))],
        out_specs=pl.BlockSpec((H, ts), lambda i: (0, i)),
        out_shape=jax.ShapeDtypeStruct((H, S), F32),
        scratch_shapes=[pltpu.VMEM((H, LANES), F32)],
        compiler_params=_params("arbitrary"),
    )(fg_t, b_col)


def _forget_bwd(fg_t, b_col, dc_plus, dc_minus, name, ts=512):
    H, S = fg_t.shape
    ts = _fit(S, ts)
    nb = S // ts

    def body(f_ref, b_ref, dcp_ref, dcm_ref, df_ref, db_ref, carry_ref):
        i = pl.program_id(0)

        @pl.when(i == 0)
        def _():
            carry_ref[...] = jnp.zeros_like(carry_ref)
            db_ref[...] = jnp.zeros_like(db_ref)

        dc = dcp_ref[...] - dcm_ref[...]
        suffix = lax.dot_general(dc, _tri(ts, False), (_NN, ((), ())), precision=lax.Precision.HIGHEST,
                                 preferred_element_type=F32) + carry_ref[:, 0:1]
        z = f_ref[...] + b_ref[...]
        sig_neg = 1.0 / (1.0 + jnp.exp(z))
        df = suffix * sig_neg
        df_ref[...] = df
        carry_ref[...] = jnp.broadcast_to(
            carry_ref[:, 0:1] + jnp.sum(dc, axis=1, keepdims=True), carry_ref.shape)
        db_ref[...] += jnp.broadcast_to(jnp.sum(df, axis=1, keepdims=True), db_ref.shape)

    rev = pl.BlockSpec((H, ts), lambda i: (0, nb - 1 - i))
    return pl.pallas_call(
        body, name=name, grid=(nb,),
        in_specs=[rev, pl.BlockSpec((H, 1), lambda i: (0, 0)), rev, rev],
        out_specs=[rev, pl.BlockSpec((H, LANES), lambda i: (0, 0))],
        out_shape=[jax.ShapeDtypeStruct((H, S), F32), jax.ShapeDtypeStruct((H, LANES), F32)],
        scratch_shapes=[pltpu.VMEM((H, LANES), F32)],
        compiler_params=_params("arbitrary"),
    )(fg_t, b_col, dc_plus, dc_minus)


ONES_ROWS = 16
EXTRA = 3


def _split3(x):
    hi = lax.reduce_precision(x, 8, 7)
    mid = lax.reduce_precision(x - hi, 8, 7)
    lo = lax.reduce_precision(x - hi - mid, 8, 7)
    return hi.astype(BF16), mid.astype(BF16), lo.astype(BF16)


def _lanes_operand(t, extras):
    block = jnp.pad(jnp.stack(extras, axis=-1), ((0, 0), (0, 0), (0, LANES - HEAD_DIM - len(extras))))
    return jnp.concatenate([t, block], axis=-1)


def _rows_operand(t, extras):
    block = jnp.pad(jnp.stack(extras, axis=1), ((0, 0), (0, LANES - HEAD_DIM - len(extras)), (0, 0)))
    return jnp.concatenate([t, block], axis=1)


def _with_ones(t):
    return jnp.concatenate([t, jnp.ones((t.shape[0], ONES_ROWS, t.shape[2]), t.dtype)], axis=1)


def _fox_fwd(qa, ka, vt, name, tq=512, tk=1024):
    H, _, S = qa.shape
    Dh = HEAD_DIM
    tk = _fit(S, tk)
    tq = _fit(tk, tq)
    ratio = tk // tq

    def body(qa_ref, ka_ref, vt_ref, o_ref, lse_ref, m_ref, acc_ref, sa_ref, sb_ref, ta_ref, tb_ref):
        i = pl.program_id(1)
        qv = qa_ref[...] * QK_SCALE
        m_ref[...] = jnp.full_like(m_ref, NEG)
        acc_ref[...] = jnp.zeros_like(acc_ref)
        n = i // ratio
        q_off = (i - n * ratio) * tq

        def scores(j, s_ref, t_ref, diagonal):
            off = pl.multiple_of(j * tk, LANES)
            s = _dot(ka_ref[pl.ds(off, tk), :], qv, _NN)
            if diagonal:
                key = lax.broadcasted_iota(jnp.int32, (tk, tq), 0)
                qry = lax.broadcasted_iota(jnp.int32, (tk, tq), 1) + q_off
                s = jnp.where(key <= qry, s, NEG)
            s_ref[...] = s
            t_ref[...] = jnp.max(s, axis=0, keepdims=True)

        def absorb(j, s_ref, t_ref):
            off = pl.multiple_of(j * tk, LANES)
            m_old = m_ref[...]
            m_new = jnp.maximum(m_old, t_ref[...])
            p = jnp.exp(s_ref[...] - m_new)
            alpha = jnp.exp(m_old - m_new)
            acc_ref[...] = alpha * acc_ref[...] + _dot(vt_ref[:, pl.ds(off, tk)], p.astype(BF16), _NN)
            m_ref[...] = m_new

        scores(n, sa_ref, ta_ref, True)

        def loop_body(jj, carry):
            scores(2 * jj, sb_ref, tb_ref, False)
            absorb(jnp.where(jj == 0, n, 2 * jj - 1), sa_ref, ta_ref)
            scores(2 * jj + 1, sa_ref, ta_ref, False)
            absorb(2 * jj, sb_ref, tb_ref)
            return carry

        pairs = n // 2
        lax.fori_loop(0, pairs, loop_body, 0)
        held = jnp.where(pairs == 0, n, 2 * pairs - 1)

        @pl.when(n % 2 == 1)
        def _():
            scores(n - 1, sb_ref, tb_ref, False)
            absorb(held, sa_ref, ta_ref)
            absorb(n - 1, sb_ref, tb_ref)

        @pl.when(n % 2 == 0)
        def _():
            absorb(held, sa_ref, ta_ref)

        l = acc_ref[Dh:Dh + 1, :]
        o_ref[...] = acc_ref[0:Dh, :] / l
        lse_ref[...] = m_ref[...] + jnp.log(l)

    return pl.pallas_call(
        body, name=name, grid=(H, S // tq),
        in_specs=[pl.BlockSpec((None, LANES, tq), lambda h, i: (h, 0, i)),
                  pl.BlockSpec((None, S, LANES), lambda h, i: (h, 0, 0)),
                  pl.BlockSpec((None, Dh + ONES_ROWS, S), lambda h, i: (h, 0, 0))],
        out_specs=[pl.BlockSpec((None, Dh, tq), lambda h, i: (h, 0, i)),
                   pl.BlockSpec((None, 1, tq), lambda h, i: (h, 0, i))],
        out_shape=[jax.ShapeDtypeStruct((H, Dh, S), F32), jax.ShapeDtypeStruct((H, 1, S), F32)],
        scratch_shapes=[pltpu.VMEM((1, tq), F32), pltpu.VMEM((Dh + ONES_ROWS, tq), F32),
                        pltpu.VMEM((tk, tq), F32), pltpu.VMEM((tk, tq), F32),
                        pltpu.VMEM((1, tq), F32), pltpu.VMEM((1, tq), F32)],
        compiler_params=_params("parallel", "arbitrary"),
    )(qa, ka, vt)


def _fox_bwd(qa, ka, kta, va, doa, name, tq=1024, tk=512):
    H, _, S = qa.shape
    Dh, Da = HEAD_DIM, HEAD_DIM + ONES_ROWS
    tq = _fit(S, tq)
    tk = _fit(tq, tk)
    ratio = tq // tk
    nq = S // tq
    nk = S // tk

    def body(ka_ref, kta_ref, va_ref, qa_ref, doa_ref, dqt_ref, dkt_ref, dvt_ref, dka_ref, dva_ref):
        j = pl.program_id(1)

        @pl.when(j == 0)
        def _():
            dqt_ref[...] = jnp.zeros_like(dqt_ref)

        kv = ka_ref[...]
        ktv = kta_ref[0:Da, :]
        vv = va_ref[...]
        dka_ref[...] = jnp.zeros_like(dka_ref)
        dva_ref[...] = jnp.zeros_like(dva_ref)
        i_diag = j // ratio
        k_off = (j - i_diag * ratio) * tk

        def step(i, diagonal):
            off = pl.multiple_of(i * tq, LANES)
            qv = qa_ref[:, pl.ds(off, tq)] * QK_SCALE
            dov = doa_ref[:, pl.ds(off, tq)]
            e = _dot(kv, qv, _NN)
            if diagonal:
                key = lax.broadcasted_iota(jnp.int32, (tk, tq), 0) + k_off
                qry = lax.broadcasted_iota(jnp.int32, (tk, tq), 1)
                e = jnp.where(key <= qry, e, NEG)
            p_t = jnp.exp(e)
            dva_ref[...] += _dot(dov[0:Dh, :], p_t.astype(BF16), _NT)
            ds_b = (p_t * _dot(vv, dov, _NN)).astype(BF16)
            dka_ref[...] += _dot(qv[0:Da, :], ds_b, _NT)
            dqt_ref[:, pl.ds(off, tq)] += _dot(ktv, ds_b, _NN)

        step(i_diag, True)

        def loop_body(i, carry):
            step(i, False)
            return carry

        lax.fori_loop(i_diag + 1, nq, loop_body, 0)
        dkt_ref[...] = dka_ref[...]
        dvt_ref[...] = dva_ref[...]

        @pl.when(j == nk - 1)
        def _():
            dqt_ref[0:Dh, :] = dqt_ref[0:Dh, :] * QK_SCALE

    lanes_tile = pl.BlockSpec((None, tk, LANES), lambda h, j: (h, j, 0))
    rows_tile = pl.BlockSpec((None, LANES, tk), lambda h, j: (h, 0, j))
    rows_full = pl.BlockSpec((None, LANES, S), lambda h, j: (h, 0, 0))
    return pl.pallas_call(
        body, name=name, grid=(H, nk),
        in_specs=[lanes_tile, rows_tile, lanes_tile, rows_full, rows_full],
        out_specs=[pl.BlockSpec((None, Da, S), lambda h, j: (h, 0, 0)),
                   pl.BlockSpec((None, Da, tk), lambda h, j: (h, 0, j)),
                   pl.BlockSpec((None, Dh, tk), lambda h, j: (h, 0, j))],
        out_shape=[jax.ShapeDtypeStruct((H, Da, S), F32), jax.ShapeDtypeStruct((H, Da, S), F32),
                   jax.ShapeDtypeStruct((H, Dh, S), F32)],
        scratch_shapes=[pltpu.VMEM((Da, tk), F32), pltpu.VMEM((Dh, tk), F32)],
        compiler_params=_params("parallel", "arbitrary"),
    )(ka, kta, va, qa, doa)


DIL_Q_BLOCK = 3 * FOX_WIDTH // LANES
HEAD_PAIRS = N_DIL_HEADS // 2
PAIR_BLOCKS = DIL_WIDTH // LANES


def _band_geometry(S, d):
    L = S // d
    chunk = min(BAND_CHUNK_MAX, L)
    assert L % chunk == 0 and chunk % BAND == 0
    return L, chunk, chunk // BAND, L // chunk


def _band_in_specs(S, d, base):
    L, chunk, nb, _ = _band_geometry(S, d)

    def col(kind):
        return lambda hp, r, i: (r, i, base + kind * PAIR_BLOCKS + hp)

    def col_prev(kind):
        return lambda hp, r, i: (r, jnp.maximum(i * nb - 1, 0), base + kind * PAIR_BLOCKS + hp)

    main = [pl.BlockSpec((None, chunk, LANES), col(kind)) for kind in range(3)]
    prev = [pl.BlockSpec((None, BAND, LANES), col_prev(kind)) for kind in range(3)]
    bias = pl.BlockSpec((None, 2 * BAND, 2 * BAND), lambda hp, r, i: (hp, 0, 0))
    stat = pl.BlockSpec((None, 2, chunk), lambda hp, r, i: (hp, 0, r * (L // chunk) + i))
    tok = pl.BlockSpec((None, chunk, LANES), lambda hp, r, i: (r, i, hp))
    return main, prev, bias, stat, tok


def _to_residues(x, col_block, width, d, name, tr=512):
    S = x.shape[0]
    tr = _fit(S, tr)

    def body(x_ref, o_ref, tmp_ref):
        for j in range(width // LANES):
            cols = slice(j * LANES, (j + 1) * LANES)
            tmp_ref[j] = x_ref[:, cols].astype(F32)
            for r in range(d):
                o_ref[r, :, cols] = tmp_ref[j, pl.ds(r, tr // d, stride=d), :].astype(o_ref.dtype)

    return pl.pallas_call(
        body, name=name, grid=(S // tr,),
        in_specs=[pl.BlockSpec((tr, width), lambda i: (i, col_block))],
        out_specs=pl.BlockSpec((d, tr // d, width), lambda i: (0, i, 0)),
        out_shape=jax.ShapeDtypeStruct((d, S // d, width), x.dtype),
        scratch_shapes=[pltpu.VMEM((width // LANES, tr, LANES), F32)],
        compiler_params=_params("parallel"),
    )(x)


def _token_rows(ref, cols, tmp_ref):
    if len(ref.shape) == 2:
        return ref[:, cols].astype(F32)
    d, rows = ref.shape[0], ref.shape[1]
    for r in range(d):
        tmp_ref[pl.ds(r, rows, stride=d), :] = ref[r, :, cols].astype(F32)
    return tmp_ref[...]


def _row_spec(t, tr):
    if t.ndim == 2:
        return pl.BlockSpec((tr, t.shape[1]), lambda i: (i, 0))
    d = t.shape[0]
    return pl.BlockSpec((d, tr // d, t.shape[2]), lambda i: (0, i, 0))


def _head_lanes(a):
    return lax.broadcasted_iota(jnp.int32, (1, LANES), 1) // HEAD_DIM == a


def _one_head(x, a):
    return jnp.where(_head_lanes(a), x, jnp.zeros_like(x))


def _head_stack(x):
    return jnp.concatenate([_one_head(x, 0), _one_head(x, 1)], axis=0)


def _pair_rows(ref, rows):
    return jnp.concatenate([ref[0:1, rows], ref[1:2, rows]], axis=1)


def _band_scores_t(kb, q_stack, bias_t, first):
    s = _dot(kb, q_stack, _NT) + bias_t
    if first is not None:
        key = lax.broadcasted_iota(jnp.int32, s.shape, 0)
        s = jnp.where(jnp.logical_and(first, key < BAND), NEG, s)
    return s


def _pair_select(stacked):
    return jnp.where(_head_lanes(0), stacked[0:BAND, :], stacked[BAND:, :])


def _dil_lse(qkv_v, base, bias_t, name):
    d, L = qkv_v.shape[:2]
    S = L * d
    _, chunk, nb, nchunks = _band_geometry(S, d)
    main, prev, bias, stat, _ = _band_in_specs(S, d, base)

    def body(q_ref, k_ref, kp_ref, b_ref, lse_ref, kext_ref):
        first = pl.program_id(2) == 0
        kext_ref[0:BAND, :] = kp_ref[...]
        kext_ref[BAND:, :] = k_ref[...]
        for b in range(nb):
            rows, ext = slice(b * BAND, (b + 1) * BAND), slice(b * BAND, (b + 2) * BAND)
            s = _band_scores_t(kext_ref[ext, :], _head_stack(q_ref[rows, :] * QK_SCALE), b_ref[...],
                               first if b == 0 else None)
            m = jnp.max(s, axis=0, keepdims=True)
            lse = m + jnp.log(jnp.sum(jnp.exp(s - m), axis=0, keepdims=True))
            lse_ref[0:1, rows] = lse[:, 0:BAND]
            lse_ref[1:2, rows] = lse[:, BAND:]

    return pl.pallas_call(
        body, name=name, grid=(HEAD_PAIRS, d, nchunks),
        in_specs=[main[0], main[1], prev[1], bias], out_specs=stat,
        out_shape=jax.ShapeDtypeStruct((HEAD_PAIRS, 2, S), F32),
        scratch_shapes=[pltpu.VMEM((chunk + BAND, LANES), BF16)],
        compiler_params=_params("parallel", "parallel", "parallel"),
    )(qkv_v, qkv_v, qkv_v, bias_t)


def _dil_out(qkv_v, base, bias_t, lse_joint, name):
    d, L = qkv_v.shape[:2]
    S = L * d
    _, chunk, nb, nchunks = _band_geometry(S, d)
    main, prev, bias, stat, tok = _band_in_specs(S, d, base)

    def body(q_ref, k_ref, kp_ref, v_ref, vp_ref, b_ref, lse_ref, o_ref, kext_ref, vext_ref):
        first = pl.program_id(2) == 0
        kext_ref[0:BAND, :] = kp_ref[...]
        kext_ref[BAND:, :] = k_ref[...]
        vext_ref[0:BAND, :] = vp_ref[...]
        vext_ref[BAND:, :] = v_ref[...]
        for b in range(nb):
            rows, ext = slice(b * BAND, (b + 1) * BAND), slice(b * BAND, (b + 2) * BAND)
            s = _band_scores_t(kext_ref[ext, :], _head_stack(q_ref[rows, :] * QK_SCALE), b_ref[...],
                               first if b == 0 else None)
            p_t = jnp.exp(s - _pair_rows(lse_ref, rows))
            o_ref[rows, :] = _pair_select(_dot(p_t.astype(BF16), vext_ref[ext, :], _TN)).astype(BF16)

    return pl.pallas_call(
        body, name=name, grid=(HEAD_PAIRS, d, nchunks),
        in_specs=[main[0], main[1], prev[1], main[2], prev[2], bias, stat], out_specs=tok,
        out_shape=jax.ShapeDtypeStruct((d, L, DIL_WIDTH), BF16),
        scratch_shapes=[pltpu.VMEM((chunk + BAND, LANES), BF16), pltpu.VMEM((chunk + BAND, LANES), BF16)],
        compiler_params=_params("parallel", "parallel", "parallel"),
    )(qkv_v, qkv_v, qkv_v, qkv_v, qkv_v, bias_t, lse_joint)


def _dil_bwd(qkv_v, base, do_v, bias_t, lse_joint, delta, name):
    d, L = qkv_v.shape[:2]
    S = L * d
    _, chunk, nb, nchunks = _band_geometry(S, d)
    main, prev, bias, stat, tok = _band_in_specs(S, d, base)
    nblocks = L // BAND

    def nxt_row(i):
        return jnp.minimum((i + 1) * nb, nblocks - 1)

    q_next = pl.BlockSpec((None, BAND, LANES), lambda hp, r, i: (r, nxt_row(i), base + hp))
    do_next = pl.BlockSpec((None, BAND, LANES), lambda hp, r, i: (r, nxt_row(i), hp))
    stat_next = pl.BlockSpec((None, 2, BAND), lambda hp, r, i: (hp, 0, r * nblocks + nxt_row(i)))

    def body(q_ref, k_ref, kp_ref, v_ref, vp_ref, do_ref, b_ref, lse_ref, dl_ref,
             qn_ref, don_ref, lsen_ref, dln_ref,
             dq_ref, dk_ref, dv_ref, db_ref, kext_ref, vext_ref, dkext_ref, dvext_ref):
        r, i = pl.program_id(1), pl.program_id(2)
        first = i == 0
        has_next = i + 1 < nchunks
        tail = slice(BAND + chunk, 2 * BAND + chunk)
        kext_ref[0:BAND, :] = kp_ref[...]
        kext_ref[BAND:BAND + chunk, :] = k_ref[...]
        kext_ref[tail, :] = jnp.zeros((BAND, LANES), BF16)
        vext_ref[0:BAND, :] = vp_ref[...]
        vext_ref[BAND:BAND + chunk, :] = v_ref[...]
        vext_ref[tail, :] = jnp.zeros((BAND, LANES), BF16)
        dkext_ref[...] = jnp.zeros_like(dkext_ref)
        dvext_ref[...] = jnp.zeros_like(dvext_ref)

        @pl.when(jnp.logical_and(r == 0, i == 0))
        def _():
            db_ref[...] = jnp.zeros_like(db_ref)

        def block(q2, do2, lse_row, dl_row, ext, mask_rows):
            q_stack, do_stack = _head_stack(q2), _head_stack(do2)
            s = _dot(kext_ref[ext, :], q_stack, _NT) + b_ref[...]
            if mask_rows is not None:
                s = jnp.where(mask_rows, NEG, s)
            p_t = jnp.exp(s - lse_row)
            ds_t = p_t * (_dot(vext_ref[ext, :], do_stack, _NT) - dl_row)
            ds_b = ds_t.astype(BF16)
            dkext_ref[ext, :] += _dot(ds_b, q_stack, _NN)
            dvext_ref[ext, :] += _dot(p_t.astype(BF16), do_stack, _NN)
            return ds_t, ds_b

        key = lax.broadcasted_iota(jnp.int32, (2 * BAND, 2 * BAND), 0)
        all_lanes = slice(0, BAND)
        for b in range(nb):
            rows, ext = slice(b * BAND, (b + 1) * BAND), slice(b * BAND, (b + 2) * BAND)
            mask = jnp.logical_and(first, key < BAND) if b == 0 else None
            ds_t, ds_b = block(q_ref[rows, :] * QK_SCALE, do_ref[rows, :], _pair_rows(lse_ref, rows),
                               _pair_rows(dl_ref, rows), ext, mask)
            dq_ref[rows, :] = (_pair_select(_dot(ds_b, kext_ref[ext, :], _TN)) * QK_SCALE).astype(BF16)
            db_ref[...] += ds_t
        block(qn_ref[...] * QK_SCALE, don_ref[...], _pair_rows(lsen_ref, all_lanes), _pair_rows(dln_ref, all_lanes),
              slice(chunk, chunk + 2 * BAND), jnp.logical_or(jnp.logical_not(has_next), key >= BAND))
        dk_ref[...] = dkext_ref[BAND:BAND + chunk, :].astype(BF16)
        dv_ref[...] = dvext_ref[BAND:BAND + chunk, :].astype(BF16)

    ext_rows = chunk + 2 * BAND
    return pl.pallas_call(
        body, name=name, grid=(HEAD_PAIRS, d, nchunks),
        in_specs=[main[0], main[1], prev[1], main[2], prev[2], tok, bias, stat, stat,
                  q_next, do_next, stat_next, stat_next],
        out_specs=[tok, tok, tok, bias],
        out_shape=[jax.ShapeDtypeStruct((d, L, DIL_WIDTH), BF16)] * 3
                  + [jax.ShapeDtypeStruct((HEAD_PAIRS, 2 * BAND, 2 * BAND), F32)],
        scratch_shapes=[pltpu.VMEM((ext_rows, LANES), BF16), pltpu.VMEM((ext_rows, LANES), BF16),
                        pltpu.VMEM((ext_rows, LANES), F32), pltpu.VMEM((ext_rows, LANES), F32)],
        compiler_params=_params("arbitrary", "arbitrary", "arbitrary"),
    )(qkv_v, qkv_v, qkv_v, qkv_v, qkv_v, do_v, bias_t, lse_joint, delta, qkv_v, do_v, lse_joint, delta)


def _lse_join(lse3, name):
    P, H, S = lse3.shape

    def body(l_ref, o_ref):
        a, b, c = l_ref[0], l_ref[1], l_ref[2]
        m = jnp.maximum(jnp.maximum(a, b), c)
        o_ref[...] = m + jnp.log(jnp.exp(a - m) + jnp.exp(b - m) + jnp.exp(c - m))

    return pl.pallas_call(body, name=name, out_shape=jax.ShapeDtypeStruct((H, S), F32))(lse3)


def _bucket_reduce(dbias_t, bucket_map_t, name):
    P, H = dbias_t.shape[:2]

    def body(db_ref, bk_ref, o_ref):
        p, h = pl.program_id(0), pl.program_id(1)

        @pl.when(jnp.logical_and(p == 0, h == 0))
        def _():
            o_ref[...] = jnp.zeros_like(o_ref)

        db, bk = db_ref[...], bk_ref[...]
        row = lax.broadcasted_iota(jnp.int32, (N_BUCKETS, LANES), 0)
        lane = lax.broadcasted_iota(jnp.int32, (N_BUCKETS, LANES), 1)

        def one(b, acc):
            val = jnp.sum(jnp.sum(jnp.where(bk == b, db, 0.0), axis=1, keepdims=True), axis=0, keepdims=True)
            return acc + jnp.where(jnp.logical_and(row == b, lane == h), val, 0.0)

        acc = jnp.zeros((N_BUCKETS, LANES), F32)
        for b in range(N_BUCKETS):
            acc = one(b, acc)
        o_ref[...] += acc

    return pl.pallas_call(
        body, name=name, grid=(P, H),
        in_specs=[pl.BlockSpec((None, None, 2 * BAND, BAND), lambda p, h: (p, h, 0, 0)),
                  pl.BlockSpec((None, 2 * BAND, BAND), lambda p, h: (p, 0, 0))],
        out_specs=pl.BlockSpec((N_BUCKETS, LANES), lambda p, h: (0, 0)),
        out_shape=jax.ShapeDtypeStruct((N_BUCKETS, LANES), F32),
        compiler_params=_params("arbitrary", "arbitrary"),
    )(dbias_t, bucket_map_t)


def _mem_fwd(q, kv, name, tq=1024):
    S, W = q.shape
    N = kv.shape[0]
    pairs = W // LANES
    tq = _fit(S, tq)

    def body(q_ref, k_ref, v_ref, o_ref, lse_ref):
        for a in range(2):
            lanes = slice(a * HEAD_DIM, (a + 1) * HEAD_DIM)
            s = _dot(k_ref[:, lanes], q_ref[:, lanes] * QK_SCALE, _NT)
            m = jnp.max(s, axis=0, keepdims=True)
            e = jnp.exp(s - m)
            l = jnp.sum(e, axis=0, keepdims=True)
            o_ref[:, lanes] = _dot((e / l).astype(BF16), v_ref[:, lanes], _TN).astype(BF16)
            lse_ref[a:a + 1, :] = m + jnp.log(l)

    return pl.pallas_call(
        body, name=name, grid=(pairs, S // tq),
        in_specs=[pl.BlockSpec((tq, LANES), lambda hp, i: (i, hp)),
                  pl.BlockSpec((N, LANES), lambda hp, i: (0, hp)),
                  pl.BlockSpec((N, LANES), lambda hp, i: (0, pairs + hp))],
        out_specs=[pl.BlockSpec((tq, LANES), lambda hp, i: (i, hp)),
                   pl.BlockSpec((None, 2, tq), lambda hp, i: (hp, 0, i))],
        out_shape=[jax.ShapeDtypeStruct((S, W), BF16), jax.ShapeDtypeStruct((pairs, 2, S), F32)],
        compiler_params=_params("parallel", "parallel"),
    )(q, kv, kv)


def _mem_bwd(q, kv, do, lse, delta, name, tq=1024):
    S, W = q.shape
    N = kv.shape[0]
    pairs = W // LANES
    tq = _fit(S, tq)

    def body(q_ref, k_ref, v_ref, do_ref, lse_ref, dl_ref, dq_ref, dk_ref, dv_ref):
        i = pl.program_id(1)

        @pl.when(i == 0)
        def _():
            dk_ref[...] = jnp.zeros_like(dk_ref)
            dv_ref[...] = jnp.zeros_like(dv_ref)

        for a in range(2):
            lanes = slice(a * HEAD_DIM, (a + 1) * HEAD_DIM)
            qv, dov = q_ref[:, lanes] * QK_SCALE, do_ref[:, lanes]
            kv_, vv = k_ref[:, lanes], v_ref[:, lanes]
            p_t = jnp.exp(_dot(kv_, qv, _NT) - lse_ref[a:a + 1, :])
            ds_t = p_t * (_dot(vv, dov, _NT) - dl_ref[a:a + 1, :])
            ds_b = ds_t.astype(BF16)
            dq_ref[:, lanes] = (_dot(ds_b, kv_, _TN) * QK_SCALE).astype(BF16)
            dk_ref[:, lanes] += _dot(ds_b, qv, _NN)
            dv_ref[:, lanes] += _dot(p_t.astype(BF16), dov, _NN)

    qs = pl.BlockSpec((tq, LANES), lambda hp, i: (i, hp))
    stat = pl.BlockSpec((None, 2, tq), lambda hp, i: (hp, 0, i))
    acc = pl.BlockSpec((N, LANES), lambda hp, i: (0, hp))
    return pl.pallas_call(
        body, name=name, grid=(pairs, S // tq),
        in_specs=[qs, acc, pl.BlockSpec((N, LANES), lambda hp, i: (0, pairs + hp)), qs, stat, stat],
        out_specs=[qs, acc, acc],
        out_shape=[jax.ShapeDtypeStruct((S, W), BF16), jax.ShapeDtypeStruct((N, W), F32),
                   jax.ShapeDtypeStruct((N, W), F32)],
        compiler_params=_params("parallel", "arbitrary"),
    )(q, kv, kv, do, lse, delta)


def _head_rowdot(a, bs, name, tr=512):
    S, W = a.shape
    tr = _fit(S, tr)

    def body(*refs):
        a_ref, b_refs, o_ref, tmp_ref = refs[0], refs[1:-2], refs[-2], refs[-1]
        col = lax.broadcasted_iota(jnp.int32, (LANES, LANES), 0)
        lane = lax.broadcasted_iota(jnp.int32, (LANES, LANES), 1)
        acc = jnp.zeros((tr, LANES), F32)
        for j in range(W // LANES):
            cols = slice(j * LANES, (j + 1) * LANES)
            tot = _token_rows(b_refs[0], cols, tmp_ref)
            for r in b_refs[1:]:
                tot = tot + _token_rows(r, cols, tmp_ref)
            sel = jnp.where(col // HEAD_DIM + j * (LANES // HEAD_DIM) == lane, 1.0, 0.0).astype(F32)
            acc = acc + lax.dot_general(a_ref[:, cols].astype(F32) * tot, sel, (_NN, ((), ())),
                                        precision=lax.Precision.HIGHEST, preferred_element_type=F32)
        o_ref[...] = acc

    return pl.pallas_call(
        body, name=name, grid=(S // tr,), in_specs=[_row_spec(t, tr) for t in [a] + list(bs)],
        out_specs=pl.BlockSpec((tr, LANES), lambda i: (i, 0)),
        out_shape=jax.ShapeDtypeStruct((S, LANES), F32),
        scratch_shapes=[pltpu.VMEM((tr, LANES), F32)],
        compiler_params=_params("parallel"),
    )(a, *bs)


def _sum_cast_cols(groups, out_dtype, name, tail=None, tr=256):
    first = groups[0][0]
    S, W = (first.shape if first.ndim == 2 else (first.shape[0] * first.shape[1], first.shape[2]))
    tr = _fit(S, tr)
    flat = [t for g in groups for t in g] + ([tail] if tail is not None else [])
    tail_w = 0 if tail is None else tail.shape[1]

    def body(*refs):
        o_ref, tmp_ref = refs[-2], refs[-1]
        if tail is not None:
            o_ref[:, W * len(groups):] = refs[-3][...].astype(out_dtype)
        k = 0
        for gi, g in enumerate(groups):
            for j in range(W // LANES):
                cols = slice(j * LANES, (j + 1) * LANES)
                acc = _token_rows(refs[k], cols, tmp_ref)
                for r in refs[k + 1:k + len(g)]:
                    acc = acc + _token_rows(r, cols, tmp_ref)
                o_ref[:, gi * W + j * LANES:gi * W + (j + 1) * LANES] = acc.astype(out_dtype)
            k += len(g)

    return pl.pallas_call(
        body, name=name, grid=(S // tr,), in_specs=[_row_spec(t, tr) for t in flat],
        out_specs=pl.BlockSpec((tr, W * len(groups) + tail_w), lambda i: (i, 0)),
        out_shape=jax.ShapeDtypeStruct((S, W * len(groups) + tail_w), out_dtype),
        scratch_shapes=[pltpu.VMEM((tr, LANES), F32)],
        compiler_params=_params("parallel"),
    )(*flat)


FF_TILE = 256


def _ffn_up(h, w_gu, name, tm=4096):
    S, D = h.shape
    F2 = w_gu.shape[1]
    tm = _fit(S, tm)

    def body(h_ref, w_ref, gu_ref, act_ref):
        gu = _dot(h_ref[...], w_ref[...], _NN)
        gu_ref[...] = gu.astype(BF16)
        g, u = gu[:, :FF_TILE], gu[:, FF_TILE:]
        act_ref[...] = (g * (1.0 / (1.0 + jnp.exp(-g))) * u).astype(BF16)

    return pl.pallas_call(
        body, name=name, grid=(S // tm, F2 // (2 * FF_TILE)),
        in_specs=[pl.BlockSpec((tm, D), lambda i, j: (i, 0)), pl.BlockSpec((D, 2 * FF_TILE), lambda i, j: (0, j))],
        out_specs=[pl.BlockSpec((tm, 2 * FF_TILE), lambda i, j: (i, j)),
                   pl.BlockSpec((tm, FF_TILE), lambda i, j: (i, j))],
        out_shape=[jax.ShapeDtypeStruct((S, F2), BF16), jax.ShapeDtypeStruct((S, F2 // 2), BF16)],
        compiler_params=_params("parallel", "arbitrary"),
    )(h, w_gu)


def _ffn_dact(dy, w_down, gu, name, tm=4096):
    S, D = dy.shape
    F2 = gu.shape[1]
    tm = _fit(S, tm)

    def body(dy_ref, w_ref, gu_ref, dgu_ref):
        dact = _dot(dy_ref[...], w_ref[...], _NT)
        gu_v = gu_ref[...].astype(F32)
        g, u = gu_v[:, :FF_TILE], gu_v[:, FF_TILE:]
        sig = 1.0 / (1.0 + jnp.exp(-g))
        silu = g * sig
        dgu_ref[:, :FF_TILE] = (dact * u * (sig + silu * (1.0 - sig))).astype(BF16)
        dgu_ref[:, FF_TILE:] = (dact * silu).astype(BF16)

    return pl.pallas_call(
        body, name=name, grid=(S // tm, F2 // (2 * FF_TILE)),
        in_specs=[pl.BlockSpec((tm, D), lambda i, j: (i, 0)), pl.BlockSpec((FF_TILE, D), lambda i, j: (j, 0)),
                  pl.BlockSpec((tm, 2 * FF_TILE), lambda i, j: (i, j))],
        out_specs=pl.BlockSpec((tm, 2 * FF_TILE), lambda i, j: (i, j)),
        out_shape=jax.ShapeDtypeStruct((S, F2), BF16),
        compiler_params=_params("parallel", "arbitrary"),
    )(dy, w_down, gu)


def _fit_rows(n, cap):
    if n <= cap:
        return n
    t = (cap // 8) * 8
    while t >= 8:
        if n % t == 0:
            return t
        t -= 8
    raise ValueError(f"no sublane-aligned tile for {n} under {cap}")


def _add_n(arrs, name, tr=512):
    R, C = arrs[0].shape
    tr = _fit_rows(R, tr)

    def body(*refs):
        acc = refs[0][...].astype(F32)
        for r in refs[1:-1]:
            acc = acc + r[...].astype(F32)
        refs[-1][...] = acc

    row = pl.BlockSpec((tr, C), lambda i: (i, 0))
    return pl.pallas_call(
        body, name=name, grid=(R // tr,), in_specs=[row] * len(arrs), out_specs=row,
        out_shape=jax.ShapeDtypeStruct((R, C), F32), compiler_params=_params("parallel"),
    )(*arrs)


def _adamw(w, g, m, v, name, tr=512):
    R, C = w.shape
    tr = _fit_rows(R, tr)
    c1 = 1.0 / (1.0 - ADAM_B1 ** ADAM_STEP)
    c2 = 1.0 / (1.0 - ADAM_B2 ** ADAM_STEP)

    def body(w_ref, g_ref, m_ref, v_ref, d_ref, nm_ref, nv_ref):
        gv = g_ref[...]
        nm = ADAM_B1 * m_ref[...] + (1.0 - ADAM_B1) * gv
        nv = ADAM_B2 * v_ref[...] + (1.0 - ADAM_B2) * (gv * gv)
        nm_ref[...] = nm
        nv_ref[...] = nv
        d_ref[...] = -ADAM_LR * ((nm * c1) / (jnp.sqrt(nv * c2) + ADAM_EPS) + ADAM_WD * w_ref[...])

    row = pl.BlockSpec((tr, C), lambda i: (i, 0))
    return pl.pallas_call(
        body, name=name, grid=(R // tr,), in_specs=[row] * 4, out_specs=[row] * 3,
        out_shape=[jax.ShapeDtypeStruct((R, C), F32)] * 3, compiler_params=_params("parallel"),
    )(w, g, m, v)


def _place():
    return lax.axis_index("x"), lax.axis_index("y"), lax.axis_index("c")


_ANY = pl.BlockSpec(memory_space=pl.ANY)


def _chip_all_gather(shard, name):
    R, C = shard.shape
    half = R // 2

    def body(x_ref, out_ref, send_sems, recv_sems, local_sem):
        x, y, c = _place()
        chips = [(1 - x, y), (x, 1 - y), (1 - x, 1 - y)]
        sibling = (x, y, 1 - c)
        mine = pltpu.make_async_copy(x_ref, out_ref.at[2 * x + y], local_sem)
        mine.start()

        def rows(chip, core):
            return out_ref.at[chip, pl.ds(core * half, half)]

        def copy(k, chip, core, to, src=None):
            return pltpu.make_async_remote_copy(
                src_ref=rows(chip, core) if src is None else src, dst_ref=rows(chip, core),
                send_sem=send_sems.at[k], recv_sem=recv_sems.at[k], device_id=to, device_id_type=MESH_IDS)

        me = 2 * x + y
        first = [copy(k, me, c, (cx, cy, c), src=x_ref.at[pl.ds(c * half, half)]) for k, (cx, cy) in enumerate(chips)]
        for cp in first:
            cp.start()
        passed = [copy(3 + k, 2 * cx + cy, c, sibling) for k, (cx, cy) in enumerate(chips)]
        for k, (cx, cy) in enumerate(chips):
            copy(k, 2 * cx + cy, c, (cx, cy, c)).wait_recv()
            passed[k].start()
        for k, (cx, cy) in enumerate(chips):
            copy(3 + k, 2 * cx + cy, 1 - c, sibling).wait_recv()
        for cp in first + passed:
            cp.wait_send()
        mine.wait()

    return pl.pallas_call(
        body, name=name, in_specs=[_ANY], out_specs=_ANY,
        out_shape=jax.ShapeDtypeStruct((N_CHIPS, R, C), shard.dtype),
        scratch_shapes=[pltpu.SemaphoreType.DMA((6,)), pltpu.SemaphoreType.DMA((6,)), pltpu.SemaphoreType.DMA],
    )(shard)


def _sibling_exchange(buf, name):
    def body(x_ref, out_ref, send_sem, recv_sem):
        x, y, c = _place()
        cp = pltpu.make_async_remote_copy(
            src_ref=x_ref, dst_ref=out_ref, send_sem=send_sem, recv_sem=recv_sem,
            device_id=(x, y, 1 - c), device_id_type=MESH_IDS)
        cp.start()
        cp.wait()

    return pl.pallas_call(
        body, name=name, in_specs=[_ANY], out_specs=_ANY,
        out_shape=jax.ShapeDtypeStruct(buf.shape, buf.dtype),
        scratch_shapes=[pltpu.SemaphoreType.DMA, pltpu.SemaphoreType.DMA],
    )(buf)


def _chip_scatter(parts, name):
    _, R, C = parts.shape

    def body(p_ref, out_ref, send_sems, recv_sems):
        x, y, c = _place()
        chips = [(1 - x, y), (x, 1 - y), (1 - x, 1 - y)]

        def copy(k, slab, to):
            return pltpu.make_async_remote_copy(
                src_ref=p_ref.at[slab], dst_ref=out_ref.at[k], send_sem=send_sems.at[k], recv_sem=recv_sems.at[k],
                device_id=to, device_id_type=MESH_IDS)

        sends = [copy(k, 2 * cx + cy, (cx, cy, c)) for k, (cx, cy) in enumerate(chips)]
        for cp in sends:
            cp.start()
        for cp in sends:
            cp.wait_recv()
        for cp in sends:
            cp.wait_send()

    return pl.pallas_call(
        body, name=name, in_specs=[_ANY], out_specs=_ANY,
        out_shape=jax.ShapeDtypeStruct((3, R, C), parts.dtype),
        scratch_shapes=[pltpu.SemaphoreType.DMA((3,)), pltpu.SemaphoreType.DMA((3,))],
    )(parts)


def _all_to_all_small(vec, name):
    R, C = vec.shape

    def body(v_ref, out_ref, send_sems, recv_sems, local_sem):
        x, y, c = _place()
        me = 4 * x + 2 * y + c
        mine = pltpu.make_async_copy(v_ref, out_ref.at[me], local_sem)
        mine.start()
        flips = [(dx, dy, dc) for dx in (0, 1) for dy in (0, 1) for dc in (0, 1)][1:]

        def peer(f):
            return (x ^ f[0], y ^ f[1], c ^ f[2])

        def copy(k, slot, to):
            return pltpu.make_async_remote_copy(
                src_ref=v_ref, dst_ref=out_ref.at[slot], send_sem=send_sems.at[k], recv_sem=recv_sems.at[k],
                device_id=to, device_id_type=MESH_IDS)

        sends = [copy(k, me, peer(f)) for k, f in enumerate(flips)]
        for cp in sends:
            cp.start()
        for k, f in enumerate(flips):
            px, py, pc = peer(f)
            copy(k, 4 * px + 2 * py + pc, peer(f)).wait_recv()
        for cp in sends:
            cp.wait_send()
        mine.wait()

    return pl.pallas_call(
        body, name=name, in_specs=[_ANY], out_specs=_ANY,
        out_shape=jax.ShapeDtypeStruct((8, R, C), vec.dtype),
        scratch_shapes=[pltpu.SemaphoreType.DMA((7,)), pltpu.SemaphoreType.DMA((7,)), pltpu.SemaphoreType.DMA],
    )(vec)


def _to_heads(t, n):
    S = t.shape[0]
    return t.reshape(S, n, HEAD_DIM).transpose(1, 0, 2)


def _to_heads_t(t, n):
    S = t.shape[0]
    return t.T.reshape(n, HEAD_DIM, S)


def _from_heads_t(t):
    H, Dh, S = t.shape
    return t.reshape(H * Dh, S).T


def _t5_bucket(dist):
    max_exact = N_BUCKETS // 2
    d = np.maximum(dist, 1).astype(np.float32)
    large = max_exact + (np.log(d / max_exact) / np.log(MAX_DISTANCE / max_exact)
                         * (N_BUCKETS - max_exact)).astype(np.int32)
    large = np.minimum(large, N_BUCKETS - 1)
    return np.where(dist < max_exact, dist, large).astype(np.int32)


def _band_tables():
    qi = np.arange(BAND)[:, None]
    kj = np.arange(2 * BAND)[None, :]
    sub = qi + BAND - kj
    band = (sub >= 0) & (sub <= BAND)
    out = []
    for d in DILATIONS:
        bucket = _t5_bucket(np.clip(sub, 0, BAND) * d)
        out.append(np.where(band, bucket, -1).astype(np.int32))
    return np.stack(out)


_PACK = (("w_in", 770), ("w_out", 256), ("w_xq", 64), ("w_xk", 64), ("w_xv", 64), ("w_xo", 64),
         ("w_gate", 704), ("w_up", 704), ("w_down", 704))


def _pack(shards):
    rows = [shards[n].reshape(-1, PACK_COLS) for n, _ in _PACK]
    total = sum(r.shape[0] for r in rows)
    pad = (-total) % 128
    if pad:
        rows.append(jnp.zeros((pad, PACK_COLS), rows[0].dtype))
    return jnp.concatenate(rows, axis=0)


def _unpack(pack, shapes):
    out, r = {}, 0
    for n, _ in _PACK:
        cnt = int(np.prod(shapes[n])) // PACK_COLS
        out[n] = pack[r:r + cnt].reshape(shapes[n])
        r += cnt
    return out


_COL_SHARDED = ("w_in", "w_xo", "w_gate", "w_up")


def _full_weight(gathered, name):
    return jnp.concatenate(gathered, axis=1 if name in _COL_SHARDED else 0)


def _split_weight(full, name):
    return jnp.split(full, N_CHIPS, axis=1 if name in _COL_SHARDED else 0)


_SMALL = ("g_mix_pre", "g_mix_post", "g_xattn_pre", "g_mem", "g_xattn_post", "g_ffn_pre", "g_ffn_post")


def _pack_small(vals):
    D = vals["g_mix_pre"].shape[1]
    rows = [vals[n].reshape(1, D) for n in _SMALL]
    misc = jnp.concatenate([vals["b_f"].reshape(-1), vals["rel_bias"].reshape(-1)])
    rows.append(jnp.pad(misc, (0, D - misc.shape[0])).reshape(1, D))
    rows.append(jnp.zeros((16 - len(rows), D), F32))
    return jnp.concatenate(rows, axis=0)


def _unpack_small(pack):
    out = {n: pack[i:i + 1] for i, n in enumerate(_SMALL)}
    out["b_f"] = pack[7, 0:N_FOX_HEADS].reshape(1, N_FOX_HEADS)
    out["rel_bias"] = pack[7, N_FOX_HEADS:N_FOX_HEADS + N_BUCKETS * N_DIL_HEADS].reshape(N_BUCKETS, N_DIL_HEADS)
    return out


def kernel(x, mem, g_mix_pre, w_in, b_f, rel_bias, w_out, g_mix_post, g_xattn_pre, g_mem, w_xq, w_xk, w_xv, w_xo, g_xattn_post, g_ffn_pre, w_gate, w_up, w_down, g_ffn_post, loss_target, m_g_mix_pre, m_w_in, m_b_f, m_rel_bias, m_w_out, m_g_mix_post, m_g_xattn_pre, m_g_mem, m_w_xq, m_w_xk, m_w_xv, m_w_xo, m_g_xattn_post, m_g_ffn_pre, m_w_gate, m_w_up, m_w_down, m_g_ffn_post, v_g_mix_pre, v_w_in, v_b_f, v_rel_bias, v_w_out, v_g_mix_post, v_g_xattn_pre, v_g_mem, v_w_xq, v_w_xk, v_w_xv, v_w_xo, v_g_xattn_post, v_g_ffn_pre, v_w_gate, v_w_up, v_w_down, v_g_ffn_post):
    args = dict(locals())
    big = [n for n, _ in _PACK]
    names = ["g_mix_pre", "w_in", "b_f", "rel_bias", "w_out", "g_mix_post", "g_xattn_pre", "g_mem", "w_xq",
             "w_xk", "w_xv", "w_xo", "g_xattn_post", "g_ffn_pre", "w_gate", "w_up", "w_down", "g_ffn_post"]
    xs = x[0]
    S, D = xs.shape
    assert S % (BAND * DILATIONS[-1]) == 0
    shard_shapes = {n: args[n].shape[1:] for n in big}
    my_x, my_y, my_c = lax.axis_index("x"), lax.axis_index("y"), lax.axis_index("c")

    gathered = _chip_all_gather(_pack({n: args[n][0].astype(BF16) for n in big}), "weights_all_gather")
    per_chip = [_unpack(gathered[j], shard_shapes) for j in range(N_CHIPS)]
    W = {n: _full_weight([pc[n] for pc in per_chip], n) for n in big}
    w_fox, w_fg, w_dil = (W["w_in"][:, :3 * FOX_WIDTH], W["w_in"][:, 3 * FOX_WIDTH:3 * FOX_WIDTH + N_FOX_HEADS],
                          W["w_in"][:, 3 * FOX_WIDTH + N_FOX_HEADS:])
    w_qkv = jnp.concatenate([w_fox, w_dil], axis=1)
    w_fg_pad = jnp.pad(w_fg, ((0, 0), (0, LANES - N_FOX_HEADS)))
    F = W["w_gate"].shape[1]
    nft = F // FF_TILE
    w_gu = jnp.stack([W["w_gate"].reshape(D, nft, FF_TILE), W["w_up"].reshape(D, nft, FF_TILE)],
                     axis=2).reshape(D, 2 * F)

    h1 = _rms_fwd(xs, g_mix_pre, "rms_mix_pre")
    qkv = _mm(h1, w_qkv, "nn", BF16, "proj_qkv", tm=2048)
    fg = _mm(h1, w_fg_pad, "nn", F32, "proj_gate")
    fg_t = fg[:, :N_FOX_HEADS].T
    b_col = b_f.reshape(N_FOX_HEADS, 1)
    c_t = _forget_fwd(fg_t, b_col, "forget_cumsum")
    fq_s, fk_s, fv_s = (qkv[:, i * FOX_WIDTH:(i + 1) * FOX_WIDTH] for i in range(3))
    fqt, fvt = _to_heads_t(fq_s, N_FOX_HEADS), _to_heads_t(fv_s, N_FOX_HEADS)
    unit = jnp.full((N_FOX_HEADS, S), 1.0, BF16)
    inv_scale = jnp.full((N_FOX_HEADS, S), 1.0 / QK_SCALE, BF16)
    ka = _lanes_operand(_to_heads(fk_s, N_FOX_HEADS), list(_split3(-c_t)) + [unit] * EXTRA)
    qa_f = _rows_operand(fqt, [inv_scale] * EXTRA)
    o_fox_t, lse_fox = _fox_fwd(qa_f, ka, _with_ones(fvt), "fox_fwd")

    bucket_map = _band_tables()
    onehot = (jnp.asarray(bucket_map)[..., None] == jnp.arange(N_BUCKETS)).astype(F32)
    bias_tab = jnp.einsum("pqkb,bh->phkq", onehot, rel_bias, precision=lax.Precision.HIGHEST)
    bias_tab = jnp.where(jnp.asarray(bucket_map.transpose(0, 2, 1) >= 0)[:, None], bias_tab, NEG)
    bias_t = bias_tab.reshape(3, HEAD_PAIRS, 2, 2 * BAND, BAND).transpose(0, 1, 3, 2, 4).reshape(
        3, HEAD_PAIRS, 2 * BAND, 2 * BAND)
    views = [(qkv.reshape(1, S, qkv.shape[1]), DIL_Q_BLOCK)] + [
        (_to_residues(qkv, 1, 3 * DIL_WIDTH, d, f"dilated_qkv_residues_{d}"), 0) for d in DILATIONS[1:]]

    def to_tok(stat, d):
        return stat.reshape(N_DIL_HEADS, d, S // d).swapaxes(1, 2).reshape(N_DIL_HEADS, S)

    def to_perm(stat, d):
        return stat.reshape(N_DIL_HEADS, S // d, d).swapaxes(1, 2).reshape(HEAD_PAIRS, 2, S)

    def tok_or_res(t):
        return t.reshape(t.shape[1:]) if t.shape[0] == 1 else t

    lse_tok = jnp.stack([to_tok(_dil_lse(*views[p], bias_t[p], f"dilated_lse_{d}"), d)
                         for p, d in enumerate(DILATIONS)])
    lse_joint = _lse_join(lse_tok, "dilated_lse_join")
    lse_perm = [to_perm(lse_joint, d) for d in DILATIONS]
    o_dil = [tok_or_res(_dil_out(*views[p], bias_t[p], lse_perm[p], f"dilated_out_{d}"))
             for p, d in enumerate(DILATIONS)]
    o_cat = _sum_cast_cols([[_from_heads_t(o_fox_t)]] + [[o] for o in o_dil], BF16, "mixer_out_cat")
    w_out_b = W["w_out"]
    w_out_cat = jnp.concatenate([w_out_b[:FOX_WIDTH]] + [w_out_b[FOX_WIDTH:]] * 3, axis=0)
    a = _mm(o_cat, w_out_cat, "nn", F32, "proj_out", tm=2048, tk=2048)
    x1, h2 = _resid_norm(xs, a, g_mix_post, g_xattn_pre, "resid_mix")

    hm = _rms_fwd(mem[0], g_mem, "rms_mem")
    q2 = _mm(h2, W["w_xq"], "nn", BF16, "xattn_q")
    w_xkv = jnp.concatenate([W["w_xk"], W["w_xv"]], axis=1)
    kvm = _mm(hm, w_xkv, "nn", BF16, "xattn_kv")
    MW = N_MEM_HEADS * HEAD_DIM
    oc, lse_mem = _mem_fwd(q2, kvm, "xattn_fwd")
    y2 = _mm(oc, W["w_xo"], "nn", F32, "xattn_o")
    x2, h3 = _resid_norm(x1, y2, g_xattn_post, g_ffn_pre, "resid_xattn")

    gu, act = _ffn_up(h3, w_gu, "ffn_up")
    y3 = _mm(act, W["w_down"], "nn", F32, "ffn_down", tk=2816)
    dx3, loss_tile = _final_loss(x2, y3, g_ffn_post, loss_target[0], "final_loss")

    grads = {}
    small = {}
    _, dy3_b, dg = _rms_bwd(y3, g_ffn_post, dx3, None, "bwd_norm_ffn_post", want=("bf16",))
    small["g_ffn_post"] = dg[0:1]
    grads["w_down"] = _mm(act, dy3_b, "tn", F32, "grad_w_down", tm=1408)
    dgu = _ffn_dact(dy3_b, W["w_down"], gu, "ffn_dact")
    dw_gu = _mm(h3, dgu, "tn", F32, "grad_w_gu", tn=1408).reshape(D, nft, 2, FF_TILE)
    grads["w_gate"], grads["w_up"] = dw_gu[:, :, 0].reshape(D, F), dw_gu[:, :, 1].reshape(D, F)
    dh3 = _mm(dgu, w_gu, "nt", F32, "bwd_ffn_in", tm=2048, tk=1408)
    dx2, _, dg = _rms_bwd(x2, g_ffn_pre, dh3, dx3, "bwd_norm_ffn_pre", want=("f32",))
    small["g_ffn_pre"] = dg[0:1]

    _, dy2_b, dg = _rms_bwd(y2, g_xattn_post, dx2, None, "bwd_norm_xattn_post", want=("bf16",))
    small["g_xattn_post"] = dg[0:1]
    grads["w_xo"] = _mm(oc, dy2_b, "tn", F32, "grad_w_xo")
    doc = _mm(dy2_b, W["w_xo"], "nt", BF16, "bwd_xattn_o")
    delta_mem = _head_rowdot(doc, [oc], "xattn_delta")[:, :N_MEM_HEADS].T.reshape(N_MEM_HEADS // 2, 2, S)
    dq2, dkm, dvm = _mem_bwd(q2, kvm, doc, lse_mem, delta_mem, "xattn_bwd")
    dkvm = jnp.concatenate([dkm, dvm], axis=1).astype(BF16)
    grads["w_xq"] = _mm(h2, dq2, "tn", F32, "grad_w_xq")
    dw_xkv = _mm(hm, dkvm, "tn", F32, "grad_w_xkv")
    grads["w_xk"], grads["w_xv"] = dw_xkv[:, :MW], dw_xkv[:, MW:]
    dhm = _mm(dkvm, w_xkv, "nt", F32, "bwd_xattn_kv")
    _, _, dg = _rms_bwd(mem[0], g_mem, dhm, None, "bwd_norm_mem", want=())
    small["g_mem"] = dg[0:1]
    dh2 = _mm(dq2, W["w_xq"], "nt", F32, "bwd_xattn_q")
    dx1, _, dg = _rms_bwd(x1, g_xattn_pre, dh2, dx2, "bwd_norm_xattn_pre", want=("f32",))
    small["g_xattn_pre"] = dg[0:1]

    _, da_b, dg = _rms_bwd(a, g_mix_post, dx1, None, "bwd_norm_mix_post", want=("bf16",))
    small["g_mix_post"] = dg[0:1]
    dw_out_cat = _mm(o_cat, da_b, "tn", F32, "grad_w_out")
    dw_out_dil = _add_n([dw_out_cat[FOX_WIDTH + p * DIL_WIDTH:FOX_WIDTH + (p + 1) * DIL_WIDTH] for p in range(3)],
                        "grad_w_out_dil")
    grads["w_out"] = jnp.concatenate([dw_out_cat[:FOX_WIDTH], dw_out_dil], axis=0)
    do = _mm(da_b, w_out_b, "nt", BF16, "bwd_proj_out")
    do_fox, do_dil = do[:, :FOX_WIDTH], do[:, FOX_WIDTH:]

    delta_fox = _head_rowdot(do_fox, [o_cat[:, :FOX_WIDTH]], "fox_delta")[:, :N_FOX_HEADS].T
    qa_b = lax.dynamic_update_slice(qa_f, jnp.stack(_split3(lse_fox[:, 0] * (-1.0 / QK_SCALE)), axis=1),
                                    (0, HEAD_DIM + EXTRA, 0))
    va = _lanes_operand(_to_heads(fv_s, N_FOX_HEADS), [unit] * EXTRA)
    doa = _rows_operand(_to_heads_t(do_fox, N_FOX_HEADS), list(_split3(-delta_fox)))
    dq_aug, dk_aug, dvf = _fox_bwd(qa_b, ka, ka.transpose(0, 2, 1), va, doa, "fox_bwd")
    dqf, dkf = dq_aug[:, :HEAD_DIM], dk_aug[:, :HEAD_DIM]
    dfg_t, db_f = _forget_bwd(fg_t, b_col, dq_aug[:, HEAD_DIM + EXTRA], dk_aug[:, HEAD_DIM], "forget_bwd")

    delta_dil = _head_rowdot(do_dil, o_dil, "dilated_delta")[:, :N_DIL_HEADS].T
    do_res = [do_dil.reshape(1, S, DIL_WIDTH)] + [
        _to_residues(do, 1, DIL_WIDTH, d, f"dilated_do_residues_{d}") for d in DILATIONS[1:]]
    dil_grads = [_dil_bwd(*views[p], do_res[p], bias_t[p], lse_perm[p], to_perm(delta_dil, d), f"dilated_bwd_{d}")
                 for p, d in enumerate(DILATIONS)]
    dbias_t = jnp.stack([g[3].reshape(HEAD_PAIRS, 2 * BAND, 2, BAND).transpose(0, 2, 1, 3).reshape(
        N_DIL_HEADS, 2 * BAND, BAND) for g in dil_grads])
    d_rel = _bucket_reduce(dbias_t, jnp.asarray(bucket_map.transpose(0, 2, 1)), "rel_bias_grad")[:, :N_DIL_HEADS]
    dfg_pad = jnp.pad(dfg_t.T, ((0, 0), (0, LANES - N_FOX_HEADS))).astype(BF16)
    dcat = _sum_cast_cols([[_from_heads_t(dqf)], [_from_heads_t(dkf)], [_from_heads_t(dvf)]]
                          + [[tok_or_res(g[j]) for g in dil_grads] for j in range(3)],
                          BF16, "dqkv_assemble", tail=dfg_pad)
    dw_cat = _mm(h1, dcat, "tn", F32, "grad_w_qkv", tm=512, tn=3200)
    n_qkv = 3 * (FOX_WIDTH + DIL_WIDTH)
    grads["w_in"] = jnp.concatenate([dw_cat[:, :3 * FOX_WIDTH], dw_cat[:, n_qkv:n_qkv + N_FOX_HEADS],
                                     dw_cat[:, 3 * FOX_WIDTH:n_qkv]], axis=1)
    w_cat = jnp.concatenate([w_qkv, w_fg_pad], axis=1)
    dh1 = _mm(dcat, w_cat, "nt", F32, "bwd_proj_in", tk=3200)
    grad_x, _, dg = _rms_bwd(xs, g_mix_pre, dh1, dx1, "bwd_norm_mix_pre", want=("f32",))
    small["g_mix_pre"] = dg[0:1]
    small["b_f"] = db_f[:, 0].reshape(1, N_FOX_HEADS)
    small["rel_bias"] = d_rel

    split = {n: _split_weight(grads[n], n) for n in big}
    parts = jnp.stack([_pack({n: split[n][j].astype(BF16) for n in big}) for j in range(N_CHIPS)])
    R = parts.shape[1]
    half = R // 2
    keep = lax.dynamic_slice_in_dim(parts, my_c * half, half, axis=1)
    give = lax.dynamic_slice_in_dim(parts, (1 - my_c) * half, half, axis=1)
    got = _sibling_exchange(give, "grads_to_sibling")
    chip_sum = _add_n([keep.reshape(-1, PACK_COLS), got.reshape(-1, PACK_COLS)], "grads_add_sibling")
    chip_sum = chip_sum.reshape(N_CHIPS, half, PACK_COLS)
    my_chip = 2 * my_x + my_y
    from_chips = _chip_scatter(chip_sum.astype(BF16), "grads_to_chips")
    own = lax.dynamic_index_in_dim(chip_sum, my_chip, axis=0, keepdims=False)
    g_half = _add_n([own, from_chips[0], from_chips[1], from_chips[2]], "grads_add_chips")
    other_half = _sibling_exchange(g_half, "grads_share_sibling")
    g_pack = jnp.where(my_c == 0, jnp.concatenate([g_half, other_half]), jnp.concatenate([other_half, g_half]))

    small_pack = _pack_small(small)
    small_pack = small_pack.at[8, 0].set(loss_tile[0, 0])
    everyone = _all_to_all_small(small_pack, "small_all_gather")
    small_sum = _add_n([everyone[i] for i in range(8)], "small_sum")
    loss = small_sum[8, 0]
    g_small = _unpack_small(small_sum)

    outs = {"grad": _unpack(g_pack, shard_shapes), "delta": {}, "new_m": {}, "new_v": {}}
    for n in big:
        outs["delta"][n], outs["new_m"][n], outs["new_v"][n] = _adamw(
            args[n][0], outs["grad"][n], args["m_" + n][0], args["v_" + n][0], f"adamw_{n}")
    sw = _pack_small({n: args[n] for n in _SMALL + ("b_f", "rel_bias")})
    sm = _pack_small({n: args["m_" + n] for n in _SMALL + ("b_f", "rel_bias")})
    sv = _pack_small({n: args["v_" + n] for n in _SMALL + ("b_f", "rel_bias")})
    sd, snm, snv = _adamw(sw, small_sum.at[8, 0].set(0.0), sm, sv, "adamw_small")
    souts = {"grad": g_small, "delta": _unpack_small(sd), "new_m": _unpack_small(snm), "new_v": _unpack_small(snv)}

    def leaf(kind, n):
        if n in souts[kind]:
            return souts[kind][n].reshape(args[n].shape)
        return outs[kind][n].reshape(args[n].shape)

    result = [loss, grad_x.reshape(x.shape)]
    for kind in ("grad", "delta", "new_m", "new_v"):
        result += [leaf(kind, n) for n in names]
    return tuple(result)
```

```python
import numpy as np
import jax
import jax.numpy as jnp
from jax import lax
from jax.experimental import pallas as pl
from jax.experimental.pallas import tpu as pltpu

F32 = jnp.float32
BF16 = jnp.bfloat16
MESH_IDS = pl.DeviceIdType.MESH

LANES = 128
HEAD_DIM = 64
N_FOX_HEADS = 8
N_DIL_HEADS = 8
N_MEM_HEADS = 4
FOX_WIDTH = N_FOX_HEADS * HEAD_DIM
DIL_WIDTH = N_DIL_HEADS * HEAD_DIM
DILATIONS = (1, 4, 16)
BAND = 128
BAND_CHUNK_MAX = 32 * BAND
N_BUCKETS = 32
MAX_DISTANCE = 2048
QK_SCALE = HEAD_DIM ** -0.5
RMS_EPS = 1e-6
NEG = -1e30
VMEM_LIMIT = 56 << 20

ADAM_LR = 0.001
ADAM_B1 = 0.9
ADAM_B2 = 0.999
ADAM_EPS = 1e-08
ADAM_WD = 0.01
ADAM_STEP = 10

N_CHIPS = 4
PACK_COLS = 1024


def _params(*sem):
    return pltpu.CompilerParams(dimension_semantics=sem, vmem_limit_bytes=VMEM_LIMIT)


def _fit(n, cap):
    if n <= cap:
        return n
    t = (cap // LANES) * LANES
    while t >= LANES:
        if n % t == 0:
            return t
        t -= LANES
    raise ValueError(f"no lane-aligned tile for {n} under {cap}")


def _dot(a, b, dims):
    return lax.dot_general(a, b, (dims, ((), ())), preferred_element_type=F32)


_NN = ((1,), (0,))
_NT = ((1,), (1,))
_TN = ((0,), (0,))


def _mm(a, b, mode, out_dtype, name, tm=1024, tn=1024, tk=1024):
    if mode == "nn":
        (M, K), N = a.shape, b.shape[1]
    elif mode == "nt":
        (M, K), N = a.shape, b.shape[0]
    else:
        (K, M), N = a.shape, b.shape[1]
    tm, tn, tk = _fit(M, tm), _fit(N, tn), _fit(K, tk)
    nk = K // tk
    if mode == "tn":
        a_spec = pl.BlockSpec((tk, tm), lambda i, j, k: (k, i))
    else:
        a_spec = pl.BlockSpec((tm, tk), lambda i, j, k: (i, k))
    if mode == "nt":
        b_spec = pl.BlockSpec((tn, tk), lambda i, j, k: (j, k))
    else:
        b_spec = pl.BlockSpec((tk, tn), lambda i, j, k: (k, j))
    dims = {"nn": _NN, "nt": _NT, "tn": _TN}[mode]

    def body(a_ref, b_ref, o_ref, *acc):
        prod = _dot(a_ref[...].astype(BF16), b_ref[...].astype(BF16), dims)
        if nk == 1:
            o_ref[...] = prod.astype(o_ref.dtype)
            return
        acc_ref, k = acc[0], pl.program_id(2)

        @pl.when(k == 0)
        def _():
            acc_ref[...] = prod

        @pl.when(k > 0)
        def _():
            acc_ref[...] += prod

        @pl.when(k == nk - 1)
        def _():
            o_ref[...] = acc_ref[...].astype(o_ref.dtype)

    return pl.pallas_call(
        body, name=name, grid=(M // tm, N // tn, nk),
        in_specs=[a_spec, b_spec],
        out_specs=pl.BlockSpec((tm, tn), lambda i, j, k: (i, j)),
        out_shape=jax.ShapeDtypeStruct((M, N), out_dtype),
        scratch_shapes=[pltpu.VMEM((tm, tn), F32)] if nk > 1 else [],
        compiler_params=_params("parallel", "parallel", "arbitrary"),
    )(a, b)


def _rms_rows(x):
    return lax.rsqrt(jnp.mean(x * x, axis=-1, keepdims=True) + RMS_EPS)


def _rms_fwd(x, g, name, tr=512):
    S, D = x.shape
    tr = _fit(S, tr)

    def body(x_ref, g_ref, h_ref):
        xv = x_ref[...]
        h_ref[...] = (xv * _rms_rows(xv) * g_ref[...]).astype(BF16)

    return pl.pallas_call(
        body, name=name, grid=(S // tr,),
        in_specs=[pl.BlockSpec((tr, D), lambda i: (i, 0)), pl.BlockSpec((1, D), lambda i: (0, 0))],
        out_specs=pl.BlockSpec((tr, D), lambda i: (i, 0)),
        out_shape=jax.ShapeDtypeStruct((S, D), BF16),
        compiler_params=_params("parallel"),
    )(x, g)


def _resid_norm(xres, y, g_post, g_next, name, tr=1024):
    S, D = xres.shape
    tr = _fit(S, tr)

    def body(x_ref, y_ref, gp_ref, gn_ref, xn_ref, h_ref):
        yv = y_ref[...]
        xn = x_ref[...] + yv * _rms_rows(yv) * gp_ref[...]
        xn_ref[...] = xn
        h_ref[...] = (xn * _rms_rows(xn) * gn_ref[...]).astype(BF16)

    row = pl.BlockSpec((tr, D), lambda i: (i, 0))
    vec = pl.BlockSpec((1, D), lambda i: (0, 0))
    return pl.pallas_call(
        body, name=name, grid=(S // tr,),
        in_specs=[row, row, vec, vec], out_specs=[row, row],
        out_shape=[jax.ShapeDtypeStruct((S, D), F32), jax.ShapeDtypeStruct((S, D), BF16)],
        compiler_params=_params("parallel"),
    )(xres, y, g_post, g_next)


def _final_loss(xres, y, g_post, target, name, tr=1024):
    S, D = xres.shape
    tr = _fit(S, tr)

    def body(x_ref, y_ref, gp_ref, t_ref, d_ref, loss_ref):
        i = pl.program_id(0)
        yv = y_ref[...]
        err = x_ref[...] + yv * _rms_rows(yv) * gp_ref[...] - t_ref[...]
        d_ref[...] = err * (1.0 / D)

        @pl.when(i == 0)
        def _():
            loss_ref[...] = jnp.zeros_like(loss_ref)

        part = jnp.sum(jnp.sum(err * err, axis=1, keepdims=True), axis=0, keepdims=True)
        loss_ref[...] += jnp.broadcast_to(part * (0.5 / D), loss_ref.shape)

    row = pl.BlockSpec((tr, D), lambda i: (i, 0))
    vec = pl.BlockSpec((1, D), lambda i: (0, 0))
    return pl.pallas_call(
        body, name=name, grid=(S // tr,),
        in_specs=[row, row, vec, row],
        out_specs=[row, pl.BlockSpec((8, LANES), lambda i: (0, 0))],
        out_shape=[jax.ShapeDtypeStruct((S, D), F32), jax.ShapeDtypeStruct((8, LANES), F32)],
        compiler_params=_params("arbitrary"),
    )(xres, y, g_post, target)


def _rms_bwd(xin, g, dy, dres, name, want=("f32", "bf16"), tr=1024):
    S, D = xin.shape
    tr = _fit(S, tr)
    has_res = dres is not None

    def body(*refs):
        refs = list(refs)
        dg_ref = refs.pop()
        dxb_ref = refs.pop() if "bf16" in want else None
        dx_ref = refs.pop() if "f32" in want else None
        dr_ref = refs.pop() if has_res else None
        x_ref, g_ref, dy_ref = refs
        i = pl.program_id(0)
        xv = x_ref[...]
        dyv = dy_ref[...].astype(F32)
        xhat = xv * _rms_rows(xv)
        dxhat = dyv * g_ref[...]
        r = _rms_rows(xv)
        dx = r * (dxhat - xhat * jnp.mean(dxhat * xhat, axis=-1, keepdims=True))
        if has_res:
            dx = dx + dr_ref[...]
        if dx_ref is not None:
            dx_ref[...] = dx
        if dxb_ref is not None:
            dxb_ref[...] = dx.astype(BF16)

        @pl.when(i == 0)
        def _():
            dg_ref[...] = jnp.zeros_like(dg_ref)

        dg_ref[...] += jnp.broadcast_to(jnp.sum(dyv * xhat, axis=0, keepdims=True), dg_ref.shape)

    row = pl.BlockSpec((tr, D), lambda i: (i, 0))
    vec = pl.BlockSpec((1, D), lambda i: (0, 0))
    acc = pl.BlockSpec((8, D), lambda i: (0, 0))
    ins = [xin, g, dy] + ([dres] if has_res else [])
    dtypes = [dt for key, dt in (("f32", F32), ("bf16", BF16)) if key in want]
    outs = pl.pallas_call(
        body, name=name, grid=(S // tr,),
        in_specs=[row, vec, row] + ([row] if has_res else []),
        out_specs=[row] * len(dtypes) + [acc],
        out_shape=[jax.ShapeDtypeStruct((S, D), dt) for dt in dtypes] + [jax.ShapeDtypeStruct((8, D), F32)],
        compiler_params=_params("arbitrary"),
    )(*ins)
    by_key = dict(zip([key for key in ("f32", "bf16") if key in want], outs[:-1]))
    return by_key.get("f32"), by_key.get("bf16"), outs[-1]


def _tri(n, upper):
    r = lax.broadcasted_iota(jnp.int32, (n, n), 0)
    c = lax.broadcasted_iota(jnp.int32, (n, n), 1)
    return jnp.where((r <= c) if upper else (r >= c), 1.0, 0.0).astype(F32)


def _forget_fwd(fg_t, b_col, name, ts=512):
    H, S = fg_t.shape
    ts = _fit(S, ts)

    def body(f_ref, b_ref, c_ref, carry_ref):
        i = pl.program_id(0)

        @pl.when(i == 0)
        def _():
            carry_ref[...] = jnp.zeros_like(carry_ref)

        z = f_ref[...] + b_ref[...]
        logf = jnp.minimum(z, 0.0) - jnp.log(1.0 + jnp.exp(-jnp.abs(z)))
        run = lax.dot_general(logf, _tri(ts, True), (_NN, ((), ())), precision=lax.Precision.HIGHEST,
                              preferred_element_type=F32) + carry_ref[:, 0:1]
        c_ref[...] = run
        carry_ref[...] = jnp.broadcast_to(
            carry_ref[:, 0:1] + jnp.sum(logf, axis=1, keepdims=True), carry_ref.shape)

    return pl.pallas_call(
        body, name=name, grid=(S // ts,),
        in_specs=[pl.BlockSpec((H, ts), lambda i: (0, i)), pl.BlockSpec((H, 1), lambda i: (0, 0))],
        out_specs=pl.BlockSpec((H, ts), lambda i: (0, i)),
        out_shape=jax.ShapeDtypeStruct((H, S), F32),
        scratch_shapes=[pltpu.VMEM((H, LANES), F32)],
        compiler_params=_params("arbitrary"),
    )(fg_t, b_col)


def _forget_bwd(fg_t, b_col, dc_plus, dc_minus, name, ts=512):
    H, S = fg_t.shape
    ts = _fit(S, ts)
    nb = S // ts

    def body(f_ref, b_ref, dcp_ref, dcm_ref, df_ref, db_ref, carry_ref):
        i = pl.program_id(0)

        @pl.when(i == 0)
        def _():
            carry_ref[...] = jnp.zeros_like(carry_ref)
            db_ref[...] = jnp.zeros_like(db_ref)

        dc = dcp_ref[...] - dcm_ref[...]
        suffix = lax.dot_general(dc, _tri(ts, False), (_NN, ((), ())), precision=lax.Precision.HIGHEST,
                                 preferred_element_type=F32) + carry_ref[:, 0:1]
        z = f_ref[...] + b_ref[...]
        sig_neg = 1.0 / (1.0 + jnp.exp(z))
        df = suffix * sig_neg
        df_ref[...] = df
        carry_ref[...] = jnp.broadcast_to(
            carry_ref[:, 0:1] + jnp.sum(dc, axis=1, keepdims=True), carry_ref.shape)
        db_ref[...] += jnp.broadcast_to(jnp.sum(df, axis=1, keepdims=True), db_ref.shape)

    rev = pl.BlockSpec((H, ts), lambda i: (0, nb - 1 - i))
    return pl.pallas_call(
        body, name=name, grid=(nb,),
        in_specs=[rev, pl.BlockSpec((H, 1), lambda i: (0, 0)), rev, rev],
        out_specs=[rev, pl.BlockSpec((H, LANES), lambda i: (0, 0))],
        out_shape=[jax.ShapeDtypeStruct((H, S), F32), jax.ShapeDtypeStruct((H, LANES), F32)],
        scratch_shapes=[pltpu.VMEM((H, LANES), F32)],
        compiler_params=_params("arbitrary"),
    )(fg_t, b_col, dc_plus, dc_minus)


ONES_ROWS = 16
EXTRA = 3


def _split3(x):
    hi = lax.reduce_precision(x, 8, 7)
    mid = lax.reduce_precision(x - hi, 8, 7)
    lo = lax.reduce_precision(x - hi - mid, 8, 7)
    return hi.astype(BF16), mid.astype(BF16), lo.astype(BF16)


def _lanes_operand(t, extras):
    block = jnp.pad(jnp.stack(extras, axis=-1), ((0, 0), (0, 0), (0, LANES - HEAD_DIM - len(extras))))
    return jnp.concatenate([t, block], axis=-1)


def _rows_operand(t, extras):
    block = jnp.pad(jnp.stack(extras, axis=1), ((0, 0), (0, LANES - HEAD_DIM - len(extras)), (0, 0)))
    return jnp.concatenate([t, block], axis=1)


def _with_ones(t):
    return jnp.concatenate([t, jnp.ones((t.shape[0], ONES_ROWS, t.shape[2]), t.dtype)], axis=1)


def _fox_fwd(qa, ka, vt, name, tq=512, tk=1024):
    H, _, S = qa.shape
    Dh = HEAD_DIM
    tk = _fit(S, tk)
    tq = _fit(tk, tq)
    ratio = tk // tq

    def body(qa_ref, ka_ref, vt_ref, o_ref, lse_ref, m_ref, acc_ref, sa_ref, sb_ref, ta_ref, tb_ref):
        i = pl.program_id(1)
        qv = qa_ref[...] * QK_SCALE
        m_ref[...] = jnp.full_like(m_ref, NEG)
        acc_ref[...] = jnp.zeros_like(acc_ref)
        n = i // ratio
        q_off = (i - n * ratio) * tq

        def scores(j, s_ref, t_ref, diagonal):
            off = pl.multiple_of(j * tk, LANES)
            s = _dot(ka_ref[pl.ds(off, tk), :], qv, _NN)
            if diagonal:
                key = lax.broadcasted_iota(jnp.int32, (tk, tq), 0)
                qry = lax.broadcasted_iota(jnp.int32, (tk, tq), 1) + q_off
                s = jnp.where(key <= qry, s, NEG)
            s_ref[...] = s
            t_ref[...] = jnp.max(s, axis=0, keepdims=True)

        def absorb(j, s_ref, t_ref):
            off = pl.multiple_of(j * tk, LANES)
            m_old = m_ref[...]
            m_new = jnp.maximum(m_old, t_ref[...])
            p = jnp.exp(s_ref[...] - m_new)
            alpha = jnp.exp(m_old - m_new)
            acc_ref[...] = alpha * acc_ref[...] + _dot(vt_ref[:, pl.ds(off, tk)], p.astype(BF16), _NN)
            m_ref[...] = m_new

        scores(n, sa_ref, ta_ref, True)

        def loop_body(jj, carry):
            scores(2 * jj, sb_ref, tb_ref, False)
            absorb(jnp.where(jj == 0, n, 2 * jj - 1), sa_ref, ta_ref)
            scores(2 * jj + 1, sa_ref, ta_ref, False)
            absorb(2 * jj, sb_ref, tb_ref)
            return carry

        pairs = n // 2
        lax.fori_loop(0, pairs, loop_body, 0)
        held = jnp.where(pairs == 0, n, 2 * pairs - 1)

        @pl.when(n % 2 == 1)
        def _():
            scores(n - 1, sb_ref, tb_ref, False)
            absorb(held, sa_ref, ta_ref)
            absorb(n - 1, sb_ref, tb_ref)

        @pl.when(n % 2 == 0)
        def _():
            absorb(held, sa_ref, ta_ref)

        l = acc_ref[Dh:Dh + 1, :]
        o_ref[...] = acc_ref[0:Dh, :] / l
        lse_ref[...] = m_ref[...] + jnp.log(l)

    return pl.pallas_call(
        body, name=name, grid=(H, S // tq),
        in_specs=[pl.BlockSpec((None, LANES, tq), lambda h, i: (h, 0, i)),
                  pl.BlockSpec((None, S, LANES), lambda h, i: (h, 0, 0)),
                  pl.BlockSpec((None, Dh + ONES_ROWS, S), lambda h, i: (h, 0, 0))],
        out_specs=[pl.BlockSpec((None, Dh, tq), lambda h, i: (h, 0, i)),
                   pl.BlockSpec((None, 1, tq), lambda h, i: (h, 0, i))],
        out_shape=[jax.ShapeDtypeStruct((H, Dh, S), F32), jax.ShapeDtypeStruct((H, 1, S), F32)],
        scratch_shapes=[pltpu.VMEM((1, tq), F32), pltpu.VMEM((Dh + ONES_ROWS, tq), F32),
                        pltpu.VMEM((tk, tq), F32), pltpu.VMEM((tk, tq), F32),
                        pltpu.VMEM((1, tq), F32), pltpu.VMEM((1, tq), F32)],
        compiler_params=_params("parallel", "arbitrary"),
    )(qa, ka, vt)


def _fox_bwd(qa, ka, kta, va, doa, name, tq=1024, tk=512):
    H, _, S = qa.shape
    Dh, Da = HEAD_DIM, HEAD_DIM + ONES_ROWS
    tq = _fit(S, tq)
    tk = _fit(tq, tk)
    ratio = tq // tk
    nq = S // tq
    nk = S // tk

    def body(ka_ref, kta_ref, va_ref, qa_ref, doa_ref, dqt_ref, dkt_ref, dvt_ref, dka_ref, dva_ref):
        j = pl.program_id(1)

        @pl.when(j == 0)
        def _():
            dqt_ref[...] = jnp.zeros_like(dqt_ref)

        kv = ka_ref[...]
        ktv = kta_ref[0:Da, :]
        vv = va_ref[...]
        dka_ref[...] = jnp.zeros_like(dka_ref)
        dva_ref[...] = jnp.zeros_like(dva_ref)
        i_diag = j // ratio
        k_off = (j - i_diag * ratio) * tk

        def step(i, diagonal):
            off = pl.multiple_of(i * tq, LANES)
            qv = qa_ref[:, pl.ds(off, tq)] * QK_SCALE
            dov = doa_ref[:, pl.ds(off, tq)]
            e = _dot(kv, qv, _NN)
            if diagonal:
                key = lax.broadcasted_iota(jnp.int32, (tk, tq), 0) + k_off
                qry = lax.broadcasted_iota(jnp.int32, (tk, tq), 1)
                e = jnp.where(key <= qry, e, NEG)
            p_t = jnp.exp(e)
            dva_ref[...] += _dot(dov[0:Dh, :], p_t.astype(BF16), _NT)
            ds_b = (p_t * _dot(vv, dov, _NN)).astype(BF16)
            dka_ref[...] += _dot(qv[0:Da, :], ds_b, _NT)
            dqt_ref[:, pl.ds(off, tq)] += _dot(ktv, ds_b, _NN)

        step(i_diag, True)

        def loop_body(i, carry):
            step(i, False)
            return carry

        lax.fori_loop(i_diag + 1, nq, loop_body, 0)
        dkt_ref[...] = dka_ref[...]
        dvt_ref[...] = dva_ref[...]

        @pl.when(j == nk - 1)
        def _():
            dqt_ref[0:Dh, :] = dqt_ref[0:Dh, :] * QK_SCALE

    lanes_tile = pl.BlockSpec((None, tk, LANES), lambda h, j: (h, j, 0))
    rows_tile = pl.BlockSpec((None, LANES, tk), lambda h, j: (h, 0, j))
    rows_full = pl.BlockSpec((None, LANES, S), lambda h, j: (h, 0, 0))
    return pl.pallas_call(
        body, name=name, grid=(H, nk),
        in_specs=[lanes_tile, rows_tile, lanes_tile, rows_full, rows_full],
        out_specs=[pl.BlockSpec((None, Da, S), lambda h, j: (h, 0, 0)),
                   pl.BlockSpec((None, Da, tk), lambda h, j: (h, 0, j)),
                   pl.BlockSpec((None, Dh, tk), lambda h, j: (h, 0, j))],
        out_shape=[jax.ShapeDtypeStruct((H, Da, S), F32), jax.ShapeDtypeStruct((H, Da, S), F32),
                   jax.ShapeDtypeStruct((H, Dh, S), F32)],
        scratch_shapes=[pltpu.VMEM((Da, tk), F32), pltpu.VMEM((Dh, tk), F32)],
        compiler_params=_params("parallel", "arbitrary"),
    )(ka, kta, va, qa, doa)


DIL_Q_BLOCK = 3 * FOX_WIDTH // LANES
HEAD_PAIRS = N_DIL_HEADS // 2
PAIR_BLOCKS = DIL_WIDTH // LANES


def _band_geometry(S, d):
    L = S // d
    chunk = min(BAND_CHUNK_MAX, L)
    assert L % chunk == 0 and chunk % BAND == 0
    return L, chunk, chunk // BAND, L // chunk


def _band_in_specs(S, d, base):
    L, chunk, nb, _ = _band_geometry(S, d)

    def col(kind):
        return lambda hp, r, i: (r, i, base + kind * PAIR_BLOCKS + hp)

    def col_prev(kind):
        return lambda hp, r, i: (r, jnp.maximum(i * nb - 1, 0), base + kind * PAIR_BLOCKS + hp)

    main = [pl.BlockSpec((None, chunk, LANES), col(kind)) for kind in range(3)]
    prev = [pl.BlockSpec((None, BAND, LANES), col_prev(kind)) for kind in range(3)]
    bias = pl.BlockSpec((None, 2 * BAND, 2 * BAND), lambda hp, r, i: (hp, 0, 0))
    stat = pl.BlockSpec((None, 2, chunk), lambda hp, r, i: (hp, 0, r * (L // chunk) + i))
    tok = pl.BlockSpec((None, chunk, LANES), lambda hp, r, i: (r, i, hp))
    return main, prev, bias, stat, tok


def _to_residues(x, col_block, width, d, name, tr=1024):
    S = x.shape[0]
    tr = _fit(S, tr)

    def body(x_ref, o_ref, tmp_ref):
        for j in range(width // LANES):
            cols = slice(j * LANES, (j + 1) * LANES)
            tmp_ref[j] = x_ref[:, cols].astype(F32)
            for r in range(d):
                o_ref[r, :, cols] = tmp_ref[j, pl.ds(r, tr // d, stride=d), :].astype(o_ref.dtype)

    return pl.pallas_call(
        body, name=name, grid=(S // tr,),
        in_specs=[pl.BlockSpec((tr, width), lambda i: (i, col_block))],
        out_specs=pl.BlockSpec((d, tr // d, width), lambda i: (0, i, 0)),
        out_shape=jax.ShapeDtypeStruct((d, S // d, width), x.dtype),
        scratch_shapes=[pltpu.VMEM((width // LANES, tr, LANES), F32)],
        compiler_params=_params("parallel"),
    )(x)


def _token_rows(ref, cols, tmp_ref):
    if len(ref.shape) == 2:
        return ref[:, cols].astype(F32)
    d, rows = ref.shape[0], ref.shape[1]
    for r in range(d):
        tmp_ref[pl.ds(r, rows, stride=d), :] = ref[r, :, cols].astype(F32)
    return tmp_ref[...]


def _row_spec(t, tr):
    if t.ndim == 2:
        return pl.BlockSpec((tr, t.shape[1]), lambda i: (i, 0))
    d = t.shape[0]
    return pl.BlockSpec((d, tr // d, t.shape[2]), lambda i: (0, i, 0))


def _head_lanes(a):
    return lax.broadcasted_iota(jnp.int32, (1, LANES), 1) // HEAD_DIM == a


def _one_head(x, a):
    return jnp.where(_head_lanes(a), x, jnp.zeros_like(x))


def _head_stack(x):
    return jnp.concatenate([_one_head(x, 0), _one_head(x, 1)], axis=0)


def _pair_rows(ref, rows):
    return jnp.concatenate([ref[0:1, rows], ref[1:2, rows]], axis=1)


def _band_scores_t(kb, q_stack, bias_t, first):
    s = _dot(kb, q_stack, _NT) + bias_t
    if first is not None:
        key = lax.broadcasted_iota(jnp.int32, s.shape, 0)
        s = jnp.where(jnp.logical_and(first, key < BAND), NEG, s)
    return s


def _pair_select(stacked):
    return jnp.where(_head_lanes(0), stacked[0:BAND, :], stacked[BAND:, :])


def _dil_lse(qkv_v, base, bias_t, name):
    d, L = qkv_v.shape[:2]
    S = L * d
    _, chunk, nb, nchunks = _band_geometry(S, d)
    main, prev, bias, stat, _ = _band_in_specs(S, d, base)

    def body(q_ref, k_ref, kp_ref, b_ref, lse_ref, kext_ref):
        first = pl.program_id(2) == 0
        kext_ref[0:BAND, :] = kp_ref[...]
        kext_ref[BAND:, :] = k_ref[...]
        for b in range(nb):
            rows, ext = slice(b * BAND, (b + 1) * BAND), slice(b * BAND, (b + 2) * BAND)
            s = _band_scores_t(kext_ref[ext, :], _head_stack(q_ref[rows, :] * QK_SCALE), b_ref[...],
                               first if b == 0 else None)
            m = jnp.max(s, axis=0, keepdims=True)
            lse = m + jnp.log(jnp.sum(jnp.exp(s - m), axis=0, keepdims=True))
            lse_ref[0:1, rows] = lse[:, 0:BAND]
            lse_ref[1:2, rows] = lse[:, BAND:]

    return pl.pallas_call(
        body, name=name, grid=(HEAD_PAIRS, d, nchunks),
        in_specs=[main[0], main[1], prev[1], bias], out_specs=stat,
        out_shape=jax.ShapeDtypeStruct((HEAD_PAIRS, 2, S), F32),
        scratch_shapes=[pltpu.VMEM((chunk + BAND, LANES), BF16)],
        compiler_params=_params("parallel", "parallel", "parallel"),
    )(qkv_v, qkv_v, qkv_v, bias_t)


def _dil_out(qkv_v, base, bias_t, lse_joint, name):
    d, L = qkv_v.shape[:2]
    S = L * d
    _, chunk, nb, nchunks = _band_geometry(S, d)
    main, prev, bias, stat, tok = _band_in_specs(S, d, base)

    def body(q_ref, k_ref, kp_ref, v_ref, vp_ref, b_ref, lse_ref, o_ref, kext_ref, vext_ref):
        first = pl.program_id(2) == 0
        kext_ref[0:BAND, :] = kp_ref[...]
        kext_ref[BAND:, :] = k_ref[...]
        vext_ref[0:BAND, :] = vp_ref[...]
        vext_ref[BAND:, :] = v_ref[...]
        for b in range(nb):
            rows, ext = slice(b * BAND, (b + 1) * BAND), slice(b * BAND, (b + 2) * BAND)
            s = _band_scores_t(kext_ref[ext, :], _head_stack(q_ref[rows, :] * QK_SCALE), b_ref[...],
                               first if b == 0 else None)
            p_t = jnp.exp(s - _pair_rows(lse_ref, rows))
            o_ref[rows, :] = _pair_select(_dot(p_t.astype(BF16), vext_ref[ext, :], _TN)).astype(BF16)

    return pl.pallas_call(
        body, name=name, grid=(HEAD_PAIRS, d, nchunks),
        in_specs=[main[0], main[1], prev[1], main[2], prev[2], bias, stat], out_specs=tok,
        out_shape=jax.ShapeDtypeStruct((d, L, DIL_WIDTH), BF16),
        scratch_shapes=[pltpu.VMEM((chunk + BAND, LANES), BF16), pltpu.VMEM((chunk + BAND, LANES), BF16)],
        compiler_params=_params("parallel", "parallel", "parallel"),
    )(qkv_v, qkv_v, qkv_v, qkv_v, qkv_v, bias_t, lse_joint)


def _dil_bwd(qkv_v, base, do_v, bias_t, lse_joint, delta, name):
    d, L = qkv_v.shape[:2]
    S = L * d
    _, chunk, nb, nchunks = _band_geometry(S, d)
    main, prev, bias, stat, tok = _band_in_specs(S, d, base)
    nblocks = L // BAND

    def nxt_row(i):
        return jnp.minimum((i + 1) * nb, nblocks - 1)

    q_next = pl.BlockSpec((None, BAND, LANES), lambda hp, r, i: (r, nxt_row(i), base + hp))
    do_next = pl.BlockSpec((None, BAND, LANES), lambda hp, r, i: (r, nxt_row(i), hp))
    stat_next = pl.BlockSpec((None, 2, BAND), lambda hp, r, i: (hp, 0, r * nblocks + nxt_row(i)))

    def body(q_ref, k_ref, kp_ref, v_ref, vp_ref, do_ref, b_ref, lse_ref, dl_ref,
             qn_ref, don_ref, lsen_ref, dln_ref,
             dq_ref, dk_ref, dv_ref, db_ref, kext_ref, vext_ref, dkext_ref, dvext_ref):
        r, i = pl.program_id(1), pl.program_id(2)
        first = i == 0
        has_next = i + 1 < nchunks
        tail = slice(BAND + chunk, 2 * BAND + chunk)
        kext_ref[0:BAND, :] = kp_ref[...]
        kext_ref[BAND:BAND + chunk, :] = k_ref[...]
        kext_ref[tail, :] = jnp.zeros((BAND, LANES), BF16)
        vext_ref[0:BAND, :] = vp_ref[...]
        vext_ref[BAND:BAND + chunk, :] = v_ref[...]
        vext_ref[tail, :] = jnp.zeros((BAND, LANES), BF16)
        dkext_ref[...] = jnp.zeros_like(dkext_ref)
        dvext_ref[...] = jnp.zeros_like(dvext_ref)

        @pl.when(jnp.logical_and(r == 0, i == 0))
        def _():
            db_ref[...] = jnp.zeros_like(db_ref)

        def block(q2, do2, lse_row, dl_row, ext, mask_rows):
            q_stack, do_stack = _head_stack(q2), _head_stack(do2)
            s = _dot(kext_ref[ext, :], q_stack, _NT) + b_ref[...]
            if mask_rows is not None:
                s = jnp.where(mask_rows, NEG, s)
            p_t = jnp.exp(s - lse_row)
            ds_t = p_t * (_dot(vext_ref[ext, :], do_stack, _NT) - dl_row)
            ds_b = ds_t.astype(BF16)
            dkext_ref[ext, :] += _dot(ds_b, q_stack, _NN)
            dvext_ref[ext, :] += _dot(p_t.astype(BF16), do_stack, _NN)
            return ds_t, ds_b

        key = lax.broadcasted_iota(jnp.int32, (2 * BAND, 2 * BAND), 0)
        all_lanes = slice(0, BAND)
        for b in range(nb):
            rows, ext = slice(b * BAND, (b + 1) * BAND), slice(b * BAND, (b + 2) * BAND)
            mask = jnp.logical_and(first, key < BAND) if b == 0 else None
            ds_t, ds_b = block(q_ref[rows, :] * QK_SCALE, do_ref[rows, :], _pair_rows(lse_ref, rows),
                               _pair_rows(dl_ref, rows), ext, mask)
            dq_ref[rows, :] = (_pair_select(_dot(ds_b, kext_ref[ext, :], _TN)) * QK_SCALE).astype(BF16)
            db_ref[...] += ds_t
        block(qn_ref[...] * QK_SCALE, don_ref[...], _pair_rows(lsen_ref, all_lanes), _pair_rows(dln_ref, all_lanes),
              slice(chunk, chunk + 2 * BAND), jnp.logical_or(jnp.logical_not(has_next), key >= BAND))
        dk_ref[...] = dkext_ref[BAND:BAND + chunk, :].astype(BF16)
        dv_ref[...] = dvext_ref[BAND:BAND + chunk, :].astype(BF16)

    ext_rows = chunk + 2 * BAND
    return pl.pallas_call(
        body, name=name, grid=(HEAD_PAIRS, d, nchunks),
        in_specs=[main[0], main[1], prev[1], main[2], prev[2], tok, bias, stat, stat,
                  q_next, do_next, stat_next, stat_next],
        out_specs=[tok, tok, tok, bias],
        out_shape=[jax.ShapeDtypeStruct((d, L, DIL_WIDTH), BF16)] * 3
                  + [jax.ShapeDtypeStruct((HEAD_PAIRS, 2 * BAND, 2 * BAND), F32)],
        scratch_shapes=[pltpu.VMEM((ext_rows, LANES), BF16), pltpu.VMEM((ext_rows, LANES), BF16),
                        pltpu.VMEM((ext_rows, LANES), F32), pltpu.VMEM((ext_rows, LANES), F32)],
        compiler_params=_params("arbitrary", "arbitrary", "arbitrary"),
    )(qkv_v, qkv_v, qkv_v, qkv_v, qkv_v, do_v, bias_t, lse_joint, delta, qkv_v, do_v, lse_joint, delta)


def _lse_join(lse3, name):
    P, H, S = lse3.shape

    def body(l_ref, o_ref):
        a, b, c = l_ref[0], l_ref[1], l_ref[2]
        m = jnp.maximum(jnp.maximum(a, b), c)
        o_ref[...] = m + jnp.log(jnp.exp(a - m) + jnp.exp(b - m) + jnp.exp(c - m))

    return pl.pallas_call(body, name=name, out_shape=jax.ShapeDtypeStruct((H, S), F32))(lse3)


def _bucket_reduce(dbias_t, bucket_map_t, name):
    P, H = dbias_t.shape[:2]

    def body(db_ref, bk_ref, o_ref):
        p, h = pl.program_id(0), pl.program_id(1)

        @pl.when(jnp.logical_and(p == 0, h == 0))
        def _():
            o_ref[...] = jnp.zeros_like(o_ref)

        db, bk = db_ref[...], bk_ref[...]
        row = lax.broadcasted_iota(jnp.int32, (N_BUCKETS, LANES), 0)
        lane = lax.broadcasted_iota(jnp.int32, (N_BUCKETS, LANES), 1)

        def one(b, acc):
            val = jnp.sum(jnp.sum(jnp.where(bk == b, db, 0.0), axis=1, keepdims=True), axis=0, keepdims=True)
            return acc + jnp.where(jnp.logical_and(row == b, lane == h), val, 0.0)

        acc = jnp.zeros((N_BUCKETS, LANES), F32)
        for b in range(N_BUCKETS):
            acc = one(b, acc)
        o_ref[...] += acc

    return pl.pallas_call(
        body, name=name, grid=(P, H),
        in_specs=[pl.BlockSpec((None, None, 2 * BAND, BAND), lambda p, h: (p, h, 0, 0)),
                  pl.BlockSpec((None, 2 * BAND, BAND), lambda p, h: (p, 0, 0))],
        out_specs=pl.BlockSpec((N_BUCKETS, LANES), lambda p, h: (0, 0)),
        out_shape=jax.ShapeDtypeStruct((N_BUCKETS, LANES), F32),
        compiler_params=_params("arbitrary", "arbitrary"),
    )(dbias_t, bucket_map_t)


def _mem_fwd(q, kv, name, tq=1024):
    S, W = q.shape
    N = kv.shape[0]
    pairs = W // LANES
    tq = _fit(S, tq)

    def body(q_ref, k_ref, v_ref, o_ref, lse_ref):
        for a in range(2):
            lanes = slice(a * HEAD_DIM, (a + 1) * HEAD_DIM)
            s = _dot(k_ref[:, lanes], q_ref[:, lanes] * QK_SCALE, _NT)
            m = jnp.max(s, axis=0, keepdims=True)
            e = jnp.exp(s - m)
            l = jnp.sum(e, axis=0, keepdims=True)
            o_ref[:, lanes] = _dot((e / l).astype(BF16), v_ref[:, lanes], _TN).astype(BF16)
            lse_ref[a:a + 1, :] = m + jnp.log(l)

    return pl.pallas_call(
        body, name=name, grid=(pairs, S // tq),
        in_specs=[pl.BlockSpec((tq, LANES), lambda hp, i: (i, hp)),
                  pl.BlockSpec((N, LANES), lambda hp, i: (0, hp)),
                  pl.BlockSpec((N, LANES), lambda hp, i: (0, pairs + hp))],
        out_specs=[pl.BlockSpec((tq, LANES), lambda hp, i: (i, hp)),
                   pl.BlockSpec((None, 2, tq), lambda hp, i: (hp, 0, i))],
        out_shape=[jax.ShapeDtypeStruct((S, W), BF16), jax.ShapeDtypeStruct((pairs, 2, S), F32)],
        compiler_params=_params("parallel", "parallel"),
    )(q, kv, kv)


def _mem_bwd(q, kv, do, lse, delta, name, tq=1024):
    S, W = q.shape
    N = kv.shape[0]
    pairs = W // LANES
    tq = _fit(S, tq)

    def body(q_ref, k_ref, v_ref, do_ref, lse_ref, dl_ref, dq_ref, dk_ref, dv_ref):
        i = pl.program_id(1)

        @pl.when(i == 0)
        def _():
            dk_ref[...] = jnp.zeros_like(dk_ref)
            dv_ref[...] = jnp.zeros_like(dv_ref)

        for a in range(2):
            lanes = slice(a * HEAD_DIM, (a + 1) * HEAD_DIM)
            qv, dov = q_ref[:, lanes] * QK_SCALE, do_ref[:, lanes]
            kv_, vv = k_ref[:, lanes], v_ref[:, lanes]
            p_t = jnp.exp(_dot(kv_, qv, _NT) - lse_ref[a:a + 1, :])
            ds_t = p_t * (_dot(vv, dov, _NT) - dl_ref[a:a + 1, :])
            ds_b = ds_t.astype(BF16)
            dq_ref[:, lanes] = (_dot(ds_b, kv_, _TN) * QK_SCALE).astype(BF16)
            dk_ref[:, lanes] += _dot(ds_b, qv, _NN)
            dv_ref[:, lanes] += _dot(p_t.astype(BF16), dov, _NN)

    qs = pl.BlockSpec((tq, LANES), lambda hp, i: (i, hp))
    stat = pl.BlockSpec((None, 2, tq), lambda hp, i: (hp, 0, i))
    acc = pl.BlockSpec((N, LANES), lambda hp, i: (0, hp))
    return pl.pallas_call(
        body, name=name, grid=(pairs, S // tq),
        in_specs=[qs, acc, pl.BlockSpec((N, LANES), lambda hp, i: (0, pairs + hp)), qs, stat, stat],
        out_specs=[qs, acc, acc],
        out_shape=[jax.ShapeDtypeStruct((S, W), BF16), jax.ShapeDtypeStruct((N, W), F32),
                   jax.ShapeDtypeStruct((N, W), F32)],
        compiler_params=_params("parallel", "arbitrary"),
    )(q, kv, kv, do, lse, delta)


def _head_rowdot(a, bs, name, tr=512):
    S, W = a.shape
    tr = _fit(S, tr)

    def body(*refs):
        a_ref, b_refs, o_ref, tmp_ref = refs[0], refs[1:-2], refs[-2], refs[-1]
        col = lax.broadcasted_iota(jnp.int32, (LANES, LANES), 0)
        lane = lax.broadcasted_iota(jnp.int32, (LANES, LANES), 1)
        acc = jnp.zeros((tr, LANES), F32)
        for j in range(W // LANES):
            cols = slice(j * LANES, (j + 1) * LANES)
            tot = _token_rows(b_refs[0], cols, tmp_ref)
            for r in b_refs[1:]:
                tot = tot + _token_rows(r, cols, tmp_ref)
            sel = jnp.where(col // HEAD_DIM + j * (LANES // HEAD_DIM) == lane, 1.0, 0.0).astype(F32)
            acc = acc + lax.dot_general(a_ref[:, cols].astype(F32) * tot, sel, (_NN, ((), ())),
                                        precision=lax.Precision.HIGHEST, preferred_element_type=F32)
        o_ref[...] = acc

    return pl.pallas_call(
        body, name=name, grid=(S // tr,), in_specs=[_row_spec(t, tr) for t in [a] + list(bs)],
        out_specs=pl.BlockSpec((tr, LANES), lambda i: (i, 0)),
        out_shape=jax.ShapeDtypeStruct((S, LANES), F32),
        scratch_shapes=[pltpu.VMEM((tr, LANES), F32)],
        compiler_params=_params("parallel"),
    )(a, *bs)


def _sum_cast_cols(groups, out_dtype, name, tail=None, tr=512):
    first = groups[0][0]
    S, W = (first.shape if first.ndim == 2 else (first.shape[0] * first.shape[1], first.shape[2]))
    tr = _fit(S, tr)
    flat = [t for g in groups for t in g] + ([tail] if tail is not None else [])
    tail_w = 0 if tail is None else tail.shape[1]

    def body(*refs):
        o_ref, tmp_ref = refs[-2], refs[-1]
        if tail is not None:
            o_ref[:, W * len(groups):] = refs[-3][...].astype(out_dtype)
        k = 0
        for gi, g in enumerate(groups):
            for j in range(W // LANES):
                cols = slice(j * LANES, (j + 1) * LANES)
                acc = _token_rows(refs[k], cols, tmp_ref)
                for r in refs[k + 1:k + len(g)]:
                    acc = acc + _token_rows(r, cols, tmp_ref)
                o_ref[:, gi * W + j * LANES:gi * W + (j + 1) * LANES] = acc.astype(out_dtype)
            k += len(g)

    return pl.pallas_call(
        body, name=name, grid=(S // tr,), in_specs=[_row_spec(t, tr) for t in flat],
        out_specs=pl.BlockSpec((tr, W * len(groups) + tail_w), lambda i: (i, 0)),
        out_shape=jax.ShapeDtypeStruct((S, W * len(groups) + tail_w), out_dtype),
        scratch_shapes=[pltpu.VMEM((tr, LANES), F32)],
        compiler_params=_params("parallel"),
    )(*flat)


FF_TILE = 256


def _ffn_up(h, w_gu, name, tm=4096):
    S, D = h.shape
    F2 = w_gu.shape[1]
    tm = _fit(S, tm)

    def body(h_ref, w_ref, gu_ref, act_ref):
        gu = _dot(h_ref[...], w_ref[...], _NN)
        gu_ref[...] = gu.astype(BF16)
        g, u = gu[:, :FF_TILE], gu[:, FF_TILE:]
        act_ref[...] = (g * (1.0 / (1.0 + jnp.exp(-g))) * u).astype(BF16)

    return pl.pallas_call(
        body, name=name, grid=(S // tm, F2 // (2 * FF_TILE)),
        in_specs=[pl.BlockSpec((tm, D), lambda i, j: (i, 0)), pl.BlockSpec((D, 2 * FF_TILE), lambda i, j: (0, j))],
        out_specs=[pl.BlockSpec((tm, 2 * FF_TILE), lambda i, j: (i, j)),
                   pl.BlockSpec((tm, FF_TILE), lambda i, j: (i, j))],
        out_shape=[jax.ShapeDtypeStruct((S, F2), BF16), jax.ShapeDtypeStruct((S, F2 // 2), BF16)],
        compiler_params=_params("parallel", "arbitrary"),
    )(h, w_gu)


def _ffn_dact(dy, w_down, gu, name, tm=4096):
    S, D = dy.shape
    F2 = gu.shape[1]
    tm = _fit(S, tm)

    def body(dy_ref, w_ref, gu_ref, dgu_ref):
        dact = _dot(dy_ref[...], w_ref[...], _NT)
        gu_v = gu_ref[...].astype(F32)
        g, u = gu_v[:, :FF_TILE], gu_v[:, FF_TILE:]
        sig = 1.0 / (1.0 + jnp.exp(-g))
        silu = g * sig
        dgu_ref[:, :FF_TILE] = (dact * u * (sig + silu * (1.0 - sig))).astype(BF16)
        dgu_ref[:, FF_TILE:] = (dact * silu).astype(BF16)

    return pl.pallas_call(
        body, name=name, grid=(S // tm, F2 // (2 * FF_TILE)),
        in_specs=[pl.BlockSpec((tm, D), lambda i, j: (i, 0)), pl.BlockSpec((FF_TILE, D), lambda i, j: (j, 0)),
                  pl.BlockSpec((tm, 2 * FF_TILE), lambda i, j: (i, j))],
        out_specs=pl.BlockSpec((tm, 2 * FF_TILE), lambda i, j: (i, j)),
        out_shape=jax.ShapeDtypeStruct((S, F2), BF16),
        compiler_params=_params("parallel", "arbitrary"),
    )(dy, w_down, gu)


def _fit_rows(n, cap):
    if n <= cap:
        return n
    t = (cap // 8) * 8
    while t >= 8:
        if n % t == 0:
            return t
        t -= 8
    raise ValueError(f"no sublane-aligned tile for {n} under {cap}")


def _add_n(arrs, name, tr=512):
    R, C = arrs[0].shape
    tr = _fit_rows(R, tr)

    def body(*refs):
        acc = refs[0][...].astype(F32)
        for r in refs[1:-1]:
            acc = acc + r[...].astype(F32)
        refs[-1][...] = acc

    row = pl.BlockSpec((tr, C), lambda i: (i, 0))
    return pl.pallas_call(
        body, name=name, grid=(R // tr,), in_specs=[row] * len(arrs), out_specs=row,
        out_shape=jax.ShapeDtypeStruct((R, C), F32), compiler_params=_params("parallel"),
    )(*arrs)


def _adamw(w, g, m, v, name, tr=512):
    R, C = w.shape
    tr = _fit_rows(R, tr)
    c1 = 1.0 / (1.0 - ADAM_B1 ** ADAM_STEP)
    c2 = 1.0 / (1.0 - ADAM_B2 ** ADAM_STEP)

    def body(w_ref, g_ref, m_ref, v_ref, d_ref, nm_ref, nv_ref):
        gv = g_ref[...]
        nm = ADAM_B1 * m_ref[...] + (1.0 - ADAM_B1) * gv
        nv = ADAM_B2 * v_ref[...] + (1.0 - ADAM_B2) * (gv * gv)
        nm_ref[...] = nm
        nv_ref[...] = nv
        d_ref[...] = -ADAM_LR * ((nm * c1) / (jnp.sqrt(nv * c2) + ADAM_EPS) + ADAM_WD * w_ref[...])

    row = pl.BlockSpec((tr, C), lambda i: (i, 0))
    return pl.pallas_call(
        body, name=name, grid=(R // tr,), in_specs=[row] * 4, out_specs=[row] * 3,
        out_shape=[jax.ShapeDtypeStruct((R, C), F32)] * 3, compiler_params=_params("parallel"),
    )(w, g, m, v)


def _place():
    return lax.axis_index("x"), lax.axis_index("y"), lax.axis_index("c")


_ANY = pl.BlockSpec(memory_space=pl.ANY)


def _chip_all_gather(shard, name):
    R, C = shard.shape
    half = R // 2

    def body(x_ref, out_ref, send_sems, recv_sems, local_sem):
        x, y, c = _place()
        chips = [(1 - x, y), (x, 1 - y), (1 - x, 1 - y)]
        sibling = (x, y, 1 - c)
        mine = pltpu.make_async_copy(x_ref, out_ref.at[2 * x + y], local_sem)
        mine.start()

        def rows(chip, core):
            return out_ref.at[chip, pl.ds(core * half, half)]

        def copy(k, chip, core, to, src=None):
            return pltpu.make_async_remote_copy(
                src_ref=rows(chip, core) if src is None else src, dst_ref=rows(chip, core),
                send_sem=send_sems.at[k], recv_sem=recv_sems.at[k], device_id=to, device_id_type=MESH_IDS)

        me = 2 * x + y
        first = [copy(k, me, c, (cx, cy, c), src=x_ref.at[pl.ds(c * half, half)]) for k, (cx, cy) in enumerate(chips)]
        for cp in first:
            cp.start()
        passed = [copy(3 + k, 2 * cx + cy, c, sibling) for k, (cx, cy) in enumerate(chips)]
        for k, (cx, cy) in enumerate(chips):
            copy(k, 2 * cx + cy, c, (cx, cy, c)).wait_recv()
            passed[k].start()
        for k, (cx, cy) in enumerate(chips):
            copy(3 + k, 2 * cx + cy, 1 - c, sibling).wait_recv()
        for cp in first + passed:
            cp.wait_send()
        mine.wait()

    return pl.pallas_call(
        body, name=name, in_specs=[_ANY], out_specs=_ANY,
        out_shape=jax.ShapeDtypeStruct((N_CHIPS, R, C), shard.dtype),
        scratch_shapes=[pltpu.SemaphoreType.DMA((6,)), pltpu.SemaphoreType.DMA((6,)), pltpu.SemaphoreType.DMA],
    )(shard)


def _sibling_exchange(buf, name):
    def body(x_ref, out_ref, send_sem, recv_sem):
        x, y, c = _place()
        cp = pltpu.make_async_remote_copy(
            src_ref=x_ref, dst_ref=out_ref, send_sem=send_sem, recv_sem=recv_sem,
            device_id=(x, y, 1 - c), device_id_type=MESH_IDS)
        cp.start()
        cp.wait()

    return pl.pallas_call(
        body, name=name, in_specs=[_ANY], out_specs=_ANY,
        out_shape=jax.ShapeDtypeStruct(buf.shape, buf.dtype),
        scratch_shapes=[pltpu.SemaphoreType.DMA, pltpu.SemaphoreType.DMA],
    )(buf)


def _chip_scatter(parts, name):
    _, R, C = parts.shape

    def body(p_ref, out_ref, send_sems, recv_sems):
        x, y, c = _place()
        chips = [(1 - x, y), (x, 1 - y), (1 - x, 1 - y)]

        def copy(k, slab, to):
            return pltpu.make_async_remote_copy(
                src_ref=p_ref.at[slab], dst_ref=out_ref.at[k], send_sem=send_sems.at[k], recv_sem=recv_sems.at[k],
                device_id=to, device_id_type=MESH_IDS)

        sends = [copy(k, 2 * cx + cy, (cx, cy, c)) for k, (cx, cy) in enumerate(chips)]
        for cp in sends:
            cp.start()
        for cp in sends:
            cp.wait_recv()
        for cp in sends:
            cp.wait_send()

    return pl.pallas_call(
        body, name=name, in_specs=[_ANY], out_specs=_ANY,
        out_shape=jax.ShapeDtypeStruct((3, R, C), parts.dtype),
        scratch_shapes=[pltpu.SemaphoreType.DMA((3,)), pltpu.SemaphoreType.DMA((3,))],
    )(parts)


def _all_to_all_small(vec, name):
    R, C = vec.shape

    def body(v_ref, out_ref, send_sems, recv_sems, local_sem):
        x, y, c = _place()
        me = 4 * x + 2 * y + c
        mine = pltpu.make_async_copy(v_ref, out_ref.at[me], local_sem)
        mine.start()
        flips = [(dx, dy, dc) for dx in (0, 1) for dy in (0, 1) for dc in (0, 1)][1:]

        def peer(f):
            return (x ^ f[0], y ^ f[1], c ^ f[2])

        def copy(k, slot, to):
            return pltpu.make_async_remote_copy(
                src_ref=v_ref, dst_ref=out_ref.at[slot], send_sem=send_sems.at[k], recv_sem=recv_sems.at[k],
                device_id=to, device_id_type=MESH_IDS)

        sends = [copy(k, me, peer(f)) for k, f in enumerate(flips)]
        for cp in sends:
            cp.start()
        for k, f in enumerate(flips):
            px, py, pc = peer(f)
            copy(k, 4 * px + 2 * py + pc, peer(f)).wait_recv()
        for cp in sends:
            cp.wait_send()
        mine.wait()

    return pl.pallas_call(
        body, name=name, in_specs=[_ANY], out_specs=_ANY,
        out_shape=jax.ShapeDtypeStruct((8, R, C), vec.dtype),
        scratch_shapes=[pltpu.SemaphoreType.DMA((7,)), pltpu.SemaphoreType.DMA((7,)), pltpu.SemaphoreType.DMA],
    )(vec)


def _to_heads(t, n):
    S = t.shape[0]
    return t.reshape(S, n, HEAD_DIM).transpose(1, 0, 2)


def _to_heads_t(t, n):
    S = t.shape[0]
    return t.T.reshape(n, HEAD_DIM, S)


def _from_heads_t(t):
    H, Dh, S = t.shape
    return t.reshape(H * Dh, S).T


def _t5_bucket(dist):
    max_exact = N_BUCKETS // 2
    d = np.maximum(dist, 1).astype(np.float32)
    large = max_exact + (np.log(d / max_exact) / np.log(MAX_DISTANCE / max_exact)
                         * (N_BUCKETS - max_exact)).astype(np.int32)
    large = np.minimum(large, N_BUCKETS - 1)
    return np.where(dist < max_exact, dist, large).astype(np.int32)


def _band_tables():
    qi = np.arange(BAND)[:, None]
    kj = np.arange(2 * BAND)[None, :]
    sub = qi + BAND - kj
    band = (sub >= 0) & (sub <= BAND)
    out = []
    for d in DILATIONS:
        bucket = _t5_bucket(np.clip(sub, 0, BAND) * d)
        out.append(np.where(band, bucket, -1).astype(np.int32))
    return np.stack(out)


_PACK = (("w_in", 770), ("w_out", 256), ("w_xq", 64), ("w_xk", 64), ("w_xv", 64), ("w_xo", 64),
         ("w_gate", 704), ("w_up", 704), ("w_down", 704))


def _pack(shards):
    rows = [shards[n].reshape(-1, PACK_COLS) for n, _ in _PACK]
    total = sum(r.shape[0] for r in rows)
    pad = (-total) % 128
    if pad:
        rows.append(jnp.zeros((pad, PACK_COLS), rows[0].dtype))
    return jnp.concatenate(rows, axis=0)


def _unpack(pack, shapes):
    out, r = {}, 0
    for n, _ in _PACK:
        cnt = int(np.prod(shapes[n])) // PACK_COLS
        out[n] = pack[r:r + cnt].reshape(shapes[n])
        r += cnt
    return out


_COL_SHARDED = ("w_in", "w_xo", "w_gate", "w_up")


def _full_weight(gathered, name):
    return jnp.concatenate(gathered, axis=1 if name in _COL_SHARDED else 0)


def _split_weight(full, name):
    return jnp.split(full, N_CHIPS, axis=1 if name in _COL_SHARDED else 0)


_SMALL = ("g_mix_pre", "g_mix_post", "g_xattn_pre", "g_mem", "g_xattn_post", "g_ffn_pre", "g_ffn_post")


def _pack_small(vals):
    D = vals["g_mix_pre"].shape[1]
    rows = [vals[n].reshape(1, D) for n in _SMALL]
    misc = jnp.concatenate([vals["b_f"].reshape(-1), vals["rel_bias"].reshape(-1)])
    rows.append(jnp.pad(misc, (0, D - misc.shape[0])).reshape(1, D))
    rows.append(jnp.zeros((16 - len(rows), D), F32))
    return jnp.concatenate(rows, axis=0)


def _unpack_small(pack):
    out = {n: pack[i:i + 1] for i, n in enumerate(_SMALL)}
    out["b_f"] = pack[7, 0:N_FOX_HEADS].reshape(1, N_FOX_HEADS)
    out["rel_bias"] = pack[7, N_FOX_HEADS:N_FOX_HEADS + N_BUCKETS * N_DIL_HEADS].reshape(N_BUCKETS, N_DIL_HEADS)
    return out


def kernel(x, mem, g_mix_pre, w_in, b_f, rel_bias, w_out, g_mix_post, g_xattn_pre, g_mem, w_xq, w_xk, w_xv, w_xo, g_xattn_post, g_ffn_pre, w_gate, w_up, w_down, g_ffn_post, loss_target, m_g_mix_pre, m_w_in, m_b_f, m_rel_bias, m_w_out, m_g_mix_post, m_g_xattn_pre, m_g_mem, m_w_xq, m_w_xk, m_w_xv, m_w_xo, m_g_xattn_post, m_g_ffn_pre, m_w_gate, m_w_up, m_w_down, m_g_ffn_post, v_g_mix_pre, v_w_in, v_b_f, v_rel_bias, v_w_out, v_g_mix_post, v_g_xattn_pre, v_g_mem, v_w_xq, v_w_xk, v_w_xv, v_w_xo, v_g_xattn_post, v_g_ffn_pre, v_w_gate, v_w_up, v_w_down, v_g_ffn_post):
    args = dict(locals())
    big = [n for n, _ in _PACK]
    names = ["g_mix_pre", "w_in", "b_f", "rel_bias", "w_out", "g_mix_post", "g_xattn_pre", "g_mem", "w_xq",
             "w_xk", "w_xv", "w_xo", "g_xattn_post", "g_ffn_pre", "w_gate", "w_up", "w_down", "g_ffn_post"]
    xs = x[0]
    S, D = xs.shape
    assert S % (BAND * DILATIONS[-1]) == 0
    shard_shapes = {n: args[n].shape[1:] for n in big}
    my_x, my_y, my_c = lax.axis_index("x"), lax.axis_index("y"), lax.axis_index("c")

    gathered = _chip_all_gather(_pack({n: args[n][0].astype(BF16) for n in big}), "weights_all_gather")
    per_chip = [_unpack(gathered[j], shard_shapes) for j in range(N_CHIPS)]
    W = {n: _full_weight([pc[n] for pc in per_chip], n) for n in big}
    w_fox, w_fg, w_dil = (W["w_in"][:, :3 * FOX_WIDTH], W["w_in"][:, 3 * FOX_WIDTH:3 * FOX_WIDTH + N_FOX_HEADS],
                          W["w_in"][:, 3 * FOX_WIDTH + N_FOX_HEADS:])
    w_qkv = jnp.concatenate([w_fox, w_dil], axis=1)
    w_fg_pad = jnp.pad(w_fg, ((0, 0), (0, LANES - N_FOX_HEADS)))
    F = W["w_gate"].shape[1]
    nft = F // FF_TILE
    w_gu = jnp.stack([W["w_gate"].reshape(D, nft, FF_TILE), W["w_up"].reshape(D, nft, FF_TILE)],
                     axis=2).reshape(D, 2 * F)

    h1 = _rms_fwd(xs, g_mix_pre, "rms_mix_pre")
    qkv = _mm(h1, w_qkv, "nn", BF16, "proj_qkv", tm=2048)
    fg = _mm(h1, w_fg_pad, "nn", F32, "proj_gate")
    fg_t = fg[:, :N_FOX_HEADS].T
    b_col = b_f.reshape(N_FOX_HEADS, 1)
    c_t = _forget_fwd(fg_t, b_col, "forget_cumsum")
    fq_s, fk_s, fv_s = (qkv[:, i * FOX_WIDTH:(i + 1) * FOX_WIDTH] for i in range(3))
    fqt, fvt = _to_heads_t(fq_s, N_FOX_HEADS), _to_heads_t(fv_s, N_FOX_HEADS)
    unit = jnp.full((N_FOX_HEADS, S), 1.0, BF16)
    inv_scale = jnp.full((N_FOX_HEADS, S), 1.0 / QK_SCALE, BF16)
    ka = _lanes_operand(_to_heads(fk_s, N_FOX_HEADS), list(_split3(-c_t)) + [unit] * EXTRA)
    qa_f = _rows_operand(fqt, [inv_scale] * EXTRA)
    o_fox_t, lse_fox = _fox_fwd(qa_f, ka, _with_ones(fvt), "fox_fwd")

    bucket_map = _band_tables()
    onehot = (jnp.asarray(bucket_map)[..., None] == jnp.arange(N_BUCKETS)).astype(F32)
    bias_tab = jnp.einsum("pqkb,bh->phkq", onehot, rel_bias, precision=lax.Precision.HIGHEST)
    bias_tab = jnp.where(jnp.asarray(bucket_map.transpose(0, 2, 1) >= 0)[:, None], bias_tab, NEG)
    bias_t = bias_tab.reshape(3, HEAD_PAIRS, 2, 2 * BAND, BAND).transpose(0, 1, 3, 2, 4).reshape(
        3, HEAD_PAIRS, 2 * BAND, 2 * BAND)
    views = [(qkv.reshape(1, S, qkv.shape[1]), DIL_Q_BLOCK)] + [
        (_to_residues(qkv, 1, 3 * DIL_WIDTH, d, f"dilated_qkv_residues_{d}"), 0) for d in DILATIONS[1:]]

    def to_tok(stat, d):
        return stat.reshape(N_DIL_HEADS, d, S // d).swapaxes(1, 2).reshape(N_DIL_HEADS, S)

    def to_perm(stat, d):
        return stat.reshape(N_DIL_HEADS, S // d, d).swapaxes(1, 2).reshape(HEAD_PAIRS, 2, S)

    def tok_or_res(t):
        return t.reshape(t.shape[1:]) if t.shape[0] == 1 else t

    lse_tok = jnp.stack([to_tok(_dil_lse(*views[p], bias_t[p], f"dilated_lse_{d}"), d)
                         for p, d in enumerate(DILATIONS)])
    lse_joint = _lse_join(lse_tok, "dilated_lse_join")
    lse_perm = [to_perm(lse_joint, d) for d in DILATIONS]
    o_dil = [tok_or_res(_dil_out(*views[p], bias_t[p], lse_perm[p], f"dilated_out_{d}"))
             for p, d in enumerate(DILATIONS)]
    o_cat = _sum_cast_cols([[_from_heads_t(o_fox_t)]] + [[o] for o in o_dil], BF16, "mixer_out_cat")
    w_out_b = W["w_out"]
    w_out_cat = jnp.concatenate([w_out_b[:FOX_WIDTH]] + [w_out_b[FOX_WIDTH:]] * 3, axis=0)
    a = _mm(o_cat, w_out_cat, "nn", F32, "proj_out", tm=2048, tk=2048)
    x1, h2 = _resid_norm(xs, a, g_mix_post, g_xattn_pre, "resid_mix")

    hm = _rms_fwd(mem[0], g_mem, "rms_mem")
    q2 = _mm(h2, W["w_xq"], "nn", BF16, "xattn_q")
    w_xkv = jnp.concatenate([W["w_xk"], W["w_xv"]], axis=1)
    kvm = _mm(hm, w_xkv, "nn", BF16, "xattn_kv")
    MW = N_MEM_HEADS * HEAD_DIM
    oc, lse_mem = _mem_fwd(q2, kvm, "xattn_fwd")
    y2 = _mm(oc, W["w_xo"], "nn", F32, "xattn_o")
    x2, h3 = _resid_norm(x1, y2, g_xattn_post, g_ffn_pre, "resid_xattn")

    gu, act = _ffn_up(h3, w_gu, "ffn_up")
    y3 = _mm(act, W["w_down"], "nn", F32, "ffn_down", tk=2816)
    dx3, loss_tile = _final_loss(x2, y3, g_ffn_post, loss_target[0], "final_loss")

    grads = {}
    small = {}
    _, dy3_b, dg = _rms_bwd(y3, g_ffn_post, dx3, None, "bwd_norm_ffn_post", want=("bf16",))
    small["g_ffn_post"] = dg[0:1]
    grads["w_down"] = _mm(act, dy3_b, "tn", F32, "grad_w_down", tm=1408)
    dgu = _ffn_dact(dy3_b, W["w_down"], gu, "ffn_dact")
    dw_gu = _mm(h3, dgu, "tn", F32, "grad_w_gu", tn=1408).reshape(D, nft, 2, FF_TILE)
    grads["w_gate"], grads["w_up"] = dw_gu[:, :, 0].reshape(D, F), dw_gu[:, :, 1].reshape(D, F)
    dh3 = _mm(dgu, w_gu, "nt", F32, "bwd_ffn_in", tm=2048, tk=1408)
    dx2, _, dg = _rms_bwd(x2, g_ffn_pre, dh3, dx3, "bwd_norm_ffn_pre", want=("f32",))
    small["g_ffn_pre"] = dg[0:1]

    _, dy2_b, dg = _rms_bwd(y2, g_xattn_post, dx2, None, "bwd_norm_xattn_post", want=("bf16",))
    small["g_xattn_post"] = dg[0:1]
    grads["w_xo"] = _mm(oc, dy2_b, "tn", F32, "grad_w_xo")
    doc = _mm(dy2_b, W["w_xo"], "nt", BF16, "bwd_xattn_o")
    delta_mem = _head_rowdot(doc, [oc], "xattn_delta")[:, :N_MEM_HEADS].T.reshape(N_MEM_HEADS // 2, 2, S)
    dq2, dkm, dvm = _mem_bwd(q2, kvm, doc, lse_mem, delta_mem, "xattn_bwd")
    dkvm = jnp.concatenate([dkm, dvm], axis=1).astype(BF16)
    grads["w_xq"] = _mm(h2, dq2, "tn", F32, "grad_w_xq")
    dw_xkv = _mm(hm, dkvm, "tn", F32, "grad_w_xkv")
    grads["w_xk"], grads["w_xv"] = dw_xkv[:, :MW], dw_xkv[:, MW:]
    dhm = _mm(dkvm, w_xkv, "nt", F32, "bwd_xattn_kv")
    _, _, dg = _rms_bwd(mem[0], g_mem, dhm, None, "bwd_norm_mem", want=())
    small["g_mem"] = dg[0:1]
    dh2 = _mm(dq2, W["w_xq"], "nt", F32, "bwd_xattn_q")
    dx1, _, dg = _rms_bwd(x1, g_xattn_pre, dh2, dx2, "bwd_norm_xattn_pre", want=("f32",))
    small["g_xattn_pre"] = dg[0:1]

    _, da_b, dg = _rms_bwd(a, g_mix_post, dx1, None, "bwd_norm_mix_post", want=("bf16",))
    small["g_mix_post"] = dg[0:1]
    dw_out_cat = _mm(o_cat, da_b, "tn", F32, "grad_w_out")
    dw_out_dil = _add_n([dw_out_cat[FOX_WIDTH + p * DIL_WIDTH:FOX_WIDTH + (p + 1) * DIL_WIDTH] for p in range(3)],
                        "grad_w_out_dil")
    grads["w_out"] = jnp.concatenate([dw_out_cat[:FOX_WIDTH], dw_out_dil], axis=0)
    do = _mm(da_b, w_out_b, "nt", BF16, "bwd_proj_out")
    do_fox, do_dil = do[:, :FOX_WIDTH], do[:, FOX_WIDTH:]

    delta_fox = _head_rowdot(do_fox, [o_cat[:, :FOX_WIDTH]], "fox_delta")[:, :N_FOX_HEADS].T
    qa_b = lax.dynamic_update_slice(qa_f, jnp.stack(_split3(lse_fox[:, 0] * (-1.0 / QK_SCALE)), axis=1),
                                    (0, HEAD_DIM + EXTRA, 0))
    va = _lanes_operand(_to_heads(fv_s, N_FOX_HEADS), [unit] * EXTRA)
    doa = _rows_operand(_to_heads_t(do_fox, N_FOX_HEADS), list(_split3(-delta_fox)))
    dq_aug, dk_aug, dvf = _fox_bwd(qa_b, ka, ka.transpose(0, 2, 1), va, doa, "fox_bwd")
    dqf, dkf = dq_aug[:, :HEAD_DIM], dk_aug[:, :HEAD_DIM]
    dfg_t, db_f = _forget_bwd(fg_t, b_col, dq_aug[:, HEAD_DIM + EXTRA], dk_aug[:, HEAD_DIM], "forget_bwd")

    delta_dil = _head_rowdot(do_dil, o_dil, "dilated_delta")[:, :N_DIL_HEADS].T
    do_res = [do_dil.reshape(1, S, DIL_WIDTH)] + [
        _to_residues(do, 1, DIL_WIDTH, d, f"dilated_do_residues_{d}") for d in DILATIONS[1:]]
    dil_grads = [_dil_bwd(*views[p], do_res[p], bias_t[p], lse_perm[p], to_perm(delta_dil, d), f"dilated_bwd_{d}")
                 for p, d in enumerate(DILATIONS)]
    dbias_t = jnp.stack([g[3].reshape(HEAD_PAIRS, 2 * BAND, 2, BAND).transpose(0, 2, 1, 3).reshape(
        N_DIL_HEADS, 2 * BAND, BAND) for g in dil_grads])
    d_rel = _bucket_reduce(dbias_t, jnp.asarray(bucket_map.transpose(0, 2, 1)), "rel_bias_grad")[:, :N_DIL_HEADS]
    dfg_pad = jnp.pad(dfg_t.T, ((0, 0), (0, LANES - N_FOX_HEADS))).astype(BF16)
    dcat = _sum_cast_cols([[_from_heads_t(dqf)], [_from_heads_t(dkf)], [_from_heads_t(dvf)]]
                          + [[tok_or_res(g[j]) for g in dil_grads] for j in range(3)],
                          BF16, "dqkv_assemble", tail=dfg_pad)
    dw_cat = _mm(h1, dcat, "tn", F32, "grad_w_qkv", tm=512, tn=3200)
    n_qkv = 3 * (FOX_WIDTH + DIL_WIDTH)
    grads["w_in"] = jnp.concatenate([dw_cat[:, :3 * FOX_WIDTH], dw_cat[:, n_qkv:n_qkv + N_FOX_HEADS],
                                     dw_cat[:, 3 * FOX_WIDTH:n_qkv]], axis=1)
    w_cat = jnp.concatenate([w_qkv, w_fg_pad], axis=1)
    dh1 = _mm(dcat, w_cat, "nt", F32, "bwd_proj_in", tk=3200)
    grad_x, _, dg = _rms_bwd(xs, g_mix_pre, dh1, dx1, "bwd_norm_mix_pre", want=("f32",))
    small["g_mix_pre"] = dg[0:1]
    small["b_f"] = db_f[:, 0].reshape(1, N_FOX_HEADS)
    small["rel_bias"] = d_rel

    split = {n: _split_weight(grads[n], n) for n in big}
    parts = jnp.stack([_pack({n: split[n][j].astype(BF16) for n in big}) for j in range(N_CHIPS)])
    R = parts.shape[1]
    half = R // 2
    keep = lax.dynamic_slice_in_dim(parts, my_c * half, half, axis=1)
    give = lax.dynamic_slice_in_dim(parts, (1 - my_c) * half, half, axis=1)
    got = _sibling_exchange(give, "grads_to_sibling")
    chip_sum = _add_n([keep.reshape(-1, PACK_COLS), got.reshape(-1, PACK_COLS)], "grads_add_sibling")
    chip_sum = chip_sum.reshape(N_CHIPS, half, PACK_COLS)
    my_chip = 2 * my_x + my_y
    from_chips = _chip_scatter(chip_sum.astype(BF16), "grads_to_chips")
    own = lax.dynamic_index_in_dim(chip_sum, my_chip, axis=0, keepdims=False)
    g_half = _add_n([own, from_chips[0], from_chips[1], from_chips[2]], "grads_add_chips")
    other_half = _sibling_exchange(g_half, "grads_share_sibling")
    g_pack = jnp.where(my_c == 0, jnp.concatenate([g_half, other_half]), jnp.concatenate([other_half, g_half]))

    small_pack = _pack_small(small)
    small_pack = small_pack.at[8, 0].set(loss_tile[0, 0])
    everyone = _all_to_all_small(small_pack, "small_all_gather")
    small_sum = _add_n([everyone[i] for i in range(8)], "small_sum")
    loss = small_sum[8, 0]
    g_small = _unpack_small(small_sum)

    outs = {"grad": _unpack(g_pack, shard_shapes), "delta": {}, "new_m": {}, "new_v": {}}
    for n in big:
        outs["delta"][n], outs["new_m"][n], outs["new_v"][n] = _adamw(
            args[n][0], outs["grad"][n], args["m_" + n][0], args["v_" + n][0], f"adamw_{n}")
    sw = _pack_small({n: args[n] for n in _SMALL + ("b_f", "rel_bias")})
    sm = _pack_small({n: args["m_" + n] for n in _SMALL + ("b_f", "rel_bias")})
    sv = _pack_small({n: args["v_" + n] for n in _SMALL + ("b_f", "rel_bias")})
    sd, snm, snv = _adamw(sw, small_sum.at[8, 0].set(0.0), sm, sv, "adamw_small")
    souts = {"grad": g_small, "delta": _unpack_small(sd), "new_m": _unpack_small(snm), "new_v": _unpack_small(snv)}

    def leaf(kind, n):
        if n in souts[kind]:
            return souts[kind][n].reshape(args[n].shape)
        return outs[kind][n].reshape(args[n].shape)

    result = [loss, grad_x.reshape(x.shape)]
    for kind in ("grad", "delta", "new_m", "new_v"):
        result += [leaf(kind, n) for n in names]
    return tuple(result)
```

```python
import numpy as np
import jax
import jax.numpy as jnp
from jax import lax
from jax.experimental import pallas as pl
from jax.experimental.pallas import tpu as pltpu

F32 = jnp.float32
BF16 = jnp.bfloat16
MESH_IDS = pl.DeviceIdType.MESH

LANES = 128
HEAD_DIM = 64
N_FOX_HEADS = 8
N_DIL_HEADS = 8
N_MEM_HEADS = 4
FOX_WIDTH = N_FOX_HEADS * HEAD_DIM
DIL_WIDTH = N_DIL_HEADS * HEAD_DIM
DILATIONS = (1, 4, 16)
BAND = 128
BAND_CHUNK_MAX = 32 * BAND
N_BUCKETS = 32
MAX_DISTANCE = 2048
QK_SCALE = HEAD_DIM ** -0.5
RMS_EPS = 1e-6
NEG = -1e30
VMEM_LIMIT = 56 << 20

ADAM_LR = 0.001
ADAM_B1 = 0.9
ADAM_B2 = 0.999
ADAM_EPS = 1e-08
ADAM_WD = 0.01
ADAM_STEP = 10

N_CHIPS = 4
PACK_COLS = 1024


def _params(*sem):
    return pltpu.CompilerParams(dimension_semantics=sem, vmem_limit_bytes=VMEM_LIMIT)


def _fit(n, cap):
    if n <= cap:
        return n
    t = (cap // LANES) * LANES
    while t >= LANES:
        if n % t == 0:
            return t
        t -= LANES
    raise ValueError(f"no lane-aligned tile for {n} under {cap}")


def _dot(a, b, dims):
    return lax.dot_general(a, b, (dims, ((), ())), preferred_element_type=F32)


_NN = ((1,), (0,))
_NT = ((1,), (1,))
_TN = ((0,), (0,))


def _mm(a, b, mode, out_dtype, name, tm=1024, tn=1024, tk=1024):
    if mode == "nn":
        (M, K), N = a.shape, b.shape[1]
    elif mode == "nt":
        (M, K), N = a.shape, b.shape[0]
    else:
        (K, M), N = a.shape, b.shape[1]
    tm, tn, tk = _fit(M, tm), _fit(N, tn), _fit(K, tk)
    nk = K // tk
    if mode == "tn":
        a_spec = pl.BlockSpec((tk, tm), lambda i, j, k: (k, i))
    else:
        a_spec = pl.BlockSpec((tm, tk), lambda i, j, k: (i, k))
    if mode == "nt":
        b_spec = pl.BlockSpec((tn, tk), lambda i, j, k: (j, k))
    else:
        b_spec = pl.BlockSpec((tk, tn), lambda i, j, k: (k, j))
    dims = {"nn": _NN, "nt": _NT, "tn": _TN}[mode]

    def body(a_ref, b_ref, o_ref, *acc):
        prod = _dot(a_ref[...].astype(BF16), b_ref[...].astype(BF16), dims)
        if nk == 1:
            o_ref[...] = prod.astype(o_ref.dtype)
            return
        acc_ref, k = acc[0], pl.program_id(2)

        @pl.when(k == 0)
        def _():
            acc_ref[...] = prod

        @pl.when(k > 0)
        def _():
            acc_ref[...] += prod

        @pl.when(k == nk - 1)
        def _():
            o_ref[...] = acc_ref[...].astype(o_ref.dtype)

    return pl.pallas_call(
        body, name=name, grid=(M // tm, N // tn, nk),
        in_specs=[a_spec, b_spec],
        out_specs=pl.BlockSpec((tm, tn), lambda i, j, k: (i, j)),
        out_shape=jax.ShapeDtypeStruct((M, N), out_dtype),
        scratch_shapes=[pltpu.VMEM((tm, tn), F32)] if nk > 1 else [],
        compiler_params=_params("parallel", "parallel", "arbitrary"),
    )(a, b)


def _rms_rows(x):
    return lax.rsqrt(jnp.mean(x * x, axis=-1, keepdims=True) + RMS_EPS)


def _rms_fwd(x, g, name, tr=512):
    S, D = x.shape
    tr = _fit(S, tr)

    def body(x_ref, g_ref, h_ref):
        xv = x_ref[...]
        h_ref[...] = (xv * _rms_rows(xv) * g_ref[...]).astype(BF16)

    return pl.pallas_call(
        body, name=name, grid=(S // tr,),
        in_specs=[pl.BlockSpec((tr, D), lambda i: (i, 0)), pl.BlockSpec((1, D), lambda i: (0, 0))],
        out_specs=pl.BlockSpec((tr, D), lambda i: (i, 0)),
        out_shape=jax.ShapeDtypeStruct((S, D), BF16),
        compiler_params=_params("parallel"),
    )(x, g)


def _resid_norm(xres, y, g_post, g_next, name, tr=1024):
    S, D = xres.shape
    tr = _fit(S, tr)

    def body(x_ref, y_ref, gp_ref, gn_ref, xn_ref, h_ref):
        yv = y_ref[...]
        xn = x_ref[...] + yv * _rms_rows(yv) * gp_ref[...]
        xn_ref[...] = xn
        h_ref[...] = (xn * _rms_rows(xn) * gn_ref[...]).astype(BF16)

    row = pl.BlockSpec((tr, D), lambda i: (i, 0))
    vec = pl.BlockSpec((1, D), lambda i: (0, 0))
    return pl.pallas_call(
        body, name=name, grid=(S // tr,),
        in_specs=[row, row, vec, vec], out_specs=[row, row],
        out_shape=[jax.ShapeDtypeStruct((S, D), F32), jax.ShapeDtypeStruct((S, D), BF16)],
        compiler_params=_params("parallel"),
    )(xres, y, g_post, g_next)


def _final_loss(xres, y, g_post, target, name, tr=1024):
    S, D = xres.shape
    tr = _fit(S, tr)

    def body(x_ref, y_ref, gp_ref, t_ref, d_ref, loss_ref):
        i = pl.program_id(0)
        yv = y_ref[...]
        err = x_ref[...] + yv * _rms_rows(yv) * gp_ref[...] - t_ref[...]
        d_ref[...] = err * (1.0 / D)

        @pl.when(i == 0)
        def _():
            loss_ref[...] = jnp.zeros_like(loss_ref)

        part = jnp.sum(jnp.sum(err * err, axis=1, keepdims=True), axis=0, keepdims=True)
        loss_ref[...] += jnp.broadcast_to(part * (0.5 / D), loss_ref.shape)

    row = pl.BlockSpec((tr, D), lambda i: (i, 0))
    vec = pl.BlockSpec((1, D), lambda i: (0, 0))
    return pl.pallas_call(
        body, name=name, grid=(S // tr,),
        in_specs=[row, row, vec, row],
        out_specs=[row, pl.BlockSpec((8, LANES), lambda i: (0, 0))],
        out_shape=[jax.ShapeDtypeStruct((S, D), F32), jax.ShapeDtypeStruct((8, LANES), F32)],
        compiler_params=_params("arbitrary"),
    )(xres, y, g_post, target)


def _rms_bwd(xin, g, dy, dres, name, want=("f32", "bf16"), tr=1024):
    S, D = xin.shape
    tr = _fit(S, tr)
    has_res = dres is not None

    def body(*refs):
        refs = list(refs)
        dg_ref = refs.pop()
        dxb_ref = refs.pop() if "bf16" in want else None
        dx_ref = refs.pop() if "f32" in want else None
        dr_ref = refs.pop() if has_res else None
        x_ref, g_ref, dy_ref = refs
        i = pl.program_id(0)
        xv = x_ref[...]
        dyv = dy_ref[...].astype(F32)
        xhat = xv * _rms_rows(xv)
        dxhat = dyv * g_ref[...]
        r = _rms_rows(xv)
        dx = r * (dxhat - xhat * jnp.mean(dxhat * xhat, axis=-1, keepdims=True))
        if has_res:
            dx = dx + dr_ref[...]
        if dx_ref is not None:
            dx_ref[...] = dx
        if dxb_ref is not None:
            dxb_ref[...] = dx.astype(BF16)

        @pl.when(i == 0)
        def _():
            dg_ref[...] = jnp.zeros_like(dg_ref)

        dg_ref[...] += jnp.broadcast_to(jnp.sum(dyv * xhat, axis=0, keepdims=True), dg_ref.shape)

    row = pl.BlockSpec((tr, D), lambda i: (i, 0))
    vec = pl.BlockSpec((1, D), lambda i: (0, 0))
    acc = pl.BlockSpec((8, D), lambda i: (0, 0))
    ins = [xin, g, dy] + ([dres] if has_res else [])
    dtypes = [dt for key, dt in (("f32", F32), ("bf16", BF16)) if key in want]
    outs = pl.pallas_call(
        body, name=name, grid=(S // tr,),
        in_specs=[row, vec, row] + ([row] if has_res else []),
        out_specs=[row] * len(dtypes) + [acc],
        out_shape=[jax.ShapeDtypeStruct((S, D), dt) for dt in dtypes] + [jax.ShapeDtypeStruct((8, D), F32)],
        compiler_params=_params("arbitrary"),
    )(*ins)
    by_key = dict(zip([key for key in ("f32", "bf16") if key in want], outs[:-1]))
    return by_key.get("f32"), by_key.get("bf16"), outs[-1]


def _tri(n, upper):
    r = lax.broadcasted_iota(jnp.int32, (n, n), 0)
    c = lax.broadcasted_iota(jnp.int32, (n, n), 1)
    return jnp.where((r <= c) if upper else (r >= c), 1.0, 0.0).astype(F32)


def _forget_fwd(fg_t, b_col, name, ts=512):
    H, S = fg_t.shape
    ts = _fit(S, ts)

    def body(f_ref, b_ref, c_ref, carry_ref):
        i = pl.program_id(0)

        @pl.when(i == 0)
        def _():
            carry_ref[...] = jnp.zeros_like(carry_ref)

        z = f_ref[...] + b_ref[...]
        logf = jnp.minimum(z, 0.0) - jnp.log(1.0 + jnp.exp(-jnp.abs(z)))
        run = lax.dot_general(logf, _tri(ts, True), (_NN, ((), ())), precision=lax.Precision.HIGHEST,
                              preferred_element_type=F32) + carry_ref[:, 0:1]
        c_ref[...] = run
        carry_ref[...] = jnp.broadcast_to(
            carry_ref[:, 0:1] + jnp.sum(logf, axis=1, keepdims=True), carry_ref.shape)

    return pl.pallas_call(
        body, name=name, grid=(S // ts,),
        in_specs=[pl.BlockSpec((H, ts), lambda i: (0, i)), pl.BlockSpec((H, 1), lambda i: (0, 0))],
        out_specs=pl.BlockSpec((H, ts), lambda i: (0, i)),
        out_shape=jax.ShapeDtypeStruct((H, S), F32),
        scratch_shapes=[pltpu.VMEM((H, LANES), F32)],
        compiler_params=_params("arbitrary"),
    )(fg_t, b_col)


def _forget_bwd(fg_t, b_col, dc_plus, dc_minus, name, ts=512):
    H, S = fg_t.shape
    ts = _fit(S, ts)
    nb = S // ts

    def body(f_ref, b_ref, dcp_ref, dcm_ref, df_ref, db_ref, carry_ref):
        i = pl.program_id(0)

        @pl.when(i == 0)
        def _():
            carry_ref[...] = jnp.zeros_like(carry_ref)
            db_ref[...] = jnp.zeros_like(db_ref)

        dc = dcp_ref[...] - dcm_ref[...]
        suffix = lax.dot_general(dc, _tri(ts, False), (_NN, ((), ())), precision=lax.Precision.HIGHEST,
                                 preferred_element_type=F32) + carry_ref[:, 0:1]
        z = f_ref[...] + b_ref[...]
        sig_neg = 1.0 / (1.0 + jnp.exp(z))
        df = suffix * sig_neg
        df_ref[...] = df
        carry_ref[...] = jnp.broadcast_to(
            carry_ref[:, 0:1] + jnp.sum(dc, axis=1, keepdims=True), carry_ref.shape)
        db_ref[...] += jnp.broadcast_to(jnp.sum(df, axis=1, keepdims=True), db_ref.shape)

    rev = pl.BlockSpec((H, ts), lambda i: (0, nb - 1 - i))
    return pl.pallas_call(
        body, name=name, grid=(nb,),
        in_specs=[rev, pl.BlockSpec((H, 1), lambda i: (0, 0)), rev, rev],
        out_specs=[rev, pl.BlockSpec((H, LANES), lambda i: (0, 0))],
        out_shape=[jax.ShapeDtypeStruct((H, S), F32), jax.ShapeDtypeStruct((H, LANES), F32)],
        scratch_shapes=[pltpu.VMEM((H, LANES), F32)],
        compiler_params=_params("arbitrary"),
    )(fg_t, b_col, dc_plus, dc_minus)


ONES_ROWS = 16
EXTRA = 3


def _split3(x):
    hi = lax.reduce_precision(x, 8, 7)
    mid = lax.reduce_precision(x - hi, 8, 7)
    lo = lax.reduce_precision(x - hi - mid, 8, 7)
    return hi.astype(BF16), mid.astype(BF16), lo.astype(BF16)


def _lanes_operand(t, extras):
    block = jnp.pad(jnp.stack(extras, axis=-1), ((0, 0), (0, 0), (0, LANES - HEAD_DIM - len(extras))))
    return jnp.concatenate([t, block], axis=-1)


def _rows_operand(t, extras):
    block = jnp.pad(jnp.stack(extras, axis=1), ((0, 0), (0, LANES - HEAD_DIM - len(extras)), (0, 0)))
    return jnp.concatenate([t, block], axis=1)


def _with_ones(t):
    return jnp.concatenate([t, jnp.ones((t.shape[0], ONES_ROWS, t.shape[2]), t.dtype)], axis=1)


def _fox_fwd(qa, ka, vt, name, tq=512, tk=1024):
    H, _, S = qa.shape
    Dh = HEAD_DIM
    tk = _fit(S, tk)
    tq = _fit(tk, tq)
    ratio = tk // tq

    def body(qa_ref, ka_ref, vt_ref, o_ref, lse_ref, m_ref, acc_ref, sa_ref, sb_ref, ta_ref, tb_ref):
        i = pl.program_id(1)
        qv = qa_ref[...] * QK_SCALE
        m_ref[...] = jnp.full_like(m_ref, NEG)
        acc_ref[...] = jnp.zeros_like(acc_ref)
        n = i // ratio
        q_off = (i - n * ratio) * tq

        def scores(j, s_ref, t_ref, diagonal):
            off = pl.multiple_of(j * tk, LANES)
            s = _dot(ka_ref[pl.ds(off, tk), :], qv, _NN)
            if diagonal:
                key = lax.broadcasted_iota(jnp.int32, (tk, tq), 0)
                qry = lax.broadcasted_iota(jnp.int32, (tk, tq), 1) + q_off
                s = jnp.where(key <= qry, s, NEG)
            s_ref[...] = s
            t_ref[...] = jnp.max(s, axis=0, keepdims=True)

        def absorb(j, s_ref, t_ref):
            off = pl.multiple_of(j * tk, LANES)
            m_old = m_ref[...]
            m_new = jnp.maximum(m_old, t_ref[...])
            p = jnp.exp(s_ref[...] - m_new)
            alpha = jnp.exp(m_old - m_new)
            acc_ref[...] = alpha * acc_ref[...] + _dot(vt_ref[:, pl.ds(off, tk)], p.astype(BF16), _NN)
            m_ref[...] = m_new

        scores(n, sa_ref, ta_ref, True)

        def loop_body(jj, carry):
            scores(2 * jj, sb_ref, tb_ref, False)
            absorb(jnp.where(jj == 0, n, 2 * jj - 1), sa_ref, ta_ref)
            scores(2 * jj + 1, sa_ref, ta_ref, False)
            absorb(2 * jj, sb_ref, tb_ref)
            return carry

        pairs = n // 2
        lax.fori_loop(0, pairs, loop_body, 0)
        held = jnp.where(pairs == 0, n, 2 * pairs - 1)

        @pl.when(n % 2 == 1)
        def _():
            scores(n - 1, sb_ref, tb_ref, False)
            absorb(held, sa_ref, ta_ref)
            absorb(n - 1, sb_ref, tb_ref)

        @pl.when(n % 2 == 0)
        def _():
            absorb(held, sa_ref, ta_ref)

        l = acc_ref[Dh:Dh + 1, :]
        o_ref[...] = acc_ref[0:Dh, :] / l
        lse_ref[...] = m_ref[...] + jnp.log(l)

    return pl.pallas_call(
        body, name=name, grid=(H, S // tq),
        in_specs=[pl.BlockSpec((None, LANES, tq), lambda h, i: (h, 0, i)),
                  pl.BlockSpec((None, S, LANES), lambda h, i: (h, 0, 0)),
                  pl.BlockSpec((None, Dh + ONES_ROWS, S), lambda h, i: (h, 0, 0))],
        out_specs=[pl.BlockSpec((None, Dh, tq), lambda h, i: (h, 0, i)),
                   pl.BlockSpec((None, 1, tq), lambda h, i: (h, 0, i))],
        out_shape=[jax.ShapeDtypeStruct((H, Dh, S), F32), jax.ShapeDtypeStruct((H, 1, S), F32)],
        scratch_shapes=[pltpu.VMEM((1, tq), F32), pltpu.VMEM((Dh + ONES_ROWS, tq), F32),
                        pltpu.VMEM((tk, tq), F32), pltpu.VMEM((tk, tq), F32),
                        pltpu.VMEM((1, tq), F32), pltpu.VMEM((1, tq), F32)],
        compiler_params=_params("parallel", "arbitrary"),
    )(qa, ka, vt)


def _fox_bwd(qa, ka, kta, va, doa, name, tq=1024, tk=512):
    H, _, S = qa.shape
    Dh, Da = HEAD_DIM, HEAD_DIM + ONES_ROWS
    tq = _fit(S, tq)
    tk = _fit(tq, tk)
    ratio = tq // tk
    nq = S // tq
    nk = S // tk

    def body(ka_ref, kta_ref, va_ref, qa_ref, doa_ref, dqt_ref, dkt_ref, dvt_ref, dka_ref, dva_ref):
        j = pl.program_id(1)

        @pl.when(j == 0)
        def _():
            dqt_ref[...] = jnp.zeros_like(dqt_ref)

        kv = ka_ref[...]
        ktv = kta_ref[0:Da, :]
        vv = va_ref[...]
        dka_ref[...] = jnp.zeros_like(dka_ref)
        dva_ref[...] = jnp.zeros_like(dva_ref)
        i_diag = j // ratio
        k_off = (j - i_diag * ratio) * tk

        def step(i, diagonal):
            off = pl.multiple_of(i * tq, LANES)
            qv = qa_ref[:, pl.ds(off, tq)] * QK_SCALE
            dov = doa_ref[:, pl.ds(off, tq)]
            e = _dot(kv, qv, _NN)
            if diagonal:
                key = lax.broadcasted_iota(jnp.int32, (tk, tq), 0) + k_off
                qry = lax.broadcasted_iota(jnp.int32, (tk, tq), 1)
                e = jnp.where(key <= qry, e, NEG)
            p_t = jnp.exp(e)
            dva_ref[...] += _dot(dov[0:Dh, :], p_t.astype(BF16), _NT)
            ds_b = (p_t * _dot(vv, dov, _NN)).astype(BF16)
            dka_ref[...] += _dot(qv[0:Da, :], ds_b, _NT)
            dqt_ref[:, pl.ds(off, tq)] += _dot(ktv, ds_b, _NN)

        step(i_diag, True)

        def loop_body(i, carry):
            step(i, False)
            return carry

        lax.fori_loop(i_diag + 1, nq, loop_body, 0)
        dkt_ref[...] = dka_ref[...]
        dvt_ref[...] = dva_ref[...]

        @pl.when(j == nk - 1)
        def _():
            dqt_ref[0:Dh, :] = dqt_ref[0:Dh, :] * QK_SCALE

    lanes_tile = pl.BlockSpec((None, tk, LANES), lambda h, j: (h, j, 0))
    rows_tile = pl.BlockSpec((None, LANES, tk), lambda h, j: (h, 0, j))
    rows_full = pl.BlockSpec((None, LANES, S), lambda h, j: (h, 0, 0))
    return pl.pallas_call(
        body, name=name, grid=(H, nk),
        in_specs=[lanes_tile, rows_tile, lanes_tile, rows_full, rows_full],
        out_specs=[pl.BlockSpec((None, Da, S), lambda h, j: (h, 0, 0)),
                   pl.BlockSpec((None, Da, tk), lambda h, j: (h, 0, j)),
                   pl.BlockSpec((None, Dh, tk), lambda h, j: (h, 0, j))],
        out_shape=[jax.ShapeDtypeStruct((H, Da, S), F32), jax.ShapeDtypeStruct((H, Da, S), F32),
                   jax.ShapeDtypeStruct((H, Dh, S), F32)],
        scratch_shapes=[pltpu.VMEM((Da, tk), F32), pltpu.VMEM((Dh, tk), F32)],
        compiler_params=_params("parallel", "arbitrary"),
    )(ka, kta, va, qa, doa)


DIL_Q_BLOCK = 3 * FOX_WIDTH // LANES
HEAD_PAIRS = N_DIL_HEADS // 2
PAIR_BLOCKS = DIL_WIDTH // LANES


def _band_geometry(S, d):
    L = S // d
    chunk = min(BAND_CHUNK_MAX, L)
    assert L % chunk == 0 and chunk % BAND == 0
    return L, chunk, chunk // BAND, L // chunk


def _band_in_specs(S, d, base):
    L, chunk, nb, _ = _band_geometry(S, d)

    def col(kind):
        return lambda hp, r, i: (r, i, base + kind * PAIR_BLOCKS + hp)

    def col_prev(kind):
        return lambda hp, r, i: (r, jnp.maximum(i * nb - 1, 0), base + kind * PAIR_BLOCKS + hp)

    main = [pl.BlockSpec((None, chunk, LANES), col(kind)) for kind in range(3)]
    prev = [pl.BlockSpec((None, BAND, LANES), col_prev(kind)) for kind in range(3)]
    bias = pl.BlockSpec((None, 2 * BAND, 2 * BAND), lambda hp, r, i: (hp, 0, 0))
    stat = pl.BlockSpec((None, 2, chunk), lambda hp, r, i: (hp, 0, r * (L // chunk) + i))
    tok = pl.BlockSpec((None, chunk, LANES), lambda hp, r, i: (r, i, hp))
    return main, prev, bias, stat, tok


def _to_residues(x, col_block, width, d, name, tr=1024):
    S = x.shape[0]
    tr = _fit(S, tr)

    def body(x_ref, o_ref, tmp_ref):
        for j in range(width // LANES):
            cols = slice(j * LANES, (j + 1) * LANES)
            tmp_ref[j] = x_ref[:, cols].astype(F32)
            for r in range(d):
                o_ref[r, :, cols] = tmp_ref[j, pl.ds(r, tr // d, stride=d), :].astype(o_ref.dtype)

    return pl.pallas_call(
        body, name=name, grid=(S // tr,),
        in_specs=[pl.BlockSpec((tr, width), lambda i: (i, col_block))],
        out_specs=pl.BlockSpec((d, tr // d, width), lambda i: (0, i, 0)),
        out_shape=jax.ShapeDtypeStruct((d, S // d, width), x.dtype),
        scratch_shapes=[pltpu.VMEM((width // LANES, tr, LANES), F32)],
        compiler_params=_params("parallel"),
    )(x)


def _token_rows(ref, cols, tmp_ref):
    if len(ref.shape) == 2:
        return ref[:, cols].astype(F32)
    d, rows = ref.shape[0], ref.shape[1]
    for r in range(d):
        tmp_ref[pl.ds(r, rows, stride=d), :] = ref[r, :, cols].astype(F32)
    return tmp_ref[...]


def _row_spec(t, tr):
    if t.ndim == 2:
        return pl.BlockSpec((tr, t.shape[1]), lambda i: (i, 0))
    d = t.shape[0]
    return pl.BlockSpec((d, tr // d, t.shape[2]), lambda i: (0, i, 0))


def _head_lanes(a):
    return lax.broadcasted_iota(jnp.int32, (1, LANES), 1) // HEAD_DIM == a


def _one_head(x, a):
    return jnp.where(_head_lanes(a), x, jnp.zeros_like(x))


def _head_stack(x):
    return jnp.concatenate([_one_head(x, 0), _one_head(x, 1)], axis=0)


def _pair_rows(ref, rows):
    return jnp.concatenate([ref[0:1, rows], ref[1:2, rows]], axis=1)


def _band_scores_t(kb, q_stack, bias_t, first):
    s = _dot(kb, q_stack, _NT) + bias_t
    if first is not None:
        key = lax.broadcasted_iota(jnp.int32, s.shape, 0)
        s = jnp.where(jnp.logical_and(first, key < BAND), NEG, s)
    return s


def _pair_select(stacked):
    return jnp.where(_head_lanes(0), stacked[0:BAND, :], stacked[BAND:, :])


def _dil_lse(qkv_v, base, bias_t, name):
    d, L = qkv_v.shape[:2]
    S = L * d
    _, chunk, nb, nchunks = _band_geometry(S, d)
    main, prev, bias, stat, _ = _band_in_specs(S, d, base)

    def body(q_ref, k_ref, kp_ref, b_ref, lse_ref, kext_ref):
        first = pl.program_id(2) == 0
        kext_ref[0:BAND, :] = kp_ref[...]
        kext_ref[BAND:, :] = k_ref[...]
        for b in range(nb):
            rows, ext = slice(b * BAND, (b + 1) * BAND), slice(b * BAND, (b + 2) * BAND)
            s = _band_scores_t(kext_ref[ext, :], _head_stack(q_ref[rows, :] * QK_SCALE), b_ref[...],
                               first if b == 0 else None)
            m = jnp.max(s, axis=0, keepdims=True)
            lse = m + jnp.log(jnp.sum(jnp.exp(s - m), axis=0, keepdims=True))
            lse_ref[0:1, rows] = lse[:, 0:BAND]
            lse_ref[1:2, rows] = lse[:, BAND:]

    return pl.pallas_call(
        body, name=name, grid=(HEAD_PAIRS, d, nchunks),
        in_specs=[main[0], main[1], prev[1], bias], out_specs=stat,
        out_shape=jax.ShapeDtypeStruct((HEAD_PAIRS, 2, S), F32),
        scratch_shapes=[pltpu.VMEM((chunk + BAND, LANES), BF16)],
        compiler_params=_params("parallel", "parallel", "parallel"),
    )(qkv_v, qkv_v, qkv_v, bias_t)


def _dil_out(qkv_v, base, bias_t, lse_joint, name):
    d, L = qkv_v.shape[:2]
    S = L * d
    _, chunk, nb, nchunks = _band_geometry(S, d)
    main, prev, bias, stat, tok = _band_in_specs(S, d, base)

    def body(q_ref, k_ref, kp_ref, v_ref, vp_ref, b_ref, lse_ref, o_ref, kext_ref, vext_ref):
        first = pl.program_id(2) == 0
        kext_ref[0:BAND, :] = kp_ref[...]
        kext_ref[BAND:, :] = k_ref[...]
        vext_ref[0:BAND, :] = vp_ref[...]
        vext_ref[BAND:, :] = v_ref[...]
        for b in range(nb):
            rows, ext = slice(b * BAND, (b + 1) * BAND), slice(b * BAND, (b + 2) * BAND)
            s = _band_scores_t(kext_ref[ext, :], _head_stack(q_ref[rows, :] * QK_SCALE), b_ref[...],
                               first if b == 0 else None)
            p_t = jnp.exp(s - _pair_rows(lse_ref, rows))
            o_ref[rows, :] = _pair_select(_dot(p_t.astype(BF16), vext_ref[ext, :], _TN)).astype(BF16)

    return pl.pallas_call(
        body, name=name, grid=(HEAD_PAIRS, d, nchunks),
        in_specs=[main[0], main[1], prev[1], main[2], prev[2], bias, stat], out_specs=tok,
        out_shape=jax.ShapeDtypeStruct((d, L, DIL_WIDTH), BF16),
        scratch_shapes=[pltpu.VMEM((chunk + BAND, LANES), BF16), pltpu.VMEM((chunk + BAND, LANES), BF16)],
        compiler_params=_params("parallel", "parallel", "parallel"),
    )(qkv_v, qkv_v, qkv_v, qkv_v, qkv_v, bias_t, lse_joint)


def _dil_bwd(qkv_v, base, do_v, bias_t, lse_joint, delta, name):
    d, L = qkv_v.shape[:2]
    S = L * d
    _, chunk, nb, nchunks = _band_geometry(S, d)
    main, prev, bias, stat, tok = _band_in_specs(S, d, base)
    nblocks = L // BAND

    def nxt_row(i):
        return jnp.minimum((i + 1) * nb, nblocks - 1)

    q_next = pl.BlockSpec((None, BAND, LANES), lambda hp, r, i: (r, nxt_row(i), base + hp))
    do_next = pl.BlockSpec((None, BAND, LANES), lambda hp, r, i: (r, nxt_row(i), hp))
    stat_next = pl.BlockSpec((None, 2, BAND), lambda hp, r, i: (hp, 0, r * nblocks + nxt_row(i)))

    def body(q_ref, k_ref, kp_ref, v_ref, vp_ref, do_ref, b_ref, lse_ref, dl_ref,
             qn_ref, don_ref, lsen_ref, dln_ref,
             dq_ref, dk_ref, dv_ref, db_ref, kext_ref, vext_ref, dkext_ref, dvext_ref):
        r, i = pl.program_id(1), pl.program_id(2)
        first = i == 0
        has_next = i + 1 < nchunks
        tail = slice(BAND + chunk, 2 * BAND + chunk)
        kext_ref[0:BAND, :] = kp_ref[...]
        kext_ref[BAND:BAND + chunk, :] = k_ref[...]
        kext_ref[tail, :] = jnp.zeros((BAND, LANES), BF16)
        vext_ref[0:BAND, :] = vp_ref[...]
        vext_ref[BAND:BAND + chunk, :] = v_ref[...]
        vext_ref[tail, :] = jnp.zeros((BAND, LANES), BF16)
        dkext_ref[...] = jnp.zeros_like(dkext_ref)
        dvext_ref[...] = jnp.zeros_like(dvext_ref)

        @pl.when(jnp.logical_and(r == 0, i == 0))
        def _():
            db_ref[...] = jnp.zeros_like(db_ref)

        def block(q2, do2, lse_row, dl_row, ext, mask_rows):
            q_stack, do_stack = _head_stack(q2), _head_stack(do2)
            s = _dot(kext_ref[ext, :], q_stack, _NT) + b_ref[...]
            if mask_rows is not None:
                s = jnp.where(mask_rows, NEG, s)
            p_t = jnp.exp(s - lse_row)
            ds_t = p_t * (_dot(vext_ref[ext, :], do_stack, _NT) - dl_row)
            ds_b = ds_t.astype(BF16)
            dkext_ref[ext, :] += _dot(ds_b, q_stack, _NN)
            dvext_ref[ext, :] += _dot(p_t.astype(BF16), do_stack, _NN)
            return ds_t, ds_b

        key = lax.broadcasted_iota(jnp.int32, (2 * BAND, 2 * BAND), 0)
        all_lanes = slice(0, BAND)
        for b in range(nb):
            rows, ext = slice(b * BAND, (b + 1) * BAND), slice(b * BAND, (b + 2) * BAND)
            mask = jnp.logical_and(first, key < BAND) if b == 0 else None
            ds_t, ds_b = block(q_ref[rows, :] * QK_SCALE, do_ref[rows, :], _pair_rows(lse_ref, rows),
                               _pair_rows(dl_ref, rows), ext, mask)
            dq_ref[rows, :] = (_pair_select(_dot(ds_b, kext_ref[ext, :], _TN)) * QK_SCALE).astype(BF16)
            db_ref[...] += ds_t
        block(qn_ref[...] * QK_SCALE, don_ref[...], _pair_rows(lsen_ref, all_lanes), _pair_rows(dln_ref, all_lanes),
              slice(chunk, chunk + 2 * BAND), jnp.logical_or(jnp.logical_not(has_next), key >= BAND))
        dk_ref[...] = dkext_ref[BAND:BAND + chunk, :].astype(BF16)
        dv_ref[...] = dvext_ref[BAND:BAND + chunk, :].astype(BF16)

    ext_rows = chunk + 2 * BAND
    return pl.pallas_call(
        body, name=name, grid=(HEAD_PAIRS, d, nchunks),
        in_specs=[main[0], main[1], prev[1], main[2], prev[2], tok, bias, stat, stat,
                  q_next, do_next, stat_next, stat_next],
        out_specs=[tok, tok, tok, bias],
        out_shape=[jax.ShapeDtypeStruct((d, L, DIL_WIDTH), BF16)] * 3
                  + [jax.ShapeDtypeStruct((HEAD_PAIRS, 2 * BAND, 2 * BAND), F32)],
        scratch_shapes=[pltpu.VMEM((ext_rows, LANES), BF16), pltpu.VMEM((ext_rows, LANES), BF16),
                        pltpu.VMEM((ext_rows, LANES), F32), pltpu.VMEM((ext_rows, LANES), F32)],
        compiler_params=_params("arbitrary", "arbitrary", "arbitrary"),
    )(qkv_v, qkv_v, qkv_v, qkv_v, qkv_v, do_v, bias_t, lse_joint, delta, qkv_v, do_v, lse_joint, delta)


def _lse_join(lse3, name):
    P, H, S = lse3.shape

    def body(l_ref, o_ref):
        a, b, c = l_ref[0], l_ref[1], l_ref[2]
        m = jnp.maximum(jnp.maximum(a, b), c)
        o_ref[...] = m + jnp.log(jnp.exp(a - m) + jnp.exp(b - m) + jnp.exp(c - m))

    return pl.pallas_call(body, name=name, out_shape=jax.ShapeDtypeStruct((H, S), F32))(lse3)


def _bucket_reduce(dbias_t, bucket_map_t, name):
    P, H = dbias_t.shape[:2]

    def body(db_ref, bk_ref, o_ref):
        p, h = pl.program_id(0), pl.program_id(1)

        @pl.when(jnp.logical_and(p == 0, h == 0))
        def _():
            o_ref[...] = jnp.zeros_like(o_ref)

        db, bk = db_ref[...], bk_ref[...]
        row = lax.broadcasted_iota(jnp.int32, (N_BUCKETS, LANES), 0)
        lane = lax.broadcasted_iota(jnp.int32, (N_BUCKETS, LANES), 1)

        def one(b, acc):
            val = jnp.sum(jnp.sum(jnp.where(bk == b, db, 0.0), axis=1, keepdims=True), axis=0, keepdims=True)
            return acc + jnp.where(jnp.logical_and(row == b, lane == h), val, 0.0)

        acc = jnp.zeros((N_BUCKETS, LANES), F32)
        for b in range(N_BUCKETS):
            acc = one(b, acc)
        o_ref[...] += acc

    return pl.pallas_call(
        body, name=name, grid=(P, H),
        in_specs=[pl.BlockSpec((None, None, 2 * BAND, BAND), lambda p, h: (p, h, 0, 0)),
                  pl.BlockSpec((None, 2 * BAND, BAND), lambda p, h: (p, 0, 0))],
        out_specs=pl.BlockSpec((N_BUCKETS, LANES), lambda p, h: (0, 0)),
        out_shape=jax.ShapeDtypeStruct((N_BUCKETS, LANES), F32),
        compiler_params=_params("arbitrary", "arbitrary"),
    )(dbias_t, bucket_map_t)


def _mem_fwd(q, kv, name, tq=2048):
    S, W = q.shape
    N = kv.shape[0]
    pairs = W // LANES
    tq = _fit(S, tq)

    def body(q_ref, k_ref, v_ref, o_ref, lse_ref):
        for a in range(2):
            lanes = slice(a * HEAD_DIM, (a + 1) * HEAD_DIM)
            s = _dot(k_ref[:, lanes], q_ref[:, lanes] * QK_SCALE, _NT)
            m = jnp.max(s, axis=0, keepdims=True)
            e = jnp.exp(s - m)
            l = jnp.sum(e, axis=0, keepdims=True)
            o_ref[:, lanes] = _dot((e / l).astype(BF16), v_ref[:, lanes], _TN).astype(BF16)
            lse_ref[a:a + 1, :] = m + jnp.log(l)

    return pl.pallas_call(
        body, name=name, grid=(pairs, S // tq),
        in_specs=[pl.BlockSpec((tq, LANES), lambda hp, i: (i, hp)),
                  pl.BlockSpec((N, LANES), lambda hp, i: (0, hp)),
                  pl.BlockSpec((N, LANES), lambda hp, i: (0, pairs + hp))],
        out_specs=[pl.BlockSpec((tq, LANES), lambda hp, i: (i, hp)),
                   pl.BlockSpec((None, 2, tq), lambda hp, i: (hp, 0, i))],
        out_shape=[jax.ShapeDtypeStruct((S, W), BF16), jax.ShapeDtypeStruct((pairs, 2, S), F32)],
        compiler_params=_params("parallel", "parallel"),
    )(q, kv, kv)


def _mem_bwd(q, kv, do, lse, delta, name, tq=2048):
    S, W = q.shape
    N = kv.shape[0]
    pairs = W // LANES
    tq = _fit(S, tq)

    def body(q_ref, k_ref, v_ref, do_ref, lse_ref, dl_ref, dq_ref, dk_ref, dv_ref):
        i = pl.program_id(1)

        @pl.when(i == 0)
        def _():
            dk_ref[...] = jnp.zeros_like(dk_ref)
            dv_ref[...] = jnp.zeros_like(dv_ref)

        for a in range(2):
            lanes = slice(a * HEAD_DIM, (a + 1) * HEAD_DIM)
            qv, dov = q_ref[:, lanes] * QK_SCALE, do_ref[:, lanes]
            kv_, vv = k_ref[:, lanes], v_ref[:, lanes]
            p_t = jnp.exp(_dot(kv_, qv, _NT) - lse_ref[a:a + 1, :])
            ds_t = p_t * (_dot(vv, dov, _NT) - dl_ref[a:a + 1, :])
            ds_b = ds_t.astype(BF16)
            dq_ref[:, lanes] = (_dot(ds_b, kv_, _TN) * QK_SCALE).astype(BF16)
            dk_ref[:, lanes] += _dot(ds_b, qv, _NN)
            dv_ref[:, lanes] += _dot(p_t.astype(BF16), dov, _NN)

    qs = pl.BlockSpec((tq, LANES), lambda hp, i: (i, hp))
    stat = pl.BlockSpec((None, 2, tq), lambda hp, i: (hp, 0, i))
    acc = pl.BlockSpec((N, LANES), lambda hp, i: (0, hp))
    return pl.pallas_call(
        body, name=name, grid=(pairs, S // tq),
        in_specs=[qs, acc, pl.BlockSpec((N, LANES), lambda hp, i: (0, pairs + hp)), qs, stat, stat],
        out_specs=[qs, acc, acc],
        out_shape=[jax.ShapeDtypeStruct((S, W), BF16), jax.ShapeDtypeStruct((N, W), F32),
                   jax.ShapeDtypeStruct((N, W), F32)],
        compiler_params=_params("parallel", "arbitrary"),
    )(q, kv, kv, do, lse, delta)


def _head_rowdot(a, bs, name, tr=1024):
    S, W = a.shape
    tr = _fit(S, tr)

    def body(*refs):
        a_ref, b_refs, o_ref, tmp_ref = refs[0], refs[1:-2], refs[-2], refs[-1]
        col = lax.broadcasted_iota(jnp.int32, (LANES, LANES), 0)
        lane = lax.broadcasted_iota(jnp.int32, (LANES, LANES), 1)
        acc = jnp.zeros((tr, LANES), F32)
        for j in range(W // LANES):
            cols = slice(j * LANES, (j + 1) * LANES)
            tot = _token_rows(b_refs[0], cols, tmp_ref)
            for r in b_refs[1:]:
                tot = tot + _token_rows(r, cols, tmp_ref)
            sel = jnp.where(col // HEAD_DIM + j * (LANES // HEAD_DIM) == lane, 1.0, 0.0).astype(F32)
            acc = acc + lax.dot_general(a_ref[:, cols].astype(F32) * tot, sel, (_NN, ((), ())),
                                        precision=lax.Precision.HIGHEST, preferred_element_type=F32)
        o_ref[...] = acc

    return pl.pallas_call(
        body, name=name, grid=(S // tr,), in_specs=[_row_spec(t, tr) for t in [a] + list(bs)],
        out_specs=pl.BlockSpec((tr, LANES), lambda i: (i, 0)),
        out_shape=jax.ShapeDtypeStruct((S, LANES), F32),
        scratch_shapes=[pltpu.VMEM((tr, LANES), F32)],
        compiler_params=_params("parallel"),
    )(a, *bs)


def _sum_cast_cols(groups, out_dtype, name, tail=None, tr=512):
    first = groups[0][0]
    S, W = (first.shape if first.ndim == 2 else (first.shape[0] * first.shape[1], first.shape[2]))
    tr = _fit(S, tr)
    flat = [t for g in groups for t in g] + ([tail] if tail is not None else [])
    tail_w = 0 if tail is None else tail.shape[1]

    def body(*refs):
        o_ref, tmp_ref = refs[-2], refs[-1]
        if tail is not None:
            o_ref[:, W * len(groups):] = refs[-3][...].astype(out_dtype)
        k = 0
        for gi, g in enumerate(groups):
            for j in range(W // LANES):
                cols = slice(j * LANES, (j + 1) * LANES)
                acc = _token_rows(refs[k], cols, tmp_ref)
                for r in refs[k + 1:k + len(g)]:
                    acc = acc + _token_rows(r, cols, tmp_ref)
                o_ref[:, gi * W + j * LANES:gi * W + (j + 1) * LANES] = acc.astype(out_dtype)
            k += len(g)

    return pl.pallas_call(
        body, name=name, grid=(S // tr,), in_specs=[_row_spec(t, tr) for t in flat],
        out_specs=pl.BlockSpec((tr, W * len(groups) + tail_w), lambda i: (i, 0)),
        out_shape=jax.ShapeDtypeStruct((S, W * len(groups) + tail_w), out_dtype),
        scratch_shapes=[pltpu.VMEM((tr, LANES), F32)],
        compiler_params=_params("parallel"),
    )(*flat)


FF_TILE = 256


def _ffn_up(h, w_gu, name, tm=4096):
    S, D = h.shape
    F2 = w_gu.shape[1]
    tm = _fit(S, tm)

    def body(h_ref, w_ref, gu_ref, act_ref):
        gu = _dot(h_ref[...], w_ref[...], _NN)
        gu_ref[...] = gu.astype(BF16)
        g, u = gu[:, :FF_TILE], gu[:, FF_TILE:]
        act_ref[...] = (g * (1.0 / (1.0 + jnp.exp(-g))) * u).astype(BF16)

    return pl.pallas_call(
        body, name=name, grid=(S // tm, F2 // (2 * FF_TILE)),
        in_specs=[pl.BlockSpec((tm, D), lambda i, j: (i, 0)), pl.BlockSpec((D, 2 * FF_TILE), lambda i, j: (0, j))],
        out_specs=[pl.BlockSpec((tm, 2 * FF_TILE), lambda i, j: (i, j)),
                   pl.BlockSpec((tm, FF_TILE), lambda i, j: (i, j))],
        out_shape=[jax.ShapeDtypeStruct((S, F2), BF16), jax.ShapeDtypeStruct((S, F2 // 2), BF16)],
        compiler_params=_params("parallel", "arbitrary"),
    )(h, w_gu)


def _ffn_dact(dy, w_down, gu, name, tm=4096):
    S, D = dy.shape
    F2 = gu.shape[1]
    tm = _fit(S, tm)

    def body(dy_ref, w_ref, gu_ref, dgu_ref):
        dact = _dot(dy_ref[...], w_ref[...], _NT)
        gu_v = gu_ref[...].astype(F32)
        g, u = gu_v[:, :FF_TILE], gu_v[:, FF_TILE:]
        sig = 1.0 / (1.0 + jnp.exp(-g))
        silu = g * sig
        dgu_ref[:, :FF_TILE] = (dact * u * (sig + silu * (1.0 - sig))).astype(BF16)
        dgu_ref[:, FF_TILE:] = (dact * silu).astype(BF16)

    return pl.pallas_call(
        body, name=name, grid=(S // tm, F2 // (2 * FF_TILE)),
        in_specs=[pl.BlockSpec((tm, D), lambda i, j: (i, 0)), pl.BlockSpec((FF_TILE, D), lambda i, j: (j, 0)),
                  pl.BlockSpec((tm, 2 * FF_TILE), lambda i, j: (i, j))],
        out_specs=pl.BlockSpec((tm, 2 * FF_TILE), lambda i, j: (i, j)),
        out_shape=jax.ShapeDtypeStruct((S, F2), BF16),
        compiler_params=_params("parallel", "arbitrary"),
    )(dy, w_down, gu)


def _fit_rows(n, cap):
    if n <= cap:
        return n
    t = (cap // 8) * 8
    while t >= 8:
        if n % t == 0:
            return t
        t -= 8
    raise ValueError(f"no sublane-aligned tile for {n} under {cap}")


def _add_n(arrs, name, tr=512):
    R, C = arrs[0].shape
    tr = _fit_rows(R, tr)

    def body(*refs):
        acc = refs[0][...].astype(F32)
        for r in refs[1:-1]:
            acc = acc + r[...].astype(F32)
        refs[-1][...] = acc

    row = pl.BlockSpec((tr, C), lambda i: (i, 0))
    return pl.pallas_call(
        body, name=name, grid=(R // tr,), in_specs=[row] * len(arrs), out_specs=row,
        out_shape=jax.ShapeDtypeStruct((R, C), F32), compiler_params=_params("parallel"),
    )(*arrs)


def _adamw(w, g, m, v, name, tr=512):
    R, C = w.shape
    tr = _fit_rows(R, tr)
    c1 = 1.0 / (1.0 - ADAM_B1 ** ADAM_STEP)
    c2 = 1.0 / (1.0 - ADAM_B2 ** ADAM_STEP)

    def body(w_ref, g_ref, m_ref, v_ref, d_ref, nm_ref, nv_ref):
        gv = g_ref[...]
        nm = ADAM_B1 * m_ref[...] + (1.0 - ADAM_B1) * gv
        nv = ADAM_B2 * v_ref[...] + (1.0 - ADAM_B2) * (gv * gv)
        nm_ref[...] = nm
        nv_ref[...] = nv
        d_ref[...] = -ADAM_LR * ((nm * c1) / (jnp.sqrt(nv * c2) + ADAM_EPS) + ADAM_WD * w_ref[...])

    row = pl.BlockSpec((tr, C), lambda i: (i, 0))
    return pl.pallas_call(
        body, name=name, grid=(R // tr,), in_specs=[row] * 4, out_specs=[row] * 3,
        out_shape=[jax.ShapeDtypeStruct((R, C), F32)] * 3, compiler_params=_params("parallel"),
    )(w, g, m, v)


def _place():
    return lax.axis_index("x"), lax.axis_index("y"), lax.axis_index("c")


_ANY = pl.BlockSpec(memory_space=pl.ANY)


def _chip_all_gather(shard, name):
    R, C = shard.shape
    half = R // 2

    def body(x_ref, out_ref, send_sems, recv_sems, local_sem):
        x, y, c = _place()
        chips = [(1 - x, y), (x, 1 - y), (1 - x, 1 - y)]
        sibling = (x, y, 1 - c)
        mine = pltpu.make_async_copy(x_ref, out_ref.at[2 * x + y], local_sem)
        mine.start()

        def rows(chip, core):
            return out_ref.at[chip, pl.ds(core * half, half)]

        def copy(k, chip, core, to, src=None):
            return pltpu.make_async_remote_copy(
                src_ref=rows(chip, core) if src is None else src, dst_ref=rows(chip, core),
                send_sem=send_sems.at[k], recv_sem=recv_sems.at[k], device_id=to, device_id_type=MESH_IDS)

        me = 2 * x + y
        first = [copy(k, me, c, (cx, cy, c), src=x_ref.at[pl.ds(c * half, half)]) for k, (cx, cy) in enumerate(chips)]
        for cp in first:
            cp.start()
        passed = [copy(3 + k, 2 * cx + cy, c, sibling) for k, (cx, cy) in enumerate(chips)]
        for k, (cx, cy) in enumerate(chips):
            copy(k, 2 * cx + cy, c, (cx, cy, c)).wait_recv()
            passed[k].start()
        for k, (cx, cy) in enumerate(chips):
            copy(3 + k, 2 * cx + cy, 1 - c, sibling).wait_recv()
        for cp in first + passed:
            cp.wait_send()
        mine.wait()

    return pl.pallas_call(
        body, name=name, in_specs=[_ANY], out_specs=_ANY,
        out_shape=jax.ShapeDtypeStruct((N_CHIPS, R, C), shard.dtype),
        scratch_shapes=[pltpu.SemaphoreType.DMA((6,)), pltpu.SemaphoreType.DMA((6,)), pltpu.SemaphoreType.DMA],
    )(shard)


def _sibling_exchange(buf, name):
    def body(x_ref, out_ref, send_sem, recv_sem):
        x, y, c = _place()
        cp = pltpu.make_async_remote_copy(
            src_ref=x_ref, dst_ref=out_ref, send_sem=send_sem, recv_sem=recv_sem,
            device_id=(x, y, 1 - c), device_id_type=MESH_IDS)
        cp.start()
        cp.wait()

    return pl.pallas_call(
        body, name=name, in_specs=[_ANY], out_specs=_ANY,
        out_shape=jax.ShapeDtypeStruct(buf.shape, buf.dtype),
        scratch_shapes=[pltpu.SemaphoreType.DMA, pltpu.SemaphoreType.DMA],
    )(buf)


def _chip_scatter(parts, name):
    _, R, C = parts.shape

    def body(p_ref, out_ref, send_sems, recv_sems):
        x, y, c = _place()
        chips = [(1 - x, y), (x, 1 - y), (1 - x, 1 - y)]

        def copy(k, slab, to):
            return pltpu.make_async_remote_copy(
                src_ref=p_ref.at[slab], dst_ref=out_ref.at[k], send_sem=send_sems.at[k], recv_sem=recv_sems.at[k],
                device_id=to, device_id_type=MESH_IDS)

        sends = [copy(k, 2 * cx + cy, (cx, cy, c)) for k, (cx, cy) in enumerate(chips)]
        for cp in sends:
            cp.start()
        for cp in sends:
            cp.wait_recv()
        for cp in sends:
            cp.wait_send()

    return pl.pallas_call(
        body, name=name, in_specs=[_ANY], out_specs=_ANY,
        out_shape=jax.ShapeDtypeStruct((3, R, C), parts.dtype),
        scratch_shapes=[pltpu.SemaphoreType.DMA((3,)), pltpu.SemaphoreType.DMA((3,))],
    )(parts)


def _all_to_all_small(vec, name):
    R, C = vec.shape

    def body(v_ref, out_ref, send_sems, recv_sems, local_sem):
        x, y, c = _place()
        me = 4 * x + 2 * y + c
        mine = pltpu.make_async_copy(v_ref, out_ref.at[me], local_sem)
        mine.start()
        flips = [(dx, dy, dc) for dx in (0, 1) for dy in (0, 1) for dc in (0, 1)][1:]

        def peer(f):
            return (x ^ f[0], y ^ f[1], c ^ f[2])

        def copy(k, slot, to):
            return pltpu.make_async_remote_copy(
                src_ref=v_ref, dst_ref=out_ref.at[slot], send_sem=send_sems.at[k], recv_sem=recv_sems.at[k],
                device_id=to, device_id_type=MESH_IDS)

        sends = [copy(k, me, peer(f)) for k, f in enumerate(flips)]
        for cp in sends:
            cp.start()
        for k, f in enumerate(flips):
            px, py, pc = peer(f)
            copy(k, 4 * px + 2 * py + pc, peer(f)).wait_recv()
        for cp in sends:
            cp.wait_send()
        mine.wait()

    return pl.pallas_call(
        body, name=name, in_specs=[_ANY], out_specs=_ANY,
        out_shape=jax.ShapeDtypeStruct((8, R, C), vec.dtype),
        scratch_shapes=[pltpu.SemaphoreType.DMA((7,)), pltpu.SemaphoreType.DMA((7,)), pltpu.SemaphoreType.DMA],
    )(vec)


def _to_heads(t, n):
    S = t.shape[0]
    return t.reshape(S, n, HEAD_DIM).transpose(1, 0, 2)


def _to_heads_t(t, n):
    S = t.shape[0]
    return t.T.reshape(n, HEAD_DIM, S)


def _from_heads_t(t):
    H, Dh, S = t.shape
    return t.reshape(H * Dh, S).T


def _t5_bucket(dist):
    max_exact = N_BUCKETS // 2
    d = np.maximum(dist, 1).astype(np.float32)
    large = max_exact + (np.log(d / max_exact) / np.log(MAX_DISTANCE / max_exact)
                         * (N_BUCKETS - max_exact)).astype(np.int32)
    large = np.minimum(large, N_BUCKETS - 1)
    return np.where(dist < max_exact, dist, large).astype(np.int32)


def _band_tables():
    qi = np.arange(BAND)[:, None]
    kj = np.arange(2 * BAND)[None, :]
    sub = qi + BAND - kj
    band = (sub >= 0) & (sub <= BAND)
    out = []
    for d in DILATIONS:
        bucket = _t5_bucket(np.clip(sub, 0, BAND) * d)
        out.append(np.where(band, bucket, -1).astype(np.int32))
    return np.stack(out)


_PACK = (("w_in", 770), ("w_out", 256), ("w_xq", 64), ("w_xk", 64), ("w_xv", 64), ("w_xo", 64),
         ("w_gate", 704), ("w_up", 704), ("w_down", 704))


def _pack(shards):
    rows = [shards[n].reshape(-1, PACK_COLS) for n, _ in _PACK]
    total = sum(r.shape[0] for r in rows)
    pad = (-total) % 128
    if pad:
        rows.append(jnp.zeros((pad, PACK_COLS), rows[0].dtype))
    return jnp.concatenate(rows, axis=0)


def _unpack(pack, shapes):
    out, r = {}, 0
    for n, _ in _PACK:
        cnt = int(np.prod(shapes[n])) // PACK_COLS
        out[n] = pack[r:r + cnt].reshape(shapes[n])
        r += cnt
    return out


_COL_SHARDED = ("w_in", "w_xo", "w_gate", "w_up")


def _full_weight(gathered, name):
    return jnp.concatenate(gathered, axis=1 if name in _COL_SHARDED else 0)


def _split_weight(full, name):
    return jnp.split(full, N_CHIPS, axis=1 if name in _COL_SHARDED else 0)


_SMALL = ("g_mix_pre", "g_mix_post", "g_xattn_pre", "g_mem", "g_xattn_post", "g_ffn_pre", "g_ffn_post")


def _pack_small(vals):
    D = vals["g_mix_pre"].shape[1]
    rows = [vals[n].reshape(1, D) for n in _SMALL]
    misc = jnp.concatenate([vals["b_f"].reshape(-1), vals["rel_bias"].reshape(-1)])
    rows.append(jnp.pad(misc, (0, D - misc.shape[0])).reshape(1, D))
    rows.append(jnp.zeros((16 - len(rows), D), F32))
    return jnp.concatenate(rows, axis=0)


def _unpack_small(pack):
    out = {n: pack[i:i + 1] for i, n in enumerate(_SMALL)}
    out["b_f"] = pack[7, 0:N_FOX_HEADS].reshape(1, N_FOX_HEADS)
    out["rel_bias"] = pack[7, N_FOX_HEADS:N_FOX_HEADS + N_BUCKETS * N_DIL_HEADS].reshape(N_BUCKETS, N_DIL_HEADS)
    return out


def kernel(x, mem, g_mix_pre, w_in, b_f, rel_bias, w_out, g_mix_post, g_xattn_pre, g_mem, w_xq, w_xk, w_xv, w_xo, g_xattn_post, g_ffn_pre, w_gate, w_up, w_down, g_ffn_post, loss_target, m_g_mix_pre, m_w_in, m_b_f, m_rel_bias, m_w_out, m_g_mix_post, m_g_xattn_pre, m_g_mem, m_w_xq, m_w_xk, m_w_xv, m_w_xo, m_g_xattn_post, m_g_ffn_pre, m_w_gate, m_w_up, m_w_down, m_g_ffn_post, v_g_mix_pre, v_w_in, v_b_f, v_rel_bias, v_w_out, v_g_mix_post, v_g_xattn_pre, v_g_mem, v_w_xq, v_w_xk, v_w_xv, v_w_xo, v_g_xattn_post, v_g_ffn_pre, v_w_gate, v_w_up, v_w_down, v_g_ffn_post):
    args = dict(locals())
    big = [n for n, _ in _PACK]
    names = ["g_mix_pre", "w_in", "b_f", "rel_bias", "w_out", "g_mix_post", "g_xattn_pre", "g_mem", "w_xq",
             "w_xk", "w_xv", "w_xo", "g_xattn_post", "g_ffn_pre", "w_gate", "w_up", "w_down", "g_ffn_post"]
    xs = x[0]
    S, D = xs.shape
    assert S % (BAND * DILATIONS[-1]) == 0
    shard_shapes = {n: args[n].shape[1:] for n in big}
    my_x, my_y, my_c = lax.axis_index("x"), lax.axis_index("y"), lax.axis_index("c")

    gathered = _chip_all_gather(_pack({n: args[n][0].astype(BF16) for n in big}), "weights_all_gather")
    per_chip = [_unpack(gathered[j], shard_shapes) for j in range(N_CHIPS)]
    W = {n: _full_weight([pc[n] for pc in per_chip], n) for n in big}
    w_fox, w_fg, w_dil = (W["w_in"][:, :3 * FOX_WIDTH], W["w_in"][:, 3 * FOX_WIDTH:3 * FOX_WIDTH + N_FOX_HEADS],
                          W["w_in"][:, 3 * FOX_WIDTH + N_FOX_HEADS:])
    w_qkv = jnp.concatenate([w_fox, w_dil], axis=1)
    w_fg_pad = jnp.pad(w_fg, ((0, 0), (0, LANES - N_FOX_HEADS)))
    F = W["w_gate"].shape[1]
    nft = F // FF_TILE
    w_gu = jnp.stack([W["w_gate"].reshape(D, nft, FF_TILE), W["w_up"].reshape(D, nft, FF_TILE)],
                     axis=2).reshape(D, 2 * F)

    h1 = _rms_fwd(xs, g_mix_pre, "rms_mix_pre")
    qkv = _mm(h1, w_qkv, "nn", BF16, "proj_qkv", tm=2048)
    fg = _mm(h1, w_fg_pad, "nn", F32, "proj_gate")
    fg_t = fg[:, :N_FOX_HEADS].T
    b_col = b_f.reshape(N_FOX_HEADS, 1)
    c_t = _forget_fwd(fg_t, b_col, "forget_cumsum")
    fq_s, fk_s, fv_s = (qkv[:, i * FOX_WIDTH:(i + 1) * FOX_WIDTH] for i in range(3))
    fqt, fvt = _to_heads_t(fq_s, N_FOX_HEADS), _to_heads_t(fv_s, N_FOX_HEADS)
    unit = jnp.full((N_FOX_HEADS, S), 1.0, BF16)
    inv_scale = jnp.full((N_FOX_HEADS, S), 1.0 / QK_SCALE, BF16)
    ka = _lanes_operand(_to_heads(fk_s, N_FOX_HEADS), list(_split3(-c_t)) + [unit] * EXTRA)
    qa_f = _rows_operand(fqt, [inv_scale] * EXTRA)
    o_fox_t, lse_fox = _fox_fwd(qa_f, ka, _with_ones(fvt), "fox_fwd")

    bucket_map = _band_tables()
    onehot = (jnp.asarray(bucket_map)[..., None] == jnp.arange(N_BUCKETS)).astype(F32)
    bias_tab = jnp.einsum("pqkb,bh->phkq", onehot, rel_bias, precision=lax.Precision.HIGHEST)
    bias_tab = jnp.where(jnp.asarray(bucket_map.transpose(0, 2, 1) >= 0)[:, None], bias_tab, NEG)
    bias_t = bias_tab.reshape(3, HEAD_PAIRS, 2, 2 * BAND, BAND).transpose(0, 1, 3, 2, 4).reshape(
        3, HEAD_PAIRS, 2 * BAND, 2 * BAND)
    views = [(qkv.reshape(1, S, qkv.shape[1]), DIL_Q_BLOCK)] + [
        (_to_residues(qkv, 1, 3 * DIL_WIDTH, d, f"dilated_qkv_residues_{d}"), 0) for d in DILATIONS[1:]]

    def to_tok(stat, d):
        return stat.reshape(N_DIL_HEADS, d, S // d).swapaxes(1, 2).reshape(N_DIL_HEADS, S)

    def to_perm(stat, d):
        return stat.reshape(N_DIL_HEADS, S // d, d).swapaxes(1, 2).reshape(HEAD_PAIRS, 2, S)

    def tok_or_res(t):
        return t.reshape(t.shape[1:]) if t.shape[0] == 1 else t

    lse_tok = jnp.stack([to_tok(_dil_lse(*views[p], bias_t[p], f"dilated_lse_{d}"), d)
                         for p, d in enumerate(DILATIONS)])
    lse_joint = _lse_join(lse_tok, "dilated_lse_join")
    lse_perm = [to_perm(lse_joint, d) for d in DILATIONS]
    o_dil = [tok_or_res(_dil_out(*views[p], bias_t[p], lse_perm[p], f"dilated_out_{d}"))
             for p, d in enumerate(DILATIONS)]
    o_cat = _sum_cast_cols([[_from_heads_t(o_fox_t)]] + [[o] for o in o_dil], BF16, "mixer_out_cat")
    w_out_b = W["w_out"]
    w_out_cat = jnp.concatenate([w_out_b[:FOX_WIDTH]] + [w_out_b[FOX_WIDTH:]] * 3, axis=0)
    a = _mm(o_cat, w_out_cat, "nn", F32, "proj_out", tm=2048, tk=2048)
    x1, h2 = _resid_norm(xs, a, g_mix_post, g_xattn_pre, "resid_mix")

    hm = _rms_fwd(mem[0], g_mem, "rms_mem")
    q2 = _mm(h2, W["w_xq"], "nn", BF16, "xattn_q")
    w_xkv = jnp.concatenate([W["w_xk"], W["w_xv"]], axis=1)
    kvm = _mm(hm, w_xkv, "nn", BF16, "xattn_kv")
    MW = N_MEM_HEADS * HEAD_DIM
    oc, lse_mem = _mem_fwd(q2, kvm, "xattn_fwd")
    y2 = _mm(oc, W["w_xo"], "nn", F32, "xattn_o")
    x2, h3 = _resid_norm(x1, y2, g_xattn_post, g_ffn_pre, "resid_xattn")

    gu, act = _ffn_up(h3, w_gu, "ffn_up")
    y3 = _mm(act, W["w_down"], "nn", F32, "ffn_down", tk=2816)
    dx3, loss_tile = _final_loss(x2, y3, g_ffn_post, loss_target[0], "final_loss")

    grads = {}
    small = {}
    _, dy3_b, dg = _rms_bwd(y3, g_ffn_post, dx3, None, "bwd_norm_ffn_post", want=("bf16",))
    small["g_ffn_post"] = dg[0:1]
    grads["w_down"] = _mm(act, dy3_b, "tn", F32, "grad_w_down", tm=1408)
    dgu = _ffn_dact(dy3_b, W["w_down"], gu, "ffn_dact")
    dw_gu = _mm(h3, dgu, "tn", F32, "grad_w_gu", tn=1408).reshape(D, nft, 2, FF_TILE)
    grads["w_gate"], grads["w_up"] = dw_gu[:, :, 0].reshape(D, F), dw_gu[:, :, 1].reshape(D, F)
    dh3 = _mm(dgu, w_gu, "nt", F32, "bwd_ffn_in", tm=2048, tk=1408)
    dx2, _, dg = _rms_bwd(x2, g_ffn_pre, dh3, dx3, "bwd_norm_ffn_pre", want=("f32",))
    small["g_ffn_pre"] = dg[0:1]

    _, dy2_b, dg = _rms_bwd(y2, g_xattn_post, dx2, None, "bwd_norm_xattn_post", want=("bf16",))
    small["g_xattn_post"] = dg[0:1]
    grads["w_xo"] = _mm(oc, dy2_b, "tn", F32, "grad_w_xo")
    doc = _mm(dy2_b, W["w_xo"], "nt", BF16, "bwd_xattn_o")
    delta_mem = _head_rowdot(doc, [oc], "xattn_delta")[:, :N_MEM_HEADS].T.reshape(N_MEM_HEADS // 2, 2, S)
    dq2, dkm, dvm = _mem_bwd(q2, kvm, doc, lse_mem, delta_mem, "xattn_bwd")
    dkvm = jnp.concatenate([dkm, dvm], axis=1).astype(BF16)
    grads["w_xq"] = _mm(h2, dq2, "tn", F32, "grad_w_xq")
    dw_xkv = _mm(hm, dkvm, "tn", F32, "grad_w_xkv")
    grads["w_xk"], grads["w_xv"] = dw_xkv[:, :MW], dw_xkv[:, MW:]
    dhm = _mm(dkvm, w_xkv, "nt", F32, "bwd_xattn_kv")
    _, _, dg = _rms_bwd(mem[0], g_mem, dhm, None, "bwd_norm_mem", want=())
    small["g_mem"] = dg[0:1]
    dh2 = _mm(dq2, W["w_xq"], "nt", F32, "bwd_xattn_q")
    dx1, _, dg = _rms_bwd(x1, g_xattn_pre, dh2, dx2, "bwd_norm_xattn_pre", want=("f32",))
    small["g_xattn_pre"] = dg[0:1]

    _, da_b, dg = _rms_bwd(a, g_mix_post, dx1, None, "bwd_norm_mix_post", want=("bf16",))
    small["g_mix_post"] = dg[0:1]
    dw_out_cat = _mm(o_cat, da_b, "tn", F32, "grad_w_out")
    dw_out_dil = _add_n([dw_out_cat[FOX_WIDTH + p * DIL_WIDTH:FOX_WIDTH + (p + 1) * DIL_WIDTH] for p in range(3)],
                        "grad_w_out_dil")
    grads["w_out"] = jnp.concatenate([dw_out_cat[:FOX_WIDTH], dw_out_dil], axis=0)
    do = _mm(da_b, w_out_b, "nt", BF16, "bwd_proj_out")
    do_fox, do_dil = do[:, :FOX_WIDTH], do[:, FOX_WIDTH:]

    delta_fox = _head_rowdot(do_fox, [o_cat[:, :FOX_WIDTH]], "fox_delta")[:, :N_FOX_HEADS].T
    qa_b = lax.dynamic_update_slice(qa_f, jnp.stack(_split3(lse_fox[:, 0] * (-1.0 / QK_SCALE)), axis=1),
                                    (0, HEAD_DIM + EXTRA, 0))
    va = _lanes_operand(_to_heads(fv_s, N_FOX_HEADS), [unit] * EXTRA)
    doa = _rows_operand(_to_heads_t(do_fox, N_FOX_HEADS), list(_split3(-delta_fox)))
    dq_aug, dk_aug, dvf = _fox_bwd(qa_b, ka, ka.transpose(0, 2, 1), va, doa, "fox_bwd")
    dqf, dkf = dq_aug[:, :HEAD_DIM], dk_aug[:, :HEAD_DIM]
    dfg_t, db_f = _forget_bwd(fg_t, b_col, dq_aug[:, HEAD_DIM + EXTRA], dk_aug[:, HEAD_DIM], "forget_bwd")

    delta_dil = _head_rowdot(do_dil, o_dil, "dilated_delta")[:, :N_DIL_HEADS].T
    do_res = [do_dil.reshape(1, S, DIL_WIDTH)] + [
        _to_residues(do, 1, DIL_WIDTH, d, f"dilated_do_residues_{d}") for d in DILATIONS[1:]]
    dil_grads = [_dil_bwd(*views[p], do_res[p], bias_t[p], lse_perm[p], to_perm(delta_dil, d), f"dilated_bwd_{d}")
                 for p, d in enumerate(DILATIONS)]
    dbias_t = jnp.stack([g[3].reshape(HEAD_PAIRS, 2 * BAND, 2, BAND).transpose(0, 2, 1, 3).reshape(
        N_DIL_HEADS, 2 * BAND, BAND) for g in dil_grads])
    d_rel = _bucket_reduce(dbias_t, jnp.asarray(bucket_map.transpose(0, 2, 1)), "rel_bias_grad")[:, :N_DIL_HEADS]
    dfg_pad = jnp.pad(dfg_t.T, ((0, 0), (0, LANES - N_FOX_HEADS))).astype(BF16)
    dcat = _sum_cast_cols([[_from_heads_t(dqf)], [_from_heads_t(dkf)], [_from_heads_t(dvf)]]
                          + [[tok_or_res(g[j]) for g in dil_grads] for j in range(3)],
                          BF16, "dqkv_assemble", tail=dfg_pad)
    dw_cat = _mm(h1, dcat, "tn", F32, "grad_w_qkv", tm=512, tn=3200)
    n_qkv = 3 * (FOX_WIDTH + DIL_WIDTH)
    grads["w_in"] = jnp.concatenate([dw_cat[:, :3 * FOX_WIDTH], dw_cat[:, n_qkv:n_qkv + N_FOX_HEADS],
                                     dw_cat[:, 3 * FOX_WIDTH:n_qkv]], axis=1)
    w_cat = jnp.concatenate([w_qkv, w_fg_pad], axis=1)
    dh1 = _mm(dcat, w_cat, "nt", F32, "bwd_proj_in", tk=3200)
    grad_x, _, dg = _rms_bwd(xs, g_mix_pre, dh1, dx1, "bwd_norm_mix_pre", want=("f32",))
    small["g_mix_pre"] = dg[0:1]
    small["b_f"] = db_f[:, 0].reshape(1, N_FOX_HEADS)
    small["rel_bias"] = d_rel

    split = {n: _split_weight(grads[n], n) for n in big}
    parts = jnp.stack([_pack({n: split[n][j].astype(BF16) for n in big}) for j in range(N_CHIPS)])
    R = parts.shape[1]
    half = R // 2
    keep = lax.dynamic_slice_in_dim(parts, my_c * half, half, axis=1)
    give = lax.dynamic_slice_in_dim(parts, (1 - my_c) * half, half, axis=1)
    got = _sibling_exchange(give, "grads_to_sibling")
    chip_sum = _add_n([keep.reshape(-1, PACK_COLS), got.reshape(-1, PACK_COLS)], "grads_add_sibling")
    chip_sum = chip_sum.reshape(N_CHIPS, half, PACK_COLS)
    my_chip = 2 * my_x + my_y
    from_chips = _chip_scatter(chip_sum.astype(BF16), "grads_to_chips")
    own = lax.dynamic_index_in_dim(chip_sum, my_chip, axis=0, keepdims=False)
    g_half = _add_n([own, from_chips[0], from_chips[1], from_chips[2]], "grads_add_chips")
    other_half = _sibling_exchange(g_half, "grads_share_sibling")
    g_pack = jnp.where(my_c == 0, jnp.concatenate([g_half, other_half]), jnp.concatenate([other_half, g_half]))

    small_pack = _pack_small(small)
    small_pack = small_pack.at[8, 0].set(loss_tile[0, 0])
    everyone = _all_to_all_small(small_pack, "small_all_gather")
    small_sum = _add_n([everyone[i] for i in range(8)], "small_sum")
    loss = small_sum[8, 0]
    g_small = _unpack_small(small_sum)

    outs = {"grad": _unpack(g_pack, shard_shapes), "delta": {}, "new_m": {}, "new_v": {}}
    for n in big:
        outs["delta"][n], outs["new_m"][n], outs["new_v"][n] = _adamw(
            args[n][0], outs["grad"][n], args["m_" + n][0], args["v_" + n][0], f"adamw_{n}")
    sw = _pack_small({n: args[n] for n in _SMALL + ("b_f", "rel_bias")})
    sm = _pack_small({n: args["m_" + n] for n in _SMALL + ("b_f", "rel_bias")})
    sv = _pack_small({n: args["v_" + n] for n in _SMALL + ("b_f", "rel_bias")})
    sd, snm, snv = _adamw(sw, small_sum.at[8, 0].set(0.0), sm, sv, "adamw_small")
    souts = {"grad": g_small, "delta": _unpack_small(sd), "new_m": _unpack_small(snm), "new_v": _unpack_small(snv)}

    def leaf(kind, n):
        if n in souts[kind]:
            return souts[kind][n].reshape(args[n].shape)
        return outs[kind][n].reshape(args[n].shape)

    result = [loss, grad_x.reshape(x.shape)]
    for kind in ("grad", "delta", "new_m", "new_v"):
        result += [leaf(kind, n) for n in names]
    return tuple(result)
```

```python
import numpy as np
import jax
import jax.numpy as jnp
from jax import lax
from jax.experimental import pallas as pl
from jax.experimental.pallas import tpu as pltpu

F32 = jnp.float32
BF16 = jnp.bfloat16
MESH_IDS = pl.DeviceIdType.MESH

LANES = 128
HEAD_DIM = 64
N_FOX_HEADS = 8
N_DIL_HEADS = 8
N_MEM_HEADS = 4
FOX_WIDTH = N_FOX_HEADS * HEAD_DIM
DIL_WIDTH = N_DIL_HEADS * HEAD_DIM
DILATIONS = (1, 4, 16)
BAND = 128
BAND_CHUNK_MAX = 32 * BAND
N_BUCKETS = 32
MAX_DISTANCE = 2048
QK_SCALE = HEAD_DIM ** -0.5
RMS_EPS = 1e-6
NEG = -1e30
VMEM_LIMIT = 56 << 20

ADAM_LR = 0.001
ADAM_B1 = 0.9
ADAM_B2 = 0.999
ADAM_EPS = 1e-08
ADAM_WD = 0.01
ADAM_STEP = 10

N_CHIPS = 4
PACK_COLS = 1024


def _params(*sem):
    return pltpu.CompilerParams(dimension_semantics=sem, vmem_limit_bytes=VMEM_LIMIT)


def _fit(n, cap):
    if n <= cap:
        return n
    t = (cap // LANES) * LANES
    while t >= LANES:
        if n % t == 0:
            return t
        t -= LANES
    raise ValueError(f"no lane-aligned tile for {n} under {cap}")


def _dot(a, b, dims):
    return lax.dot_general(a, b, (dims, ((), ())), preferred_element_type=F32)


_NN = ((1,), (0,))
_NT = ((1,), (1,))
_TN = ((0,), (0,))


def _mm(a, b, mode, out_dtype, name, tm=1024, tn=1024, tk=1024):
    if mode == "nn":
        (M, K), N = a.shape, b.shape[1]
    elif mode == "nt":
        (M, K), N = a.shape, b.shape[0]
    else:
        (K, M), N = a.shape, b.shape[1]
    tm, tn, tk = _fit(M, tm), _fit(N, tn), _fit(K, tk)
    nk = K // tk
    if mode == "tn":
        a_spec = pl.BlockSpec((tk, tm), lambda i, j, k: (k, i))
    else:
        a_spec = pl.BlockSpec((tm, tk), lambda i, j, k: (i, k))
    if mode == "nt":
        b_spec = pl.BlockSpec((tn, tk), lambda i, j, k: (j, k))
    else:
        b_spec = pl.BlockSpec((tk, tn), lambda i, j, k: (k, j))
    dims = {"nn": _NN, "nt": _NT, "tn": _TN}[mode]

    def body(a_ref, b_ref, o_ref, *acc):
        prod = _dot(a_ref[...].astype(BF16), b_ref[...].astype(BF16), dims)
        if nk == 1:
            o_ref[...] = prod.astype(o_ref.dtype)
            return
        acc_ref, k = acc[0], pl.program_id(2)

        @pl.when(k == 0)
        def _():
            acc_ref[...] = prod

        @pl.when(k > 0)
        def _():
            acc_ref[...] += prod

        @pl.when(k == nk - 1)
        def _():
            o_ref[...] = acc_ref[...].astype(o_ref.dtype)

    return pl.pallas_call(
        body, name=name, grid=(M // tm, N // tn, nk),
        in_specs=[a_spec, b_spec],
        out_specs=pl.BlockSpec((tm, tn), lambda i, j, k: (i, j)),
        out_shape=jax.ShapeDtypeStruct((M, N), out_dtype),
        scratch_shapes=[pltpu.VMEM((tm, tn), F32)] if nk > 1 else [],
        compiler_params=_params("parallel", "parallel", "arbitrary"),
    )(a, b)


def _rms_rows(x):
    return lax.rsqrt(jnp.mean(x * x, axis=-1, keepdims=True) + RMS_EPS)


def _rms_fwd(x, g, name, tr=512):
    S, D = x.shape
    tr = _fit(S, tr)

    def body(x_ref, g_ref, h_ref):
        xv = x_ref[...]
        h_ref[...] = (xv * _rms_rows(xv) * g_ref[...]).astype(BF16)

    return pl.pallas_call(
        body, name=name, grid=(S // tr,),
        in_specs=[pl.BlockSpec((tr, D), lambda i: (i, 0)), pl.BlockSpec((1, D), lambda i: (0, 0))],
        out_specs=pl.BlockSpec((tr, D), lambda i: (i, 0)),
        out_shape=jax.ShapeDtypeStruct((S, D), BF16),
        compiler_params=_params("parallel"),
    )(x, g)


def _resid_norm(xres, y, g_post, g_next, name, tr=1024):
    S, D = xres.shape
    tr = _fit(S, tr)

    def body(x_ref, y_ref, gp_ref, gn_ref, xn_ref, h_ref):
        yv = y_ref[...]
        xn = x_ref[...] + yv * _rms_rows(yv) * gp_ref[...]
        xn_ref[...] = xn
        h_ref[...] = (xn * _rms_rows(xn) * gn_ref[...]).astype(BF16)

    row = pl.BlockSpec((tr, D), lambda i: (i, 0))
    vec = pl.BlockSpec((1, D), lambda i: (0, 0))
    return pl.pallas_call(
        body, name=name, grid=(S // tr,),
        in_specs=[row, row, vec, vec], out_specs=[row, row],
        out_shape=[jax.ShapeDtypeStruct((S, D), F32), jax.ShapeDtypeStruct((S, D), BF16)],
        compiler_params=_params("parallel"),
    )(xres, y, g_post, g_next)


def _final_loss(xres, y, g_post, target, name, tr=1024):
    S, D = xres.shape
    tr = _fit(S, tr)

    def body(x_ref, y_ref, gp_ref, t_ref, d_ref, loss_ref):
        i = pl.program_id(0)
        yv = y_ref[...]
        err = x_ref[...] + yv * _rms_rows(yv) * gp_ref[...] - t_ref[...]
        d_ref[...] = err * (1.0 / D)

        @pl.when(i == 0)
        def _():
            loss_ref[...] = jnp.zeros_like(loss_ref)

        part = jnp.sum(jnp.sum(err * err, axis=1, keepdims=True), axis=0, keepdims=True)
        loss_ref[...] += jnp.broadcast_to(part * (0.5 / D), loss_ref.shape)

    row = pl.BlockSpec((tr, D), lambda i: (i, 0))
    vec = pl.BlockSpec((1, D), lambda i: (0, 0))
    return pl.pallas_call(
        body, name=name, grid=(S // tr,),
        in_specs=[row, row, vec, row],
        out_specs=[row, pl.BlockSpec((8, LANES), lambda i: (0, 0))],
        out_shape=[jax.ShapeDtypeStruct((S, D), F32), jax.ShapeDtypeStruct((8, LANES), F32)],
        compiler_params=_params("arbitrary"),
    )(xres, y, g_post, target)


def _rms_bwd(xin, g, dy, dres, name, want=("f32", "bf16"), tr=1024):
    S, D = xin.shape
    tr = _fit(S, tr)
    has_res = dres is not None

    def body(*refs):
        refs = list(refs)
        dg_ref = refs.pop()
        dxb_ref = refs.pop() if "bf16" in want else None
        dx_ref = refs.pop() if "f32" in want else None
        dr_ref = refs.pop() if has_res else None
        x_ref, g_ref, dy_ref = refs
        i = pl.program_id(0)
        xv = x_ref[...]
        dyv = dy_ref[...].astype(F32)
        xhat = xv * _rms_rows(xv)
        dxhat = dyv * g_ref[...]
        r = _rms_rows(xv)
        dx = r * (dxhat - xhat * jnp.mean(dxhat * xhat, axis=-1, keepdims=True))
        if has_res:
            dx = dx + dr_ref[...]
        if dx_ref is not None:
            dx_ref[...] = dx
        if dxb_ref is not None:
            dxb_ref[...] = dx.astype(BF16)

        @pl.when(i == 0)
        def _():
            dg_ref[...] = jnp.zeros_like(dg_ref)

        dg_ref[...] += jnp.broadcast_to(jnp.sum(dyv * xhat, axis=0, keepdims=True), dg_ref.shape)

    row = pl.BlockSpec((tr, D), lambda i: (i, 0))
    vec = pl.BlockSpec((1, D), lambda i: (0, 0))
    acc = pl.BlockSpec((8, D), lambda i: (0, 0))
    ins = [xin, g, dy] + ([dres] if has_res else [])
    dtypes = [dt for key, dt in (("f32", F32), ("bf16", BF16)) if key in want]
    outs = pl.pallas_call(
        body, name=name, grid=(S // tr,),
        in_specs=[row, vec, row] + ([row] if has_res else []),
        out_specs=[row] * len(dtypes) + [acc],
        out_shape=[jax.ShapeDtypeStruct((S, D), dt) for dt in dtypes] + [jax.ShapeDtypeStruct((8, D), F32)],
        compiler_params=_params("arbitrary"),
    )(*ins)
    by_key = dict(zip([key for key in ("f32", "bf16") if key in want], outs[:-1]))
    return by_key.get("f32"), by_key.get("bf16"), outs[-1]


def _tri(n, upper):
    r = lax.broadcasted_iota(jnp.int32, (n, n), 0)
    c = lax.broadcasted_iota(jnp.int32, (n, n), 1)
    return jnp.where((r <= c) if upper else (r >= c), 1.0, 0.0).astype(F32)


def _forget_fwd(fg_t, b_col, name, ts=512):
    H, S = fg_t.shape
    ts = _fit(S, ts)

    def body(f_ref, b_ref, c_ref, carry_ref):
        i = pl.program_id(0)

        @pl.when(i == 0)
        def _():
            carry_ref[...] = jnp.zeros_like(carry_ref)

        z = f_ref[...] + b_ref[...]
        logf = jnp.minimum(z, 0.0) - jnp.log(1.0 + jnp.exp(-jnp.abs(z)))
        run = lax.dot_general(logf, _tri(ts, True), (_NN, ((), ())), precision=lax.Precision.HIGHEST,
                              preferred_element_type=F32) + carry_ref[:, 0:1]
        c_ref[...] = run
        carry_ref[...] = jnp.broadcast_to(
            carry_ref[:, 0:1] + jnp.sum(logf, axis=1, keepdims=True), carry_ref.shape)

    return pl.pallas_call(
        body, name=name, grid=(S // ts,),
        in_specs=[pl.BlockSpec((H, ts), lambda i: (0, i)), pl.BlockSpec((H, 1), lambda i: (0, 0))],
        out_specs=pl.BlockSpec((H, ts), lambda i: (0, i)),
        out_shape=jax.ShapeDtypeStruct((H, S), F32),
        scratch_shapes=[pltpu.VMEM((H, LANES), F32)],
        compiler_params=_params("arbitrary"),
    )(fg_t, b_col)


def _forget_bwd(fg_t, b_col, dc_plus, dc_minus, name, ts=512):
    H, S = fg_t.shape
    ts = _fit(S, ts)
    nb = S // ts

    def body(f_ref, b_ref, dcp_ref, dcm_ref, df_ref, db_ref, carry_ref):
        i = pl.program_id(0)

        @pl.when(i == 0)
        def _():
            carry_ref[...] = jnp.zeros_like(carry_ref)
            db_ref[...] = jnp.zeros_like(db_ref)

        dc = dcp_ref[...] - dcm_ref[...]
        suffix = lax.dot_general(dc, _tri(ts, False), (_NN, ((), ())), precision=lax.Precision.HIGHEST,
                                 preferred_element_type=F32) + carry_ref[:, 0:1]
        z = f_ref[...] + b_ref[...]
        sig_neg = 1.0 / (1.0 + jnp.exp(z))
        df = suffix * sig_neg
        df_ref[...] = df
        carry_ref[...] = jnp.broadcast_to(
            carry_ref[:, 0:1] + jnp.sum(dc, axis=1, keepdims=True), carry_ref.shape)
        db_ref[...] += jnp.broadcast_to(jnp.sum(df, axis=1, keepdims=True), db_ref.shape)

    rev = pl.BlockSpec((H, ts), lambda i: (0, nb - 1 - i))
    return pl.pallas_call(
        body, name=name, grid=(nb,),
        in_specs=[rev, pl.BlockSpec((H, 1), lambda i: (0, 0)), rev, rev],
        out_specs=[rev, pl.BlockSpec((H, LANES), lambda i: (0, 0))],
        out_shape=[jax.ShapeDtypeStruct((H, S), F32), jax.ShapeDtypeStruct((H, LANES), F32)],
        scratch_shapes=[pltpu.VMEM((H, LANES), F32)],
        compiler_params=_params("arbitrary"),
    )(fg_t, b_col, dc_plus, dc_minus)


ONES_ROWS = 16
EXTRA = 3


def _split3(x):
    hi = lax.reduce_precision(x, 8, 7)
    mid = lax.reduce_precision(x - hi, 8, 7)
    lo = lax.reduce_precision(x - hi - mid, 8, 7)
    return hi.astype(BF16), mid.astype(BF16), lo.astype(BF16)


def _lanes_operand(t, extras):
    block = jnp.pad(jnp.stack(extras, axis=-1), ((0, 0), (0, 0), (0, LANES - HEAD_DIM - len(extras))))
    return jnp.concatenate([t, block], axis=-1)


def _rows_operand(t, extras):
    block = jnp.pad(jnp.stack(extras, axis=1), ((0, 0), (0, LANES - HEAD_DIM - len(extras)), (0, 0)))
    return jnp.concatenate([t, block], axis=1)


def _with_ones(t):
    return jnp.concatenate([t, jnp.ones((t.shape[0], ONES_ROWS, t.shape[2]), t.dtype)], axis=1)


def _fox_fwd(qa, ka, vt, name, tq=512, tk=1024):
    H, _, S = qa.shape
    Dh = HEAD_DIM
    tk = _fit(S, tk)
    tq = _fit(tk, tq)
    ratio = tk // tq

    def body(qa_ref, ka_ref, vt_ref, o_ref, lse_ref, m_ref, acc_ref, sa_ref, sb_ref, ta_ref, tb_ref):
        i = pl.program_id(1)
        qv = qa_ref[...] * QK_SCALE
        m_ref[...] = jnp.full_like(m_ref, NEG)
        acc_ref[...] = jnp.zeros_like(acc_ref)
        n = i // ratio
        q_off = (i - n * ratio) * tq

        def scores(j, s_ref, t_ref, diagonal):
            off = pl.multiple_of(j * tk, LANES)
            s = _dot(ka_ref[pl.ds(off, tk), :], qv, _NN)
            if diagonal:
                key = lax.broadcasted_iota(jnp.int32, (tk, tq), 0)
                qry = lax.broadcasted_iota(jnp.int32, (tk, tq), 1) + q_off
                s = jnp.where(key <= qry, s, NEG)
            s_ref[...] = s
            t_ref[...] = jnp.max(s, axis=0, keepdims=True)

        def absorb(j, s_ref, t_ref):
            off = pl.multiple_of(j * tk, LANES)
            m_old = m_ref[...]
            m_new = jnp.maximum(m_old, t_ref[...])
            p = jnp.exp(s_ref[...] - m_new)
            alpha = jnp.exp(m_old - m_new)
            acc_ref[...] = alpha * acc_ref[...] + _dot(vt_ref[:, pl.ds(off, tk)], p.astype(BF16), _NN)
            m_ref[...] = m_new

        scores(n, sa_ref, ta_ref, True)

        def loop_body(jj, carry):
            scores(2 * jj, sb_ref, tb_ref, False)
            absorb(jnp.where(jj == 0, n, 2 * jj - 1), sa_ref, ta_ref)
            scores(2 * jj + 1, sa_ref, ta_ref, False)
            absorb(2 * jj, sb_ref, tb_ref)
            return carry

        pairs = n // 2
        lax.fori_loop(0, pairs, loop_body, 0)
        held = jnp.where(pairs == 0, n, 2 * pairs - 1)

        @pl.when(n % 2 == 1)
        def _():
            scores(n - 1, sb_ref, tb_ref, False)
            absorb(held, sa_ref, ta_ref)
            absorb(n - 1, sb_ref, tb_ref)

        @pl.when(n % 2 == 0)
        def _():
            absorb(held, sa_ref, ta_ref)

        l = acc_ref[Dh:Dh + 1, :]
        o_ref[...] = acc_ref[0:Dh, :] / l
        lse_ref[...] = m_ref[...] + jnp.log(l)

    return pl.pallas_call(
        body, name=name, grid=(H, S // tq),
        in_specs=[pl.BlockSpec((None, LANES, tq), lambda h, i: (h, 0, i)),
                  pl.BlockSpec((None, S, LANES), lambda h, i: (h, 0, 0), pipeline_mode=pl.Buffered(1)),
                  pl.BlockSpec((None, Dh + ONES_ROWS, S), lambda h, i: (h, 0, 0), pipeline_mode=pl.Buffered(1))],
        out_specs=[pl.BlockSpec((None, Dh, tq), lambda h, i: (h, 0, i)),
                   pl.BlockSpec((None, 1, tq), lambda h, i: (h, 0, i))],
        out_shape=[jax.ShapeDtypeStruct((H, Dh, S), F32), jax.ShapeDtypeStruct((H, 1, S), F32)],
        scratch_shapes=[pltpu.VMEM((1, tq), F32), pltpu.VMEM((Dh + ONES_ROWS, tq), F32),
                        pltpu.VMEM((tk, tq), F32), pltpu.VMEM((tk, tq), F32),
                        pltpu.VMEM((1, tq), F32), pltpu.VMEM((1, tq), F32)],
        compiler_params=_params("parallel", "arbitrary"),
    )(qa, ka, vt)


def _fox_bwd(qa, ka, kta, va, doa, name, tq=1024, tk=512):
    H, _, S = qa.shape
    Dh, Da = HEAD_DIM, HEAD_DIM + ONES_ROWS
    tq = _fit(S, tq)
    tk = _fit(tq, tk)
    ratio = tq // tk
    nq = S // tq
    nk = S // tk

    def body(ka_ref, kta_ref, va_ref, qa_ref, doa_ref, dqt_ref, dkt_ref, dvt_ref, dka_ref, dva_ref):
        j = pl.program_id(1)

        @pl.when(j == 0)
        def _():
            dqt_ref[...] = jnp.zeros_like(dqt_ref)

        kv = ka_ref[...]
        ktv = kta_ref[0:Da, :]
        vv = va_ref[...]
        dka_ref[...] = jnp.zeros_like(dka_ref)
        dva_ref[...] = jnp.zeros_like(dva_ref)
        i_diag = j // ratio
        k_off = (j - i_diag * ratio) * tk

        def step(i, diagonal):
            off = pl.multiple_of(i * tq, LANES)
            qv = qa_ref[:, pl.ds(off, tq)] * QK_SCALE
            dov = doa_ref[:, pl.ds(off, tq)]
            e = _dot(kv, qv, _NN)
            if diagonal:
                key = lax.broadcasted_iota(jnp.int32, (tk, tq), 0) + k_off
                qry = lax.broadcasted_iota(jnp.int32, (tk, tq), 1)
                e = jnp.where(key <= qry, e, NEG)
            p_t = jnp.exp(e)
            dva_ref[...] += _dot(dov[0:Dh, :], p_t.astype(BF16), _NT)
            ds_b = (p_t * _dot(vv, dov, _NN)).astype(BF16)
            dka_ref[...] += _dot(qv[0:Da, :], ds_b, _NT)
            dqt_ref[:, pl.ds(off, tq)] += _dot(ktv, ds_b, _NN)

        step(i_diag, True)

        def loop_body(i, carry):
            step(i, False)
            return carry

        lax.fori_loop(i_diag + 1, nq, loop_body, 0)
        dkt_ref[...] = dka_ref[...]
        dvt_ref[...] = dva_ref[...]

        @pl.when(j == nk - 1)
        def _():
            dqt_ref[0:Dh, :] = dqt_ref[0:Dh, :] * QK_SCALE

    lanes_tile = pl.BlockSpec((None, tk, LANES), lambda h, j: (h, j, 0))
    rows_tile = pl.BlockSpec((None, LANES, tk), lambda h, j: (h, 0, j))
    rows_full = pl.BlockSpec((None, LANES, S), lambda h, j: (h, 0, 0), pipeline_mode=pl.Buffered(1))
    return pl.pallas_call(
        body, name=name, grid=(H, nk),
        in_specs=[lanes_tile, rows_tile, lanes_tile, rows_full, rows_full],
        out_specs=[pl.BlockSpec((None, Da, S), lambda h, j: (h, 0, 0)),
                   pl.BlockSpec((None, Da, tk), lambda h, j: (h, 0, j)),
                   pl.BlockSpec((None, Dh, tk), lambda h, j: (h, 0, j))],
        out_shape=[jax.ShapeDtypeStruct((H, Da, S), F32), jax.ShapeDtypeStruct((H, Da, S), F32),
                   jax.ShapeDtypeStruct((H, Dh, S), F32)],
        scratch_shapes=[pltpu.VMEM((Da, tk), F32), pltpu.VMEM((Dh, tk), F32)],
        compiler_params=_params("parallel", "arbitrary"),
    )(ka, kta, va, qa, doa)


DIL_Q_BLOCK = 3 * FOX_WIDTH // LANES
HEAD_PAIRS = N_DIL_HEADS // 2
PAIR_BLOCKS = DIL_WIDTH // LANES


def _band_geometry(S, d):
    L = S // d
    chunk = min(BAND_CHUNK_MAX, L)
    assert L % chunk == 0 and chunk % BAND == 0
    return L, chunk, chunk // BAND, L // chunk


def _band_in_specs(S, d, base):
    L, chunk, nb, _ = _band_geometry(S, d)

    def col(kind):
        return lambda hp, r, i: (r, i, base + kind * PAIR_BLOCKS + hp)

    def col_prev(kind):
        return lambda hp, r, i: (r, jnp.maximum(i * nb - 1, 0), base + kind * PAIR_BLOCKS + hp)

    main = [pl.BlockSpec((None, chunk, LANES), col(kind)) for kind in range(3)]
    prev = [pl.BlockSpec((None, BAND, LANES), col_prev(kind)) for kind in range(3)]
    bias = pl.BlockSpec((None, 2 * BAND, 2 * BAND), lambda hp, r, i: (hp, 0, 0))
    stat = pl.BlockSpec((None, 2, chunk), lambda hp, r, i: (hp, 0, r * (L // chunk) + i))
    tok = pl.BlockSpec((None, chunk, LANES), lambda hp, r, i: (r, i, hp))
    return main, prev, bias, stat, tok


def _to_residues(x, col_block, width, d, name, tr=1024):
    S = x.shape[0]
    tr = _fit(S, tr)

    def body(x_ref, o_ref, tmp_ref):
        for j in range(width // LANES):
            cols = slice(j * LANES, (j + 1) * LANES)
            tmp_ref[j] = x_ref[:, cols].astype(F32)
            for r in range(d):
                o_ref[r, :, cols] = tmp_ref[j, pl.ds(r, tr // d, stride=d), :].astype(o_ref.dtype)

    return pl.pallas_call(
        body, name=name, grid=(S // tr,),
        in_specs=[pl.BlockSpec((tr, width), lambda i: (i, col_block))],
        out_specs=pl.BlockSpec((d, tr // d, width), lambda i: (0, i, 0)),
        out_shape=jax.ShapeDtypeStruct((d, S // d, width), x.dtype),
        scratch_shapes=[pltpu.VMEM((width // LANES, tr, LANES), F32)],
        compiler_params=_params("parallel"),
    )(x)


def _token_rows(ref, cols, tmp_ref):
    if len(ref.shape) == 2:
        return ref[:, cols].astype(F32)
    d, rows = ref.shape[0], ref.shape[1]
    for r in range(d):
        tmp_ref[pl.ds(r, rows, stride=d), :] = ref[r, :, cols].astype(F32)
    return tmp_ref[...]


def _row_spec(t, tr):
    if t.ndim == 2:
        return pl.BlockSpec((tr, t.shape[1]), lambda i: (i, 0))
    d = t.shape[0]
    return pl.BlockSpec((d, tr // d, t.shape[2]), lambda i: (0, i, 0))


def _head_lanes(a):
    return lax.broadcasted_iota(jnp.int32, (1, LANES), 1) // HEAD_DIM == a


def _one_head(x, a):
    return jnp.where(_head_lanes(a), x, jnp.zeros_like(x))


def _head_stack(x):
    return jnp.concatenate([_one_head(x, 0), _one_head(x, 1)], axis=0)


def _pair_rows(ref, rows):
    return jnp.concatenate([ref[0:1, rows], ref[1:2, rows]], axis=1)


def _band_scores_t(kb, q_stack, bias_t, first):
    s = _dot(kb, q_stack, _NT) + bias_t
    if first is not None:
        key = lax.broadcasted_iota(jnp.int32, s.shape, 0)
        s = jnp.where(jnp.logical_and(first, key < BAND), NEG, s)
    return s


def _pair_select(stacked):
    return jnp.where(_head_lanes(0), stacked[0:BAND, :], stacked[BAND:, :])


def _dil_lse(qkv_v, base, bias_t, name):
    d, L = qkv_v.shape[:2]
    S = L * d
    _, chunk, nb, nchunks = _band_geometry(S, d)
    main, prev, bias, stat, _ = _band_in_specs(S, d, base)

    def body(q_ref, k_ref, kp_ref, b_ref, lse_ref, kext_ref):
        first = pl.program_id(2) == 0
        kext_ref[0:BAND, :] = kp_ref[...]
        kext_ref[BAND:, :] = k_ref[...]
        for b in range(nb):
            rows, ext = slice(b * BAND, (b + 1) * BAND), slice(b * BAND, (b + 2) * BAND)
            s = _band_scores_t(kext_ref[ext, :], _head_stack(q_ref[rows, :] * QK_SCALE), b_ref[...],
                               first if b == 0 else None)
            m = jnp.max(s, axis=0, keepdims=True)
            lse = m + jnp.log(jnp.sum(jnp.exp(s - m), axis=0, keepdims=True))
            lse_ref[0:1, rows] = lse[:, 0:BAND]
            lse_ref[1:2, rows] = lse[:, BAND:]

    return pl.pallas_call(
        body, name=name, grid=(HEAD_PAIRS, d, nchunks),
        in_specs=[main[0], main[1], prev[1], bias], out_specs=stat,
        out_shape=jax.ShapeDtypeStruct((HEAD_PAIRS, 2, S), F32),
        scratch_shapes=[pltpu.VMEM((chunk + BAND, LANES), BF16)],
        compiler_params=_params("parallel", "parallel", "parallel"),
    )(qkv_v, qkv_v, qkv_v, bias_t)


def _dil_out(qkv_v, base, bias_t, lse_joint, name):
    d, L = qkv_v.shape[:2]
    S = L * d
    _, chunk, nb, nchunks = _band_geometry(S, d)
    main, prev, bias, stat, tok = _band_in_specs(S, d, base)

    def body(q_ref, k_ref, kp_ref, v_ref, vp_ref, b_ref, lse_ref, o_ref, kext_ref, vext_ref):
        first = pl.program_id(2) == 0
        kext_ref[0:BAND, :] = kp_ref[...]
        kext_ref[BAND:, :] = k_ref[...]
        vext_ref[0:BAND, :] = vp_ref[...]
        vext_ref[BAND:, :] = v_ref[...]
        for b in range(nb):
            rows, ext = slice(b * BAND, (b + 1) * BAND), slice(b * BAND, (b + 2) * BAND)
            s = _band_scores_t(kext_ref[ext, :], _head_stack(q_ref[rows, :] * QK_SCALE), b_ref[...],
                               first if b == 0 else None)
            p_t = jnp.exp(s - _pair_rows(lse_ref, rows))
            o_ref[rows, :] = _pair_select(_dot(p_t.astype(BF16), vext_ref[ext, :], _TN)).astype(BF16)

    return pl.pallas_call(
        body, name=name, grid=(HEAD_PAIRS, d, nchunks),
        in_specs=[main[0], main[1], prev[1], main[2], prev[2], bias, stat], out_specs=tok,
        out_shape=jax.ShapeDtypeStruct((d, L, DIL_WIDTH), BF16),
        scratch_shapes=[pltpu.VMEM((chunk + BAND, LANES), BF16), pltpu.VMEM((chunk + BAND, LANES), BF16)],
        compiler_params=_params("parallel", "parallel", "parallel"),
    )(qkv_v, qkv_v, qkv_v, qkv_v, qkv_v, bias_t, lse_joint)


def _dil_bwd(qkv_v, base, do_v, bias_t, lse_joint, delta, name):
    d, L = qkv_v.shape[:2]
    S = L * d
    _, chunk, nb, nchunks = _band_geometry(S, d)
    main, prev, bias, stat, tok = _band_in_specs(S, d, base)
    nblocks = L // BAND

    def nxt_row(i):
        return jnp.minimum((i + 1) * nb, nblocks - 1)

    q_next = pl.BlockSpec((None, BAND, LANES), lambda hp, r, i: (r, nxt_row(i), base + hp))
    do_next = pl.BlockSpec((None, BAND, LANES), lambda hp, r, i: (r, nxt_row(i), hp))
    stat_next = pl.BlockSpec((None, 2, BAND), lambda hp, r, i: (hp, 0, r * nblocks + nxt_row(i)))

    def body(q_ref, k_ref, kp_ref, v_ref, vp_ref, do_ref, b_ref, lse_ref, dl_ref,
             qn_ref, don_ref, lsen_ref, dln_ref,
             dq_ref, dk_ref, dv_ref, db_ref, kext_ref, vext_ref, dkext_ref, dvext_ref):
        r, i = pl.program_id(1), pl.program_id(2)
        first = i == 0
        has_next = i + 1 < nchunks
        tail = slice(BAND + chunk, 2 * BAND + chunk)
        kext_ref[0:BAND, :] = kp_ref[...]
        kext_ref[BAND:BAND + chunk, :] = k_ref[...]
        kext_ref[tail, :] = jnp.zeros((BAND, LANES), BF16)
        vext_ref[0:BAND, :] = vp_ref[...]
        vext_ref[BAND:BAND + chunk, :] = v_ref[...]
        vext_ref[tail, :] = jnp.zeros((BAND, LANES), BF16)
        dkext_ref[...] = jnp.zeros_like(dkext_ref)
        dvext_ref[...] = jnp.zeros_like(dvext_ref)

        @pl.when(jnp.logical_and(r == 0, i == 0))
        def _():
            db_ref[...] = jnp.zeros_like(db_ref)

        def block(q2, do2, lse_row, dl_row, ext, mask_rows):
            q_stack, do_stack = _head_stack(q2), _head_stack(do2)
            s = _dot(kext_ref[ext, :], q_stack, _NT) + b_ref[...]
            if mask_rows is not None:
                s = jnp.where(mask_rows, NEG, s)
            p_t = jnp.exp(s - lse_row)
            ds_t = p_t * (_dot(vext_ref[ext, :], do_stack, _NT) - dl_row)
            ds_b = ds_t.astype(BF16)
            dkext_ref[ext, :] += _dot(ds_b, q_stack, _NN)
            dvext_ref[ext, :] += _dot(p_t.astype(BF16), do_stack, _NN)
            return ds_t, ds_b

        key = lax.broadcasted_iota(jnp.int32, (2 * BAND, 2 * BAND), 0)
        all_lanes = slice(0, BAND)
        for b in range(nb):
            rows, ext = slice(b * BAND, (b + 1) * BAND), slice(b * BAND, (b + 2) * BAND)
            mask = jnp.logical_and(first, key < BAND) if b == 0 else None
            ds_t, ds_b = block(q_ref[rows, :] * QK_SCALE, do_ref[rows, :], _pair_rows(lse_ref, rows),
                               _pair_rows(dl_ref, rows), ext, mask)
            dq_ref[rows, :] = (_pair_select(_dot(ds_b, kext_ref[ext, :], _TN)) * QK_SCALE).astype(BF16)
            db_ref[...] += ds_t
        block(qn_ref[...] * QK_SCALE, don_ref[...], _pair_rows(lsen_ref, all_lanes), _pair_rows(dln_ref, all_lanes),
              slice(chunk, chunk + 2 * BAND), jnp.logical_or(jnp.logical_not(has_next), key >= BAND))
        dk_ref[...] = dkext_ref[BAND:BAND + chunk, :].astype(BF16)
        dv_ref[...] = dvext_ref[BAND:BAND + chunk, :].astype(BF16)

    ext_rows = chunk + 2 * BAND
    return pl.pallas_call(
        body, name=name, grid=(HEAD_PAIRS, d, nchunks),
        in_specs=[main[0], main[1], prev[1], main[2], prev[2], tok, bias, stat, stat,
                  q_next, do_next, stat_next, stat_next],
        out_specs=[tok, tok, tok, bias],
        out_shape=[jax.ShapeDtypeStruct((d, L, DIL_WIDTH), BF16)] * 3
                  + [jax.ShapeDtypeStruct((HEAD_PAIRS, 2 * BAND, 2 * BAND), F32)],
        scratch_shapes=[pltpu.VMEM((ext_rows, LANES), BF16), pltpu.VMEM((ext_rows, LANES), BF16),
                        pltpu.VMEM((ext_rows, LANES), F32), pltpu.VMEM((ext_rows, LANES), F32)],
        compiler_params=_params("arbitrary", "arbitrary", "arbitrary"),
    )(qkv_v, qkv_v, qkv_v, qkv_v, qkv_v, do_v, bias_t, lse_joint, delta, qkv_v, do_v, lse_joint, delta)


def _lse_join(lse3, name):
    P, H, S = lse3.shape

    def body(l_ref, o_ref):
        a, b, c = l_ref[0], l_ref[1], l_ref[2]
        m = jnp.maximum(jnp.maximum(a, b), c)
        o_ref[...] = m + jnp.log(jnp.exp(a - m) + jnp.exp(b - m) + jnp.exp(c - m))

    return pl.pallas_call(body, name=name, out_shape=jax.ShapeDtypeStruct((H, S), F32))(lse3)


def _bucket_reduce(dbias_t, bucket_map_t, name):
    P, H = dbias_t.shape[:2]

    def body(db_ref, bk_ref, o_ref):
        p, h = pl.program_id(0), pl.program_id(1)

        @pl.when(jnp.logical_and(p == 0, h == 0))
        def _():
            o_ref[...] = jnp.zeros_like(o_ref)

        db, bk = db_ref[...], bk_ref[...]
        row = lax.broadcasted_iota(jnp.int32, (N_BUCKETS, LANES), 0)
        lane = lax.broadcasted_iota(jnp.int32, (N_BUCKETS, LANES), 1)

        def one(b, acc):
            val = jnp.sum(jnp.sum(jnp.where(bk == b, db, 0.0), axis=1, keepdims=True), axis=0, keepdims=True)
            return acc + jnp.where(jnp.logical_and(row == b, lane == h), val, 0.0)

        acc = jnp.zeros((N_BUCKETS, LANES), F32)
        for b in range(N_BUCKETS):
            acc = one(b, acc)
        o_ref[...] += acc

    return pl.pallas_call(
        body, name=name, grid=(P, H),
        in_specs=[pl.BlockSpec((None, None, 2 * BAND, BAND), lambda p, h: (p, h, 0, 0)),
                  pl.BlockSpec((None, 2 * BAND, BAND), lambda p, h: (p, 0, 0))],
        out_specs=pl.BlockSpec((N_BUCKETS, LANES), lambda p, h: (0, 0)),
        out_shape=jax.ShapeDtypeStruct((N_BUCKETS, LANES), F32),
        compiler_params=_params("arbitrary", "arbitrary"),
    )(dbias_t, bucket_map_t)


def _mem_fwd(q, kv, name, tq=1024):
    S, W = q.shape
    N = kv.shape[0]
    pairs = W // LANES
    tq = _fit(S, tq)

    def body(q_ref, k_ref, v_ref, o_ref, lse_ref):
        for a in range(2):
            lanes = slice(a * HEAD_DIM, (a + 1) * HEAD_DIM)
            s = _dot(k_ref[:, lanes], q_ref[:, lanes] * QK_SCALE, _NT)
            m = jnp.max(s, axis=0, keepdims=True)
            e = jnp.exp(s - m)
            l = jnp.sum(e, axis=0, keepdims=True)
            o_ref[:, lanes] = _dot((e / l).astype(BF16), v_ref[:, lanes], _TN).astype(BF16)
            lse_ref[a:a + 1, :] = m + jnp.log(l)

    return pl.pallas_call(
        body, name=name, grid=(pairs, S // tq),
        in_specs=[pl.BlockSpec((tq, LANES), lambda hp, i: (i, hp)),
                  pl.BlockSpec((N, LANES), lambda hp, i: (0, hp)),
                  pl.BlockSpec((N, LANES), lambda hp, i: (0, pairs + hp))],
        out_specs=[pl.BlockSpec((tq, LANES), lambda hp, i: (i, hp)),
                   pl.BlockSpec((None, 2, tq), lambda hp, i: (hp, 0, i))],
        out_shape=[jax.ShapeDtypeStruct((S, W), BF16), jax.ShapeDtypeStruct((pairs, 2, S), F32)],
        compiler_params=_params("parallel", "parallel"),
    )(q, kv, kv)


def _mem_bwd(q, kv, do, lse, delta, name, tq=1024):
    S, W = q.shape
    N = kv.shape[0]
    pairs = W // LANES
    tq = _fit(S, tq)

    def body(q_ref, k_ref, v_ref, do_ref, lse_ref, dl_ref, dq_ref, dk_ref, dv_ref):
        i = pl.program_id(1)

        @pl.when(i == 0)
        def _():
            dk_ref[...] = jnp.zeros_like(dk_ref)
            dv_ref[...] = jnp.zeros_like(dv_ref)

        for a in range(2):
            lanes = slice(a * HEAD_DIM, (a + 1) * HEAD_DIM)
            qv, dov = q_ref[:, lanes] * QK_SCALE, do_ref[:, lanes]
            kv_, vv = k_ref[:, lanes], v_ref[:, lanes]
            p_t = jnp.exp(_dot(kv_, qv, _NT) - lse_ref[a:a + 1, :])
            ds_t = p_t * (_dot(vv, dov, _NT) - dl_ref[a:a + 1, :])
            ds_b = ds_t.astype(BF16)
            dq_ref[:, lanes] = (_dot(ds_b, kv_, _TN) * QK_SCALE).astype(BF16)
            dk_ref[:, lanes] += _dot(ds_b, qv, _NN)
            dv_ref[:, lanes] += _dot(p_t.astype(BF16), dov, _NN)

    qs = pl.BlockSpec((tq, LANES), lambda hp, i: (i, hp))
    stat = pl.BlockSpec((None, 2, tq), lambda hp, i: (hp, 0, i))
    acc = pl.BlockSpec((N, LANES), lambda hp, i: (0, hp))
    return pl.pallas_call(
        body, name=name, grid=(pairs, S // tq),
        in_specs=[qs, acc, pl.BlockSpec((N, LANES), lambda hp, i: (0, pairs + hp)), qs, stat, stat],
        out_specs=[qs, acc, acc],
        out_shape=[jax.ShapeDtypeStruct((S, W), BF16), jax.ShapeDtypeStruct((N, W), F32),
                   jax.ShapeDtypeStruct((N, W), F32)],
        compiler_params=_params("parallel", "arbitrary"),
    )(q, kv, kv, do, lse, delta)


def _head_rowdot(a, bs, name, tr=512):
    S, W = a.shape
    tr = _fit(S, tr)

    def body(*refs):
        a_ref, b_refs, o_ref, tmp_ref = refs[0], refs[1:-2], refs[-2], refs[-1]
        col = lax.broadcasted_iota(jnp.int32, (LANES, LANES), 0)
        lane = lax.broadcasted_iota(jnp.int32, (LANES, LANES), 1)
        acc = jnp.zeros((tr, LANES), F32)
        for j in range(W // LANES):
            cols = slice(j * LANES, (j + 1) * LANES)
            tot = _token_rows(b_refs[0], cols, tmp_ref)
            for r in b_refs[1:]:
                tot = tot + _token_rows(r, cols, tmp_ref)
            sel = jnp.where(col // HEAD_DIM + j * (LANES // HEAD_DIM) == lane, 1.0, 0.0).astype(F32)
            acc = acc + lax.dot_general(a_ref[:, cols].astype(F32) * tot, sel, (_NN, ((), ())),
                                        precision=lax.Precision.HIGHEST, preferred_element_type=F32)
        o_ref[...] = acc

    return pl.pallas_call(
        body, name=name, grid=(S // tr,), in_specs=[_row_spec(t, tr) for t in [a] + list(bs)],
        out_specs=pl.BlockSpec((tr, LANES), lambda i: (i, 0)),
        out_shape=jax.ShapeDtypeStruct((S, LANES), F32),
        scratch_shapes=[pltpu.VMEM((tr, LANES), F32)],
        compiler_params=_params("parallel"),
    )(a, *bs)


def _sum_cast_cols(groups, out_dtype, name, tail=None, tr=512):
    first = groups[0][0]
    S, W = (first.shape if first.ndim == 2 else (first.shape[0] * first.shape[1], first.shape[2]))
    tr = _fit(S, tr)
    flat = [t for g in groups for t in g] + ([tail] if tail is not None else [])
    tail_w = 0 if tail is None else tail.shape[1]

    def body(*refs):
        o_ref, tmp_ref = refs[-2], refs[-1]
        if tail is not None:
            o_ref[:, W * len(groups):] = refs[-3][...].astype(out_dtype)
        k = 0
        for gi, g in enumerate(groups):
            for j in range(W // LANES):
                cols = slice(j * LANES, (j + 1) * LANES)
                acc = _token_rows(refs[k], cols, tmp_ref)
                for r in refs[k + 1:k + len(g)]:
                    acc = acc + _token_rows(r, cols, tmp_ref)
                o_ref[:, gi * W + j * LANES:gi * W + (j + 1) * LANES] = acc.astype(out_dtype)
            k += len(g)

    return pl.pallas_call(
        body, name=name, grid=(S // tr,), in_specs=[_row_spec(t, tr) for t in flat],
        out_specs=pl.BlockSpec((tr, W * len(groups) + tail_w), lambda i: (i, 0)),
        out_shape=jax.ShapeDtypeStruct((S, W * len(groups) + tail_w), out_dtype),
        scratch_shapes=[pltpu.VMEM((tr, LANES), F32)],
        compiler_params=_params("parallel"),
    )(*flat)


FF_TILE = 256


def _ffn_up(h, w_gu, name, tm=4096):
    S, D = h.shape
    F2 = w_gu.shape[1]
    tm = _fit(S, tm)

    def body(h_ref, w_ref, gu_ref, act_ref):
        gu = _dot(h_ref[...], w_ref[...], _NN)
        gu_ref[...] = gu.astype(BF16)
        g, u = gu[:, :FF_TILE], gu[:, FF_TILE:]
        act_ref[...] = (g * (1.0 / (1.0 + jnp.exp(-g))) * u).astype(BF16)

    return pl.pallas_call(
        body, name=name, grid=(S // tm, F2 // (2 * FF_TILE)),
        in_specs=[pl.BlockSpec((tm, D), lambda i, j: (i, 0)), pl.BlockSpec((D, 2 * FF_TILE), lambda i, j: (0, j))],
        out_specs=[pl.BlockSpec((tm, 2 * FF_TILE), lambda i, j: (i, j)),
                   pl.BlockSpec((tm, FF_TILE), lambda i, j: (i, j))],
        out_shape=[jax.ShapeDtypeStruct((S, F2), BF16), jax.ShapeDtypeStruct((S, F2 // 2), BF16)],
        compiler_params=_params("parallel", "arbitrary"),
    )(h, w_gu)


def _ffn_dact(dy, w_down, gu, name, tm=4096):
    S, D = dy.shape
    F2 = gu.shape[1]
    tm = _fit(S, tm)

    def body(dy_ref, w_ref, gu_ref, dgu_ref):
        dact = _dot(dy_ref[...], w_ref[...], _NT)
        gu_v = gu_ref[...].astype(F32)
        g, u = gu_v[:, :FF_TILE], gu_v[:, FF_TILE:]
        sig = 1.0 / (1.0 + jnp.exp(-g))
        silu = g * sig
        dgu_ref[:, :FF_TILE] = (dact * u * (sig + silu * (1.0 - sig))).astype(BF16)
        dgu_ref[:, FF_TILE:] = (dact * silu).astype(BF16)

    return pl.pallas_call(
        body, name=name, grid=(S // tm, F2 // (2 * FF_TILE)),
        in_specs=[pl.BlockSpec((tm, D), lambda i, j: (i, 0)), pl.BlockSpec((FF_TILE, D), lambda i, j: (j, 0)),
                  pl.BlockSpec((tm, 2 * FF_TILE), lambda i, j: (i, j))],
        out_specs=pl.BlockSpec((tm, 2 * FF_TILE), lambda i, j: (i, j)),
        out_shape=jax.ShapeDtypeStruct((S, F2), BF16),
        compiler_params=_params("parallel", "arbitrary"),
    )(dy, w_down, gu)


def _fit_rows(n, cap):
    if n <= cap:
        return n
    t = (cap // 8) * 8
    while t >= 8:
        if n % t == 0:
            return t
        t -= 8
    raise ValueError(f"no sublane-aligned tile for {n} under {cap}")


def _add_n(arrs, name, tr=512):
    R, C = arrs[0].shape
    tr = _fit_rows(R, tr)

    def body(*refs):
        acc = refs[0][...].astype(F32)
        for r in refs[1:-1]:
            acc = acc + r[...].astype(F32)
        refs[-1][...] = acc

    row = pl.BlockSpec((tr, C), lambda i: (i, 0))
    return pl.pallas_call(
        body, name=name, grid=(R // tr,), in_specs=[row] * len(arrs), out_specs=row,
        out_shape=jax.ShapeDtypeStruct((R, C), F32), compiler_params=_params("parallel"),
    )(*arrs)


def _adamw(w, g, m, v, name, tr=512):
    R, C = w.shape
    tr = _fit_rows(R, tr)
    c1 = 1.0 / (1.0 - ADAM_B1 ** ADAM_STEP)
    c2 = 1.0 / (1.0 - ADAM_B2 ** ADAM_STEP)

    def body(w_ref, g_ref, m_ref, v_ref, d_ref, nm_ref, nv_ref):
        gv = g_ref[...]
        nm = ADAM_B1 * m_ref[...] + (1.0 - ADAM_B1) * gv
        nv = ADAM_B2 * v_ref[...] + (1.0 - ADAM_B2) * (gv * gv)
        nm_ref[...] = nm
        nv_ref[...] = nv
        d_ref[...] = -ADAM_LR * ((nm * c1) / (jnp.sqrt(nv * c2) + ADAM_EPS) + ADAM_WD * w_ref[...])

    row = pl.BlockSpec((tr, C), lambda i: (i, 0))
    return pl.pallas_call(
        body, name=name, grid=(R // tr,), in_specs=[row] * 4, out_specs=[row] * 3,
        out_shape=[jax.ShapeDtypeStruct((R, C), F32)] * 3, compiler_params=_params("parallel"),
    )(w, g, m, v)


def _place():
    return lax.axis_index("x"), lax.axis_index("y"), lax.axis_index("c")


_ANY = pl.BlockSpec(memory_space=pl.ANY)


def _chip_all_gather(shard, name):
    R, C = shard.shape
    half = R // 2

    def body(x_ref, out_ref, send_sems, recv_sems, local_sem):
        x, y, c = _place()
        chips = [(1 - x, y), (x, 1 - y), (1 - x, 1 - y)]
        sibling = (x, y, 1 - c)
        mine = pltpu.make_async_copy(x_ref, out_ref.at[2 * x + y], local_sem)
        mine.start()

        def rows(chip, core):
            return out_ref.at[chip, pl.ds(core * half, half)]

        def copy(k, chip, core, to, src=None):
            return pltpu.make_async_remote_copy(
                src_ref=rows(chip, core) if src is None else src, dst_ref=rows(chip, core),
                send_sem=send_sems.at[k], recv_sem=recv_sems.at[k], device_id=to, device_id_type=MESH_IDS)

        me = 2 * x + y
        first = [copy(k, me, c, (cx, cy, c), src=x_ref.at[pl.ds(c * half, half)]) for k, (cx, cy) in enumerate(chips)]
        for cp in first:
            cp.start()
        passed = [copy(3 + k, 2 * cx + cy, c, sibling) for k, (cx, cy) in enumerate(chips)]
        for k, (cx, cy) in enumerate(chips):
            copy(k, 2 * cx + cy, c, (cx, cy, c)).wait_recv()
            passed[k].start()
        for k, (cx, cy) in enumerate(chips):
            copy(3 + k, 2 * cx + cy, 1 - c, sibling).wait_recv()
        for cp in first + passed:
            cp.wait_send()
        mine.wait()

    return pl.pallas_call(
        body, name=name, in_specs=[_ANY], out_specs=_ANY,
        out_shape=jax.ShapeDtypeStruct((N_CHIPS, R, C), shard.dtype),
        scratch_shapes=[pltpu.SemaphoreType.DMA((6,)), pltpu.SemaphoreType.DMA((6,)), pltpu.SemaphoreType.DMA],
    )(shard)


def _sibling_exchange(buf, name):
    def body(x_ref, out_ref, send_sem, recv_sem):
        x, y, c = _place()
        cp = pltpu.make_async_remote_copy(
            src_ref=x_ref, dst_ref=out_ref, send_sem=send_sem, recv_sem=recv_sem,
            device_id=(x, y, 1 - c), device_id_type=MESH_IDS)
        cp.start()
        cp.wait()

    return pl.pallas_call(
        body, name=name, in_specs=[_ANY], out_specs=_ANY,
        out_shape=jax.ShapeDtypeStruct(buf.shape, buf.dtype),
        scratch_shapes=[pltpu.SemaphoreType.DMA, pltpu.SemaphoreType.DMA],
    )(buf)


def _chip_scatter(parts, name):
    _, R, C = parts.shape

    def body(p_ref, out_ref, send_sems, recv_sems):
        x, y, c = _place()
        chips = [(1 - x, y), (x, 1 - y), (1 - x, 1 - y)]

        def copy(k, slab, to):
            return pltpu.make_async_remote_copy(
                src_ref=p_ref.at[slab], dst_ref=out_ref.at[k], send_sem=send_sems.at[k], recv_sem=recv_sems.at[k],
                device_id=to, device_id_type=MESH_IDS)

        sends = [copy(k, 2 * cx + cy, (cx, cy, c)) for k, (cx, cy) in enumerate(chips)]
        for cp in sends:
            cp.start()
        for cp in sends:
            cp.wait_recv()
        for cp in sends:
            cp.wait_send()

    return pl.pallas_call(
        body, name=name, in_specs=[_ANY], out_specs=_ANY,
        out_shape=jax.ShapeDtypeStruct((3, R, C), parts.dtype),
        scratch_shapes=[pltpu.SemaphoreType.DMA((3,)), pltpu.SemaphoreType.DMA((3,))],
    )(parts)


def _all_to_all_small(vec, name):
    R, C = vec.shape

    def body(v_ref, out_ref, send_sems, recv_sems, local_sem):
        x, y, c = _place()
        me = 4 * x + 2 * y + c
        mine = pltpu.make_async_copy(v_ref, out_ref.at[me], local_sem)
        mine.start()
        flips = [(dx, dy, dc) for dx in (0, 1) for dy in (0, 1) for dc in (0, 1)][1:]

        def peer(f):
            return (x ^ f[0], y ^ f[1], c ^ f[2])

        def copy(k, slot, to):
            return pltpu.make_async_remote_copy(
                src_ref=v_ref, dst_ref=out_ref.at[slot], send_sem=send_sems.at[k], recv_sem=recv_sems.at[k],
                device_id=to, device_id_type=MESH_IDS)

        sends = [copy(k, me, peer(f)) for k, f in enumerate(flips)]
        for cp in sends:
            cp.start()
        for k, f in enumerate(flips):
            px, py, pc = peer(f)
            copy(k, 4 * px + 2 * py + pc, peer(f)).wait_recv()
        for cp in sends:
            cp.wait_send()
        mine.wait()

    return pl.pallas_call(
        body, name=name, in_specs=[_ANY], out_specs=_ANY,
        out_shape=jax.ShapeDtypeStruct((8, R, C), vec.dtype),
        scratch_shapes=[pltpu.SemaphoreType.DMA((7,)), pltpu.SemaphoreType.DMA((7,)), pltpu.SemaphoreType.DMA],
    )(vec)


def _to_heads(t, n):
    S = t.shape[0]
    return t.reshape(S, n, HEAD_DIM).transpose(1, 0, 2)


def _to_heads_t(t, n):
    S = t.shape[0]
    return t.T.reshape(n, HEAD_DIM, S)


def _from_heads_t(t):
    H, Dh, S = t.shape
    return t.reshape(H * Dh, S).T


def _t5_bucket(dist):
    max_exact = N_BUCKETS // 2
    d = np.maximum(dist, 1).astype(np.float32)
    large = max_exact + (np.log(d / max_exact) / np.log(MAX_DISTANCE / max_exact)
                         * (N_BUCKETS - max_exact)).astype(np.int32)
    large = np.minimum(large, N_BUCKETS - 1)
    return np.where(dist < max_exact, dist, large).astype(np.int32)


def _band_tables():
    qi = np.arange(BAND)[:, None]
    kj = np.arange(2 * BAND)[None, :]
    sub = qi + BAND - kj
    band = (sub >= 0) & (sub <= BAND)
    out = []
    for d in DILATIONS:
        bucket = _t5_bucket(np.clip(sub, 0, BAND) * d)
        out.append(np.where(band, bucket, -1).astype(np.int32))
    return np.stack(out)


_PACK = (("w_in", 770), ("w_out", 256), ("w_xq", 64), ("w_xk", 64), ("w_xv", 64), ("w_xo", 64),
         ("w_gate", 704), ("w_up", 704), ("w_down", 704))


def _pack(shards):
    rows = [shards[n].reshape(-1, PACK_COLS) for n, _ in _PACK]
    total = sum(r.shape[0] for r in rows)
    pad = (-total) % 128
    if pad:
        rows.append(jnp.zeros((pad, PACK_COLS), rows[0].dtype))
    return jnp.concatenate(rows, axis=0)


def _unpack(pack, shapes):
    out, r = {}, 0
    for n, _ in _PACK:
        cnt = int(np.prod(shapes[n])) // PACK_COLS
        out[n] = pack[r:r + cnt].reshape(shapes[n])
        r += cnt
    return out


_COL_SHARDED = ("w_in", "w_xo", "w_gate", "w_up")


def _full_weight(gathered, name):
    return jnp.concatenate(gathered, axis=1 if name in _COL_SHARDED else 0)


def _split_weight(full, name):
    return jnp.split(full, N_CHIPS, axis=1 if name in _COL_SHARDED else 0)


_SMALL = ("g_mix_pre", "g_mix_post", "g_xattn_pre", "g_mem", "g_xattn_post", "g_ffn_pre", "g_ffn_post")


def _pack_small(vals):
    D = vals["g_mix_pre"].shape[1]
    rows = [vals[n].reshape(1, D) for n in _SMALL]
    misc = jnp.concatenate([vals["b_f"].reshape(-1), vals["rel_bias"].reshape(-1)])
    rows.append(jnp.pad(misc, (0, D - misc.shape[0])).reshape(1, D))
    rows.append(jnp.zeros((16 - len(rows), D), F32))
    return jnp.concatenate(rows, axis=0)


def _unpack_small(pack):
    out = {n: pack[i:i + 1] for i, n in enumerate(_SMALL)}
    out["b_f"] = pack[7, 0:N_FOX_HEADS].reshape(1, N_FOX_HEADS)
    out["rel_bias"] = pack[7, N_FOX_HEADS:N_FOX_HEADS + N_BUCKETS * N_DIL_HEADS].reshape(N_BUCKETS, N_DIL_HEADS)
    return out


def kernel(x, mem, g_mix_pre, w_in, b_f, rel_bias, w_out, g_mix_post, g_xattn_pre, g_mem, w_xq, w_xk, w_xv, w_xo, g_xattn_post, g_ffn_pre, w_gate, w_up, w_down, g_ffn_post, loss_target, m_g_mix_pre, m_w_in, m_b_f, m_rel_bias, m_w_out, m_g_mix_post, m_g_xattn_pre, m_g_mem, m_w_xq, m_w_xk, m_w_xv, m_w_xo, m_g_xattn_post, m_g_ffn_pre, m_w_gate, m_w_up, m_w_down, m_g_ffn_post, v_g_mix_pre, v_w_in, v_b_f, v_rel_bias, v_w_out, v_g_mix_post, v_g_xattn_pre, v_g_mem, v_w_xq, v_w_xk, v_w_xv, v_w_xo, v_g_xattn_post, v_g_ffn_pre, v_w_gate, v_w_up, v_w_down, v_g_ffn_post):
    args = dict(locals())
    big = [n for n, _ in _PACK]
    names = ["g_mix_pre", "w_in", "b_f", "rel_bias", "w_out", "g_mix_post", "g_xattn_pre", "g_mem", "w_xq",
             "w_xk", "w_xv", "w_xo", "g_xattn_post", "g_ffn_pre", "w_gate", "w_up", "w_down", "g_ffn_post"]
    xs = x[0]
    S, D = xs.shape
    assert S % (BAND * DILATIONS[-1]) == 0
    shard_shapes = {n: args[n].shape[1:] for n in big}
    my_x, my_y, my_c = lax.axis_index("x"), lax.axis_index("y"), lax.axis_index("c")

    gathered = _chip_all_gather(_pack({n: args[n][0].astype(BF16) for n in big}), "weights_all_gather")
    per_chip = [_unpack(gathered[j], shard_shapes) for j in range(N_CHIPS)]
    W = {n: _full_weight([pc[n] for pc in per_chip], n) for n in big}
    w_fox, w_fg, w_dil = (W["w_in"][:, :3 * FOX_WIDTH], W["w_in"][:, 3 * FOX_WIDTH:3 * FOX_WIDTH + N_FOX_HEADS],
                          W["w_in"][:, 3 * FOX_WIDTH + N_FOX_HEADS:])
    w_qkv = jnp.concatenate([w_fox, w_dil], axis=1)
    w_fg_pad = jnp.pad(w_fg, ((0, 0), (0, LANES - N_FOX_HEADS)))
    F = W["w_gate"].shape[1]
    nft = F // FF_TILE
    w_gu = jnp.stack([W["w_gate"].reshape(D, nft, FF_TILE), W["w_up"].reshape(D, nft, FF_TILE)],
                     axis=2).reshape(D, 2 * F)

    h1 = _rms_fwd(xs, g_mix_pre, "rms_mix_pre")
    qkv = _mm(h1, w_qkv, "nn", BF16, "proj_qkv", tm=2048)
    fg = _mm(h1, w_fg_pad, "nn", F32, "proj_gate")
    fg_t = fg[:, :N_FOX_HEADS].T
    b_col = b_f.reshape(N_FOX_HEADS, 1)
    c_t = _forget_fwd(fg_t, b_col, "forget_cumsum")
    fq_s, fk_s, fv_s = (qkv[:, i * FOX_WIDTH:(i + 1) * FOX_WIDTH] for i in range(3))
    fqt, fvt = _to_heads_t(fq_s, N_FOX_HEADS), _to_heads_t(fv_s, N_FOX_HEADS)
    unit = jnp.full((N_FOX_HEADS, S), 1.0, BF16)
    inv_scale = jnp.full((N_FOX_HEADS, S), 1.0 / QK_SCALE, BF16)
    ka = _lanes_operand(_to_heads(fk_s, N_FOX_HEADS), list(_split3(-c_t)) + [unit] * EXTRA)
    qa_f = _rows_operand(fqt, [inv_scale] * EXTRA)
    o_fox_t, lse_fox = _fox_fwd(qa_f, ka, _with_ones(fvt), "fox_fwd")

    bucket_map = _band_tables()
    onehot = (jnp.asarray(bucket_map)[..., None] == jnp.arange(N_BUCKETS)).astype(F32)
    bias_tab = jnp.einsum("pqkb,bh->phkq", onehot, rel_bias, precision=lax.Precision.HIGHEST)
    bias_tab = jnp.where(jnp.asarray(bucket_map.transpose(0, 2, 1) >= 0)[:, None], bias_tab, NEG)
    bias_t = bias_tab.reshape(3, HEAD_PAIRS, 2, 2 * BAND, BAND).transpose(0, 1, 3, 2, 4).reshape(
        3, HEAD_PAIRS, 2 * BAND, 2 * BAND)
    views = [(qkv.reshape(1, S, qkv.shape[1]), DIL_Q_BLOCK)] + [
        (_to_residues(qkv, 1, 3 * DIL_WIDTH, d, f"dilated_qkv_residues_{d}"), 0) for d in DILATIONS[1:]]

    def to_tok(stat, d):
        return stat.reshape(N_DIL_HEADS, d, S // d).swapaxes(1, 2).reshape(N_DIL_HEADS, S)

    def to_perm(stat, d):
        return stat.reshape(N_DIL_HEADS, S // d, d).swapaxes(1, 2).reshape(HEAD_PAIRS, 2, S)

    def tok_or_res(t):
        return t.reshape(t.shape[1:]) if t.shape[0] == 1 else t

    lse_tok = jnp.stack([to_tok(_dil_lse(*views[p], bias_t[p], f"dilated_lse_{d}"), d)
                         for p, d in enumerate(DILATIONS)])
    lse_joint = _lse_join(lse_tok, "dilated_lse_join")
    lse_perm = [to_perm(lse_joint, d) for d in DILATIONS]
    o_dil = [tok_or_res(_dil_out(*views[p], bias_t[p], lse_perm[p], f"dilated_out_{d}"))
             for p, d in enumerate(DILATIONS)]
    o_cat = _sum_cast_cols([[_from_heads_t(o_fox_t)]] + [[o] for o in o_dil], BF16, "mixer_out_cat")
    w_out_b = W["w_out"]
    w_out_cat = jnp.concatenate([w_out_b[:FOX_WIDTH]] + [w_out_b[FOX_WIDTH:]] * 3, axis=0)
    a = _mm(o_cat, w_out_cat, "nn", F32, "proj_out", tm=2048, tk=2048)
    x1, h2 = _resid_norm(xs, a, g_mix_post, g_xattn_pre, "resid_mix")

    hm = _rms_fwd(mem[0], g_mem, "rms_mem")
    q2 = _mm(h2, W["w_xq"], "nn", BF16, "xattn_q")
    w_xkv = jnp.concatenate([W["w_xk"], W["w_xv"]], axis=1)
    kvm = _mm(hm, w_xkv, "nn", BF16, "xattn_kv")
    MW = N_MEM_HEADS * HEAD_DIM
    oc, lse_mem = _mem_fwd(q2, kvm, "xattn_fwd")
    y2 = _mm(oc, W["w_xo"], "nn", F32, "xattn_o")
    x2, h3 = _resid_norm(x1, y2, g_xattn_post, g_ffn_pre, "resid_xattn")

    gu, act = _ffn_up(h3, w_gu, "ffn_up")
    y3 = _mm(act, W["w_down"], "nn", F32, "ffn_down", tk=2816)
    dx3, loss_tile = _final_loss(x2, y3, g_ffn_post, loss_target[0], "final_loss")

    grads = {}
    small = {}
    _, dy3_b, dg = _rms_bwd(y3, g_ffn_post, dx3, None, "bwd_norm_ffn_post", want=("bf16",))
    small["g_ffn_post"] = dg[0:1]
    grads["w_down"] = _mm(act, dy3_b, "tn", F32, "grad_w_down", tm=1408)
    dgu = _ffn_dact(dy3_b, W["w_down"], gu, "ffn_dact")
    dw_gu = _mm(h3, dgu, "tn", F32, "grad_w_gu", tn=1408).reshape(D, nft, 2, FF_TILE)
    grads["w_gate"], grads["w_up"] = dw_gu[:, :, 0].reshape(D, F), dw_gu[:, :, 1].reshape(D, F)
    dh3 = _mm(dgu, w_gu, "nt", F32, "bwd_ffn_in", tm=2048, tk=1408)
    dx2, _, dg = _rms_bwd(x2, g_ffn_pre, dh3, dx3, "bwd_norm_ffn_pre", want=("f32",))
    small["g_ffn_pre"] = dg[0:1]

    _, dy2_b, dg = _rms_bwd(y2, g_xattn_post, dx2, None, "bwd_norm_xattn_post", want=("bf16",))
    small["g_xattn_post"] = dg[0:1]
    grads["w_xo"] = _mm(oc, dy2_b, "tn", F32, "grad_w_xo")
    doc = _mm(dy2_b, W["w_xo"], "nt", BF16, "bwd_xattn_o")
    delta_mem = _head_rowdot(doc, [oc], "xattn_delta")[:, :N_MEM_HEADS].T.reshape(N_MEM_HEADS // 2, 2, S)
    dq2, dkm, dvm = _mem_bwd(q2, kvm, doc, lse_mem, delta_mem, "xattn_bwd")
    dkvm = jnp.concatenate([dkm, dvm], axis=1).astype(BF16)
    grads["w_xq"] = _mm(h2, dq2, "tn", F32, "grad_w_xq")
    dw_xkv = _mm(hm, dkvm, "tn", F32, "grad_w_xkv")
    grads["w_xk"], grads["w_xv"] = dw_xkv[:, :MW], dw_xkv[:, MW:]
    dhm = _mm(dkvm, w_xkv, "nt", F32, "bwd_xattn_kv")
    _, _, dg = _rms_bwd(mem[0], g_mem, dhm, None, "bwd_norm_mem", want=())
    small["g_mem"] = dg[0:1]
    dh2 = _mm(dq2, W["w_xq"], "nt", F32, "bwd_xattn_q")
    dx1, _, dg = _rms_bwd(x1, g_xattn_pre, dh2, dx2, "bwd_norm_xattn_pre", want=("f32",))
    small["g_xattn_pre"] = dg[0:1]

    _, da_b, dg = _rms_bwd(a, g_mix_post, dx1, None, "bwd_norm_mix_post", want=("bf16",))
    small["g_mix_post"] = dg[0:1]
    dw_out_cat = _mm(o_cat, da_b, "tn", F32, "grad_w_out")
    dw_out_dil = _add_n([dw_out_cat[FOX_WIDTH + p * DIL_WIDTH:FOX_WIDTH + (p + 1) * DIL_WIDTH] for p in range(3)],
                        "grad_w_out_dil")
    grads["w_out"] = jnp.concatenate([dw_out_cat[:FOX_WIDTH], dw_out_dil], axis=0)
    do = _mm(da_b, w_out_b, "nt", BF16, "bwd_proj_out")
    do_fox, do_dil = do[:, :FOX_WIDTH], do[:, FOX_WIDTH:]

    delta_fox = _head_rowdot(do_fox, [o_cat[:, :FOX_WIDTH]], "fox_delta")[:, :N_FOX_HEADS].T
    qa_b = lax.dynamic_update_slice(qa_f, jnp.stack(_split3(lse_fox[:, 0] * (-1.0 / QK_SCALE)), axis=1),
                                    (0, HEAD_DIM + EXTRA, 0))
    va = _lanes_operand(_to_heads(fv_s, N_FOX_HEADS), [unit] * EXTRA)
    doa = _rows_operand(_to_heads_t(do_fox, N_FOX_HEADS), list(_split3(-delta_fox)))
    dq_aug, dk_aug, dvf = _fox_bwd(qa_b, ka, ka.transpose(0, 2, 1), va, doa, "fox_bwd")
    dqf, dkf = dq_aug[:, :HEAD_DIM], dk_aug[:, :HEAD_DIM]
    dfg_t, db_f = _forget_bwd(fg_t, b_col, dq_aug[:, HEAD_DIM + EXTRA], dk_aug[:, HEAD_DIM], "forget_bwd")

    delta_dil = _head_rowdot(do_dil, o_dil, "dilated_delta")[:, :N_DIL_HEADS].T
    do_res = [do_dil.reshape(1, S, DIL_WIDTH)] + [
        _to_residues(do, 1, DIL_WIDTH, d, f"dilated_do_residues_{d}") for d in DILATIONS[1:]]
    dil_grads = [_dil_bwd(*views[p], do_res[p], bias_t[p], lse_perm[p], to_perm(delta_dil, d), f"dilated_bwd_{d}")
                 for p, d in enumerate(DILATIONS)]
    dbias_t = jnp.stack([g[3].reshape(HEAD_PAIRS, 2 * BAND, 2, BAND).transpose(0, 2, 1, 3).reshape(
        N_DIL_HEADS, 2 * BAND, BAND) for g in dil_grads])
    d_rel = _bucket_reduce(dbias_t, jnp.asarray(bucket_map.transpose(0, 2, 1)), "rel_bias_grad")[:, :N_DIL_HEADS]
    dfg_pad = jnp.pad(dfg_t.T, ((0, 0), (0, LANES - N_FOX_HEADS))).astype(BF16)
    dcat = _sum_cast_cols([[_from_heads_t(dqf)], [_from_heads_t(dkf)], [_from_heads_t(dvf)]]
                          + [[tok_or_res(g[j]) for g in dil_grads] for j in range(3)],
                          BF16, "dqkv_assemble", tail=dfg_pad)
    dw_cat = _mm(h1, dcat, "tn", F32, "grad_w_qkv", tm=512, tn=3200)
    n_qkv = 3 * (FOX_WIDTH + DIL_WIDTH)
    grads["w_in"] = jnp.concatenate([dw_cat[:, :3 * FOX_WIDTH], dw_cat[:, n_qkv:n_qkv + N_FOX_HEADS],
                                     dw_cat[:, 3 * FOX_WIDTH:n_qkv]], axis=1)
    w_cat = jnp.concatenate([w_qkv, w_fg_pad], axis=1)
    dh1 = _mm(dcat, w_cat, "nt", F32, "bwd_proj_in", tk=3200)
    grad_x, _, dg = _rms_bwd(xs, g_mix_pre, dh1, dx1, "bwd_norm_mix_pre", want=("f32",))
    small["g_mix_pre"] = dg[0:1]
    small["b_f"] = db_f[:, 0].reshape(1, N_FOX_HEADS)
    small["rel_bias"] = d_rel

    split = {n: _split_weight(grads[n], n) for n in big}
    parts = jnp.stack([_pack({n: split[n][j].astype(BF16) for n in big}) for j in range(N_CHIPS)])
    R = parts.shape[1]
    half = R // 2
    keep = lax.dynamic_slice_in_dim(parts, my_c * half, half, axis=1)
    give = lax.dynamic_slice_in_dim(parts, (1 - my_c) * half, half, axis=1)
    got = _sibling_exchange(give, "grads_to_sibling")
    chip_sum = _add_n([keep.reshape(-1, PACK_COLS), got.reshape(-1, PACK_COLS)], "grads_add_sibling")
    chip_sum = chip_sum.reshape(N_CHIPS, half, PACK_COLS)
    my_chip = 2 * my_x + my_y
    from_chips = _chip_scatter(chip_sum.astype(BF16), "grads_to_chips")
    own = lax.dynamic_index_in_dim(chip_sum, my_chip, axis=0, keepdims=False)
    g_half = _add_n([own, from_chips[0], from_chips[1], from_chips[2]], "grads_add_chips")
    other_half = _sibling_exchange(g_half, "grads_share_sibling")
    g_pack = jnp.where(my_c == 0, jnp.concatenate([g_half, other_half]), jnp.concatenate([other_half, g_half]))

    small_pack = _pack_small(small)
    small_pack = small_pack.at[8, 0].set(loss_tile[0, 0])
    everyone = _all_to_all_small(small_pack, "small_all_gather")
    small_sum = _add_n([everyone[i] for i in range(8)], "small_sum")
    loss = small_sum[8, 0]
    g_small = _unpack_small(small_sum)

    outs = {"grad": _unpack(g_pack, shard_shapes), "delta": {}, "new_m": {}, "new_v": {}}
    for n in big:
        outs["delta"][n], outs["new_m"][n], outs["new_v"][n] = _adamw(
            args[n][0], outs["grad"][n], args["m_" + n][0], args["v_" + n][0], f"adamw_{n}")
    sw = _pack_small({n: args[n] for n in _SMALL + ("b_f", "rel_bias")})
    sm = _pack_small({n: args["m_" + n] for n in _SMALL + ("b_f", "rel_bias")})
    sv = _pack_small({n: args["v_" + n] for n in _SMALL + ("b_f", "rel_bias")})
    sd, snm, snv = _adamw(sw, small_sum.at[8, 0].set(0.0), sm, sv, "adamw_small")
    souts = {"grad": g_small, "delta": _unpack_small(sd), "new_m": _unpack_small(snm), "new_v": _unpack_small(snv)}

    def leaf(kind, n):
        if n in souts[kind]:
            return souts[kind][n].reshape(args[n].shape)
        return outs[kind][n].reshape(args[n].shape)

    result = [loss, grad_x.reshape(x.shape)]
    for kind in ("grad", "delta", "new_m", "new_v"):
        result += [leaf(kind, n) for n in names]
    return tuple(result)
```
